```python
import math
import jax, jax.numpy as jnp
from jax import lax
import numpy as np

D_MODEL = 1024
BATCH = 8
SEQ = 4096
DEPTH = 1
DEC_BATCH = 16
DEC_SEQ = 16
PAST_LEN = 2048

CHUNK = 64
N_META = 16
EPS = 1e-6
HGRN_HEADS = 4
HGRN_DK = 128
HGRN_DV = 128
HGRN_WIDTH = HGRN_HEADS * HGRN_DK
ATTN_HEADS = 8
KV_HEADS = 2
HEAD_DIM = 64
GQA_GROUP = ATTN_HEADS // KV_HEADS
ATTN_WIDTH = ATTN_HEADS * HEAD_DIM
KV_WIDTH = KV_HEADS * HEAD_DIM
WINDOW = 128
WINDOW_CHUNKS = WINDOW // CHUNK
MIX_WIDTH = HGRN_WIDTH + ATTN_WIDTH
NUM_BUCKETS = 32
MAX_DISTANCE = 128
N_GROUPS = 4
EXPERTS_PER_GROUP = 8
N_EXPERTS = N_GROUPS * EXPERTS_PER_GROUP
TOP_K = 2
D_FF_EXPERT = 512
MOE_BLOCK = 128
SPLITS = (HGRN_WIDTH, 2 * HGRN_WIDTH, 3 * HGRN_WIDTH, 4 * HGRN_WIDTH,
          4 * HGRN_WIDTH + ATTN_WIDTH, 4 * HGRN_WIDTH + ATTN_WIDTH + KV_WIDTH,
          4 * HGRN_WIDTH + ATTN_WIDTH + 2 * KV_WIDTH, 4 * HGRN_WIDTH + ATTN_WIDTH + 2 * KV_WIDTH + D_MODEL)
IN_COLS = 4 * HGRN_WIDTH + ATTN_WIDTH + 2 * KV_WIDTH + 2 * D_MODEL

kernel_name = 'hybrid_hgrn2_swa_hmoe_stream_step'


def rms_norm(x, w):
    xf = x.astype(jnp.float32)
    y = xf * lax.rsqrt(jnp.mean(xf * xf, axis=-1, keepdims=True) + EPS)
    return (y * w.astype(jnp.float32)).astype(x.dtype)


def t5_bucket(rel):
    half = NUM_BUCKETS // 2
    max_exact = half // 2
    n = jnp.abs(rel)
    nf = jnp.maximum(n, 1).astype(jnp.float32)
    large = max_exact + (jnp.log(nf / max_exact) / math.log(MAX_DISTANCE / max_exact)
                         * (half - max_exact)).astype(jnp.int32)
    large = jnp.minimum(large, half - 1)
    return jnp.where(rel > 0, half, 0) + jnp.where(n < max_exact, n, large)


def rel_bias(table, q_pos, k_pos):
    bucket = t5_bucket(k_pos[..., None, :] - q_pos[..., :, None])
    return jnp.moveaxis(table[bucket].astype(jnp.float32), -1, 0)


def hgrn_lower_bound(lower_bounds, layer):
    p = jax.nn.softmax(lower_bounds.astype(jnp.float32), axis=0)
    return jnp.cumsum(p, axis=0)[layer + 1] - p[0]


def mixer_inputs(h, w_norm, w_in, lb, q_gain, k_gain):
    lead = h.shape[:-1]
    p = rms_norm(h, w_norm) @ w_in
    hq, hf, hv, hg, aq, ak, av, ga, gb = jnp.split(p, SPLITS, axis=-1)
    z = hf.astype(jnp.float32)
    logf = jnp.log(lb + (1.0 - lb) * jax.nn.sigmoid(z))
    kin = (1.0 - lb) * jax.nn.sigmoid(-z)
    qr = jax.nn.silu(hq.astype(jnp.float32)) * HGRN_DK ** -0.5
    rh = lambda a: a.reshape(*lead, HGRN_HEADS, -1)
    rec = (rh(qr), rh(kin), rh(hv.astype(jnp.float32)), rh(logf), rh(hg))
    q = rms_norm(aq.reshape(*lead, ATTN_HEADS, HEAD_DIM), q_gain)
    k = rms_norm(ak.reshape(*lead, KV_HEADS, HEAD_DIM), k_gain)
    v = av.reshape(*lead, KV_HEADS, HEAD_DIM)
    return rec, (q, k, v), (ga, gb)


def hgrn2_scan(q, k, v, logf, s0):
    B, L, H, _ = q.shape
    pad = (-L) % CHUNK
    n = (L + pad) // CHUNK

    def to_blocks(a):
        a = jnp.pad(a, ((0, 0), (0, pad), (0, 0), (0, 0)))
        return a.reshape(B, n, CHUNK, H, a.shape[-1]).transpose(1, 0, 3, 2, 4)

    causal = jnp.tril(jnp.ones((CHUNK, CHUNK), bool))[:, :, None]

    def step(state, blk):
        qc, kc, vc, gc = blk
        b = jnp.cumsum(gc, axis=2)
        decay = jnp.exp(jnp.where(causal, b[:, :, :, None] - b[:, :, None], -jnp.inf))
        scores = jnp.einsum('bhtsk,bhsk->bhts', qc[:, :, :, None] * decay, kc)
        o = (jnp.einsum('bhts,bhsv->bhtv', scores, vc)
             + jnp.einsum('bhtk,bhkv->bhtv', qc * jnp.exp(b), state))
        b_last = b[:, :, -1]
        state = (jnp.exp(b_last)[..., None] * state
                 + jnp.einsum('bhsk,bhsv->bhkv', kc * jnp.exp(b_last[:, :, None] - b), vc))
        return state, o

    f32 = jnp.float32
    state, o = lax.scan(step, s0.astype(f32),
                        (to_blocks(q.astype(f32)), to_blocks(k.astype(f32)),
                         to_blocks(v.astype(f32)), to_blocks(logf.astype(f32))))
    o = o.transpose(1, 0, 3, 2, 4).reshape(B, n * CHUNK, H, -1)[:, :L]
    return o, state


def attend(q, k, v, bias, valid, sinks):
    B, N, Tq = q.shape[:3]
    Tk = k.shape[2]
    qg = q.reshape(B, N, Tq, KV_HEADS, GQA_GROUP, HEAD_DIM)
    s = jnp.einsum('bnqhgd,bnkhd->bnhgqk', qg, k, preferred_element_type=jnp.float32) * HEAD_DIM ** -0.5
    s = s + bias.reshape(KV_HEADS, GQA_GROUP, N, Tq, Tk).transpose(2, 0, 1, 3, 4)[None]
    s = jnp.where(valid[None, :, None, None], s, -jnp.inf)
    sink = sinks.astype(jnp.float32).reshape(KV_HEADS, GQA_GROUP)[None, None, :, :, None, None]
    m = jnp.maximum(jnp.max(s, axis=-1, keepdims=True), sink)
    p = jnp.exp(s - m)
    w = p / (jnp.sum(p, axis=-1, keepdims=True) + jnp.exp(sink - m))
    o = jnp.einsum('bnhgqk,bnkhd->bnqhgd', w.astype(v.dtype), v)
    return o.reshape(B, N, Tq, ATTN_WIDTH)


def merge_branches(h, o_rec, g_rec, o_att, ga, gb, out_gain, w_branch, w_out):
    lead = h.shape[:-1]
    y_rec = rms_norm(o_rec, out_gain) * jax.nn.silu(g_rec.astype(jnp.float32))
    y_rec = y_rec.reshape(*lead, HGRN_WIDTH).astype(h.dtype)
    branch_rec = y_rec @ w_branch[:HGRN_WIDTH]
    branch_att = o_att.astype(h.dtype) @ w_branch[HGRN_WIDTH:]
    merged = jax.nn.sigmoid(ga) * branch_rec + jax.nn.sigmoid(gb) * branch_att
    return h + merged @ w_out


def expert_dispatch(x, expert, weight, wg, wu, wd):
    T = x.shape[0]
    A = T * TOP_K
    n_blocks = -(-(A + N_EXPERTS * (MOE_BLOCK - 1)) // MOE_BLOCK)
    P = n_blocks * MOE_BLOCK
    flat_e = expert.reshape(-1)
    flat_tok = jnp.arange(A, dtype=jnp.int32) // TOP_K
    flat_w = weight.reshape(-1).astype(jnp.float32)
    order = jnp.argsort(flat_e)
    se = flat_e[order]
    counts = jnp.bincount(flat_e, length=N_EXPERTS)
    starts = jnp.cumsum(counts) - counts
    pcounts = (counts + MOE_BLOCK - 1) // MOE_BLOCK * MOE_BLOCK
    pends = jnp.cumsum(pcounts)
    pstarts = pends - pcounts
    dest = pstarts[se] + jnp.arange(A, dtype=jnp.int32) - starts[se]
    slot_tok = jnp.full((P,), T, jnp.int32).at[dest].set(flat_tok[order])
    slot_w = jnp.zeros((P,), jnp.float32).at[dest].set(flat_w[order])
    block_e = jnp.minimum(jnp.searchsorted(pends, jnp.arange(n_blocks) * MOE_BLOCK, side='right'),
                          N_EXPERTS - 1)
    x_pad = jnp.concatenate([x, jnp.zeros((1, x.shape[1]), x.dtype)], axis=0)
    xs = x_pad[slot_tok].reshape(n_blocks, MOE_BLOCK, x.shape[1])

    def run(args):
        xb, e = args
        return (jax.nn.silu(xb @ wg[e]) * (xb @ wu[e])) @ wd[e]

    ys = lax.map(run, (xs, block_e)).reshape(P, -1)
    out = jnp.zeros((T + 1, x.shape[1]), jnp.float32).at[slot_tok].add(ys.astype(jnp.float32) * slot_w[:, None])
    return out[:T]


def moe(h, w_norm, wrg, brg, wre, bre, wg, wu, wd):
    lead = h.shape[:-1]
    x = rms_norm(h, w_norm).reshape(-1, D_MODEL)
    T = x.shape[0]
    pg = jax.nn.softmax((x @ wrg).astype(jnp.float32) + brg.astype(jnp.float32), axis=-1)
    g_val, g_idx = lax.top_k(pg, 1)
    le = ((x @ wre).astype(jnp.float32) + bre.astype(jnp.float32)).reshape(T, N_GROUPS, EXPERTS_PER_GROUP)
    le = jnp.take_along_axis(le, g_idx[:, :, None], axis=1)[:, 0]
    e_val, e_idx = lax.top_k(le, TOP_K)
    e_w = jax.nn.softmax(e_val, axis=-1) * g_val
    expert = g_idx * EXPERTS_PER_GROUP + e_idx
    y = expert_dispatch(x, expert, e_w, wg, wu, wd)
    return h + y.reshape(*lead, D_MODEL).astype(h.dtype)


def setup_inputs(seed: int = 0) -> dict:
    key = jax.random.key(seed)
    ks = jax.random.split(key, 24)
    f32 = jnp.float32
    nrm = lambda k, shape, scale: jax.random.normal(k, shape, f32) * scale
    return {
        'x_prompt': nrm(ks[0], (BATCH, SEQ, D_MODEL), 1.0),
        'x_sample': nrm(ks[1], (DEC_BATCH, DEC_SEQ, D_MODEL), 1.0),
        'cache_swa_k': nrm(ks[2], (DEPTH, DEC_BATCH, WINDOW, KV_HEADS, HEAD_DIM), 1.0),
        'cache_swa_v': nrm(ks[3], (DEPTH, DEC_BATCH, WINDOW, KV_HEADS, HEAD_DIM), 1.0),
        'state_hgrn': nrm(ks[4], (DEPTH, DEC_BATCH, HGRN_HEADS, HGRN_DK, HGRN_DV), 1.0),
        'meta_tokens': nrm(ks[5], (N_META, D_MODEL), 1.0),
        'rel_bias_table': nrm(ks[6], (NUM_BUCKETS, ATTN_HEADS), 0.5),
        'hgrn_lower_bounds': 1.0 + nrm(ks[7], (DEPTH + 1, HGRN_WIDTH), 0.5),
        'w_norm_mix': 1.0 + nrm(ks[8], (DEPTH, D_MODEL), 0.05),
        'w_in': nrm(ks[9], (DEPTH, D_MODEL, IN_COLS), D_MODEL ** -0.5),
        'hgrn_out_norm': 1.0 + nrm(ks[10], (DEPTH, HGRN_DV), 0.05),
        'q_norm': 1.0 + nrm(ks[11], (DEPTH, HEAD_DIM), 0.05),
        'k_norm': 1.0 + nrm(ks[12], (DEPTH, HEAD_DIM), 0.05),
        'attn_sinks': nrm(ks[13], (DEPTH, ATTN_HEADS), 1.0),
        'w_branch': nrm(ks[14], (DEPTH, MIX_WIDTH, D_MODEL), HGRN_WIDTH ** -0.5),
        'w_out': nrm(ks[15], (DEPTH, D_MODEL, D_MODEL), D_MODEL ** -0.5),
        'w_norm_ffn': 1.0 + nrm(ks[16], (DEPTH, D_MODEL), 0.05),
        'w_router_group': nrm(ks[17], (DEPTH, D_MODEL, N_GROUPS), D_MODEL ** -0.5),
        'b_router_group': nrm(ks[18], (DEPTH, N_GROUPS), 0.01),
        'w_router_expert': nrm(ks[19], (DEPTH, D_MODEL, N_EXPERTS), D_MODEL ** -0.5),
        'b_router_expert': nrm(ks[20], (DEPTH, N_EXPERTS), 0.01),
        'w_expert_gate': nrm(ks[21], (DEPTH, N_EXPERTS, D_MODEL, D_FF_EXPERT), D_MODEL ** -0.5),
        'w_expert_up': nrm(ks[22], (DEPTH, N_EXPERTS, D_MODEL, D_FF_EXPERT), D_MODEL ** -0.5),
        'w_expert_down': nrm(ks[23], (DEPTH, N_EXPERTS, D_FF_EXPERT, D_MODEL), D_FF_EXPERT ** -0.5),
    }


def reference(x_prompt, x_sample, cache_swa_k, cache_swa_v, state_hgrn, meta_tokens, rel_bias_table,
              hgrn_lower_bounds, w_norm_mix, w_in, hgrn_out_norm, q_norm, k_norm, attn_sinks,
              w_branch, w_out, w_norm_ffn, w_router_group, b_router_group, w_router_expert,
              b_router_expert, w_expert_gate, w_expert_up, w_expert_down):
    f32 = jnp.float32
    i32 = jnp.int32
    B, S, _ = x_prompt.shape
    Bd, Sd, _ = x_sample.shape
    nc = S // CHUNK

    meta_pos = jnp.arange(N_META, dtype=i32)
    c_idx = jnp.arange(nc, dtype=i32)[:, None]
    q_pos_p = N_META + c_idx * CHUNK + jnp.arange(CHUNK, dtype=i32)[None]
    win_pos = N_META + (c_idx - WINDOW_CHUNKS) * CHUNK + jnp.arange(WINDOW + CHUNK, dtype=i32)[None]
    k_pos_p = jnp.concatenate([jnp.broadcast_to(meta_pos, (nc, N_META)), win_pos], axis=1)
    bias_p = rel_bias(rel_bias_table, q_pos_p, k_pos_p)
    valid_p = jnp.concatenate([jnp.ones((nc, N_META), bool), win_pos >= N_META], axis=1)[:, None, :]

    q_pos_s = N_META + PAST_LEN + jnp.arange(Sd, dtype=i32)
    k_pos_s = jnp.concatenate([meta_pos, N_META + PAST_LEN - WINDOW + jnp.arange(WINDOW, dtype=i32), q_pos_s])
    bias_s = rel_bias(rel_bias_table, q_pos_s, k_pos_s)[:, None]
    valid_s = jnp.ones((1, 1, k_pos_s.shape[0]), bool)

    h_meta = meta_tokens.astype(x_prompt.dtype)[None]
    hp, hs = x_prompt, x_sample
    new_kp, new_vp, new_sp, new_ks, new_vs, new_ss = [], [], [], [], [], []

    for l in range(DEPTH):
        lb = hgrn_lower_bound(hgrn_lower_bounds, l)
        proj_w = (w_norm_mix[l], w_in[l], lb, q_norm[l], k_norm[l])
        merge_w = (hgrn_out_norm[l], w_branch[l], w_out[l])
        ffn_w = (w_norm_ffn[l], w_router_group[l], b_router_group[l], w_router_expert[l],
                 b_router_expert[l], w_expert_gate[l], w_expert_up[l], w_expert_down[l])

        rec_m, (q_m, k_m, v_m), (ga_m, gb_m) = mixer_inputs(h_meta, *proj_w)
        o_m, s_m = hgrn2_scan(rec_m[0], rec_m[1], rec_m[2], rec_m[3],
                              jnp.zeros((1, HGRN_HEADS, HGRN_DK, HGRN_DV), f32))

        rec_p, (q_p, k_p, v_p), (ga_p, gb_p) = mixer_inputs(hp, *proj_w)
        o_p, s_p = hgrn2_scan(rec_p[0], rec_p[1], rec_p[2], rec_p[3],
                              jnp.broadcast_to(s_m, (B,) + s_m.shape[1:]))

        def blocks(a):
            return a.reshape(B, nc, CHUNK, *a.shape[2:])

        def banded(a, a_meta):
            ab = jnp.pad(blocks(a), ((0, 0), (WINDOW_CHUNKS, 0), (0, 0), (0, 0), (0, 0)))
            win = jnp.concatenate([ab[:, j:j + nc] for j in range(WINDOW_CHUNKS + 1)], axis=2)
            meta = jnp.broadcast_to(a_meta[:, None], (B, nc) + a_meta.shape[1:]).astype(win.dtype)
            return jnp.concatenate([meta, win], axis=2)

        att_p = attend(blocks(q_p), banded(k_p, k_m), banded(v_p, v_m), bias_p, valid_p,
                       attn_sinks[l]).reshape(B, S, ATTN_WIDTH)
        hp = merge_branches(hp, o_p, rec_p[4], att_p, ga_p, gb_p, *merge_w)
        hp = moe(hp, *ffn_w)
        new_kp.append(k_p[:, S - WINDOW:])
        new_vp.append(v_p[:, S - WINDOW:])
        new_sp.append(s_p)

        rec_s, (q_s, k_s, v_s), (ga_s, gb_s) = mixer_inputs(hs, *proj_w)
        o_s, s_s = hgrn2_scan(rec_s[0], rec_s[1], rec_s[2], rec_s[3], state_hgrn[l])
        k_roll = jnp.concatenate([cache_swa_k[l].astype(k_s.dtype), k_s], axis=1)
        v_roll = jnp.concatenate([cache_swa_v[l].astype(v_s.dtype), v_s], axis=1)
        keys = jnp.concatenate([jnp.broadcast_to(k_m, (Bd,) + k_m.shape[1:]).astype(k_s.dtype), k_roll], axis=1)
        vals = jnp.concatenate([jnp.broadcast_to(v_m, (Bd,) + v_m.shape[1:]).astype(v_s.dtype), v_roll], axis=1)
        att_s = attend(q_s[:, None], keys[:, None], vals[:, None], bias_s, valid_s, attn_sinks[l])[:, 0]
        hs = merge_branches(hs, o_s, rec_s[4], att_s, ga_s, gb_s, *merge_w)
        hs = moe(hs, *ffn_w)
        new_ks.append(k_roll[:, -WINDOW:])
        new_vs.append(v_roll[:, -WINDOW:])
        new_ss.append(s_s)

        if l + 1 < DEPTH:
            bias_m = rel_bias(rel_bias_table, meta_pos, meta_pos)[:, None]
            valid_m = jnp.ones((1, 1, N_META), bool)
            att_m = attend(q_m[:, None], k_m[:, None], v_m[:, None], bias_m, valid_m, attn_sinks[l])[:, 0]
            h_meta = moe(merge_branches(h_meta, o_m, rec_m[4], att_m, ga_m, gb_m, *merge_w), *ffn_w)

    return (hp, hs, jnp.stack(new_kp), jnp.stack(new_vp), jnp.stack(new_sp),
            jnp.stack(new_ks), jnp.stack(new_vs), jnp.stack(new_ss))
```

```python
import functools
import math

import numpy as np
import jax
import jax.numpy as jnp
from jax import lax
from jax.experimental import pallas as pl
from jax.experimental.pallas import tpu as pltpu

F32 = jnp.float32
BF16 = jnp.bfloat16
I32 = jnp.int32

CHUNK = 64
N_META = 16
PAST_LEN = 2048
EPS = 1e-6
HGRN_HEADS = 4
ATTN_HEADS = 8
KV_HEADS = 2
HEAD_DIM = 64
GQA_GROUP = ATTN_HEADS // KV_HEADS
WINDOW = 128
WINDOW_CHUNKS = WINDOW // CHUNK
NUM_BUCKETS = 32
MAX_DISTANCE = 128
N_GROUPS = 4
EXPERTS_PER_GROUP = 8
N_EXPERTS = N_GROUPS * EXPERTS_PER_GROUP
TOP_K = 2

LANES = 128
VMEM_LIMIT = 56 * 1024 * 1024

PROJ_TILE = 512
HGRN_CHUNK = 128
EXPERT_TILE = 256
MOVE_TILE = 256


def _sigmoid(x):
    return 1.0 / (1.0 + jnp.exp(-x))


def _split3(x):
    hi = x.astype(BF16)
    r1 = x - hi.astype(F32)
    mid = r1.astype(BF16)
    lo = (r1 - mid.astype(F32)).astype(BF16)
    return hi, mid, lo


def _dot(a, b):
    return jnp.dot(a, b, preferred_element_type=F32)


def _dot_nt(a, b):
    return lax.dot_general(a, b, (((1,), (1,)), ((), ())), preferred_element_type=F32)


def _dot_tn(a, b):
    return lax.dot_general(a, b, (((0,), (0,)), ((), ())), preferred_element_type=F32)


def _params(*sem):
    return pltpu.CompilerParams(dimension_semantics=sem, vmem_limit_bytes=VMEM_LIMIT)


def _inproj_kernel(hw, aw, kvw, d, x_ref, wn_ref, w_ref, qg_ref, kg_ref, bdq_ref, bdk_ref,
                   qr_ref, z_ref, hv_ref, hg_ref, qa_ref, k_ref, v_ref, sga_ref, sgb_ref):
    x = x_ref[...]
    ms = jnp.mean(x * x, axis=-1, keepdims=True)
    xn = (x * lax.rsqrt(ms + EPS) * wn_ref[...]).astype(BF16)

    def seg(a, b):
        return _dot(xn, w_ref[:, a:b])

    def head_rms(a, bd_ref, gain):
        sq = a * a
        hi = sq.astype(BF16)
        lo = (sq - hi.astype(F32)).astype(BF16)
        m = _dot(hi, bd_ref[...]) + _dot(lo, bd_ref[...])
        return a * lax.rsqrt(m + EPS) * gain

    o = 0
    hq = seg(o, o + hw)
    qr_ref[...] = (hq * _sigmoid(hq) * (hw // HGRN_HEADS) ** -0.5).astype(BF16)
    o += hw
    z_ref[...] = seg(o, o + hw)
    o += hw
    hv_ref[...] = seg(o, o + hw).astype(BF16)
    o += hw
    hg_ref[...] = seg(o, o + hw).astype(BF16)
    o += hw
    aq = seg(o, o + aw)
    qa_ref[...] = (head_rms(aq, bdq_ref, qg_ref[...]) * HEAD_DIM ** -0.5).astype(BF16)
    o += aw
    k_ref[...] = head_rms(seg(o, o + kvw), bdk_ref, kg_ref[...])
    o += kvw
    v_ref[...] = seg(o, o + kvw)
    o += kvw
    sga_ref[...] = _sigmoid(seg(o, o + d)).astype(BF16)
    o += d
    sgb_ref[...] = _sigmoid(seg(o, o + d)).astype(BF16)


def _block_diag_mean(width, group):
    i = np.arange(width)
    return jnp.asarray((i[:, None] // group == i[None, :] // group) / group, dtype=BF16)


def _inproj(x, w_norm, w_in_bf16, q_gain, k_gain, hw, aw, kvw):
    t, d = x.shape
    tm = PROJ_TILE if t % PROJ_TILE == 0 else t
    cols = w_in_bf16.shape[1]
    row = lambda w: pl.BlockSpec((tm, w), lambda i: (i, 0))
    const = lambda a, b: pl.BlockSpec((a, b), lambda i: (0, 0))
    outs = [(hw, BF16), (hw, F32), (hw, BF16), (hw, BF16), (aw, BF16), (kvw, F32), (kvw, F32), (d, BF16), (d, BF16)]
    return pl.pallas_call(
        functools.partial(_inproj_kernel, hw, aw, kvw, d),
        grid=(t // tm,),
        in_specs=[row(d), const(1, d), const(d, cols), const(1, aw), const(1, kvw), const(aw, aw), const(kvw, kvw)],
        out_specs=[row(w) for w, _ in outs],
        out_shape=[jax.ShapeDtypeStruct((t, w), dt) for w, dt in outs],
        compiler_params=_params("arbitrary"),
        name="inproj",
    )(x, w_norm.reshape(1, d), w_in_bf16,
      jnp.tile(q_gain, aw // HEAD_DIM).reshape(1, aw), jnp.tile(k_gain, kvw // HEAD_DIM).reshape(1, kvw),
      _block_diag_mean(aw, HEAD_DIM), _block_diag_mean(kvw, HEAD_DIM))


def _hgrn_consts(L):
    t = np.arange(L)
    u = t[None, :]
    blocks = [u <= t[:, None], u > t[:, None]]
    levels = []
    m = L // 2
    while m >= 1:
        levels.append(m)
        m //= 2
    lvl = np.full((L, L), -1, np.int32)
    lvl[t, t] = len(levels)
    isq_cols = []
    for j, m in enumerate(levels):
        bnd = (t // (2 * m)) * (2 * m) + m - 1
        isq = (t % (2 * m)) >= m
        cq = isq[:, None] & (u > bnd[:, None]) & (u <= t[:, None])
        ck = (~isq)[:, None] & (u > t[:, None]) & (u <= bnd[:, None])
        blocks.append(cq | ck)
        same = (t[:, None] // (2 * m)) == (t[None, :] // (2 * m))
        lvl[same & isq[:, None] & (~isq)[None, :]] = j
        isq_cols.append(isq)
    c = np.concatenate(blocks, axis=0).astype(np.float32)
    isq = np.stack(isq_cols, axis=1).astype(np.float32)
    isq = np.pad(isq, ((0, 0), (0, LANES - isq.shape[1])))
    return jnp.asarray(c, dtype=BF16), jnp.asarray(lvl), jnp.asarray(isq), len(levels)


def _hgrn_kernel(L, nlev, heads, dk, qr_ref, z_ref, hv_ref, hg_ref, lb_ref, og_ref, c_ref, lvl_ref, isq_ref,
                 s0_ref, y_ref, sout_ref, st_ref):
    c = pl.program_id(1)

    @pl.when(c == 0)
    def _():
        for h in range(heads):
            st_ref[h] = s0_ref[0, h].T

    z = z_ref[0]
    lb = lb_ref[...]
    e = jnp.exp(-jnp.abs(z))
    r = 1.0 / (1.0 + e)
    pos = z >= 0
    sig = jnp.where(pos, r, e * r)
    sig_neg = jnp.where(pos, e * r, r)
    logf = jnp.log(lb + (1.0 - lb) * sig)
    kin = (1.0 - lb) * sig_neg
    q = qr_ref[0].astype(F32)

    hi, mid, lo = _split3(logf)
    cm = c_ref[...]
    ex = jnp.exp(_dot(cm, hi) + _dot(cm, mid) + _dot(cm, lo))
    e_b = ex[0:L]
    e_rev = ex[L:2 * L]

    q_in = (q * e_b).astype(BF16)
    k_out = (kin * e_rev).astype(BF16)
    q_b = q.astype(BF16)
    k_b = kin.astype(BF16)
    xs = []
    for j in range(nlev):
        isq = isq_ref[:, j:j + 1] > 0.5
        xs.append((jnp.where(isq, q, kin) * ex[(2 + j) * L:(3 + j) * L]).astype(BF16))
    lvl = lvl_ref[...]
    v = hv_ref[0]
    g = hg_ref[0].astype(F32)
    og = og_ref[...]

    for h in range(heads):
        sl = slice(h * dk, (h + 1) * dk)
        a = jnp.where(lvl == nlev, _dot_nt(q_b[:, sl], k_b[:, sl]), 0.0)
        for j in range(nlev):
            xh = xs[j][:, sl]
            a = jnp.where(lvl == j, _dot_nt(xh, xh), a)
        st = st_ref[h]
        vh = v[:, sl]
        o = _dot(a.astype(BF16), vh) + _dot_nt(q_in[:, sl], st.astype(BF16))
        st_ref[h] = st * e_b[L - 1:L, sl] + _dot_tn(vh, k_out[:, sl])
        ms = jnp.mean(o * o, axis=-1, keepdims=True)
        gh = g[:, sl]
        y_ref[0, :, sl] = (o * lax.rsqrt(ms + EPS) * og[:, sl] * (gh * _sigmoid(gh))).astype(BF16)

    @pl.when(c == pl.num_programs(1) - 1)
    def _():
        for h in range(heads):
            sout_ref[0, h] = st_ref[h].T


def _hgrn(qr, z, hv, hg, lb, out_gain, s0, L):
    b, s, w = z.shape
    heads, dk = s0.shape[1], s0.shape[2]
    cm, lvl, isq, nlev = _hgrn_consts(L)
    seq = pl.BlockSpec((1, L, w), lambda i, c: (i, c, 0))
    const = lambda a: pl.BlockSpec(a.shape, lambda i, c: (0,) * a.ndim)
    state = pl.BlockSpec((1, heads, dk, dk), lambda i, c: (i, 0, 0, 0))
    lb2 = lb.reshape(1, w)
    og2 = jnp.tile(out_gain, heads).reshape(1, w)
    return pl.pallas_call(
        functools.partial(_hgrn_kernel, L, nlev, heads, dk),
        grid=(b, s // L),
        in_specs=[seq, seq, seq, seq, const(lb2), const(og2), const(cm), const(lvl), const(isq), state],
        out_specs=[seq, state],
        out_shape=[jax.ShapeDtypeStruct((b, s, w), BF16), jax.ShapeDtypeStruct(s0.shape, F32)],
        scratch_shapes=[pltpu.VMEM((heads, dk, dk), F32)],
        compiler_params=_params("arbitrary", "arbitrary"),
        name=f"hgrn_scan_{L}",
    )(qr, z, hv, hg, lb2, og2, cm, lvl, isq, s0)


def _t5_bucket_np(rel):
    half = NUM_BUCKETS // 2
    max_exact = half // 2
    assert (NUM_BUCKETS, MAX_DISTANCE) == (32, 128)
    n = np.abs(rel).astype(np.int64)
    nn = np.maximum(n, 1)
    k = np.zeros_like(nn)
    for j in range(1, 48):
        k = np.where(64 * (1 << j) <= nn * nn, j, k)
    large = np.minimum(max_exact + k, half - 1)
    return np.where(rel > 0, half, 0) + np.where(n < max_exact, n, large)


def _softmax_pv(s, sink, v):
    m = jnp.maximum(jnp.max(s, axis=-1, keepdims=True), sink)
    p = jnp.exp(s - m)
    w = p / (jnp.sum(p, axis=-1, keepdims=True) + jnp.exp(sink - m))
    return _dot(w.astype(BF16), v)


def _attend_heads(q, keys, vals, bias_ref, bias_idx, neg, sink_ref, out_ref):
    outs = []
    for h in range(ATTN_HEADS):
        g = h // GQA_GROUP
        qh = q[:, h * HEAD_DIM:(h + 1) * HEAD_DIM]
        kg = keys[:, g * HEAD_DIM:(g + 1) * HEAD_DIM]
        vg = vals[:, g * HEAD_DIM:(g + 1) * HEAD_DIM]
        s = _dot_nt(qh, kg) + bias_ref[bias_idx, h]
        if neg is not None:
            s = s + neg
        outs.append(_softmax_pv(s, sink_ref[h], vg))
    out_ref[0] = jnp.concatenate(outs, axis=1).astype(out_ref.dtype)


def _attn_prompt_kernel(q_ref, k_ref, v_ref, km_ref, vm_ref, bias_ref, sink_ref, o_ref):
    c = pl.program_id(1)
    kb, vb = [km_ref[...]], [vm_ref[...]]
    for j in range(WINDOW_CHUNKS + 1):
        start = pl.multiple_of(jnp.maximum(c - WINDOW_CHUNKS + j, 0) * CHUNK, CHUNK)
        kb.append(k_ref[0, pl.ds(start, CHUNK), :])
        vb.append(v_ref[0, pl.ds(start, CHUNK), :])
    keys = jnp.concatenate(kb, axis=0).astype(BF16)
    vals = jnp.concatenate(vb, axis=0).astype(BF16)
    tk = keys.shape[0]
    col = lax.broadcasted_iota(I32, (1, tk), 1)
    frame = (c - WINDOW_CHUNKS) * CHUNK + col - N_META
    neg = jnp.where((col >= N_META) & (frame < 0), -jnp.inf, 0.0)
    _attend_heads(q_ref[0], keys, vals, bias_ref, jnp.minimum(c, bias_ref.shape[0] - 1), neg, sink_ref, o_ref)


def _attn_prompt(q, k, v, k_meta, v_meta, table, sinks):
    b, s, aw = q.shape
    kvw = k.shape[-1]
    nc = s // CHUNK
    n_bias = 1
    while True:
        qpos = N_META + (n_bias - 1) * CHUNK
        if np.all(_t5_bucket_np(np.arange(N_META) - qpos) == _t5_bucket_np(np.arange(N_META) - qpos - 10 ** 6)):
            break
        n_bias += 1
    n_bias = min(n_bias, nc)
    cs = np.arange(n_bias)[:, None]
    qpos = N_META + cs * CHUNK + np.arange(CHUNK)[None]
    wpos = N_META + (cs - WINDOW_CHUNKS) * CHUNK + np.arange(WINDOW + CHUNK)[None]
    kpos = np.concatenate([np.broadcast_to(np.arange(N_META), (n_bias, N_META)), wpos], axis=1)
    bucket = _t5_bucket_np(kpos[:, None, :] - qpos[:, :, None])
    bias = jnp.moveaxis(table[jnp.asarray(bucket)].astype(F32), -1, 1)
    qs = pl.BlockSpec((1, CHUNK, aw), lambda i, c: (i, c, 0))
    kv = pl.BlockSpec((1, s, kvw), lambda i, c: (i, 0, 0))
    meta = pl.BlockSpec((N_META, kvw), lambda i, c: (0, 0))
    return pl.pallas_call(
        _attn_prompt_kernel,
        grid=(b, nc),
        in_specs=[qs, kv, kv, meta, meta, pl.BlockSpec(bias.shape, lambda i, c: (0, 0, 0, 0)),
                  pl.BlockSpec(memory_space=pltpu.SMEM)],
        out_specs=qs,
        out_shape=jax.ShapeDtypeStruct((b, s, aw), BF16),
        compiler_params=_params("arbitrary", "arbitrary"),
        name="attn_prompt",
    )(q, k, v, k_meta, v_meta, bias, sinks.astype(F32))


def _attn_sample_kernel(q_ref, kc_ref, vc_ref, kn_ref, vn_ref, km_ref, vm_ref, bias_ref, sink_ref, o_ref):
    keys = jnp.concatenate([km_ref[...], kc_ref[0], kn_ref[0]], axis=0).astype(BF16)
    vals = jnp.concatenate([vm_ref[...], vc_ref[0], vn_ref[0]], axis=0).astype(BF16)
    _attend_heads(q_ref[0], keys, vals, bias_ref, 0, None, sink_ref, o_ref)


def _attn_sample(q, k_cache, v_cache, k_new, v_new, k_meta, v_meta, table, sinks):
    bd, sd, aw = q.shape
    kvw = k_new.shape[-1]
    win = k_cache.shape[1]
    qpos = N_META + PAST_LEN + np.arange(sd)
    kpos = np.concatenate([np.arange(N_META), N_META + PAST_LEN - win + np.arange(win), qpos])
    bucket = _t5_bucket_np(kpos[None, :] - qpos[:, None])
    bias = jnp.moveaxis(table[jnp.asarray(bucket)].astype(F32), -1, 0)[None]
    per = lambda n, w: pl.BlockSpec((1, n, w), lambda i: (i, 0, 0))
    meta = pl.BlockSpec((N_META, kvw), lambda i: (0, 0))
    return pl.pallas_call(
        _attn_sample_kernel,
        grid=(bd,),
        in_specs=[per(sd, aw), per(win, kvw), per(win, kvw), per(sd, kvw), per(sd, kvw), meta, meta,
                  pl.BlockSpec(bias.shape, lambda i: (0, 0, 0, 0)), pl.BlockSpec(memory_space=pltpu.SMEM)],
        out_specs=per(sd, aw),
        out_shape=jax.ShapeDtypeStruct((bd, sd, aw), BF16),
        compiler_params=_params("arbitrary"),
        name="attn_sample",
    )(q, k_cache, v_cache, k_new, v_new, k_meta, v_meta, bias, sinks.astype(F32))


ROUTE_E1, ROUTE_E2, ROUTE_W1, ROUTE_W2, ROUTE_R1, ROUTE_R2 = range(6)


def _merge_kernel(hw, x_ref, yr_ref, at_ref, sga_ref, sgb_ref, wb_ref, wo_ref, wn_ref, wr_ref, br_ref, cnt0_ref,
                  h_ref, xn_ref, route_ref, cnt_ref, carry_ref):
    i = pl.program_id(0)

    @pl.when(i == 0)
    def _():
        carry_ref[...] = cnt0_ref[...]

    br = _dot(yr_ref[...], wb_ref[0:hw, :])
    ba = _dot(at_ref[...], wb_ref[hw:, :])
    merged = sga_ref[...].astype(F32) * br + sgb_ref[...].astype(F32) * ba
    h = x_ref[...] + _dot(merged.astype(BF16), wo_ref[...])
    h_ref[...] = h
    ms = jnp.mean(h * h, axis=-1, keepdims=True)
    xn = h * lax.rsqrt(ms + EPS) * wn_ref[...]
    xn_ref[...] = xn

    x_hi = xn.astype(BF16)
    x_lo = (xn - x_hi.astype(F32)).astype(BF16)
    w = wr_ref[...]
    w_hi = w.astype(BF16)
    w_lo = (w - w_hi.astype(F32)).astype(BF16)
    logits = _dot(x_hi, w_hi) + _dot(x_hi, w_lo) + _dot(x_lo, w_hi) + br_ref[...]

    tm = logits.shape[0]
    lane_i = lax.broadcasted_iota(I32, (tm, LANES), 1)
    lane = lane_i.astype(F32)
    group_of_lane = (lane_i >> int(math.log2(EXPERTS_PER_GROUP))).astype(F32)
    ninf = -jnp.inf
    first = lambda hit, idx: jnp.min(jnp.where(hit, idx, float(LANES)), axis=-1, keepdims=True)
    gmask = (lane_i >= N_EXPERTS) & (lane_i < N_EXPERTS + N_GROUPS)
    gl = jnp.where(gmask, logits, ninf)
    gmax = jnp.max(gl, axis=-1, keepdims=True)
    gidx = first(gl == gmax, lane - N_EXPERTS)
    gval = 1.0 / jnp.sum(jnp.exp(gl - gmax), axis=-1, keepdims=True)
    emask = (lane_i < N_EXPERTS) & (group_of_lane == gidx)
    el = jnp.where(emask, logits, ninf)
    m1 = jnp.max(el, axis=-1, keepdims=True)
    i1 = first(el == m1, lane)
    el2 = jnp.where(lane == i1, ninf, el)
    m2 = jnp.max(el2, axis=-1, keepdims=True)
    i2 = first(el2 == m2, lane)
    e21 = jnp.exp(m2 - m1)
    w1 = gval / (1.0 + e21)
    w2 = gval * e21 / (1.0 + e21)

    sel1 = lane == i1
    sel2 = lane == i2
    oh = (sel1 | sel2).astype(BF16)
    ri = lax.broadcasted_iota(I32, (tm, tm), 0)
    ci = lax.broadcasted_iota(I32, (tm, tm), 1)
    before = _dot((ci < ri).astype(BF16), oh) + carry_ref[...]
    r1 = jnp.sum(jnp.where(sel1, before, 0.0), axis=-1, keepdims=True)
    r2 = jnp.sum(jnp.where(sel2, before, 0.0), axis=-1, keepdims=True)
    carry_ref[...] = carry_ref[...] + jnp.sum(oh.astype(F32), axis=0, keepdims=True)
    cnt_ref[...] = carry_ref[...]

    rec = jnp.zeros((tm, LANES), F32)
    for slot, val in ((ROUTE_E1, i1), (ROUTE_E2, i2), (ROUTE_W1, w1), (ROUTE_W2, w2), (ROUTE_R1, r1), (ROUTE_R2, r2)):
        rec = jnp.where(lane_i == slot, val, rec)
    route_ref[...] = rec


def _merge(x, y_rec, att, sga, sgb, wb, wo, w_norm, w_router, b_router, cnt0):
    t, d = x.shape
    hw = y_rec.shape[1]
    tm = PROJ_TILE if t % PROJ_TILE == 0 else t
    row = lambda w: pl.BlockSpec((tm, w), lambda i: (i, 0))
    const = lambda a, b: pl.BlockSpec((a, b), lambda i: (0, 0))
    return pl.pallas_call(
        functools.partial(_merge_kernel, hw),
        grid=(t // tm,),
        in_specs=[row(d), row(hw), row(att.shape[1]), row(d), row(d), const(*wb.shape), const(d, d), const(1, d),
                  const(d, LANES), const(1, LANES), const(1, LANES)],
        out_specs=[row(d), row(d), row(LANES), const(1, LANES)],
        out_shape=[jax.ShapeDtypeStruct((t, d), F32), jax.ShapeDtypeStruct((t, d), F32),
                   jax.ShapeDtypeStruct((t, LANES), F32), jax.ShapeDtypeStruct((1, LANES), F32)],
        scratch_shapes=[pltpu.VMEM((1, LANES), F32)],
        compiler_params=_params("arbitrary"),
        name="merge_route",
    )(x, y_rec, att, sga, sgb, wb, wo, w_norm.reshape(1, d), w_router, b_router, cnt0)


def _row_copy(src, dst, sem):
    return pltpu.make_async_copy(src, dst, sem)


def _scatter_kernel(n_tok, dest_ref, x_ref, xs_in_ref, xs_ref, sem):
    del xs_in_ref

    def issue(r, _):
        for k in range(TOP_K):
            _row_copy(x_ref.at[pl.ds(r, 1)], xs_ref.at[pl.ds(dest_ref[0, 0, TOP_K * r + k], 1)], sem).start()
        return 0

    lax.fori_loop(0, n_tok, issue, 0)

    def drain(r, _):
        for k in range(TOP_K):
            _row_copy(x_ref.at[pl.ds(0, 1)], xs_ref.at[pl.ds(0, 1)], sem).wait()
        return 0

    lax.fori_loop(0, n_tok, drain, 0)


def _scatter_rows(x, dest, xs):
    t, d = x.shape
    tm = MOVE_TILE if t % MOVE_TILE == 0 else t
    dest3 = dest.reshape(t // tm, 1, tm * TOP_K)
    return pl.pallas_call(
        functools.partial(_scatter_kernel, tm),
        grid=(t // tm,),
        in_specs=[pl.BlockSpec((1, 1, tm * TOP_K), lambda i: (i, 0, 0), memory_space=pltpu.SMEM),
                  pl.BlockSpec((tm, d), lambda i: (i, 0)),
                  pl.BlockSpec(memory_space=pl.ANY)],
        out_specs=pl.BlockSpec(memory_space=pl.ANY),
        out_shape=jax.ShapeDtypeStruct(xs.shape, xs.dtype),
        scratch_shapes=[pltpu.SemaphoreType.DMA(())],
        input_output_aliases={2: 0},
        compiler_params=_params("arbitrary"),
        name="moe_scatter",
    )(dest3, x, xs)


def _expert_kernel(be_ref, nb_ref, x_ref, wg_ref, wu_ref, wd_ref, y_ref):
    i = pl.program_id(0)

    @pl.when(i < nb_ref[0])
    def _():
        x = x_ref[...].astype(BF16)
        g = _dot(x, wg_ref[0].astype(BF16))
        u = _dot(x, wu_ref[0].astype(BF16))
        hmid = (g * _sigmoid(g) * u).astype(BF16)
        y_ref[...] = _dot(hmid, wd_ref[0].astype(BF16))

    @pl.when(i >= nb_ref[0])
    def _():
        y_ref[...] = jnp.zeros_like(y_ref)


def _experts(xs, block_e, n_used, wg, wu, wd):
    p, d = xs.shape
    tm = EXPERT_TILE
    ff = wg.shape[-1]
    grid_spec = pltpu.PrefetchScalarGridSpec(
        num_scalar_prefetch=2,
        grid=(p // tm,),
        in_specs=[pl.BlockSpec((tm, d), lambda i, be, nb: (i, 0)),
                  pl.BlockSpec((1, d, ff), lambda i, be, nb: (be[i], 0, 0)),
                  pl.BlockSpec((1, d, ff), lambda i, be, nb: (be[i], 0, 0)),
                  pl.BlockSpec((1, ff, d), lambda i, be, nb: (be[i], 0, 0))],
        out_specs=pl.BlockSpec((tm, d), lambda i, be, nb: (i, 0)),
    )
    return pl.pallas_call(
        _expert_kernel,
        grid_spec=grid_spec,
        out_shape=jax.ShapeDtypeStruct((p, d), F32),
        compiler_params=_params("arbitrary"),
        name="moe_experts",
    )(block_e, n_used, xs, wg, wu, wd)


def _combine_kernel(n_tok, dcur_ref, dnext_ref, h_ref, route_ref, ys_ref, o_ref, buf, sem):
    i = pl.program_id(0)
    n = pl.num_programs(0)
    slot = i % 2

    def start(dref, s):
        def issue(r, _):
            for k in range(TOP_K):
                _row_copy(ys_ref.at[pl.ds(dref[0, 0, TOP_K * r + k], 1)], buf.at[s, k, pl.ds(r, 1)], sem.at[s]).start()
            return 0
        lax.fori_loop(0, n_tok, issue, 0)

    @pl.when(i == 0)
    def _():
        start(dcur_ref, 0)

    @pl.when(i + 1 < n)
    def _():
        start(dnext_ref, 1 - slot)

    def drain(r, _):
        for k in range(TOP_K):
            _row_copy(ys_ref.at[pl.ds(0, 1)], buf.at[slot, k, pl.ds(0, 1)], sem.at[slot]).wait()
        return 0

    lax.fori_loop(0, n_tok, drain, 0)
    route = route_ref[...]
    w1 = route[:, ROUTE_W1:ROUTE_W1 + 1]
    w2 = route[:, ROUTE_W2:ROUTE_W2 + 1]
    o_ref[...] = h_ref[...] + (buf[slot, 0] * w1 + buf[slot, 1] * w2)


def _combine(h, route, dest, ys):
    t, d = h.shape
    tm = MOVE_TILE if t % MOVE_TILE == 0 else t
    n = t // tm
    dest3 = dest.reshape(n, 1, tm * TOP_K)
    dspec = lambda f: pl.BlockSpec((1, 1, tm * TOP_K), f, memory_space=pltpu.SMEM)
    return pl.pallas_call(
        functools.partial(_combine_kernel, tm),
        grid=(n,),
        in_specs=[dspec(lambda i: (i, 0, 0)), dspec(lambda i: (jnp.minimum(i + 1, n - 1), 0, 0)),
                  pl.BlockSpec((tm, d), lambda i: (i, 0)), pl.BlockSpec((tm, LANES), lambda i: (i, 0)),
                  pl.BlockSpec(memory_space=pl.ANY)],
        out_specs=pl.BlockSpec((tm, d), lambda i: (i, 0)),
        out_shape=jax.ShapeDtypeStruct((t, d), F32),
        scratch_shapes=[pltpu.VMEM((2, TOP_K, tm, d), F32), pltpu.SemaphoreType.DMA((2,))],
        compiler_params=_params("arbitrary"),
        name="moe_combine",
    )(dest3, dest3, h, route, ys)


def _moe(parts, wg, wu, wd):
    tm = EXPERT_TILE
    n_assign = sum(part[0].shape[0] for part in parts) * TOP_K
    n_blocks = -(-(n_assign + N_EXPERTS * (tm - 1)) // tm)
    counts = parts[-1][3][0, :N_EXPERTS].astype(I32)
    pcounts = (counts + tm - 1) // tm * tm
    pends = jnp.cumsum(pcounts)
    pstarts = pends - pcounts
    block_e = jnp.minimum(jnp.searchsorted(pends, jnp.arange(n_blocks, dtype=I32) * tm, side='right'),
                          N_EXPERTS - 1).astype(I32)
    n_used = (pends[-1:] // tm).astype(I32)
    d = parts[0][0].shape[1]
    xs = jnp.zeros((n_blocks * tm, d), F32)
    dests = []
    for _, xn, route, _ in parts:
        e = route[:, ROUTE_E1:ROUTE_E2 + 1].astype(I32)
        rank = route[:, ROUTE_R1:ROUTE_R2 + 1].astype(I32)
        dest = pstarts[e] + rank
        dests.append(dest)
        xs = _scatter_rows(xn, dest, xs)
    ys = _experts(xs, block_e, n_used, wg, wu, wd)
    return [_combine(h, route, dest, ys) for (h, _, route, _), dest in zip(parts, dests)]


def kernel(x_prompt, x_sample, cache_swa_k, cache_swa_v, state_hgrn, meta_tokens, rel_bias_table, hgrn_lower_bounds, w_norm_mix, w_in, hgrn_out_norm, q_norm, k_norm, attn_sinks, w_branch, w_out, w_norm_ffn, w_router_group, b_router_group, w_router_expert, b_router_expert, w_expert_gate, w_expert_up, w_expert_down):
    b, s, d = x_prompt.shape
    bd, sd, _ = x_sample.shape
    depth, _, heads, dk, dv = state_hgrn.shape
    assert depth == 1 and heads == HGRN_HEADS and dk == dv
    hw = heads * dk
    aw = ATTN_HEADS * HEAD_DIM
    kvw = KV_HEADS * HEAD_DIM
    assert w_in.shape[-1] == 4 * hw + aw + 2 * kvw + 2 * d
    assert s % HGRN_CHUNK == 0 and s % CHUNK == 0 and sd == N_META and N_EXPERTS + N_GROUPS <= LANES
    l = 0

    p = jax.nn.softmax(hgrn_lower_bounds.astype(F32), axis=0)
    lb = jnp.cumsum(p, axis=0)[l + 1] - p[0]

    w_in_b = w_in[l].astype(BF16)
    proj = functools.partial(_inproj, w_norm=w_norm_mix[l], w_in_bf16=w_in_b, q_gain=q_norm[l], k_gain=k_norm[l],
                             hw=hw, aw=aw, kvw=kvw)
    x_small = jnp.concatenate([x_sample.reshape(bd * sd, d), meta_tokens.astype(F32)], axis=0)
    qr_s, z_s, hv_s, hg_s, qa_s, k_s, v_s, sga_s, sgb_s = proj(x_small)
    qr_p, z_p, hv_p, hg_p, qa_p, k_p, v_p, sga_p, sgb_p = proj(x_prompt.reshape(b * s, d))
    ns = bd * sd
    k_meta, v_meta = k_s[ns:], v_s[ns:]

    streams = lambda a, n: a.reshape(n, -1, a.shape[-1])
    s0_small = jnp.concatenate([state_hgrn[l].astype(F32), jnp.zeros((1, heads, dk, dv), F32)], axis=0)
    y_small, st_small = _hgrn(streams(qr_s, bd + 1), streams(z_s, bd + 1), streams(hv_s, bd + 1),
                              streams(hg_s, bd + 1), lb, hgrn_out_norm[l], s0_small, sd)
    s0_p = jnp.broadcast_to(st_small[bd:], (b, heads, dk, dv))
    y_p, st_p = _hgrn(streams(qr_p, b), streams(z_p, b), streams(hv_p, b), streams(hg_p, b), lb,
                      hgrn_out_norm[l], s0_p, HGRN_CHUNK)

    table = rel_bias_table.astype(F32)
    att_p = _attn_prompt(streams(qa_p, b), streams(k_p, b), streams(v_p, b), k_meta, v_meta, table, attn_sinks[l])
    kc = cache_swa_k[l].astype(F32).reshape(bd, -1, kvw)
    vc = cache_swa_v[l].astype(F32).reshape(bd, -1, kvw)
    k_new, v_new = k_s[:ns].reshape(bd, sd, kvw), v_s[:ns].reshape(bd, sd, kvw)
    att_s = _attn_sample(qa_s[:ns].reshape(bd, sd, aw), kc, vc, k_new, v_new, k_meta, v_meta, table, attn_sinks[l])

    wb = w_branch[l].astype(BF16)
    wo = w_out[l].astype(BF16)
    w_router = jnp.pad(jnp.concatenate([w_router_expert[l], w_router_group[l]], axis=1).astype(F32),
                       ((0, 0), (0, LANES - N_EXPERTS - N_GROUPS)))
    b_router = jnp.pad(jnp.concatenate([b_router_expert[l], b_router_group[l]]).astype(F32),
                       (0, LANES - N_EXPERTS - N_GROUPS)).reshape(1, LANES)
    merge = functools.partial(_merge, wb=wb, wo=wo, w_norm=w_norm_ffn[l], w_router=w_router, b_router=b_router)
    h_p, xn_p, route_p, cnt_p = merge(x_prompt.reshape(b * s, d), y_p.reshape(b * s, hw), att_p.reshape(b * s, aw),
                                      sga_p, sgb_p, cnt0=jnp.zeros((1, LANES), F32))
    h_s, xn_s, route_s, cnt_s = merge(x_sample.reshape(ns, d), y_small[:bd].reshape(ns, hw), att_s.reshape(ns, aw),
                                      sga_s[:ns], sgb_s[:ns], cnt0=cnt_p)

    out_p, out_s = _moe([(h_p, xn_p, route_p, cnt_p), (h_s, xn_s, route_s, cnt_s)],
                        w_expert_gate[l], w_expert_up[l], w_expert_down[l])

    tail = lambda a: a.reshape(b, s, KV_HEADS, HEAD_DIM)[:, s - WINDOW:][None]
    roll = lambda cache, new: jnp.concatenate([cache, new], axis=1)[:, -WINDOW:].reshape(
        bd, WINDOW, KV_HEADS, HEAD_DIM)[None]
    return (out_p.reshape(b, s, d), out_s.reshape(bd, sd, d),
            tail(k_p), tail(v_p), st_p[None],
            roll(kc, k_new), roll(vc, v_new), st_small[:bd][None])
```

```python
import functools
import math

import numpy as np
import jax
import jax.numpy as jnp
from jax import lax
from jax.experimental import pallas as pl
from jax.experimental.pallas import tpu as pltpu

F32 = jnp.float32
BF16 = jnp.bfloat16
I32 = jnp.int32

CHUNK = 64
N_META = 16
PAST_LEN = 2048
EPS = 1e-6
HGRN_HEADS = 4
ATTN_HEADS = 8
KV_HEADS = 2
HEAD_DIM = 64
GQA_GROUP = ATTN_HEADS // KV_HEADS
WINDOW = 128
WINDOW_CHUNKS = WINDOW // CHUNK
NUM_BUCKETS = 32
MAX_DISTANCE = 128
N_GROUPS = 4
EXPERTS_PER_GROUP = 8
N_EXPERTS = N_GROUPS * EXPERTS_PER_GROUP
TOP_K = 2

LANES = 128
VMEM_LIMIT = 56 * 1024 * 1024

PROJ_TILE = 512
HGRN_CHUNK = 128
ATTN_CHUNKS_PER_STEP = 4
EXPERT_TILE = 256
MOVE_TILE = 256


def _sigmoid(x):
    return 1.0 / (1.0 + jnp.exp(-x))


def _split3(x):
    hi = x.astype(BF16)
    r1 = x - hi.astype(F32)
    mid = r1.astype(BF16)
    lo = (r1 - mid.astype(F32)).astype(BF16)
    return hi, mid, lo


def _dot(a, b):
    return jnp.dot(a, b, preferred_element_type=F32)


def _dot_nt(a, b):
    return lax.dot_general(a, b, (((1,), (1,)), ((), ())), preferred_element_type=F32)


def _dot_tn(a, b):
    return lax.dot_general(a, b, (((0,), (0,)), ((), ())), preferred_element_type=F32)


def _params(*sem):
    return pltpu.CompilerParams(dimension_semantics=sem, vmem_limit_bytes=VMEM_LIMIT)


def _inproj_kernel(hw, aw, kvw, d, x_ref, wn_ref, w_ref, qg_ref, kg_ref, bdq_ref, bdk_ref,
                   qr_ref, z_ref, hv_ref, hg_ref, qa_ref, k_ref, v_ref, sga_ref, sgb_ref):
    x = x_ref[...]
    ms = jnp.mean(x * x, axis=-1, keepdims=True)
    xn = (x * lax.rsqrt(ms + EPS) * wn_ref[...]).astype(BF16)

    def seg(a, b):
        return _dot(xn, w_ref[:, a:b])

    def head_rms(a, bd_ref, gain):
        sq = a * a
        hi = sq.astype(BF16)
        lo = (sq - hi.astype(F32)).astype(BF16)
        m = _dot(hi, bd_ref[...]) + _dot(lo, bd_ref[...])
        return a * lax.rsqrt(m + EPS) * gain

    o = 0
    hq = seg(o, o + hw)
    qr_ref[...] = (hq * _sigmoid(hq) * (hw // HGRN_HEADS) ** -0.5).astype(BF16)
    o += hw
    z_ref[...] = seg(o, o + hw)
    o += hw
    hv_ref[...] = seg(o, o + hw).astype(BF16)
    o += hw
    hg_ref[...] = seg(o, o + hw).astype(BF16)
    o += hw
    aq = seg(o, o + aw)
    qa_ref[...] = (head_rms(aq, bdq_ref, qg_ref[...]) * HEAD_DIM ** -0.5).astype(BF16)
    o += aw
    k_ref[...] = head_rms(seg(o, o + kvw), bdk_ref, kg_ref[...])
    o += kvw
    v_ref[...] = seg(o, o + kvw)
    o += kvw
    sga_ref[...] = _sigmoid(seg(o, o + d)).astype(BF16)
    o += d
    sgb_ref[...] = _sigmoid(seg(o, o + d)).astype(BF16)


def _block_diag_mean(width, group):
    i = np.arange(width)
    return jnp.asarray((i[:, None] // group == i[None, :] // group) / group, dtype=BF16)


def _inproj(x, w_norm, w_in_bf16, q_gain, k_gain, hw, aw, kvw):
    t, d = x.shape
    tm = PROJ_TILE if t % PROJ_TILE == 0 else t
    cols = w_in_bf16.shape[1]
    row = lambda w: pl.BlockSpec((tm, w), lambda i: (i, 0))
    const = lambda a, b: pl.BlockSpec((a, b), lambda i: (0, 0))
    outs = [(hw, BF16), (hw, F32), (hw, BF16), (hw, BF16), (aw, BF16), (kvw, F32), (kvw, F32), (d, BF16), (d, BF16)]
    return pl.pallas_call(
        functools.partial(_inproj_kernel, hw, aw, kvw, d),
        grid=(t // tm,),
        in_specs=[row(d), const(1, d), const(d, cols), const(1, aw), const(1, kvw), const(aw, aw), const(kvw, kvw)],
        out_specs=[row(w) for w, _ in outs],
        out_shape=[jax.ShapeDtypeStruct((t, w), dt) for w, dt in outs],
        compiler_params=_params("arbitrary"),
        name="inproj",
    )(x, w_norm.reshape(1, d), w_in_bf16,
      jnp.tile(q_gain, aw // HEAD_DIM).reshape(1, aw), jnp.tile(k_gain, kvw // HEAD_DIM).reshape(1, kvw),
      _block_diag_mean(aw, HEAD_DIM), _block_diag_mean(kvw, HEAD_DIM))


def _hgrn_consts(L):
    t = np.arange(L)
    u = t[None, :]
    blocks = [u <= t[:, None], u > t[:, None]]
    levels = []
    m = L // 2
    while m >= 1:
        levels.append(m)
        m //= 2
    lvl = np.full((L, L), -1, np.int32)
    lvl[t, t] = len(levels)
    isq_cols = []
    for j, m in enumerate(levels):
        bnd = (t // (2 * m)) * (2 * m) + m - 1
        isq = (t % (2 * m)) >= m
        cq = isq[:, None] & (u > bnd[:, None]) & (u <= t[:, None])
        ck = (~isq)[:, None] & (u > t[:, None]) & (u <= bnd[:, None])
        blocks.append(cq | ck)
        same = (t[:, None] // (2 * m)) == (t[None, :] // (2 * m))
        lvl[same & isq[:, None] & (~isq)[None, :]] = j
        isq_cols.append(isq)
    c = np.concatenate(blocks, axis=0).astype(np.float32)
    isq = np.stack(isq_cols, axis=1).astype(np.float32)
    isq = np.pad(isq, ((0, 0), (0, LANES - isq.shape[1])))
    return jnp.asarray(c, dtype=BF16), jnp.asarray(lvl), jnp.asarray(isq), len(levels)


def _hgrn_kernel(L, nlev, heads, dk, qr_ref, z_ref, hv_ref, hg_ref, lb_ref, og_ref, c_ref, lvl_ref, isq_ref,
                 s0_ref, y_ref, sout_ref, st_ref):
    c = pl.program_id(1)

    @pl.when(c == 0)
    def _():
        for h in range(heads):
            st_ref[h] = s0_ref[0, h].T

    z = z_ref[0]
    lb = lb_ref[...]
    e = jnp.exp(-jnp.abs(z))
    r = 1.0 / (1.0 + e)
    pos = z >= 0
    sig = jnp.where(pos, r, e * r)
    sig_neg = jnp.where(pos, e * r, r)
    logf = jnp.log(lb + (1.0 - lb) * sig)
    kin = (1.0 - lb) * sig_neg
    q = qr_ref[0].astype(F32)

    hi, mid, lo = _split3(logf)
    cm = c_ref[...]
    ex = jnp.exp(_dot(cm, hi) + _dot(cm, mid) + _dot(cm, lo))
    e_b = ex[0:L]
    e_rev = ex[L:2 * L]

    q_in = (q * e_b).astype(BF16)
    k_out = (kin * e_rev).astype(BF16)
    q_b = q.astype(BF16)
    k_b = kin.astype(BF16)
    xs = []
    for j in range(nlev):
        isq = isq_ref[:, j:j + 1] > 0.5
        xs.append((jnp.where(isq, q, kin) * ex[(2 + j) * L:(3 + j) * L]).astype(BF16))
    lvl = lvl_ref[...]
    v = hv_ref[0]
    g = hg_ref[0].astype(F32)
    og = og_ref[...]

    for h in range(heads):
        sl = slice(h * dk, (h + 1) * dk)
        a = jnp.where(lvl == nlev, _dot_nt(q_b[:, sl], k_b[:, sl]), 0.0)
        for j in range(nlev):
            xh = xs[j][:, sl]
            a = jnp.where(lvl == j, _dot_nt(xh, xh), a)
        st = st_ref[h]
        vh = v[:, sl]
        o = _dot(a.astype(BF16), vh) + _dot_nt(q_in[:, sl], st.astype(BF16))
        st_ref[h] = st * e_b[L - 1:L, sl] + _dot_tn(vh, k_out[:, sl])
        ms = jnp.mean(o * o, axis=-1, keepdims=True)
        gh = g[:, sl]
        y_ref[0, :, sl] = (o * lax.rsqrt(ms + EPS) * og[:, sl] * (gh * _sigmoid(gh))).astype(BF16)

    @pl.when(c == pl.num_programs(1) - 1)
    def _():
        for h in range(heads):
            sout_ref[0, h] = st_ref[h].T


def _hgrn(qr, z, hv, hg, lb, out_gain, s0, L):
    b, s, w = z.shape
    heads, dk = s0.shape[1], s0.shape[2]
    cm, lvl, isq, nlev = _hgrn_consts(L)
    seq = pl.BlockSpec((1, L, w), lambda i, c: (i, c, 0))
    const = lambda a: pl.BlockSpec(a.shape, lambda i, c: (0,) * a.ndim)
    state = pl.BlockSpec((1, heads, dk, dk), lambda i, c: (i, 0, 0, 0))
    lb2 = lb.reshape(1, w)
    og2 = jnp.tile(out_gain, heads).reshape(1, w)
    return pl.pallas_call(
        functools.partial(_hgrn_kernel, L, nlev, heads, dk),
        grid=(b, s // L),
        in_specs=[seq, seq, seq, seq, const(lb2), const(og2), const(cm), const(lvl), const(isq), state],
        out_specs=[seq, state],
        out_shape=[jax.ShapeDtypeStruct((b, s, w), BF16), jax.ShapeDtypeStruct(s0.shape, F32)],
        scratch_shapes=[pltpu.VMEM((heads, dk, dk), F32)],
        compiler_params=_params("arbitrary", "arbitrary"),
        name=f"hgrn_scan_{L}",
    )(qr, z, hv, hg, lb2, og2, cm, lvl, isq, s0)


def _t5_bucket_np(rel):
    half = NUM_BUCKETS // 2
    max_exact = half // 2
    assert (NUM_BUCKETS, MAX_DISTANCE) == (32, 128)
    n = np.abs(rel).astype(np.int64)
    nn = np.maximum(n, 1)
    k = np.zeros_like(nn)
    for j in range(1, 48):
        k = np.where(64 * (1 << j) <= nn * nn, j, k)
    large = np.minimum(max_exact + k, half - 1)
    return np.where(rel > 0, half, 0) + np.where(n < max_exact, n, large)


HEADS_PER_COL = LANES // HEAD_DIM
COLS_PER_GROUP = GQA_GROUP // HEADS_PER_COL
HEAD_ORDER = tuple(g * GQA_GROUP + col * HEADS_PER_COL + half
                   for g in range(KV_HEADS) for half in range(HEADS_PER_COL) for col in range(COLS_PER_GROUP))
KV_EXPAND = KV_HEADS * HEADS_PER_COL


def _expand_kv(x):
    assert HEADS_PER_COL == 2 and KV_HEADS == 2 and x.shape[1] == LANES
    low = lax.broadcasted_iota(I32, x.shape, 1) < HEAD_DIM
    xr = pltpu.roll(x, HEAD_DIM, axis=1)
    zero = jnp.zeros_like(x)
    blocks = [jnp.where(low, x, zero), jnp.where(low, zero, xr), jnp.where(low, xr, zero), jnp.where(low, zero, x)]
    return jnp.concatenate(blocks, axis=1).astype(BF16)


def _attn_core(q, kx, vx, bias, sink):
    tq = q.shape[0]
    scores = []
    for g in range(KV_HEADS):
        cols = [q[:, (g * COLS_PER_GROUP + c) * LANES:(g * COLS_PER_GROUP + c + 1) * LANES]
                for c in range(COLS_PER_GROUP)]
        qst = jnp.concatenate(cols, axis=0)
        for half in range(HEADS_PER_COL):
            blk = g * HEADS_PER_COL + half
            scores.append(_dot_nt(qst, kx[:, blk * LANES:(blk + 1) * LANES]))
    s = jnp.concatenate(scores, axis=0) + bias
    m =jnp.maximum(jnp.max(s, axis=-1, keepdims=True), sink)
    p = jnp.exp(s - m)
    den = jnp.sum(p, axis=-1, keepdims=True) + jnp.exp(sink - m)
    pn = (p * (1.0 / den)).astype(BF16)
    rows = COLS_PER_GROUP * tq
    outs = []
    for g in range(KV_HEADS):
        o = None
        for half in range(HEADS_PER_COL):
            blk = g * HEADS_PER_COL + half
            part = _dot(pn[blk * rows:(blk + 1) * rows], vx[:, blk * LANES:(blk + 1) * LANES])
            o = part if o is None else o + part
        outs.extend(o[c * tq:(c + 1) * tq] for c in range(COLS_PER_GROUP))
    return outs


def _bias_rows(table, bucket):
    onehot = (jnp.asarray(bucket)[..., None] == jnp.arange(NUM_BUCKETS)).astype(F32)
    cols = jnp.stack([table[:, h] for h in HEAD_ORDER], axis=1)
    bias = jnp.einsum('...qkb,bh->...hqk', onehot, cols, precision=lax.Precision.HIGHEST)
    return bias.reshape(*bucket.shape[:-2], ATTN_HEADS * bucket.shape[-2], bucket.shape[-1])


def _pad_keys(bias):
    tk = bias.shape[-1]
    pad = [(0, 0)] * (bias.ndim - 1) + [(0, -tk % LANES)]
    return jnp.pad(bias, pad, constant_values=-jnp.inf)


def _sink_rows(sinks, tq):
    return jnp.repeat(jnp.stack([sinks[h] for h in HEAD_ORDER]).astype(F32), tq).reshape(ATTN_HEADS * tq, 1)


def _attn_prompt_kernel(cb, q_ref, k_ref, v_ref, km_ref, vm_ref, bias_ref, sink_ref, o_ref, kx_ref, vx_ref):
    step = pl.program_id(1)
    s_len = k_ref.shape[1]
    meta_at = WINDOW + s_len

    @pl.when(step == 0)
    def _():
        piece = min(s_len, 512)
        for src, meta, dst in ((k_ref, km_ref, kx_ref), (v_ref, vm_ref, vx_ref)):
            dst[0:WINDOW] = jnp.zeros((WINDOW, dst.shape[1]), BF16)
            for r in range(0, s_len, piece):
                dst[WINDOW + r:WINDOW + r + piece] = _expand_kv(src[0, r:r + piece, :])
            dst[meta_at:meta_at + N_META] = _expand_kv(meta[...])
            dst[meta_at + N_META:] = jnp.zeros((dst.shape[0] - meta_at - N_META, dst.shape[1]), BF16)

    win = WINDOW + CHUNK
    tail = kx_ref.shape[0] - meta_at
    for j in range(cb):
        c = step * cb + j
        start = pl.multiple_of(c * CHUNK, CHUNK)
        kall = jnp.concatenate([kx_ref[pl.ds(start, win), :], kx_ref[meta_at:meta_at + tail, :]], axis=0)
        vall = jnp.concatenate([vx_ref[pl.ds(start, win), :], vx_ref[meta_at:meta_at + tail, :]], axis=0)
        rows = slice(j * CHUNK, (j + 1) * CHUNK)
        outs = _attn_core(q_ref[0, rows, :], kall, vall, bias_ref[jnp.minimum(c, bias_ref.shape[0] - 1)],
                          sink_ref[...])
        for ci, o in enumerate(outs):
            o_ref[0, rows, ci * LANES:(ci + 1) * LANES] = o.astype(o_ref.dtype)


def _attn_prompt(q, k, v, k_meta, v_meta, table, sinks):
    b, s, aw = q.shape
    kvw = k.shape[-1]
    nc = s // CHUNK
    cb = ATTN_CHUNKS_PER_STEP if nc % ATTN_CHUNKS_PER_STEP == 0 else 1
    assert s % min(s, 512) == 0
    n_bias = 1
    while True:
        qpos = N_META + (n_bias - 1) * CHUNK
        if np.all(_t5_bucket_np(np.arange(N_META) - qpos) == _t5_bucket_np(np.arange(N_META) - qpos - 10 ** 6)):
            break
        n_bias += 1
    n_bias = min(max(n_bias, WINDOW_CHUNKS + 1), nc)
    cs = np.arange(n_bias)[:, None]
    qpos = N_META + cs * CHUNK + np.arange(CHUNK)[None]
    wpos = N_META + (cs - WINDOW_CHUNKS) * CHUNK + np.arange(WINDOW + CHUNK)[None]
    kpos = np.concatenate([wpos, np.broadcast_to(np.arange(N_META), (n_bias, N_META))], axis=1)
    valid = np.concatenate([wpos >= N_META, np.ones((n_bias, N_META), bool)], axis=1)
    bias = _bias_rows(table, _t5_bucket_np(kpos[:, None, :] - qpos[:, :, None]))
    bias = _pad_keys(jnp.where(valid[:, None, :], bias, -jnp.inf))
    sink = _sink_rows(sinks, CHUNK)
    qs = pl.BlockSpec((1, cb * CHUNK, aw), lambda i, c: (i, c, 0))
    kv = pl.BlockSpec((1, s, kvw), lambda i, c: (i, 0, 0))
    meta = pl.BlockSpec((N_META, kvw), lambda i, c: (0, 0))
    xrows = WINDOW + s + bias.shape[-1] - (WINDOW + CHUNK)
    return pl.pallas_call(
        functools.partial(_attn_prompt_kernel, cb),
        grid=(b, nc // cb),
        in_specs=[qs, kv, kv, meta, meta, pl.BlockSpec(bias.shape, lambda i, c: (0, 0, 0)),
                  pl.BlockSpec(sink.shape, lambda i, c: (0, 0))],
        out_specs=qs,
        out_shape=jax.ShapeDtypeStruct((b, s, aw), BF16),
        scratch_shapes=[pltpu.VMEM((xrows, KV_EXPAND * LANES), BF16), pltpu.VMEM((xrows, KV_EXPAND * LANES), BF16)],
        compiler_params=_params("arbitrary", "arbitrary"),
        name="attn_prompt",
    )(q, k, v, k_meta, v_meta, bias, sink)


def _attn_sample_kernel(q_ref, kc_ref, vc_ref, kn_ref, vn_ref, km_ref, vm_ref, bias_ref, sink_ref, o_ref):
    tk = kc_ref.shape[1] + kn_ref.shape[1] + km_ref.shape[0]
    zeros = jnp.zeros((bias_ref.shape[1] - tk, km_ref.shape[1]), F32)
    kall = _expand_kv(jnp.concatenate([kc_ref[0], kn_ref[0], km_ref[...], zeros], axis=0))
    vall = _expand_kv(jnp.concatenate([vc_ref[0], vn_ref[0], vm_ref[...], zeros], axis=0))
    outs = _attn_core(q_ref[0], kall, vall, bias_ref[...], sink_ref[...])
    for ci, o in enumerate(outs):
        o_ref[0, :, ci * LANES:(ci + 1) * LANES] = o.astype(o_ref.dtype)


def _attn_sample(q, k_cache, v_cache, k_new, v_new, k_meta, v_meta, table, sinks):
    bd, sd, aw = q.shape
    kvw = k_new.shape[-1]
    win = k_cache.shape[1]
    qpos = N_META + PAST_LEN + np.arange(sd)
    kpos = np.concatenate([N_META + PAST_LEN - win + np.arange(win), qpos, np.arange(N_META)])
    bias = _pad_keys(_bias_rows(table, _t5_bucket_np(kpos[None, :] - qpos[:, None])))
    sink = _sink_rows(sinks, sd)
    per = lambda n, w: pl.BlockSpec((1, n, w), lambda i: (i, 0, 0))
    meta = pl.BlockSpec((N_META, kvw), lambda i: (0, 0))
    return pl.pallas_call(
        _attn_sample_kernel,
        grid=(bd,),
        in_specs=[per(sd, aw), per(win, kvw), per(win, kvw), per(sd, kvw), per(sd, kvw), meta, meta,
                  pl.BlockSpec(bias.shape, lambda i: (0, 0)), pl.BlockSpec(sink.shape, lambda i: (0, 0))],
        out_specs=per(sd, aw),
        out_shape=jax.ShapeDtypeStruct((bd, sd, aw), BF16),
        compiler_params=_params("arbitrary"),
        name="attn_sample",
    )(q, k_cache, v_cache, k_new, v_new, k_meta, v_meta, bias, sink)


ROUTE_E1, ROUTE_E2, ROUTE_W1, ROUTE_W2, ROUTE_R1, ROUTE_R2 = range(6)


def _merge_kernel(hw, x_ref, yr_ref, at_ref, sga_ref, sgb_ref, wb_ref, wo_ref, wn_ref, wr_ref, br_ref, cnt0_ref,
                  h_ref, xn_ref, route_ref, cnt_ref, carry_ref):
    i = pl.program_id(0)

    @pl.when(i == 0)
    def _():
        carry_ref[...] = cnt0_ref[...]

    br = _dot(yr_ref[...], wb_ref[0:hw, :])
    ba = _dot(at_ref[...], wb_ref[hw:, :])
    merged = sga_ref[...].astype(F32) * br + sgb_ref[...].astype(F32) * ba
    h = x_ref[...] + _dot(merged.astype(BF16), wo_ref[...])
    h_ref[...] = h
    ms = jnp.mean(h * h, axis=-1, keepdims=True)
    xn = h * lax.rsqrt(ms + EPS) * wn_ref[...]
    xn_ref[...] = xn

    x_hi = xn.astype(BF16)
    x_lo = (xn - x_hi.astype(F32)).astype(BF16)
    w = wr_ref[...]
    w_hi = w.astype(BF16)
    w_lo = (w - w_hi.astype(F32)).astype(BF16)
    logits = _dot(x_hi, w_hi) + _dot(x_hi, w_lo) + _dot(x_lo, w_hi) + br_ref[...]

    tm = logits.shape[0]
    lane_i = lax.broadcasted_iota(I32, (tm, LANES), 1)
    lane = lane_i.astype(F32)
    group_of_lane = (lane_i >> int(math.log2(EXPERTS_PER_GROUP))).astype(F32)
    ninf = -jnp.inf
    first = lambda hit, idx: jnp.min(jnp.where(hit, idx, float(LANES)), axis=-1, keepdims=True)
    gmask = (lane_i >= N_EXPERTS) & (lane_i < N_EXPERTS + N_GROUPS)
    gl = jnp.where(gmask, logits, ninf)
    gmax = jnp.max(gl, axis=-1, keepdims=True)
    gidx = first(gl == gmax, lane - N_EXPERTS)
    gval = 1.0 / jnp.sum(jnp.exp(gl - gmax), axis=-1, keepdims=True)
    emask = (lane_i < N_EXPERTS) & (group_of_lane == gidx)
    el = jnp.where(emask, logits, ninf)
    m1 = jnp.max(el, axis=-1, keepdims=True)
    i1 = first(el == m1, lane)
    el2 = jnp.where(lane == i1, ninf, el)
    m2 = jnp.max(el2, axis=-1, keepdims=True)
    i2 = first(el2 == m2, lane)
    e21 = jnp.exp(m2 - m1)
    w1 = gval / (1.0 + e21)
    w2 = gval * e21 / (1.0 + e21)

    sel1 = lane == i1
    sel2 = lane == i2
    oh = (sel1 | sel2).astype(BF16)
    ri = lax.broadcasted_iota(I32, (tm, tm), 0)
    ci = lax.broadcasted_iota(I32, (tm, tm), 1)
    before = _dot((ci < ri).astype(BF16), oh) + carry_ref[...]
    r1 = jnp.sum(jnp.where(sel1, before, 0.0), axis=-1, keepdims=True)
    r2 = jnp.sum(jnp.where(sel2, before, 0.0), axis=-1, keepdims=True)
    carry_ref[...] = carry_ref[...] + jnp.sum(oh.astype(F32), axis=0, keepdims=True)
    cnt_ref[...] = carry_ref[...]

    rec = jnp.zeros((tm, LANES), F32)
    for slot, val in ((ROUTE_E1, i1), (ROUTE_E2, i2), (ROUTE_W1, w1), (ROUTE_W2, w2), (ROUTE_R1, r1), (ROUTE_R2, r2)):
        rec = jnp.where(lane_i == slot, val, rec)
    route_ref[...] = rec


def _merge(x, y_rec, att, sga, sgb, wb, wo, w_norm, w_router, b_router, cnt0):
    t, d = x.shape
    hw = y_rec.shape[1]
    tm = PROJ_TILE if t % PROJ_TILE == 0 else t
    row = lambda w: pl.BlockSpec((tm, w), lambda i: (i, 0))
    const = lambda a, b: pl.BlockSpec((a, b), lambda i: (0, 0))
    return pl.pallas_call(
        functools.partial(_merge_kernel, hw),
        grid=(t // tm,),
        in_specs=[row(d), row(hw), row(att.shape[1]), row(d), row(d), const(*wb.shape), const(d, d), const(1, d),
                  const(d, LANES), const(1, LANES), const(1, LANES)],
        out_specs=[row(d), row(d), row(LANES), const(1, LANES)],
        out_shape=[jax.ShapeDtypeStruct((t, d), F32), jax.ShapeDtypeStruct((t, d), F32),
                   jax.ShapeDtypeStruct((t, LANES), F32), jax.ShapeDtypeStruct((1, LANES), F32)],
        scratch_shapes=[pltpu.VMEM((1, LANES), F32)],
        compiler_params=_params("arbitrary"),
        name="merge_route",
    )(x, y_rec, att, sga, sgb, wb, wo, w_norm.reshape(1, d), w_router, b_router, cnt0)


def _row_copy(src, dst, sem):
    return pltpu.make_async_copy(src, dst, sem)


META_FIELDS = 2 * TOP_K
ROW_UNROLL = 8


def _slot(meta_ref, pstart_ref, r, k):
    return pstart_ref[meta_ref[0, 0, META_FIELDS * r + k]] + meta_ref[0, 0, META_FIELDS * r + TOP_K + k]


def _scatter_kernel(n_tok, first, meta_ref, pstart_ref, pend_ref, x_ref, *rest):
    xs_ref, zero_ref, sem, zsem = rest[-4:]

    if first:
        @pl.when(pl.program_id(0) == 0)
        def _():
            zero_ref[...] = jnp.zeros_like(zero_ref)
            n_rows = xs_ref.shape[0]
            used = pend_ref[N_EXPERTS - 1]
            blocks = [(pend_ref[e] > pstart_ref[e], pend_ref[e] - EXPERT_TILE) for e in range(N_EXPERTS)]
            blocks += [(n_rows - (j + 1) * EXPERT_TILE >= used, n_rows - (j + 1) * EXPERT_TILE)
                       for j in range(N_EXPERTS)]
            for wait in (False, True):
                for cond, row in blocks:
                    @pl.when(cond)
                    def _():
                        at = row if isinstance(row, int) else pl.multiple_of(row, EXPERT_TILE)
                        cp = _row_copy(zero_ref, xs_ref.at[pl.ds(at, EXPERT_TILE)], zsem)
                        cp.wait() if wait else cp.start()

    def issue(grp, _):
        for u in range(ROW_UNROLL):
            r = grp * ROW_UNROLL + u
            for k in range(TOP_K):
                _row_copy(x_ref.at[pl.ds(r, 1)], xs_ref.at[pl.ds(_slot(meta_ref, pstart_ref, r, k), 1)],
                          sem).start(priority=(u * TOP_K + k) % 2)
        return 0

    lax.fori_loop(0, n_tok // ROW_UNROLL, issue, 0)
    for k in range(TOP_K):
        _row_copy(x_ref, xs_ref.at[pl.ds(0, n_tok)], sem).wait()


def _scatter_rows(x, meta, pstarts, pends, xs, rows):
    t, d = x.shape
    tm = meta.shape[-1] // META_FIELDS
    first = xs is None
    smem = pl.BlockSpec(memory_space=pltpu.SMEM)
    return pl.pallas_call(
        functools.partial(_scatter_kernel, tm, first),
        grid=(t // tm,),
        in_specs=[pl.BlockSpec((1, 1, tm * META_FIELDS), lambda i: (i, 0, 0), memory_space=pltpu.SMEM), smem, smem,
                  pl.BlockSpec((tm, d), lambda i: (i, 0))] + ([] if first else [pl.BlockSpec(memory_space=pl.ANY)]),
        out_specs=pl.BlockSpec(memory_space=pl.ANY),
        out_shape=jax.ShapeDtypeStruct((rows, d), F32),
        scratch_shapes=[pltpu.VMEM((EXPERT_TILE, d), F32), pltpu.SemaphoreType.DMA(()), pltpu.SemaphoreType.DMA(())],
        input_output_aliases={} if first else {4: 0},
        compiler_params=_params("arbitrary"),
        name="moe_scatter",
    )(meta, pstarts, pends, x, *([] if first else [xs]))


def _expert_kernel(be_ref, nb_ref, x_ref, wg_ref, wu_ref, wd_ref, y_ref):
    i = pl.program_id(0)

    @pl.when(i < nb_ref[0])
    def _():
        x = x_ref[...].astype(BF16)
        g = _dot(x, wg_ref[0].astype(BF16))
        u = _dot(x, wu_ref[0].astype(BF16))
        hmid = (g * _sigmoid(g) * u).astype(BF16)
        y_ref[...] = _dot(hmid, wd_ref[0].astype(BF16))

    @pl.when(i >= nb_ref[0])
    def _():
        y_ref[...] = jnp.zeros_like(y_ref)


def _experts(xs, block_e, n_used, wg, wu, wd):
    p, d = xs.shape
    tm = EXPERT_TILE
    ff = wg.shape[-1]
    grid_spec = pltpu.PrefetchScalarGridSpec(
        num_scalar_prefetch=2,
        grid=(p // tm,),
        in_specs=[pl.BlockSpec((tm, d), lambda i, be, nb: (i, 0)),
                  pl.BlockSpec((1, d, ff), lambda i, be, nb: (be[i], 0, 0)),
                  pl.BlockSpec((1, d, ff), lambda i, be, nb: (be[i], 0, 0)),
                  pl.BlockSpec((1, ff, d), lambda i, be, nb: (be[i], 0, 0))],
        out_specs=pl.BlockSpec((tm, d), lambda i, be, nb: (i, 0)),
    )
    return pl.pallas_call(
        _expert_kernel,
        grid_spec=grid_spec,
        out_shape=jax.ShapeDtypeStruct((p, d), F32),
        compiler_params=_params("arbitrary"),
        name="moe_experts",
    )(block_e, n_used, xs, wg, wu, wd)


def _combine_kernel(n_tok, mcur_ref, mnext_ref, pstart_ref, h_ref, route_ref, ys_ref, o_ref, buf, sem):
    i = pl.program_id(0)
    n = pl.num_programs(0)
    slot = i % 2

    def start(meta_ref, s):
        def issue(grp, _):
            for u in range(ROW_UNROLL):
                r = grp * ROW_UNROLL + u
                for k in range(TOP_K):
                    _row_copy(ys_ref.at[pl.ds(_slot(meta_ref, pstart_ref, r, k), 1)], buf.at[s, k, pl.ds(r, 1)],
                              sem.at[s]).start(priority=(u * TOP_K + k) % 2)
            return 0
        lax.fori_loop(0, n_tok // ROW_UNROLL, issue, 0)

    @pl.when(i == 0)
    def _():
        start(mcur_ref, 0)

    @pl.when(i + 1 < n)
    def _():
        start(mnext_ref, 1 - slot)

    for k in range(TOP_K):
        _row_copy(ys_ref.at[pl.ds(0, n_tok)], buf.at[slot, k], sem.at[slot]).wait()
    route = route_ref[...]
    w1 = route[:, ROUTE_W1:ROUTE_W1 + 1]
    w2 = route[:, ROUTE_W2:ROUTE_W2 + 1]
    o_ref[...] = h_ref[...] + (buf[slot, 0] * w1 + buf[slot, 1] * w2)


def _combine(h, route, meta, pstarts, ys):
    t, d = h.shape
    n = meta.shape[0]
    tm = t // n
    mspec = lambda f: pl.BlockSpec((1, 1, tm * META_FIELDS), f, memory_space=pltpu.SMEM)
    return pl.pallas_call(
        functools.partial(_combine_kernel, tm),
        grid=(n,),
        in_specs=[mspec(lambda i: (i, 0, 0)), mspec(lambda i: (jnp.minimum(i + 1, n - 1), 0, 0)),
                  pl.BlockSpec(memory_space=pltpu.SMEM),
                  pl.BlockSpec((tm, d), lambda i: (i, 0)), pl.BlockSpec((tm, LANES), lambda i: (i, 0)),
                  pl.BlockSpec(memory_space=pl.ANY)],
        out_specs=pl.BlockSpec((tm, d), lambda i: (i, 0)),
        out_shape=jax.ShapeDtypeStruct((t, d), F32),
        scratch_shapes=[pltpu.VMEM((2, TOP_K, tm, d), F32), pltpu.SemaphoreType.DMA((2,))],
        compiler_params=_params("arbitrary"),
        name="moe_combine",
    )(meta, meta, pstarts, h, route, ys)


def _moe(parts, counts, wg, wu, wd):
    tm = EXPERT_TILE
    n_assign = sum(part[0].shape[0] for part in parts) * TOP_K
    n_blocks = -(-(n_assign + N_EXPERTS * (tm - 1)) // tm)
    counts = counts[0, :N_EXPERTS].astype(I32)
    pcounts = (counts + tm - 1) // tm * tm
    pends = jnp.cumsum(pcounts)
    pstarts = pends - pcounts
    block_start = jnp.arange(n_blocks, dtype=I32) * tm
    block_e = jnp.minimum(jnp.sum((pends[None, :] <= block_start[:, None]).astype(I32), axis=1), N_EXPERTS - 1)
    n_used = pends[-1:] // tm
    xs = None
    metas = []
    for _, xn, route in parts:
        t = xn.shape[0]
        mt = MOVE_TILE if t % MOVE_TILE == 0 else t
        assert mt % ROW_UNROLL == 0
        ids = jnp.concatenate([route[:, ROUTE_E1:ROUTE_E2 + 1], route[:, ROUTE_R1:ROUTE_R2 + 1]], axis=1)
        metas.append(ids.astype(I32).reshape(t // mt, 1, mt * META_FIELDS))
        xs = _scatter_rows(xn, metas[-1], pstarts, pends, xs, n_blocks * tm)
    ys = _experts(xs, block_e, n_used, wg, wu, wd)
    return [_combine(h, route, meta, pstarts, ys) for (h, _, route), meta in zip(parts, metas)]


def kernel(x_prompt, x_sample, cache_swa_k, cache_swa_v, state_hgrn, meta_tokens, rel_bias_table, hgrn_lower_bounds, w_norm_mix, w_in, hgrn_out_norm, q_norm, k_norm, attn_sinks, w_branch, w_out, w_norm_ffn, w_router_group, b_router_group, w_router_expert, b_router_expert, w_expert_gate, w_expert_up, w_expert_down):
    b, s, d = x_prompt.shape
    bd, sd, _ = x_sample.shape
    depth, _, heads, dk, dv = state_hgrn.shape
    assert depth == 1 and heads == HGRN_HEADS and dk == dv
    hw = heads * dk
    aw = ATTN_HEADS * HEAD_DIM
    kvw = KV_HEADS * HEAD_DIM
    assert w_in.shape[-1] == 4 * hw + aw + 2 * kvw + 2 * d
    assert s % HGRN_CHUNK == 0 and s % CHUNK == 0 and sd == N_META and N_EXPERTS + N_GROUPS <= LANES
    l = 0

    p = jax.nn.softmax(hgrn_lower_bounds.astype(F32), axis=0)
    lb = jnp.cumsum(p, axis=0)[l + 1] - p[0]

    w_in_b = w_in[l].astype(BF16)
    proj = functools.partial(_inproj, w_norm=w_norm_mix[l], w_in_bf16=w_in_b, q_gain=q_norm[l], k_gain=k_norm[l],
                             hw=hw, aw=aw, kvw=kvw)
    x_small = jnp.concatenate([x_sample.reshape(bd * sd, d), meta_tokens.astype(F32)], axis=0)
    qr_s, z_s, hv_s, hg_s, qa_s, k_s, v_s, sga_s, sgb_s = proj(x_small)
    qr_p, z_p, hv_p, hg_p, qa_p, k_p, v_p, sga_p, sgb_p = proj(x_prompt.reshape(b * s, d))
    ns = bd * sd
    k_meta, v_meta = k_s[ns:], v_s[ns:]

    streams = lambda a, n: a.reshape(n, -1, a.shape[-1])
    s0_small = jnp.concatenate([state_hgrn[l].astype(F32), jnp.zeros((1, heads, dk, dv), F32)], axis=0)
    y_small, st_small = _hgrn(streams(qr_s, bd + 1), streams(z_s, bd + 1), streams(hv_s, bd + 1),
                              streams(hg_s, bd + 1), lb, hgrn_out_norm[l], s0_small, sd)
    s0_p = jnp.broadcast_to(st_small[bd:], (b, heads, dk, dv))
    y_p, st_p = _hgrn(streams(qr_p, b), streams(z_p, b), streams(hv_p, b), streams(hg_p, b), lb,
                      hgrn_out_norm[l], s0_p, HGRN_CHUNK)

    table = rel_bias_table.astype(F32)
    att_p = _attn_prompt(streams(qa_p, b), streams(k_p, b), streams(v_p, b), k_meta, v_meta, table, attn_sinks[l])
    kc = cache_swa_k[l].astype(F32).reshape(bd, -1, kvw)
    vc = cache_swa_v[l].astype(F32).reshape(bd, -1, kvw)
    k_new, v_new = k_s[:ns].reshape(bd, sd, kvw), v_s[:ns].reshape(bd, sd, kvw)
    att_s = _attn_sample(qa_s[:ns].reshape(bd, sd, aw), kc, vc, k_new, v_new, k_meta, v_meta, table, attn_sinks[l])

    wb = w_branch[l].astype(BF16)
    wo = w_out[l].astype(BF16)
    w_router = jnp.pad(jnp.concatenate([w_router_expert[l], w_router_group[l]], axis=1).astype(F32),
                       ((0, 0), (0, LANES - N_EXPERTS - N_GROUPS)))
    b_router = jnp.pad(jnp.concatenate([b_router_expert[l], b_router_group[l]]).astype(F32),
                       (0, LANES - N_EXPERTS - N_GROUPS)).reshape(1, LANES)
    merge = functools.partial(_merge, wb=wb, wo=wo, w_norm=w_norm_ffn[l], w_router=w_router, b_router=b_router)
    h_p, xn_p, route_p, cnt_p = merge(x_prompt.reshape(b * s, d), y_p.reshape(b * s, hw), att_p.reshape(b * s, aw),
                                      sga_p, sgb_p, cnt0=jnp.zeros((1, LANES), F32))
    h_s, xn_s, route_s, cnt_s = merge(x_sample.reshape(ns, d), y_small[:bd].reshape(ns, hw), att_s.reshape(ns, aw),
                                      sga_s[:ns], sgb_s[:ns], cnt0=cnt_p)

    out_p, out_s = _moe([(h_p, xn_p, route_p), (h_s, xn_s, route_s)], cnt_s,
                        w_expert_gate[l], w_expert_up[l], w_expert_down[l])

    tail = lambda a: a.reshape(b, s, KV_HEADS, HEAD_DIM)[:, s - WINDOW:][None]
    roll = lambda cache, new: jnp.concatenate([cache, new], axis=1)[:, -WINDOW:].reshape(
        bd, WINDOW, KV_HEADS, HEAD_DIM)[None]
    return (out_p.reshape(b, s, d), out_s.reshape(bd, sd, d),
            tail(k_p), tail(v_p), st_p[None],
            roll(kc, k_new), roll(vc, v_new), st_small[:bd][None])
```

```python
import functools
import math

import numpy as np
import jax
import jax.numpy as jnp
from jax import lax
from jax.experimental import pallas as pl
from jax.experimental.pallas import tpu as pltpu

F32 = jnp.float32
BF16 = jnp.bfloat16
I32 = jnp.int32

CHUNK = 64
N_META = 16
PAST_LEN = 2048
EPS = 1e-6
HGRN_HEADS = 4
ATTN_HEADS = 8
KV_HEADS = 2
HEAD_DIM = 64
GQA_GROUP = ATTN_HEADS // KV_HEADS
WINDOW = 128
WINDOW_CHUNKS = WINDOW // CHUNK
NUM_BUCKETS = 32
MAX_DISTANCE = 128
N_GROUPS = 4
EXPERTS_PER_GROUP = 8
N_EXPERTS = N_GROUPS * EXPERTS_PER_GROUP
TOP_K = 2

LANES = 128
VMEM_LIMIT = 56 * 1024 * 1024

PROJ_TILE = 512
HGRN_CHUNK = 128
ATTN_CHUNKS_PER_STEP = 4
EXPERT_TILE = 256
MOVE_TILE = 256


def _sigmoid(x):
    return 1.0 / (1.0 + jnp.exp(-x))


def _split3(x):
    hi = x.astype(BF16)
    r1 = x - hi.astype(F32)
    mid = r1.astype(BF16)
    lo = (r1 - mid.astype(F32)).astype(BF16)
    return hi, mid, lo


def _dot(a, b):
    return jnp.dot(a, b, preferred_element_type=F32)


def _dot_nt(a, b):
    return lax.dot_general(a, b, (((1,), (1,)), ((), ())), preferred_element_type=F32)


def _dot_tn(a, b):
    return lax.dot_general(a, b, (((0,), (0,)), ((), ())), preferred_element_type=F32)


SUBLANES = 8


def _store_row_tiles(ref, x):
    n, d = x.shape
    sub = d // LANES
    for j in range(sub):
        ref[pl.ds(j, n, stride=sub), :] = x[:, j * LANES:(j + 1) * LANES]


def _load_row_tiles(ref, n, sub):
    return jnp.concatenate([ref[pl.ds(j, n, stride=sub), :] for j in range(sub)], axis=1)


def _params(*sem):
    return pltpu.CompilerParams(dimension_semantics=sem, vmem_limit_bytes=VMEM_LIMIT)


def _inproj_kernel(hw, aw, kvw, d, x_ref, wn_ref, w_ref, qg_ref, kg_ref, bdq_ref, bdk_ref,
                   qr_ref, z_ref, hv_ref, hg_ref, qa_ref, k_ref, v_ref, sga_ref, sgb_ref):
    x = x_ref[...]
    ms = jnp.mean(x * x, axis=-1, keepdims=True)
    xn = (x * lax.rsqrt(ms + EPS) * wn_ref[...]).astype(BF16)

    def seg(a, b):
        return _dot(xn, w_ref[:, a:b])

    def head_rms(a, bd_ref, gain):
        sq = a * a
        hi = sq.astype(BF16)
        lo = (sq - hi.astype(F32)).astype(BF16)
        m = _dot(hi, bd_ref[...]) + _dot(lo, bd_ref[...])
        return a * lax.rsqrt(m + EPS) * gain

    o = 0
    hq = seg(o, o + hw)
    qr_ref[...] = (hq * _sigmoid(hq) * (hw // HGRN_HEADS) ** -0.5).astype(BF16)
    o += hw
    z_ref[...] = seg(o, o + hw)
    o += hw
    hv_ref[...] = seg(o, o + hw).astype(BF16)
    o += hw
    hg_ref[...] = seg(o, o + hw).astype(BF16)
    o += hw
    aq = seg(o, o + aw)
    qa_ref[...] = (head_rms(aq, bdq_ref, qg_ref[...]) * HEAD_DIM ** -0.5).astype(BF16)
    o += aw
    k_ref[...] = head_rms(seg(o, o + kvw), bdk_ref, kg_ref[...])
    o += kvw
    v_ref[...] = seg(o, o + kvw)
    o += kvw
    sga_ref[...] = _sigmoid(seg(o, o + d)).astype(BF16)
    o += d
    sgb_ref[...] = _sigmoid(seg(o, o + d)).astype(BF16)


def _block_diag_mean(width, group):
    i = np.arange(width)
    return jnp.asarray((i[:, None] // group == i[None, :] // group) / group, dtype=BF16)


def _inproj(x, w_norm, w_in_bf16, q_gain, k_gain, hw, aw, kvw):
    t, d = x.shape
    tm = PROJ_TILE if t % PROJ_TILE == 0 else t
    cols = w_in_bf16.shape[1]
    row = lambda w: pl.BlockSpec((tm, w), lambda i: (i, 0))
    const = lambda a, b: pl.BlockSpec((a, b), lambda i: (0, 0))
    outs = [(hw, BF16), (hw, F32), (hw, BF16), (hw, BF16), (aw, BF16), (kvw, F32), (kvw, F32), (d, BF16), (d, BF16)]
    return pl.pallas_call(
        functools.partial(_inproj_kernel, hw, aw, kvw, d),
        grid=(t // tm,),
        in_specs=[row(d), const(1, d), const(d, cols), const(1, aw), const(1, kvw), const(aw, aw), const(kvw, kvw)],
        out_specs=[row(w) for w, _ in outs],
        out_shape=[jax.ShapeDtypeStruct((t, w), dt) for w, dt in outs],
        compiler_params=_params("arbitrary"),
        name="inproj",
    )(x, w_norm.reshape(1, d), w_in_bf16,
      jnp.tile(q_gain, aw // HEAD_DIM).reshape(1, aw), jnp.tile(k_gain, kvw // HEAD_DIM).reshape(1, kvw),
      _block_diag_mean(aw, HEAD_DIM), _block_diag_mean(kvw, HEAD_DIM))


def _hgrn_consts(L):
    t = np.arange(L)
    u = t[None, :]
    blocks = [u <= t[:, None], u > t[:, None]]
    levels = []
    m = L // 2
    while m >= 1:
        levels.append(m)
        m //= 2
    lvl = np.full((L, L), -1, np.int32)
    lvl[t, t] = len(levels)
    isq_cols = []
    for j, m in enumerate(levels):
        bnd = (t // (2 * m)) * (2 * m) + m - 1
        isq = (t % (2 * m)) >= m
        cq = isq[:, None] & (u > bnd[:, None]) & (u <= t[:, None])
        ck = (~isq)[:, None] & (u > t[:, None]) & (u <= bnd[:, None])
        blocks.append(cq | ck)
        same = (t[:, None] // (2 * m)) == (t[None, :] // (2 * m))
        lvl[same & isq[:, None] & (~isq)[None, :]] = j
        isq_cols.append(isq)
    c = np.concatenate(blocks, axis=0).astype(np.float32)
    isq = np.stack(isq_cols, axis=1).astype(np.float32)
    isq = np.pad(isq, ((0, 0), (0, LANES - isq.shape[1])))
    return jnp.asarray(c, dtype=BF16), jnp.asarray(lvl), jnp.asarray(isq), len(levels)


def _hgrn_kernel(L, nlev, heads, dk, qr_ref, z_ref, hv_ref, hg_ref, lb_ref, og_ref, c_ref, lvl_ref, isq_ref,
                 s0_ref, y_ref, sout_ref, st_ref):
    c = pl.program_id(1)

    @pl.when(c == 0)
    def _():
        for h in range(heads):
            st_ref[h] = s0_ref[0, h].T

    z = z_ref[0]
    lb = lb_ref[...]
    e = jnp.exp(-jnp.abs(z))
    r = 1.0 / (1.0 + e)
    pos = z >= 0
    sig = jnp.where(pos, r, e * r)
    sig_neg = jnp.where(pos, e * r, r)
    logf = jnp.log(lb + (1.0 - lb) * sig)
    kin = (1.0 - lb) * sig_neg
    q = qr_ref[0].astype(F32)

    hi, mid, lo = _split3(logf)
    cm = c_ref[...]
    ex = jnp.exp(_dot(cm, hi) + _dot(cm, mid) + _dot(cm, lo))
    e_b = ex[0:L]
    e_rev = ex[L:2 * L]

    q_in = (q * e_b).astype(BF16)
    k_out = (kin * e_rev).astype(BF16)
    q_b = q.astype(BF16)
    k_b = kin.astype(BF16)
    xs = []
    for j in range(nlev):
        isq = isq_ref[:, j:j + 1] > 0.5
        xs.append((jnp.where(isq, q, kin) * ex[(2 + j) * L:(3 + j) * L]).astype(BF16))
    lvl = lvl_ref[...]
    v = hv_ref[0]
    g = hg_ref[0].astype(F32)
    og = og_ref[...]

    for h in range(heads):
        sl = slice(h * dk, (h + 1) * dk)
        a = jnp.where(lvl == nlev, _dot_nt(q_b[:, sl], k_b[:, sl]), 0.0)
        for j in range(nlev):
            xh = xs[j][:, sl]
            a = jnp.where(lvl == j, _dot_nt(xh, xh), a)
        st = st_ref[h]
        vh = v[:, sl]
        o = _dot(a.astype(BF16), vh) + _dot_nt(q_in[:, sl], st.astype(BF16))
        st_ref[h] = st * e_b[L - 1:L, sl] + _dot_tn(vh, k_out[:, sl])
        ms = jnp.mean(o * o, axis=-1, keepdims=True)
        gh = g[:, sl]
        y_ref[0, :, sl] = (o * lax.rsqrt(ms + EPS) * og[:, sl] * (gh * _sigmoid(gh))).astype(BF16)

    @pl.when(c == pl.num_programs(1) - 1)
    def _():
        for h in range(heads):
            sout_ref[0, h] = st_ref[h].T


def _hgrn(qr, z, hv, hg, lb, out_gain, s0, L):
    b, s, w = z.shape
    heads, dk = s0.shape[1], s0.shape[2]
    cm, lvl, isq, nlev = _hgrn_consts(L)
    seq = pl.BlockSpec((1, L, w), lambda i, c: (i, c, 0))
    const = lambda a: pl.BlockSpec(a.shape, lambda i, c: (0,) * a.ndim)
    state = pl.BlockSpec((1, heads, dk, dk), lambda i, c: (i, 0, 0, 0))
    lb2 = lb.reshape(1, w)
    og2 = jnp.tile(out_gain, heads).reshape(1, w)
    return pl.pallas_call(
        functools.partial(_hgrn_kernel, L, nlev, heads, dk),
        grid=(b, s // L),
        in_specs=[seq, seq, seq, seq, const(lb2), const(og2), const(cm), const(lvl), const(isq), state],
        out_specs=[seq, state],
        out_shape=[jax.ShapeDtypeStruct((b, s, w), BF16), jax.ShapeDtypeStruct(s0.shape, F32)],
        scratch_shapes=[pltpu.VMEM((heads, dk, dk), F32)],
        compiler_params=_params("arbitrary", "arbitrary"),
        name=f"hgrn_scan_{L}",
    )(qr, z, hv, hg, lb2, og2, cm, lvl, isq, s0)


def _t5_bucket_np(rel):
    half = NUM_BUCKETS // 2
    max_exact = half // 2
    assert (NUM_BUCKETS, MAX_DISTANCE) == (32, 128)
    n = np.abs(rel).astype(np.int64)
    nn = np.maximum(n, 1)
    k = np.zeros_like(nn)
    for j in range(1, 48):
        k = np.where(64 * (1 << j) <= nn * nn, j, k)
    large = np.minimum(max_exact + k, half - 1)
    return np.where(rel > 0, half, 0) + np.where(n < max_exact, n, large)


HEADS_PER_COL = LANES // HEAD_DIM
COLS_PER_GROUP = GQA_GROUP // HEADS_PER_COL
HEAD_ORDER = tuple(g * GQA_GROUP + col * HEADS_PER_COL + half
                   for g in range(KV_HEADS) for half in range(HEADS_PER_COL) for col in range(COLS_PER_GROUP))
KV_EXPAND = KV_HEADS * HEADS_PER_COL


def _expand_kv(x):
    assert HEADS_PER_COL == 2 and KV_HEADS == 2 and x.shape[1] == LANES
    low = lax.broadcasted_iota(I32, x.shape, 1) < HEAD_DIM
    xr = pltpu.roll(x, HEAD_DIM, axis=1)
    zero = jnp.zeros_like(x)
    blocks = [jnp.where(low, x, zero), jnp.where(low, zero, xr), jnp.where(low, xr, zero), jnp.where(low, zero, x)]
    return jnp.concatenate(blocks, axis=1).astype(BF16)


def _attn_core(q, kx, vx, bias, sink):
    tq = q.shape[0]
    scores = []
    for g in range(KV_HEADS):
        cols = [q[:, (g * COLS_PER_GROUP + c) * LANES:(g * COLS_PER_GROUP + c + 1) * LANES]
                for c in range(COLS_PER_GROUP)]
        qst = jnp.concatenate(cols, axis=0)
        for half in range(HEADS_PER_COL):
            blk = g * HEADS_PER_COL + half
            scores.append(_dot_nt(qst, kx[:, blk * LANES:(blk + 1) * LANES]))
    s = jnp.concatenate(scores, axis=0) + bias
    m =jnp.maximum(jnp.max(s, axis=-1, keepdims=True), sink)
    p = jnp.exp(s - m)
    den = jnp.sum(p, axis=-1, keepdims=True) + jnp.exp(sink - m)
    pn = (p * (1.0 / den)).astype(BF16)
    rows = COLS_PER_GROUP * tq
    outs = []
    for g in range(KV_HEADS):
        o = None
        for half in range(HEADS_PER_COL):
            blk = g * HEADS_PER_COL + half
            part = _dot(pn[blk * rows:(blk + 1) * rows], vx[:, blk * LANES:(blk + 1) * LANES])
            o = part if o is None else o + part
        outs.extend(o[c * tq:(c + 1) * tq] for c in range(COLS_PER_GROUP))
    return outs


def _bias_rows(table, bucket):
    onehot = (jnp.asarray(bucket)[..., None] == jnp.arange(NUM_BUCKETS)).astype(F32)
    cols = jnp.stack([table[:, h] for h in HEAD_ORDER], axis=1)
    bias = jnp.einsum('...qkb,bh->...hqk', onehot, cols, precision=lax.Precision.HIGHEST)
    return bias.reshape(*bucket.shape[:-2], ATTN_HEADS * bucket.shape[-2], bucket.shape[-1])


def _pad_keys(bias):
    tk = bias.shape[-1]
    pad = [(0, 0)] * (bias.ndim - 1) + [(0, -tk % LANES)]
    return jnp.pad(bias, pad, constant_values=-jnp.inf)


def _sink_rows(sinks, tq):
    return jnp.repeat(jnp.stack([sinks[h] for h in HEAD_ORDER]).astype(F32), tq).reshape(ATTN_HEADS * tq, 1)


def _attn_prompt_kernel(cb, q_ref, k_ref, v_ref, km_ref, vm_ref, bias_ref, sink_ref, o_ref, kx_ref, vx_ref):
    step = pl.program_id(1)
    s_len = k_ref.shape[1]
    meta_at = WINDOW + s_len

    @pl.when(step == 0)
    def _():
        piece = min(s_len, 512)
        for src, meta, dst in ((k_ref, km_ref, kx_ref), (v_ref, vm_ref, vx_ref)):
            dst[0:WINDOW] = jnp.zeros((WINDOW, dst.shape[1]), BF16)
            for r in range(0, s_len, piece):
                dst[WINDOW + r:WINDOW + r + piece] = _expand_kv(src[0, r:r + piece, :])
            dst[meta_at:meta_at + N_META] = _expand_kv(meta[...])
            dst[meta_at + N_META:] = jnp.zeros((dst.shape[0] - meta_at - N_META, dst.shape[1]), BF16)

    win = WINDOW + CHUNK
    tail = kx_ref.shape[0] - meta_at
    for j in range(cb):
        c = step * cb + j
        start = pl.multiple_of(c * CHUNK, CHUNK)
        kall = jnp.concatenate([kx_ref[pl.ds(start, win), :], kx_ref[meta_at:meta_at + tail, :]], axis=0)
        vall = jnp.concatenate([vx_ref[pl.ds(start, win), :], vx_ref[meta_at:meta_at + tail, :]], axis=0)
        rows = slice(j * CHUNK, (j + 1) * CHUNK)
        outs = _attn_core(q_ref[0, rows, :], kall, vall, bias_ref[jnp.minimum(c, bias_ref.shape[0] - 1)],
                          sink_ref[...])
        for ci, o in enumerate(outs):
            o_ref[0, rows, ci * LANES:(ci + 1) * LANES] = o.astype(o_ref.dtype)


def _attn_prompt(q, k, v, k_meta, v_meta, table, sinks):
    b, s, aw = q.shape
    kvw = k.shape[-1]
    nc = s // CHUNK
    cb = ATTN_CHUNKS_PER_STEP if nc % ATTN_CHUNKS_PER_STEP == 0 else 1
    assert s % min(s, 512) == 0
    n_bias = 1
    while True:
        qpos = N_META + (n_bias - 1) * CHUNK
        if np.all(_t5_bucket_np(np.arange(N_META) - qpos) == _t5_bucket_np(np.arange(N_META) - qpos - 10 ** 6)):
            break
        n_bias += 1
    n_bias = min(max(n_bias, WINDOW_CHUNKS + 1), nc)
    cs = np.arange(n_bias)[:, None]
    qpos = N_META + cs * CHUNK + np.arange(CHUNK)[None]
    wpos = N_META + (cs - WINDOW_CHUNKS) * CHUNK + np.arange(WINDOW + CHUNK)[None]
    kpos = np.concatenate([wpos, np.broadcast_to(np.arange(N_META), (n_bias, N_META))], axis=1)
    valid = np.concatenate([wpos >= N_META, np.ones((n_bias, N_META), bool)], axis=1)
    bias = _bias_rows(table, _t5_bucket_np(kpos[:, None, :] - qpos[:, :, None]))
    bias = _pad_keys(jnp.where(valid[:, None, :], bias, -jnp.inf))
    sink = _sink_rows(sinks, CHUNK)
    qs = pl.BlockSpec((1, cb * CHUNK, aw), lambda i, c: (i, c, 0))
    kv = pl.BlockSpec((1, s, kvw), lambda i, c: (i, 0, 0))
    meta = pl.BlockSpec((N_META, kvw), lambda i, c: (0, 0))
    xrows = WINDOW + s + bias.shape[-1] - (WINDOW + CHUNK)
    return pl.pallas_call(
        functools.partial(_attn_prompt_kernel, cb),
        grid=(b, nc // cb),
        in_specs=[qs, kv, kv, meta, meta, pl.BlockSpec(bias.shape, lambda i, c: (0, 0, 0)),
                  pl.BlockSpec(sink.shape, lambda i, c: (0, 0))],
        out_specs=qs,
        out_shape=jax.ShapeDtypeStruct((b, s, aw), BF16),
        scratch_shapes=[pltpu.VMEM((xrows, KV_EXPAND * LANES), BF16), pltpu.VMEM((xrows, KV_EXPAND * LANES), BF16)],
        compiler_params=_params("arbitrary", "arbitrary"),
        name="attn_prompt",
    )(q, k, v, k_meta, v_meta, bias, sink)


def _attn_sample_kernel(q_ref, kc_ref, vc_ref, kn_ref, vn_ref, km_ref, vm_ref, bias_ref, sink_ref, o_ref):
    tk = kc_ref.shape[1] + kn_ref.shape[1] + km_ref.shape[0]
    zeros = jnp.zeros((bias_ref.shape[1] - tk, km_ref.shape[1]), F32)
    kall = _expand_kv(jnp.concatenate([kc_ref[0], kn_ref[0], km_ref[...], zeros], axis=0))
    vall = _expand_kv(jnp.concatenate([vc_ref[0], vn_ref[0], vm_ref[...], zeros], axis=0))
    outs = _attn_core(q_ref[0], kall, vall, bias_ref[...], sink_ref[...])
    for ci, o in enumerate(outs):
        o_ref[0, :, ci * LANES:(ci + 1) * LANES] = o.astype(o_ref.dtype)


def _attn_sample(q, k_cache, v_cache, k_new, v_new, k_meta, v_meta, table, sinks):
    bd, sd, aw = q.shape
    kvw = k_new.shape[-1]
    win = k_cache.shape[1]
    qpos = N_META + PAST_LEN + np.arange(sd)
    kpos = np.concatenate([N_META + PAST_LEN - win + np.arange(win), qpos, np.arange(N_META)])
    bias = _pad_keys(_bias_rows(table, _t5_bucket_np(kpos[None, :] - qpos[:, None])))
    sink = _sink_rows(sinks, sd)
    per = lambda n, w: pl.BlockSpec((1, n, w), lambda i: (i, 0, 0))
    meta = pl.BlockSpec((N_META, kvw), lambda i: (0, 0))
    return pl.pallas_call(
        _attn_sample_kernel,
        grid=(bd,),
        in_specs=[per(sd, aw), per(win, kvw), per(win, kvw), per(sd, kvw), per(sd, kvw), meta, meta,
                  pl.BlockSpec(bias.shape, lambda i: (0, 0)), pl.BlockSpec(sink.shape, lambda i: (0, 0))],
        out_specs=per(sd, aw),
        out_shape=jax.ShapeDtypeStruct((bd, sd, aw), BF16),
        compiler_params=_params("arbitrary"),
        name="attn_sample",
    )(q, k_cache, v_cache, k_new, v_new, k_meta, v_meta, bias, sink)


ROUTE_E1, ROUTE_E2, ROUTE_W1, ROUTE_W2, ROUTE_R1, ROUTE_R2 = range(6)


def _merge_kernel(hw, x_ref, yr_ref, at_ref, sga_ref, sgb_ref, wb_ref, wo_ref, wn_ref, wr_ref, br_ref, cnt0_ref,
                  h_ref, xn_ref, route_ref, cnt_ref, carry_ref):
    i = pl.program_id(0)

    @pl.when(i == 0)
    def _():
        carry_ref[...] = cnt0_ref[...]

    br = _dot(yr_ref[...], wb_ref[0:hw, :])
    ba = _dot(at_ref[...], wb_ref[hw:, :])
    merged = sga_ref[...].astype(F32) * br + sgb_ref[...].astype(F32) * ba
    h = x_ref[...] + _dot(merged.astype(BF16), wo_ref[...])
    h_ref[...] = h
    ms = jnp.mean(h * h, axis=-1, keepdims=True)
    xn = h * lax.rsqrt(ms + EPS) * wn_ref[...]
    _store_row_tiles(xn_ref, xn)

    x_hi = xn.astype(BF16)
    x_lo = (xn - x_hi.astype(F32)).astype(BF16)
    w = wr_ref[...]
    w_hi = w.astype(BF16)
    w_lo = (w - w_hi.astype(F32)).astype(BF16)
    logits = _dot(x_hi, w_hi) + _dot(x_hi, w_lo) + _dot(x_lo, w_hi) + br_ref[...]

    tm = logits.shape[0]
    lane_i = lax.broadcasted_iota(I32, (tm, LANES), 1)
    lane = lane_i.astype(F32)
    group_of_lane = (lane_i >> int(math.log2(EXPERTS_PER_GROUP))).astype(F32)
    ninf = -jnp.inf
    first = lambda hit, idx: jnp.min(jnp.where(hit, idx, float(LANES)), axis=-1, keepdims=True)
    gmask = (lane_i >= N_EXPERTS) & (lane_i < N_EXPERTS + N_GROUPS)
    gl = jnp.where(gmask, logits, ninf)
    gmax = jnp.max(gl, axis=-1, keepdims=True)
    gidx = first(gl == gmax, lane - N_EXPERTS)
    gval = 1.0 / jnp.sum(jnp.exp(gl - gmax), axis=-1, keepdims=True)
    emask = (lane_i < N_EXPERTS) & (group_of_lane == gidx)
    el = jnp.where(emask, logits, ninf)
    m1 = jnp.max(el, axis=-1, keepdims=True)
    i1 = first(el == m1, lane)
    el2 = jnp.where(lane == i1, ninf, el)
    m2 = jnp.max(el2, axis=-1, keepdims=True)
    i2 = first(el2 == m2, lane)
    e21 = jnp.exp(m2 - m1)
    w1 = gval / (1.0 + e21)
    w2 = gval * e21 / (1.0 + e21)

    sel1 = lane == i1
    sel2 = lane == i2
    oh = (sel1 | sel2).astype(BF16)
    ri = lax.broadcasted_iota(I32, (tm, tm), 0)
    ci = lax.broadcasted_iota(I32, (tm, tm), 1)
    before = _dot((ci < ri).astype(BF16), oh) + carry_ref[...]
    r1 = jnp.sum(jnp.where(sel1, before, 0.0), axis=-1, keepdims=True)
    r2 = jnp.sum(jnp.where(sel2, before, 0.0), axis=-1, keepdims=True)
    carry_ref[...] = carry_ref[...] + jnp.sum(oh.astype(F32), axis=0, keepdims=True)
    cnt_ref[...] = carry_ref[...]

    rec = jnp.zeros((tm, LANES), F32)
    for slot, val in ((ROUTE_E1, i1), (ROUTE_E2, i2), (ROUTE_W1, w1), (ROUTE_W2, w2), (ROUTE_R1, r1), (ROUTE_R2, r2)):
        rec = jnp.where(lane_i == slot, val, rec)
    route_ref[...] = rec


def _merge(x, y_rec, att, sga, sgb, wb, wo, w_norm, w_router, b_router, cnt0):
    t, d = x.shape
    hw = y_rec.shape[1]
    tm = PROJ_TILE if t % PROJ_TILE == 0 else t
    row = lambda w: pl.BlockSpec((tm, w), lambda i: (i, 0))
    const = lambda a, b: pl.BlockSpec((a, b), lambda i: (0, 0))
    return pl.pallas_call(
        functools.partial(_merge_kernel, hw),
        grid=(t // tm,),
        in_specs=[row(d), row(hw), row(att.shape[1]), row(d), row(d), const(*wb.shape), const(d, d), const(1, d),
                  const(d, LANES), const(1, LANES), const(1, LANES)],
        out_specs=[row(d), pl.BlockSpec((tm * d // LANES, LANES), lambda i: (i, 0)), row(LANES), const(1, LANES)],
        out_shape=[jax.ShapeDtypeStruct((t, d), F32), jax.ShapeDtypeStruct((t * d // LANES, LANES), F32),
                   jax.ShapeDtypeStruct((t, LANES), F32), jax.ShapeDtypeStruct((1, LANES), F32)],
        scratch_shapes=[pltpu.VMEM((1, LANES), F32)],
        compiler_params=_params("arbitrary"),
        name="merge_route",
    )(x, y_rec, att, sga, sgb, wb, wo, w_norm.reshape(1, d), w_router, b_router, cnt0)


def _row_copy(src, dst, sem):
    return pltpu.make_async_copy(src, dst, sem)


ROW_UNROLL = 8


def _tile_rows(ref, row, sub):
    return ref.at[pl.ds(pl.multiple_of(row * sub, sub), sub)]


def _scatter_kernel(n_tok, sub, first, dest_ref, pstart_ref, pend_ref, x_ref, *rest):
    xs_ref, zero_ref, sem, zsem = rest[-4:]

    if first:
        @pl.when(pl.program_id(0) == 0)
        def _():
            zero_ref[...] = jnp.zeros_like(zero_ref)
            n_rows = xs_ref.shape[0] // sub
            used = pend_ref[N_EXPERTS - 1]
            blocks = [(pend_ref[e] > pstart_ref[e], pend_ref[e] - EXPERT_TILE) for e in range(N_EXPERTS)]
            blocks += [(n_rows - (j + 1) * EXPERT_TILE >= used, n_rows - (j + 1) * EXPERT_TILE)
                       for j in range(N_EXPERTS)]
            for wait in (False, True):
                for cond, row in blocks:
                    @pl.when(cond)
                    def _():
                        at = row * sub if isinstance(row, int) else pl.multiple_of(row * sub, EXPERT_TILE * sub)
                        cp = _row_copy(zero_ref, xs_ref.at[pl.ds(at, EXPERT_TILE * sub)], zsem)
                        cp.wait() if wait else cp.start()

    def issue(grp, _):
        for u in range(ROW_UNROLL):
            r = grp * ROW_UNROLL + u
            for k in range(TOP_K):
                _row_copy(_tile_rows(x_ref, r, sub), _tile_rows(xs_ref, dest_ref[0, 0, TOP_K * r + k], sub),
                          sem).start(priority=(u * TOP_K + k) % 2)
        return 0

    lax.fori_loop(0, n_tok // ROW_UNROLL, issue, 0)
    for k in range(TOP_K):
        _row_copy(x_ref, xs_ref.at[pl.ds(0, n_tok * sub)], sem).wait()


def _scatter_rows(x, dest, pstarts, pends, xs, rows):
    n, _, per = dest.shape
    tm = per // TOP_K
    sub = x.shape[0] // (n * tm)
    first = xs is None
    smem = pl.BlockSpec(memory_space=pltpu.SMEM)
    return pl.pallas_call(
        functools.partial(_scatter_kernel, tm, sub, first),
        grid=(n,),
        in_specs=[pl.BlockSpec((1, 1, per), lambda i: (i, 0, 0), memory_space=pltpu.SMEM), smem, smem,
                  pl.BlockSpec((tm * sub, LANES), lambda i: (i, 0))]
        + ([] if first else [pl.BlockSpec(memory_space=pl.ANY)]),
        out_specs=pl.BlockSpec(memory_space=pl.ANY),
        out_shape=jax.ShapeDtypeStruct((rows * sub, LANES), F32),
        scratch_shapes=[pltpu.VMEM((EXPERT_TILE * sub, LANES), F32), pltpu.SemaphoreType.DMA(()),
                        pltpu.SemaphoreType.DMA(())],
        input_output_aliases={} if first else {4: 0},
        compiler_params=_params("arbitrary"),
        name="moe_scatter",
    )(dest, pstarts, pends, x, *([] if first else [xs]))


def _expert_kernel(tm, sub, be_ref, nb_ref, x_ref, wg_ref, wu_ref, wd_ref, y_ref, wgb_ref, wub_ref, wdb_ref):
    i = pl.program_id(0)
    used = i < nb_ref[0]

    @pl.when(used & ((i == 0) | (be_ref[i] != be_ref[jnp.maximum(i - 1, 0)])))
    def _():
        wgb_ref[...] = wg_ref[0].astype(BF16)
        wub_ref[...] = wu_ref[0].astype(BF16)
        wdb_ref[...] = wd_ref[0].astype(BF16)

    @pl.when(used)
    def _():
        x = _load_row_tiles(x_ref, tm, sub).astype(BF16)
        g = _dot(x, wgb_ref[...])
        u = _dot(x, wub_ref[...])
        hmid = (g * _sigmoid(g) * u).astype(BF16)
        _store_row_tiles(y_ref, _dot(hmid, wdb_ref[...]))

    @pl.when(jnp.logical_not(used))
    def _():
        y_ref[...] = jnp.zeros_like(y_ref)


def _experts(xs, block_e, n_used, wg, wu, wd):
    _, d, ff = wg.shape
    sub = d // LANES
    tm = EXPERT_TILE
    blk = pl.BlockSpec((tm * sub, LANES), lambda i, be, nb: (i, 0))
    grid_spec = pltpu.PrefetchScalarGridSpec(
        num_scalar_prefetch=2,
        grid=(xs.shape[0] // (tm * sub),),
        in_specs=[blk,
                  pl.BlockSpec((1, d, ff), lambda i, be, nb: (be[i], 0, 0)),
                  pl.BlockSpec((1, d, ff), lambda i, be, nb: (be[i], 0, 0)),
                  pl.BlockSpec((1, ff, d), lambda i, be, nb: (be[i], 0, 0))],
        out_specs=blk,
        scratch_shapes=[pltpu.VMEM((d, ff), BF16), pltpu.VMEM((d, ff), BF16), pltpu.VMEM((ff, d), BF16)],
    )
    return pl.pallas_call(
        functools.partial(_expert_kernel, tm, sub),
        grid_spec=grid_spec,
        out_shape=jax.ShapeDtypeStruct(xs.shape, F32),
        compiler_params=_params("arbitrary"),
        name="moe_experts",
    )(block_e, n_used, xs, wg, wu, wd)


def _combine_kernel(n_tok, sub, dcur_ref, dnext_ref, h_ref, route_ref, ys_ref, o_ref, buf, sem):
    i = pl.program_id(0)
    n = pl.num_programs(0)
    slot = i % 2

    def start(dest_ref, s):
        def issue(grp, _):
            for u in range(ROW_UNROLL):
                r = grp * ROW_UNROLL + u
                for k in range(TOP_K):
                    _row_copy(_tile_rows(ys_ref, dest_ref[0, 0, TOP_K * r + k], sub), _tile_rows(buf.at[s, k], r, sub),
                              sem.at[s]).start(priority=(u * TOP_K + k) % 2)
            return 0
        lax.fori_loop(0, n_tok // ROW_UNROLL, issue, 0)

    @pl.when(i == 0)
    def _():
        start(dcur_ref, 0)

    @pl.when(i + 1 < n)
    def _():
        start(dnext_ref, 1 - slot)

    for k in range(TOP_K):
        _row_copy(ys_ref.at[pl.ds(0, n_tok * sub)], buf.at[slot, k], sem.at[slot]).wait()
    route = route_ref[...]
    w1 = route[:, ROUTE_W1:ROUTE_W1 + 1]
    w2 = route[:, ROUTE_W2:ROUTE_W2 + 1]
    for j in range(sub):
        cols = slice(j * LANES, (j + 1) * LANES)
        part = lambda k: buf[slot, k, pl.ds(j, n_tok, stride=sub), :]
        o_ref[:, cols] = h_ref[:, cols] + (part(0) * w1 + part(1) * w2)


def _combine(h, route, dest, ys):
    t, d = h.shape
    n, _, per = dest.shape
    tm = t // n
    sub = d // LANES
    dspec = lambda f: pl.BlockSpec((1, 1, per), f, memory_space=pltpu.SMEM)
    return pl.pallas_call(
        functools.partial(_combine_kernel, tm, sub),
        grid=(n,),
        in_specs=[dspec(lambda i: (i, 0, 0)), dspec(lambda i: (jnp.minimum(i + 1, n - 1), 0, 0)),
                  pl.BlockSpec((tm, d), lambda i: (i, 0)), pl.BlockSpec((tm, LANES), lambda i: (i, 0)),
                  pl.BlockSpec(memory_space=pl.ANY)],
        out_specs=pl.BlockSpec((tm, d), lambda i: (i, 0)),
        out_shape=jax.ShapeDtypeStruct((t, d), F32),
        scratch_shapes=[pltpu.VMEM((2, TOP_K, tm * sub, LANES), F32), pltpu.SemaphoreType.DMA((2,))],
        compiler_params=_params("arbitrary"),
        name="moe_combine",
    )(dest, dest, h, route, ys)


def _moe(parts, counts, wg, wu, wd):
    tm = EXPERT_TILE
    n_assign = sum(part[0].shape[0] for part in parts) * TOP_K
    n_blocks = -(-(n_assign + N_EXPERTS * (tm - 1)) // tm)
    counts = counts[0, :N_EXPERTS].astype(I32)
    pcounts = (counts + tm - 1) // tm * tm
    pends = jnp.cumsum(pcounts)
    pstarts = pends - pcounts
    block_start = jnp.arange(n_blocks, dtype=I32) * tm
    block_e = jnp.minimum(jnp.sum((pends[None, :] <= block_start[:, None]).astype(I32), axis=1), N_EXPERTS - 1)
    n_used = pends[-1:] // tm
    xs = None
    dests = []
    for h, xn, route in parts:
        t = h.shape[0]
        mt = MOVE_TILE if t % MOVE_TILE == 0 else t
        assert mt % ROW_UNROLL == 0
        e = route[:, ROUTE_E1:ROUTE_E2 + 1].astype(I32)
        rank = route[:, ROUTE_R1:ROUTE_R2 + 1].astype(I32)
        seg = jnp.sum(jnp.where(e[..., None] == jnp.arange(N_EXPERTS, dtype=I32), pstarts, 0), axis=-1)
        dests.append((seg + rank).reshape(t // mt, 1, mt * TOP_K))
        xs = _scatter_rows(xn, dests[-1], pstarts, pends, xs, n_blocks * tm)
    ys = _experts(xs, block_e, n_used, wg, wu, wd)
    return [_combine(h, route, dest, ys) for (h, _, route), dest in zip(parts, dests)]


def kernel(x_prompt, x_sample, cache_swa_k, cache_swa_v, state_hgrn, meta_tokens, rel_bias_table, hgrn_lower_bounds, w_norm_mix, w_in, hgrn_out_norm, q_norm, k_norm, attn_sinks, w_branch, w_out, w_norm_ffn, w_router_group, b_router_group, w_router_expert, b_router_expert, w_expert_gate, w_expert_up, w_expert_down):
    b, s, d = x_prompt.shape
    bd, sd, _ = x_sample.shape
    depth, _, heads, dk, dv = state_hgrn.shape
    assert depth == 1 and heads == HGRN_HEADS and dk == dv
    hw = heads * dk
    aw = ATTN_HEADS * HEAD_DIM
    kvw = KV_HEADS * HEAD_DIM
    assert w_in.shape[-1] == 4 * hw + aw + 2 * kvw + 2 * d
    assert s % HGRN_CHUNK == 0 and s % CHUNK == 0 and sd == N_META and N_EXPERTS + N_GROUPS <= LANES
    l = 0

    p = jax.nn.softmax(hgrn_lower_bounds.astype(F32), axis=0)
    lb = jnp.cumsum(p, axis=0)[l + 1] - p[0]

    w_in_b = w_in[l].astype(BF16)
    proj = functools.partial(_inproj, w_norm=w_norm_mix[l], w_in_bf16=w_in_b, q_gain=q_norm[l], k_gain=k_norm[l],
                             hw=hw, aw=aw, kvw=kvw)
    x_small = jnp.concatenate([x_sample.reshape(bd * sd, d), meta_tokens.astype(F32)], axis=0)
    qr_s, z_s, hv_s, hg_s, qa_s, k_s, v_s, sga_s, sgb_s = proj(x_small)
    qr_p, z_p, hv_p, hg_p, qa_p, k_p, v_p, sga_p, sgb_p = proj(x_prompt.reshape(b * s, d))
    ns = bd * sd
    k_meta, v_meta = k_s[ns:], v_s[ns:]

    streams = lambda a, n: a.reshape(n, -1, a.shape[-1])
    s0_small = jnp.concatenate([state_hgrn[l].astype(F32), jnp.zeros((1, heads, dk, dv), F32)], axis=0)
    y_small, st_small = _hgrn(streams(qr_s, bd + 1), streams(z_s, bd + 1), streams(hv_s, bd + 1),
                              streams(hg_s, bd + 1), lb, hgrn_out_norm[l], s0_small, sd)
    s0_p = jnp.broadcast_to(st_small[bd:], (b, heads, dk, dv))
    y_p, st_p = _hgrn(streams(qr_p, b), streams(z_p, b), streams(hv_p, b), streams(hg_p, b), lb,
                      hgrn_out_norm[l], s0_p, HGRN_CHUNK)

    table = rel_bias_table.astype(F32)
    att_p = _attn_prompt(streams(qa_p, b), streams(k_p, b), streams(v_p, b), k_meta, v_meta, table, attn_sinks[l])
    kc = cache_swa_k[l].astype(F32).reshape(bd, -1, kvw)
    vc = cache_swa_v[l].astype(F32).reshape(bd, -1, kvw)
    k_new, v_new = k_s[:ns].reshape(bd, sd, kvw), v_s[:ns].reshape(bd, sd, kvw)
    att_s = _attn_sample(qa_s[:ns].reshape(bd, sd, aw), kc, vc, k_new, v_new, k_meta, v_meta, table, attn_sinks[l])

    wb = w_branch[l].astype(BF16)
    wo = w_out[l].astype(BF16)
    w_router = jnp.pad(jnp.concatenate([w_router_expert[l], w_router_group[l]], axis=1).astype(F32),
                       ((0, 0), (0, LANES - N_EXPERTS - N_GROUPS)))
    b_router = jnp.pad(jnp.concatenate([b_router_expert[l], b_router_group[l]]).astype(F32),
                       (0, LANES - N_EXPERTS - N_GROUPS)).reshape(1, LANES)
    merge = functools.partial(_merge, wb=wb, wo=wo, w_norm=w_norm_ffn[l], w_router=w_router, b_router=b_router)
    h_p, xn_p, route_p, cnt_p = merge(x_prompt.reshape(b * s, d), y_p.reshape(b * s, hw), att_p.reshape(b * s, aw),
                                      sga_p, sgb_p, cnt0=jnp.zeros((1, LANES), F32))
    h_s, xn_s, route_s, cnt_s = merge(x_sample.reshape(ns, d), y_small[:bd].reshape(ns, hw), att_s.reshape(ns, aw),
                                      sga_s[:ns], sgb_s[:ns], cnt0=cnt_p)

    out_p, out_s = _moe([(h_p, xn_p, route_p), (h_s, xn_s, route_s)], cnt_s,
                        w_expert_gate[l], w_expert_up[l], w_expert_down[l])

    tail = lambda a: a.reshape(b, s, KV_HEADS, HEAD_DIM)[:, s - WINDOW:][None]
    roll = lambda cache, new: jnp.concatenate([cache, new], axis=1)[:, -WINDOW:].reshape(
        bd, WINDOW, KV_HEADS, HEAD_DIM)[None]
    return (out_p.reshape(b, s, d), out_s.reshape(bd, sd, d),
            tail(k_p), tail(v_p), st_p[None],
            roll(kc, k_new), roll(vc, v_new), st_small[:bd][None])
```

```python
import functools
import math

import numpy as np
import jax
import jax.numpy as jnp
from jax import lax
from jax.experimental import pallas as pl
from jax.experimental.pallas import tpu as pltpu

F32 = jnp.float32
BF16 = jnp.bfloat16
I32 = jnp.int32

CHUNK = 64
N_META = 16
PAST_LEN = 2048
EPS = 1e-6
HGRN_HEADS = 4
ATTN_HEADS = 8
KV_HEADS = 2
HEAD_DIM = 64
GQA_GROUP = ATTN_HEADS // KV_HEADS
WINDOW = 128
WINDOW_CHUNKS = WINDOW // CHUNK
NUM_BUCKETS = 32
MAX_DISTANCE = 128
N_GROUPS = 4
EXPERTS_PER_GROUP = 8
N_EXPERTS = N_GROUPS * EXPERTS_PER_GROUP
TOP_K = 2

LANES = 128
VMEM_LIMIT = 56 * 1024 * 1024

PROJ_TILE = 512
HGRN_CHUNK = 128
ATTN_CHUNKS_PER_STEP = 4
EXPERT_TILE = 256
MOVE_TILE = 256


def _sigmoid(x):
    return 1.0 / (1.0 + jnp.exp(-x))


def _split3(x):
    hi = x.astype(BF16)
    r1 = x - hi.astype(F32)
    mid = r1.astype(BF16)
    lo = (r1 - mid.astype(F32)).astype(BF16)
    return hi, mid, lo


def _dot(a, b):
    return jnp.dot(a, b, preferred_element_type=F32)


def _dot_nt(a, b):
    return lax.dot_general(a, b, (((1,), (1,)), ((), ())), preferred_element_type=F32)


def _dot_tn(a, b):
    return lax.dot_general(a, b, (((0,), (0,)), ((), ())), preferred_element_type=F32)


SUBLANES = 8


def _store_row_tiles(ref, x):
    n, d = x.shape
    sub = d // LANES
    for j in range(sub):
        ref[pl.ds(j, n, stride=sub), :] = x[:, j * LANES:(j + 1) * LANES]


def _load_row_tiles(ref, n, sub):
    return jnp.concatenate([ref[pl.ds(j, n, stride=sub), :] for j in range(sub)], axis=1)


def _params(*sem):
    return pltpu.CompilerParams(dimension_semantics=sem, vmem_limit_bytes=VMEM_LIMIT)


def _inproj_kernel(hw, aw, kvw, d, x_ref, wn_ref, w_ref, qg_ref, kg_ref, bdq_ref, bdk_ref,
                   qr_ref, z_ref, hv_ref, hg_ref, qa_ref, k_ref, v_ref, sga_ref, sgb_ref):
    x = x_ref[...]
    ms = jnp.mean(x * x, axis=-1, keepdims=True)
    xn = (x * lax.rsqrt(ms + EPS) * wn_ref[...]).astype(BF16)

    def seg(a, b):
        return _dot(xn, w_ref[:, a:b])

    def head_rms(a, bd_ref, gain):
        sq = a * a
        hi = sq.astype(BF16)
        lo = (sq - hi.astype(F32)).astype(BF16)
        m = _dot(hi, bd_ref[...]) + _dot(lo, bd_ref[...])
        return a * lax.rsqrt(m + EPS) * gain

    o = 0
    hq = seg(o, o + hw)
    qr_ref[...] = (hq * _sigmoid(hq) * (hw // HGRN_HEADS) ** -0.5).astype(BF16)
    o += hw
    z_ref[...] = seg(o, o + hw)
    o += hw
    hv_ref[...] = seg(o, o + hw).astype(BF16)
    o += hw
    hg_ref[...] = seg(o, o + hw).astype(BF16)
    o += hw
    aq = seg(o, o + aw)
    qa_ref[...] = (head_rms(aq, bdq_ref, qg_ref[...]) * HEAD_DIM ** -0.5).astype(BF16)
    o += aw
    k_ref[...] = head_rms(seg(o, o + kvw), bdk_ref, kg_ref[...])
    o += kvw
    v_ref[...] = seg(o, o + kvw)
    o += kvw
    sga_ref[...] = _sigmoid(seg(o, o + d)).astype(BF16)
    o += d
    sgb_ref[...] = _sigmoid(seg(o, o + d)).astype(BF16)


def _block_diag_mean(width, group):
    i = np.arange(width)
    return jnp.asarray((i[:, None] // group == i[None, :] // group) / group, dtype=BF16)


def _inproj(x, w_norm, w_in_bf16, q_gain, k_gain, hw, aw, kvw):
    t, d = x.shape
    tm = PROJ_TILE if t % PROJ_TILE == 0 else t
    cols = w_in_bf16.shape[1]
    row = lambda w: pl.BlockSpec((tm, w), lambda i: (i, 0))
    const = lambda a, b: pl.BlockSpec((a, b), lambda i: (0, 0))
    outs = [(hw, BF16), (hw, F32), (hw, BF16), (hw, BF16), (aw, BF16), (kvw, F32), (kvw, F32), (d, BF16), (d, BF16)]
    return pl.pallas_call(
        functools.partial(_inproj_kernel, hw, aw, kvw, d),
        grid=(t // tm,),
        in_specs=[row(d), const(1, d), const(d, cols), const(1, aw), const(1, kvw), const(aw, aw), const(kvw, kvw)],
        out_specs=[row(w) for w, _ in outs],
        out_shape=[jax.ShapeDtypeStruct((t, w), dt) for w, dt in outs],
        compiler_params=_params("arbitrary"),
        name="inproj",
    )(x, w_norm.reshape(1, d), w_in_bf16,
      jnp.tile(q_gain, aw // HEAD_DIM).reshape(1, aw), jnp.tile(k_gain, kvw // HEAD_DIM).reshape(1, kvw),
      _block_diag_mean(aw, HEAD_DIM), _block_diag_mean(kvw, HEAD_DIM))


def _hgrn_consts(L):
    t = np.arange(L)
    u = t[None, :]
    blocks = [u <= t[:, None], u > t[:, None]]
    levels = []
    m = L // 2
    while m >= 1:
        levels.append(m)
        m //= 2
    lvl = np.full((L, L), -1, np.int32)
    lvl[t, t] = len(levels)
    isq_cols = []
    for j, m in enumerate(levels):
        bnd = (t // (2 * m)) * (2 * m) + m - 1
        isq = (t % (2 * m)) >= m
        cq = isq[:, None] & (u > bnd[:, None]) & (u <= t[:, None])
        ck = (~isq)[:, None] & (u > t[:, None]) & (u <= bnd[:, None])
        blocks.append(cq | ck)
        same = (t[:, None] // (2 * m)) == (t[None, :] // (2 * m))
        lvl[same & isq[:, None] & (~isq)[None, :]] = j
        isq_cols.append(isq)
    c = np.concatenate(blocks, axis=0).astype(np.float32)
    isq = np.stack(isq_cols, axis=1).astype(np.float32)
    isq = np.pad(isq, ((0, 0), (0, LANES - isq.shape[1])))
    c2 = np.tile(c, (1, 2))
    return jnp.asarray(c2, dtype=BF16), jnp.asarray(np.tile(lvl, (1, 2))), jnp.asarray(isq), len(levels)


def _hgrn_kernel(L, nlev, heads, dk, qr_ref, z_ref, hv_ref, hg_ref, lb_ref, og_ref, c_ref, lvl_ref, isq_ref,
                 s0_ref, y_ref, sout_ref, st_ref):
    c = pl.program_id(1)

    @pl.when(c == 0)
    def _():
        for h in range(heads):
            st_ref[h] = s0_ref[0, h].T

    z = z_ref[0]
    lb = lb_ref[...]
    e = jnp.exp(-jnp.abs(z))
    r = 1.0 / (1.0 + e)
    pos = z >= 0
    sig = jnp.where(pos, r, e * r)
    sig_neg = jnp.where(pos, e * r, r)
    logf = jnp.log(lb + (1.0 - lb) * sig)
    kin = (1.0 - lb) * sig_neg
    q = qr_ref[0].astype(F32)

    hi, mid, _ = _split3(logf)
    ex = jnp.exp(_dot(c_ref[...], jnp.concatenate([hi, mid], axis=0)))
    e_b = ex[0:L]
    e_rev = ex[L:2 * L]

    q_in = (q * e_b).astype(BF16)
    k_out = (kin * e_rev).astype(BF16)
    q_b = q.astype(BF16)
    k_b = kin.astype(BF16)
    xs = []
    for j in range(nlev):
        isq = isq_ref[:, j:j + 1] > 0.5
        xs.append((jnp.where(isq, q, kin) * ex[(2 + j) * L:(3 + j) * L]).astype(BF16))
    lvl = lvl_ref[...]
    v = hv_ref[0]
    g = hg_ref[0].astype(F32)
    og = og_ref[...]

    def block_diag(x):
        zero = jnp.zeros((x.shape[0], dk), x.dtype)
        return jnp.concatenate([jnp.concatenate([x[:, :dk], zero], axis=1),
                                jnp.concatenate([zero, x[:, dk:]], axis=1)], axis=0)

    for pair in range(heads // 2):
        sl = slice(2 * pair * dk, 2 * (pair + 1) * dk)
        a = jnp.where(lvl == nlev, _dot_nt(q_b[:, sl], block_diag(k_b[:, sl])), 0.0)
        for j in range(nlev):
            xp = xs[j][:, sl]
            a = jnp.where(lvl == j, _dot_nt(xp, block_diag(xp)), a)
        st = jnp.concatenate([st_ref[2 * pair], st_ref[2 * pair + 1]], axis=1)
        vp = v[:, sl]
        o = _dot(a.astype(BF16), block_diag(vp)) + _dot_nt(q_in[:, sl], block_diag(st.astype(BF16)))
        for half in range(2):
            h = 2 * pair + half
            hs = slice(h * dk, (h + 1) * dk)
            st_ref[h] = st_ref[h] * e_b[L - 1:L, hs] + _dot_tn(v[:, hs], k_out[:, hs])
            oh = o[:, half * dk:(half + 1) * dk]
            ms = jnp.mean(oh * oh, axis=-1, keepdims=True)
            gh = g[:, hs]
            y_ref[0, :, hs] = (oh * lax.rsqrt(ms + EPS) * og[:, hs] * (gh * _sigmoid(gh))).astype(BF16)

    @pl.when(c == pl.num_programs(1) - 1)
    def _():
        for h in range(heads):
            sout_ref[0, h] = st_ref[h].T


def _hgrn(qr, z, hv, hg, lb, out_gain, s0, L):
    b, s, w = z.shape
    heads, dk = s0.shape[1], s0.shape[2]
    cm, lvl, isq, nlev = _hgrn_consts(L)
    seq = pl.BlockSpec((1, L, w), lambda i, c: (i, c, 0))
    const = lambda a: pl.BlockSpec(a.shape, lambda i, c: (0,) * a.ndim)
    state = pl.BlockSpec((1, heads, dk, dk), lambda i, c: (i, 0, 0, 0))
    lb2 = lb.reshape(1, w)
    og2 = jnp.tile(out_gain, heads).reshape(1, w)
    return pl.pallas_call(
        functools.partial(_hgrn_kernel, L, nlev, heads, dk),
        grid=(b, s // L),
        in_specs=[seq, seq, seq, seq, const(lb2), const(og2), const(cm), const(lvl), const(isq), state],
        out_specs=[seq, state],
        out_shape=[jax.ShapeDtypeStruct((b, s, w), BF16), jax.ShapeDtypeStruct(s0.shape, F32)],
        scratch_shapes=[pltpu.VMEM((heads, dk, dk), F32)],
        compiler_params=_params("arbitrary", "arbitrary"),
        name=f"hgrn_scan_{L}",
    )(qr, z, hv, hg, lb2, og2, cm, lvl, isq, s0)


def _t5_bucket_np(rel):
    half = NUM_BUCKETS // 2
    max_exact = half // 2
    assert (NUM_BUCKETS, MAX_DISTANCE) == (32, 128)
    n = np.abs(rel).astype(np.int64)
    nn = np.maximum(n, 1)
    k = np.zeros_like(nn)
    for j in range(1, 48):
        k = np.where(64 * (1 << j) <= nn * nn, j, k)
    large = np.minimum(max_exact + k, half - 1)
    return np.where(rel > 0, half, 0) + np.where(n < max_exact, n, large)


HEADS_PER_COL = LANES // HEAD_DIM
COLS_PER_GROUP = GQA_GROUP // HEADS_PER_COL
HEAD_ORDER = tuple(g * GQA_GROUP + col * HEADS_PER_COL + half
                   for g in range(KV_HEADS) for half in range(HEADS_PER_COL) for col in range(COLS_PER_GROUP))
KV_EXPAND = KV_HEADS * HEADS_PER_COL


def _expand_kv(x):
    assert HEADS_PER_COL == 2 and KV_HEADS == 2 and x.shape[1] == LANES
    low = lax.broadcasted_iota(I32, x.shape, 1) < HEAD_DIM
    xr = pltpu.roll(x, HEAD_DIM, axis=1)
    zero = jnp.zeros_like(x)
    blocks = [jnp.where(low, x, zero), jnp.where(low, zero, xr), jnp.where(low, xr, zero), jnp.where(low, zero, x)]
    return jnp.concatenate(blocks, axis=1).astype(BF16)


def _attn_core(q, kx, vx, bias, sink):
    tq = q.shape[0]
    scores = []
    for g in range(KV_HEADS):
        cols = [q[:, (g * COLS_PER_GROUP + c) * LANES:(g * COLS_PER_GROUP + c + 1) * LANES]
                for c in range(COLS_PER_GROUP)]
        qst = jnp.concatenate(cols, axis=0)
        for half in range(HEADS_PER_COL):
            blk = g * HEADS_PER_COL + half
            scores.append(_dot_nt(qst, kx[:, blk * LANES:(blk + 1) * LANES]))
    s = jnp.concatenate(scores, axis=0) + bias
    m =jnp.maximum(jnp.max(s, axis=-1, keepdims=True), sink)
    p = jnp.exp(s - m)
    den = jnp.sum(p, axis=-1, keepdims=True) + jnp.exp(sink - m)
    pn = (p * (1.0 / den)).astype(BF16)
    rows = COLS_PER_GROUP * tq
    outs = []
    for g in range(KV_HEADS):
        o = None
        for half in range(HEADS_PER_COL):
            blk = g * HEADS_PER_COL + half
            part = _dot(pn[blk * rows:(blk + 1) * rows], vx[:, blk * LANES:(blk + 1) * LANES])
            o = part if o is None else o + part
        outs.extend(o[c * tq:(c + 1) * tq] for c in range(COLS_PER_GROUP))
    return outs


def _bias_rows(table, bucket):
    onehot = (jnp.asarray(bucket)[..., None] == jnp.arange(NUM_BUCKETS)).astype(F32)
    cols = jnp.stack([table[:, h] for h in HEAD_ORDER], axis=1)
    bias = jnp.einsum('...qkb,bh->...hqk', onehot, cols, precision=lax.Precision.HIGHEST)
    return bias.reshape(*bucket.shape[:-2], ATTN_HEADS * bucket.shape[-2], bucket.shape[-1])


def _pad_keys(bias):
    tk = bias.shape[-1]
    pad = [(0, 0)] * (bias.ndim - 1) + [(0, -tk % LANES)]
    return jnp.pad(bias, pad, constant_values=-jnp.inf)


def _sink_rows(sinks, tq):
    return jnp.repeat(jnp.stack([sinks[h] for h in HEAD_ORDER]).astype(F32), tq).reshape(ATTN_HEADS * tq, 1)


def _attn_prompt_kernel(cb, q_ref, k_ref, v_ref, km_ref, vm_ref, bias_ref, sink_ref, o_ref, kx_ref, vx_ref):
    step = pl.program_id(1)
    s_len = k_ref.shape[1]
    meta_at = WINDOW + s_len

    @pl.when(step == 0)
    def _():
        piece = min(s_len, 512)
        for src, meta, dst in ((k_ref, km_ref, kx_ref), (v_ref, vm_ref, vx_ref)):
            dst[0:WINDOW] = jnp.zeros((WINDOW, dst.shape[1]), BF16)
            for r in range(0, s_len, piece):
                dst[WINDOW + r:WINDOW + r + piece] = _expand_kv(src[0, r:r + piece, :])
            dst[meta_at:meta_at + N_META] = _expand_kv(meta[...])
            dst[meta_at + N_META:] = jnp.zeros((dst.shape[0] - meta_at - N_META, dst.shape[1]), BF16)

    win = WINDOW + CHUNK
    tail = kx_ref.shape[0] - meta_at
    for j in range(cb):
        c = step * cb + j
        start = pl.multiple_of(c * CHUNK, CHUNK)
        kall = jnp.concatenate([kx_ref[pl.ds(start, win), :], kx_ref[meta_at:meta_at + tail, :]], axis=0)
        vall = jnp.concatenate([vx_ref[pl.ds(start, win), :], vx_ref[meta_at:meta_at + tail, :]], axis=0)
        rows = slice(j * CHUNK, (j + 1) * CHUNK)
        outs = _attn_core(q_ref[0, rows, :], kall, vall, bias_ref[jnp.minimum(c, bias_ref.shape[0] - 1)],
                          sink_ref[...])
        for ci, o in enumerate(outs):
            o_ref[0, rows, ci * LANES:(ci + 1) * LANES] = o.astype(o_ref.dtype)


def _attn_prompt(q, k, v, k_meta, v_meta, table, sinks):
    b, s, aw = q.shape
    kvw = k.shape[-1]
    nc = s // CHUNK
    cb = ATTN_CHUNKS_PER_STEP if nc % ATTN_CHUNKS_PER_STEP == 0 else 1
    assert s % min(s, 512) == 0
    n_bias = 1
    while True:
        qpos = N_META + (n_bias - 1) * CHUNK
        if np.all(_t5_bucket_np(np.arange(N_META) - qpos) == _t5_bucket_np(np.arange(N_META) - qpos - 10 ** 6)):
            break
        n_bias += 1
    n_bias = min(max(n_bias, WINDOW_CHUNKS + 1), nc)
    cs = np.arange(n_bias)[:, None]
    qpos = N_META + cs * CHUNK + np.arange(CHUNK)[None]
    wpos = N_META + (cs - WINDOW_CHUNKS) * CHUNK + np.arange(WINDOW + CHUNK)[None]
    kpos = np.concatenate([wpos, np.broadcast_to(np.arange(N_META), (n_bias, N_META))], axis=1)
    valid = np.concatenate([wpos >= N_META, np.ones((n_bias, N_META), bool)], axis=1)
    bias = _bias_rows(table, _t5_bucket_np(kpos[:, None, :] - qpos[:, :, None]))
    bias = _pad_keys(jnp.where(valid[:, None, :], bias, -jnp.inf))
    sink = _sink_rows(sinks, CHUNK)
    qs = pl.BlockSpec((1, cb * CHUNK, aw), lambda i, c: (i, c, 0))
    kv = pl.BlockSpec((1, s, kvw), lambda i, c: (i, 0, 0))
    meta = pl.BlockSpec((N_META, kvw), lambda i, c: (0, 0))
    xrows = WINDOW + s + bias.shape[-1] - (WINDOW + CHUNK)
    return pl.pallas_call(
        functools.partial(_attn_prompt_kernel, cb),
        grid=(b, nc // cb),
        in_specs=[qs, kv, kv, meta, meta, pl.BlockSpec(bias.shape, lambda i, c: (0, 0, 0)),
                  pl.BlockSpec(sink.shape, lambda i, c: (0, 0))],
        out_specs=qs,
        out_shape=jax.ShapeDtypeStruct((b, s, aw), BF16),
        scratch_shapes=[pltpu.VMEM((xrows, KV_EXPAND * LANES), BF16), pltpu.VMEM((xrows, KV_EXPAND * LANES), BF16)],
        compiler_params=_params("arbitrary", "arbitrary"),
        name="attn_prompt",
    )(q, k, v, k_meta, v_meta, bias, sink)


def _attn_sample_kernel(q_ref, kc_ref, vc_ref, kn_ref, vn_ref, km_ref, vm_ref, bias_ref, sink_ref, o_ref):
    tk = kc_ref.shape[1] + kn_ref.shape[1] + km_ref.shape[0]
    zeros = jnp.zeros((bias_ref.shape[1] - tk, km_ref.shape[1]), F32)
    kall = _expand_kv(jnp.concatenate([kc_ref[0], kn_ref[0], km_ref[...], zeros], axis=0))
    vall = _expand_kv(jnp.concatenate([vc_ref[0], vn_ref[0], vm_ref[...], zeros], axis=0))
    outs = _attn_core(q_ref[0], kall, vall, bias_ref[...], sink_ref[...])
    for ci, o in enumerate(outs):
        o_ref[0, :, ci * LANES:(ci + 1) * LANES] = o.astype(o_ref.dtype)


def _attn_sample(q, k_cache, v_cache, k_new, v_new, k_meta, v_meta, table, sinks):
    bd, sd, aw = q.shape
    kvw = k_new.shape[-1]
    win = k_cache.shape[1]
    qpos = N_META + PAST_LEN + np.arange(sd)
    kpos = np.concatenate([N_META + PAST_LEN - win + np.arange(win), qpos, np.arange(N_META)])
    bias = _pad_keys(_bias_rows(table, _t5_bucket_np(kpos[None, :] - qpos[:, None])))
    sink = _sink_rows(sinks, sd)
    per = lambda n, w: pl.BlockSpec((1, n, w), lambda i: (i, 0, 0))
    meta = pl.BlockSpec((N_META, kvw), lambda i: (0, 0))
    return pl.pallas_call(
        _attn_sample_kernel,
        grid=(bd,),
        in_specs=[per(sd, aw), per(win, kvw), per(win, kvw), per(sd, kvw), per(sd, kvw), meta, meta,
                  pl.BlockSpec(bias.shape, lambda i: (0, 0)), pl.BlockSpec(sink.shape, lambda i: (0, 0))],
        out_specs=per(sd, aw),
        out_shape=jax.ShapeDtypeStruct((bd, sd, aw), BF16),
        compiler_params=_params("arbitrary"),
        name="attn_sample",
    )(q, k_cache, v_cache, k_new, v_new, k_meta, v_meta, bias, sink)


ROUTE_E1, ROUTE_E2, ROUTE_W1, ROUTE_W2, ROUTE_R1, ROUTE_R2 = range(6)


def _merge_kernel(hw, x_ref, yr_ref, at_ref, sga_ref, sgb_ref, wb_ref, wo_ref, wn_ref, wr_ref, br_ref, cnt0_ref,
                  h_ref, xn_ref, route_ref, cnt_ref, carry_ref):
    i = pl.program_id(0)

    @pl.when(i == 0)
    def _():
        carry_ref[...] = cnt0_ref[...]

    br = _dot(yr_ref[...], wb_ref[0:hw, :])
    ba = _dot(at_ref[...], wb_ref[hw:, :])
    merged = sga_ref[...].astype(F32) * br + sgb_ref[...].astype(F32) * ba
    h = x_ref[...] + _dot(merged.astype(BF16), wo_ref[...])
    h_ref[...] = h
    ms = jnp.mean(h * h, axis=-1, keepdims=True)
    xn = h * lax.rsqrt(ms + EPS) * wn_ref[...]
    _store_row_tiles(xn_ref, xn)

    x_hi = xn.astype(BF16)
    x_lo = (xn - x_hi.astype(F32)).astype(BF16)
    w = wr_ref[...]
    w_hi = w.astype(BF16)
    w_lo = (w - w_hi.astype(F32)).astype(BF16)
    logits = _dot(x_hi, w_hi) + _dot(x_hi, w_lo) + _dot(x_lo, w_hi) + br_ref[...]

    tm = logits.shape[0]
    lane_i = lax.broadcasted_iota(I32, (tm, LANES), 1)
    lane = lane_i.astype(F32)
    group_of_lane = (lane_i >> int(math.log2(EXPERTS_PER_GROUP))).astype(F32)
    ninf = -jnp.inf
    first = lambda hit, idx: jnp.min(jnp.where(hit, idx, float(LANES)), axis=-1, keepdims=True)
    gmask = (lane_i >= N_EXPERTS) & (lane_i < N_EXPERTS + N_GROUPS)
    gl = jnp.where(gmask, logits, ninf)
    gmax = jnp.max(gl, axis=-1, keepdims=True)
    gidx = first(gl == gmax, lane - N_EXPERTS)
    gval = 1.0 / jnp.sum(jnp.exp(gl - gmax), axis=-1, keepdims=True)
    emask = (lane_i < N_EXPERTS) & (group_of_lane == gidx)
    el = jnp.where(emask, logits, ninf)
    m1 = jnp.max(el, axis=-1, keepdims=True)
    i1 = first(el == m1, lane)
    el2 = jnp.where(lane == i1, ninf, el)
    m2 = jnp.max(el2, axis=-1, keepdims=True)
    i2 = first(el2 == m2, lane)
    e21 = jnp.exp(m2 - m1)
    w1 = gval / (1.0 + e21)
    w2 = gval * e21 / (1.0 + e21)

    sel1 = lane == i1
    sel2 = lane == i2
    oh = (sel1 | sel2).astype(BF16)
    ri = lax.broadcasted_iota(I32, (tm, tm), 0)
    ci = lax.broadcasted_iota(I32, (tm, tm), 1)
    before = _dot((ci < ri).astype(BF16), oh) + carry_ref[...]
    r1 = jnp.sum(jnp.where(sel1, before, 0.0), axis=-1, keepdims=True)
    r2 = jnp.sum(jnp.where(sel2, before, 0.0), axis=-1, keepdims=True)
    carry_ref[...] = carry_ref[...] + jnp.sum(oh.astype(F32), axis=0, keepdims=True)
    cnt_ref[...] = carry_ref[...]

    rec = jnp.zeros((tm, LANES), F32)
    for slot, val in ((ROUTE_E1, i1), (ROUTE_E2, i2), (ROUTE_W1, w1), (ROUTE_W2, w2), (ROUTE_R1, r1), (ROUTE_R2, r2)):
        rec = jnp.where(lane_i == slot, val, rec)
    route_ref[...] = rec


def _merge(x, y_rec, att, sga, sgb, wb, wo, w_norm, w_router, b_router, cnt0):
    t, d = x.shape
    hw = y_rec.shape[1]
    tm = PROJ_TILE if t % PROJ_TILE == 0 else t
    row = lambda w: pl.BlockSpec((tm, w), lambda i: (i, 0))
    const = lambda a, b: pl.BlockSpec((a, b), lambda i: (0, 0))
    return pl.pallas_call(
        functools.partial(_merge_kernel, hw),
        grid=(t // tm,),
        in_specs=[row(d), row(hw), row(att.shape[1]), row(d), row(d), const(*wb.shape), const(d, d), const(1, d),
                  const(d, LANES), const(1, LANES), const(1, LANES)],
        out_specs=[row(d), pl.BlockSpec((tm * d // LANES, LANES), lambda i: (i, 0)), row(LANES), const(1, LANES)],
        out_shape=[jax.ShapeDtypeStruct((t, d), F32), jax.ShapeDtypeStruct((t * d // LANES, LANES), F32),
                   jax.ShapeDtypeStruct((t, LANES), F32), jax.ShapeDtypeStruct((1, LANES), F32)],
        scratch_shapes=[pltpu.VMEM((1, LANES), F32)],
        compiler_params=_params("arbitrary"),
        name="merge_route",
    )(x, y_rec, att, sga, sgb, wb, wo, w_norm.reshape(1, d), w_router, b_router, cnt0)


def _row_copy(src, dst, sem):
    return pltpu.make_async_copy(src, dst, sem)


ROW_UNROLL = 8


def _tile_rows(ref, row, sub):
    return ref.at[pl.ds(pl.multiple_of(row * sub, sub), sub)]


def _scatter_kernel(n_tok, sub, first, dest_ref, pstart_ref, pend_ref, x_ref, *rest):
    xs_ref, zero_ref, sem, zsem = rest[-4:]

    if first:
        @pl.when(pl.program_id(0) == 0)
        def _():
            zero_ref[...] = jnp.zeros_like(zero_ref)
            n_rows = xs_ref.shape[0] // sub
            used = pend_ref[N_EXPERTS - 1]
            blocks = [(pend_ref[e] > pstart_ref[e], pend_ref[e] - EXPERT_TILE) for e in range(N_EXPERTS)]
            blocks += [(n_rows - (j + 1) * EXPERT_TILE >= used, n_rows - (j + 1) * EXPERT_TILE)
                       for j in range(N_EXPERTS)]
            for wait in (False, True):
                for cond, row in blocks:
                    @pl.when(cond)
                    def _():
                        at = row * sub if isinstance(row, int) else pl.multiple_of(row * sub, EXPERT_TILE * sub)
                        cp = _row_copy(zero_ref, xs_ref.at[pl.ds(at, EXPERT_TILE * sub)], zsem)
                        cp.wait() if wait else cp.start()

    def issue(grp, _):
        for u in range(ROW_UNROLL):
            r = grp * ROW_UNROLL + u
            for k in range(TOP_K):
                _row_copy(_tile_rows(x_ref, r, sub), _tile_rows(xs_ref, dest_ref[0, 0, TOP_K * r + k], sub),
                          sem).start(priority=(u * TOP_K + k) % 2)
        return 0

    lax.fori_loop(0, n_tok // ROW_UNROLL, issue, 0)
    for k in range(TOP_K):
        _row_copy(x_ref, xs_ref.at[pl.ds(0, n_tok * sub)], sem).wait()


def _scatter_rows(x, dest, pstarts, pends, xs, rows):
    n, _, per = dest.shape
    tm = per // TOP_K
    sub = x.shape[0] // (n * tm)
    first = xs is None
    smem = pl.BlockSpec(memory_space=pltpu.SMEM)
    return pl.pallas_call(
        functools.partial(_scatter_kernel, tm, sub, first),
        grid=(n,),
        in_specs=[pl.BlockSpec((1, 1, per), lambda i: (i, 0, 0), memory_space=pltpu.SMEM), smem, smem,
                  pl.BlockSpec((tm * sub, LANES), lambda i: (i, 0))]
        + ([] if first else [pl.BlockSpec(memory_space=pl.ANY)]),
        out_specs=pl.BlockSpec(memory_space=pl.ANY),
        out_shape=jax.ShapeDtypeStruct((rows * sub, LANES), F32),
        scratch_shapes=[pltpu.VMEM((EXPERT_TILE * sub, LANES), F32), pltpu.SemaphoreType.DMA(()),
                        pltpu.SemaphoreType.DMA(())],
        input_output_aliases={} if first else {4: 0},
        compiler_params=_params("arbitrary"),
        name="moe_scatter",
    )(dest, pstarts, pends, x, *([] if first else [xs]))


def _expert_kernel(tm, sub, be_ref, nb_ref, x_ref, wg_ref, wu_ref, wd_ref, y_ref, wgb_ref, wub_ref, wdb_ref):
    i = pl.program_id(0)
    used = i < nb_ref[0]

    @pl.when(used & ((i == 0) | (be_ref[i] != be_ref[jnp.maximum(i - 1, 0)])))
    def _():
        wgb_ref[...] = wg_ref[0].astype(BF16)
        wub_ref[...] = wu_ref[0].astype(BF16)
        wdb_ref[...] = wd_ref[0].astype(BF16)

    @pl.when(used)
    def _():
        x = _load_row_tiles(x_ref, tm, sub).astype(BF16)
        g = _dot(x, wgb_ref[...])
        u = _dot(x, wub_ref[...])
        hmid = (g * _sigmoid(g) * u).astype(BF16)
        _store_row_tiles(y_ref, _dot(hmid, wdb_ref[...]))

    @pl.when(jnp.logical_not(used))
    def _():
        y_ref[...] = jnp.zeros_like(y_ref)


def _experts(xs, block_e, n_used, wg, wu, wd):
    _, d, ff = wg.shape
    sub = d // LANES
    tm = EXPERT_TILE
    blk = pl.BlockSpec((tm * sub, LANES), lambda i, be, nb: (i, 0))
    grid_spec = pltpu.PrefetchScalarGridSpec(
        num_scalar_prefetch=2,
        grid=(xs.shape[0] // (tm * sub),),
        in_specs=[blk,
                  pl.BlockSpec((1, d, ff), lambda i, be, nb: (be[i], 0, 0)),
                  pl.BlockSpec((1, d, ff), lambda i, be, nb: (be[i], 0, 0)),
                  pl.BlockSpec((1, ff, d), lambda i, be, nb: (be[i], 0, 0))],
        out_specs=blk,
        scratch_shapes=[pltpu.VMEM((d, ff), BF16), pltpu.VMEM((d, ff), BF16), pltpu.VMEM((ff, d), BF16)],
    )
    return pl.pallas_call(
        functools.partial(_expert_kernel, tm, sub),
        grid_spec=grid_spec,
        out_shape=jax.ShapeDtypeStruct(xs.shape, F32),
        compiler_params=_params("arbitrary"),
        name="moe_experts",
    )(block_e, n_used, xs, wg, wu, wd)


def _combine_kernel(n_tok, sub, dcur_ref, dnext_ref, h_ref, route_ref, ys_ref, o_ref, buf, sem):
    i = pl.program_id(0)
    n = pl.num_programs(0)
    slot = i % 2

    def start(dest_ref, s):
        def issue(grp, _):
            for u in range(ROW_UNROLL):
                r = grp * ROW_UNROLL + u
                for k in range(TOP_K):
                    _row_copy(_tile_rows(ys_ref, dest_ref[0, 0, TOP_K * r + k], sub), _tile_rows(buf.at[s, k], r, sub),
                              sem.at[s]).start(priority=(u * TOP_K + k) % 2)
            return 0
        lax.fori_loop(0, n_tok // ROW_UNROLL, issue, 0)

    @pl.when(i == 0)
    def _():
        start(dcur_ref, 0)

    @pl.when(i + 1 < n)
    def _():
        start(dnext_ref, 1 - slot)

    for k in range(TOP_K):
        _row_copy(ys_ref.at[pl.ds(0, n_tok * sub)], buf.at[slot, k], sem.at[slot]).wait()
    route = route_ref[...]
    w1 = route[:, ROUTE_W1:ROUTE_W1 + 1]
    w2 = route[:, ROUTE_W2:ROUTE_W2 + 1]
    for j in range(sub):
        cols = slice(j * LANES, (j + 1) * LANES)
        part = lambda k: buf[slot, k, pl.ds(j, n_tok, stride=sub), :]
        o_ref[:, cols] = h_ref[:, cols] + (part(0) * w1 + part(1) * w2)


def _combine(h, route, dest, ys):
    t, d = h.shape
    n, _, per = dest.shape
    tm = t // n
    sub = d // LANES
    dspec = lambda f: pl.BlockSpec((1, 1, per), f, memory_space=pltpu.SMEM)
    return pl.pallas_call(
        functools.partial(_combine_kernel, tm, sub),
        grid=(n,),
        in_specs=[dspec(lambda i: (i, 0, 0)), dspec(lambda i: (jnp.minimum(i + 1, n - 1), 0, 0)),
                  pl.BlockSpec((tm, d), lambda i: (i, 0)), pl.BlockSpec((tm, LANES), lambda i: (i, 0)),
                  pl.BlockSpec(memory_space=pl.ANY)],
        out_specs=pl.BlockSpec((tm, d), lambda i: (i, 0)),
        out_shape=jax.ShapeDtypeStruct((t, d), F32),
        scratch_shapes=[pltpu.VMEM((2, TOP_K, tm * sub, LANES), F32), pltpu.SemaphoreType.DMA((2,))],
        compiler_params=_params("arbitrary"),
        name="moe_combine",
    )(dest, dest, h, route, ys)


def _moe(parts, counts, wg, wu, wd):
    tm = EXPERT_TILE
    n_assign = sum(part[0].shape[0] for part in parts) * TOP_K
    n_blocks = -(-(n_assign + N_EXPERTS * (tm - 1)) // tm)
    counts = counts[0, :N_EXPERTS].astype(I32)
    pcounts = (counts + tm - 1) // tm * tm
    pends = jnp.cumsum(pcounts)
    pstarts = pends - pcounts
    block_start = jnp.arange(n_blocks, dtype=I32) * tm
    block_e = jnp.minimum(jnp.sum((pends[None, :] <= block_start[:, None]).astype(I32), axis=1), N_EXPERTS - 1)
    n_used = pends[-1:] // tm
    xs = None
    dests = []
    for h, xn, route in parts:
        t = h.shape[0]
        mt = MOVE_TILE if t % MOVE_TILE == 0 else t
        assert mt % ROW_UNROLL == 0
        e = route[:, ROUTE_E1:ROUTE_E2 + 1].astype(I32)
        rank = route[:, ROUTE_R1:ROUTE_R2 + 1].astype(I32)
        seg = jnp.sum(jnp.where(e[..., None] == jnp.arange(N_EXPERTS, dtype=I32), pstarts, 0), axis=-1)
        dests.append((seg + rank).reshape(t // mt, 1, mt * TOP_K))
        xs = _scatter_rows(xn, dests[-1], pstarts, pends, xs, n_blocks * tm)
    ys = _experts(xs, block_e, n_used, wg, wu, wd)
    return [_combine(h, route, dest, ys) for (h, _, route), dest in zip(parts, dests)]


def kernel(x_prompt, x_sample, cache_swa_k, cache_swa_v, state_hgrn, meta_tokens, rel_bias_table, hgrn_lower_bounds, w_norm_mix, w_in, hgrn_out_norm, q_norm, k_norm, attn_sinks, w_branch, w_out, w_norm_ffn, w_router_group, b_router_group, w_router_expert, b_router_expert, w_expert_gate, w_expert_up, w_expert_down):
    b, s, d = x_prompt.shape
    bd, sd, _ = x_sample.shape
    depth, _, heads, dk, dv = state_hgrn.shape
    assert depth == 1 and heads == HGRN_HEADS and dk == dv
    hw = heads * dk
    aw = ATTN_HEADS * HEAD_DIM
    kvw = KV_HEADS * HEAD_DIM
    assert w_in.shape[-1] == 4 * hw + aw + 2 * kvw + 2 * d
    assert s % HGRN_CHUNK == 0 and s % CHUNK == 0 and sd == N_META and N_EXPERTS + N_GROUPS <= LANES
    l = 0

    p = jax.nn.softmax(hgrn_lower_bounds.astype(F32), axis=0)
    lb = jnp.cumsum(p, axis=0)[l + 1] - p[0]

    w_in_b = w_in[l].astype(BF16)
    proj = functools.partial(_inproj, w_norm=w_norm_mix[l], w_in_bf16=w_in_b, q_gain=q_norm[l], k_gain=k_norm[l],
                             hw=hw, aw=aw, kvw=kvw)
    x_small = jnp.concatenate([x_sample.reshape(bd * sd, d), meta_tokens.astype(F32)], axis=0)
    qr_s, z_s, hv_s, hg_s, qa_s, k_s, v_s, sga_s, sgb_s = proj(x_small)
    qr_p, z_p, hv_p, hg_p, qa_p, k_p, v_p, sga_p, sgb_p = proj(x_prompt.reshape(b * s, d))
    ns = bd * sd
    k_meta, v_meta = k_s[ns:], v_s[ns:]

    streams = lambda a, n: a.reshape(n, -1, a.shape[-1])
    s0_small = jnp.concatenate([state_hgrn[l].astype(F32), jnp.zeros((1, heads, dk, dv), F32)], axis=0)
    y_small, st_small = _hgrn(streams(qr_s, bd + 1), streams(z_s, bd + 1), streams(hv_s, bd + 1),
                              streams(hg_s, bd + 1), lb, hgrn_out_norm[l], s0_small, sd)
    s0_p = jnp.broadcast_to(st_small[bd:], (b, heads, dk, dv))
    y_p, st_p = _hgrn(streams(qr_p, b), streams(z_p, b), streams(hv_p, b), streams(hg_p, b), lb,
                      hgrn_out_norm[l], s0_p, HGRN_CHUNK)

    table = rel_bias_table.astype(F32)
    att_p = _attn_prompt(streams(qa_p, b), streams(k_p, b), streams(v_p, b), k_meta, v_meta, table, attn_sinks[l])
    kc = cache_swa_k[l].astype(F32).reshape(bd, -1, kvw)
    vc = cache_swa_v[l].astype(F32).reshape(bd, -1, kvw)
    k_new, v_new = k_s[:ns].reshape(bd, sd, kvw), v_s[:ns].reshape(bd, sd, kvw)
    att_s = _attn_sample(qa_s[:ns].reshape(bd, sd, aw), kc, vc, k_new, v_new, k_meta, v_meta, table, attn_sinks[l])

    wb = w_branch[l].astype(BF16)
    wo = w_out[l].astype(BF16)
    w_router = jnp.pad(jnp.concatenate([w_router_expert[l], w_router_group[l]], axis=1).astype(F32),
                       ((0, 0), (0, LANES - N_EXPERTS - N_GROUPS)))
    b_router = jnp.pad(jnp.concatenate([b_router_expert[l], b_router_group[l]]).astype(F32),
                       (0, LANES - N_EXPERTS - N_GROUPS)).reshape(1, LANES)
    merge = functools.partial(_merge, wb=wb, wo=wo, w_norm=w_norm_ffn[l], w_router=w_router, b_router=b_router)
    h_p, xn_p, route_p, cnt_p = merge(x_prompt.reshape(b * s, d), y_p.reshape(b * s, hw), att_p.reshape(b * s, aw),
                                      sga_p, sgb_p, cnt0=jnp.zeros((1, LANES), F32))
    h_s, xn_s, route_s, cnt_s = merge(x_sample.reshape(ns, d), y_small[:bd].reshape(ns, hw), att_s.reshape(ns, aw),
                                      sga_s[:ns], sgb_s[:ns], cnt0=cnt_p)

    out_p, out_s = _moe([(h_p, xn_p, route_p), (h_s, xn_s, route_s)], cnt_s,
                        w_expert_gate[l], w_expert_up[l], w_expert_down[l])

    tail = lambda a: a.reshape(b, s, KV_HEADS, HEAD_DIM)[:, s - WINDOW:][None]
    roll = lambda cache, new: jnp.concatenate([cache, new], axis=1)[:, -WINDOW:].reshape(
        bd, WINDOW, KV_HEADS, HEAD_DIM)[None]
    return (out_p.reshape(b, s, d), out_s.reshape(bd, sd, d),
            tail(k_p), tail(v_p), st_p[None],
            roll(kc, k_new), roll(vc, v_new), st_small[:bd][None])
```

```python
import functools
import math

import numpy as np
import jax
import jax.numpy as jnp
from jax import lax
from jax.experimental import pallas as pl
from jax.experimental.pallas import tpu as pltpu

F32 = jnp.float32
BF16 = jnp.bfloat16
I32 = jnp.int32

CHUNK = 64
N_META = 16
PAST_LEN = 2048
EPS = 1e-6
HGRN_HEADS = 4
ATTN_HEADS = 8
KV_HEADS = 2
HEAD_DIM = 64
GQA_GROUP = ATTN_HEADS // KV_HEADS
WINDOW = 128
WINDOW_CHUNKS = WINDOW // CHUNK
NUM_BUCKETS = 32
MAX_DISTANCE = 128
N_GROUPS = 4
EXPERTS_PER_GROUP = 8
N_EXPERTS = N_GROUPS * EXPERTS_PER_GROUP
TOP_K = 2

LANES = 128
VMEM_LIMIT = 56 * 1024 * 1024

PROJ_TILE = 512
HGRN_CHUNK = 128
HGRN_CHUNKS_PER_STEP = 4
ATTN_CHUNKS_PER_STEP = 4
EXPERT_TILE = 256
MOVE_TILE = 256


LOG2E = math.log2(math.e)


def _sigmoid(x):
    return 1.0 / (1.0 + jnp.exp(-x))


def _split3(x):
    hi = x.astype(BF16)
    r1 = x - hi.astype(F32)
    mid = r1.astype(BF16)
    lo = (r1 - mid.astype(F32)).astype(BF16)
    return hi, mid, lo


def _dot(a, b):
    return jnp.dot(a, b, preferred_element_type=F32)


def _dot_nt(a, b):
    return lax.dot_general(a, b, (((1,), (1,)), ((), ())), preferred_element_type=F32)


def _dot_tn(a, b):
    return lax.dot_general(a, b, (((0,), (0,)), ((), ())), preferred_element_type=F32)


SUBLANES = 8


def _store_row_tiles(ref, x):
    n, d = x.shape
    sub = d // LANES
    for j in range(sub):
        ref[pl.ds(j, n, stride=sub), :] = x[:, j * LANES:(j + 1) * LANES]


def _load_row_tiles(ref, n, sub):
    return jnp.concatenate([ref[pl.ds(j, n, stride=sub), :] for j in range(sub)], axis=1)


def _params(*sem):
    return pltpu.CompilerParams(dimension_semantics=sem, vmem_limit_bytes=VMEM_LIMIT)


def _inproj_kernel(hw, aw, kvw, d, x_ref, wn_ref, w_ref, qg_ref, kg_ref, bdq_ref, bdk_ref,
                   qr_ref, z_ref, hv_ref, hg_ref, qa_ref, k_ref, v_ref, sga_ref, sgb_ref):
    x = x_ref[...]
    ms = jnp.mean(x * x, axis=-1, keepdims=True)
    xn = (x * lax.rsqrt(ms + EPS) * wn_ref[...]).astype(BF16)

    def seg(a, b):
        return _dot(xn, w_ref[:, a:b])

    def head_rms(a, bd_ref, gain):
        sq = a * a
        hi = sq.astype(BF16)
        lo = (sq - hi.astype(F32)).astype(BF16)
        m = _dot(hi, bd_ref[...]) + _dot(lo, bd_ref[...])
        return a * lax.rsqrt(m + EPS) * gain

    o = 0
    hq = seg(o, o + hw)
    qr_ref[...] = (hq * _sigmoid(hq) * (hw // HGRN_HEADS) ** -0.5).astype(BF16)
    o += hw
    z_ref[...] = seg(o, o + hw)
    o += hw
    hv_ref[...] = seg(o, o + hw).astype(BF16)
    o += hw
    hg_ref[...] = seg(o, o + hw).astype(BF16)
    o += hw
    aq = seg(o, o + aw)
    qa_ref[...] = (head_rms(aq, bdq_ref, qg_ref[...]) * (HEAD_DIM ** -0.5 * LOG2E)).astype(BF16)
    o += aw
    k_ref[...] = head_rms(seg(o, o + kvw), bdk_ref, kg_ref[...])
    o += kvw
    v_ref[...] = seg(o, o + kvw)
    o += kvw
    sga_ref[...] = _sigmoid(seg(o, o + d)).astype(BF16)
    o += d
    sgb_ref[...] = _sigmoid(seg(o, o + d)).astype(BF16)


def _block_diag_mean(width, group):
    i = np.arange(width)
    return jnp.asarray((i[:, None] // group == i[None, :] // group) / group, dtype=BF16)


def _inproj(x, w_norm, w_in_bf16, q_gain, k_gain, hw, aw, kvw):
    t, d = x.shape
    tm = PROJ_TILE if t % PROJ_TILE == 0 else t
    cols = w_in_bf16.shape[1]
    row = lambda w: pl.BlockSpec((tm, w), lambda i: (i, 0))
    const = lambda a, b: pl.BlockSpec((a, b), lambda i: (0, 0))
    outs = [(hw, BF16), (hw, F32), (hw, BF16), (hw, BF16), (aw, BF16), (kvw, F32), (kvw, F32), (d, BF16), (d, BF16)]
    return pl.pallas_call(
        functools.partial(_inproj_kernel, hw, aw, kvw, d),
        grid=(t // tm,),
        in_specs=[row(d), const(1, d), const(d, cols), const(1, aw), const(1, kvw), const(aw, aw), const(kvw, kvw)],
        out_specs=[row(w) for w, _ in outs],
        out_shape=[jax.ShapeDtypeStruct((t, w), dt) for w, dt in outs],
        compiler_params=_params("arbitrary"),
        name="inproj",
    )(x, w_norm.reshape(1, d), w_in_bf16,
      jnp.tile(q_gain, aw // HEAD_DIM).reshape(1, aw), jnp.tile(k_gain, kvw // HEAD_DIM).reshape(1, kvw),
      _block_diag_mean(aw, HEAD_DIM), _block_diag_mean(kvw, HEAD_DIM))


def _hgrn_consts(L):
    t = np.arange(L)
    u = t[None, :]
    blocks = [u <= t[:, None], u > t[:, None]]
    levels = []
    m = L // 2
    while m >= 1:
        levels.append(m)
        m //= 2
    lvl = np.full((L, L), -1, np.int32)
    lvl[t, t] = len(levels)
    isq_cols = []
    for j, m in enumerate(levels):
        bnd = (t // (2 * m)) * (2 * m) + m - 1
        isq = (t % (2 * m)) >= m
        cq = isq[:, None] & (u > bnd[:, None]) & (u <= t[:, None])
        ck = (~isq)[:, None] & (u > t[:, None]) & (u <= bnd[:, None])
        blocks.append(cq | ck)
        same = (t[:, None] // (2 * m)) == (t[None, :] // (2 * m))
        lvl[same & isq[:, None] & (~isq)[None, :]] = j
        isq_cols.append(isq)
    c = np.concatenate(blocks, axis=0).astype(np.float32)
    isq = np.stack(isq_cols, axis=1).astype(np.float32)
    isq = np.pad(isq, ((0, 0), (0, LANES - isq.shape[1])))
    c2 = np.tile(c, (1, 2))
    return jnp.asarray(c2, dtype=BF16), jnp.asarray(np.tile(lvl, (1, 2))), jnp.asarray(isq), len(levels)


def _hgrn_kernel(L, nlev, heads, dk, qr_ref, z_ref, hv_ref, hg_ref, lb_ref, og_ref, c_ref, lvl_ref, isq_ref,
                 s0_ref, y_ref, sout_ref, st_ref):
    c = pl.program_id(1)

    @pl.when(c == 0)
    def _():
        for h in range(heads):
            st_ref[h] = s0_ref[0, h].T

    for cc in range(z_ref.shape[1] // L):
        _hgrn_chunk(L, nlev, heads, dk, slice(cc * L, (cc + 1) * L), qr_ref, z_ref, hv_ref, hg_ref, lb_ref, og_ref,
                    c_ref, lvl_ref, isq_ref, y_ref, st_ref)

    @pl.when(c == pl.num_programs(1) - 1)
    def _():
        for h in range(heads):
            sout_ref[0, h] = st_ref[h].T


def _hgrn_chunk(L, nlev, heads, dk, rows, qr_ref, z_ref, hv_ref, hg_ref, lb_ref, og_ref, c_ref, lvl_ref, isq_ref,
                y_ref, st_ref):
    z = z_ref[0, rows, :]
    lb = lb_ref[...]
    e = jnp.exp(-jnp.abs(z))
    r = 1.0 / (1.0 + e)
    pos = z >= 0
    sig = jnp.where(pos, r, e * r)
    sig_neg = jnp.where(pos, e * r, r)
    logf = jnp.log(lb + (1.0 - lb) * sig)
    kin = (1.0 - lb) * sig_neg
    q = qr_ref[0, rows, :].astype(F32)

    hi, mid, _ = _split3(logf * LOG2E)
    ex = jnp.exp2(_dot(c_ref[...], jnp.concatenate([hi, mid], axis=0)))
    e_b = ex[0:L]
    e_rev = ex[L:2 * L]

    q_in = (q * e_b).astype(BF16)
    k_out = (kin * e_rev).astype(BF16)
    q_b = q.astype(BF16)
    k_b = kin.astype(BF16)
    xs = []
    for j in range(nlev):
        m = L >> (j + 1)
        if m % SUBLANES == 0:
            qk = jnp.concatenate([(q if blk % 2 else kin)[blk * m:(blk + 1) * m] for blk in range(L // m)], axis=0)
        else:
            qk = jnp.where(isq_ref[:, j:j + 1] > 0.5, q, kin)
        xs.append((qk * ex[(2 + j) * L:(3 + j) * L]).astype(BF16))
    lvl = lvl_ref[...]
    v = hv_ref[0, rows, :]
    g = hg_ref[0, rows, :].astype(F32)
    og = og_ref[...]

    def block_diag(x):
        zero = jnp.zeros((x.shape[0], dk), x.dtype)
        return jnp.concatenate([jnp.concatenate([x[:, :dk], zero], axis=1),
                                jnp.concatenate([zero, x[:, dk:]], axis=1)], axis=0)

    for pair in range(heads // 2):
        sl = slice(2 * pair * dk, 2 * (pair + 1) * dk)
        a = jnp.where(lvl == nlev, _dot_nt(q_b[:, sl], block_diag(k_b[:, sl])), 0.0)
        for j in range(nlev):
            xp = xs[j][:, sl]
            a = jnp.where(lvl == j, _dot_nt(xp, block_diag(xp)), a)
        st = jnp.concatenate([st_ref[2 * pair], st_ref[2 * pair + 1]], axis=1)
        vp = v[:, sl]
        o = _dot(a.astype(BF16), block_diag(vp)) + _dot_nt(q_in[:, sl], block_diag(st.astype(BF16)))
        for half in range(2):
            h = 2 * pair + half
            hs = slice(h * dk, (h + 1) * dk)
            st_ref[h] = st_ref[h] * e_b[L - 1:L, hs] + _dot_tn(v[:, hs], k_out[:, hs])
            oh = o[:, half * dk:(half + 1) * dk]
            ms = jnp.mean(oh * oh, axis=-1, keepdims=True)
            gh = g[:, hs]
            y_ref[0, rows, hs] = (oh * lax.rsqrt(ms + EPS) * og[:, hs] * (gh * _sigmoid(gh))).astype(BF16)


def _hgrn(qr, z, hv, hg, lb, out_gain, s0, L):
    b, s, w = z.shape
    heads, dk = s0.shape[1], s0.shape[2]
    cm, lvl, isq, nlev = _hgrn_consts(L)
    per_step = HGRN_CHUNKS_PER_STEP if s % (HGRN_CHUNKS_PER_STEP * L) == 0 else 1
    seq = pl.BlockSpec((1, per_step * L, w), lambda i, c: (i, c, 0))
    const = lambda a: pl.BlockSpec(a.shape, lambda i, c: (0,) * a.ndim)
    state = pl.BlockSpec((1, heads, dk, dk), lambda i, c: (i, 0, 0, 0))
    lb2 = lb.reshape(1, w)
    og2 = jnp.tile(out_gain, heads).reshape(1, w)
    return pl.pallas_call(
        functools.partial(_hgrn_kernel, L, nlev, heads, dk),
        grid=(b, s // (per_step * L)),
        in_specs=[seq, seq, seq, seq, const(lb2), const(og2), const(cm), const(lvl), const(isq), state],
        out_specs=[seq, state],
        out_shape=[jax.ShapeDtypeStruct((b, s, w), BF16), jax.ShapeDtypeStruct(s0.shape, F32)],
        scratch_shapes=[pltpu.VMEM((heads, dk, dk), F32)],
        compiler_params=_params("arbitrary", "arbitrary"),
        name=f"hgrn_scan_{L}",
    )(qr, z, hv, hg, lb2, og2, cm, lvl, isq, s0)


def _t5_bucket_np(rel):
    half = NUM_BUCKETS // 2
    max_exact = half // 2
    assert (NUM_BUCKETS, MAX_DISTANCE) == (32, 128)
    n = np.abs(rel).astype(np.int64)
    nn = np.maximum(n, 1)
    k = np.zeros_like(nn)
    for j in range(1, 48):
        k = np.where(64 * (1 << j) <= nn * nn, j, k)
    large = np.minimum(max_exact + k, half - 1)
    return np.where(rel > 0, half, 0) + np.where(n < max_exact, n, large)


HEADS_PER_COL = LANES // HEAD_DIM
COLS_PER_GROUP = GQA_GROUP // HEADS_PER_COL
HEAD_ORDER = tuple(g * GQA_GROUP + col * HEADS_PER_COL + half
                   for g in range(KV_HEADS) for half in range(HEADS_PER_COL) for col in range(COLS_PER_GROUP))
KV_EXPAND = KV_HEADS * HEADS_PER_COL


def _expand_kv(x):
    assert HEADS_PER_COL == 2 and KV_HEADS == 2 and x.shape[1] == LANES
    low = lax.broadcasted_iota(I32, x.shape, 1) < HEAD_DIM
    xr = pltpu.roll(x, HEAD_DIM, axis=1)
    zero = jnp.zeros_like(x)
    blocks = [jnp.where(low, x, zero), jnp.where(low, zero, xr), jnp.where(low, xr, zero), jnp.where(low, zero, x)]
    return jnp.concatenate(blocks, axis=1).astype(BF16)


def _attn_core(q, kx, vx, bias, sink):
    tq = q.shape[0]
    scores = []
    for g in range(KV_HEADS):
        cols = [q[:, (g * COLS_PER_GROUP + c) * LANES:(g * COLS_PER_GROUP + c + 1) * LANES]
                for c in range(COLS_PER_GROUP)]
        qst = jnp.concatenate(cols, axis=0)
        for half in range(HEADS_PER_COL):
            blk = g * HEADS_PER_COL + half
            scores.append(_dot_nt(qst, kx[:, blk * LANES:(blk + 1) * LANES]))
    s = jnp.concatenate(scores, axis=0) + bias
    m =jnp.maximum(jnp.max(s, axis=-1, keepdims=True), sink)
    p = jnp.exp2(s - m)
    den = jnp.sum(p, axis=-1, keepdims=True) + jnp.exp2(sink - m)
    pn = (p * (1.0 / den)).astype(BF16)
    rows = COLS_PER_GROUP * tq
    outs = []
    for g in range(KV_HEADS):
        o = None
        for half in range(HEADS_PER_COL):
            blk = g * HEADS_PER_COL + half
            part = _dot(pn[blk * rows:(blk + 1) * rows], vx[:, blk * LANES:(blk + 1) * LANES])
            o = part if o is None else o + part
        outs.extend(o[c * tq:(c + 1) * tq] for c in range(COLS_PER_GROUP))
    return outs


def _bias_rows(table, bucket):
    onehot = (jnp.asarray(bucket)[..., None] == jnp.arange(NUM_BUCKETS)).astype(F32)
    cols = jnp.stack([table[:, h] for h in HEAD_ORDER], axis=1)
    bias = jnp.einsum('...qkb,bh->...hqk', onehot, cols, precision=lax.Precision.HIGHEST)
    return bias.reshape(*bucket.shape[:-2], ATTN_HEADS * bucket.shape[-2], bucket.shape[-1]) * LOG2E


def _pad_keys(bias):
    tk = bias.shape[-1]
    pad = [(0, 0)] * (bias.ndim - 1) + [(0, -tk % LANES)]
    return jnp.pad(bias, pad, constant_values=-jnp.inf)


def _sink_rows(sinks, tq):
    rows = jnp.repeat(jnp.stack([sinks[h] for h in HEAD_ORDER]).astype(F32), tq)
    return rows.reshape(ATTN_HEADS * tq, 1) * LOG2E


def _attn_prompt_kernel(cb, q_ref, k_ref, v_ref, km_ref, vm_ref, bias_ref, sink_ref, o_ref, kx_ref, vx_ref):
    step = pl.program_id(1)
    s_len = k_ref.shape[1]
    meta_at = WINDOW + s_len

    @pl.when(step == 0)
    def _():
        piece = min(s_len, 512)
        for src, meta, dst in ((k_ref, km_ref, kx_ref), (v_ref, vm_ref, vx_ref)):
            dst[0:WINDOW] = jnp.zeros((WINDOW, dst.shape[1]), BF16)
            for r in range(0, s_len, piece):
                dst[WINDOW + r:WINDOW + r + piece] = _expand_kv(src[0, r:r + piece, :])
            dst[meta_at:meta_at + N_META] = _expand_kv(meta[...])
            dst[meta_at + N_META:] = jnp.zeros((dst.shape[0] - meta_at - N_META, dst.shape[1]), BF16)

    win = WINDOW + CHUNK
    tail = kx_ref.shape[0] - meta_at
    for j in range(cb):
        c = step * cb + j
        start = pl.multiple_of(c * CHUNK, CHUNK)
        kall = jnp.concatenate([kx_ref[pl.ds(start, win), :], kx_ref[meta_at:meta_at + tail, :]], axis=0)
        vall = jnp.concatenate([vx_ref[pl.ds(start, win), :], vx_ref[meta_at:meta_at + tail, :]], axis=0)
        rows = slice(j * CHUNK, (j + 1) * CHUNK)
        outs = _attn_core(q_ref[0, rows, :], kall, vall, bias_ref[jnp.minimum(c, bias_ref.shape[0] - 1)],
                          sink_ref[...])
        for ci, o in enumerate(outs):
            o_ref[0, rows, ci * LANES:(ci + 1) * LANES] = o.astype(o_ref.dtype)


def _attn_prompt(q, k, v, k_meta, v_meta, table, sinks):
    b, s, aw = q.shape
    kvw = k.shape[-1]
    nc = s // CHUNK
    cb = ATTN_CHUNKS_PER_STEP if nc % ATTN_CHUNKS_PER_STEP == 0 else 1
    assert s % min(s, 512) == 0
    n_bias = 1
    while True:
        qpos = N_META + (n_bias - 1) * CHUNK
        if np.all(_t5_bucket_np(np.arange(N_META) - qpos) == _t5_bucket_np(np.arange(N_META) - qpos - 10 ** 6)):
            break
        n_bias += 1
    n_bias = min(max(n_bias, WINDOW_CHUNKS + 1), nc)
    cs = np.arange(n_bias)[:, None]
    qpos = N_META + cs * CHUNK + np.arange(CHUNK)[None]
    wpos = N_META + (cs - WINDOW_CHUNKS) * CHUNK + np.arange(WINDOW + CHUNK)[None]
    kpos = np.concatenate([wpos, np.broadcast_to(np.arange(N_META), (n_bias, N_META))], axis=1)
    valid = np.concatenate([wpos >= N_META, np.ones((n_bias, N_META), bool)], axis=1)
    bias = _bias_rows(table, _t5_bucket_np(kpos[:, None, :] - qpos[:, :, None]))
    bias = _pad_keys(jnp.where(valid[:, None, :], bias, -jnp.inf))
    sink = _sink_rows(sinks, CHUNK)
    qs = pl.BlockSpec((1, cb * CHUNK, aw), lambda i, c: (i, c, 0))
    kv = pl.BlockSpec((1, s, kvw), lambda i, c: (i, 0, 0))
    meta = pl.BlockSpec((N_META, kvw), lambda i, c: (0, 0))
    xrows = WINDOW + s + bias.shape[-1] - (WINDOW + CHUNK)
    return pl.pallas_call(
        functools.partial(_attn_prompt_kernel, cb),
        grid=(b, nc // cb),
        in_specs=[qs, kv, kv, meta, meta, pl.BlockSpec(bias.shape, lambda i, c: (0, 0, 0)),
                  pl.BlockSpec(sink.shape, lambda i, c: (0, 0))],
        out_specs=qs,
        out_shape=jax.ShapeDtypeStruct((b, s, aw), BF16),
        scratch_shapes=[pltpu.VMEM((xrows, KV_EXPAND * LANES), BF16), pltpu.VMEM((xrows, KV_EXPAND * LANES), BF16)],
        compiler_params=_params("arbitrary", "arbitrary"),
        name="attn_prompt",
    )(q, k, v, k_meta, v_meta, bias, sink)


def _attn_sample_kernel(q_ref, kc_ref, vc_ref, kn_ref, vn_ref, km_ref, vm_ref, bias_ref, sink_ref, o_ref):
    tk = kc_ref.shape[1] + kn_ref.shape[1] + km_ref.shape[0]
    zeros = jnp.zeros((bias_ref.shape[1] - tk, km_ref.shape[1]), F32)
    kall = _expand_kv(jnp.concatenate([kc_ref[0], kn_ref[0], km_ref[...], zeros], axis=0))
    vall = _expand_kv(jnp.concatenate([vc_ref[0], vn_ref[0], vm_ref[...], zeros], axis=0))
    outs = _attn_core(q_ref[0], kall, vall, bias_ref[...], sink_ref[...])
    for ci, o in enumerate(outs):
        o_ref[0, :, ci * LANES:(ci + 1) * LANES] = o.astype(o_ref.dtype)


def _attn_sample(q, k_cache, v_cache, k_new, v_new, k_meta, v_meta, table, sinks):
    bd, sd, aw = q.shape
    kvw = k_new.shape[-1]
    win = k_cache.shape[1]
    qpos = N_META + PAST_LEN + np.arange(sd)
    kpos = np.concatenate([N_META + PAST_LEN - win + np.arange(win), qpos, np.arange(N_META)])
    bias = _pad_keys(_bias_rows(table, _t5_bucket_np(kpos[None, :] - qpos[:, None])))
    sink = _sink_rows(sinks, sd)
    per = lambda n, w: pl.BlockSpec((1, n, w), lambda i: (i, 0, 0))
    meta = pl.BlockSpec((N_META, kvw), lambda i: (0, 0))
    return pl.pallas_call(
        _attn_sample_kernel,
        grid=(bd,),
        in_specs=[per(sd, aw), per(win, kvw), per(win, kvw), per(sd, kvw), per(sd, kvw), meta, meta,
                  pl.BlockSpec(bias.shape, lambda i: (0, 0)), pl.BlockSpec(sink.shape, lambda i: (0, 0))],
        out_specs=per(sd, aw),
        out_shape=jax.ShapeDtypeStruct((bd, sd, aw), BF16),
        compiler_params=_params("arbitrary"),
        name="attn_sample",
    )(q, k_cache, v_cache, k_new, v_new, k_meta, v_meta, bias, sink)


ROUTE_E1, ROUTE_E2, ROUTE_W1, ROUTE_W2, ROUTE_R1, ROUTE_R2 = range(6)


def _merge_kernel(hw, x_ref, yr_ref, at_ref, sga_ref, sgb_ref, wb_ref, wo_ref, wn_ref, wr_ref, br_ref, cnt0_ref, tri_ref,
                  h_ref, xn_ref, route_ref, cnt_ref, carry_ref):
    i = pl.program_id(0)

    @pl.when(i == 0)
    def _():
        carry_ref[...] = cnt0_ref[...]

    br = _dot(yr_ref[...], wb_ref[0:hw, :])
    ba = _dot(at_ref[...], wb_ref[hw:, :])
    merged = sga_ref[...].astype(F32) * br + sgb_ref[...].astype(F32) * ba
    h = x_ref[...] + _dot(merged.astype(BF16), wo_ref[...])
    h_ref[...] = h
    ms = jnp.mean(h * h, axis=-1, keepdims=True)
    xn = h * lax.rsqrt(ms + EPS) * wn_ref[...]
    _store_row_tiles(xn_ref, xn)

    x_hi = xn.astype(BF16)
    x_lo = (xn - x_hi.astype(F32)).astype(BF16)
    w = wr_ref[...]
    w_hi = w.astype(BF16)
    w_lo = (w - w_hi.astype(F32)).astype(BF16)
    logits = _dot(x_hi, w_hi) + _dot(x_hi, w_lo) + _dot(x_lo, w_hi) + br_ref[...]

    tm = logits.shape[0]
    lane_i = lax.broadcasted_iota(I32, (tm, LANES), 1)
    lane = lane_i.astype(F32)
    group_of_lane = (lane_i >> int(math.log2(EXPERTS_PER_GROUP))).astype(F32)
    ninf = -jnp.inf
    first = lambda hit, idx: jnp.min(jnp.where(hit, idx, float(LANES)), axis=-1, keepdims=True)
    gmask = (lane_i >= N_EXPERTS) & (lane_i < N_EXPERTS + N_GROUPS)
    gl = jnp.where(gmask, logits, ninf)
    gmax = jnp.max(gl, axis=-1, keepdims=True)
    gidx = first(gl == gmax, lane - N_EXPERTS)
    gval = 1.0 / jnp.sum(jnp.exp(gl - gmax), axis=-1, keepdims=True)
    emask = (lane_i < N_EXPERTS) & (group_of_lane == gidx)
    el = jnp.where(emask, logits, ninf)
    m1 = jnp.max(el, axis=-1, keepdims=True)
    i1 = first(el == m1, lane)
    el2 = jnp.where(lane == i1, ninf, el)
    m2 = jnp.max(el2, axis=-1, keepdims=True)
    i2 = first(el2 == m2, lane)
    e21 = jnp.exp(m2 - m1)
    w1 = gval / (1.0 + e21)
    w2 = gval * e21 / (1.0 + e21)

    sel1 = lane == i1
    sel2 = lane == i2
    oh = (sel1 | sel2).astype(BF16)
    before = _dot(tri_ref[...], oh) + carry_ref[...]
    r1 = jnp.sum(jnp.where(sel1, before, 0.0), axis=-1, keepdims=True)
    r2 = jnp.sum(jnp.where(sel2, before, 0.0), axis=-1, keepdims=True)
    carry_ref[...] = carry_ref[...] + jnp.sum(oh.astype(F32), axis=0, keepdims=True)
    cnt_ref[...] = carry_ref[...]

    rec = jnp.zeros((tm, LANES), F32)
    for slot, val in ((ROUTE_E1, i1), (ROUTE_E2, i2), (ROUTE_W1, w1), (ROUTE_W2, w2), (ROUTE_R1, r1), (ROUTE_R2, r2)):
        rec = jnp.where(lane_i == slot, val, rec)
    route_ref[...] = rec


def _merge(x, y_rec, att, sga, sgb, wb, wo, w_norm, w_router, b_router, cnt0):
    t, d = x.shape
    hw = y_rec.shape[1]
    tm = PROJ_TILE if t % PROJ_TILE == 0 else t
    row = lambda w: pl.BlockSpec((tm, w), lambda i: (i, 0))
    const = lambda a, b: pl.BlockSpec((a, b), lambda i: (0, 0))
    return pl.pallas_call(
        functools.partial(_merge_kernel, hw),
        grid=(t // tm,),
        in_specs=[row(d), row(hw), row(att.shape[1]), row(d), row(d), const(*wb.shape), const(d, d), const(1, d),
                  const(d, LANES), const(1, LANES), const(1, LANES), const(tm, tm)],
        out_specs=[row(d), pl.BlockSpec((tm * d // LANES, LANES), lambda i: (i, 0)), row(LANES), const(1, LANES)],
        out_shape=[jax.ShapeDtypeStruct((t, d), F32), jax.ShapeDtypeStruct((t * d // LANES, LANES), F32),
                   jax.ShapeDtypeStruct((t, LANES), F32), jax.ShapeDtypeStruct((1, LANES), F32)],
        scratch_shapes=[pltpu.VMEM((1, LANES), F32)],
        compiler_params=_params("arbitrary"),
        name="merge_route",
    )(x, y_rec, att, sga, sgb, wb, wo, w_norm.reshape(1, d), w_router, b_router, cnt0,
      jnp.asarray(np.tri(tm, k=-1), dtype=BF16))


def _row_copy(src, dst, sem):
    return pltpu.make_async_copy(src, dst, sem)


ROW_UNROLL = 8


def _tile_rows(ref, row, sub):
    return ref.at[pl.ds(pl.multiple_of(row * sub, sub), sub)]


def _scatter_kernel(n_tok, sub, first, dest_ref, pstart_ref, pend_ref, x_ref, *rest):
    xs_ref, zero_ref, sem, zsem = rest[-4:]

    if first:
        @pl.when(pl.program_id(0) == 0)
        def _():
            zero_ref[...] = jnp.zeros_like(zero_ref)
            n_rows = xs_ref.shape[0] // sub
            used = pend_ref[N_EXPERTS - 1]
            blocks = [(pend_ref[e] > pstart_ref[e], pend_ref[e] - EXPERT_TILE) for e in range(N_EXPERTS)]
            blocks += [(n_rows - (j + 1) * EXPERT_TILE >= used, n_rows - (j + 1) * EXPERT_TILE)
                       for j in range(N_EXPERTS)]
            for wait in (False, True):
                for cond, row in blocks:
                    @pl.when(cond)
                    def _():
                        at = row * sub if isinstance(row, int) else pl.multiple_of(row * sub, EXPERT_TILE * sub)
                        cp = _row_copy(zero_ref, xs_ref.at[pl.ds(at, EXPERT_TILE * sub)], zsem)
                        cp.wait() if wait else cp.start()

    def issue(grp, _):
        for u in range(ROW_UNROLL):
            r = grp * ROW_UNROLL + u
            for k in range(TOP_K):
                _row_copy(_tile_rows(x_ref, r, sub), _tile_rows(xs_ref, dest_ref[0, 0, TOP_K * r + k], sub),
                          sem).start(priority=(u * TOP_K + k) % 2)
        return 0

    lax.fori_loop(0, n_tok // ROW_UNROLL, issue, 0)
    for k in range(TOP_K):
        _row_copy(x_ref, xs_ref.at[pl.ds(0, n_tok * sub)], sem).wait()


def _scatter_rows(x, dest, pstarts, pends, xs, rows):
    n, _, per = dest.shape
    tm = per // TOP_K
    sub = x.shape[0] // (n * tm)
    first = xs is None
    smem = pl.BlockSpec(memory_space=pltpu.SMEM)
    return pl.pallas_call(
        functools.partial(_scatter_kernel, tm, sub, first),
        grid=(n,),
        in_specs=[pl.BlockSpec((1, 1, per), lambda i: (i, 0, 0), memory_space=pltpu.SMEM), smem, smem,
                  pl.BlockSpec((tm * sub, LANES), lambda i: (i, 0))]
        + ([] if first else [pl.BlockSpec(memory_space=pl.ANY)]),
        out_specs=pl.BlockSpec(memory_space=pl.ANY),
        out_shape=jax.ShapeDtypeStruct((rows * sub, LANES), F32),
        scratch_shapes=[pltpu.VMEM((EXPERT_TILE * sub, LANES), F32), pltpu.SemaphoreType.DMA(()),
                        pltpu.SemaphoreType.DMA(())],
        input_output_aliases={} if first else {4: 0},
        compiler_params=_params("arbitrary"),
        name="moe_scatter",
    )(dest, pstarts, pends, x, *([] if first else [xs]))


def _expert_kernel(tm, sub, be_ref, nb_ref, x_ref, wg_ref, wu_ref, wd_ref, y_ref, wgb_ref, wub_ref, wdb_ref):
    i = pl.program_id(0)
    used = i < nb_ref[0]

    @pl.when(used & ((i == 0) | (be_ref[i] != be_ref[jnp.maximum(i - 1, 0)])))
    def _():
        wgb_ref[...] = wg_ref[0].astype(BF16)
        wub_ref[...] = wu_ref[0].astype(BF16)
        wdb_ref[...] = wd_ref[0].astype(BF16)

    @pl.when(used)
    def _():
        x = _load_row_tiles(x_ref, tm, sub).astype(BF16)
        g = _dot(x, wgb_ref[...])
        u = _dot(x, wub_ref[...])
        hmid = (g * _sigmoid(g) * u).astype(BF16)
        _store_row_tiles(y_ref, _dot(hmid, wdb_ref[...]))

    @pl.when(jnp.logical_not(used))
    def _():
        y_ref[...] = jnp.zeros_like(y_ref)


def _experts(xs, block_e, n_used, wg, wu, wd):
    _, d, ff = wg.shape
    sub = d // LANES
    tm = EXPERT_TILE
    blk = pl.BlockSpec((tm * sub, LANES), lambda i, be, nb: (i, 0))
    grid_spec = pltpu.PrefetchScalarGridSpec(
        num_scalar_prefetch=2,
        grid=(xs.shape[0] // (tm * sub),),
        in_specs=[blk,
                  pl.BlockSpec((1, d, ff), lambda i, be, nb: (be[i], 0, 0)),
                  pl.BlockSpec((1, d, ff), lambda i, be, nb: (be[i], 0, 0)),
                  pl.BlockSpec((1, ff, d), lambda i, be, nb: (be[i], 0, 0))],
        out_specs=blk,
        scratch_shapes=[pltpu.VMEM((d, ff), BF16), pltpu.VMEM((d, ff), BF16), pltpu.VMEM((ff, d), BF16)],
    )
    return pl.pallas_call(
        functools.partial(_expert_kernel, tm, sub),
        grid_spec=grid_spec,
        out_shape=jax.ShapeDtypeStruct(xs.shape, F32),
        compiler_params=_params("arbitrary"),
        name="moe_experts",
    )(block_e, n_used, xs, wg, wu, wd)


def _combine_kernel(n_tok, sub, dcur_ref, dnext_ref, h_ref, route_ref, ys_ref, o_ref, buf, sem):
    i = pl.program_id(0)
    n = pl.num_programs(0)
    slot = i % 2

    def start(dest_ref, s):
        def issue(grp, _):
            for u in range(ROW_UNROLL):
                r = grp * ROW_UNROLL + u
                for k in range(TOP_K):
                    _row_copy(_tile_rows(ys_ref, dest_ref[0, 0, TOP_K * r + k], sub), _tile_rows(buf.at[s, k], r, sub),
                              sem.at[s]).start(priority=(u * TOP_K + k) % 2)
            return 0
        lax.fori_loop(0, n_tok // ROW_UNROLL, issue, 0)

    @pl.when(i == 0)
    def _():
        start(dcur_ref, 0)

    @pl.when(i + 1 < n)
    def _():
        start(dnext_ref, 1 - slot)

    for k in range(TOP_K):
        _row_copy(ys_ref.at[pl.ds(0, n_tok * sub)], buf.at[slot, k], sem.at[slot]).wait()
    route = route_ref[...]
    w1 = route[:, ROUTE_W1:ROUTE_W1 + 1]
    w2 = route[:, ROUTE_W2:ROUTE_W2 + 1]
    for j in range(sub):
        cols = slice(j * LANES, (j + 1) * LANES)
        part = lambda k: buf[slot, k, pl.ds(j, n_tok, stride=sub), :]
        o_ref[:, cols] = h_ref[:, cols] + (part(0) * w1 + part(1) * w2)


def _combine(h, route, dest, ys):
    t, d = h.shape
    n, _, per = dest.shape
    tm = t // n
    sub = d // LANES
    dspec = lambda f: pl.BlockSpec((1, 1, per), f, memory_space=pltpu.SMEM)
    return pl.pallas_call(
        functools.partial(_combine_kernel, tm, sub),
        grid=(n,),
        in_specs=[dspec(lambda i: (i, 0, 0)), dspec(lambda i: (jnp.minimum(i + 1, n - 1), 0, 0)),
                  pl.BlockSpec((tm, d), lambda i: (i, 0)), pl.BlockSpec((tm, LANES), lambda i: (i, 0)),
                  pl.BlockSpec(memory_space=pl.ANY)],
        out_specs=pl.BlockSpec((tm, d), lambda i: (i, 0)),
        out_shape=jax.ShapeDtypeStruct((t, d), F32),
        scratch_shapes=[pltpu.VMEM((2, TOP_K, tm * sub, LANES), F32), pltpu.SemaphoreType.DMA((2,))],
        compiler_params=_params("arbitrary"),
        name="moe_combine",
    )(dest, dest, h, route, ys)


def _moe(parts, counts, wg, wu, wd):
    tm = EXPERT_TILE
    n_assign = sum(part[0].shape[0] for part in parts) * TOP_K
    n_blocks = -(-(n_assign + N_EXPERTS * (tm - 1)) // tm)
    counts = counts[0, :N_EXPERTS].astype(I32)
    pcounts = (counts + tm - 1) // tm * tm
    pends = jnp.cumsum(pcounts)
    pstarts = pends - pcounts
    block_start = jnp.arange(n_blocks, dtype=I32) * tm
    block_e = jnp.minimum(jnp.sum((pends[None, :] <= block_start[:, None]).astype(I32), axis=1), N_EXPERTS - 1)
    n_used = pends[-1:] // tm
    xs = None
    dests = []
    for h, xn, route in parts:
        t = h.shape[0]
        mt = MOVE_TILE if t % MOVE_TILE == 0 else t
        assert mt % ROW_UNROLL == 0
        e = route[:, ROUTE_E1:ROUTE_E2 + 1].astype(I32)
        rank = route[:, ROUTE_R1:ROUTE_R2 + 1].astype(I32)
        seg = jnp.sum(jnp.where(e[..., None] == jnp.arange(N_EXPERTS, dtype=I32), pstarts, 0), axis=-1)
        dests.append((seg + rank).reshape(t // mt, 1, mt * TOP_K))
        xs = _scatter_rows(xn, dests[-1], pstarts, pends, xs, n_blocks * tm)
    ys = _experts(xs, block_e, n_used, wg, wu, wd)
    return [_combine(h, route, dest, ys) for (h, _, route), dest in zip(parts, dests)]


def kernel(x_prompt, x_sample, cache_swa_k, cache_swa_v, state_hgrn, meta_tokens, rel_bias_table, hgrn_lower_bounds, w_norm_mix, w_in, hgrn_out_norm, q_norm, k_norm, attn_sinks, w_branch, w_out, w_norm_ffn, w_router_group, b_router_group, w_router_expert, b_router_expert, w_expert_gate, w_expert_up, w_expert_down):
    b, s, d = x_prompt.shape
    bd, sd, _ = x_sample.shape
    depth, _, heads, dk, dv = state_hgrn.shape
    assert depth == 1 and heads == HGRN_HEADS and dk == dv
    hw = heads * dk
    aw = ATTN_HEADS * HEAD_DIM
    kvw = KV_HEADS * HEAD_DIM
    assert w_in.shape[-1] == 4 * hw + aw + 2 * kvw + 2 * d
    assert s % HGRN_CHUNK == 0 and s % CHUNK == 0 and sd == N_META and N_EXPERTS + N_GROUPS <= LANES
    l = 0

    p = jax.nn.softmax(hgrn_lower_bounds.astype(F32), axis=0)
    lb = jnp.cumsum(p, axis=0)[l + 1] - p[0]

    w_in_b = w_in[l].astype(BF16)
    proj = functools.partial(_inproj, w_norm=w_norm_mix[l], w_in_bf16=w_in_b, q_gain=q_norm[l], k_gain=k_norm[l],
                             hw=hw, aw=aw, kvw=kvw)
    x_small = jnp.concatenate([x_sample.reshape(bd * sd, d), meta_tokens.astype(F32)], axis=0)
    qr_s, z_s, hv_s, hg_s, qa_s, k_s, v_s, sga_s, sgb_s = proj(x_small)
    qr_p, z_p, hv_p, hg_p, qa_p, k_p, v_p, sga_p, sgb_p = proj(x_prompt.reshape(b * s, d))
    ns = bd * sd
    k_meta, v_meta = k_s[ns:], v_s[ns:]

    streams = lambda a, n: a.reshape(n, -1, a.shape[-1])
    s0_small = jnp.concatenate([state_hgrn[l].astype(F32), jnp.zeros((1, heads, dk, dv), F32)], axis=0)
    y_small, st_small = _hgrn(streams(qr_s, bd + 1), streams(z_s, bd + 1), streams(hv_s, bd + 1),
                              streams(hg_s, bd + 1), lb, hgrn_out_norm[l], s0_small, sd)
    s0_p = jnp.broadcast_to(st_small[bd:], (b, heads, dk, dv))
    y_p, st_p = _hgrn(streams(qr_p, b), streams(z_p, b), streams(hv_p, b), streams(hg_p, b), lb,
                      hgrn_out_norm[l], s0_p, HGRN_CHUNK)

    table = rel_bias_table.astype(F32)
    att_p = _attn_prompt(streams(qa_p, b), streams(k_p, b), streams(v_p, b), k_meta, v_meta, table, attn_sinks[l])
    kc = cache_swa_k[l].astype(F32).reshape(bd, -1, kvw)
    vc = cache_swa_v[l].astype(F32).reshape(bd, -1, kvw)
    k_new, v_new = k_s[:ns].reshape(bd, sd, kvw), v_s[:ns].reshape(bd, sd, kvw)
    att_s = _attn_sample(qa_s[:ns].reshape(bd, sd, aw), kc, vc, k_new, v_new, k_meta, v_meta, table, attn_sinks[l])

    wb = w_branch[l].astype(BF16)
    wo = w_out[l].astype(BF16)
    w_router = jnp.pad(jnp.concatenate([w_router_expert[l], w_router_group[l]], axis=1).astype(F32),
                       ((0, 0), (0, LANES - N_EXPERTS - N_GROUPS)))
    b_router = jnp.pad(jnp.concatenate([b_router_expert[l], b_router_group[l]]).astype(F32),
                       (0, LANES - N_EXPERTS - N_GROUPS)).reshape(1, LANES)
    merge = functools.partial(_merge, wb=wb, wo=wo, w_norm=w_norm_ffn[l], w_router=w_router, b_router=b_router)
    h_p, xn_p, route_p, cnt_p = merge(x_prompt.reshape(b * s, d), y_p.reshape(b * s, hw), att_p.reshape(b * s, aw),
                                      sga_p, sgb_p, cnt0=jnp.zeros((1, LANES), F32))
    h_s, xn_s, route_s, cnt_s = merge(x_sample.reshape(ns, d), y_small[:bd].reshape(ns, hw), att_s.reshape(ns, aw),
                                      sga_s[:ns], sgb_s[:ns], cnt0=cnt_p)

    out_p, out_s = _moe([(h_p, xn_p, route_p), (h_s, xn_s, route_s)], cnt_s,
                        w_expert_gate[l], w_expert_up[l], w_expert_down[l])

    tail = lambda a: a.reshape(b, s, KV_HEADS, HEAD_DIM)[:, s - WINDOW:][None]
    roll = lambda cache, new: jnp.concatenate([cache, new], axis=1)[:, -WINDOW:].reshape(
        bd, WINDOW, KV_HEADS, HEAD_DIM)[None]
    return (out_p.reshape(b, s, d), out_s.reshape(bd, sd, d),
            tail(k_p), tail(v_p), st_p[None],
            roll(kc, k_new), roll(vc, v_new), st_small[:bd][None])
```

```python
import functools
import math

import numpy as np
import jax
import jax.numpy as jnp
from jax import lax
from jax.experimental import pallas as pl
from jax.experimental.pallas import tpu as pltpu

F32 = jnp.float32
BF16 = jnp.bfloat16
I32 = jnp.int32

CHUNK = 64
N_META = 16
PAST_LEN = 2048
EPS = 1e-6
HGRN_HEADS = 4
ATTN_HEADS = 8
KV_HEADS = 2
HEAD_DIM = 64
GQA_GROUP = ATTN_HEADS // KV_HEADS
WINDOW = 128
WINDOW_CHUNKS = WINDOW // CHUNK
NUM_BUCKETS = 32
MAX_DISTANCE = 128
N_GROUPS = 4
EXPERTS_PER_GROUP = 8
N_EXPERTS = N_GROUPS * EXPERTS_PER_GROUP
TOP_K = 2

LANES = 128
VMEM_LIMIT = 56 * 1024 * 1024

INPROJ_TILE = 512
PROJ_TILE = 512
MERGE_SUBTILE = 512
HGRN_CHUNK = 128
HGRN_CHUNKS_PER_STEP = 4
ATTN_CHUNKS_PER_STEP = 8
EXPERT_TILE = 512
MOVE_TILE = 256


LOG2E = math.log2(math.e)


def _sigmoid(x):
    return 1.0 / (1.0 + jnp.exp(-x))


def _split3(x):
    hi = x.astype(BF16)
    r1 = x - hi.astype(F32)
    mid = r1.astype(BF16)
    lo = (r1 - mid.astype(F32)).astype(BF16)
    return hi, mid, lo


def _dot(a, b):
    return jnp.dot(a, b, preferred_element_type=F32)


def _dot_nt(a, b):
    return lax.dot_general(a, b, (((1,), (1,)), ((), ())), preferred_element_type=F32)


def _dot_tn(a, b):
    return lax.dot_general(a, b, (((0,), (0,)), ((), ())), preferred_element_type=F32)


SUBLANES = 8


def _store_row_tiles(ref, x):
    n, d = x.shape
    sub = d // LANES
    for j in range(sub):
        ref[pl.ds(j, n, stride=sub), :] = x[:, j * LANES:(j + 1) * LANES]


def _load_row_tiles(ref, n, sub):
    return jnp.concatenate([ref[pl.ds(j, n, stride=sub), :] for j in range(sub)], axis=1)


def _params(*sem):
    return pltpu.CompilerParams(dimension_semantics=sem, vmem_limit_bytes=VMEM_LIMIT)


def _inproj_kernel(hw, aw, kvw, d, x_ref, wn_ref, w_ref, qg_ref, kg_ref, bdq_ref, bdk_ref,
                   qr_ref, z_ref, hv_ref, hg_ref, qa_ref, k_ref, v_ref, sga_ref, sgb_ref):
    x = x_ref[...]
    ms = jnp.mean(x * x, axis=-1, keepdims=True)
    xn = (x * lax.rsqrt(ms + EPS) * wn_ref[...]).astype(BF16)

    def seg(a, b):
        return _dot(xn, w_ref[:, a:b])

    def head_rms(a, bd_ref, gain):
        sq = a * a
        hi = sq.astype(BF16)
        lo = (sq - hi.astype(F32)).astype(BF16)
        m = _dot(hi, bd_ref[...]) + _dot(lo, bd_ref[...])
        return a * lax.rsqrt(m + EPS) * gain

    o = 0
    hq = seg(o, o + hw)
    qr_ref[...] = (hq * _sigmoid(hq) * (hw // HGRN_HEADS) ** -0.5).astype(BF16)
    o += hw
    z_ref[...] = seg(o, o + hw)
    o += hw
    hv_ref[...] = seg(o, o + hw).astype(BF16)
    o += hw
    hg_ref[...] = seg(o, o + hw).astype(BF16)
    o += hw
    aq = seg(o, o + aw)
    qa_ref[...] = (head_rms(aq, bdq_ref, qg_ref[...]) * (HEAD_DIM ** -0.5 * LOG2E)).astype(BF16)
    o += aw
    k_ref[...] = head_rms(seg(o, o + kvw), bdk_ref, kg_ref[...])
    o += kvw
    v_ref[...] = seg(o, o + kvw)
    o += kvw
    sga_ref[...] = _sigmoid(seg(o, o + d)).astype(BF16)
    o += d
    sgb_ref[...] = _sigmoid(seg(o, o + d)).astype(BF16)


def _block_diag_mean(width, group):
    i = np.arange(width)
    return jnp.asarray((i[:, None] // group == i[None, :] // group) / group, dtype=BF16)


def _inproj(x, w_norm, w_in_bf16, q_gain, k_gain, hw, aw, kvw):
    t, d = x.shape
    tm = INPROJ_TILE if t % INPROJ_TILE == 0 else t
    cols = w_in_bf16.shape[1]
    row = lambda w: pl.BlockSpec((tm, w), lambda i: (i, 0))
    const = lambda a, b: pl.BlockSpec((a, b), lambda i: (0, 0))
    outs = [(hw, BF16), (hw, F32), (hw, BF16), (hw, BF16), (aw, BF16), (kvw, F32), (kvw, F32), (d, BF16), (d, BF16)]
    return pl.pallas_call(
        functools.partial(_inproj_kernel, hw, aw, kvw, d),
        grid=(t // tm,),
        in_specs=[row(d), const(1, d), const(d, cols), const(1, aw), const(1, kvw), const(aw, aw), const(kvw, kvw)],
        out_specs=[row(w) for w, _ in outs],
        out_shape=[jax.ShapeDtypeStruct((t, w), dt) for w, dt in outs],
        compiler_params=_params("arbitrary"),
        name="inproj",
    )(x, w_norm.reshape(1, d), w_in_bf16,
      jnp.tile(q_gain, aw // HEAD_DIM).reshape(1, aw), jnp.tile(k_gain, kvw // HEAD_DIM).reshape(1, kvw),
      _block_diag_mean(aw, HEAD_DIM), _block_diag_mean(kvw, HEAD_DIM))


def _hgrn_consts(L):
    t = np.arange(L)
    u = t[None, :]
    blocks = [u <= t[:, None], u > t[:, None]]
    levels = []
    m = L // 2
    while m >= 1:
        levels.append(m)
        m //= 2
    lvl = np.full((L, L), -1, np.int32)
    lvl[t, t] = len(levels)
    isq_cols = []
    for j, m in enumerate(levels):
        bnd = (t // (2 * m)) * (2 * m) + m - 1
        isq = (t % (2 * m)) >= m
        cq = isq[:, None] & (u > bnd[:, None]) & (u <= t[:, None])
        ck = (~isq)[:, None] & (u > t[:, None]) & (u <= bnd[:, None])
        blocks.append(cq | ck)
        same = (t[:, None] // (2 * m)) == (t[None, :] // (2 * m))
        lvl[same & isq[:, None] & (~isq)[None, :]] = j
        isq_cols.append(isq)
    c = np.concatenate(blocks, axis=0).astype(np.float32)
    isq = np.stack(isq_cols, axis=1).astype(np.float32)
    isq = np.pad(isq, ((0, 0), (0, LANES - isq.shape[1])))
    c2 = np.tile(c, (1, 2))
    return jnp.asarray(c2, dtype=BF16), jnp.asarray(np.tile(lvl, (1, 2))), jnp.asarray(isq), len(levels)


def _hgrn_kernel(L, nlev, heads, dk, qr_ref, z_ref, hv_ref, hg_ref, lb_ref, og_ref, c_ref, lvl_ref, isq_ref,
                 s0_ref, y_ref, sout_ref, st_ref):
    c = pl.program_id(1)

    @pl.when(c == 0)
    def _():
        for h in range(heads):
            st_ref[h] = s0_ref[0, h].T

    for cc in range(z_ref.shape[1] // L):
        _hgrn_chunk(L, nlev, heads, dk, slice(cc * L, (cc + 1) * L), qr_ref, z_ref, hv_ref, hg_ref, lb_ref, og_ref,
                    c_ref, lvl_ref, isq_ref, y_ref, st_ref)

    @pl.when(c == pl.num_programs(1) - 1)
    def _():
        for h in range(heads):
            sout_ref[0, h] = st_ref[h].T


def _hgrn_chunk(L, nlev, heads, dk, rows, qr_ref, z_ref, hv_ref, hg_ref, lb_ref, og_ref, c_ref, lvl_ref, isq_ref,
                y_ref, st_ref):
    z = z_ref[0, rows, :]
    lb = lb_ref[...]
    e = jnp.exp(-jnp.abs(z))
    r = 1.0 / (1.0 + e)
    pos = z >= 0
    sig = jnp.where(pos, r, e * r)
    sig_neg = jnp.where(pos, e * r, r)
    logf = jnp.log(lb + (1.0 - lb) * sig)
    kin = (1.0 - lb) * sig_neg
    q = qr_ref[0, rows, :].astype(F32)

    hi, mid, _ = _split3(logf * LOG2E)
    ex = jnp.exp2(_dot(c_ref[...], jnp.concatenate([hi, mid], axis=0)))
    e_b = ex[0:L]
    e_rev = ex[L:2 * L]

    q_in = (q * e_b).astype(BF16)
    k_out = (kin * e_rev).astype(BF16)
    q_b = q.astype(BF16)
    k_b = kin.astype(BF16)
    xs = []
    for j in range(nlev):
        m = L >> (j + 1)
        if m % SUBLANES == 0:
            qk = jnp.concatenate([(q if blk % 2 else kin)[blk * m:(blk + 1) * m] for blk in range(L // m)], axis=0)
        else:
            qk = jnp.where(isq_ref[:, j:j + 1] > 0.5, q, kin)
        xs.append((qk * ex[(2 + j) * L:(3 + j) * L]).astype(BF16))
    lvl = lvl_ref[...]
    v = hv_ref[0, rows, :]
    g = hg_ref[0, rows, :].astype(F32)
    og = og_ref[...]

    def block_diag(x):
        zero = jnp.zeros((x.shape[0], dk), x.dtype)
        return jnp.concatenate([jnp.concatenate([x[:, :dk], zero], axis=1),
                                jnp.concatenate([zero, x[:, dk:]], axis=1)], axis=0)

    for pair in range(heads // 2):
        sl = slice(2 * pair * dk, 2 * (pair + 1) * dk)
        a = jnp.where(lvl == nlev, _dot_nt(q_b[:, sl], block_diag(k_b[:, sl])), 0.0)
        for j in range(nlev):
            xp = xs[j][:, sl]
            a = jnp.where(lvl == j, _dot_nt(xp, block_diag(xp)), a)
        st = jnp.concatenate([st_ref[2 * pair], st_ref[2 * pair + 1]], axis=1)
        vp = v[:, sl]
        o = _dot(a.astype(BF16), block_diag(vp)) + _dot_nt(q_in[:, sl], block_diag(st.astype(BF16)))
        for half in range(2):
            h = 2 * pair + half
            hs = slice(h * dk, (h + 1) * dk)
            st_ref[h] = st_ref[h] * e_b[L - 1:L, hs] + _dot_tn(v[:, hs], k_out[:, hs])
            oh = o[:, half * dk:(half + 1) * dk]
            ms = jnp.mean(oh * oh, axis=-1, keepdims=True)
            gh = g[:, hs]
            y_ref[0, rows, hs] = (oh * lax.rsqrt(ms + EPS) * og[:, hs] * (gh * _sigmoid(gh))).astype(BF16)


def _hgrn(qr, z, hv, hg, lb, out_gain, s0, L):
    b, s, w = z.shape
    heads, dk = s0.shape[1], s0.shape[2]
    cm, lvl, isq, nlev = _hgrn_consts(L)
    per_step = HGRN_CHUNKS_PER_STEP if s % (HGRN_CHUNKS_PER_STEP * L) == 0 else 1
    seq = pl.BlockSpec((1, per_step * L, w), lambda i, c: (i, c, 0))
    const = lambda a: pl.BlockSpec(a.shape, lambda i, c: (0,) * a.ndim)
    state = pl.BlockSpec((1, heads, dk, dk), lambda i, c: (i, 0, 0, 0))
    lb2 = lb.reshape(1, w)
    og2 = jnp.tile(out_gain, heads).reshape(1, w)
    return pl.pallas_call(
        functools.partial(_hgrn_kernel, L, nlev, heads, dk),
        grid=(b, s // (per_step * L)),
        in_specs=[seq, seq, seq, seq, const(lb2), const(og2), const(cm), const(lvl), const(isq), state],
        out_specs=[seq, state],
        out_shape=[jax.ShapeDtypeStruct((b, s, w), BF16), jax.ShapeDtypeStruct(s0.shape, F32)],
        scratch_shapes=[pltpu.VMEM((heads, dk, dk), F32)],
        compiler_params=_params("arbitrary", "arbitrary"),
        name=f"hgrn_scan_{L}",
    )(qr, z, hv, hg, lb2, og2, cm, lvl, isq, s0)


def _t5_bucket_np(rel):
    half = NUM_BUCKETS // 2
    max_exact = half // 2
    assert (NUM_BUCKETS, MAX_DISTANCE) == (32, 128)
    n = np.abs(rel).astype(np.int64)
    nn = np.maximum(n, 1)
    k = np.zeros_like(nn)
    for j in range(1, 48):
        k = np.where(64 * (1 << j) <= nn * nn, j, k)
    large = np.minimum(max_exact + k, half - 1)
    return np.where(rel > 0, half, 0) + np.where(n < max_exact, n, large)


HEADS_PER_COL = LANES // HEAD_DIM
COLS_PER_GROUP = GQA_GROUP // HEADS_PER_COL
HEAD_ORDER = tuple(g * GQA_GROUP + col * HEADS_PER_COL + half
                   for g in range(KV_HEADS) for half in range(HEADS_PER_COL) for col in range(COLS_PER_GROUP))
KV_EXPAND = KV_HEADS * HEADS_PER_COL


def _expand_kv(x):
    assert HEADS_PER_COL == 2 and KV_HEADS == 2 and x.shape[1] == LANES
    low = lax.broadcasted_iota(I32, x.shape, 1) < HEAD_DIM
    xr = pltpu.roll(x, HEAD_DIM, axis=1)
    zero = jnp.zeros_like(x)
    blocks = [jnp.where(low, x, zero), jnp.where(low, zero, xr), jnp.where(low, xr, zero), jnp.where(low, zero, x)]
    return jnp.concatenate(blocks, axis=1).astype(BF16)


def _attn_core(q, kx, vx, bias, sink):
    tq = q.shape[0]
    scores = []
    for g in range(KV_HEADS):
        cols = [q[:, (g * COLS_PER_GROUP + c) * LANES:(g * COLS_PER_GROUP + c + 1) * LANES]
                for c in range(COLS_PER_GROUP)]
        qst = jnp.concatenate(cols, axis=0)
        for half in range(HEADS_PER_COL):
            blk = g * HEADS_PER_COL + half
            scores.append(_dot_nt(qst, kx[:, blk * LANES:(blk + 1) * LANES]))
    s = jnp.concatenate(scores, axis=0) + bias
    m =jnp.maximum(jnp.max(s, axis=-1, keepdims=True), sink)
    p = jnp.exp2(s - m)
    den = jnp.sum(p, axis=-1, keepdims=True) + jnp.exp2(sink - m)
    pb = p.astype(BF16)
    rden = 1.0 / den
    rows = COLS_PER_GROUP * tq
    outs = []
    for g in range(KV_HEADS):
        o = None
        for half in range(HEADS_PER_COL):
            blk = g * HEADS_PER_COL + half
            rs = slice(blk * rows, (blk + 1) * rows)
            part = _dot(pb[rs], vx[:, blk * LANES:(blk + 1) * LANES]) * rden[rs]
            o = part if o is None else o + part
        outs.extend(o[c * tq:(c + 1) * tq] for c in range(COLS_PER_GROUP))
    return outs


def _bias_rows(table, bucket):
    onehot = (jnp.asarray(bucket)[..., None] == jnp.arange(NUM_BUCKETS)).astype(F32)
    cols = jnp.stack([table[:, h] for h in HEAD_ORDER], axis=1)
    bias = jnp.einsum('...qkb,bh->...hqk', onehot, cols, precision=lax.Precision.HIGHEST)
    return bias.reshape(*bucket.shape[:-2], ATTN_HEADS * bucket.shape[-2], bucket.shape[-1]) * LOG2E


def _pad_keys(bias):
    tk = bias.shape[-1]
    pad = [(0, 0)] * (bias.ndim - 1) + [(0, -tk % LANES)]
    return jnp.pad(bias, pad, constant_values=-jnp.inf)


def _sink_rows(sinks, tq):
    rows = jnp.repeat(jnp.stack([sinks[h] for h in HEAD_ORDER]).astype(F32), tq)
    return rows.reshape(ATTN_HEADS * tq, 1) * LOG2E


def _attn_prompt_kernel(cb, q_ref, k_ref, v_ref, km_ref, vm_ref, bias_ref, sink_ref, o_ref, kx_ref, vx_ref):
    step = pl.program_id(1)
    s_len = k_ref.shape[1]
    meta_at = WINDOW + s_len

    @pl.when(step == 0)
    def _():
        piece = min(s_len, 512)
        for src, meta, dst in ((k_ref, km_ref, kx_ref), (v_ref, vm_ref, vx_ref)):
            dst[0:WINDOW] = jnp.zeros((WINDOW, dst.shape[1]), BF16)
            for r in range(0, s_len, piece):
                dst[WINDOW + r:WINDOW + r + piece] = _expand_kv(src[0, r:r + piece, :])
            dst[meta_at:meta_at + N_META] = _expand_kv(meta[...])
            dst[meta_at + N_META:] = jnp.zeros((dst.shape[0] - meta_at - N_META, dst.shape[1]), BF16)

    win = WINDOW + CHUNK
    tail = kx_ref.shape[0] - meta_at
    for j in range(cb):
        c = step * cb + j
        start = pl.multiple_of(c * CHUNK, CHUNK)
        kall = jnp.concatenate([kx_ref[pl.ds(start, win), :], kx_ref[meta_at:meta_at + tail, :]], axis=0)
        vall = jnp.concatenate([vx_ref[pl.ds(start, win), :], vx_ref[meta_at:meta_at + tail, :]], axis=0)
        rows = slice(j * CHUNK, (j + 1) * CHUNK)
        outs = _attn_core(q_ref[0, rows, :], kall, vall, bias_ref[jnp.minimum(c, bias_ref.shape[0] - 1)],
                          sink_ref[...])
        for ci, o in enumerate(outs):
            o_ref[0, rows, ci * LANES:(ci + 1) * LANES] = o.astype(o_ref.dtype)


def _attn_prompt(q, k, v, k_meta, v_meta, table, sinks):
    b, s, aw = q.shape
    kvw = k.shape[-1]
    nc = s // CHUNK
    cb = ATTN_CHUNKS_PER_STEP if nc % ATTN_CHUNKS_PER_STEP == 0 else 1
    assert s % min(s, 512) == 0
    n_bias = 1
    while True:
        qpos = N_META + (n_bias - 1) * CHUNK
        if np.all(_t5_bucket_np(np.arange(N_META) - qpos) == _t5_bucket_np(np.arange(N_META) - qpos - 10 ** 6)):
            break
        n_bias += 1
    n_bias = min(max(n_bias, WINDOW_CHUNKS + 1), nc)
    cs = np.arange(n_bias)[:, None]
    qpos = N_META + cs * CHUNK + np.arange(CHUNK)[None]
    wpos = N_META + (cs - WINDOW_CHUNKS) * CHUNK + np.arange(WINDOW + CHUNK)[None]
    kpos = np.concatenate([wpos, np.broadcast_to(np.arange(N_META), (n_bias, N_META))], axis=1)
    valid = np.concatenate([wpos >= N_META, np.ones((n_bias, N_META), bool)], axis=1)
    bias = _bias_rows(table, _t5_bucket_np(kpos[:, None, :] - qpos[:, :, None]))
    bias = _pad_keys(jnp.where(valid[:, None, :], bias, -jnp.inf))
    sink = _sink_rows(sinks, CHUNK)
    qs = pl.BlockSpec((1, cb * CHUNK, aw), lambda i, c: (i, c, 0))
    kv = pl.BlockSpec((1, s, kvw), lambda i, c: (i, 0, 0))
    meta = pl.BlockSpec((N_META, kvw), lambda i, c: (0, 0))
    xrows = WINDOW + s + bias.shape[-1] - (WINDOW + CHUNK)
    return pl.pallas_call(
        functools.partial(_attn_prompt_kernel, cb),
        grid=(b, nc // cb),
        in_specs=[qs, kv, kv, meta, meta, pl.BlockSpec(bias.shape, lambda i, c: (0, 0, 0)),
                  pl.BlockSpec(sink.shape, lambda i, c: (0, 0))],
        out_specs=qs,
        out_shape=jax.ShapeDtypeStruct((b, s, aw), BF16),
        scratch_shapes=[pltpu.VMEM((xrows, KV_EXPAND * LANES), BF16), pltpu.VMEM((xrows, KV_EXPAND * LANES), BF16)],
        compiler_params=_params("arbitrary", "arbitrary"),
        name="attn_prompt",
    )(q, k, v, k_meta, v_meta, bias, sink)


def _attn_sample_kernel(q_ref, kc_ref, vc_ref, kn_ref, vn_ref, km_ref, vm_ref, bias_ref, sink_ref, o_ref):
    tk = kc_ref.shape[1] + kn_ref.shape[1] + km_ref.shape[0]
    zeros = jnp.zeros((bias_ref.shape[1] - tk, km_ref.shape[1]), F32)
    kall = _expand_kv(jnp.concatenate([kc_ref[0], kn_ref[0], km_ref[...], zeros], axis=0))
    vall = _expand_kv(jnp.concatenate([vc_ref[0], vn_ref[0], vm_ref[...], zeros], axis=0))
    outs = _attn_core(q_ref[0], kall, vall, bias_ref[...], sink_ref[...])
    for ci, o in enumerate(outs):
        o_ref[0, :, ci * LANES:(ci + 1) * LANES] = o.astype(o_ref.dtype)


def _attn_sample(q, k_cache, v_cache, k_new, v_new, k_meta, v_meta, table, sinks):
    bd, sd, aw = q.shape
    kvw = k_new.shape[-1]
    win = k_cache.shape[1]
    qpos = N_META + PAST_LEN + np.arange(sd)
    kpos = np.concatenate([N_META + PAST_LEN - win + np.arange(win), qpos, np.arange(N_META)])
    bias = _pad_keys(_bias_rows(table, _t5_bucket_np(kpos[None, :] - qpos[:, None])))
    sink = _sink_rows(sinks, sd)
    per = lambda n, w: pl.BlockSpec((1, n, w), lambda i: (i, 0, 0))
    meta = pl.BlockSpec((N_META, kvw), lambda i: (0, 0))
    return pl.pallas_call(
        _attn_sample_kernel,
        grid=(bd,),
        in_specs=[per(sd, aw), per(win, kvw), per(win, kvw), per(sd, kvw), per(sd, kvw), meta, meta,
                  pl.BlockSpec(bias.shape, lambda i: (0, 0)), pl.BlockSpec(sink.shape, lambda i: (0, 0))],
        out_specs=per(sd, aw),
        out_shape=jax.ShapeDtypeStruct((bd, sd, aw), BF16),
        compiler_params=_params("arbitrary"),
        name="attn_sample",
    )(q, k_cache, v_cache, k_new, v_new, k_meta, v_meta, bias, sink)


ROUTE_E1, ROUTE_E2, ROUTE_W1, ROUTE_W2, ROUTE_R1, ROUTE_R2 = range(6)


def _merge_kernel(hw, x_ref, yr_ref, at_ref, sga_ref, sgb_ref, wb_ref, wo_ref, wn_ref, wr_ref, br_ref, cnt0_ref, tri_ref,
                  h_ref, xn_ref, route_ref, cnt_ref, carry_ref):
    i = pl.program_id(0)

    @pl.when(i == 0)
    def _():
        carry_ref[...] = cnt0_ref[...]

    n = x_ref.shape[0]
    sub = MERGE_SUBTILE if n % MERGE_SUBTILE == 0 else n
    for r0 in range(0, n, sub):
        _merge_rows(hw, r0, sub, x_ref, yr_ref, at_ref, sga_ref, sgb_ref, wb_ref, wo_ref, wn_ref, wr_ref, br_ref,
                    tri_ref, h_ref, xn_ref, route_ref, carry_ref)
    cnt_ref[...] = carry_ref[...]


def _merge_rows(hw, r0, tm, x_ref, yr_ref, at_ref, sga_ref, sgb_ref, wb_ref, wo_ref, wn_ref, wr_ref, br_ref, tri_ref,
                h_ref, xn_ref, route_ref, carry_ref):
    rows = slice(r0, r0 + tm)
    br = _dot(yr_ref[rows, :], wb_ref[0:hw, :])
    ba = _dot(at_ref[rows, :], wb_ref[hw:, :])
    merged = sga_ref[rows, :].astype(F32) * br + sgb_ref[rows, :].astype(F32) * ba
    h = x_ref[rows, :] + _dot(merged.astype(BF16), wo_ref[...])
    h_ref[rows, :] = h
    ms = jnp.mean(h * h, axis=-1, keepdims=True)
    xn = h * lax.rsqrt(ms + EPS) * wn_ref[...]
    per_row = xn.shape[1] // LANES
    _store_row_tiles(xn_ref.at[r0 * per_row:(r0 + tm) * per_row], xn)

    x_hi = xn.astype(BF16)
    x_lo = (xn - x_hi.astype(F32)).astype(BF16)
    both = _dot(x_hi, wr_ref[...])
    logits = both[:, :LANES] + both[:, LANES:] + _dot(x_lo, wr_ref[:, :LANES]) + br_ref[...]

    lane_i =lax.broadcasted_iota(I32, (tm, LANES), 1)
    lane = lane_i.astype(F32)
    group_of_lane = (lane_i >> int(math.log2(EXPERTS_PER_GROUP))).astype(F32)
    ninf = -jnp.inf
    first = lambda hit, idx: jnp.min(jnp.where(hit, idx, float(LANES)), axis=-1, keepdims=True)
    gmask = (lane_i >= N_EXPERTS) & (lane_i < N_EXPERTS + N_GROUPS)
    gl = jnp.where(gmask, logits, ninf)
    gmax = jnp.max(gl, axis=-1, keepdims=True)
    gidx = first(gl == gmax, lane - N_EXPERTS)
    gval = 1.0 / jnp.sum(jnp.exp(gl - gmax), axis=-1, keepdims=True)
    emask = (lane_i < N_EXPERTS) & (group_of_lane == gidx)
    el = jnp.where(emask, logits, ninf)
    m1 = jnp.max(el, axis=-1, keepdims=True)
    i1 = first(el == m1, lane)
    el2 = jnp.where(lane == i1, ninf, el)
    m2 = jnp.max(el2, axis=-1, keepdims=True)
    i2 = first(el2 == m2, lane)
    e21 = jnp.exp(m2 - m1)
    w1 = gval / (1.0 + e21)
    w2 = gval * e21 / (1.0 + e21)

    sel1 = lane == i1
    sel2 = lane == i2
    oh = (sel1 | sel2).astype(BF16)
    before = _dot(tri_ref[...], oh) + carry_ref[...]
    r1 = jnp.sum(jnp.where(sel1, before, 0.0), axis=-1, keepdims=True)
    r2 = jnp.sum(jnp.where(sel2, before, 0.0), axis=-1, keepdims=True)
    carry_ref[...] = carry_ref[...] + jnp.sum(oh.astype(F32), axis=0, keepdims=True)

    rec = jnp.zeros((tm, LANES), F32)
    for slot, val in ((ROUTE_E1, i1), (ROUTE_E2, i2), (ROUTE_W1, w1), (ROUTE_W2, w2), (ROUTE_R1, r1), (ROUTE_R2, r2)):
        rec = jnp.where(lane_i == slot, val, rec)
    route_ref[rows, :] = rec


def _merge(x, y_rec, att, sga, sgb, wb, wo, w_norm, w_router, b_router, cnt0):
    t, d = x.shape
    hw = y_rec.shape[1]
    tm = PROJ_TILE if t % PROJ_TILE == 0 else t
    sub = MERGE_SUBTILE if tm % MERGE_SUBTILE == 0 else tm
    row = lambda w: pl.BlockSpec((tm, w), lambda i: (i, 0))
    const = lambda a, b: pl.BlockSpec((a, b), lambda i: (0, 0))
    return pl.pallas_call(
        functools.partial(_merge_kernel, hw),
        grid=(t // tm,),
        in_specs=[row(d), row(hw), row(att.shape[1]), row(d), row(d), const(*wb.shape), const(d, d), const(1, d),
                  const(d, 2 * LANES), const(1, LANES), const(1, LANES), const(sub, sub)],
        out_specs=[row(d), pl.BlockSpec((tm * d // LANES, LANES), lambda i: (i, 0)), row(LANES), const(1, LANES)],
        out_shape=[jax.ShapeDtypeStruct((t, d), F32), jax.ShapeDtypeStruct((t * d // LANES, LANES), F32),
                   jax.ShapeDtypeStruct((t, LANES), F32), jax.ShapeDtypeStruct((1, LANES), F32)],
        scratch_shapes=[pltpu.VMEM((1, LANES), F32)],
        compiler_params=_params("arbitrary"),
        name="merge_route",
    )(x, y_rec, att, sga, sgb, wb, wo, w_norm.reshape(1, d), w_router, b_router, cnt0,
      jnp.asarray(np.tri(sub, k=-1), dtype=BF16))


def _row_copy(src, dst, sem):
    return pltpu.make_async_copy(src, dst, sem)


ROW_UNROLL = 8


def _tile_rows(ref, row, sub):
    return ref.at[pl.ds(pl.multiple_of(row * sub, sub), sub)]


def _scatter_kernel(n_tok, sub, first, dest_ref, pstart_ref, pend_ref, x_ref, *rest):
    xs_ref, zero_ref, sem, zsem = rest[-4:]

    if first:
        @pl.when(pl.program_id(0) == 0)
        def _():
            zero_ref[...] = jnp.zeros_like(zero_ref)
            n_rows = xs_ref.shape[0] // sub
            used = pend_ref[N_EXPERTS - 1]
            blocks = [(pend_ref[e] > pstart_ref[e], pend_ref[e] - EXPERT_TILE) for e in range(N_EXPERTS)]
            blocks += [(n_rows - (j + 1) * EXPERT_TILE >= used, n_rows - (j + 1) * EXPERT_TILE)
                       for j in range(N_EXPERTS)]
            for wait in (False, True):
                for cond, row in blocks:
                    @pl.when(cond)
                    def _():
                        at = row * sub if isinstance(row, int) else pl.multiple_of(row * sub, EXPERT_TILE * sub)
                        cp = _row_copy(zero_ref, xs_ref.at[pl.ds(at, EXPERT_TILE * sub)], zsem)
                        cp.wait() if wait else cp.start()

    def issue(grp, _):
        for u in range(ROW_UNROLL):
            r = grp * ROW_UNROLL + u
            for k in range(TOP_K):
                _row_copy(_tile_rows(x_ref, r, sub), _tile_rows(xs_ref, dest_ref[0, 0, TOP_K * r + k], sub),
                          sem).start(priority=(u * TOP_K + k) % 2)
        return 0

    lax.fori_loop(0, n_tok // ROW_UNROLL, issue, 0)
    for k in range(TOP_K):
        _row_copy(x_ref, xs_ref.at[pl.ds(0, n_tok * sub)], sem).wait()


def _scatter_rows(x, dest, pstarts, pends, xs, rows):
    n, _, per = dest.shape
    tm = per // TOP_K
    sub = x.shape[0] // (n * tm)
    first = xs is None
    smem = pl.BlockSpec(memory_space=pltpu.SMEM)
    return pl.pallas_call(
        functools.partial(_scatter_kernel, tm, sub, first),
        grid=(n,),
        in_specs=[pl.BlockSpec((1, 1, per), lambda i: (i, 0, 0), memory_space=pltpu.SMEM), smem, smem,
                  pl.BlockSpec((tm * sub, LANES), lambda i: (i, 0))]
        + ([] if first else [pl.BlockSpec(memory_space=pl.ANY)]),
        out_specs=pl.BlockSpec(memory_space=pl.ANY),
        out_shape=jax.ShapeDtypeStruct((rows * sub, LANES), F32),
        scratch_shapes=[pltpu.VMEM((EXPERT_TILE * sub, LANES), F32), pltpu.SemaphoreType.DMA(()),
                        pltpu.SemaphoreType.DMA(())],
        input_output_aliases={} if first else {4: 0},
        compiler_params=_params("arbitrary"),
        name="moe_scatter",
    )(dest, pstarts, pends, x, *([] if first else [xs]))


def _expert_kernel(tm, sub, be_ref, nb_ref, x_ref, wg_ref, wu_ref, wd_ref, y_ref, wgb_ref, wub_ref, wdb_ref):
    i = pl.program_id(0)
    used = i < nb_ref[0]

    @pl.when(used & ((i == 0) | (be_ref[i] != be_ref[jnp.maximum(i - 1, 0)])))
    def _():
        wgb_ref[...] = wg_ref[0].astype(BF16)
        wub_ref[...] = wu_ref[0].astype(BF16)
        wdb_ref[...] = wd_ref[0].astype(BF16)

    @pl.when(used)
    def _():
        x = _load_row_tiles(x_ref, tm, sub).astype(BF16)
        g = _dot(x, wgb_ref[...])
        u = _dot(x, wub_ref[...])
        hmid = (g * _sigmoid(g) * u).astype(BF16)
        _store_row_tiles(y_ref, _dot(hmid, wdb_ref[...]))

    @pl.when(jnp.logical_not(used))
    def _():
        y_ref[...] = jnp.zeros_like(y_ref)


def _experts(xs, block_e, n_used, wg, wu, wd):
    _, d, ff = wg.shape
    sub = d // LANES
    tm = EXPERT_TILE
    blk = pl.BlockSpec((tm * sub, LANES), lambda i, be, nb: (i, 0))
    grid_spec = pltpu.PrefetchScalarGridSpec(
        num_scalar_prefetch=2,
        grid=(xs.shape[0] // (tm * sub),),
        in_specs=[blk,
                  pl.BlockSpec((1, d, ff), lambda i, be, nb: (be[i], 0, 0)),
                  pl.BlockSpec((1, d, ff), lambda i, be, nb: (be[i], 0, 0)),
                  pl.BlockSpec((1, ff, d), lambda i, be, nb: (be[i], 0, 0))],
        out_specs=blk,
        scratch_shapes=[pltpu.VMEM((d, ff), BF16), pltpu.VMEM((d, ff), BF16), pltpu.VMEM((ff, d), BF16)],
    )
    return pl.pallas_call(
        functools.partial(_expert_kernel, tm, sub),
        grid_spec=grid_spec,
        out_shape=jax.ShapeDtypeStruct(xs.shape, F32),
        compiler_params=_params("arbitrary"),
        name="moe_experts",
    )(block_e, n_used, xs, wg, wu, wd)


def _combine_kernel(n_tok, sub, dcur_ref, dnext_ref, h_ref, route_ref, ys_ref, o_ref, buf, sem):
    i = pl.program_id(0)
    n = pl.num_programs(0)
    slot = i % 2

    def start(dest_ref, s):
        def issue(grp, _):
            for u in range(ROW_UNROLL):
                r = grp * ROW_UNROLL + u
                for k in range(TOP_K):
                    _row_copy(_tile_rows(ys_ref, dest_ref[0, 0, TOP_K * r + k], sub), _tile_rows(buf.at[s, k], r, sub),
                              sem.at[s]).start(priority=(u * TOP_K + k) % 2)
            return 0
        lax.fori_loop(0, n_tok // ROW_UNROLL, issue, 0)

    @pl.when(i == 0)
    def _():
        start(dcur_ref, 0)

    @pl.when(i + 1 < n)
    def _():
        start(dnext_ref, 1 - slot)

    for k in range(TOP_K):
        _row_copy(ys_ref.at[pl.ds(0, n_tok * sub)], buf.at[slot, k], sem.at[slot]).wait()
    route = route_ref[...]
    w1 = route[:, ROUTE_W1:ROUTE_W1 + 1]
    w2 = route[:, ROUTE_W2:ROUTE_W2 + 1]
    for j in range(sub):
        cols = slice(j * LANES, (j + 1) * LANES)
        part = lambda k: buf[slot, k, pl.ds(j, n_tok, stride=sub), :]
        o_ref[:, cols] = h_ref[:, cols] + (part(0) * w1 + part(1) * w2)


def _combine(h, route, dest, ys):
    t, d = h.shape
    n, _, per = dest.shape
    tm = t // n
    sub = d // LANES
    dspec = lambda f: pl.BlockSpec((1, 1, per), f, memory_space=pltpu.SMEM)
    return pl.pallas_call(
        functools.partial(_combine_kernel, tm, sub),
        grid=(n,),
        in_specs=[dspec(lambda i: (i, 0, 0)), dspec(lambda i: (jnp.minimum(i + 1, n - 1), 0, 0)),
                  pl.BlockSpec((tm, d), lambda i: (i, 0)), pl.BlockSpec((tm, LANES), lambda i: (i, 0)),
                  pl.BlockSpec(memory_space=pl.ANY)],
        out_specs=pl.BlockSpec((tm, d), lambda i: (i, 0)),
        out_shape=jax.ShapeDtypeStruct((t, d), F32),
        scratch_shapes=[pltpu.VMEM((2, TOP_K, tm * sub, LANES), F32), pltpu.SemaphoreType.DMA((2,))],
        compiler_params=_params("arbitrary"),
        name="moe_combine",
    )(dest, dest, h, route, ys)


def _moe(parts, counts, wg, wu, wd):
    tm = EXPERT_TILE
    n_assign = sum(part[0].shape[0] for part in parts) * TOP_K
    n_blocks = -(-(n_assign + N_EXPERTS * (tm - 1)) // tm)
    counts = counts[0, :N_EXPERTS].astype(I32)
    pcounts = (counts + tm - 1) // tm * tm
    pends = jnp.cumsum(pcounts)
    pstarts = pends - pcounts
    block_start = jnp.arange(n_blocks, dtype=I32) * tm
    block_e = jnp.minimum(jnp.sum((pends[None, :] <= block_start[:, None]).astype(I32), axis=1), N_EXPERTS - 1)
    n_used = pends[-1:] // tm
    xs = None
    dests = []
    for h, xn, route in parts:
        t = h.shape[0]
        mt = MOVE_TILE if t % MOVE_TILE == 0 else t
        assert mt % ROW_UNROLL == 0
        e = route[:, ROUTE_E1:ROUTE_E2 + 1].astype(I32)
        rank = route[:, ROUTE_R1:ROUTE_R2 + 1].astype(I32)
        seg = jnp.sum(jnp.where(e[..., None] == jnp.arange(N_EXPERTS, dtype=I32), pstarts, 0), axis=-1)
        dests.append((seg + rank).reshape(t // mt, 1, mt * TOP_K))
        xs = _scatter_rows(xn, dests[-1], pstarts, pends, xs, n_blocks * tm)
    ys = _experts(xs, block_e, n_used, wg, wu, wd)
    return [_combine(h, route, dest, ys) for (h, _, route), dest in zip(parts, dests)]


def kernel(x_prompt, x_sample, cache_swa_k, cache_swa_v, state_hgrn, meta_tokens, rel_bias_table, hgrn_lower_bounds, w_norm_mix, w_in, hgrn_out_norm, q_norm, k_norm, attn_sinks, w_branch, w_out, w_norm_ffn, w_router_group, b_router_group, w_router_expert, b_router_expert, w_expert_gate, w_expert_up, w_expert_down):
    b, s, d = x_prompt.shape
    bd, sd, _ = x_sample.shape
    depth, _, heads, dk, dv = state_hgrn.shape
    assert depth == 1 and heads == HGRN_HEADS and dk == dv
    hw = heads * dk
    aw = ATTN_HEADS * HEAD_DIM
    kvw = KV_HEADS * HEAD_DIM
    assert w_in.shape[-1] == 4 * hw + aw + 2 * kvw + 2 * d
    assert s % HGRN_CHUNK == 0 and s % CHUNK == 0 and sd == N_META and N_EXPERTS + N_GROUPS <= LANES
    l = 0

    p = jax.nn.softmax(hgrn_lower_bounds.astype(F32), axis=0)
    lb = jnp.cumsum(p, axis=0)[l + 1] - p[0]

    w_in_b = w_in[l].astype(BF16)
    proj = functools.partial(_inproj, w_norm=w_norm_mix[l], w_in_bf16=w_in_b, q_gain=q_norm[l], k_gain=k_norm[l],
                             hw=hw, aw=aw, kvw=kvw)
    x_small = jnp.concatenate([x_sample.reshape(bd * sd, d), meta_tokens.astype(F32)], axis=0)
    qr_s, z_s, hv_s, hg_s, qa_s, k_s, v_s, sga_s, sgb_s = proj(x_small)
    qr_p, z_p, hv_p, hg_p, qa_p, k_p, v_p, sga_p, sgb_p = proj(x_prompt.reshape(b * s, d))
    ns = bd * sd
    k_meta, v_meta = k_s[ns:], v_s[ns:]

    streams = lambda a, n: a.reshape(n, -1, a.shape[-1])
    s0_small = jnp.concatenate([state_hgrn[l].astype(F32), jnp.zeros((1, heads, dk, dv), F32)], axis=0)
    y_small, st_small = _hgrn(streams(qr_s, bd + 1), streams(z_s, bd + 1), streams(hv_s, bd + 1),
                              streams(hg_s, bd + 1), lb, hgrn_out_norm[l], s0_small, sd)
    s0_p = jnp.broadcast_to(st_small[bd:], (b, heads, dk, dv))
    y_p, st_p = _hgrn(streams(qr_p, b), streams(z_p, b), streams(hv_p, b), streams(hg_p, b), lb,
                      hgrn_out_norm[l], s0_p, HGRN_CHUNK)

    table = rel_bias_table.astype(F32)
    att_p = _attn_prompt(streams(qa_p, b), streams(k_p, b), streams(v_p, b), k_meta, v_meta, table, attn_sinks[l])
    kc = cache_swa_k[l].astype(F32).reshape(bd, -1, kvw)
    vc = cache_swa_v[l].astype(F32).reshape(bd, -1, kvw)
    k_new, v_new = k_s[:ns].reshape(bd, sd, kvw), v_s[:ns].reshape(bd, sd, kvw)
    att_s = _attn_sample(qa_s[:ns].reshape(bd, sd, aw), kc, vc, k_new, v_new, k_meta, v_meta, table, attn_sinks[l])

    wb = w_branch[l].astype(BF16)
    wo = w_out[l].astype(BF16)
    w_router = jnp.pad(jnp.concatenate([w_router_expert[l], w_router_group[l]], axis=1).astype(F32),
                       ((0, 0), (0, LANES - N_EXPERTS - N_GROUPS)))
    b_router = jnp.pad(jnp.concatenate([b_router_expert[l], b_router_group[l]]).astype(F32),
                       (0, LANES - N_EXPERTS - N_GROUPS)).reshape(1, LANES)
    w_router_hi = w_router.astype(BF16)
    w_router_lo = (w_router - w_router_hi.astype(F32)).astype(BF16)
    w_router = jnp.concatenate([w_router_hi, w_router_lo], axis=1)
    merge = functools.partial(_merge, wb=wb, wo=wo, w_norm=w_norm_ffn[l], w_router=w_router, b_router=b_router)
    h_p, xn_p, route_p, cnt_p = merge(x_prompt.reshape(b * s, d), y_p.reshape(b * s, hw), att_p.reshape(b * s, aw),
                                      sga_p, sgb_p, cnt0=jnp.zeros((1, LANES), F32))
    h_s, xn_s, route_s, cnt_s = merge(x_sample.reshape(ns, d), y_small[:bd].reshape(ns, hw), att_s.reshape(ns, aw),
                                      sga_s[:ns], sgb_s[:ns], cnt0=cnt_p)

    out_p, out_s = _moe([(h_p, xn_p, route_p), (h_s, xn_s, route_s)], cnt_s,
                        w_expert_gate[l], w_expert_up[l], w_expert_down[l])

    tail = lambda a: a.reshape(b, s, KV_HEADS, HEAD_DIM)[:, s - WINDOW:][None]
    roll = lambda cache, new: jnp.concatenate([cache, new], axis=1)[:, -WINDOW:].reshape(
        bd, WINDOW, KV_HEADS, HEAD_DIM)[None]
    return (out_p.reshape(b, s, d), out_s.reshape(bd, sd, d),
            tail(k_p), tail(v_p), st_p[None],
            roll(kc, k_new), roll(vc, v_new), st_small[:bd][None])
```

```python
import functools
import math

import numpy as np
import jax
import jax.numpy as jnp
from jax import lax
from jax.experimental import pallas as pl
from jax.experimental.pallas import tpu as pltpu

F32 = jnp.float32
BF16 = jnp.bfloat16
I32 = jnp.int32

CHUNK = 64
N_META = 16
PAST_LEN = 2048
EPS = 1e-6
HGRN_HEADS = 4
ATTN_HEADS = 8
KV_HEADS = 2
HEAD_DIM = 64
GQA_GROUP = ATTN_HEADS // KV_HEADS
WINDOW = 128
WINDOW_CHUNKS = WINDOW // CHUNK
NUM_BUCKETS = 32
MAX_DISTANCE = 128
N_GROUPS = 4
EXPERTS_PER_GROUP = 8
N_EXPERTS = N_GROUPS * EXPERTS_PER_GROUP
TOP_K = 2

LANES = 128
VMEM_LIMIT = 56 * 1024 * 1024

INPROJ_TILE = 512
PROJ_TILE = 512
MERGE_SUBTILE = 512
HGRN_CHUNK = 128
HGRN_CHUNKS_PER_STEP = 4
ATTN_CHUNKS_PER_STEP = 8
EXPERT_TILE = 512


LOG2E = math.log2(math.e)


def _sigmoid(x):
    return 1.0 / (1.0 + jnp.exp(-x))


def _split3(x):
    hi = x.astype(BF16)
    r1 = x - hi.astype(F32)
    mid = r1.astype(BF16)
    lo = (r1 - mid.astype(F32)).astype(BF16)
    return hi, mid, lo


def _dot(a, b):
    return jnp.dot(a, b, preferred_element_type=F32)


def _dot_nt(a, b):
    return lax.dot_general(a, b, (((1,), (1,)), ((), ())), preferred_element_type=F32)


def _dot_tn(a, b):
    return lax.dot_general(a, b, (((0,), (0,)), ((), ())), preferred_element_type=F32)


SUBLANES = 8


def _store_row_tiles(ref, x):
    n, d = x.shape
    sub = d // LANES
    for j in range(sub):
        ref[pl.ds(j, n, stride=sub), :] = x[:, j * LANES:(j + 1) * LANES]


def _load_row_tiles(ref, n, sub):
    return jnp.concatenate([ref[pl.ds(j, n, stride=sub), :] for j in range(sub)], axis=1)


def _params(*sem):
    return pltpu.CompilerParams(dimension_semantics=sem, vmem_limit_bytes=VMEM_LIMIT)


def _inproj_kernel(hw, aw, kvw, d, x_ref, wn_ref, w_ref, qg_ref, kg_ref, bdq_ref, bdk_ref,
                   qr_ref, z_ref, hv_ref, hg_ref, qa_ref, k_ref, v_ref, sga_ref, sgb_ref):
    x = x_ref[...]
    ms = jnp.mean(x * x, axis=-1, keepdims=True)
    xn = (x * lax.rsqrt(ms + EPS) * wn_ref[...]).astype(BF16)

    def seg(a, b):
        return _dot(xn, w_ref[:, a:b])

    def head_rms(a, bd_ref, gain):
        sq = a * a
        hi = sq.astype(BF16)
        lo = (sq - hi.astype(F32)).astype(BF16)
        m = _dot(hi, bd_ref[...]) + _dot(lo, bd_ref[...])
        return a * lax.rsqrt(m + EPS) * gain

    o = 0
    hq = seg(o, o + hw)
    qr_ref[...] = (hq * _sigmoid(hq) * (hw // HGRN_HEADS) ** -0.5).astype(BF16)
    o += hw
    z_ref[...] = seg(o, o + hw)
    o += hw
    hv_ref[...] = seg(o, o + hw).astype(BF16)
    o += hw
    hg_ref[...] = seg(o, o + hw).astype(BF16)
    o += hw
    aq = seg(o, o + aw)
    qa_ref[...] = (head_rms(aq, bdq_ref, qg_ref[...]) * (HEAD_DIM ** -0.5 * LOG2E)).astype(BF16)
    o += aw
    k_ref[...] = head_rms(seg(o, o + kvw), bdk_ref, kg_ref[...])
    o += kvw
    v_ref[...] = seg(o, o + kvw)
    o += kvw
    sga_ref[...] = _sigmoid(seg(o, o + d)).astype(BF16)
    o += d
    sgb_ref[...] = _sigmoid(seg(o, o + d)).astype(BF16)


def _block_diag_mean(width, group):
    i = np.arange(width)
    return jnp.asarray((i[:, None] // group == i[None, :] // group) / group, dtype=BF16)


def _inproj(x, w_norm, w_in_bf16, q_gain, k_gain, hw, aw, kvw):
    t, d = x.shape
    tm = INPROJ_TILE if t % INPROJ_TILE == 0 else t
    cols = w_in_bf16.shape[1]
    row = lambda w: pl.BlockSpec((tm, w), lambda i: (i, 0))
    const = lambda a, b: pl.BlockSpec((a, b), lambda i: (0, 0))
    outs = [(hw, BF16), (hw, F32), (hw, BF16), (hw, BF16), (aw, BF16), (kvw, F32), (kvw, F32), (d, BF16), (d, BF16)]
    return pl.pallas_call(
        functools.partial(_inproj_kernel, hw, aw, kvw, d),
        grid=(t // tm,),
        in_specs=[row(d), const(1, d), const(d, cols), const(1, aw), const(1, kvw), const(aw, aw), const(kvw, kvw)],
        out_specs=[row(w) for w, _ in outs],
        out_shape=[jax.ShapeDtypeStruct((t, w), dt) for w, dt in outs],
        compiler_params=_params("arbitrary"),
        name="inproj",
    )(x, w_norm.reshape(1, d), w_in_bf16,
      jnp.tile(q_gain, aw // HEAD_DIM).reshape(1, aw), jnp.tile(k_gain, kvw // HEAD_DIM).reshape(1, kvw),
      _block_diag_mean(aw, HEAD_DIM), _block_diag_mean(kvw, HEAD_DIM))


def _hgrn_consts(L):
    t = np.arange(L)
    u = t[None, :]
    blocks = [u <= t[:, None], u > t[:, None]]
    levels = []
    m = L // 2
    while m >= 1:
        levels.append(m)
        m //= 2
    lvl = np.full((L, L), -1, np.int32)
    lvl[t, t] = len(levels)
    isq_cols = []
    for j, m in enumerate(levels):
        bnd = (t // (2 * m)) * (2 * m) + m - 1
        isq = (t % (2 * m)) >= m
        cq = isq[:, None] & (u > bnd[:, None]) & (u <= t[:, None])
        ck = (~isq)[:, None] & (u > t[:, None]) & (u <= bnd[:, None])
        blocks.append(cq | ck)
        same = (t[:, None] // (2 * m)) == (t[None, :] // (2 * m))
        lvl[same & isq[:, None] & (~isq)[None, :]] = j
        isq_cols.append(isq)
    c = np.concatenate(blocks, axis=0).astype(np.float32)
    isq = np.stack(isq_cols, axis=1).astype(np.float32)
    isq = np.pad(isq, ((0, 0), (0, LANES - isq.shape[1])))
    c2 = np.tile(c, (1, 2))
    return jnp.asarray(c2, dtype=BF16), jnp.asarray(np.tile(lvl, (1, 2))), jnp.asarray(isq), len(levels)


def _hgrn_kernel(L, nlev, heads, dk, qr_ref, z_ref, hv_ref, hg_ref, lb_ref, og_ref, c_ref, lvl_ref, isq_ref,
                 s0_ref, y_ref, sout_ref, st_ref):
    c = pl.program_id(1)

    @pl.when(c == 0)
    def _():
        for h in range(heads):
            st_ref[h] = s0_ref[0, h].T

    for cc in range(z_ref.shape[1] // L):
        _hgrn_chunk(L, nlev, heads, dk, slice(cc * L, (cc + 1) * L), qr_ref, z_ref, hv_ref, hg_ref, lb_ref, og_ref,
                    c_ref, lvl_ref, isq_ref, y_ref, st_ref)

    @pl.when(c == pl.num_programs(1) - 1)
    def _():
        for h in range(heads):
            sout_ref[0, h] = st_ref[h].T


def _hgrn_chunk(L, nlev, heads, dk, rows, qr_ref, z_ref, hv_ref, hg_ref, lb_ref, og_ref, c_ref, lvl_ref, isq_ref,
                y_ref, st_ref):
    z = z_ref[0, rows, :]
    lb = lb_ref[...]
    e = jnp.exp(-jnp.abs(z))
    r = 1.0 / (1.0 + e)
    pos = z >= 0
    sig = jnp.where(pos, r, e * r)
    sig_neg = jnp.where(pos, e * r, r)
    logf = jnp.log(lb + (1.0 - lb) * sig)
    kin = (1.0 - lb) * sig_neg
    q = qr_ref[0, rows, :].astype(F32)

    hi, mid, _ = _split3(logf * LOG2E)
    ex = jnp.exp2(_dot(c_ref[...], jnp.concatenate([hi, mid], axis=0)))
    e_b = ex[0:L]
    e_rev = ex[L:2 * L]

    q_in = (q * e_b).astype(BF16)
    k_out = (kin * e_rev).astype(BF16)
    q_b = q.astype(BF16)
    k_b = kin.astype(BF16)
    xs = []
    for j in range(nlev):
        m = L >> (j + 1)
        if m % SUBLANES == 0:
            qk = jnp.concatenate([(q if blk % 2 else kin)[blk * m:(blk + 1) * m] for blk in range(L // m)], axis=0)
        else:
            qk = jnp.where(isq_ref[:, j:j + 1] > 0.5, q, kin)
        xs.append((qk * ex[(2 + j) * L:(3 + j) * L]).astype(BF16))
    lvl = lvl_ref[...]
    v = hv_ref[0, rows, :]
    g = hg_ref[0, rows, :].astype(F32)
    og = og_ref[...]

    def block_diag(x):
        zero = jnp.zeros((x.shape[0], dk), x.dtype)
        return jnp.concatenate([jnp.concatenate([x[:, :dk], zero], axis=1),
                                jnp.concatenate([zero, x[:, dk:]], axis=1)], axis=0)

    for pair in range(heads // 2):
        sl = slice(2 * pair * dk, 2 * (pair + 1) * dk)
        a = jnp.where(lvl == nlev, _dot_nt(q_b[:, sl], block_diag(k_b[:, sl])), 0.0)
        for j in range(nlev):
            xp = xs[j][:, sl]
            a = jnp.where(lvl == j, _dot_nt(xp, block_diag(xp)), a)
        st = jnp.concatenate([st_ref[2 * pair], st_ref[2 * pair + 1]], axis=1)
        vp = v[:, sl]
        o = _dot(a.astype(BF16), block_diag(vp)) + _dot_nt(q_in[:, sl], block_diag(st.astype(BF16)))
        for half in range(2):
            h = 2 * pair + half
            hs = slice(h * dk, (h + 1) * dk)
            st_ref[h] = st_ref[h] * e_b[L - 1:L, hs] + _dot_tn(v[:, hs], k_out[:, hs])
            oh = o[:, half * dk:(half + 1) * dk]
            ms = jnp.mean(oh * oh, axis=-1, keepdims=True)
            gh = g[:, hs]
            y_ref[0, rows, hs] = (oh * lax.rsqrt(ms + EPS) * og[:, hs] * (gh * _sigmoid(gh))).astype(BF16)


def _hgrn(qr, z, hv, hg, lb, out_gain, s0, L):
    b, s, w = z.shape
    heads, dk = s0.shape[1], s0.shape[2]
    cm, lvl, isq, nlev = _hgrn_consts(L)
    per_step = HGRN_CHUNKS_PER_STEP if s % (HGRN_CHUNKS_PER_STEP * L) == 0 else 1
    seq = pl.BlockSpec((1, per_step * L, w), lambda i, c: (i, c, 0))
    const = lambda a: pl.BlockSpec(a.shape, lambda i, c: (0,) * a.ndim)
    state = pl.BlockSpec((1, heads, dk, dk), lambda i, c: (i, 0, 0, 0))
    lb2 = lb.reshape(1, w)
    og2 = jnp.tile(out_gain, heads).reshape(1, w)
    return pl.pallas_call(
        functools.partial(_hgrn_kernel, L, nlev, heads, dk),
        grid=(b, s // (per_step * L)),
        in_specs=[seq, seq, seq, seq, const(lb2), const(og2), const(cm), const(lvl), const(isq), state],
        out_specs=[seq, state],
        out_shape=[jax.ShapeDtypeStruct((b, s, w), BF16), jax.ShapeDtypeStruct(s0.shape, F32)],
        scratch_shapes=[pltpu.VMEM((heads, dk, dk), F32)],
        compiler_params=_params("arbitrary", "arbitrary"),
        name=f"hgrn_scan_{L}",
    )(qr, z, hv, hg, lb2, og2, cm, lvl, isq, s0)


def _t5_bucket_np(rel):
    half = NUM_BUCKETS // 2
    max_exact = half // 2
    assert (NUM_BUCKETS, MAX_DISTANCE) == (32, 128)
    n = np.abs(rel).astype(np.int64)
    nn = np.maximum(n, 1)
    k = np.zeros_like(nn)
    for j in range(1, 48):
        k = np.where(64 * (1 << j) <= nn * nn, j, k)
    large = np.minimum(max_exact + k, half - 1)
    return np.where(rel > 0, half, 0) + np.where(n < max_exact, n, large)


HEADS_PER_COL = LANES // HEAD_DIM
COLS_PER_GROUP = GQA_GROUP // HEADS_PER_COL
HEAD_ORDER = tuple(g * GQA_GROUP + col * HEADS_PER_COL + half
                   for g in range(KV_HEADS) for half in range(HEADS_PER_COL) for col in range(COLS_PER_GROUP))
KV_EXPAND = KV_HEADS * HEADS_PER_COL


def _expand_kv(x):
    assert HEADS_PER_COL == 2 and KV_HEADS == 2 and x.shape[1] == LANES
    low = lax.broadcasted_iota(I32, x.shape, 1) < HEAD_DIM
    xr = pltpu.roll(x, HEAD_DIM, axis=1)
    zero = jnp.zeros_like(x)
    blocks = [jnp.where(low, x, zero), jnp.where(low, zero, xr), jnp.where(low, xr, zero), jnp.where(low, zero, x)]
    return jnp.concatenate(blocks, axis=1).astype(BF16)


def _attn_core(q, kx, vx, bias, sink):
    tq = q.shape[0]
    scores = []
    for g in range(KV_HEADS):
        cols = [q[:, (g * COLS_PER_GROUP + c) * LANES:(g * COLS_PER_GROUP + c + 1) * LANES]
                for c in range(COLS_PER_GROUP)]
        qst = jnp.concatenate(cols, axis=0)
        for half in range(HEADS_PER_COL):
            blk = g * HEADS_PER_COL + half
            scores.append(_dot_nt(qst, kx[:, blk * LANES:(blk + 1) * LANES]))
    s = jnp.concatenate(scores, axis=0) + bias
    m =jnp.maximum(jnp.max(s, axis=-1, keepdims=True), sink)
    p = jnp.exp2(s - m)
    den = jnp.sum(p, axis=-1, keepdims=True) + jnp.exp2(sink - m)
    pb = p.astype(BF16)
    rden = 1.0 / den
    rows = COLS_PER_GROUP * tq
    outs = []
    for g in range(KV_HEADS):
        o = None
        for half in range(HEADS_PER_COL):
            blk = g * HEADS_PER_COL + half
            rs = slice(blk * rows, (blk + 1) * rows)
            part = _dot(pb[rs], vx[:, blk * LANES:(blk + 1) * LANES]) * rden[rs]
            o = part if o is None else o + part
        outs.extend(o[c * tq:(c + 1) * tq] for c in range(COLS_PER_GROUP))
    return outs


def _bias_rows(table, bucket):
    onehot = (jnp.asarray(bucket)[..., None] == jnp.arange(NUM_BUCKETS)).astype(F32)
    cols = jnp.stack([table[:, h] for h in HEAD_ORDER], axis=1)
    bias = jnp.einsum('...qkb,bh->...hqk', onehot, cols, precision=lax.Precision.HIGHEST)
    return bias.reshape(*bucket.shape[:-2], ATTN_HEADS * bucket.shape[-2], bucket.shape[-1]) * LOG2E


def _pad_keys(bias):
    tk = bias.shape[-1]
    pad = [(0, 0)] * (bias.ndim - 1) + [(0, -tk % LANES)]
    return jnp.pad(bias, pad, constant_values=-jnp.inf)


def _sink_rows(sinks, tq):
    rows = jnp.repeat(jnp.stack([sinks[h] for h in HEAD_ORDER]).astype(F32), tq)
    return rows.reshape(ATTN_HEADS * tq, 1) * LOG2E


def _store_heads(ref, x):
    for g in range(KV_HEADS):
        ref[0, 0, :, g, :] = x[:, g * HEAD_DIM:(g + 1) * HEAD_DIM]


def _load_heads(ref):
    return jnp.concatenate([ref[0, 0, :, g, :] for g in range(KV_HEADS)], axis=1)


def _attn_prompt_kernel(cb, q_ref, k_ref, v_ref, km_ref, vm_ref, bias_ref, sink_ref, o_ref, nk_ref, nv_ref,
                        kx_ref, vx_ref):
    step = pl.program_id(1)
    s_len = k_ref.shape[1]
    meta_at = WINDOW + s_len

    @pl.when(step == pl.num_programs(1) - 1)
    def _():
        _store_heads(nk_ref, k_ref[0, s_len - WINDOW:, :])
        _store_heads(nv_ref, v_ref[0, s_len - WINDOW:, :])

    @pl.when(step == 0)
    def _():
        piece = min(s_len, 512)
        for src, meta, dst in ((k_ref, km_ref, kx_ref), (v_ref, vm_ref, vx_ref)):
            dst[0:WINDOW] = jnp.zeros((WINDOW, dst.shape[1]), BF16)
            for r in range(0, s_len, piece):
                dst[WINDOW + r:WINDOW + r + piece] = _expand_kv(src[0, r:r + piece, :])
            dst[meta_at:meta_at + N_META] = _expand_kv(meta[...])
            dst[meta_at + N_META:] = jnp.zeros((dst.shape[0] - meta_at - N_META, dst.shape[1]), BF16)

    win = WINDOW + CHUNK
    tail = kx_ref.shape[0] - meta_at
    for j in range(cb):
        c = step * cb + j
        start = pl.multiple_of(c * CHUNK, CHUNK)
        kall = jnp.concatenate([kx_ref[pl.ds(start, win), :], kx_ref[meta_at:meta_at + tail, :]], axis=0)
        vall = jnp.concatenate([vx_ref[pl.ds(start, win), :], vx_ref[meta_at:meta_at + tail, :]], axis=0)
        rows = slice(j * CHUNK, (j + 1) * CHUNK)
        outs = _attn_core(q_ref[0, rows, :], kall, vall, bias_ref[jnp.minimum(c, bias_ref.shape[0] - 1)],
                          sink_ref[...])
        for ci, o in enumerate(outs):
            o_ref[0, rows, ci * LANES:(ci + 1) * LANES] = o.astype(o_ref.dtype)


def _attn_prompt(q, k, v, k_meta, v_meta, table, sinks):
    b, s, aw = q.shape
    kvw = k.shape[-1]
    nc = s // CHUNK
    cb = ATTN_CHUNKS_PER_STEP if nc % ATTN_CHUNKS_PER_STEP == 0 else 1
    assert s % min(s, 512) == 0
    n_bias = 1
    while True:
        qpos = N_META + (n_bias - 1) * CHUNK
        if np.all(_t5_bucket_np(np.arange(N_META) - qpos) == _t5_bucket_np(np.arange(N_META) - qpos - 10 ** 6)):
            break
        n_bias += 1
    n_bias = min(max(n_bias, WINDOW_CHUNKS + 1), nc)
    cs = np.arange(n_bias)[:, None]
    qpos = N_META + cs * CHUNK + np.arange(CHUNK)[None]
    wpos = N_META + (cs - WINDOW_CHUNKS) * CHUNK + np.arange(WINDOW + CHUNK)[None]
    kpos = np.concatenate([wpos, np.broadcast_to(np.arange(N_META), (n_bias, N_META))], axis=1)
    valid = np.concatenate([wpos >= N_META, np.ones((n_bias, N_META), bool)], axis=1)
    bias = _bias_rows(table, _t5_bucket_np(kpos[:, None, :] - qpos[:, :, None]))
    bias = _pad_keys(jnp.where(valid[:, None, :], bias, -jnp.inf))
    sink = _sink_rows(sinks, CHUNK)
    qs = pl.BlockSpec((1, cb * CHUNK, aw), lambda i, c: (i, c, 0))
    kv = pl.BlockSpec((1, s, kvw), lambda i, c: (i, 0, 0))
    meta = pl.BlockSpec((N_META, kvw), lambda i, c: (0, 0))
    xrows = WINDOW + s + bias.shape[-1] - (WINDOW + CHUNK)
    cache_shape = (1, b, WINDOW, KV_HEADS, HEAD_DIM)
    cache = pl.BlockSpec((1, 1) + cache_shape[2:], lambda i, c: (0, i, 0, 0, 0))
    return pl.pallas_call(
        functools.partial(_attn_prompt_kernel, cb),
        grid=(b, nc // cb),
        in_specs=[qs, kv, kv, meta, meta, pl.BlockSpec(bias.shape, lambda i, c: (0, 0, 0)),
                  pl.BlockSpec(sink.shape, lambda i, c: (0, 0))],
        out_specs=[qs, cache, cache],
        out_shape=[jax.ShapeDtypeStruct((b, s, aw), BF16)] + [jax.ShapeDtypeStruct(cache_shape, F32)] * 2,
        scratch_shapes=[pltpu.VMEM((xrows, KV_EXPAND * LANES), BF16), pltpu.VMEM((xrows, KV_EXPAND * LANES), BF16)],
        compiler_params=_params("arbitrary", "arbitrary"),
        name="attn_prompt",
    )(q, k, v, k_meta, v_meta, bias, sink)


def _attn_sample_kernel(q_ref, kc_ref, vc_ref, kn_ref, vn_ref, km_ref, vm_ref, bias_ref, sink_ref,
                        o_ref, nk_ref, nv_ref):
    tk = kc_ref.shape[2] + kn_ref.shape[1] + km_ref.shape[0]
    zeros = jnp.zeros((bias_ref.shape[1] - tk, km_ref.shape[1]), F32)
    kc, vc, kn, vn = _load_heads(kc_ref), _load_heads(vc_ref), kn_ref[0], vn_ref[0]
    kall = _expand_kv(jnp.concatenate([kc, kn, km_ref[...], zeros], axis=0))
    vall = _expand_kv(jnp.concatenate([vc, vn, vm_ref[...], zeros], axis=0))
    outs = _attn_core(q_ref[0], kall, vall, bias_ref[...], sink_ref[...])
    for ci, o in enumerate(outs):
        o_ref[0, :, ci * LANES:(ci + 1) * LANES] = o.astype(o_ref.dtype)
    n_new = kn.shape[0]
    _store_heads(nk_ref, jnp.concatenate([kc[n_new:], kn], axis=0))
    _store_heads(nv_ref, jnp.concatenate([vc[n_new:], vn], axis=0))


def _attn_sample(q, k_cache, v_cache, k_new, v_new, k_meta, v_meta, table, sinks):
    bd, sd, aw = q.shape
    kvw = k_new.shape[-1]
    win = k_cache.shape[2]
    cache = pl.BlockSpec((1, 1) + k_cache.shape[2:], lambda i: (0, i, 0, 0, 0))
    qpos = N_META + PAST_LEN + np.arange(sd)
    kpos = np.concatenate([N_META + PAST_LEN - win + np.arange(win), qpos, np.arange(N_META)])
    bias = _pad_keys(_bias_rows(table, _t5_bucket_np(kpos[None, :] - qpos[:, None])))
    sink = _sink_rows(sinks, sd)
    per = lambda n, w: pl.BlockSpec((1, n, w), lambda i: (i, 0, 0))
    meta = pl.BlockSpec((N_META, kvw), lambda i: (0, 0))
    return pl.pallas_call(
        _attn_sample_kernel,
        grid=(bd,),
        in_specs=[per(sd, aw), cache, cache, per(sd, kvw), per(sd, kvw), meta, meta,
                  pl.BlockSpec(bias.shape, lambda i: (0, 0)), pl.BlockSpec(sink.shape, lambda i: (0, 0))],
        out_specs=[per(sd, aw), cache, cache],
        out_shape=[jax.ShapeDtypeStruct((bd, sd, aw), BF16)] + [jax.ShapeDtypeStruct(k_cache.shape, F32)] * 2,
        compiler_params=_params("arbitrary"),
        name="attn_sample",
    )(q, k_cache, v_cache, k_new, v_new, k_meta, v_meta, bias, sink)


ROUTE_E1, ROUTE_E2, ROUTE_R1, ROUTE_R2, ROUTE_W1, ROUTE_W2 = range(6)
META_FIELDS = 2 * TOP_K


def _merge_kernel(hw, x_ref, yr_ref, at_ref, sga_ref, sgb_ref, wb_ref, wo_ref, wn_ref, wr_ref, br_ref, cnt0_ref, tri_ref,
                  h_ref, xn_ref, route_ref, meta_ref, cnt_ref, carry_ref):
    i = pl.program_id(0)

    @pl.when(i == 0)
    def _():
        carry_ref[...] = cnt0_ref[...]

    n = x_ref.shape[0]
    sub = MERGE_SUBTILE if n % MERGE_SUBTILE == 0 else n
    for r0 in range(0, n, sub):
        _merge_rows(hw, r0, sub, x_ref, yr_ref, at_ref, sga_ref, sgb_ref, wb_ref, wo_ref, wn_ref, wr_ref, br_ref,
                    tri_ref, h_ref, xn_ref, route_ref, meta_ref, carry_ref)
    cnt_ref[...] = carry_ref[...]


def _merge_rows(hw, r0, tm, x_ref, yr_ref, at_ref, sga_ref, sgb_ref, wb_ref, wo_ref, wn_ref, wr_ref, br_ref, tri_ref,
                h_ref, xn_ref, route_ref, meta_ref, carry_ref):
    rows = slice(r0, r0 + tm)
    br = _dot(yr_ref[rows, :], wb_ref[0:hw, :])
    ba = _dot(at_ref[rows, :], wb_ref[hw:, :])
    merged = sga_ref[rows, :].astype(F32) * br + sgb_ref[rows, :].astype(F32) * ba
    h = x_ref[rows, :] + _dot(merged.astype(BF16), wo_ref[...])
    h_ref[rows, :] = h
    ms = jnp.mean(h * h, axis=-1, keepdims=True)
    xn = h * lax.rsqrt(ms + EPS) * wn_ref[...]
    per_row = xn.shape[1] // LANES
    _store_row_tiles(xn_ref.at[r0 * per_row:(r0 + tm) * per_row], xn)

    x_hi = xn.astype(BF16)
    x_lo = (xn - x_hi.astype(F32)).astype(BF16)
    both = _dot(x_hi, wr_ref[...])
    logits = both[:, :LANES] + both[:, LANES:] + _dot(x_lo, wr_ref[:, :LANES]) + br_ref[...]

    lane_i =lax.broadcasted_iota(I32, (tm, LANES), 1)
    lane = lane_i.astype(F32)
    group_of_lane = (lane_i >> int(math.log2(EXPERTS_PER_GROUP))).astype(F32)
    ninf = -jnp.inf
    first = lambda hit, idx: jnp.min(jnp.where(hit, idx, float(LANES)), axis=-1, keepdims=True)
    gmask = (lane_i >= N_EXPERTS) & (lane_i < N_EXPERTS + N_GROUPS)
    gl = jnp.where(gmask, logits, ninf)
    gmax = jnp.max(gl, axis=-1, keepdims=True)
    gidx = first(gl == gmax, lane - N_EXPERTS)
    gval = 1.0 / jnp.sum(jnp.exp(gl - gmax), axis=-1, keepdims=True)
    emask = (lane_i < N_EXPERTS) & (group_of_lane == gidx)
    el = jnp.where(emask, logits, ninf)
    m1 = jnp.max(el, axis=-1, keepdims=True)
    i1 = first(el == m1, lane)
    el2 = jnp.where(lane == i1, ninf, el)
    m2 = jnp.max(el2, axis=-1, keepdims=True)
    i2 = first(el2 == m2, lane)
    e21 = jnp.exp(m2 - m1)
    w1 = gval / (1.0 + e21)
    w2 = gval * e21 / (1.0 + e21)

    sel1 = lane == i1
    sel2 = lane == i2
    oh = (sel1 | sel2).astype(BF16)
    before = _dot(tri_ref[...], oh) + carry_ref[...]
    r1 = jnp.sum(jnp.where(sel1, before, 0.0), axis=-1, keepdims=True)
    r2 = jnp.sum(jnp.where(sel2, before, 0.0), axis=-1, keepdims=True)
    carry_ref[...] = carry_ref[...] + jnp.sum(oh.astype(F32), axis=0, keepdims=True)

    rec = jnp.zeros((tm, LANES), F32)
    for slot, val in ((ROUTE_E1, i1), (ROUTE_E2, i2), (ROUTE_W1, w1), (ROUTE_W2, w2), (ROUTE_R1, r1), (ROUTE_R2, r2)):
        rec = jnp.where(lane_i == slot, val, rec)
    route_ref[rows, :] = rec
    meta_ref[0, :, rows] = rec.T[ROUTE_E1:ROUTE_E1 + META_FIELDS].astype(I32)


def _merge(x, y_rec, att, sga, sgb, wb, wo, w_norm, w_router, b_router, cnt0):
    t, d = x.shape
    hw = y_rec.shape[1]
    tm = PROJ_TILE if t % PROJ_TILE == 0 else t
    sub = MERGE_SUBTILE if tm % MERGE_SUBTILE == 0 else tm
    row = lambda w: pl.BlockSpec((tm, w), lambda i: (i, 0))
    const = lambda a, b: pl.BlockSpec((a, b), lambda i: (0, 0))
    return pl.pallas_call(
        functools.partial(_merge_kernel, hw),
        grid=(t // tm,),
        in_specs=[row(d), row(hw), row(att.shape[1]), row(d), row(d), const(*wb.shape), const(d, d), const(1, d),
                  const(d, 2 * LANES), const(1, LANES), const(1, LANES), const(sub, sub)],
        out_specs=[row(d), pl.BlockSpec((tm * d // LANES, LANES), lambda i: (i, 0)), row(LANES),
                   pl.BlockSpec((1, META_FIELDS, tm), lambda i: (i, 0, 0)), const(1, LANES)],
        out_shape=[jax.ShapeDtypeStruct((t, d), F32), jax.ShapeDtypeStruct((t * d // LANES, LANES), F32),
                   jax.ShapeDtypeStruct((t, LANES), F32), jax.ShapeDtypeStruct((t // tm, META_FIELDS, tm), I32),
                   jax.ShapeDtypeStruct((1, LANES), F32)],
        scratch_shapes=[pltpu.VMEM((1, LANES), F32)],
        compiler_params=_params("arbitrary"),
        name="merge_route",
    )(x, y_rec, att, sga, sgb, wb, wo, w_norm.reshape(1, d), w_router, b_router, cnt0,
      jnp.asarray(np.tri(sub, k=-1), dtype=BF16))


def _row_copy(src, dst, sem):
    return pltpu.make_async_copy(src, dst, sem)


ROW_UNROLL = 8


def _tile_rows(ref, row, sub):
    return ref.at[pl.ds(pl.multiple_of(row * sub, sub), sub)]


def _slot(meta_ref, pstart_ref, r, k):
    return pstart_ref[meta_ref[0, k, r]] + meta_ref[0, TOP_K + k, r]


def _scatter_kernel(n_tok, sub, first, meta_ref, pstart_ref, pend_ref, x_ref, *rest):
    xs_ref, zero_ref, sem, zsem, tsem = rest[-5:]

    def zero_blocks(blocks, zs, wait):
        for cond, row in blocks:
            @pl.when(cond)
            def _():
                at = row * sub if isinstance(row, int) else pl.multiple_of(row * sub, EXPERT_TILE * sub)
                cp = _row_copy(zero_ref, xs_ref.at[pl.ds(at, EXPERT_TILE * sub)], zs)
                cp.wait() if wait else cp.start()

    if first:
        n_rows = xs_ref.shape[0] // sub
        tails = [(pend_ref[e] > pstart_ref[e], pend_ref[e] - EXPERT_TILE) for e in range(N_EXPERTS)]
        unused = [(n_rows - (j + 1) * EXPERT_TILE >= pend_ref[N_EXPERTS - 1], n_rows - (j + 1) * EXPERT_TILE)
                  for j in range(N_EXPERTS)]

        @pl.when(pl.program_id(0) == 0)
        def _():
            zero_ref[...] = jnp.zeros_like(zero_ref)
            zero_blocks(tails, zsem, False)
            zero_blocks(unused, tsem, False)
            zero_blocks(tails, zsem, True)

    def issue(grp, _):
        for u in range(ROW_UNROLL):
            r = grp * ROW_UNROLL + u
            for k in range(TOP_K):
                _row_copy(_tile_rows(x_ref, r, sub), _tile_rows(xs_ref, _slot(meta_ref, pstart_ref, r, k), sub),
                          sem).start(priority=(u * TOP_K + k) % 2)
        return 0

    lax.fori_loop(0, n_tok // ROW_UNROLL, issue, 0)
    for k in range(TOP_K):
        _row_copy(x_ref, xs_ref.at[pl.ds(0, n_tok * sub)], sem).wait()

    if first:
        @pl.when(pl.program_id(0) == 0)
        def _():
            zero_blocks(unused, tsem, True)


def _scatter_rows(x, meta, pstarts, pends, xs, rows):
    n, fields, tm = meta.shape
    sub = x.shape[0] // (n * tm)
    first = xs is None
    smem = pl.BlockSpec(memory_space=pltpu.SMEM)
    return pl.pallas_call(
        functools.partial(_scatter_kernel, tm, sub, first),
        grid=(n,),
        in_specs=[pl.BlockSpec((1, fields, tm), lambda i: (i, 0, 0), memory_space=pltpu.SMEM), smem, smem,
                  pl.BlockSpec((tm * sub, LANES), lambda i: (i, 0))]
        + ([] if first else [pl.BlockSpec(memory_space=pl.ANY)]),
        out_specs=pl.BlockSpec(memory_space=pl.ANY),
        out_shape=jax.ShapeDtypeStruct((rows * sub, LANES), F32),
        scratch_shapes=[pltpu.VMEM((EXPERT_TILE * sub, LANES), F32)] + [pltpu.SemaphoreType.DMA(())] * 3,
        input_output_aliases={} if first else {4: 0},
        compiler_params=_params("arbitrary"),
        name="moe_scatter",
    )(meta, pstarts, pends, x, *([] if first else [xs]))


def _expert_kernel(tm, sub, be_ref, nb_ref, x_ref, wg_ref, wu_ref, wd_ref, y_ref, wgb_ref, wub_ref, wdb_ref):
    i = pl.program_id(0)
    used = i < nb_ref[0]

    @pl.when(used & ((i == 0) | (be_ref[i] != be_ref[jnp.maximum(i - 1, 0)])))
    def _():
        wgb_ref[...] = wg_ref[0].astype(BF16)
        wub_ref[...] = wu_ref[0].astype(BF16)
        wdb_ref[...] = wd_ref[0].astype(BF16)

    @pl.when(used)
    def _():
        x = _load_row_tiles(x_ref, tm, sub).astype(BF16)
        g = _dot(x, wgb_ref[...])
        u = _dot(x, wub_ref[...])
        hmid = (g * _sigmoid(g) * u).astype(BF16)
        _store_row_tiles(y_ref, _dot(hmid, wdb_ref[...]))

    @pl.when(jnp.logical_not(used))
    def _():
        y_ref[...] = jnp.zeros_like(y_ref)


def _experts(xs, block_e, n_used, wg, wu, wd):
    _, d, ff = wg.shape
    sub = d // LANES
    tm = EXPERT_TILE
    blk = pl.BlockSpec((tm * sub, LANES), lambda i, be, nb: (i, 0))
    grid_spec = pltpu.PrefetchScalarGridSpec(
        num_scalar_prefetch=2,
        grid=(xs.shape[0] // (tm * sub),),
        in_specs=[blk,
                  pl.BlockSpec((1, d, ff), lambda i, be, nb: (be[i], 0, 0)),
                  pl.BlockSpec((1, d, ff), lambda i, be, nb: (be[i], 0, 0)),
                  pl.BlockSpec((1, ff, d), lambda i, be, nb: (be[i], 0, 0))],
        out_specs=blk,
        scratch_shapes=[pltpu.VMEM((d, ff), BF16), pltpu.VMEM((d, ff), BF16), pltpu.VMEM((ff, d), BF16)],
    )
    return pl.pallas_call(
        functools.partial(_expert_kernel, tm, sub),
        grid_spec=grid_spec,
        out_shape=jax.ShapeDtypeStruct(xs.shape, F32),
        compiler_params=_params("arbitrary"),
        name="moe_experts",
    )(block_e, n_used, xs, wg, wu, wd)


def _combine_kernel(n_tok, sub, mcur_ref, mnext_ref, pstart_ref, h_ref, route_ref, ys_ref, o_ref, buf, sem):
    i = pl.program_id(0)
    n = pl.num_programs(0)
    slot = i % 2

    def start(meta_ref, s):
        def issue(grp, _):
            for u in range(ROW_UNROLL):
                r = grp * ROW_UNROLL + u
                for k in range(TOP_K):
                    _row_copy(_tile_rows(ys_ref, _slot(meta_ref, pstart_ref, r, k), sub), _tile_rows(buf.at[s, k], r, sub),
                              sem.at[s]).start(priority=(u * TOP_K + k) % 2)
            return 0
        lax.fori_loop(0, n_tok // ROW_UNROLL, issue, 0)

    @pl.when(i == 0)
    def _():
        start(mcur_ref, 0)

    @pl.when(i + 1 < n)
    def _():
        start(mnext_ref, 1 - slot)

    for k in range(TOP_K):
        _row_copy(ys_ref.at[pl.ds(0, n_tok * sub)], buf.at[slot, k], sem.at[slot]).wait()
    route = route_ref[...]
    w1 = route[:, ROUTE_W1:ROUTE_W1 + 1]
    w2 = route[:, ROUTE_W2:ROUTE_W2 + 1]
    for j in range(sub):
        cols = slice(j * LANES, (j + 1) * LANES)
        part = lambda k: buf[slot, k, pl.ds(j, n_tok, stride=sub), :]
        o_ref[:, cols] = h_ref[:, cols] + (part(0) * w1 + part(1) * w2)


def _combine(h, route, meta, pstarts, ys):
    t, d = h.shape
    n, fields, tm = meta.shape
    sub = d // LANES
    mspec = lambda f: pl.BlockSpec((1, fields, tm), f, memory_space=pltpu.SMEM)
    return pl.pallas_call(
        functools.partial(_combine_kernel, tm, sub),
        grid=(n,),
        in_specs=[mspec(lambda i: (i, 0, 0)), mspec(lambda i: (jnp.minimum(i + 1, n - 1), 0, 0)),
                  pl.BlockSpec(memory_space=pltpu.SMEM),
                  pl.BlockSpec((tm, d), lambda i: (i, 0)), pl.BlockSpec((tm, LANES), lambda i: (i, 0)),
                  pl.BlockSpec(memory_space=pl.ANY)],
        out_specs=pl.BlockSpec((tm, d), lambda i: (i, 0)),
        out_shape=jax.ShapeDtypeStruct((t, d), F32),
        scratch_shapes=[pltpu.VMEM((2, TOP_K, tm * sub, LANES), F32), pltpu.SemaphoreType.DMA((2,))],
        compiler_params=_params("arbitrary"),
        name="moe_combine",
    )(meta, meta, pstarts, h, route, ys)


def _moe(parts, counts, wg, wu, wd):
    tm = EXPERT_TILE
    n_assign = sum(part[0].shape[0] for part in parts) * TOP_K
    n_blocks = -(-(n_assign + N_EXPERTS * (tm - 1)) // tm)
    counts = counts[0, :N_EXPERTS].astype(I32)
    pcounts = (counts + tm - 1) // tm * tm
    pends = jnp.cumsum(pcounts)
    pstarts = pends - pcounts
    block_start = jnp.arange(n_blocks, dtype=I32) * tm
    block_e = jnp.minimum(jnp.sum((pends[None, :] <= block_start[:, None]).astype(I32), axis=1), N_EXPERTS - 1)
    n_used = pends[-1:] // tm
    xs = None
    for _, xn, _, meta in parts:
        assert meta.shape[-1] % ROW_UNROLL == 0
        xs = _scatter_rows(xn, meta, pstarts, pends, xs, n_blocks * tm)
    ys = _experts(xs, block_e, n_used, wg, wu, wd)
    return [_combine(h, route, meta, pstarts, ys) for h, _, route, meta in parts]


def kernel(x_prompt, x_sample, cache_swa_k, cache_swa_v, state_hgrn, meta_tokens, rel_bias_table, hgrn_lower_bounds, w_norm_mix, w_in, hgrn_out_norm, q_norm, k_norm, attn_sinks, w_branch, w_out, w_norm_ffn, w_router_group, b_router_group, w_router_expert, b_router_expert, w_expert_gate, w_expert_up, w_expert_down):
    b, s, d = x_prompt.shape
    bd, sd, _ = x_sample.shape
    depth, _, heads, dk, dv = state_hgrn.shape
    assert depth == 1 and heads == HGRN_HEADS and dk == dv
    hw = heads * dk
    aw = ATTN_HEADS * HEAD_DIM
    kvw = KV_HEADS * HEAD_DIM
    assert w_in.shape[-1] == 4 * hw + aw + 2 * kvw + 2 * d
    assert s % HGRN_CHUNK == 0 and s % CHUNK == 0 and sd == N_META and N_EXPERTS + N_GROUPS <= LANES
    l = 0

    p = jax.nn.softmax(hgrn_lower_bounds.astype(F32), axis=0)
    lb = jnp.cumsum(p, axis=0)[l + 1] - p[0]

    w_in_b = w_in[l].astype(BF16)
    proj = functools.partial(_inproj, w_norm=w_norm_mix[l], w_in_bf16=w_in_b, q_gain=q_norm[l], k_gain=k_norm[l],
                             hw=hw, aw=aw, kvw=kvw)
    x_small = jnp.concatenate([x_sample.reshape(bd * sd, d), meta_tokens.astype(F32)], axis=0)
    qr_s, z_s, hv_s, hg_s, qa_s, k_s, v_s, sga_s, sgb_s = proj(x_small)
    qr_p, z_p, hv_p, hg_p, qa_p, k_p, v_p, sga_p, sgb_p = proj(x_prompt.reshape(b * s, d))
    ns = bd * sd
    k_meta, v_meta = k_s[ns:], v_s[ns:]

    streams = lambda a, n: a.reshape(n, -1, a.shape[-1])
    s0_small = jnp.concatenate([state_hgrn[l].astype(F32), jnp.zeros((1, heads, dk, dv), F32)], axis=0)
    y_small, st_small = _hgrn(streams(qr_s, bd + 1), streams(z_s, bd + 1), streams(hv_s, bd + 1),
                              streams(hg_s, bd + 1), lb, hgrn_out_norm[l], s0_small, sd)
    s0_p = jnp.broadcast_to(st_small[bd:], (b, heads, dk, dv))
    y_p, st_p = _hgrn(streams(qr_p, b), streams(z_p, b), streams(hv_p, b), streams(hg_p, b), lb,
                      hgrn_out_norm[l], s0_p, HGRN_CHUNK)

    table = rel_bias_table.astype(F32)
    att_p, new_k_p, new_v_p = _attn_prompt(streams(qa_p, b), streams(k_p, b), streams(v_p, b), k_meta, v_meta,
                                           table, attn_sinks[l])
    k_new, v_new = k_s[:ns].reshape(bd, sd, kvw), v_s[:ns].reshape(bd, sd, kvw)
    att_s, new_k_s, new_v_s = _attn_sample(qa_s[:ns].reshape(bd, sd, aw), cache_swa_k[l:l + 1].astype(F32),
                                           cache_swa_v[l:l + 1].astype(F32), k_new, v_new, k_meta, v_meta,
                                           table, attn_sinks[l])

    wb = w_branch[l].astype(BF16)
    wo = w_out[l].astype(BF16)
    w_router = jnp.pad(jnp.concatenate([w_router_expert[l], w_router_group[l]], axis=1).astype(F32),
                       ((0, 0), (0, LANES - N_EXPERTS - N_GROUPS)))
    b_router = jnp.pad(jnp.concatenate([b_router_expert[l], b_router_group[l]]).astype(F32),
                       (0, LANES - N_EXPERTS - N_GROUPS)).reshape(1, LANES)
    w_router_hi = w_router.astype(BF16)
    w_router_lo = (w_router - w_router_hi.astype(F32)).astype(BF16)
    w_router = jnp.concatenate([w_router_hi, w_router_lo], axis=1)
    merge = functools.partial(_merge, wb=wb, wo=wo, w_norm=w_norm_ffn[l], w_router=w_router, b_router=b_router)
    *part_p, cnt_p = merge(x_prompt.reshape(b * s, d), y_p.reshape(b * s, hw), att_p.reshape(b * s, aw),
                           sga_p, sgb_p, cnt0=jnp.zeros((1, LANES), F32))
    *part_s, cnt_s = merge(x_sample.reshape(ns, d), y_small[:bd].reshape(ns, hw), att_s.reshape(ns, aw),
                           sga_s[:ns], sgb_s[:ns], cnt0=cnt_p)

    out_p, out_s = _moe([part_p, part_s], cnt_s, w_expert_gate[l], w_expert_up[l], w_expert_down[l])

    return (out_p.reshape(b, s, d), out_s.reshape(bd, sd, d), new_k_p, new_v_p, st_p[None],
            new_k_s, new_v_s, st_small[:bd][None])
```

```python
import functools
import math

import numpy as np
import jax
import jax.numpy as jnp
from jax import lax
from jax.experimental import pallas as pl
from jax.experimental.pallas import tpu as pltpu

F32 = jnp.float32
BF16 = jnp.bfloat16
I32 = jnp.int32

CHUNK = 64
N_META = 16
PAST_LEN = 2048
EPS = 1e-6
HGRN_HEADS = 4
ATTN_HEADS = 8
KV_HEADS = 2
HEAD_DIM = 64
GQA_GROUP = ATTN_HEADS // KV_HEADS
WINDOW = 128
WINDOW_CHUNKS = WINDOW // CHUNK
NUM_BUCKETS = 32
MAX_DISTANCE = 128
N_GROUPS = 4
EXPERTS_PER_GROUP = 8
N_EXPERTS = N_GROUPS * EXPERTS_PER_GROUP
TOP_K = 2

LANES = 128
VMEM_LIMIT = 56 * 1024 * 1024

INPROJ_TILE = 512
PROJ_TILE = 512
MERGE_SUBTILE = 512
HGRN_CHUNK = 128
HGRN_CHUNKS_PER_STEP = 4
ATTN_CHUNKS_PER_STEP = 8
EXPERT_TILE = 512


LOG2E = math.log2(math.e)


def _sigmoid(x):
    return 1.0 / (1.0 + jnp.exp(-x))


def _split3(x):
    hi = x.astype(BF16)
    r1 = x - hi.astype(F32)
    mid = r1.astype(BF16)
    lo = (r1 - mid.astype(F32)).astype(BF16)
    return hi, mid, lo


def _dot(a, b):
    return jnp.dot(a, b, preferred_element_type=F32)


def _dot_nt(a, b):
    return lax.dot_general(a, b, (((1,), (1,)), ((), ())), preferred_element_type=F32)


def _dot_tn(a, b):
    return lax.dot_general(a, b, (((0,), (0,)), ((), ())), preferred_element_type=F32)


SUBLANES = 8


def _store_row_tiles(ref, x):
    n, d = x.shape
    sub = d // LANES
    for j in range(sub):
        ref[pl.ds(j, n, stride=sub), :] = x[:, j * LANES:(j + 1) * LANES]


def _load_row_tiles(ref, n, sub):
    return jnp.concatenate([ref[pl.ds(j, n, stride=sub), :] for j in range(sub)], axis=1)


def _params(*sem):
    return pltpu.CompilerParams(dimension_semantics=sem, vmem_limit_bytes=VMEM_LIMIT)


def _inproj_kernel(hw, aw, kvw, d, x_ref, wn_ref, w_ref, qg_ref, kg_ref, bdq_ref, bdk_ref,
                   qr_ref, z_ref, hv_ref, hg_ref, qa_ref, k_ref, v_ref, sga_ref, sgb_ref):
    x = x_ref[...]
    ms = jnp.mean(x * x, axis=-1, keepdims=True)
    xn = (x * lax.rsqrt(ms + EPS) * wn_ref[...]).astype(BF16)

    def seg(a, b):
        return _dot(xn, w_ref[:, a:b])

    def head_rms(a, bd_ref, gain):
        sq = a * a
        hi = sq.astype(BF16)
        lo = (sq - hi.astype(F32)).astype(BF16)
        m = _dot(hi, bd_ref[...]) + _dot(lo, bd_ref[...])
        return a * lax.rsqrt(m + EPS) * gain

    o = 0
    hq = seg(o, o + hw)
    qr_ref[...] = (hq * _sigmoid(hq) * (hw // HGRN_HEADS) ** -0.5).astype(BF16)
    o += hw
    z_ref[...] = seg(o, o + hw)
    o += hw
    hv_ref[...] = seg(o, o + hw).astype(BF16)
    o += hw
    hg_ref[...] = seg(o, o + hw).astype(BF16)
    o += hw
    aq = seg(o, o + aw)
    qa_ref[...] = (head_rms(aq, bdq_ref, qg_ref[...]) * (HEAD_DIM ** -0.5 * LOG2E)).astype(BF16)
    o += aw
    k_ref[...] = head_rms(seg(o, o + kvw), bdk_ref, kg_ref[...])
    o += kvw
    v_ref[...] = seg(o, o + kvw)
    o += kvw
    sga_ref[...] = _sigmoid(seg(o, o + d)).astype(BF16)
    o += d
    sgb_ref[...] = _sigmoid(seg(o, o + d)).astype(BF16)


def _block_diag_mean(width, group):
    i = np.arange(width)
    return jnp.asarray((i[:, None] // group == i[None, :] // group) / group, dtype=BF16)


def _inproj(x, w_norm, w_in_bf16, q_gain, k_gain, hw, aw, kvw):
    t, d = x.shape
    tm = INPROJ_TILE if t % INPROJ_TILE == 0 else t
    cols = w_in_bf16.shape[1]
    row = lambda w: pl.BlockSpec((tm, w), lambda i: (i, 0))
    const = lambda a, b: pl.BlockSpec((a, b), lambda i: (0, 0))
    outs = [(hw, BF16), (hw, F32), (hw, BF16), (hw, BF16), (aw, BF16), (kvw, F32), (kvw, F32), (d, BF16), (d, BF16)]
    return pl.pallas_call(
        functools.partial(_inproj_kernel, hw, aw, kvw, d),
        grid=(t // tm,),
        in_specs=[row(d), const(1, d), const(d, cols), const(1, aw), const(1, kvw), const(aw, aw), const(kvw, kvw)],
        out_specs=[row(w) for w, _ in outs],
        out_shape=[jax.ShapeDtypeStruct((t, w), dt) for w, dt in outs],
        compiler_params=_params("arbitrary"),
        name="inproj",
    )(x, w_norm.reshape(1, d), w_in_bf16,
      jnp.tile(q_gain, aw // HEAD_DIM).reshape(1, aw), jnp.tile(k_gain, kvw // HEAD_DIM).reshape(1, kvw),
      _block_diag_mean(aw, HEAD_DIM), _block_diag_mean(kvw, HEAD_DIM))


def _hgrn_consts(L):
    t = np.arange(L)
    u = t[None, :]
    blocks = [u <= t[:, None], u > t[:, None]]
    levels = []
    m = L // 2
    while m >= 1:
        levels.append(m)
        m //= 2
    lvl = np.full((L, L), -1, np.int32)
    lvl[t, t] = len(levels)
    isq_cols = []
    for j, m in enumerate(levels):
        bnd = (t // (2 * m)) * (2 * m) + m - 1
        isq = (t % (2 * m)) >= m
        cq = isq[:, None] & (u > bnd[:, None]) & (u <= t[:, None])
        ck = (~isq)[:, None] & (u > t[:, None]) & (u <= bnd[:, None])
        blocks.append(cq | ck)
        same = (t[:, None] // (2 * m)) == (t[None, :] // (2 * m))
        lvl[same & isq[:, None] & (~isq)[None, :]] = j
        isq_cols.append(isq)
    c = np.concatenate(blocks, axis=0).astype(np.float32)
    isq = np.stack(isq_cols, axis=1).astype(np.float32)
    isq = np.pad(isq, ((0, 0), (0, LANES - isq.shape[1])))
    c2 = np.tile(c, (1, 2))
    return jnp.asarray(c2, dtype=BF16), jnp.asarray(np.tile(lvl, (1, 2))), jnp.asarray(isq), len(levels)


def _hgrn_kernel(L, nlev, heads, dk, qr_ref, z_ref, hv_ref, hg_ref, lb_ref, og_ref, c_ref, lvl_ref, isq_ref,
                 s0_ref, y_ref, sout_ref, st_ref):
    c = pl.program_id(1)

    @pl.when(c == 0)
    def _():
        for h in range(heads):
            st_ref[h] = s0_ref[0, h].T

    for cc in range(z_ref.shape[1] // L):
        _hgrn_chunk(L, nlev, heads, dk, slice(cc * L, (cc + 1) * L), qr_ref, z_ref, hv_ref, hg_ref, lb_ref, og_ref,
                    c_ref, lvl_ref, isq_ref, y_ref, st_ref)

    @pl.when(c == pl.num_programs(1) - 1)
    def _():
        for h in range(heads):
            sout_ref[0, h] = st_ref[h].T


def _hgrn_chunk(L, nlev, heads, dk, rows, qr_ref, z_ref, hv_ref, hg_ref, lb_ref, og_ref, c_ref, lvl_ref, isq_ref,
                y_ref, st_ref):
    z = z_ref[0, rows, :]
    lb = lb_ref[...]
    e = jnp.exp(-jnp.abs(z))
    r = 1.0 / (1.0 + e)
    pos = z >= 0
    sig = jnp.where(pos, r, e * r)
    sig_neg = jnp.where(pos, e * r, r)
    logf = jnp.log(lb + (1.0 - lb) * sig)
    kin = (1.0 - lb) * sig_neg
    q = qr_ref[0, rows, :].astype(F32)

    hi, mid, _ = _split3(logf * LOG2E)
    ex = jnp.exp2(_dot(c_ref[...], jnp.concatenate([hi, mid], axis=0)))
    e_b = ex[0:L]
    e_rev = ex[L:2 * L]

    q_in = (q * e_b).astype(BF16)
    k_out = (kin * e_rev).astype(BF16)
    q_b = q.astype(BF16)
    k_b = kin.astype(BF16)
    xs = []
    for j in range(nlev):
        m = L >> (j + 1)
        if m % SUBLANES == 0:
            qk = jnp.concatenate([(q if blk % 2 else kin)[blk * m:(blk + 1) * m] for blk in range(L // m)], axis=0)
        else:
            qk = jnp.where(isq_ref[:, j:j + 1] > 0.5, q, kin)
        xs.append((qk * ex[(2 + j) * L:(3 + j) * L]).astype(BF16))
    lvl = lvl_ref[...]
    v = hv_ref[0, rows, :]
    g = hg_ref[0, rows, :].astype(F32)
    og = og_ref[...]

    def block_diag(x):
        zero = jnp.zeros((x.shape[0], dk), x.dtype)
        return jnp.concatenate([jnp.concatenate([x[:, :dk], zero], axis=1),
                                jnp.concatenate([zero, x[:, dk:]], axis=1)], axis=0)

    for pair in range(heads // 2):
        sl = slice(2 * pair * dk, 2 * (pair + 1) * dk)
        a = jnp.where(lvl == nlev, _dot_nt(q_b[:, sl], block_diag(k_b[:, sl])), 0.0)
        for j in range(nlev):
            xp = xs[j][:, sl]
            a = jnp.where(lvl == j, _dot_nt(xp, block_diag(xp)), a)
        st = jnp.concatenate([st_ref[2 * pair], st_ref[2 * pair + 1]], axis=1)
        vp = v[:, sl]
        o = _dot(a.astype(BF16), block_diag(vp)) + _dot_nt(q_in[:, sl], block_diag(st.astype(BF16)))
        for half in range(2):
            h = 2 * pair + half
            hs = slice(h * dk, (h + 1) * dk)
            st_ref[h] = st_ref[h] * e_b[L - 1:L, hs] + _dot_tn(v[:, hs], k_out[:, hs])
            oh = o[:, half * dk:(half + 1) * dk]
            ms = jnp.mean(oh * oh, axis=-1, keepdims=True)
            gh = g[:, hs]
            y_ref[0, rows, hs] = (oh * lax.rsqrt(ms + EPS) * og[:, hs] * (gh * _sigmoid(gh))).astype(BF16)


def _hgrn(qr, z, hv, hg, lb, out_gain, s0, L):
    b, s, w = z.shape
    heads, dk = s0.shape[1], s0.shape[2]
    cm, lvl, isq, nlev = _hgrn_consts(L)
    per_step = HGRN_CHUNKS_PER_STEP if s % (HGRN_CHUNKS_PER_STEP * L) == 0 else 1
    seq = pl.BlockSpec((1, per_step * L, w), lambda i, c: (i, c, 0))
    const = lambda a: pl.BlockSpec(a.shape, lambda i, c: (0,) * a.ndim)
    state = pl.BlockSpec((1, heads, dk, dk), lambda i, c: (i, 0, 0, 0))
    lb2 = lb.reshape(1, w)
    og2 = jnp.tile(out_gain, heads).reshape(1, w)
    return pl.pallas_call(
        functools.partial(_hgrn_kernel, L, nlev, heads, dk),
        grid=(b, s // (per_step * L)),
        in_specs=[seq, seq, seq, seq, const(lb2), const(og2), const(cm), const(lvl), const(isq), state],
        out_specs=[seq, state],
        out_shape=[jax.ShapeDtypeStruct((b, s, w), BF16), jax.ShapeDtypeStruct(s0.shape, F32)],
        scratch_shapes=[pltpu.VMEM((heads, dk, dk), F32)],
        compiler_params=_params("arbitrary", "arbitrary"),
        name=f"hgrn_scan_{L}",
    )(qr, z, hv, hg, lb2, og2, cm, lvl, isq, s0)


def _t5_bucket_np(rel):
    half = NUM_BUCKETS // 2
    max_exact = half // 2
    assert (NUM_BUCKETS, MAX_DISTANCE) == (32, 128)
    n = np.abs(rel).astype(np.int64)
    nn = np.maximum(n, 1)
    k = np.zeros_like(nn)
    for j in range(1, 48):
        k = np.where(64 * (1 << j) <= nn * nn, j, k)
    large = np.minimum(max_exact + k, half - 1)
    return np.where(rel > 0, half, 0) + np.where(n < max_exact, n, large)


HEADS_PER_COL = LANES // HEAD_DIM
COLS_PER_GROUP = GQA_GROUP // HEADS_PER_COL
HEAD_ORDER = tuple(g * GQA_GROUP + col * HEADS_PER_COL + half
                   for g in range(KV_HEADS) for half in range(HEADS_PER_COL) for col in range(COLS_PER_GROUP))
KV_EXPAND = KV_HEADS * HEADS_PER_COL


def _expand_kv(x):
    assert HEADS_PER_COL == 2 and KV_HEADS == 2 and x.shape[1] == LANES
    low = lax.broadcasted_iota(I32, x.shape, 1) < HEAD_DIM
    xr = pltpu.roll(x, HEAD_DIM, axis=1)
    zero = jnp.zeros_like(x)
    blocks = [jnp.where(low, x, zero), jnp.where(low, zero, xr), jnp.where(low, xr, zero), jnp.where(low, zero, x)]
    return jnp.concatenate(blocks, axis=1).astype(BF16)


def _attn_core(q, kx, vx, bias, sink):
    tq = q.shape[0]
    scores = []
    for g in range(KV_HEADS):
        cols = [q[:, (g * COLS_PER_GROUP + c) * LANES:(g * COLS_PER_GROUP + c + 1) * LANES]
                for c in range(COLS_PER_GROUP)]
        qst = jnp.concatenate(cols, axis=0)
        for half in range(HEADS_PER_COL):
            blk = g * HEADS_PER_COL + half
            scores.append(_dot_nt(qst, kx[:, blk * LANES:(blk + 1) * LANES]))
    s = jnp.concatenate(scores, axis=0) + bias
    m =jnp.maximum(jnp.max(s, axis=-1, keepdims=True), sink)
    p = jnp.exp2(s - m)
    den = jnp.sum(p, axis=-1, keepdims=True) + jnp.exp2(sink - m)
    pb = p.astype(BF16)
    rden = 1.0 / den
    rows = COLS_PER_GROUP * tq
    outs = []
    for g in range(KV_HEADS):
        o = None
        for half in range(HEADS_PER_COL):
            blk = g * HEADS_PER_COL + half
            rs = slice(blk * rows, (blk + 1) * rows)
            part = _dot(pb[rs], vx[:, blk * LANES:(blk + 1) * LANES]) * rden[rs]
            o = part if o is None else o + part
        outs.extend(o[c * tq:(c + 1) * tq] for c in range(COLS_PER_GROUP))
    return outs


def _bias_rows(table, bucket):
    onehot = (jnp.asarray(bucket)[..., None] == jnp.arange(NUM_BUCKETS)).astype(F32)
    cols = jnp.stack([table[:, h] for h in HEAD_ORDER], axis=1)
    bias = jnp.einsum('...qkb,bh->...hqk', onehot, cols, precision=lax.Precision.HIGHEST)
    return bias.reshape(*bucket.shape[:-2], ATTN_HEADS * bucket.shape[-2], bucket.shape[-1]) * LOG2E


def _pad_keys(bias):
    tk = bias.shape[-1]
    pad = [(0, 0)] * (bias.ndim - 1) + [(0, -tk % LANES)]
    return jnp.pad(bias, pad, constant_values=-jnp.inf)


def _sink_rows(sinks, tq):
    rows = jnp.repeat(jnp.stack([sinks[h] for h in HEAD_ORDER]).astype(F32), tq)
    return rows.reshape(ATTN_HEADS * tq, 1) * LOG2E


def _store_heads(ref, x):
    for g in range(KV_HEADS):
        ref[0, 0, :, g, :] = x[:, g * HEAD_DIM:(g + 1) * HEAD_DIM]


def _load_heads(ref):
    return jnp.concatenate([ref[0, 0, :, g, :] for g in range(KV_HEADS)], axis=1)


def _attn_prompt_kernel(cb, q_ref, k_ref, v_ref, km_ref, vm_ref, bias_ref, sink_ref, o_ref, nk_ref, nv_ref,
                        kx_ref, vx_ref):
    step = pl.program_id(1)
    s_len = k_ref.shape[1]
    meta_at = WINDOW + s_len

    @pl.when(step == pl.num_programs(1) - 1)
    def _():
        _store_heads(nk_ref, k_ref[0, s_len - WINDOW:, :])
        _store_heads(nv_ref, v_ref[0, s_len - WINDOW:, :])

    @pl.when(step == 0)
    def _():
        piece = min(s_len, 512)
        for src, meta, dst in ((k_ref, km_ref, kx_ref), (v_ref, vm_ref, vx_ref)):
            dst[0:WINDOW] = jnp.zeros((WINDOW, dst.shape[1]), BF16)
            for r in range(0, s_len, piece):
                dst[WINDOW + r:WINDOW + r + piece] = _expand_kv(src[0, r:r + piece, :])
            dst[meta_at:meta_at + N_META] = _expand_kv(meta[...])
            dst[meta_at + N_META:] = jnp.zeros((dst.shape[0] - meta_at - N_META, dst.shape[1]), BF16)

    win = WINDOW + CHUNK
    tail = kx_ref.shape[0] - meta_at
    for j in range(cb):
        c = step * cb + j
        start = pl.multiple_of(c * CHUNK, CHUNK)
        kall = jnp.concatenate([kx_ref[pl.ds(start, win), :], kx_ref[meta_at:meta_at + tail, :]], axis=0)
        vall = jnp.concatenate([vx_ref[pl.ds(start, win), :], vx_ref[meta_at:meta_at + tail, :]], axis=0)
        rows = slice(j * CHUNK, (j + 1) * CHUNK)
        outs = _attn_core(q_ref[0, rows, :], kall, vall, bias_ref[jnp.minimum(c, bias_ref.shape[0] - 1)],
                          sink_ref[...])
        for ci, o in enumerate(outs):
            o_ref[0, rows, ci * LANES:(ci + 1) * LANES] = o.astype(o_ref.dtype)


def _attn_prompt(q, k, v, k_meta, v_meta, table, sinks):
    b, s, aw = q.shape
    kvw = k.shape[-1]
    nc = s // CHUNK
    cb = ATTN_CHUNKS_PER_STEP if nc % ATTN_CHUNKS_PER_STEP == 0 else 1
    assert s % min(s, 512) == 0
    n_bias = 1
    while True:
        qpos = N_META + (n_bias - 1) * CHUNK
        if np.all(_t5_bucket_np(np.arange(N_META) - qpos) == _t5_bucket_np(np.arange(N_META) - qpos - 10 ** 6)):
            break
        n_bias += 1
    n_bias = min(max(n_bias, WINDOW_CHUNKS + 1), nc)
    cs = np.arange(n_bias)[:, None]
    qpos = N_META + cs * CHUNK + np.arange(CHUNK)[None]
    wpos = N_META + (cs - WINDOW_CHUNKS) * CHUNK + np.arange(WINDOW + CHUNK)[None]
    kpos = np.concatenate([wpos, np.broadcast_to(np.arange(N_META), (n_bias, N_META))], axis=1)
    valid = np.concatenate([wpos >= N_META, np.ones((n_bias, N_META), bool)], axis=1)
    bias = _bias_rows(table, _t5_bucket_np(kpos[:, None, :] - qpos[:, :, None]))
    bias = _pad_keys(jnp.where(valid[:, None, :], bias, -jnp.inf))
    sink = _sink_rows(sinks, CHUNK)
    qs = pl.BlockSpec((1, cb * CHUNK, aw), lambda i, c: (i, c, 0))
    kv = pl.BlockSpec((1, s, kvw), lambda i, c: (i, 0, 0))
    meta = pl.BlockSpec((N_META, kvw), lambda i, c: (0, 0))
    xrows = WINDOW + s + bias.shape[-1] - (WINDOW + CHUNK)
    cache_shape = (1, b, WINDOW, KV_HEADS, HEAD_DIM)
    cache = pl.BlockSpec((1, 1) + cache_shape[2:], lambda i, c: (0, i, 0, 0, 0))
    return pl.pallas_call(
        functools.partial(_attn_prompt_kernel, cb),
        grid=(b, nc // cb),
        in_specs=[qs, kv, kv, meta, meta, pl.BlockSpec(bias.shape, lambda i, c: (0, 0, 0)),
                  pl.BlockSpec(sink.shape, lambda i, c: (0, 0))],
        out_specs=[qs, cache, cache],
        out_shape=[jax.ShapeDtypeStruct((b, s, aw), BF16)] + [jax.ShapeDtypeStruct(cache_shape, F32)] * 2,
        scratch_shapes=[pltpu.VMEM((xrows, KV_EXPAND * LANES), BF16), pltpu.VMEM((xrows, KV_EXPAND * LANES), BF16)],
        compiler_params=_params("arbitrary", "arbitrary"),
        name="attn_prompt",
    )(q, k, v, k_meta, v_meta, bias, sink)


def _attn_sample_kernel(q_ref, kc_ref, vc_ref, kn_ref, vn_ref, km_ref, vm_ref, bias_ref, sink_ref,
                        o_ref, nk_ref, nv_ref):
    tk = kc_ref.shape[2] + kn_ref.shape[1] + km_ref.shape[0]
    zeros = jnp.zeros((bias_ref.shape[1] - tk, km_ref.shape[1]), F32)
    kc, vc, kn, vn = _load_heads(kc_ref), _load_heads(vc_ref), kn_ref[0], vn_ref[0]
    kall = _expand_kv(jnp.concatenate([kc, kn, km_ref[...], zeros], axis=0))
    vall = _expand_kv(jnp.concatenate([vc, vn, vm_ref[...], zeros], axis=0))
    outs = _attn_core(q_ref[0], kall, vall, bias_ref[...], sink_ref[...])
    for ci, o in enumerate(outs):
        o_ref[0, :, ci * LANES:(ci + 1) * LANES] = o.astype(o_ref.dtype)
    n_new = kn.shape[0]
    _store_heads(nk_ref, jnp.concatenate([kc[n_new:], kn], axis=0))
    _store_heads(nv_ref, jnp.concatenate([vc[n_new:], vn], axis=0))


def _attn_sample(q, k_cache, v_cache, k_new, v_new, k_meta, v_meta, table, sinks):
    bd, sd, aw = q.shape
    kvw = k_new.shape[-1]
    win = k_cache.shape[2]
    cache = pl.BlockSpec((1, 1) + k_cache.shape[2:], lambda i: (0, i, 0, 0, 0))
    qpos = N_META + PAST_LEN + np.arange(sd)
    kpos = np.concatenate([N_META + PAST_LEN - win + np.arange(win), qpos, np.arange(N_META)])
    bias = _pad_keys(_bias_rows(table, _t5_bucket_np(kpos[None, :] - qpos[:, None])))
    sink = _sink_rows(sinks, sd)
    per = lambda n, w: pl.BlockSpec((1, n, w), lambda i: (i, 0, 0))
    meta = pl.BlockSpec((N_META, kvw), lambda i: (0, 0))
    return pl.pallas_call(
        _attn_sample_kernel,
        grid=(bd,),
        in_specs=[per(sd, aw), cache, cache, per(sd, kvw), per(sd, kvw), meta, meta,
                  pl.BlockSpec(bias.shape, lambda i: (0, 0)), pl.BlockSpec(sink.shape, lambda i: (0, 0))],
        out_specs=[per(sd, aw), cache, cache],
        out_shape=[jax.ShapeDtypeStruct((bd, sd, aw), BF16)] + [jax.ShapeDtypeStruct(k_cache.shape, F32)] * 2,
        compiler_params=_params("arbitrary"),
        name="attn_sample",
    )(q, k_cache, v_cache, k_new, v_new, k_meta, v_meta, bias, sink)


ROUTE_E1, ROUTE_E2, ROUTE_R1, ROUTE_R2, ROUTE_W1, ROUTE_W2 = range(6)
META_FIELDS = 2 * TOP_K


def _merge_kernel(hw, x_ref, yr_ref, at_ref, sga_ref, sgb_ref, wb_ref, wo_ref, wn_ref, wr_ref, br_ref, cnt0_ref, tri_ref,
                  h_ref, xn_ref, route_ref, meta_ref, cnt_ref, carry_ref):
    i = pl.program_id(0)

    @pl.when(i == 0)
    def _():
        carry_ref[...] = cnt0_ref[...]

    n = x_ref.shape[0]
    sub = MERGE_SUBTILE if n % MERGE_SUBTILE == 0 else n
    for r0 in range(0, n, sub):
        _merge_rows(hw, r0, sub, x_ref, yr_ref, at_ref, sga_ref, sgb_ref, wb_ref, wo_ref, wn_ref, wr_ref, br_ref,
                    tri_ref, h_ref, xn_ref, route_ref, meta_ref, carry_ref)
    cnt_ref[...] = carry_ref[...]


def _merge_rows(hw, r0, tm, x_ref, yr_ref, at_ref, sga_ref, sgb_ref, wb_ref, wo_ref, wn_ref, wr_ref, br_ref, tri_ref,
                h_ref, xn_ref, route_ref, meta_ref, carry_ref):
    rows = slice(r0, r0 + tm)
    br = _dot(yr_ref[rows, :], wb_ref[0:hw, :])
    ba = _dot(at_ref[rows, :], wb_ref[hw:, :])
    merged = sga_ref[rows, :].astype(F32) * br + sgb_ref[rows, :].astype(F32) * ba
    h = x_ref[rows, :] + _dot(merged.astype(BF16), wo_ref[...])
    h_ref[rows, :] = h
    ms = jnp.mean(h * h, axis=-1, keepdims=True)
    xn = h * lax.rsqrt(ms + EPS) * wn_ref[...]
    per_row = xn.shape[1] // LANES
    _store_row_tiles(xn_ref.at[r0 * per_row:(r0 + tm) * per_row], xn)

    x_hi = xn.astype(BF16)
    x_lo = (xn - x_hi.astype(F32)).astype(BF16)
    both = _dot(x_hi, wr_ref[...])
    logits = both[:, :LANES] + both[:, LANES:] + _dot(x_lo, wr_ref[:, :LANES]) + br_ref[...]

    lane_i =lax.broadcasted_iota(I32, (tm, LANES), 1)
    lane = lane_i.astype(F32)
    group_of_lane = (lane_i >> int(math.log2(EXPERTS_PER_GROUP))).astype(F32)
    ninf = -jnp.inf
    first = lambda hit, idx: jnp.min(jnp.where(hit, idx, float(LANES)), axis=-1, keepdims=True)
    gmask = (lane_i >= N_EXPERTS) & (lane_i < N_EXPERTS + N_GROUPS)
    gl = jnp.where(gmask, logits, ninf)
    gmax = jnp.max(gl, axis=-1, keepdims=True)
    gidx = first(gl == gmax, lane - N_EXPERTS)
    gval = 1.0 / jnp.sum(jnp.exp(gl - gmax), axis=-1, keepdims=True)
    emask = (lane_i < N_EXPERTS) & (group_of_lane == gidx)
    el = jnp.where(emask, logits, ninf)
    m1 = jnp.max(el, axis=-1, keepdims=True)
    i1 = first(el == m1, lane)
    el2 = jnp.where(lane == i1, ninf, el)
    m2 = jnp.max(el2, axis=-1, keepdims=True)
    i2 = first(el2 == m2, lane)
    e21 = jnp.exp(m2 - m1)
    w1 = gval / (1.0 + e21)
    w2 = gval * e21 / (1.0 + e21)

    sel1 = lane == i1
    sel2 = lane == i2
    oh = (sel1 | sel2).astype(BF16)
    before = _dot(tri_ref[...], oh) + carry_ref[...]
    r1 = jnp.sum(jnp.where(sel1, before, 0.0), axis=-1, keepdims=True)
    r2 = jnp.sum(jnp.where(sel2, before, 0.0), axis=-1, keepdims=True)
    carry_ref[...] = carry_ref[...] + jnp.sum(oh.astype(F32), axis=0, keepdims=True)

    rec = jnp.zeros((tm, LANES), F32)
    for slot, val in ((ROUTE_E1, i1), (ROUTE_E2, i2), (ROUTE_W1, w1), (ROUTE_W2, w2), (ROUTE_R1, r1), (ROUTE_R2, r2)):
        rec = jnp.where(lane_i == slot, val, rec)
    route_ref[rows, :] = rec
    meta_ref[0, :, rows] = rec.T[ROUTE_E1:ROUTE_E1 + META_FIELDS].astype(I32)


def _merge(x, y_rec, att, sga, sgb, wb, wo, w_norm, w_router, b_router, cnt0):
    t, d = x.shape
    hw = y_rec.shape[1]
    tm = PROJ_TILE if t % PROJ_TILE == 0 else t
    sub = MERGE_SUBTILE if tm % MERGE_SUBTILE == 0 else tm
    row = lambda w: pl.BlockSpec((tm, w), lambda i: (i, 0))
    const = lambda a, b: pl.BlockSpec((a, b), lambda i: (0, 0))
    return pl.pallas_call(
        functools.partial(_merge_kernel, hw),
        grid=(t // tm,),
        in_specs=[row(d), row(hw), row(att.shape[1]), row(d), row(d), const(*wb.shape), const(d, d), const(1, d),
                  const(d, 2 * LANES), const(1, LANES), const(1, LANES), const(sub, sub)],
        out_specs=[row(d), pl.BlockSpec((tm * d // LANES, LANES), lambda i: (i, 0)), row(LANES),
                   pl.BlockSpec((1, META_FIELDS, tm), lambda i: (i, 0, 0)), const(1, LANES)],
        out_shape=[jax.ShapeDtypeStruct((t, d), F32), jax.ShapeDtypeStruct((t * d // LANES, LANES), F32),
                   jax.ShapeDtypeStruct((t, LANES), F32), jax.ShapeDtypeStruct((t // tm, META_FIELDS, tm), I32),
                   jax.ShapeDtypeStruct((1, LANES), F32)],
        scratch_shapes=[pltpu.VMEM((1, LANES), F32)],
        compiler_params=_params("arbitrary"),
        name="merge_route",
    )(x, y_rec, att, sga, sgb, wb, wo, w_norm.reshape(1, d), w_router, b_router, cnt0,
      jnp.asarray(np.tri(sub, k=-1), dtype=BF16))


def _row_copy(src, dst, sem):
    return pltpu.make_async_copy(src, dst, sem)


ROW_UNROLL = 8


def _tile_rows(ref, row, sub):
    return ref.at[pl.ds(pl.multiple_of(row * sub, sub), sub)]


def _scatter_kernel(n_tok, sub, first, slot_ref, pstart_ref, pend_ref, x_ref, *rest):
    xs_ref, zero_ref, sem, zsem, tsem = rest[-5:]

    def zero_blocks(blocks, zs, wait):
        for cond, row in blocks:
            @pl.when(cond)
            def _():
                at = row * sub if isinstance(row, int) else pl.multiple_of(row * sub, EXPERT_TILE * sub)
                cp = _row_copy(zero_ref, xs_ref.at[pl.ds(at, EXPERT_TILE * sub)], zs)
                cp.wait() if wait else cp.start()

    if first:
        n_rows = xs_ref.shape[0] // sub
        tails = [(pend_ref[e] > pstart_ref[e], pend_ref[e] - EXPERT_TILE) for e in range(N_EXPERTS)]
        unused = [(n_rows - (j + 1) * EXPERT_TILE >= pend_ref[N_EXPERTS - 1], n_rows - (j + 1) * EXPERT_TILE)
                  for j in range(N_EXPERTS)]

        @pl.when(pl.program_id(0) == 0)
        def _():
            zero_ref[...] = jnp.zeros_like(zero_ref)
            zero_blocks(tails, zsem, False)
            zero_blocks(unused, tsem, False)
            zero_blocks(tails, zsem, True)

    def issue(grp, _):
        for u in range(ROW_UNROLL):
            r = grp * ROW_UNROLL + u
            for k in range(TOP_K):
                _row_copy(_tile_rows(x_ref, r, sub), _tile_rows(xs_ref, slot_ref[0, k, r], sub),
                          sem).start(priority=(u * TOP_K + k) % 2)
        return 0

    lax.fori_loop(0, n_tok // ROW_UNROLL, issue, 0)
    for k in range(TOP_K):
        _row_copy(x_ref, xs_ref.at[pl.ds(0, n_tok * sub)], sem).wait()

    if first:
        @pl.when(pl.program_id(0) == 0)
        def _():
            zero_blocks(unused, tsem, True)


def _scatter_rows(x, slots, pstarts, pends, xs, rows):
    n, fields, tm = slots.shape
    sub = x.shape[0] // (n * tm)
    first = xs is None
    smem = pl.BlockSpec(memory_space=pltpu.SMEM)
    return pl.pallas_call(
        functools.partial(_scatter_kernel, tm, sub, first),
        grid=(n,),
        in_specs=[pl.BlockSpec((1, fields, tm), lambda i: (i, 0, 0), memory_space=pltpu.SMEM), smem, smem,
                  pl.BlockSpec((tm * sub, LANES), lambda i: (i, 0))]
        + ([] if first else [pl.BlockSpec(memory_space=pl.ANY)]),
        out_specs=pl.BlockSpec(memory_space=pl.ANY),
        out_shape=jax.ShapeDtypeStruct((rows * sub, LANES), F32),
        scratch_shapes=[pltpu.VMEM((EXPERT_TILE * sub, LANES), F32)] + [pltpu.SemaphoreType.DMA(())] * 3,
        input_output_aliases={} if first else {4: 0},
        compiler_params=_params("arbitrary"),
        name="moe_scatter",
    )(slots, pstarts, pends, x, *([] if first else [xs]))


def _expert_kernel(tm, sub, be_ref, nb_ref, x_ref, wg_ref, wu_ref, wd_ref, y_ref, wgb_ref, wub_ref, wdb_ref):
    i = pl.program_id(0)
    used = i < nb_ref[0]

    @pl.when(used & ((i == 0) | (be_ref[i] != be_ref[jnp.maximum(i - 1, 0)])))
    def _():
        wgb_ref[...] = wg_ref[0].astype(BF16)
        wub_ref[...] = wu_ref[0].astype(BF16)
        wdb_ref[...] = wd_ref[0].astype(BF16)

    @pl.when(used)
    def _():
        x = _load_row_tiles(x_ref, tm, sub).astype(BF16)
        g = _dot(x, wgb_ref[...])
        u = _dot(x, wub_ref[...])
        hmid = (g * _sigmoid(g) * u).astype(BF16)
        _store_row_tiles(y_ref, _dot(hmid, wdb_ref[...]))

    @pl.when(jnp.logical_not(used))
    def _():
        y_ref[...] = jnp.zeros_like(y_ref)


def _experts(xs, block_e, n_used, wg, wu, wd):
    _, d, ff = wg.shape
    sub = d // LANES
    tm = EXPERT_TILE
    blk = pl.BlockSpec((tm * sub, LANES), lambda i, be, nb: (i, 0))
    grid_spec = pltpu.PrefetchScalarGridSpec(
        num_scalar_prefetch=2,
        grid=(xs.shape[0] // (tm * sub),),
        in_specs=[blk,
                  pl.BlockSpec((1, d, ff), lambda i, be, nb: (be[i], 0, 0)),
                  pl.BlockSpec((1, d, ff), lambda i, be, nb: (be[i], 0, 0)),
                  pl.BlockSpec((1, ff, d), lambda i, be, nb: (be[i], 0, 0))],
        out_specs=blk,
        scratch_shapes=[pltpu.VMEM((d, ff), BF16), pltpu.VMEM((d, ff), BF16), pltpu.VMEM((ff, d), BF16)],
    )
    return pl.pallas_call(
        functools.partial(_expert_kernel, tm, sub),
        grid_spec=grid_spec,
        out_shape=jax.ShapeDtypeStruct(xs.shape, F32),
        compiler_params=_params("arbitrary"),
        name="moe_experts",
    )(block_e, n_used, xs, wg, wu, wd)


def _combine_kernel(n_tok, sub, scur_ref, snext_ref, h_ref, route_ref, ys_ref, o_ref, buf, sem):
    i = pl.program_id(0)
    n = pl.num_programs(0)
    slot = i % 2

    def start(slot_ref, s):
        def issue(grp, _):
            for u in range(ROW_UNROLL):
                r = grp * ROW_UNROLL + u
                for k in range(TOP_K):
                    _row_copy(_tile_rows(ys_ref, slot_ref[0, k, r], sub), _tile_rows(buf.at[s, k], r, sub),
                              sem.at[s]).start(priority=(u * TOP_K + k) % 2)
            return 0
        lax.fori_loop(0, n_tok // ROW_UNROLL, issue, 0)

    @pl.when(i == 0)
    def _():
        start(scur_ref, 0)

    @pl.when(i + 1 < n)
    def _():
        start(snext_ref, 1 - slot)

    for k in range(TOP_K):
        _row_copy(ys_ref.at[pl.ds(0, n_tok * sub)], buf.at[slot, k], sem.at[slot]).wait()
    route = route_ref[...]
    w1 = route[:, ROUTE_W1:ROUTE_W1 + 1]
    w2 = route[:, ROUTE_W2:ROUTE_W2 + 1]
    for j in range(sub):
        cols = slice(j * LANES, (j + 1) * LANES)
        part = lambda k: buf[slot, k, pl.ds(j, n_tok, stride=sub), :]
        o_ref[:, cols] = h_ref[:, cols] + (part(0) * w1 + part(1) * w2)


def _combine(h, route, slots, ys):
    t, d = h.shape
    n, fields, tm = slots.shape
    sub = d // LANES
    mspec = lambda f: pl.BlockSpec((1, fields, tm), f, memory_space=pltpu.SMEM)
    return pl.pallas_call(
        functools.partial(_combine_kernel, tm, sub),
        grid=(n,),
        in_specs=[mspec(lambda i: (i, 0, 0)), mspec(lambda i: (jnp.minimum(i + 1, n - 1), 0, 0)),
                  pl.BlockSpec((tm, d), lambda i: (i, 0)), pl.BlockSpec((tm, LANES), lambda i: (i, 0)),
                  pl.BlockSpec(memory_space=pl.ANY)],
        out_specs=pl.BlockSpec((tm, d), lambda i: (i, 0)),
        out_shape=jax.ShapeDtypeStruct((t, d), F32),
        scratch_shapes=[pltpu.VMEM((2, TOP_K, tm * sub, LANES), F32), pltpu.SemaphoreType.DMA((2,))],
        compiler_params=_params("arbitrary"),
        name="moe_combine",
    )(slots, slots, h, route, ys)


def _moe(parts, counts, wg, wu, wd):
    tm = EXPERT_TILE
    n_assign = sum(part[0].shape[0] for part in parts) * TOP_K
    n_blocks = -(-(n_assign + N_EXPERTS * (tm - 1)) // tm)
    counts = counts[0, :N_EXPERTS].astype(I32)
    pcounts = (counts + tm - 1) // tm * tm
    pends = jnp.cumsum(pcounts)
    pstarts = pends - pcounts
    block_start = jnp.arange(n_blocks, dtype=I32) * tm
    block_e = jnp.minimum(jnp.sum((pends[None, :] <= block_start[:, None]).astype(I32), axis=1), N_EXPERTS - 1)
    n_used = pends[-1:] // tm
    xs = None
    slots = []
    for _, xn, _, meta in parts:
        assert meta.shape[-1] % ROW_UNROLL == 0
        e, pos = meta[:, :TOP_K], meta[:, TOP_K:]
        seg = jnp.sum(jnp.where(e[..., None] == jnp.arange(N_EXPERTS, dtype=I32), pstarts, 0), axis=-1)
        slots.append(seg + pos)
        xs = _scatter_rows(xn, slots[-1], pstarts, pends, xs, n_blocks * tm)
    ys = _experts(xs, block_e, n_used, wg, wu, wd)
    return [_combine(h, route, s, ys) for (h, _, route, _), s in zip(parts, slots)]


def kernel(x_prompt, x_sample, cache_swa_k, cache_swa_v, state_hgrn, meta_tokens, rel_bias_table, hgrn_lower_bounds, w_norm_mix, w_in, hgrn_out_norm, q_norm, k_norm, attn_sinks, w_branch, w_out, w_norm_ffn, w_router_group, b_router_group, w_router_expert, b_router_expert, w_expert_gate, w_expert_up, w_expert_down):
    b, s, d = x_prompt.shape
    bd, sd, _ = x_sample.shape
    depth, _, heads, dk, dv = state_hgrn.shape
    assert depth == 1 and heads == HGRN_HEADS and dk == dv
    hw = heads * dk
    aw = ATTN_HEADS * HEAD_DIM
    kvw = KV_HEADS * HEAD_DIM
    assert w_in.shape[-1] == 4 * hw + aw + 2 * kvw + 2 * d
    assert s % HGRN_CHUNK == 0 and s % CHUNK == 0 and sd == N_META and N_EXPERTS + N_GROUPS <= LANES
    l = 0

    p = jax.nn.softmax(hgrn_lower_bounds.astype(F32), axis=0)
    lb = jnp.cumsum(p, axis=0)[l + 1] - p[0]

    w_in_b = w_in[l].astype(BF16)
    proj = functools.partial(_inproj, w_norm=w_norm_mix[l], w_in_bf16=w_in_b, q_gain=q_norm[l], k_gain=k_norm[l],
                             hw=hw, aw=aw, kvw=kvw)
    x_small = jnp.concatenate([x_sample.reshape(bd * sd, d), meta_tokens.astype(F32)], axis=0)
    qr_s, z_s, hv_s, hg_s, qa_s, k_s, v_s, sga_s, sgb_s = proj(x_small)
    qr_p, z_p, hv_p, hg_p, qa_p, k_p, v_p, sga_p, sgb_p = proj(x_prompt.reshape(b * s, d))
    ns = bd * sd
    k_meta, v_meta = k_s[ns:], v_s[ns:]

    streams = lambda a, n: a.reshape(n, -1, a.shape[-1])
    s0_small = jnp.concatenate([state_hgrn[l].astype(F32), jnp.zeros((1, heads, dk, dv), F32)], axis=0)
    y_small, st_small = _hgrn(streams(qr_s, bd + 1), streams(z_s, bd + 1), streams(hv_s, bd + 1),
                              streams(hg_s, bd + 1), lb, hgrn_out_norm[l], s0_small, sd)
    s0_p = jnp.broadcast_to(st_small[bd:], (b, heads, dk, dv))
    y_p, st_p = _hgrn(streams(qr_p, b), streams(z_p, b), streams(hv_p, b), streams(hg_p, b), lb,
                      hgrn_out_norm[l], s0_p, HGRN_CHUNK)

    table = rel_bias_table.astype(F32)
    att_p, new_k_p, new_v_p = _attn_prompt(streams(qa_p, b), streams(k_p, b), streams(v_p, b), k_meta, v_meta,
                                           table, attn_sinks[l])
    k_new, v_new = k_s[:ns].reshape(bd, sd, kvw), v_s[:ns].reshape(bd, sd, kvw)
    att_s, new_k_s, new_v_s = _attn_sample(qa_s[:ns].reshape(bd, sd, aw), cache_swa_k[l:l + 1].astype(F32),
                                           cache_swa_v[l:l + 1].astype(F32), k_new, v_new, k_meta, v_meta,
                                           table, attn_sinks[l])

    wb = w_branch[l].astype(BF16)
    wo = w_out[l].astype(BF16)
    w_router = jnp.pad(jnp.concatenate([w_router_expert[l], w_router_group[l]], axis=1).astype(F32),
                       ((0, 0), (0, LANES - N_EXPERTS - N_GROUPS)))
    b_router = jnp.pad(jnp.concatenate([b_router_expert[l], b_router_group[l]]).astype(F32),
                       (0, LANES - N_EXPERTS - N_GROUPS)).reshape(1, LANES)
    w_router_hi = w_router.astype(BF16)
    w_router_lo = (w_router - w_router_hi.astype(F32)).astype(BF16)
    w_router = jnp.concatenate([w_router_hi, w_router_lo], axis=1)
    merge = functools.partial(_merge, wb=wb, wo=wo, w_norm=w_norm_ffn[l], w_router=w_router, b_router=b_router)
    *part_p, cnt_p = merge(x_prompt.reshape(b * s, d), y_p.reshape(b * s, hw), att_p.reshape(b * s, aw),
                           sga_p, sgb_p, cnt0=jnp.zeros((1, LANES), F32))
    *part_s, cnt_s = merge(x_sample.reshape(ns, d), y_small[:bd].reshape(ns, hw), att_s.reshape(ns, aw),
                           sga_s[:ns], sgb_s[:ns], cnt0=cnt_p)

    out_p, out_s = _moe([part_p, part_s], cnt_s, w_expert_gate[l], w_expert_up[l], w_expert_down[l])

    return (out_p.reshape(b, s, d), out_s.reshape(bd, sd, d), new_k_p, new_v_p, st_p[None],
            new_k_s, new_v_s, st_small[:bd][None])
```

```python
import functools
import math

import numpy as np
import jax
import jax.numpy as jnp
from jax import lax
from jax.experimental import pallas as pl
from jax.experimental.pallas import tpu as pltpu

F32 = jnp.float32
BF16 = jnp.bfloat16
I32 = jnp.int32

CHUNK = 64
N_META = 16
PAST_LEN = 2048
EPS = 1e-6
HGRN_HEADS = 4
ATTN_HEADS = 8
KV_HEADS = 2
HEAD_DIM = 64
GQA_GROUP = ATTN_HEADS // KV_HEADS
WINDOW = 128
WINDOW_CHUNKS = WINDOW // CHUNK
NUM_BUCKETS = 32
MAX_DISTANCE = 128
N_GROUPS = 4
EXPERTS_PER_GROUP = 8
N_EXPERTS = N_GROUPS * EXPERTS_PER_GROUP
TOP_K = 2

LANES = 128
VMEM_LIMIT = 56 * 1024 * 1024

INPROJ_TILE = 512
PROJ_TILE = 512
HGRN_CHUNK = 128
HGRN_CHUNKS_PER_STEP = 8
ATTN_CHUNKS_PER_STEP = 8
EXPERT_TILE = 512


LOG2E = math.log2(math.e)


def _sigmoid(x):
    return 1.0 / (1.0 + jnp.exp(-x))


def _split3(x):
    hi = x.astype(BF16)
    r1 = x - hi.astype(F32)
    mid = r1.astype(BF16)
    lo = (r1 - mid.astype(F32)).astype(BF16)
    return hi, mid, lo


def _dot(a, b):
    return jnp.dot(a, b, preferred_element_type=F32)


def _dot_nt(a, b):
    return lax.dot_general(a, b, (((1,), (1,)), ((), ())), preferred_element_type=F32)


def _dot_tn(a, b):
    return lax.dot_general(a, b, (((0,), (0,)), ((), ())), preferred_element_type=F32)


SUBLANES = 8


def _store_row_tiles(ref, x):
    n, d = x.shape
    sub = d // LANES
    for j in range(sub):
        ref[pl.ds(j, n, stride=sub), :] = x[:, j * LANES:(j + 1) * LANES]


def _load_row_tiles(ref, n, sub):
    return jnp.concatenate([ref[pl.ds(j, n, stride=sub), :] for j in range(sub)], axis=1)


def _params(*sem):
    return pltpu.CompilerParams(dimension_semantics=sem, vmem_limit_bytes=VMEM_LIMIT)


def _inproj_kernel(hw, aw, kvw, d, x_ref, wn_ref, w_ref, qg_ref, kg_ref, bdq_ref, bdk_ref,
                   qr_ref, z_ref, hv_ref, hg_ref, qa_ref, k_ref, v_ref, sga_ref, sgb_ref):
    x = x_ref[...]
    ms = jnp.mean(x * x, axis=-1, keepdims=True)
    xn = (x * lax.rsqrt(ms + EPS) * wn_ref[...]).astype(BF16)

    def seg(a, b):
        return _dot(xn, w_ref[:, a:b])

    def head_rms(a, bd_ref, gain):
        sq = a * a
        hi = sq.astype(BF16)
        lo = (sq - hi.astype(F32)).astype(BF16)
        m = _dot(hi, bd_ref[...]) + _dot(lo, bd_ref[...])
        return a * lax.rsqrt(m + EPS) * gain

    o = 0
    hq = seg(o, o + hw)
    qr_ref[...] = (hq * _sigmoid(hq) * (hw // HGRN_HEADS) ** -0.5).astype(BF16)
    o += hw
    z_ref[...] = seg(o, o + hw)
    o += hw
    hv_ref[...] = seg(o, o + hw).astype(BF16)
    o += hw
    hg_ref[...] = seg(o, o + hw).astype(BF16)
    o += hw
    aq = seg(o, o + aw)
    qa_ref[...] = (head_rms(aq, bdq_ref, qg_ref[...]) * (HEAD_DIM ** -0.5 * LOG2E)).astype(BF16)
    o += aw
    k_ref[...] = head_rms(seg(o, o + kvw), bdk_ref, kg_ref[...])
    o += kvw
    v_ref[...] = seg(o, o + kvw)
    o += kvw
    sga_ref[...] = _sigmoid(seg(o, o + d)).astype(BF16)
    o += d
    sgb_ref[...] = _sigmoid(seg(o, o + d)).astype(BF16)


def _block_diag_mean(width, group):
    i = np.arange(width)
    return jnp.asarray((i[:, None] // group == i[None, :] // group) / group, dtype=BF16)


def _inproj(x, w_norm, w_in_bf16, q_gain, k_gain, hw, aw, kvw):
    t, d = x.shape
    tm = INPROJ_TILE if t % INPROJ_TILE == 0 else t
    cols = w_in_bf16.shape[1]
    row = lambda w: pl.BlockSpec((tm, w), lambda i: (i, 0))
    const = lambda a, b: pl.BlockSpec((a, b), lambda i: (0, 0))
    outs = [(hw, BF16), (hw, F32), (hw, BF16), (hw, BF16), (aw, BF16), (kvw, F32), (kvw, F32), (d, BF16), (d, BF16)]
    return pl.pallas_call(
        functools.partial(_inproj_kernel, hw, aw, kvw, d),
        grid=(t // tm,),
        in_specs=[row(d), const(1, d), const(d, cols), const(1, aw), const(1, kvw), const(aw, aw), const(kvw, kvw)],
        out_specs=[row(w) for w, _ in outs],
        out_shape=[jax.ShapeDtypeStruct((t, w), dt) for w, dt in outs],
        compiler_params=_params("arbitrary"),
        name="inproj",
    )(x, w_norm.reshape(1, d), w_in_bf16,
      jnp.tile(q_gain, aw // HEAD_DIM).reshape(1, aw), jnp.tile(k_gain, kvw // HEAD_DIM).reshape(1, kvw),
      _block_diag_mean(aw, HEAD_DIM), _block_diag_mean(kvw, HEAD_DIM))


def _hgrn_consts(L):
    t = np.arange(L)
    u = t[None, :]
    blocks = [u <= t[:, None], u > t[:, None]]
    levels = []
    m = L // 2
    while m >= 1:
        levels.append(m)
        m //= 2
    lvl = np.full((L, L), -1, np.int32)
    lvl[t, t] = len(levels)
    isq_cols = []
    for j, m in enumerate(levels):
        bnd = (t // (2 * m)) * (2 * m) + m - 1
        isq = (t % (2 * m)) >= m
        cq = isq[:, None] & (u > bnd[:, None]) & (u <= t[:, None])
        ck = (~isq)[:, None] & (u > t[:, None]) & (u <= bnd[:, None])
        blocks.append(cq | ck)
        same = (t[:, None] // (2 * m)) == (t[None, :] // (2 * m))
        lvl[same & isq[:, None] & (~isq)[None, :]] = j
        isq_cols.append(isq)
    c = np.concatenate(blocks, axis=0).astype(np.float32)
    isq = np.stack(isq_cols, axis=1).astype(np.float32)
    isq = np.pad(isq, ((0, 0), (0, LANES - isq.shape[1])))
    c2 = np.tile(c, (1, 2))
    return jnp.asarray(c2, dtype=BF16), jnp.asarray(np.tile(lvl, (1, 2))), jnp.asarray(isq), len(levels)


def _hgrn_kernel(L, nlev, heads, dk, qr_ref, z_ref, hv_ref, hg_ref, lb_ref, og_ref, c_ref, lvl_ref, isq_ref,
                 s0_ref, y_ref, sout_ref, st_ref):
    c = pl.program_id(1)

    @pl.when(c == 0)
    def _():
        for h in range(heads):
            st_ref[h] = s0_ref[0, h].T

    for cc in range(z_ref.shape[1] // L):
        _hgrn_chunk(L, nlev, heads, dk, slice(cc * L, (cc + 1) * L), qr_ref, z_ref, hv_ref, hg_ref, lb_ref, og_ref,
                    c_ref, lvl_ref, isq_ref, y_ref, st_ref)

    @pl.when(c == pl.num_programs(1) - 1)
    def _():
        for h in range(heads):
            sout_ref[0, h] = st_ref[h].T


def _hgrn_chunk(L, nlev, heads, dk, rows, qr_ref, z_ref, hv_ref, hg_ref, lb_ref, og_ref, c_ref, lvl_ref, isq_ref,
                y_ref, st_ref):
    z = z_ref[0, rows, :]
    lb = lb_ref[...]
    e = jnp.exp(-jnp.abs(z))
    r = 1.0 / (1.0 + e)
    pos = z >= 0
    sig = jnp.where(pos, r, e * r)
    sig_neg = jnp.where(pos, e * r, r)
    logf = jnp.log(lb + (1.0 - lb) * sig)
    kin = (1.0 - lb) * sig_neg
    q = qr_ref[0, rows, :].astype(F32)

    hi, mid, _ = _split3(logf * LOG2E)
    ex = jnp.exp2(_dot(c_ref[...], jnp.concatenate([hi, mid], axis=0)))
    e_b = ex[0:L]
    e_rev = ex[L:2 * L]

    q_in = (q * e_b).astype(BF16)
    k_out = (kin * e_rev).astype(BF16)
    q_b = q.astype(BF16)
    k_b = kin.astype(BF16)
    xs = []
    for j in range(nlev):
        m = L >> (j + 1)
        if m % SUBLANES == 0:
            qk = jnp.concatenate([(q if blk % 2 else kin)[blk * m:(blk + 1) * m] for blk in range(L // m)], axis=0)
        else:
            qk = jnp.where(isq_ref[:, j:j + 1] > 0.5, q, kin)
        xs.append((qk * ex[(2 + j) * L:(3 + j) * L]).astype(BF16))
    lvl = lvl_ref[...]
    v = hv_ref[0, rows, :]
    g = hg_ref[0, rows, :].astype(F32)
    og = og_ref[...]

    def block_diag(x):
        zero = jnp.zeros((x.shape[0], dk), x.dtype)
        return jnp.concatenate([jnp.concatenate([x[:, :dk], zero], axis=1),
                                jnp.concatenate([zero, x[:, dk:]], axis=1)], axis=0)

    for pair in range(heads // 2):
        sl = slice(2 * pair * dk, 2 * (pair + 1) * dk)
        a = jnp.where(lvl == nlev, _dot_nt(q_b[:, sl], block_diag(k_b[:, sl])), 0.0)
        for j in range(nlev):
            xp = xs[j][:, sl]
            a = jnp.where(lvl == j, _dot_nt(xp, block_diag(xp)), a)
        st = jnp.concatenate([st_ref[2 * pair], st_ref[2 * pair + 1]], axis=1)
        vp = v[:, sl]
        o = _dot(a.astype(BF16), block_diag(vp)) + _dot_nt(q_in[:, sl], block_diag(st.astype(BF16)))
        for half in range(2):
            h = 2 * pair + half
            hs = slice(h * dk, (h + 1) * dk)
            st_ref[h] = st_ref[h] * e_b[L - 1:L, hs] + _dot_tn(v[:, hs], k_out[:, hs])
            oh = o[:, half * dk:(half + 1) * dk]
            ms = jnp.mean(oh * oh, axis=-1, keepdims=True)
            gh = g[:, hs]
            y_ref[0, rows, hs] = (oh * lax.rsqrt(ms + EPS) * og[:, hs] * (gh * _sigmoid(gh))).astype(BF16)


def _hgrn(qr, z, hv, hg, lb, out_gain, s0, L):
    b, s, w = z.shape
    heads, dk = s0.shape[1], s0.shape[2]
    cm, lvl, isq, nlev = _hgrn_consts(L)
    per_step = HGRN_CHUNKS_PER_STEP if s % (HGRN_CHUNKS_PER_STEP * L) == 0 else 1
    seq = pl.BlockSpec((1, per_step * L, w), lambda i, c: (i, c, 0))
    const = lambda a: pl.BlockSpec(a.shape, lambda i, c: (0,) * a.ndim)
    state = pl.BlockSpec((1, heads, dk, dk), lambda i, c: (i, 0, 0, 0))
    lb2 = lb.reshape(1, w)
    og2 = jnp.tile(out_gain, heads).reshape(1, w)
    return pl.pallas_call(
        functools.partial(_hgrn_kernel, L, nlev, heads, dk),
        grid=(b, s // (per_step * L)),
        in_specs=[seq, seq, seq, seq, const(lb2), const(og2), const(cm), const(lvl), const(isq), state],
        out_specs=[seq, state],
        out_shape=[jax.ShapeDtypeStruct((b, s, w), BF16), jax.ShapeDtypeStruct(s0.shape, F32)],
        scratch_shapes=[pltpu.VMEM((heads, dk, dk), F32)],
        compiler_params=_params("arbitrary", "arbitrary"),
        name=f"hgrn_scan_{L}",
    )(qr, z, hv, hg, lb2, og2, cm, lvl, isq, s0)


def _t5_bucket_np(rel):
    half = NUM_BUCKETS // 2
    max_exact = half // 2
    assert (NUM_BUCKETS, MAX_DISTANCE) == (32, 128)
    n = np.abs(rel).astype(np.int64)
    nn = np.maximum(n, 1)
    k = np.zeros_like(nn)
    for j in range(1, 48):
        k = np.where(64 * (1 << j) <= nn * nn, j, k)
    large = np.minimum(max_exact + k, half - 1)
    return np.where(rel > 0, half, 0) + np.where(n < max_exact, n, large)


HEADS_PER_COL = LANES // HEAD_DIM
COLS_PER_GROUP = GQA_GROUP // HEADS_PER_COL
HEAD_ORDER = tuple(g * GQA_GROUP + col * HEADS_PER_COL + half
                   for g in range(KV_HEADS) for half in range(HEADS_PER_COL) for col in range(COLS_PER_GROUP))
KV_EXPAND = KV_HEADS * HEADS_PER_COL


def _expand_kv(x):
    assert HEADS_PER_COL == 2 and KV_HEADS == 2 and x.shape[1] == LANES
    low = lax.broadcasted_iota(I32, x.shape, 1) < HEAD_DIM
    xr = pltpu.roll(x, HEAD_DIM, axis=1)
    zero = jnp.zeros_like(x)
    blocks = [jnp.where(low, x, zero), jnp.where(low, zero, xr), jnp.where(low, xr, zero), jnp.where(low, zero, x)]
    return jnp.concatenate(blocks, axis=1).astype(BF16)


def _attn_core(q, kx, vx, bias, sink):
    tq = q.shape[0]
    scores = []
    for g in range(KV_HEADS):
        cols = [q[:, (g * COLS_PER_GROUP + c) * LANES:(g * COLS_PER_GROUP + c + 1) * LANES]
                for c in range(COLS_PER_GROUP)]
        qst = jnp.concatenate(cols, axis=0)
        for half in range(HEADS_PER_COL):
            blk = g * HEADS_PER_COL + half
            scores.append(_dot_nt(qst, kx[:, blk * LANES:(blk + 1) * LANES]))
    s = jnp.concatenate(scores, axis=0) + bias
    m =jnp.maximum(jnp.max(s, axis=-1, keepdims=True), sink)
    p = jnp.exp2(s - m)
    den = jnp.sum(p, axis=-1, keepdims=True) + jnp.exp2(sink - m)
    pb = p.astype(BF16)
    rden = 1.0 / den
    rows = COLS_PER_GROUP * tq
    outs = []
    for g in range(KV_HEADS):
        o = None
        for half in range(HEADS_PER_COL):
            blk = g * HEADS_PER_COL + half
            rs = slice(blk * rows, (blk + 1) * rows)
            part = _dot(pb[rs], vx[:, blk * LANES:(blk + 1) * LANES]) * rden[rs]
            o = part if o is None else o + part
        outs.extend(o[c * tq:(c + 1) * tq] for c in range(COLS_PER_GROUP))
    return outs


def _bias_rows(table, bucket):
    onehot = (jnp.asarray(bucket)[..., None] == jnp.arange(NUM_BUCKETS)).astype(F32)
    cols = jnp.stack([table[:, h] for h in HEAD_ORDER], axis=1)
    bias = jnp.einsum('...qkb,bh->...hqk', onehot, cols, precision=lax.Precision.HIGHEST)
    return bias.reshape(*bucket.shape[:-2], ATTN_HEADS * bucket.shape[-2], bucket.shape[-1]) * LOG2E


def _pad_keys(bias):
    tk = bias.shape[-1]
    pad = [(0, 0)] * (bias.ndim - 1) + [(0, -tk % LANES)]
    return jnp.pad(bias, pad, constant_values=-jnp.inf)


def _sink_rows(sinks, tq):
    rows = jnp.repeat(jnp.stack([sinks[h] for h in HEAD_ORDER]).astype(F32), tq)
    return rows.reshape(ATTN_HEADS * tq, 1) * LOG2E


def _store_heads(ref, x):
    for g in range(KV_HEADS):
        ref[0, 0, :, g, :] = x[:, g * HEAD_DIM:(g + 1) * HEAD_DIM]


def _load_heads(ref):
    return jnp.concatenate([ref[0, 0, :, g, :] for g in range(KV_HEADS)], axis=1)


def _attn_prompt_kernel(cb, q_ref, k_ref, v_ref, km_ref, vm_ref, bias_ref, sink_ref, o_ref, nk_ref, nv_ref,
                        kx_ref, vx_ref):
    step = pl.program_id(1)
    s_len = k_ref.shape[1]
    meta_at = WINDOW + s_len

    @pl.when(step == pl.num_programs(1) - 1)
    def _():
        _store_heads(nk_ref, k_ref[0, s_len - WINDOW:, :])
        _store_heads(nv_ref, v_ref[0, s_len - WINDOW:, :])

    @pl.when(step == 0)
    def _():
        piece = min(s_len, 512)
        for src, meta, dst in ((k_ref, km_ref, kx_ref), (v_ref, vm_ref, vx_ref)):
            dst[0:WINDOW] = jnp.zeros((WINDOW, dst.shape[1]), BF16)
            for r in range(0, s_len, piece):
                dst[WINDOW + r:WINDOW + r + piece] = _expand_kv(src[0, r:r + piece, :])
            dst[meta_at:meta_at + N_META] = _expand_kv(meta[...])
            dst[meta_at + N_META:] = jnp.zeros((dst.shape[0] - meta_at - N_META, dst.shape[1]), BF16)

    win = WINDOW + CHUNK
    tail = kx_ref.shape[0] - meta_at
    for j in range(cb):
        c = step * cb + j
        start = pl.multiple_of(c * CHUNK, CHUNK)
        kall = jnp.concatenate([kx_ref[pl.ds(start, win), :], kx_ref[meta_at:meta_at + tail, :]], axis=0)
        vall = jnp.concatenate([vx_ref[pl.ds(start, win), :], vx_ref[meta_at:meta_at + tail, :]], axis=0)
        rows = slice(j * CHUNK, (j + 1) * CHUNK)
        outs = _attn_core(q_ref[0, rows, :], kall, vall, bias_ref[jnp.minimum(c, bias_ref.shape[0] - 1)],
                          sink_ref[...])
        for ci, o in enumerate(outs):
            o_ref[0, rows, ci * LANES:(ci + 1) * LANES] = o.astype(o_ref.dtype)


def _attn_prompt(q, k, v, k_meta, v_meta, table, sinks):
    b, s, aw = q.shape
    kvw = k.shape[-1]
    nc = s // CHUNK
    cb = ATTN_CHUNKS_PER_STEP if nc % ATTN_CHUNKS_PER_STEP == 0 else 1
    assert s % min(s, 512) == 0
    n_bias = 1
    while True:
        qpos = N_META + (n_bias - 1) * CHUNK
        if np.all(_t5_bucket_np(np.arange(N_META) - qpos) == _t5_bucket_np(np.arange(N_META) - qpos - 10 ** 6)):
            break
        n_bias += 1
    n_bias = min(max(n_bias, WINDOW_CHUNKS + 1), nc)
    cs = np.arange(n_bias)[:, None]
    qpos = N_META + cs * CHUNK + np.arange(CHUNK)[None]
    wpos = N_META + (cs - WINDOW_CHUNKS) * CHUNK + np.arange(WINDOW + CHUNK)[None]
    kpos = np.concatenate([wpos, np.broadcast_to(np.arange(N_META), (n_bias, N_META))], axis=1)
    valid = np.concatenate([wpos >= N_META, np.ones((n_bias, N_META), bool)], axis=1)
    bias = _bias_rows(table, _t5_bucket_np(kpos[:, None, :] - qpos[:, :, None]))
    bias = _pad_keys(jnp.where(valid[:, None, :], bias, -jnp.inf))
    sink = _sink_rows(sinks, CHUNK)
    qs = pl.BlockSpec((1, cb * CHUNK, aw), lambda i, c: (i, c, 0))
    kv = pl.BlockSpec((1, s, kvw), lambda i, c: (i, 0, 0))
    meta = pl.BlockSpec((N_META, kvw), lambda i, c: (0, 0))
    xrows = WINDOW + s + bias.shape[-1] - (WINDOW + CHUNK)
    cache_shape = (1, b, WINDOW, KV_HEADS, HEAD_DIM)
    cache = pl.BlockSpec((1, 1) + cache_shape[2:], lambda i, c: (0, i, 0, 0, 0))
    return pl.pallas_call(
        functools.partial(_attn_prompt_kernel, cb),
        grid=(b, nc // cb),
        in_specs=[qs, kv, kv, meta, meta, pl.BlockSpec(bias.shape, lambda i, c: (0, 0, 0)),
                  pl.BlockSpec(sink.shape, lambda i, c: (0, 0))],
        out_specs=[qs, cache, cache],
        out_shape=[jax.ShapeDtypeStruct((b, s, aw), BF16)] + [jax.ShapeDtypeStruct(cache_shape, F32)] * 2,
        scratch_shapes=[pltpu.VMEM((xrows, KV_EXPAND * LANES), BF16), pltpu.VMEM((xrows, KV_EXPAND * LANES), BF16)],
        compiler_params=_params("arbitrary", "arbitrary"),
        name="attn_prompt",
    )(q, k, v, k_meta, v_meta, bias, sink)


def _attn_sample_kernel(q_ref, kc_ref, vc_ref, kn_ref, vn_ref, km_ref, vm_ref, bias_ref, sink_ref,
                        o_ref, nk_ref, nv_ref):
    tk = kc_ref.shape[2] + kn_ref.shape[1] + km_ref.shape[0]
    zeros = jnp.zeros((bias_ref.shape[1] - tk, km_ref.shape[1]), F32)
    kc, vc, kn, vn = _load_heads(kc_ref), _load_heads(vc_ref), kn_ref[0], vn_ref[0]
    kall = _expand_kv(jnp.concatenate([kc, kn, km_ref[...], zeros], axis=0))
    vall = _expand_kv(jnp.concatenate([vc, vn, vm_ref[...], zeros], axis=0))
    outs = _attn_core(q_ref[0], kall, vall, bias_ref[...], sink_ref[...])
    for ci, o in enumerate(outs):
        o_ref[0, :, ci * LANES:(ci + 1) * LANES] = o.astype(o_ref.dtype)
    n_new = kn.shape[0]
    _store_heads(nk_ref, jnp.concatenate([kc[n_new:], kn], axis=0))
    _store_heads(nv_ref, jnp.concatenate([vc[n_new:], vn], axis=0))


def _attn_sample(q, k_cache, v_cache, k_new, v_new, k_meta, v_meta, table, sinks):
    bd, sd, aw = q.shape
    kvw = k_new.shape[-1]
    win = k_cache.shape[2]
    cache = pl.BlockSpec((1, 1) + k_cache.shape[2:], lambda i: (0, i, 0, 0, 0))
    qpos = N_META + PAST_LEN + np.arange(sd)
    kpos = np.concatenate([N_META + PAST_LEN - win + np.arange(win), qpos, np.arange(N_META)])
    bias = _pad_keys(_bias_rows(table, _t5_bucket_np(kpos[None, :] - qpos[:, None])))
    sink = _sink_rows(sinks, sd)
    per = lambda n, w: pl.BlockSpec((1, n, w), lambda i: (i, 0, 0))
    meta = pl.BlockSpec((N_META, kvw), lambda i: (0, 0))
    return pl.pallas_call(
        _attn_sample_kernel,
        grid=(bd,),
        in_specs=[per(sd, aw), cache, cache, per(sd, kvw), per(sd, kvw), meta, meta,
                  pl.BlockSpec(bias.shape, lambda i: (0, 0)), pl.BlockSpec(sink.shape, lambda i: (0, 0))],
        out_specs=[per(sd, aw), cache, cache],
        out_shape=[jax.ShapeDtypeStruct((bd, sd, aw), BF16)] + [jax.ShapeDtypeStruct(k_cache.shape, F32)] * 2,
        compiler_params=_params("arbitrary"),
        name="attn_sample",
    )(q, k_cache, v_cache, k_new, v_new, k_meta, v_meta, bias, sink)


ROUTE_E1, ROUTE_E2, ROUTE_R1, ROUTE_R2, ROUTE_W1, ROUTE_W2 = range(6)
META_FIELDS = 2 * TOP_K


def _merge_kernel(hw, x_ref, yr_ref, at_ref, sga_ref, sgb_ref, wb_ref, wo_ref, wn_ref, wr_ref, br_ref, cnt0_ref, tri_ref,
                  h_ref, xn_ref, route_ref, meta_ref, cnt_ref, carry_ref):
    @pl.when(pl.program_id(0) == 0)
    def _():
        carry_ref[...] = cnt0_ref[...]

    logits = _project_rows(hw, x_ref, yr_ref, at_ref, sga_ref, sgb_ref, wb_ref, wo_ref, wn_ref, wr_ref, br_ref,
                           h_ref, xn_ref)
    _route_rows(logits, tri_ref, route_ref, meta_ref, carry_ref)
    cnt_ref[...] = carry_ref[...]


def _project_rows(hw, x_ref, yr_ref, at_ref, sga_ref, sgb_ref, wb_ref, wo_ref, wn_ref, wr_ref, br_ref, h_ref, xn_ref):
    br = _dot(yr_ref[...], wb_ref[0:hw, :])
    ba = _dot(at_ref[...], wb_ref[hw:, :])
    merged = sga_ref[...].astype(F32) * br + sgb_ref[...].astype(F32) * ba
    h = x_ref[...] + _dot(merged.astype(BF16), wo_ref[...])
    h_ref[...] = h
    ms = jnp.mean(h * h, axis=-1, keepdims=True)
    xn = h * lax.rsqrt(ms + EPS) * wn_ref[...]
    _store_row_tiles(xn_ref, xn)

    x_hi = xn.astype(BF16)
    x_lo = (xn - x_hi.astype(F32)).astype(BF16)
    both = _dot(x_hi, wr_ref[...])
    return both[:, :LANES] + both[:, LANES:] + _dot(x_lo, wr_ref[:, :LANES]) + br_ref[...]


def _route_rows(logits, tri_ref, route_ref, meta_ref, carry_ref):
    tm = logits.shape[0]
    lane_i = lax.broadcasted_iota(I32, (tm, LANES), 1)
    lane = lane_i.astype(F32)
    group_of_lane = (lane_i >> int(math.log2(EXPERTS_PER_GROUP))).astype(F32)
    ninf = -jnp.inf
    first = lambda hit, idx: jnp.min(jnp.where(hit, idx, float(LANES)), axis=-1, keepdims=True)
    gmask = (lane_i >= N_EXPERTS) & (lane_i < N_EXPERTS + N_GROUPS)
    gl = jnp.where(gmask, logits, ninf)
    gmax = jnp.max(gl, axis=-1, keepdims=True)
    gidx = first(gl == gmax, lane - N_EXPERTS)
    gval = 1.0 / jnp.sum(jnp.exp(gl - gmax), axis=-1, keepdims=True)
    emask = (lane_i < N_EXPERTS) & (group_of_lane == gidx)
    el = jnp.where(emask, logits, ninf)
    m1 = jnp.max(el, axis=-1, keepdims=True)
    i1 = first(el == m1, lane)
    el2 = jnp.where(lane == i1, ninf, el)
    m2 = jnp.max(el2, axis=-1, keepdims=True)
    i2 = first(el2 == m2, lane)
    e21 = jnp.exp(m2 - m1)
    w1 = gval / (1.0 + e21)
    w2 = gval * e21 / (1.0 + e21)

    sel1 = lane == i1
    sel2 = lane == i2
    oh = (sel1 | sel2).astype(BF16)
    before = _dot(tri_ref[...], oh) + carry_ref[...]
    r1 = jnp.sum(jnp.where(sel1, before, 0.0), axis=-1, keepdims=True)
    r2 = jnp.sum(jnp.where(sel2, before, 0.0), axis=-1, keepdims=True)
    carry_ref[...] = carry_ref[...] + jnp.sum(oh.astype(F32), axis=0, keepdims=True)

    rec = jnp.zeros((tm, LANES), F32)
    for slot, val in ((ROUTE_E1, i1), (ROUTE_E2, i2), (ROUTE_W1, w1), (ROUTE_W2, w2), (ROUTE_R1, r1), (ROUTE_R2, r2)):
        rec = jnp.where(lane_i == slot, val, rec)
    route_ref[...] = rec
    meta_ref[0] = rec.T[ROUTE_E1:ROUTE_E1 + META_FIELDS].astype(I32)


def _merge(x, y_rec, att, sga, sgb, wb, wo, w_norm, w_router, b_router, cnt0):
    t, d = x.shape
    hw = y_rec.shape[1]
    tm = PROJ_TILE if t % PROJ_TILE == 0 else t
    row = lambda w: pl.BlockSpec((tm, w), lambda i: (i, 0))
    const = lambda a, b: pl.BlockSpec((a, b), lambda i: (0, 0))
    return pl.pallas_call(
        functools.partial(_merge_kernel, hw),
        grid=(t // tm,),
        in_specs=[row(d), row(hw), row(att.shape[1]), row(d), row(d), const(*wb.shape), const(d, d), const(1, d),
                  const(d, 2 * LANES), const(1, LANES), const(1, LANES), const(tm, tm)],
        out_specs=[row(d), pl.BlockSpec((tm * d // LANES, LANES), lambda i: (i, 0)), row(LANES),
                   pl.BlockSpec((1, META_FIELDS, tm), lambda i: (i, 0, 0)), const(1, LANES)],
        out_shape=[jax.ShapeDtypeStruct((t, d), F32), jax.ShapeDtypeStruct((t * d // LANES, LANES), F32),
                   jax.ShapeDtypeStruct((t, LANES), F32), jax.ShapeDtypeStruct((t // tm, META_FIELDS, tm), I32),
                   jax.ShapeDtypeStruct((1, LANES), F32)],
        scratch_shapes=[pltpu.VMEM((1, LANES), F32)],
        compiler_params=_params("arbitrary"),
        name="merge_route",
    )(x, y_rec, att, sga, sgb, wb, wo, w_norm.reshape(1, d), w_router, b_router, cnt0,
      jnp.asarray(np.tri(tm, k=-1), dtype=BF16))


def _row_copy(src, dst, sem):
    return pltpu.make_async_copy(src, dst, sem)


ROW_UNROLL = 8


def _tile_rows(ref, row, sub):
    return ref.at[pl.ds(pl.multiple_of(row * sub, sub), sub)]


def _scatter_kernel(n_tok, sub, first, slot_ref, pstart_ref, pend_ref, x_ref, *rest):
    xs_ref, zero_ref, sem, zsem, tsem = rest[-5:]

    def zero_blocks(blocks, zs, wait):
        for cond, row in blocks:
            @pl.when(cond)
            def _():
                at = row * sub if isinstance(row, int) else pl.multiple_of(row * sub, EXPERT_TILE * sub)
                cp = _row_copy(zero_ref, xs_ref.at[pl.ds(at, EXPERT_TILE * sub)], zs)
                cp.wait() if wait else cp.start()

    if first:
        n_rows = xs_ref.shape[0] // sub
        tails = [(pend_ref[e] > pstart_ref[e], pend_ref[e] - EXPERT_TILE) for e in range(N_EXPERTS)]
        unused = [(n_rows - (j + 1) * EXPERT_TILE >= pend_ref[N_EXPERTS - 1], n_rows - (j + 1) * EXPERT_TILE)
                  for j in range(N_EXPERTS)]

        @pl.when(pl.program_id(0) == 0)
        def _():
            zero_ref[...] = jnp.zeros_like(zero_ref)
            zero_blocks(tails, zsem, False)
            zero_blocks(unused, tsem, False)
            zero_blocks(tails, zsem, True)

    def issue(grp, _):
        for u in range(ROW_UNROLL):
            r = grp * ROW_UNROLL + u
            for k in range(TOP_K):
                _row_copy(_tile_rows(x_ref, r, sub), _tile_rows(xs_ref, slot_ref[0, k, r], sub),
                          sem).start(priority=(u * TOP_K + k) % 2)
        return 0

    lax.fori_loop(0, n_tok // ROW_UNROLL, issue, 0)
    for k in range(TOP_K):
        _row_copy(x_ref, xs_ref.at[pl.ds(0, n_tok * sub)], sem).wait()

    if first:
        @pl.when(pl.program_id(0) == 0)
        def _():
            zero_blocks(unused, tsem, True)


def _scatter_rows(x, slots, pstarts, pends, xs, rows):
    n, fields, tm = slots.shape
    sub = x.shape[0] // (n * tm)
    first = xs is None
    smem = pl.BlockSpec(memory_space=pltpu.SMEM)
    return pl.pallas_call(
        functools.partial(_scatter_kernel, tm, sub, first),
        grid=(n,),
        in_specs=[pl.BlockSpec((1, fields, tm), lambda i: (i, 0, 0), memory_space=pltpu.SMEM), smem, smem,
                  pl.BlockSpec((tm * sub, LANES), lambda i: (i, 0))]
        + ([] if first else [pl.BlockSpec(memory_space=pl.ANY)]),
        out_specs=pl.BlockSpec(memory_space=pl.ANY),
        out_shape=jax.ShapeDtypeStruct((rows * sub, LANES), F32),
        scratch_shapes=[pltpu.VMEM((EXPERT_TILE * sub, LANES), F32)] + [pltpu.SemaphoreType.DMA(())] * 3,
        input_output_aliases={} if first else {4: 0},
        compiler_params=_params("arbitrary"),
        name="moe_scatter",
    )(slots, pstarts, pends, x, *([] if first else [xs]))


def _expert_kernel(tm, sub, be_ref, nb_ref, x_ref, wg_ref, wu_ref, wd_ref, y_ref, wgb_ref, wub_ref, wdb_ref):
    i = pl.program_id(0)
    used = i < nb_ref[0]

    @pl.when(used & ((i == 0) | (be_ref[i] != be_ref[jnp.maximum(i - 1, 0)])))
    def _():
        wgb_ref[...] = wg_ref[0].astype(BF16)
        wub_ref[...] = wu_ref[0].astype(BF16)
        wdb_ref[...] = wd_ref[0].astype(BF16)

    @pl.when(used)
    def _():
        x = _load_row_tiles(x_ref, tm, sub).astype(BF16)
        g = _dot(x, wgb_ref[...])
        u = _dot(x, wub_ref[...])
        hmid = (g * _sigmoid(g) * u).astype(BF16)
        _store_row_tiles(y_ref, _dot(hmid, wdb_ref[...]))

    @pl.when(jnp.logical_not(used))
    def _():
        y_ref[...] = jnp.zeros_like(y_ref)


def _experts(xs, block_e, n_used, wg, wu, wd):
    _, d, ff = wg.shape
    sub = d // LANES
    tm = EXPERT_TILE
    blk = pl.BlockSpec((tm * sub, LANES), lambda i, be, nb: (i, 0))
    grid_spec = pltpu.PrefetchScalarGridSpec(
        num_scalar_prefetch=2,
        grid=(xs.shape[0] // (tm * sub),),
        in_specs=[blk,
                  pl.BlockSpec((1, d, ff), lambda i, be, nb: (be[i], 0, 0)),
                  pl.BlockSpec((1, d, ff), lambda i, be, nb: (be[i], 0, 0)),
                  pl.BlockSpec((1, ff, d), lambda i, be, nb: (be[i], 0, 0))],
        out_specs=blk,
        scratch_shapes=[pltpu.VMEM((d, ff), BF16), pltpu.VMEM((d, ff), BF16), pltpu.VMEM((ff, d), BF16)],
    )
    return pl.pallas_call(
        functools.partial(_expert_kernel, tm, sub),
        grid_spec=grid_spec,
        out_shape=jax.ShapeDtypeStruct(xs.shape, F32),
        compiler_params=_params("arbitrary"),
        name="moe_experts",
    )(block_e, n_used, xs, wg, wu, wd)


def _combine_kernel(n_tok, sub, scur_ref, snext_ref, h_ref, route_ref, ys_ref, o_ref, buf, sem):
    i = pl.program_id(0)
    n = pl.num_programs(0)
    slot = i % 2

    def start(slot_ref, s):
        def issue(grp, _):
            for u in range(ROW_UNROLL):
                r = grp * ROW_UNROLL + u
                for k in range(TOP_K):
                    _row_copy(_tile_rows(ys_ref, slot_ref[0, k, r], sub), _tile_rows(buf.at[s, k], r, sub),
                              sem.at[s]).start(priority=(u * TOP_K + k) % 2)
            return 0
        lax.fori_loop(0, n_tok // ROW_UNROLL, issue, 0)

    @pl.when(i == 0)
    def _():
        start(scur_ref, 0)

    @pl.when(i + 1 < n)
    def _():
        start(snext_ref, 1 - slot)

    for k in range(TOP_K):
        _row_copy(ys_ref.at[pl.ds(0, n_tok * sub)], buf.at[slot, k], sem.at[slot]).wait()
    route = route_ref[...]
    w1 = route[:, ROUTE_W1:ROUTE_W1 + 1]
    w2 = route[:, ROUTE_W2:ROUTE_W2 + 1]
    for j in range(sub):
        cols = slice(j * LANES, (j + 1) * LANES)
        part = lambda k: buf[slot, k, pl.ds(j, n_tok, stride=sub), :]
        o_ref[:, cols] = h_ref[:, cols] + (part(0) * w1 + part(1) * w2)


def _combine(h, route, slots, ys):
    t, d = h.shape
    n, fields, tm = slots.shape
    sub = d // LANES
    mspec = lambda f: pl.BlockSpec((1, fields, tm), f, memory_space=pltpu.SMEM)
    return pl.pallas_call(
        functools.partial(_combine_kernel, tm, sub),
        grid=(n,),
        in_specs=[mspec(lambda i: (i, 0, 0)), mspec(lambda i: (jnp.minimum(i + 1, n - 1), 0, 0)),
                  pl.BlockSpec((tm, d), lambda i: (i, 0)), pl.BlockSpec((tm, LANES), lambda i: (i, 0)),
                  pl.BlockSpec(memory_space=pl.ANY)],
        out_specs=pl.BlockSpec((tm, d), lambda i: (i, 0)),
        out_shape=jax.ShapeDtypeStruct((t, d), F32),
        scratch_shapes=[pltpu.VMEM((2, TOP_K, tm * sub, LANES), F32), pltpu.SemaphoreType.DMA((2,))],
        compiler_params=_params("arbitrary"),
        name="moe_combine",
    )(slots, slots, h, route, ys)


def _moe(parts, counts, wg, wu, wd):
    tm = EXPERT_TILE
    n_assign = sum(part[0].shape[0] for part in parts) * TOP_K
    n_blocks = -(-(n_assign + N_EXPERTS * (tm - 1)) // tm)
    counts = counts[0, :N_EXPERTS].astype(I32)
    pcounts = (counts + tm - 1) // tm * tm
    pends = jnp.cumsum(pcounts)
    pstarts = pends - pcounts
    block_start = jnp.arange(n_blocks, dtype=I32) * tm
    block_e = jnp.minimum(jnp.sum((pends[None, :] <= block_start[:, None]).astype(I32), axis=1), N_EXPERTS - 1)
    n_used = pends[-1:] // tm
    xs = None
    slots = []
    for _, xn, _, meta in parts:
        assert meta.shape[-1] % ROW_UNROLL == 0
        e, pos = meta[:, :TOP_K], meta[:, TOP_K:]
        seg = jnp.sum(jnp.where(e[..., None] == jnp.arange(N_EXPERTS, dtype=I32), pstarts, 0), axis=-1)
        slots.append(seg + pos)
        xs = _scatter_rows(xn, slots[-1], pstarts, pends, xs, n_blocks * tm)
    ys = _experts(xs, block_e, n_used, wg, wu, wd)
    return [_combine(h, route, s, ys) for (h, _, route, _), s in zip(parts, slots)]


def kernel(x_prompt, x_sample, cache_swa_k, cache_swa_v, state_hgrn, meta_tokens, rel_bias_table, hgrn_lower_bounds, w_norm_mix, w_in, hgrn_out_norm, q_norm, k_norm, attn_sinks, w_branch, w_out, w_norm_ffn, w_router_group, b_router_group, w_router_expert, b_router_expert, w_expert_gate, w_expert_up, w_expert_down):
    b, s, d = x_prompt.shape
    bd, sd, _ = x_sample.shape
    depth, _, heads, dk, dv = state_hgrn.shape
    assert depth == 1 and heads == HGRN_HEADS and dk == dv
    hw = heads * dk
    aw = ATTN_HEADS * HEAD_DIM
    kvw = KV_HEADS * HEAD_DIM
    assert w_in.shape[-1] == 4 * hw + aw + 2 * kvw + 2 * d
    assert s % HGRN_CHUNK == 0 and s % CHUNK == 0 and sd == N_META and N_EXPERTS + N_GROUPS <= LANES
    l = 0

    p = jax.nn.softmax(hgrn_lower_bounds.astype(F32), axis=0)
    lb = jnp.cumsum(p, axis=0)[l + 1] - p[0]

    w_in_b = w_in[l].astype(BF16)
    proj = functools.partial(_inproj, w_norm=w_norm_mix[l], w_in_bf16=w_in_b, q_gain=q_norm[l], k_gain=k_norm[l],
                             hw=hw, aw=aw, kvw=kvw)
    x_small = jnp.concatenate([x_sample.reshape(bd * sd, d), meta_tokens.astype(F32)], axis=0)
    qr_s, z_s, hv_s, hg_s, qa_s, k_s, v_s, sga_s, sgb_s = proj(x_small)
    qr_p, z_p, hv_p, hg_p, qa_p, k_p, v_p, sga_p, sgb_p = proj(x_prompt.reshape(b * s, d))
    ns = bd * sd
    k_meta, v_meta = k_s[ns:], v_s[ns:]

    streams = lambda a, n: a.reshape(n, -1, a.shape[-1])
    s0_small = jnp.concatenate([state_hgrn[l].astype(F32), jnp.zeros((1, heads, dk, dv), F32)], axis=0)
    y_small, st_small = _hgrn(streams(qr_s, bd + 1), streams(z_s, bd + 1), streams(hv_s, bd + 1),
                              streams(hg_s, bd + 1), lb, hgrn_out_norm[l], s0_small, sd)
    s0_p = jnp.broadcast_to(st_small[bd:], (b, heads, dk, dv))
    y_p, st_p = _hgrn(streams(qr_p, b), streams(z_p, b), streams(hv_p, b), streams(hg_p, b), lb,
                      hgrn_out_norm[l], s0_p, HGRN_CHUNK)

    table = rel_bias_table.astype(F32)
    att_p, new_k_p, new_v_p = _attn_prompt(streams(qa_p, b), streams(k_p, b), streams(v_p, b), k_meta, v_meta,
                                           table, attn_sinks[l])
    k_new, v_new = k_s[:ns].reshape(bd, sd, kvw), v_s[:ns].reshape(bd, sd, kvw)
    att_s, new_k_s, new_v_s = _attn_sample(qa_s[:ns].reshape(bd, sd, aw), cache_swa_k[l:l + 1].astype(F32),
                                           cache_swa_v[l:l + 1].astype(F32), k_new, v_new, k_meta, v_meta,
                                           table, attn_sinks[l])

    wb = w_branch[l].astype(BF16)
    wo = w_out[l].astype(BF16)
    w_router = jnp.pad(jnp.concatenate([w_router_expert[l], w_router_group[l]], axis=1).astype(F32),
                       ((0, 0), (0, LANES - N_EXPERTS - N_GROUPS)))
    b_router = jnp.pad(jnp.concatenate([b_router_expert[l], b_router_group[l]]).astype(F32),
                       (0, LANES - N_EXPERTS - N_GROUPS)).reshape(1, LANES)
    w_router_hi = w_router.astype(BF16)
    w_router_lo = (w_router - w_router_hi.astype(F32)).astype(BF16)
    w_router = jnp.concatenate([w_router_hi, w_router_lo], axis=1)
    merge = functools.partial(_merge, wb=wb, wo=wo, w_norm=w_norm_ffn[l], w_router=w_router, b_router=b_router)
    *part_p, cnt_p = merge(x_prompt.reshape(b * s, d), y_p.reshape(b * s, hw), att_p.reshape(b * s, aw),
                           sga_p, sgb_p, cnt0=jnp.zeros((1, LANES), F32))
    *part_s, cnt_s = merge(x_sample.reshape(ns, d), y_small[:bd].reshape(ns, hw), att_s.reshape(ns, aw),
                           sga_s[:ns], sgb_s[:ns], cnt0=cnt_p)

    out_p, out_s = _moe([part_p, part_s], cnt_s, w_expert_gate[l], w_expert_up[l], w_expert_down[l])

    return (out_p.reshape(b, s, d), out_s.reshape(bd, sd, d), new_k_p, new_v_p, st_p[None],
            new_k_s, new_v_s, st_small[:bd][None])
```

```python
import functools
import math

import numpy as np
import jax
import jax.numpy as jnp
from jax import lax
from jax.experimental import pallas as pl
from jax.experimental.pallas import tpu as pltpu

F32 = jnp.float32
BF16 = jnp.bfloat16
I32 = jnp.int32

CHUNK = 64
N_META = 16
PAST_LEN = 2048
EPS = 1e-6
HGRN_HEADS = 4
ATTN_HEADS = 8
KV_HEADS = 2
HEAD_DIM = 64
GQA_GROUP = ATTN_HEADS // KV_HEADS
WINDOW = 128
WINDOW_CHUNKS = WINDOW // CHUNK
NUM_BUCKETS = 32
MAX_DISTANCE = 128
N_GROUPS = 4
EXPERTS_PER_GROUP = 8
N_EXPERTS = N_GROUPS * EXPERTS_PER_GROUP
TOP_K = 2

LANES = 128
VMEM_LIMIT = 56 * 1024 * 1024

INPROJ_TILE = 512
PROJ_TILE = 512
HGRN_CHUNK = 128
HGRN_CHUNKS_PER_STEP = 8
ATTN_CHUNKS_PER_STEP = 8
EXPERT_TILE = 512


LOG2E = math.log2(math.e)


def _sigmoid(x):
    return 1.0 / (1.0 + jnp.exp(-x))


def _split3(x):
    hi = x.astype(BF16)
    r1 = x - hi.astype(F32)
    mid = r1.astype(BF16)
    lo = (r1 - mid.astype(F32)).astype(BF16)
    return hi, mid, lo


def _dot(a, b):
    return jnp.dot(a, b, preferred_element_type=F32)


def _dot_nt(a, b):
    return lax.dot_general(a, b, (((1,), (1,)), ((), ())), preferred_element_type=F32)


def _dot_tn(a, b):
    return lax.dot_general(a, b, (((0,), (0,)), ((), ())), preferred_element_type=F32)


SUBLANES = 8


def _store_row_tiles(ref, x):
    n, d = x.shape
    sub = d // LANES
    for j in range(sub):
        ref[pl.ds(j, n, stride=sub), :] = x[:, j * LANES:(j + 1) * LANES]


def _load_row_tiles(ref, n, sub):
    return jnp.concatenate([ref[pl.ds(j, n, stride=sub), :] for j in range(sub)], axis=1)


def _params(*sem):
    return pltpu.CompilerParams(dimension_semantics=sem, vmem_limit_bytes=VMEM_LIMIT)


def _inproj_kernel(hw, aw, kvw, d, x_ref, wn_ref, w_ref, qg_ref, kg_ref, bdq_ref, bdk_ref,
                   qr_ref, z_ref, hv_ref, hg_ref, qa_ref, k_ref, v_ref, sga_ref, sgb_ref):
    x = x_ref[...]
    ms = jnp.mean(x * x, axis=-1, keepdims=True)
    xn = (x * lax.rsqrt(ms + EPS) * wn_ref[...]).astype(BF16)

    def seg(a, b):
        return _dot(xn, w_ref[:, a:b])

    def head_rms(a, bd_ref, gain):
        sq = a * a
        hi = sq.astype(BF16)
        lo = (sq - hi.astype(F32)).astype(BF16)
        m = _dot(hi, bd_ref[...]) + _dot(lo, bd_ref[...])
        return a * lax.rsqrt(m + EPS) * gain

    o = 0
    hq = seg(o, o + hw)
    qr_ref[...] = (hq * _sigmoid(hq) * (hw // HGRN_HEADS) ** -0.5).astype(BF16)
    o += hw
    z_ref[...] = seg(o, o + hw)
    o += hw
    hv_ref[...] = seg(o, o + hw).astype(BF16)
    o += hw
    hg_ref[...] = seg(o, o + hw).astype(BF16)
    o += hw
    aq = seg(o, o + aw)
    qa_ref[...] = (head_rms(aq, bdq_ref, qg_ref[...]) * (HEAD_DIM ** -0.5 * LOG2E)).astype(BF16)
    o += aw
    k_ref[...] = head_rms(seg(o, o + kvw), bdk_ref, kg_ref[...])
    o += kvw
    v_ref[...] = seg(o, o + kvw)
    o += kvw
    sga_ref[...] = _sigmoid(seg(o, o + d)).astype(BF16)
    o += d
    sgb_ref[...] = _sigmoid(seg(o, o + d)).astype(BF16)


def _block_diag_mean(width, group):
    i = np.arange(width)
    return jnp.asarray((i[:, None] // group == i[None, :] // group) / group, dtype=BF16)


def _inproj(x, w_norm, w_in_bf16, q_gain, k_gain, hw, aw, kvw):
    t, d = x.shape
    tm = INPROJ_TILE if t % INPROJ_TILE == 0 else t
    cols = w_in_bf16.shape[1]
    row = lambda w: pl.BlockSpec((tm, w), lambda i: (i, 0))
    const = lambda a, b: pl.BlockSpec((a, b), lambda i: (0, 0))
    outs = [(hw, BF16), (hw, F32), (hw, BF16), (hw, BF16), (aw, BF16), (kvw, F32), (kvw, F32), (d, BF16), (d, BF16)]
    return pl.pallas_call(
        functools.partial(_inproj_kernel, hw, aw, kvw, d),
        grid=(t // tm,),
        in_specs=[row(d), const(1, d), const(d, cols), const(1, aw), const(1, kvw), const(aw, aw), const(kvw, kvw)],
        out_specs=[row(w) for w, _ in outs],
        out_shape=[jax.ShapeDtypeStruct((t, w), dt) for w, dt in outs],
        compiler_params=_params("arbitrary"),
        name="inproj",
    )(x, w_norm.reshape(1, d), w_in_bf16,
      jnp.tile(q_gain, aw // HEAD_DIM).reshape(1, aw), jnp.tile(k_gain, kvw // HEAD_DIM).reshape(1, kvw),
      _block_diag_mean(aw, HEAD_DIM), _block_diag_mean(kvw, HEAD_DIM))


def _hgrn_consts(L):
    t = np.arange(L)
    u = t[None, :]
    blocks = [u <= t[:, None], u > t[:, None]]
    levels = []
    m = L // 2
    while m >= 1:
        levels.append(m)
        m //= 2
    lvl = np.full((L, L), -1, np.int32)
    lvl[t, t] = len(levels)
    isq_cols = []
    for j, m in enumerate(levels):
        bnd = (t // (2 * m)) * (2 * m) + m - 1
        isq = (t % (2 * m)) >= m
        cq = isq[:, None] & (u > bnd[:, None]) & (u <= t[:, None])
        ck = (~isq)[:, None] & (u > t[:, None]) & (u <= bnd[:, None])
        blocks.append(cq | ck)
        same = (t[:, None] // (2 * m)) == (t[None, :] // (2 * m))
        lvl[same & isq[:, None] & (~isq)[None, :]] = j
        isq_cols.append(isq)
    c = np.concatenate(blocks, axis=0).astype(np.float32)
    isq = np.stack(isq_cols, axis=1).astype(np.float32)
    isq = np.pad(isq, ((0, 0), (0, LANES - isq.shape[1])))
    c2 = np.tile(c, (1, 2))
    return jnp.asarray(c2, dtype=BF16), jnp.asarray(np.tile(lvl, (1, 2))), jnp.asarray(isq), len(levels)


def _hgrn_kernel(L, nlev, heads, dk, qr_ref, z_ref, hv_ref, hg_ref, lb_ref, og_ref, c_ref, lvl_ref, isq_ref,
                 s0_ref, y_ref, sout_ref, st_ref):
    c = pl.program_id(1)

    @pl.when(c == 0)
    def _():
        for h in range(heads):
            st_ref[h] = s0_ref[0, h].T

    for cc in range(z_ref.shape[1] // L):
        _hgrn_chunk(L, nlev, heads, dk, slice(cc * L, (cc + 1) * L), qr_ref, z_ref, hv_ref, hg_ref, lb_ref, og_ref,
                    c_ref, lvl_ref, isq_ref, y_ref, st_ref)

    @pl.when(c == pl.num_programs(1) - 1)
    def _():
        for h in range(heads):
            sout_ref[0, h] = st_ref[h].T


def _hgrn_chunk(L, nlev, heads, dk, rows, qr_ref, z_ref, hv_ref, hg_ref, lb_ref, og_ref, c_ref, lvl_ref, isq_ref,
                y_ref, st_ref):
    z = z_ref[0, rows, :]
    lb = lb_ref[...]
    e = jnp.exp(-jnp.abs(z))
    r = 1.0 / (1.0 + e)
    pos = z >= 0
    sig = jnp.where(pos, r, e * r)
    sig_neg = jnp.where(pos, e * r, r)
    logf = jnp.log(lb + (1.0 - lb) * sig)
    kin = (1.0 - lb) * sig_neg
    q = qr_ref[0, rows, :].astype(F32)

    hi, mid, _ = _split3(logf * LOG2E)
    ex = jnp.exp2(_dot(c_ref[...], jnp.concatenate([hi, mid], axis=0)))
    e_b = ex[0:L]
    e_rev = ex[L:2 * L]

    q_in = (q * e_b).astype(BF16)
    k_out = (kin * e_rev).astype(BF16)
    q_b = q.astype(BF16)
    k_b = kin.astype(BF16)
    xs = []
    for j in range(nlev):
        m = L >> (j + 1)
        if m % SUBLANES == 0:
            qk = jnp.concatenate([(q if blk % 2 else kin)[blk * m:(blk + 1) * m] for blk in range(L // m)], axis=0)
        else:
            qk = jnp.where(isq_ref[:, j:j + 1] > 0.5, q, kin)
        xs.append((qk * ex[(2 + j) * L:(3 + j) * L]).astype(BF16))
    lvl = lvl_ref[...]
    v = hv_ref[0, rows, :]
    g = hg_ref[0, rows, :].astype(F32)
    og = og_ref[...]

    def block_diag(x):
        zero = jnp.zeros((x.shape[0], dk), x.dtype)
        return jnp.concatenate([jnp.concatenate([x[:, :dk], zero], axis=1),
                                jnp.concatenate([zero, x[:, dk:]], axis=1)], axis=0)

    for pair in range(heads // 2):
        sl = slice(2 * pair * dk, 2 * (pair + 1) * dk)
        a = jnp.where(lvl == nlev, _dot_nt(q_b[:, sl], block_diag(k_b[:, sl])), 0.0)
        for j in range(nlev):
            xp = xs[j][:, sl]
            a = jnp.where(lvl == j, _dot_nt(xp, block_diag(xp)), a)
        st = jnp.concatenate([st_ref[2 * pair], st_ref[2 * pair + 1]], axis=1)
        vp = v[:, sl]
        o = _dot(a.astype(BF16), block_diag(vp)) + _dot_nt(q_in[:, sl], block_diag(st.astype(BF16)))
        for half in range(2):
            h = 2 * pair + half
            hs = slice(h * dk, (h + 1) * dk)
            st_ref[h] = st_ref[h] * e_b[L - 1:L, hs] + _dot_tn(v[:, hs], k_out[:, hs])
            oh = o[:, half * dk:(half + 1) * dk]
            ms = jnp.mean(oh * oh, axis=-1, keepdims=True)
            gh = g[:, hs]
            y_ref[0, rows, hs] = (oh * lax.rsqrt(ms + EPS) * og[:, hs] * (gh * _sigmoid(gh))).astype(BF16)


def _hgrn(qr, z, hv, hg, lb, out_gain, s0, L):
    b, s, w = z.shape
    heads, dk = s0.shape[1], s0.shape[2]
    cm, lvl, isq, nlev = _hgrn_consts(L)
    per_step = HGRN_CHUNKS_PER_STEP if s % (HGRN_CHUNKS_PER_STEP * L) == 0 else 1
    seq = pl.BlockSpec((1, per_step * L, w), lambda i, c: (i, c, 0))
    const = lambda a: pl.BlockSpec(a.shape, lambda i, c: (0,) * a.ndim)
    state = pl.BlockSpec((1, heads, dk, dk), lambda i, c: (i, 0, 0, 0))
    lb2 = lb.reshape(1, w)
    og2 = jnp.tile(out_gain, heads).reshape(1, w)
    return pl.pallas_call(
        functools.partial(_hgrn_kernel, L, nlev, heads, dk),
        grid=(b, s // (per_step * L)),
        in_specs=[seq, seq, seq, seq, const(lb2), const(og2), const(cm), const(lvl), const(isq), state],
        out_specs=[seq, state],
        out_shape=[jax.ShapeDtypeStruct((b, s, w), BF16), jax.ShapeDtypeStruct(s0.shape, F32)],
        scratch_shapes=[pltpu.VMEM((heads, dk, dk), F32)],
        compiler_params=_params("arbitrary", "arbitrary"),
        name=f"hgrn_scan_{L}",
    )(qr, z, hv, hg, lb2, og2, cm, lvl, isq, s0)


def _t5_bucket_np(rel):
    half = NUM_BUCKETS // 2
    max_exact = half // 2
    assert (NUM_BUCKETS, MAX_DISTANCE) == (32, 128)
    n = np.abs(rel).astype(np.int64)
    nn = np.maximum(n, 1)
    k = np.zeros_like(nn)
    for j in range(1, 48):
        k = np.where(64 * (1 << j) <= nn * nn, j, k)
    large = np.minimum(max_exact + k, half - 1)
    return np.where(rel > 0, half, 0) + np.where(n < max_exact, n, large)


HEADS_PER_COL = LANES // HEAD_DIM
COLS_PER_GROUP = GQA_GROUP // HEADS_PER_COL
HEAD_ORDER = tuple(g * GQA_GROUP + col * HEADS_PER_COL + half
                   for g in range(KV_HEADS) for half in range(HEADS_PER_COL) for col in range(COLS_PER_GROUP))
KV_EXPAND = KV_HEADS * HEADS_PER_COL


def _expand_kv(x, with_ones=False):
    assert HEADS_PER_COL == 2 and KV_HEADS == 2 and x.shape[1] == LANES
    low = lax.broadcasted_iota(I32, x.shape, 1) < HEAD_DIM
    xr = pltpu.roll(x, HEAD_DIM, axis=1)
    zero = jnp.zeros_like(x)
    blocks = [jnp.where(low, x, zero), jnp.where(low, zero, xr), jnp.where(low, xr, zero), jnp.where(low, zero, x)]
    if with_ones:
        blocks = [b for blk in blocks for b in (blk, jnp.ones_like(x))]
    return jnp.concatenate(blocks, axis=1).astype(BF16)


def _attn_core(q, kx, vx, bias):
    tq = q.shape[0]
    scores = []
    for g in range(KV_HEADS):
        cols = [q[:, (g * COLS_PER_GROUP + c) * LANES:(g * COLS_PER_GROUP + c + 1) * LANES]
                for c in range(COLS_PER_GROUP)]
        qst = jnp.concatenate(cols, axis=0)
        for half in range(HEADS_PER_COL):
            blk = g * HEADS_PER_COL + half
            scores.append(_dot_nt(qst, kx[:, blk * LANES:(blk + 1) * LANES]))
    s = jnp.concatenate(scores, axis=0) + bias
    pb = jnp.exp2(s - jnp.max(s, axis=-1, keepdims=True)).astype(BF16)
    rows = COLS_PER_GROUP * tq
    outs = []
    for g in range(KV_HEADS):
        o = None
        for half in range(HEADS_PER_COL):
            blk = g * HEADS_PER_COL + half
            pv = _dot(pb[blk * rows:(blk + 1) * rows], vx[:, 2 * blk * LANES:2 * (blk + 1) * LANES])
            part = pv[:, :LANES] * (1.0 / pv[:, LANES:])
            o = part if o is None else o + part
        outs.extend(o[c * tq:(c + 1) * tq] for c in range(COLS_PER_GROUP))
    return outs


def _bias_rows(table, bucket):
    onehot = (jnp.asarray(bucket)[..., None] == jnp.arange(NUM_BUCKETS)).astype(F32)
    cols = jnp.stack([table[:, h] for h in HEAD_ORDER], axis=1)
    bias = jnp.einsum('...qkb,bh->...hqk', onehot, cols, precision=lax.Precision.HIGHEST)
    return bias.reshape(*bucket.shape[:-2], ATTN_HEADS * bucket.shape[-2], bucket.shape[-1]) * LOG2E


def _pad_keys(bias, sinks):
    tk = bias.shape[-1]
    n_pad = -tk % LANES or LANES
    tq = bias.shape[-2] // ATTN_HEADS
    sink = jnp.repeat(jnp.stack([sinks[h] for h in HEAD_ORDER]).astype(F32), tq) * LOG2E
    sink = jnp.broadcast_to(sink[:, None], bias.shape[:-1] + (1,))
    masked = jnp.full(bias.shape[:-1] + (n_pad - 1,), -jnp.inf, F32)
    return jnp.concatenate([bias, sink, masked], axis=-1)


def _store_heads(ref, x):
    for g in range(KV_HEADS):
        ref[0, 0, :, g, :] = x[:, g * HEAD_DIM:(g + 1) * HEAD_DIM]


def _load_heads(ref):
    return jnp.concatenate([ref[0, 0, :, g, :] for g in range(KV_HEADS)], axis=1)


def _attn_prompt_kernel(cb, q_ref, k_ref, v_ref, km_ref, vm_ref, bias_ref, o_ref, nk_ref, nv_ref,
                        kx_ref, vx_ref):
    step = pl.program_id(1)
    s_len = k_ref.shape[1]
    meta_at = WINDOW + s_len

    @pl.when(step == pl.num_programs(1) - 1)
    def _():
        _store_heads(nk_ref, k_ref[0, s_len - WINDOW:, :])
        _store_heads(nv_ref, v_ref[0, s_len - WINDOW:, :])

    @pl.when(step == 0)
    def _():
        piece = min(s_len, 512)
        for src, meta, dst in ((k_ref, km_ref, kx_ref), (v_ref, vm_ref, vx_ref)):
            expand = functools.partial(_expand_kv, with_ones=dst is vx_ref)
            blank = lambda n: expand(jnp.zeros((n, src.shape[2]), F32))
            dst[0:WINDOW] = blank(WINDOW)
            for r in range(0, s_len, piece):
                dst[WINDOW + r:WINDOW + r + piece] = expand(src[0, r:r + piece, :])
            dst[meta_at:meta_at + N_META] = expand(meta[...])
            dst[meta_at + N_META:] = blank(dst.shape[0] - meta_at - N_META)

    win = WINDOW + CHUNK
    tail = kx_ref.shape[0] - meta_at
    for j in range(cb):
        c = step * cb + j
        start = pl.multiple_of(c * CHUNK, CHUNK)
        kall = jnp.concatenate([kx_ref[pl.ds(start, win), :], kx_ref[meta_at:meta_at + tail, :]], axis=0)
        vall = jnp.concatenate([vx_ref[pl.ds(start, win), :], vx_ref[meta_at:meta_at + tail, :]], axis=0)
        rows = slice(j * CHUNK, (j + 1) * CHUNK)
        outs = _attn_core(q_ref[0, rows, :], kall, vall, bias_ref[jnp.minimum(c, bias_ref.shape[0] - 1)])
        for ci, o in enumerate(outs):
            o_ref[0, rows, ci * LANES:(ci + 1) * LANES] = o.astype(o_ref.dtype)


def _attn_prompt(q, k, v, k_meta, v_meta, table, sinks):
    b, s, aw = q.shape
    kvw = k.shape[-1]
    nc = s // CHUNK
    cb = ATTN_CHUNKS_PER_STEP if nc % ATTN_CHUNKS_PER_STEP == 0 else 1
    assert s % min(s, 512) == 0
    n_bias = 1
    while True:
        qpos = N_META + (n_bias - 1) * CHUNK
        if np.all(_t5_bucket_np(np.arange(N_META) - qpos) == _t5_bucket_np(np.arange(N_META) - qpos - 10 ** 6)):
            break
        n_bias += 1
    n_bias = min(max(n_bias, WINDOW_CHUNKS + 1), nc)
    cs = np.arange(n_bias)[:, None]
    qpos = N_META + cs * CHUNK + np.arange(CHUNK)[None]
    wpos = N_META + (cs - WINDOW_CHUNKS) * CHUNK + np.arange(WINDOW + CHUNK)[None]
    kpos = np.concatenate([wpos, np.broadcast_to(np.arange(N_META), (n_bias, N_META))], axis=1)
    valid = np.concatenate([wpos >= N_META, np.ones((n_bias, N_META), bool)], axis=1)
    bias = _bias_rows(table, _t5_bucket_np(kpos[:, None, :] - qpos[:, :, None]))
    bias = _pad_keys(jnp.where(valid[:, None, :], bias, -jnp.inf), sinks)
    qs =pl.BlockSpec((1, cb * CHUNK, aw), lambda i, c: (i, c, 0))
    kv = pl.BlockSpec((1, s, kvw), lambda i, c: (i, 0, 0))
    meta = pl.BlockSpec((N_META, kvw), lambda i, c: (0, 0))
    xrows = WINDOW + s + bias.shape[-1] - (WINDOW + CHUNK)
    cache_shape = (1, b, WINDOW, KV_HEADS, HEAD_DIM)
    cache = pl.BlockSpec((1, 1) + cache_shape[2:], lambda i, c: (0, i, 0, 0, 0))
    return pl.pallas_call(
        functools.partial(_attn_prompt_kernel, cb),
        grid=(b, nc // cb),
        in_specs=[qs, kv, kv, meta, meta, pl.BlockSpec(bias.shape, lambda i, c: (0, 0, 0))],
        out_specs=[qs, cache, cache],
        out_shape=[jax.ShapeDtypeStruct((b, s, aw), BF16)] + [jax.ShapeDtypeStruct(cache_shape, F32)] * 2,
        scratch_shapes=[pltpu.VMEM((xrows, KV_EXPAND * LANES), BF16),
                        pltpu.VMEM((xrows, 2 * KV_EXPAND * LANES), BF16)],
        compiler_params=_params("arbitrary", "arbitrary"),
        name="attn_prompt",
    )(q, k, v, k_meta, v_meta, bias)


def _attn_sample_kernel(q_ref, kc_ref, vc_ref, kn_ref, vn_ref, km_ref, vm_ref, bias_ref, o_ref, nk_ref, nv_ref):
    tk = kc_ref.shape[2] + kn_ref.shape[1] + km_ref.shape[0]
    zeros = jnp.zeros((bias_ref.shape[1] - tk, km_ref.shape[1]), F32)
    kc, vc, kn, vn = _load_heads(kc_ref), _load_heads(vc_ref), kn_ref[0], vn_ref[0]
    kall = _expand_kv(jnp.concatenate([kc, kn, km_ref[...], zeros], axis=0))
    vall = _expand_kv(jnp.concatenate([vc, vn, vm_ref[...], zeros], axis=0), with_ones=True)
    outs = _attn_core(q_ref[0], kall, vall, bias_ref[...])
    for ci, o in enumerate(outs):
        o_ref[0, :, ci * LANES:(ci + 1) * LANES] = o.astype(o_ref.dtype)
    n_new = kn.shape[0]
    _store_heads(nk_ref, jnp.concatenate([kc[n_new:], kn], axis=0))
    _store_heads(nv_ref, jnp.concatenate([vc[n_new:], vn], axis=0))


def _attn_sample(q, k_cache, v_cache, k_new, v_new, k_meta, v_meta, table, sinks):
    bd, sd, aw = q.shape
    kvw = k_new.shape[-1]
    win = k_cache.shape[2]
    cache = pl.BlockSpec((1, 1) + k_cache.shape[2:], lambda i: (0, i, 0, 0, 0))
    qpos = N_META + PAST_LEN + np.arange(sd)
    kpos = np.concatenate([N_META + PAST_LEN - win + np.arange(win), qpos, np.arange(N_META)])
    bias = _pad_keys(_bias_rows(table, _t5_bucket_np(kpos[None, :] - qpos[:, None])), sinks)
    per = lambda n, w: pl.BlockSpec((1, n, w), lambda i: (i, 0, 0))
    meta = pl.BlockSpec((N_META, kvw), lambda i: (0, 0))
    return pl.pallas_call(
        _attn_sample_kernel,
        grid=(bd,),
        in_specs=[per(sd, aw), cache, cache, per(sd, kvw), per(sd, kvw), meta, meta,
                  pl.BlockSpec(bias.shape, lambda i: (0, 0))],
        out_specs=[per(sd, aw), cache, cache],
        out_shape=[jax.ShapeDtypeStruct((bd, sd, aw), BF16)] + [jax.ShapeDtypeStruct(k_cache.shape, F32)] * 2,
        compiler_params=_params("arbitrary"),
        name="attn_sample",
    )(q, k_cache, v_cache, k_new, v_new, k_meta, v_meta, bias)


ROUTE_E1, ROUTE_E2, ROUTE_R1, ROUTE_R2, ROUTE_W1, ROUTE_W2 = range(6)
META_FIELDS = 2 * TOP_K


def _merge_kernel(hw, x_ref, yr_ref, at_ref, sga_ref, sgb_ref, wb_ref, wo_ref, wn_ref, wr_ref, br_ref, cnt0_ref, tri_ref,
                  h_ref, xn_ref, route_ref, meta_ref, cnt_ref, carry_ref):
    @pl.when(pl.program_id(0) == 0)
    def _():
        carry_ref[...] = cnt0_ref[...]

    logits = _project_rows(hw, x_ref, yr_ref, at_ref, sga_ref, sgb_ref, wb_ref, wo_ref, wn_ref, wr_ref, br_ref,
                           h_ref, xn_ref)
    _route_rows(logits, tri_ref, route_ref, meta_ref, carry_ref)
    cnt_ref[...] = carry_ref[...]


def _project_rows(hw, x_ref, yr_ref, at_ref, sga_ref, sgb_ref, wb_ref, wo_ref, wn_ref, wr_ref, br_ref, h_ref, xn_ref):
    br = _dot(yr_ref[...], wb_ref[0:hw, :])
    ba = _dot(at_ref[...], wb_ref[hw:, :])
    merged = sga_ref[...].astype(F32) * br + sgb_ref[...].astype(F32) * ba
    h = x_ref[...] + _dot(merged.astype(BF16), wo_ref[...])
    h_ref[...] = h
    ms = jnp.mean(h * h, axis=-1, keepdims=True)
    xn = h * lax.rsqrt(ms + EPS) * wn_ref[...]
    _store_row_tiles(xn_ref, xn)

    x_hi = xn.astype(BF16)
    x_lo = (xn - x_hi.astype(F32)).astype(BF16)
    both = _dot(x_hi, wr_ref[...])
    return both[:, :LANES] + both[:, LANES:] + _dot(x_lo, wr_ref[:, :LANES]) + br_ref[...]


def _route_rows(logits, tri_ref, route_ref, meta_ref, carry_ref):
    tm = logits.shape[0]
    lane_i = lax.broadcasted_iota(I32, (tm, LANES), 1)
    lane = lane_i.astype(F32)
    group_of_lane = (lane_i >> int(math.log2(EXPERTS_PER_GROUP))).astype(F32)
    ninf = -jnp.inf
    first = lambda hit, idx: jnp.min(jnp.where(hit, idx, float(LANES)), axis=-1, keepdims=True)
    gmask = (lane_i >= N_EXPERTS) & (lane_i < N_EXPERTS + N_GROUPS)
    gl = jnp.where(gmask, logits, ninf)
    gmax = jnp.max(gl, axis=-1, keepdims=True)
    gidx = first(gl == gmax, lane - N_EXPERTS)
    gval = 1.0 / jnp.sum(jnp.exp(gl - gmax), axis=-1, keepdims=True)
    emask = (lane_i < N_EXPERTS) & (group_of_lane == gidx)
    el = jnp.where(emask, logits, ninf)
    m1 = jnp.max(el, axis=-1, keepdims=True)
    i1 = first(el == m1, lane)
    el2 = jnp.where(lane == i1, ninf, el)
    m2 = jnp.max(el2, axis=-1, keepdims=True)
    i2 = first(el2 == m2, lane)
    e21 = jnp.exp(m2 - m1)
    w1 = gval / (1.0 + e21)
    w2 = gval * e21 / (1.0 + e21)

    sel1 = lane == i1
    sel2 = lane == i2
    oh = (sel1 | sel2).astype(BF16)
    before = _dot(tri_ref[...], oh) + carry_ref[...]
    r1 = jnp.sum(jnp.where(sel1, before, 0.0), axis=-1, keepdims=True)
    r2 = jnp.sum(jnp.where(sel2, before, 0.0), axis=-1, keepdims=True)
    carry_ref[...] = carry_ref[...] + jnp.sum(oh.astype(F32), axis=0, keepdims=True)

    rec = jnp.zeros((tm, LANES), F32)
    for slot, val in ((ROUTE_E1, i1), (ROUTE_E2, i2), (ROUTE_W1, w1), (ROUTE_W2, w2), (ROUTE_R1, r1), (ROUTE_R2, r2)):
        rec = jnp.where(lane_i == slot, val, rec)
    route_ref[...] = rec
    meta_ref[0] = rec.T[ROUTE_E1:ROUTE_E1 + META_FIELDS].astype(I32)


def _merge(x, y_rec, att, sga, sgb, wb, wo, w_norm, w_router, b_router, cnt0):
    t, d = x.shape
    hw = y_rec.shape[1]
    tm = PROJ_TILE if t % PROJ_TILE == 0 else t
    row = lambda w: pl.BlockSpec((tm, w), lambda i: (i, 0))
    const = lambda a, b: pl.BlockSpec((a, b), lambda i: (0, 0))
    return pl.pallas_call(
        functools.partial(_merge_kernel, hw),
        grid=(t // tm,),
        in_specs=[row(d), row(hw), row(att.shape[1]), row(d), row(d), const(*wb.shape), const(d, d), const(1, d),
                  const(d, 2 * LANES), const(1, LANES), const(1, LANES), const(tm, tm)],
        out_specs=[row(d), pl.BlockSpec((tm * d // LANES, LANES), lambda i: (i, 0)), row(LANES),
                   pl.BlockSpec((1, META_FIELDS, tm), lambda i: (i, 0, 0)), const(1, LANES)],
        out_shape=[jax.ShapeDtypeStruct((t, d), F32), jax.ShapeDtypeStruct((t * d // LANES, LANES), F32),
                   jax.ShapeDtypeStruct((t, LANES), F32), jax.ShapeDtypeStruct((t // tm, META_FIELDS, tm), I32),
                   jax.ShapeDtypeStruct((1, LANES), F32)],
        scratch_shapes=[pltpu.VMEM((1, LANES), F32)],
        compiler_params=_params("arbitrary"),
        name="merge_route",
    )(x, y_rec, att, sga, sgb, wb, wo, w_norm.reshape(1, d), w_router, b_router, cnt0,
      jnp.asarray(np.tri(tm, k=-1), dtype=BF16))


def _row_copy(src, dst, sem):
    return pltpu.make_async_copy(src, dst, sem)


ROW_UNROLL = 8


def _tile_rows(ref, row, sub):
    return ref.at[pl.ds(pl.multiple_of(row * sub, sub), sub)]


def _scatter_kernel(n_tok, sub, first, slot_ref, pstart_ref, pend_ref, x_ref, *rest):
    xs_ref, zero_ref, sem, zsem, tsem = rest[-5:]

    def zero_blocks(blocks, zs, wait):
        for cond, row in blocks:
            @pl.when(cond)
            def _():
                at = row * sub if isinstance(row, int) else pl.multiple_of(row * sub, EXPERT_TILE * sub)
                cp = _row_copy(zero_ref, xs_ref.at[pl.ds(at, EXPERT_TILE * sub)], zs)
                cp.wait() if wait else cp.start()

    if first:
        n_rows = xs_ref.shape[0] // sub
        tails = [(pend_ref[e] > pstart_ref[e], pend_ref[e] - EXPERT_TILE) for e in range(N_EXPERTS)]
        unused = [(n_rows - (j + 1) * EXPERT_TILE >= pend_ref[N_EXPERTS - 1], n_rows - (j + 1) * EXPERT_TILE)
                  for j in range(N_EXPERTS)]

        @pl.when(pl.program_id(0) == 0)
        def _():
            zero_ref[...] = jnp.zeros_like(zero_ref)
            zero_blocks(tails, zsem, False)
            zero_blocks(unused, tsem, False)
            zero_blocks(tails, zsem, True)

    def issue(grp, _):
        for u in range(ROW_UNROLL):
            r = grp * ROW_UNROLL + u
            for k in range(TOP_K):
                _row_copy(_tile_rows(x_ref, r, sub), _tile_rows(xs_ref, slot_ref[0, k, r], sub),
                          sem).start(priority=(u * TOP_K + k) % 2)
        return 0

    lax.fori_loop(0, n_tok // ROW_UNROLL, issue, 0)
    for k in range(TOP_K):
        _row_copy(x_ref, xs_ref.at[pl.ds(0, n_tok * sub)], sem).wait()

    if first:
        @pl.when(pl.program_id(0) == 0)
        def _():
            zero_blocks(unused, tsem, True)


def _scatter_rows(x, slots, pstarts, pends, xs, rows):
    n, fields, tm = slots.shape
    sub = x.shape[0] // (n * tm)
    first = xs is None
    smem = pl.BlockSpec(memory_space=pltpu.SMEM)
    return pl.pallas_call(
        functools.partial(_scatter_kernel, tm, sub, first),
        grid=(n,),
        in_specs=[pl.BlockSpec((1, fields, tm), lambda i: (i, 0, 0), memory_space=pltpu.SMEM), smem, smem,
                  pl.BlockSpec((tm * sub, LANES), lambda i: (i, 0))]
        + ([] if first else [pl.BlockSpec(memory_space=pl.ANY)]),
        out_specs=pl.BlockSpec(memory_space=pl.ANY),
        out_shape=jax.ShapeDtypeStruct((rows * sub, LANES), F32),
        scratch_shapes=[pltpu.VMEM((EXPERT_TILE * sub, LANES), F32)] + [pltpu.SemaphoreType.DMA(())] * 3,
        input_output_aliases={} if first else {4: 0},
        compiler_params=_params("arbitrary"),
        name="moe_scatter",
    )(slots, pstarts, pends, x, *([] if first else [xs]))


def _expert_kernel(tm, sub, be_ref, nb_ref, x_ref, wg_ref, wu_ref, wd_ref, y_ref, wgb_ref, wub_ref, wdb_ref):
    i = pl.program_id(0)
    used = i < nb_ref[0]

    @pl.when(used & ((i == 0) | (be_ref[i] != be_ref[jnp.maximum(i - 1, 0)])))
    def _():
        wgb_ref[...] = wg_ref[0].astype(BF16)
        wub_ref[...] = wu_ref[0].astype(BF16)
        wdb_ref[...] = wd_ref[0].astype(BF16)

    @pl.when(used)
    def _():
        x = _load_row_tiles(x_ref, tm, sub).astype(BF16)
        g = _dot(x, wgb_ref[...])
        u = _dot(x, wub_ref[...])
        hmid = (g * _sigmoid(g) * u).astype(BF16)
        _store_row_tiles(y_ref, _dot(hmid, wdb_ref[...]))

    @pl.when(jnp.logical_not(used))
    def _():
        y_ref[...] = jnp.zeros_like(y_ref)


def _experts(xs, block_e, n_used, wg, wu, wd):
    _, d, ff = wg.shape
    sub = d // LANES
    tm = EXPERT_TILE
    blk = pl.BlockSpec((tm * sub, LANES), lambda i, be, nb: (i, 0))
    grid_spec = pltpu.PrefetchScalarGridSpec(
        num_scalar_prefetch=2,
        grid=(xs.shape[0] // (tm * sub),),
        in_specs=[blk,
                  pl.BlockSpec((1, d, ff), lambda i, be, nb: (be[i], 0, 0)),
                  pl.BlockSpec((1, d, ff), lambda i, be, nb: (be[i], 0, 0)),
                  pl.BlockSpec((1, ff, d), lambda i, be, nb: (be[i], 0, 0))],
        out_specs=blk,
        scratch_shapes=[pltpu.VMEM((d, ff), BF16), pltpu.VMEM((d, ff), BF16), pltpu.VMEM((ff, d), BF16)],
    )
    return pl.pallas_call(
        functools.partial(_expert_kernel, tm, sub),
        grid_spec=grid_spec,
        out_shape=jax.ShapeDtypeStruct(xs.shape, F32),
        compiler_params=_params("arbitrary"),
        name="moe_experts",
    )(block_e, n_used, xs, wg, wu, wd)


def _combine_kernel(n_tok, sub, scur_ref, snext_ref, h_ref, route_ref, ys_ref, o_ref, buf, sem):
    i = pl.program_id(0)
    n = pl.num_programs(0)
    slot = i % 2

    def start(slot_ref, s):
        def issue(grp, _):
            for u in range(ROW_UNROLL):
                r = grp * ROW_UNROLL + u
                for k in range(TOP_K):
                    _row_copy(_tile_rows(ys_ref, slot_ref[0, k, r], sub), _tile_rows(buf.at[s, k], r, sub),
                              sem.at[s]).start(priority=(u * TOP_K + k) % 2)
            return 0
        lax.fori_loop(0, n_tok // ROW_UNROLL, issue, 0)

    @pl.when(i == 0)
    def _():
        start(scur_ref, 0)

    @pl.when(i + 1 < n)
    def _():
        start(snext_ref, 1 - slot)

    for k in range(TOP_K):
        _row_copy(ys_ref.at[pl.ds(0, n_tok * sub)], buf.at[slot, k], sem.at[slot]).wait()
    route = route_ref[...]
    w1 = route[:, ROUTE_W1:ROUTE_W1 + 1]
    w2 = route[:, ROUTE_W2:ROUTE_W2 + 1]
    for j in range(sub):
        cols = slice(j * LANES, (j + 1) * LANES)
        part = lambda k: buf[slot, k, pl.ds(j, n_tok, stride=sub), :]
        o_ref[:, cols] = h_ref[:, cols] + (part(0) * w1 + part(1) * w2)


def _combine(h, route, slots, ys):
    t, d = h.shape
    n, fields, tm = slots.shape
    sub = d // LANES
    mspec = lambda f: pl.BlockSpec((1, fields, tm), f, memory_space=pltpu.SMEM)
    return pl.pallas_call(
        functools.partial(_combine_kernel, tm, sub),
        grid=(n,),
        in_specs=[mspec(lambda i: (i, 0, 0)), mspec(lambda i: (jnp.minimum(i + 1, n - 1), 0, 0)),
                  pl.BlockSpec((tm, d), lambda i: (i, 0)), pl.BlockSpec((tm, LANES), lambda i: (i, 0)),
                  pl.BlockSpec(memory_space=pl.ANY)],
        out_specs=pl.BlockSpec((tm, d), lambda i: (i, 0)),
        out_shape=jax.ShapeDtypeStruct((t, d), F32),
        scratch_shapes=[pltpu.VMEM((2, TOP_K, tm * sub, LANES), F32), pltpu.SemaphoreType.DMA((2,))],
        compiler_params=_params("arbitrary"),
        name="moe_combine",
    )(slots, slots, h, route, ys)


def _moe(parts, counts, wg, wu, wd):
    tm = EXPERT_TILE
    n_assign = sum(part[0].shape[0] for part in parts) * TOP_K
    n_blocks = -(-(n_assign + N_EXPERTS * (tm - 1)) // tm)
    counts = counts[0, :N_EXPERTS].astype(I32)
    pcounts = (counts + tm - 1) // tm * tm
    pends = jnp.cumsum(pcounts)
    pstarts = pends - pcounts
    block_start = jnp.arange(n_blocks, dtype=I32) * tm
    block_e = jnp.minimum(jnp.sum((pends[None, :] <= block_start[:, None]).astype(I32), axis=1), N_EXPERTS - 1)
    n_used = pends[-1:] // tm
    xs = None
    slots = []
    for _, xn, _, meta in parts:
        assert meta.shape[-1] % ROW_UNROLL == 0
        e, pos = meta[:, :TOP_K], meta[:, TOP_K:]
        seg = jnp.sum(jnp.where(e[..., None] == jnp.arange(N_EXPERTS, dtype=I32), pstarts, 0), axis=-1)
        slots.append(seg + pos)
        xs = _scatter_rows(xn, slots[-1], pstarts, pends, xs, n_blocks * tm)
    ys = _experts(xs, block_e, n_used, wg, wu, wd)
    return [_combine(h, route, s, ys) for (h, _, route, _), s in zip(parts, slots)]


def kernel(x_prompt, x_sample, cache_swa_k, cache_swa_v, state_hgrn, meta_tokens, rel_bias_table, hgrn_lower_bounds, w_norm_mix, w_in, hgrn_out_norm, q_norm, k_norm, attn_sinks, w_branch, w_out, w_norm_ffn, w_router_group, b_router_group, w_router_expert, b_router_expert, w_expert_gate, w_expert_up, w_expert_down):
    b, s, d = x_prompt.shape
    bd, sd, _ = x_sample.shape
    depth, _, heads, dk, dv = state_hgrn.shape
    assert depth == 1 and heads == HGRN_HEADS and dk == dv
    hw = heads * dk
    aw = ATTN_HEADS * HEAD_DIM
    kvw = KV_HEADS * HEAD_DIM
    assert w_in.shape[-1] == 4 * hw + aw + 2 * kvw + 2 * d
    assert s % HGRN_CHUNK == 0 and s % CHUNK == 0 and sd == N_META and N_EXPERTS + N_GROUPS <= LANES
    l = 0

    p = jax.nn.softmax(hgrn_lower_bounds.astype(F32), axis=0)
    lb = jnp.cumsum(p, axis=0)[l + 1] - p[0]

    w_in_b = w_in[l].astype(BF16)
    proj = functools.partial(_inproj, w_norm=w_norm_mix[l], w_in_bf16=w_in_b, q_gain=q_norm[l], k_gain=k_norm[l],
                             hw=hw, aw=aw, kvw=kvw)
    x_small = jnp.concatenate([x_sample.reshape(bd * sd, d), meta_tokens.astype(F32)], axis=0)
    qr_s, z_s, hv_s, hg_s, qa_s, k_s, v_s, sga_s, sgb_s = proj(x_small)
    qr_p, z_p, hv_p, hg_p, qa_p, k_p, v_p, sga_p, sgb_p = proj(x_prompt.reshape(b * s, d))
    ns = bd * sd
    k_meta, v_meta = k_s[ns:], v_s[ns:]

    streams = lambda a, n: a.reshape(n, -1, a.shape[-1])
    s0_small = jnp.concatenate([state_hgrn[l].astype(F32), jnp.zeros((1, heads, dk, dv), F32)], axis=0)
    y_small, st_small = _hgrn(streams(qr_s, bd + 1), streams(z_s, bd + 1), streams(hv_s, bd + 1),
                              streams(hg_s, bd + 1), lb, hgrn_out_norm[l], s0_small, sd)
    s0_p = jnp.broadcast_to(st_small[bd:], (b, heads, dk, dv))
    y_p, st_p = _hgrn(streams(qr_p, b), streams(z_p, b), streams(hv_p, b), streams(hg_p, b), lb,
                      hgrn_out_norm[l], s0_p, HGRN_CHUNK)

    table = rel_bias_table.astype(F32)
    att_p, new_k_p, new_v_p = _attn_prompt(streams(qa_p, b), streams(k_p, b), streams(v_p, b), k_meta, v_meta,
                                           table, attn_sinks[l])
    k_new, v_new = k_s[:ns].reshape(bd, sd, kvw), v_s[:ns].reshape(bd, sd, kvw)
    att_s, new_k_s, new_v_s = _attn_sample(qa_s[:ns].reshape(bd, sd, aw), cache_swa_k[l:l + 1].astype(F32),
                                           cache_swa_v[l:l + 1].astype(F32), k_new, v_new, k_meta, v_meta,
                                           table, attn_sinks[l])

    wb = w_branch[l].astype(BF16)
    wo = w_out[l].astype(BF16)
    w_router = jnp.pad(jnp.concatenate([w_router_expert[l], w_router_group[l]], axis=1).astype(F32),
                       ((0, 0), (0, LANES - N_EXPERTS - N_GROUPS)))
    b_router = jnp.pad(jnp.concatenate([b_router_expert[l], b_router_group[l]]).astype(F32),
                       (0, LANES - N_EXPERTS - N_GROUPS)).reshape(1, LANES)
    w_router_hi = w_router.astype(BF16)
    w_router_lo = (w_router - w_router_hi.astype(F32)).astype(BF16)
    w_router = jnp.concatenate([w_router_hi, w_router_lo], axis=1)
    merge = functools.partial(_merge, wb=wb, wo=wo, w_norm=w_norm_ffn[l], w_router=w_router, b_router=b_router)
    *part_p, cnt_p = merge(x_prompt.reshape(b * s, d), y_p.reshape(b * s, hw), att_p.reshape(b * s, aw),
                           sga_p, sgb_p, cnt0=jnp.zeros((1, LANES), F32))
    *part_s, cnt_s = merge(x_sample.reshape(ns, d), y_small[:bd].reshape(ns, hw), att_s.reshape(ns, aw),
                           sga_s[:ns], sgb_s[:ns], cnt0=cnt_p)

    out_p, out_s = _moe([part_p, part_s], cnt_s, w_expert_gate[l], w_expert_up[l], w_expert_down[l])

    return (out_p.reshape(b, s, d), out_s.reshape(bd, sd, d), new_k_p, new_v_p, st_p[None],
            new_k_s, new_v_s, st_small[:bd][None])
```

```python
import functools
import math

import numpy as np
import jax
import jax.numpy as jnp
from jax import lax
from jax.experimental import pallas as pl
from jax.experimental.pallas import tpu as pltpu

F32 = jnp.float32
BF16 = jnp.bfloat16
I32 = jnp.int32

CHUNK = 64
N_META = 16
PAST_LEN = 2048
EPS = 1e-6
HGRN_HEADS = 4
ATTN_HEADS = 8
KV_HEADS = 2
HEAD_DIM = 64
GQA_GROUP = ATTN_HEADS // KV_HEADS
WINDOW = 128
WINDOW_CHUNKS = WINDOW // CHUNK
NUM_BUCKETS = 32
MAX_DISTANCE = 128
N_GROUPS = 4
EXPERTS_PER_GROUP = 8
N_EXPERTS = N_GROUPS * EXPERTS_PER_GROUP
TOP_K = 2

LANES = 128
MXU_WIDTH = 256
VMEM_LIMIT = 56 * 1024 * 1024

INPROJ_TILE = 512
PROJ_TILE = 512
HGRN_CHUNK = 128
HGRN_CHUNKS_PER_STEP = 8
ATTN_CHUNKS_PER_STEP = 8
EXPERT_TILE = 512


LOG2E = math.log2(math.e)


def _sigmoid(x):
    return 1.0 / (1.0 + jnp.exp(-x))


def _split3(x):
    hi = x.astype(BF16)
    r1 = x - hi.astype(F32)
    mid = r1.astype(BF16)
    lo = (r1 - mid.astype(F32)).astype(BF16)
    return hi, mid, lo


def _dot(a, b):
    return jnp.dot(a, b, preferred_element_type=F32)


def _dot_nt(a, b):
    return lax.dot_general(a, b, (((1,), (1,)), ((), ())), preferred_element_type=F32)


def _dot_tn(a, b):
    return lax.dot_general(a, b, (((0,), (0,)), ((), ())), preferred_element_type=F32)


SUBLANES = 8


def _store_row_tiles(ref, x):
    n, d = x.shape
    sub = d // LANES
    for j in range(sub):
        ref[pl.ds(j, n, stride=sub), :] = x[:, j * LANES:(j + 1) * LANES]


def _load_row_tiles(ref, n, sub):
    return jnp.concatenate([ref[pl.ds(j, n, stride=sub), :] for j in range(sub)], axis=1)


def _params(*sem):
    return pltpu.CompilerParams(dimension_semantics=sem, vmem_limit_bytes=VMEM_LIMIT)


def _inproj_kernel(hw, aw, kvw, d, x_ref, wn_ref, w_ref, qg_ref, kg_ref, bdq_ref, bdk_ref,
                   qr_ref, z_ref, hv_ref, hg_ref, qa_ref, k_ref, v_ref, sga_ref, sgb_ref):
    x = x_ref[...]
    ms = jnp.mean(x * x, axis=-1, keepdims=True)
    xn = (x * lax.rsqrt(ms + EPS) * wn_ref[...]).astype(BF16)

    def seg(a, b):
        return _dot(xn, w_ref[:, a:b])

    def head_rms(a, bd_ref, gain):
        sq = (a * a).astype(BF16)
        wb = bd_ref.shape[0]
        m = jnp.concatenate([_dot(sq[:, c:c + wb], bd_ref[...]) for c in range(0, a.shape[1], wb)], axis=1)
        return a * lax.rsqrt(m + EPS) * gain

    o = 0
    hq = seg(o, o + hw)
    qr_ref[...] = (hq * _sigmoid(hq) * (hw // HGRN_HEADS) ** -0.5).astype(BF16)
    o += hw
    z_ref[...] = seg(o, o + hw)
    o += hw
    hv_ref[...] = seg(o, o + hw).astype(BF16)
    o += hw
    hg_ref[...] = seg(o, o + hw).astype(BF16)
    o += hw
    aq = seg(o, o + aw)
    qa_ref[...] = (head_rms(aq, bdq_ref, qg_ref[...]) * (HEAD_DIM ** -0.5 * LOG2E)).astype(BF16)
    o += aw
    k_ref[...] = head_rms(seg(o, o + kvw), bdk_ref, kg_ref[...])
    o += kvw
    v_ref[...] = seg(o, o + kvw)
    o += kvw
    sga_ref[...] = _sigmoid(seg(o, o + d)).astype(BF16)
    o += d
    sgb_ref[...] = _sigmoid(seg(o, o + d)).astype(BF16)


def _block_diag_mean(width, group):
    i = np.arange(width)
    return jnp.asarray((i[:, None] // group == i[None, :] // group) / group, dtype=BF16)


def _inproj(x, w_norm, w_in_bf16, q_gain, k_gain, hw, aw, kvw):
    t, d = x.shape
    tm = INPROJ_TILE if t % INPROJ_TILE == 0 else t
    cols = w_in_bf16.shape[1]
    row = lambda w: pl.BlockSpec((tm, w), lambda i: (i, 0))
    const = lambda a, b: pl.BlockSpec((a, b), lambda i: (0, 0))
    outs = [(hw, BF16), (hw, F32), (hw, BF16), (hw, BF16), (aw, BF16), (kvw, F32), (kvw, F32), (d, BF16), (d, BF16)]
    bdq = min(aw, MXU_WIDTH)
    assert aw % bdq == 0
    return pl.pallas_call(
        functools.partial(_inproj_kernel, hw, aw, kvw, d),
        grid=(t // tm,),
        in_specs=[row(d), const(1, d), const(d, cols), const(1, aw), const(1, kvw), const(bdq, bdq), const(kvw, kvw)],
        out_specs=[row(w) for w, _ in outs],
        out_shape=[jax.ShapeDtypeStruct((t, w), dt) for w, dt in outs],
        compiler_params=_params("arbitrary"),
        name="inproj",
    )(x, w_norm.reshape(1, d), w_in_bf16,
      jnp.tile(q_gain, aw // HEAD_DIM).reshape(1, aw), jnp.tile(k_gain, kvw // HEAD_DIM).reshape(1, kvw),
      _block_diag_mean(bdq, HEAD_DIM), _block_diag_mean(kvw, HEAD_DIM))


def _hgrn_consts(L):
    t = np.arange(L)
    u = t[None, :]
    blocks = [u <= t[:, None], u > t[:, None]]
    levels = []
    m = L // 2
    while m >= 1:
        levels.append(m)
        m //= 2
    lvl = np.full((L, L), -1, np.int32)
    lvl[t, t] = len(levels)
    isq_cols = []
    for j, m in enumerate(levels):
        bnd = (t // (2 * m)) * (2 * m) + m - 1
        isq = (t % (2 * m)) >= m
        cq = isq[:, None] & (u > bnd[:, None]) & (u <= t[:, None])
        ck = (~isq)[:, None] & (u > t[:, None]) & (u <= bnd[:, None])
        blocks.append(cq | ck)
        same = (t[:, None] // (2 * m)) == (t[None, :] // (2 * m))
        lvl[same & isq[:, None] & (~isq)[None, :]] = j
        isq_cols.append(isq)
    c = np.concatenate(blocks, axis=0).astype(np.float32)
    isq = np.stack(isq_cols, axis=1).astype(np.float32)
    isq = np.pad(isq, ((0, 0), (0, LANES - isq.shape[1])))
    c2 = np.tile(c, (1, 2))
    return jnp.asarray(c2, dtype=BF16), jnp.asarray(np.tile(lvl, (1, 2))), jnp.asarray(isq), len(levels)


def _hgrn_kernel(L, nlev, heads, dk, qr_ref, z_ref, hv_ref, hg_ref, lb_ref, og_ref, c_ref, lvl_ref, isq_ref,
                 s0_ref, y_ref, sout_ref, st_ref):
    c = pl.program_id(1)

    @pl.when(c == 0)
    def _():
        for h in range(heads):
            st_ref[h] = s0_ref[0, h].T

    for cc in range(z_ref.shape[1] // L):
        _hgrn_chunk(L, nlev, heads, dk, slice(cc * L, (cc + 1) * L), qr_ref, z_ref, hv_ref, hg_ref, lb_ref, og_ref,
                    c_ref, lvl_ref, isq_ref, y_ref, st_ref)

    @pl.when(c == pl.num_programs(1) - 1)
    def _():
        for h in range(heads):
            sout_ref[0, h] = st_ref[h].T


def _hgrn_chunk(L, nlev, heads, dk, rows, qr_ref, z_ref, hv_ref, hg_ref, lb_ref, og_ref, c_ref, lvl_ref, isq_ref,
                y_ref, st_ref):
    z = z_ref[0, rows, :]
    lb = lb_ref[...]
    e = jnp.exp(-jnp.abs(z))
    r = 1.0 / (1.0 + e)
    pos = z >= 0
    sig = jnp.where(pos, r, e * r)
    sig_neg = jnp.where(pos, e * r, r)
    logf = jnp.log(lb + (1.0 - lb) * sig)
    kin = (1.0 - lb) * sig_neg
    q = qr_ref[0, rows, :].astype(F32)

    hi, mid, _ = _split3(logf * LOG2E)
    ex = jnp.exp2(_dot(c_ref[...], jnp.concatenate([hi, mid], axis=0)))
    e_b = ex[0:L]
    e_rev = ex[L:2 * L]

    q_in = (q * e_b).astype(BF16)
    k_out = (kin * e_rev).astype(BF16)
    q_b = q.astype(BF16)
    k_b = kin.astype(BF16)
    xs = []
    for j in range(nlev):
        m = L >> (j + 1)
        if m % SUBLANES == 0:
            qk = jnp.concatenate([(q if blk % 2 else kin)[blk * m:(blk + 1) * m] for blk in range(L // m)], axis=0)
        else:
            qk = jnp.where(isq_ref[:, j:j + 1] > 0.5, q, kin)
        xs.append((qk * ex[(2 + j) * L:(3 + j) * L]).astype(BF16))
    lvl = lvl_ref[...]
    v = hv_ref[0, rows, :]
    g = hg_ref[0, rows, :].astype(F32)
    og = og_ref[...]

    def block_diag(x):
        zero = jnp.zeros((x.shape[0], dk), x.dtype)
        return jnp.concatenate([jnp.concatenate([x[:, :dk], zero], axis=1),
                                jnp.concatenate([zero, x[:, dk:]], axis=1)], axis=0)

    for pair in range(heads // 2):
        sl = slice(2 * pair * dk, 2 * (pair + 1) * dk)
        a = jnp.where(lvl == nlev, _dot_nt(q_b[:, sl], block_diag(k_b[:, sl])), 0.0)
        for j in range(nlev):
            xp = xs[j][:, sl]
            a = jnp.where(lvl == j, _dot_nt(xp, block_diag(xp)), a)
        st = jnp.concatenate([st_ref[2 * pair], st_ref[2 * pair + 1]], axis=1)
        vp = v[:, sl]
        o = _dot(a.astype(BF16), block_diag(vp)) + _dot_nt(q_in[:, sl], block_diag(st.astype(BF16)))
        for half in range(2):
            h = 2 * pair + half
            hs = slice(h * dk, (h + 1) * dk)
            st_ref[h] = st_ref[h] * e_b[L - 1:L, hs] + _dot_tn(v[:, hs], k_out[:, hs])
            oh = o[:, half * dk:(half + 1) * dk]
            ms = jnp.mean(oh * oh, axis=-1, keepdims=True)
            gh = g[:, hs]
            y_ref[0, rows, hs] = (oh * lax.rsqrt(ms + EPS) * og[:, hs] * (gh * _sigmoid(gh))).astype(BF16)


def _hgrn(qr, z, hv, hg, lb, out_gain, s0, L):
    b, s, w = z.shape
    heads, dk = s0.shape[1], s0.shape[2]
    cm, lvl, isq, nlev = _hgrn_consts(L)
    per_step = HGRN_CHUNKS_PER_STEP if s % (HGRN_CHUNKS_PER_STEP * L) == 0 else 1
    seq = pl.BlockSpec((1, per_step * L, w), lambda i, c: (i, c, 0))
    const = lambda a: pl.BlockSpec(a.shape, lambda i, c: (0,) * a.ndim)
    state = pl.BlockSpec((1, heads, dk, dk), lambda i, c: (i, 0, 0, 0))
    lb2 = lb.reshape(1, w)
    og2 = jnp.tile(out_gain, heads).reshape(1, w)
    return pl.pallas_call(
        functools.partial(_hgrn_kernel, L, nlev, heads, dk),
        grid=(b, s // (per_step * L)),
        in_specs=[seq, seq, seq, seq, const(lb2), const(og2), const(cm), const(lvl), const(isq), state],
        out_specs=[seq, state],
        out_shape=[jax.ShapeDtypeStruct((b, s, w), BF16), jax.ShapeDtypeStruct(s0.shape, F32)],
        scratch_shapes=[pltpu.VMEM((heads, dk, dk), F32)],
        compiler_params=_params("arbitrary", "arbitrary"),
        name=f"hgrn_scan_{L}",
    )(qr, z, hv, hg, lb2, og2, cm, lvl, isq, s0)


def _t5_bucket_np(rel):
    half = NUM_BUCKETS // 2
    max_exact = half // 2
    assert (NUM_BUCKETS, MAX_DISTANCE) == (32, 128)
    n = np.abs(rel).astype(np.int64)
    nn = np.maximum(n, 1)
    k = np.zeros_like(nn)
    for j in range(1, 48):
        k = np.where(64 * (1 << j) <= nn * nn, j, k)
    large = np.minimum(max_exact + k, half - 1)
    return np.where(rel > 0, half, 0) + np.where(n < max_exact, n, large)


HEADS_PER_COL = LANES // HEAD_DIM
COLS_PER_GROUP = GQA_GROUP // HEADS_PER_COL
HEAD_ORDER = tuple(g * GQA_GROUP + col * HEADS_PER_COL + half
                   for g in range(KV_HEADS) for half in range(HEADS_PER_COL) for col in range(COLS_PER_GROUP))
KV_EXPAND = KV_HEADS * HEADS_PER_COL


def _expand_kv(x, with_ones=False):
    assert HEADS_PER_COL == 2 and KV_HEADS == 2 and x.shape[1] == LANES
    low = lax.broadcasted_iota(I32, x.shape, 1) < HEAD_DIM
    xr = pltpu.roll(x, HEAD_DIM, axis=1)
    zero = jnp.zeros_like(x)
    blocks = [jnp.where(low, x, zero), jnp.where(low, zero, xr), jnp.where(low, xr, zero), jnp.where(low, zero, x)]
    if with_ones:
        blocks = [b for blk in blocks for b in (blk, jnp.ones_like(x))]
    return jnp.concatenate(blocks, axis=1).astype(BF16)


def _attn_core(q, kx, vx, bias):
    tq = q.shape[0]
    scores = []
    for g in range(KV_HEADS):
        cols = [q[:, (g * COLS_PER_GROUP + c) * LANES:(g * COLS_PER_GROUP + c + 1) * LANES]
                for c in range(COLS_PER_GROUP)]
        qst = jnp.concatenate(cols, axis=0)
        for half in range(HEADS_PER_COL):
            blk = g * HEADS_PER_COL + half
            scores.append(_dot_nt(qst, kx[:, blk * LANES:(blk + 1) * LANES]))
    s = jnp.concatenate(scores, axis=0) + bias
    pb = jnp.exp2(s - jnp.max(s, axis=-1, keepdims=True)).astype(BF16)
    rows = COLS_PER_GROUP * tq
    outs = []
    for g in range(KV_HEADS):
        o = None
        for half in range(HEADS_PER_COL):
            blk = g * HEADS_PER_COL + half
            pv = _dot(pb[blk * rows:(blk + 1) * rows], vx[:, 2 * blk * LANES:2 * (blk + 1) * LANES])
            part = pv[:, :LANES] * (1.0 / pv[:, LANES:])
            o = part if o is None else o + part
        outs.extend(o[c * tq:(c + 1) * tq] for c in range(COLS_PER_GROUP))
    return outs


def _bias_rows(table, bucket):
    onehot = (jnp.asarray(bucket)[..., None] == jnp.arange(NUM_BUCKETS)).astype(F32)
    cols = jnp.stack([table[:, h] for h in HEAD_ORDER], axis=1)
    bias = jnp.einsum('...qkb,bh->...hqk', onehot, cols, precision=lax.Precision.HIGHEST)
    return bias.reshape(*bucket.shape[:-2], ATTN_HEADS * bucket.shape[-2], bucket.shape[-1]) * LOG2E


def _pad_keys(bias, sinks):
    tk = bias.shape[-1]
    n_pad = -tk % LANES or LANES
    tq = bias.shape[-2] // ATTN_HEADS
    sink = jnp.repeat(jnp.stack([sinks[h] for h in HEAD_ORDER]).astype(F32), tq) * LOG2E
    sink = jnp.broadcast_to(sink[:, None], bias.shape[:-1] + (1,))
    masked = jnp.full(bias.shape[:-1] + (n_pad - 1,), -jnp.inf, F32)
    return jnp.concatenate([bias, sink, masked], axis=-1)


def _store_heads(ref, x):
    for g in range(KV_HEADS):
        ref[0, 0, :, g, :] = x[:, g * HEAD_DIM:(g + 1) * HEAD_DIM]


def _load_heads(ref):
    return jnp.concatenate([ref[0, 0, :, g, :] for g in range(KV_HEADS)], axis=1)


def _attn_prompt_kernel(cb, q_ref, k_ref, v_ref, km_ref, vm_ref, bias_ref, o_ref, nk_ref, nv_ref,
                        kx_ref, vx_ref):
    step = pl.program_id(1)
    s_len = k_ref.shape[1]
    meta_at = WINDOW + s_len

    @pl.when(step == pl.num_programs(1) - 1)
    def _():
        _store_heads(nk_ref, k_ref[0, s_len - WINDOW:, :])
        _store_heads(nv_ref, v_ref[0, s_len - WINDOW:, :])

    @pl.when(step == 0)
    def _():
        piece = min(s_len, 512)
        for src, meta, dst in ((k_ref, km_ref, kx_ref), (v_ref, vm_ref, vx_ref)):
            expand = functools.partial(_expand_kv, with_ones=dst is vx_ref)
            blank = lambda n: expand(jnp.zeros((n, src.shape[2]), F32))
            dst[0:WINDOW] = blank(WINDOW)
            for r in range(0, s_len, piece):
                dst[WINDOW + r:WINDOW + r + piece] = expand(src[0, r:r + piece, :])
            dst[meta_at:meta_at + N_META] = expand(meta[...])
            dst[meta_at + N_META:] = blank(dst.shape[0] - meta_at - N_META)

    win = WINDOW + CHUNK
    tail = kx_ref.shape[0] - meta_at
    for j in range(cb):
        c = step * cb + j
        start = pl.multiple_of(c * CHUNK, CHUNK)
        kall = jnp.concatenate([kx_ref[pl.ds(start, win), :], kx_ref[meta_at:meta_at + tail, :]], axis=0)
        vall = jnp.concatenate([vx_ref[pl.ds(start, win), :], vx_ref[meta_at:meta_at + tail, :]], axis=0)
        rows = slice(j * CHUNK, (j + 1) * CHUNK)
        outs = _attn_core(q_ref[0, rows, :], kall, vall, bias_ref[jnp.minimum(c, bias_ref.shape[0] - 1)])
        for ci, o in enumerate(outs):
            o_ref[0, rows, ci * LANES:(ci + 1) * LANES] = o.astype(o_ref.dtype)


def _attn_prompt(q, k, v, k_meta, v_meta, table, sinks):
    b, s, aw = q.shape
    kvw = k.shape[-1]
    nc = s // CHUNK
    cb = ATTN_CHUNKS_PER_STEP if nc % ATTN_CHUNKS_PER_STEP == 0 else 1
    assert s % min(s, 512) == 0
    n_bias = 1
    while True:
        qpos = N_META + (n_bias - 1) * CHUNK
        if np.all(_t5_bucket_np(np.arange(N_META) - qpos) == _t5_bucket_np(np.arange(N_META) - qpos - 10 ** 6)):
            break
        n_bias += 1
    n_bias = min(max(n_bias, WINDOW_CHUNKS + 1), nc)
    cs = np.arange(n_bias)[:, None]
    qpos = N_META + cs * CHUNK + np.arange(CHUNK)[None]
    wpos = N_META + (cs - WINDOW_CHUNKS) * CHUNK + np.arange(WINDOW + CHUNK)[None]
    kpos = np.concatenate([wpos, np.broadcast_to(np.arange(N_META), (n_bias, N_META))], axis=1)
    valid = np.concatenate([wpos >= N_META, np.ones((n_bias, N_META), bool)], axis=1)
    bias = _bias_rows(table, _t5_bucket_np(kpos[:, None, :] - qpos[:, :, None]))
    bias = _pad_keys(jnp.where(valid[:, None, :], bias, -jnp.inf), sinks)
    qs =pl.BlockSpec((1, cb * CHUNK, aw), lambda i, c: (i, c, 0))
    kv = pl.BlockSpec((1, s, kvw), lambda i, c: (i, 0, 0))
    meta = pl.BlockSpec((N_META, kvw), lambda i, c: (0, 0))
    xrows = WINDOW + s + bias.shape[-1] - (WINDOW + CHUNK)
    cache_shape = (1, b, WINDOW, KV_HEADS, HEAD_DIM)
    cache = pl.BlockSpec((1, 1) + cache_shape[2:], lambda i, c: (0, i, 0, 0, 0))
    return pl.pallas_call(
        functools.partial(_attn_prompt_kernel, cb),
        grid=(b, nc // cb),
        in_specs=[qs, kv, kv, meta, meta, pl.BlockSpec(bias.shape, lambda i, c: (0, 0, 0))],
        out_specs=[qs, cache, cache],
        out_shape=[jax.ShapeDtypeStruct((b, s, aw), BF16)] + [jax.ShapeDtypeStruct(cache_shape, F32)] * 2,
        scratch_shapes=[pltpu.VMEM((xrows, KV_EXPAND * LANES), BF16),
                        pltpu.VMEM((xrows, 2 * KV_EXPAND * LANES), BF16)],
        compiler_params=_params("arbitrary", "arbitrary"),
        name="attn_prompt",
    )(q, k, v, k_meta, v_meta, bias)


def _attn_sample_kernel(q_ref, kc_ref, vc_ref, kn_ref, vn_ref, km_ref, vm_ref, bias_ref, o_ref, nk_ref, nv_ref):
    tk = kc_ref.shape[2] + kn_ref.shape[1] + km_ref.shape[0]
    zeros = jnp.zeros((bias_ref.shape[1] - tk, km_ref.shape[1]), F32)
    kc, vc, kn, vn = _load_heads(kc_ref), _load_heads(vc_ref), kn_ref[0], vn_ref[0]
    kall = _expand_kv(jnp.concatenate([kc, kn, km_ref[...], zeros], axis=0))
    vall = _expand_kv(jnp.concatenate([vc, vn, vm_ref[...], zeros], axis=0), with_ones=True)
    outs = _attn_core(q_ref[0], kall, vall, bias_ref[...])
    for ci, o in enumerate(outs):
        o_ref[0, :, ci * LANES:(ci + 1) * LANES] = o.astype(o_ref.dtype)
    n_new = kn.shape[0]
    _store_heads(nk_ref, jnp.concatenate([kc[n_new:], kn], axis=0))
    _store_heads(nv_ref, jnp.concatenate([vc[n_new:], vn], axis=0))


def _attn_sample(q, k_cache, v_cache, k_new, v_new, k_meta, v_meta, table, sinks):
    bd, sd, aw = q.shape
    kvw = k_new.shape[-1]
    win = k_cache.shape[2]
    cache = pl.BlockSpec((1, 1) + k_cache.shape[2:], lambda i: (0, i, 0, 0, 0))
    qpos = N_META + PAST_LEN + np.arange(sd)
    kpos = np.concatenate([N_META + PAST_LEN - win + np.arange(win), qpos, np.arange(N_META)])
    bias = _pad_keys(_bias_rows(table, _t5_bucket_np(kpos[None, :] - qpos[:, None])), sinks)
    per = lambda n, w: pl.BlockSpec((1, n, w), lambda i: (i, 0, 0))
    meta = pl.BlockSpec((N_META, kvw), lambda i: (0, 0))
    return pl.pallas_call(
        _attn_sample_kernel,
        grid=(bd,),
        in_specs=[per(sd, aw), cache, cache, per(sd, kvw), per(sd, kvw), meta, meta,
                  pl.BlockSpec(bias.shape, lambda i: (0, 0))],
        out_specs=[per(sd, aw), cache, cache],
        out_shape=[jax.ShapeDtypeStruct((bd, sd, aw), BF16)] + [jax.ShapeDtypeStruct(k_cache.shape, F32)] * 2,
        compiler_params=_params("arbitrary"),
        name="attn_sample",
    )(q, k_cache, v_cache, k_new, v_new, k_meta, v_meta, bias)


ROUTE_E1, ROUTE_E2, ROUTE_R1, ROUTE_R2, ROUTE_W1, ROUTE_W2 = range(6)
META_FIELDS = 2 * TOP_K


def _merge_kernel(hw, x_ref, yr_ref, at_ref, sga_ref, sgb_ref, wb_ref, wo_ref, wn_ref, wr_ref, br_ref, cnt0_ref, tri_ref,
                  h_ref, xn_ref, route_ref, meta_ref, cnt_ref, carry_ref):
    @pl.when(pl.program_id(0) == 0)
    def _():
        carry_ref[...] = cnt0_ref[...]

    logits = _project_rows(hw, x_ref, yr_ref, at_ref, sga_ref, sgb_ref, wb_ref, wo_ref, wn_ref, wr_ref, br_ref,
                           h_ref, xn_ref)
    _route_rows(logits, tri_ref, route_ref, meta_ref, carry_ref)
    cnt_ref[...] = carry_ref[...]


def _project_rows(hw, x_ref, yr_ref, at_ref, sga_ref, sgb_ref, wb_ref, wo_ref, wn_ref, wr_ref, br_ref, h_ref, xn_ref):
    br = _dot(yr_ref[...], wb_ref[0:hw, :])
    ba = _dot(at_ref[...], wb_ref[hw:, :])
    merged = sga_ref[...].astype(F32) * br + sgb_ref[...].astype(F32) * ba
    h = x_ref[...] + _dot(merged.astype(BF16), wo_ref[...])
    h_ref[...] = h
    ms = jnp.mean(h * h, axis=-1, keepdims=True)
    xn = h * lax.rsqrt(ms + EPS) * wn_ref[...]
    _store_row_tiles(xn_ref, xn)

    x_hi = xn.astype(BF16)
    x_lo = (xn - x_hi.astype(F32)).astype(BF16)
    both = _dot(x_hi, wr_ref[...])
    return both[:, :LANES] + both[:, LANES:] + _dot(x_lo, wr_ref[:, :LANES]) + br_ref[...]


def _route_rows(logits, tri_ref, route_ref, meta_ref, carry_ref):
    tm = logits.shape[0]
    lane_i = lax.broadcasted_iota(I32, (tm, LANES), 1)
    lane = lane_i.astype(F32)
    group_of_lane = (lane_i >> int(math.log2(EXPERTS_PER_GROUP))).astype(F32)
    ninf = -jnp.inf
    first = lambda hit, idx: jnp.min(jnp.where(hit, idx, float(LANES)), axis=-1, keepdims=True)
    gmask = (lane_i >= N_EXPERTS) & (lane_i < N_EXPERTS + N_GROUPS)
    gl = jnp.where(gmask, logits, ninf)
    gmax = jnp.max(gl, axis=-1, keepdims=True)
    gidx = first(gl == gmax, lane - N_EXPERTS)
    gval = 1.0 / jnp.sum(jnp.exp(gl - gmax), axis=-1, keepdims=True)
    emask = (lane_i < N_EXPERTS) & (group_of_lane == gidx)
    el = jnp.where(emask, logits, ninf)
    m1 = jnp.max(el, axis=-1, keepdims=True)
    i1 = first(el == m1, lane)
    el2 = jnp.where(lane == i1, ninf, el)
    m2 = jnp.max(el2, axis=-1, keepdims=True)
    i2 = first(el2 == m2, lane)
    e21 = jnp.exp(m2 - m1)
    w1 = gval / (1.0 + e21)
    w2 = gval * e21 / (1.0 + e21)

    sel1 = lane == i1
    sel2 = lane == i2
    oh = (sel1 | sel2).astype(BF16)
    before = _dot(tri_ref[...], oh) + carry_ref[...]
    r1 = jnp.sum(jnp.where(sel1, before, 0.0), axis=-1, keepdims=True)
    r2 = jnp.sum(jnp.where(sel2, before, 0.0), axis=-1, keepdims=True)
    carry_ref[...] = carry_ref[...] + jnp.sum(oh.astype(F32), axis=0, keepdims=True)

    rec = jnp.zeros((tm, LANES), F32)
    for slot, val in ((ROUTE_E1, i1), (ROUTE_E2, i2), (ROUTE_W1, w1), (ROUTE_W2, w2), (ROUTE_R1, r1), (ROUTE_R2, r2)):
        rec = jnp.where(lane_i == slot, val, rec)
    route_ref[...] = rec
    meta_ref[0] = rec.T[ROUTE_E1:ROUTE_E1 + META_FIELDS].astype(I32)


def _merge(x, y_rec, att, sga, sgb, wb, wo, w_norm, w_router, b_router, cnt0):
    t, d = x.shape
    hw = y_rec.shape[1]
    tm = PROJ_TILE if t % PROJ_TILE == 0 else t
    row = lambda w: pl.BlockSpec((tm, w), lambda i: (i, 0))
    const = lambda a, b: pl.BlockSpec((a, b), lambda i: (0, 0))
    return pl.pallas_call(
        functools.partial(_merge_kernel, hw),
        grid=(t // tm,),
        in_specs=[row(d), row(hw), row(att.shape[1]), row(d), row(d), const(*wb.shape), const(d, d), const(1, d),
                  const(d, 2 * LANES), const(1, LANES), const(1, LANES), const(tm, tm)],
        out_specs=[row(d), pl.BlockSpec((tm * d // LANES, LANES), lambda i: (i, 0)), row(LANES),
                   pl.BlockSpec((1, META_FIELDS, tm), lambda i: (i, 0, 0)), const(1, LANES)],
        out_shape=[jax.ShapeDtypeStruct((t, d), F32), jax.ShapeDtypeStruct((t * d // LANES, LANES), F32),
                   jax.ShapeDtypeStruct((t, LANES), F32), jax.ShapeDtypeStruct((t // tm, META_FIELDS, tm), I32),
                   jax.ShapeDtypeStruct((1, LANES), F32)],
        scratch_shapes=[pltpu.VMEM((1, LANES), F32)],
        compiler_params=_params("arbitrary"),
        name="merge_route",
    )(x, y_rec, att, sga, sgb, wb, wo, w_norm.reshape(1, d), w_router, b_router, cnt0,
      jnp.asarray(np.tri(tm, k=-1), dtype=BF16))


def _row_copy(src, dst, sem):
    return pltpu.make_async_copy(src, dst, sem)


ROW_UNROLL = 8


def _tile_rows(ref, row, sub):
    return ref.at[pl.ds(pl.multiple_of(row * sub, sub), sub)]


def _scatter_kernel(n_tok, sub, first, slot_ref, pstart_ref, pend_ref, x_ref, *rest):
    xs_ref, zero_ref, sem, zsem, tsem = rest[-5:]

    def zero_blocks(blocks, zs, wait):
        for cond, row in blocks:
            @pl.when(cond)
            def _():
                at = row * sub if isinstance(row, int) else pl.multiple_of(row * sub, EXPERT_TILE * sub)
                cp = _row_copy(zero_ref, xs_ref.at[pl.ds(at, EXPERT_TILE * sub)], zs)
                cp.wait() if wait else cp.start()

    if first:
        n_rows = xs_ref.shape[0] // sub
        tails = [(pend_ref[e] > pstart_ref[e], pend_ref[e] - EXPERT_TILE) for e in range(N_EXPERTS)]
        unused = [(n_rows - (j + 1) * EXPERT_TILE >= pend_ref[N_EXPERTS - 1], n_rows - (j + 1) * EXPERT_TILE)
                  for j in range(N_EXPERTS)]

        @pl.when(pl.program_id(0) == 0)
        def _():
            zero_ref[...] = jnp.zeros_like(zero_ref)
            zero_blocks(tails, zsem, False)
            zero_blocks(unused, tsem, False)
            zero_blocks(tails, zsem, True)

    def issue(grp, _):
        for u in range(ROW_UNROLL):
            r = grp * ROW_UNROLL + u
            for k in range(TOP_K):
                _row_copy(_tile_rows(x_ref, r, sub), _tile_rows(xs_ref, slot_ref[0, k, r], sub),
                          sem).start(priority=(u * TOP_K + k) % 2)
        return 0

    lax.fori_loop(0, n_tok // ROW_UNROLL, issue, 0)
    for k in range(TOP_K):
        _row_copy(x_ref, xs_ref.at[pl.ds(0, n_tok * sub)], sem).wait()

    if first:
        @pl.when(pl.program_id(0) == 0)
        def _():
            zero_blocks(unused, tsem, True)


def _scatter_rows(x, slots, pstarts, pends, xs, rows):
    n, fields, tm = slots.shape
    sub = x.shape[0] // (n * tm)
    first = xs is None
    smem = pl.BlockSpec(memory_space=pltpu.SMEM)
    return pl.pallas_call(
        functools.partial(_scatter_kernel, tm, sub, first),
        grid=(n,),
        in_specs=[pl.BlockSpec((1, fields, tm), lambda i: (i, 0, 0), memory_space=pltpu.SMEM), smem, smem,
                  pl.BlockSpec((tm * sub, LANES), lambda i: (i, 0))]
        + ([] if first else [pl.BlockSpec(memory_space=pl.ANY)]),
        out_specs=pl.BlockSpec(memory_space=pl.ANY),
        out_shape=jax.ShapeDtypeStruct((rows * sub, LANES), F32),
        scratch_shapes=[pltpu.VMEM((EXPERT_TILE * sub, LANES), F32)] + [pltpu.SemaphoreType.DMA(())] * 3,
        input_output_aliases={} if first else {4: 0},
        compiler_params=_params("arbitrary"),
        name="moe_scatter",
    )(slots, pstarts, pends, x, *([] if first else [xs]))


def _expert_kernel(tm, sub, be_ref, nb_ref, x_ref, wg_ref, wu_ref, wd_ref, y_ref, wgb_ref, wub_ref, wdb_ref):
    i = pl.program_id(0)
    used = i < nb_ref[0]

    @pl.when(used & ((i == 0) | (be_ref[i] != be_ref[jnp.maximum(i - 1, 0)])))
    def _():
        wgb_ref[...] = wg_ref[0].astype(BF16)
        wub_ref[...] = wu_ref[0].astype(BF16)
        wdb_ref[...] = wd_ref[0].astype(BF16)

    @pl.when(used)
    def _():
        x = _load_row_tiles(x_ref, tm, sub).astype(BF16)
        g = _dot(x, wgb_ref[...])
        u = _dot(x, wub_ref[...])
        hmid = (g * _sigmoid(g) * u).astype(BF16)
        _store_row_tiles(y_ref, _dot(hmid, wdb_ref[...]))

    @pl.when(jnp.logical_not(used))
    def _():
        y_ref[...] = jnp.zeros_like(y_ref)


def _experts(xs, block_e, n_used, wg, wu, wd):
    _, d, ff = wg.shape
    sub = d // LANES
    tm = EXPERT_TILE
    blk = pl.BlockSpec((tm * sub, LANES), lambda i, be, nb: (i, 0))
    grid_spec = pltpu.PrefetchScalarGridSpec(
        num_scalar_prefetch=2,
        grid=(xs.shape[0] // (tm * sub),),
        in_specs=[blk,
                  pl.BlockSpec((1, d, ff), lambda i, be, nb: (be[i], 0, 0)),
                  pl.BlockSpec((1, d, ff), lambda i, be, nb: (be[i], 0, 0)),
                  pl.BlockSpec((1, ff, d), lambda i, be, nb: (be[i], 0, 0))],
        out_specs=blk,
        scratch_shapes=[pltpu.VMEM((d, ff), BF16), pltpu.VMEM((d, ff), BF16), pltpu.VMEM((ff, d), BF16)],
    )
    return pl.pallas_call(
        functools.partial(_expert_kernel, tm, sub),
        grid_spec=grid_spec,
        out_shape=jax.ShapeDtypeStruct(xs.shape, F32),
        compiler_params=_params("arbitrary"),
        name="moe_experts",
    )(block_e, n_used, xs, wg, wu, wd)


def _combine_kernel(n_tok, sub, scur_ref, snext_ref, h_ref, route_ref, ys_ref, o_ref, buf, sem):
    i = pl.program_id(0)
    n = pl.num_programs(0)
    slot = i % 2

    def start(slot_ref, s):
        def issue(grp, _):
            for u in range(ROW_UNROLL):
                r = grp * ROW_UNROLL + u
                for k in range(TOP_K):
                    _row_copy(_tile_rows(ys_ref, slot_ref[0, k, r], sub), _tile_rows(buf.at[s, k], r, sub),
                              sem.at[s]).start(priority=(u * TOP_K + k) % 2)
            return 0
        lax.fori_loop(0, n_tok // ROW_UNROLL, issue, 0)

    @pl.when(i == 0)
    def _():
        start(scur_ref, 0)

    @pl.when(i + 1 < n)
    def _():
        start(snext_ref, 1 - slot)

    for k in range(TOP_K):
        _row_copy(ys_ref.at[pl.ds(0, n_tok * sub)], buf.at[slot, k], sem.at[slot]).wait()
    route = route_ref[...]
    w1 = route[:, ROUTE_W1:ROUTE_W1 + 1]
    w2 = route[:, ROUTE_W2:ROUTE_W2 + 1]
    for j in range(sub):
        cols = slice(j * LANES, (j + 1) * LANES)
        part = lambda k: buf[slot, k, pl.ds(j, n_tok, stride=sub), :]
        o_ref[:, cols] = h_ref[:, cols] + (part(0) * w1 + part(1) * w2)


def _combine(h, route, slots, ys):
    t, d = h.shape
    n, fields, tm = slots.shape
    sub = d // LANES
    mspec = lambda f: pl.BlockSpec((1, fields, tm), f, memory_space=pltpu.SMEM)
    return pl.pallas_call(
        functools.partial(_combine_kernel, tm, sub),
        grid=(n,),
        in_specs=[mspec(lambda i: (i, 0, 0)), mspec(lambda i: (jnp.minimum(i + 1, n - 1), 0, 0)),
                  pl.BlockSpec((tm, d), lambda i: (i, 0)), pl.BlockSpec((tm, LANES), lambda i: (i, 0)),
                  pl.BlockSpec(memory_space=pl.ANY)],
        out_specs=pl.BlockSpec((tm, d), lambda i: (i, 0)),
        out_shape=jax.ShapeDtypeStruct((t, d), F32),
        scratch_shapes=[pltpu.VMEM((2, TOP_K, tm * sub, LANES), F32), pltpu.SemaphoreType.DMA((2,))],
        compiler_params=_params("arbitrary"),
        name="moe_combine",
    )(slots, slots, h, route, ys)


def _moe(parts, counts, wg, wu, wd):
    tm = EXPERT_TILE
    n_assign = sum(part[0].shape[0] for part in parts) * TOP_K
    n_blocks = -(-(n_assign + N_EXPERTS * (tm - 1)) // tm)
    counts = counts[0, :N_EXPERTS].astype(I32)
    pcounts = (counts + tm - 1) // tm * tm
    pends = jnp.cumsum(pcounts)
    pstarts = pends - pcounts
    block_start = jnp.arange(n_blocks, dtype=I32) * tm
    block_e = jnp.minimum(jnp.sum((pends[None, :] <= block_start[:, None]).astype(I32), axis=1), N_EXPERTS - 1)
    n_used = pends[-1:] // tm
    xs = None
    slots = []
    for _, xn, _, meta in parts:
        assert meta.shape[-1] % ROW_UNROLL == 0
        e, pos = meta[:, :TOP_K], meta[:, TOP_K:]
        seg = jnp.sum(jnp.where(e[..., None] == jnp.arange(N_EXPERTS, dtype=I32), pstarts, 0), axis=-1)
        slots.append(seg + pos)
        xs = _scatter_rows(xn, slots[-1], pstarts, pends, xs, n_blocks * tm)
    ys = _experts(xs, block_e, n_used, wg, wu, wd)
    return [_combine(h, route, s, ys) for (h, _, route, _), s in zip(parts, slots)]


def kernel(x_prompt, x_sample, cache_swa_k, cache_swa_v, state_hgrn, meta_tokens, rel_bias_table, hgrn_lower_bounds, w_norm_mix, w_in, hgrn_out_norm, q_norm, k_norm, attn_sinks, w_branch, w_out, w_norm_ffn, w_router_group, b_router_group, w_router_expert, b_router_expert, w_expert_gate, w_expert_up, w_expert_down):
    b, s, d = x_prompt.shape
    bd, sd, _ = x_sample.shape
    depth, _, heads, dk, dv = state_hgrn.shape
    assert depth == 1 and heads == HGRN_HEADS and dk == dv
    hw = heads * dk
    aw = ATTN_HEADS * HEAD_DIM
    kvw = KV_HEADS * HEAD_DIM
    assert w_in.shape[-1] == 4 * hw + aw + 2 * kvw + 2 * d
    assert s % HGRN_CHUNK == 0 and s % CHUNK == 0 and sd == N_META and N_EXPERTS + N_GROUPS <= LANES
    l = 0

    p = jax.nn.softmax(hgrn_lower_bounds.astype(F32), axis=0)
    lb = jnp.cumsum(p, axis=0)[l + 1] - p[0]

    w_in_b = w_in[l].astype(BF16)
    proj = functools.partial(_inproj, w_norm=w_norm_mix[l], w_in_bf16=w_in_b, q_gain=q_norm[l], k_gain=k_norm[l],
                             hw=hw, aw=aw, kvw=kvw)
    x_small = jnp.concatenate([x_sample.reshape(bd * sd, d), meta_tokens.astype(F32)], axis=0)
    qr_s, z_s, hv_s, hg_s, qa_s, k_s, v_s, sga_s, sgb_s = proj(x_small)
    qr_p, z_p, hv_p, hg_p, qa_p, k_p, v_p, sga_p, sgb_p = proj(x_prompt.reshape(b * s, d))
    ns = bd * sd
    k_meta, v_meta = k_s[ns:], v_s[ns:]

    streams = lambda a, n: a.reshape(n, -1, a.shape[-1])
    s0_small = jnp.concatenate([state_hgrn[l].astype(F32), jnp.zeros((1, heads, dk, dv), F32)], axis=0)
    y_small, st_small = _hgrn(streams(qr_s, bd + 1), streams(z_s, bd + 1), streams(hv_s, bd + 1),
                              streams(hg_s, bd + 1), lb, hgrn_out_norm[l], s0_small, sd)
    s0_p = jnp.broadcast_to(st_small[bd:], (b, heads, dk, dv))
    y_p, st_p = _hgrn(streams(qr_p, b), streams(z_p, b), streams(hv_p, b), streams(hg_p, b), lb,
                      hgrn_out_norm[l], s0_p, HGRN_CHUNK)

    table = rel_bias_table.astype(F32)
    att_p, new_k_p, new_v_p = _attn_prompt(streams(qa_p, b), streams(k_p, b), streams(v_p, b), k_meta, v_meta,
                                           table, attn_sinks[l])
    k_new, v_new = k_s[:ns].reshape(bd, sd, kvw), v_s[:ns].reshape(bd, sd, kvw)
    att_s, new_k_s, new_v_s = _attn_sample(qa_s[:ns].reshape(bd, sd, aw), cache_swa_k[l:l + 1].astype(F32),
                                           cache_swa_v[l:l + 1].astype(F32), k_new, v_new, k_meta, v_meta,
                                           table, attn_sinks[l])

    wb = w_branch[l].astype(BF16)
    wo = w_out[l].astype(BF16)
    w_router = jnp.pad(jnp.concatenate([w_router_expert[l], w_router_group[l]], axis=1).astype(F32),
                       ((0, 0), (0, LANES - N_EXPERTS - N_GROUPS)))
    b_router = jnp.pad(jnp.concatenate([b_router_expert[l], b_router_group[l]]).astype(F32),
                       (0, LANES - N_EXPERTS - N_GROUPS)).reshape(1, LANES)
    w_router_hi = w_router.astype(BF16)
    w_router_lo = (w_router - w_router_hi.astype(F32)).astype(BF16)
    w_router = jnp.concatenate([w_router_hi, w_router_lo], axis=1)
    merge = functools.partial(_merge, wb=wb, wo=wo, w_norm=w_norm_ffn[l], w_router=w_router, b_router=b_router)
    *part_p, cnt_p = merge(x_prompt.reshape(b * s, d), y_p.reshape(b * s, hw), att_p.reshape(b * s, aw),
                           sga_p, sgb_p, cnt0=jnp.zeros((1, LANES), F32))
    *part_s, cnt_s = merge(x_sample.reshape(ns, d), y_small[:bd].reshape(ns, hw), att_s.reshape(ns, aw),
                           sga_s[:ns], sgb_s[:ns], cnt0=cnt_p)

    out_p, out_s = _moe([part_p, part_s], cnt_s, w_expert_gate[l], w_expert_up[l], w_expert_down[l])

    return (out_p.reshape(b, s, d), out_s.reshape(bd, sd, d), new_k_p, new_v_p, st_p[None],
            new_k_s, new_v_s, st_small[:bd][None])
```

```python
import functools
import math

import numpy as np
import jax
import jax.numpy as jnp
from jax import lax
from jax.experimental import pallas as pl
from jax.experimental.pallas import tpu as pltpu

F32 = jnp.float32
BF16 = jnp.bfloat16
I32 = jnp.int32

CHUNK = 64
N_META = 16
PAST_LEN = 2048
EPS = 1e-6
HGRN_HEADS = 4
ATTN_HEADS = 8
KV_HEADS = 2
HEAD_DIM = 64
GQA_GROUP = ATTN_HEADS // KV_HEADS
WINDOW = 128
WINDOW_CHUNKS = WINDOW // CHUNK
NUM_BUCKETS = 32
MAX_DISTANCE = 128
N_GROUPS = 4
EXPERTS_PER_GROUP = 8
N_EXPERTS = N_GROUPS * EXPERTS_PER_GROUP
TOP_K = 2

LANES = 128
MXU_WIDTH = 256
VMEM_LIMIT = 56 * 1024 * 1024

INPROJ_TILE = 512
PROJ_TILE = 512
HGRN_CHUNK = 128
HGRN_CHUNKS_PER_STEP = 8
ATTN_CHUNKS_PER_STEP = 16
SAMPLE_STREAMS_PER_STEP = 4
EXPERT_TILE = 512


LOG2E = math.log2(math.e)


def _sigmoid(x):
    return 1.0 / (1.0 + jnp.exp(-x))


def _split3(x):
    hi = x.astype(BF16)
    r1 = x - hi.astype(F32)
    mid = r1.astype(BF16)
    lo = (r1 - mid.astype(F32)).astype(BF16)
    return hi, mid, lo


def _dot(a, b):
    return jnp.dot(a, b, preferred_element_type=F32)


def _dot_nt(a, b):
    return lax.dot_general(a, b, (((1,), (1,)), ((), ())), preferred_element_type=F32)


def _dot_tn(a, b):
    return lax.dot_general(a, b, (((0,), (0,)), ((), ())), preferred_element_type=F32)


SUBLANES = 8


def _store_row_tiles(ref, x):
    n, d = x.shape
    sub = d // LANES
    for j in range(sub):
        ref[pl.ds(j, n, stride=sub), :] = x[:, j * LANES:(j + 1) * LANES]


def _load_row_tiles(ref, n, sub):
    return jnp.concatenate([ref[pl.ds(j, n, stride=sub), :] for j in range(sub)], axis=1)


def _params(*sem):
    return pltpu.CompilerParams(dimension_semantics=sem, vmem_limit_bytes=VMEM_LIMIT)


def _inproj_kernel(hw, aw, kvw, d, x_ref, wn_ref, w_ref, qg_ref, kg_ref, bdq_ref, bdk_ref,
                   qr_ref, z_ref, hv_ref, hg_ref, qa_ref, k_ref, v_ref, sga_ref, sgb_ref):
    x = x_ref[...]
    ms = jnp.mean(x * x, axis=-1, keepdims=True)
    xn = (x * lax.rsqrt(ms + EPS) * wn_ref[...]).astype(BF16)

    def seg(a, b):
        return _dot(xn, w_ref[:, a:b])

    def head_rms(a, bd_ref, gain):
        sq = (a * a).astype(BF16)
        wb = bd_ref.shape[0]
        m = jnp.concatenate([_dot(sq[:, c:c + wb], bd_ref[...]) for c in range(0, a.shape[1], wb)], axis=1)
        return a * lax.rsqrt(m + EPS) * gain

    o = 0
    hq = seg(o, o + hw)
    qr_ref[...] = (hq * _sigmoid(hq) * (hw // HGRN_HEADS) ** -0.5).astype(BF16)
    o += hw
    z_ref[...] = seg(o, o + hw)
    o += hw
    hv_ref[...] = seg(o, o + hw).astype(BF16)
    o += hw
    hg_ref[...] = seg(o, o + hw).astype(BF16)
    o += hw
    aq = seg(o, o + aw)
    qa_ref[...] = (head_rms(aq, bdq_ref, qg_ref[...]) * (HEAD_DIM ** -0.5 * LOG2E)).astype(BF16)
    o += aw
    k_ref[...] = head_rms(seg(o, o + kvw), bdk_ref, kg_ref[...])
    o += kvw
    v_ref[...] = seg(o, o + kvw)
    o += kvw
    sga_ref[...] = _sigmoid(seg(o, o + d)).astype(BF16)
    o += d
    sgb_ref[...] = _sigmoid(seg(o, o + d)).astype(BF16)


def _block_diag_mean(width, group):
    i = np.arange(width)
    return jnp.asarray((i[:, None] // group == i[None, :] // group) / group, dtype=BF16)


def _inproj(x, w_norm, w_in_bf16, q_gain, k_gain, hw, aw, kvw):
    t, d = x.shape
    tm = INPROJ_TILE if t % INPROJ_TILE == 0 else t
    cols = w_in_bf16.shape[1]
    row = lambda w: pl.BlockSpec((tm, w), lambda i: (i, 0))
    const = lambda a, b: pl.BlockSpec((a, b), lambda i: (0, 0))
    outs = [(hw, BF16), (hw, F32), (hw, BF16), (hw, BF16), (aw, BF16), (kvw, F32), (kvw, F32), (d, BF16), (d, BF16)]
    bdq = min(aw, MXU_WIDTH)
    assert aw % bdq == 0
    return pl.pallas_call(
        functools.partial(_inproj_kernel, hw, aw, kvw, d),
        grid=(t // tm,),
        in_specs=[row(d), const(1, d), const(d, cols), const(1, aw), const(1, kvw), const(bdq, bdq), const(kvw, kvw)],
        out_specs=[row(w) for w, _ in outs],
        out_shape=[jax.ShapeDtypeStruct((t, w), dt) for w, dt in outs],
        compiler_params=_params("arbitrary"),
        name="inproj",
    )(x, w_norm.reshape(1, d), w_in_bf16,
      jnp.tile(q_gain, aw // HEAD_DIM).reshape(1, aw), jnp.tile(k_gain, kvw // HEAD_DIM).reshape(1, kvw),
      _block_diag_mean(bdq, HEAD_DIM), _block_diag_mean(kvw, HEAD_DIM))


def _hgrn_consts(L):
    t = np.arange(L)
    u = t[None, :]
    blocks = [u <= t[:, None], u > t[:, None]]
    levels = []
    m = L // 2
    while m >= 1:
        levels.append(m)
        m //= 2
    lvl = np.full((L, L), -1, np.int32)
    lvl[t, t] = len(levels)
    isq_cols = []
    for j, m in enumerate(levels):
        bnd = (t // (2 * m)) * (2 * m) + m - 1
        isq = (t % (2 * m)) >= m
        cq = isq[:, None] & (u > bnd[:, None]) & (u <= t[:, None])
        ck = (~isq)[:, None] & (u > t[:, None]) & (u <= bnd[:, None])
        blocks.append(cq | ck)
        same = (t[:, None] // (2 * m)) == (t[None, :] // (2 * m))
        lvl[same & isq[:, None] & (~isq)[None, :]] = j
        isq_cols.append(isq)
    c = np.concatenate(blocks, axis=0).astype(np.float32)
    isq = np.stack(isq_cols, axis=1).astype(np.float32)
    isq = np.pad(isq, ((0, 0), (0, LANES - isq.shape[1])))
    c2 = np.tile(c, (1, 2))
    return jnp.asarray(c2, dtype=BF16), jnp.asarray(np.tile(lvl, (1, 2))), jnp.asarray(isq), len(levels)


def _hgrn_kernel(L, nlev, heads, dk, qr_ref, z_ref, hv_ref, hg_ref, lb_ref, og_ref, c_ref, lvl_ref, isq_ref,
                 s0_ref, y_ref, sout_ref, st_ref):
    c = pl.program_id(1)

    @pl.when(c == 0)
    def _():
        for h in range(heads):
            st_ref[h] = s0_ref[0, h].T

    for cc in range(z_ref.shape[1] // L):
        _hgrn_chunk(L, nlev, heads, dk, slice(cc * L, (cc + 1) * L), qr_ref, z_ref, hv_ref, hg_ref, lb_ref, og_ref,
                    c_ref, lvl_ref, isq_ref, y_ref, st_ref)

    @pl.when(c == pl.num_programs(1) - 1)
    def _():
        for h in range(heads):
            sout_ref[0, h] = st_ref[h].T


def _hgrn_chunk(L, nlev, heads, dk, rows, qr_ref, z_ref, hv_ref, hg_ref, lb_ref, og_ref, c_ref, lvl_ref, isq_ref,
                y_ref, st_ref):
    z = z_ref[0, rows, :]
    lb = lb_ref[...]
    e = jnp.exp(-jnp.abs(z))
    r = 1.0 / (1.0 + e)
    pos = z >= 0
    sig = jnp.where(pos, r, e * r)
    sig_neg = jnp.where(pos, e * r, r)
    logf = jnp.log(lb + (1.0 - lb) * sig)
    kin = (1.0 - lb) * sig_neg
    q = qr_ref[0, rows, :].astype(F32)

    hi, mid, _ = _split3(logf * LOG2E)
    ex = jnp.exp2(_dot(c_ref[...], jnp.concatenate([hi, mid], axis=0)))
    e_b = ex[0:L]
    e_rev = ex[L:2 * L]

    q_in = (q * e_b).astype(BF16)
    k_out = (kin * e_rev).astype(BF16)
    q_b = q.astype(BF16)
    k_b = kin.astype(BF16)
    xs = []
    for j in range(nlev):
        m = L >> (j + 1)
        if m % SUBLANES == 0:
            qk = jnp.concatenate([(q if blk % 2 else kin)[blk * m:(blk + 1) * m] for blk in range(L // m)], axis=0)
        else:
            qk = jnp.where(isq_ref[:, j:j + 1] > 0.5, q, kin)
        xs.append((qk * ex[(2 + j) * L:(3 + j) * L]).astype(BF16))
    lvl = lvl_ref[...]
    v = hv_ref[0, rows, :]
    g = hg_ref[0, rows, :].astype(F32)
    og = og_ref[...]

    def block_diag(x):
        zero = jnp.zeros((x.shape[0], dk), x.dtype)
        return jnp.concatenate([jnp.concatenate([x[:, :dk], zero], axis=1),
                                jnp.concatenate([zero, x[:, dk:]], axis=1)], axis=0)

    for pair in range(heads // 2):
        sl = slice(2 * pair * dk, 2 * (pair + 1) * dk)
        a = jnp.where(lvl == nlev, _dot_nt(q_b[:, sl], block_diag(k_b[:, sl])), 0.0)
        for j in range(nlev):
            xp = xs[j][:, sl]
            a = jnp.where(lvl == j, _dot_nt(xp, block_diag(xp)), a)
        st = jnp.concatenate([st_ref[2 * pair], st_ref[2 * pair + 1]], axis=1)
        vp = v[:, sl]
        o = _dot(a.astype(BF16), block_diag(vp)) + _dot_nt(q_in[:, sl], block_diag(st.astype(BF16)))
        for half in range(2):
            h = 2 * pair + half
            hs = slice(h * dk, (h + 1) * dk)
            st_ref[h] = st_ref[h] * e_b[L - 1:L, hs] + _dot_tn(v[:, hs], k_out[:, hs])
            oh = o[:, half * dk:(half + 1) * dk]
            ms = jnp.mean(oh * oh, axis=-1, keepdims=True)
            gh = g[:, hs]
            y_ref[0, rows, hs] = (oh * lax.rsqrt(ms + EPS) * og[:, hs] * (gh * _sigmoid(gh))).astype(BF16)


def _hgrn(qr, z, hv, hg, lb, out_gain, s0, L, shared_s0=None):
    b, s, w = z.shape
    heads, dk = s0.shape[1], s0.shape[2]
    cm, lvl, isq, nlev = _hgrn_consts(L)
    per_step = HGRN_CHUNKS_PER_STEP if s % (HGRN_CHUNKS_PER_STEP * L) == 0 else 1
    seq = pl.BlockSpec((1, per_step * L, w), lambda i, c: (i, c, 0))
    const = lambda a: pl.BlockSpec(a.shape, lambda i, c: (0,) * a.ndim)
    state = pl.BlockSpec((1, heads, dk, dk), lambda i, c: (i, 0, 0, 0))
    start = state if shared_s0 is None else pl.BlockSpec((1, heads, dk, dk), lambda i, c: (shared_s0, 0, 0, 0))
    lb2 = lb.reshape(1, w)
    og2 = jnp.tile(out_gain, heads).reshape(1, w)
    return pl.pallas_call(
        functools.partial(_hgrn_kernel, L, nlev, heads, dk),
        grid=(b, s // (per_step * L)),
        in_specs=[seq, seq, seq, seq, const(lb2), const(og2), const(cm), const(lvl), const(isq), start],
        out_specs=[seq, state],
        out_shape=[jax.ShapeDtypeStruct((b, s, w), BF16), jax.ShapeDtypeStruct((b,) + s0.shape[1:], F32)],
        scratch_shapes=[pltpu.VMEM((heads, dk, dk), F32)],
        compiler_params=_params("arbitrary", "arbitrary"),
        name=f"hgrn_scan_{L}",
    )(qr, z, hv, hg, lb2, og2, cm, lvl, isq, s0)


def _t5_bucket_np(rel):
    half = NUM_BUCKETS // 2
    max_exact = half // 2
    assert (NUM_BUCKETS, MAX_DISTANCE) == (32, 128)
    n = np.abs(rel).astype(np.int64)
    nn = np.maximum(n, 1)
    k = np.zeros_like(nn)
    for j in range(1, 48):
        k = np.where(64 * (1 << j) <= nn * nn, j, k)
    large = np.minimum(max_exact + k, half - 1)
    return np.where(rel > 0, half, 0) + np.where(n < max_exact, n, large)


HEADS_PER_COL = LANES // HEAD_DIM
COLS_PER_GROUP = GQA_GROUP // HEADS_PER_COL
HEAD_ORDER = tuple(g * GQA_GROUP + col * HEADS_PER_COL + half
                   for g in range(KV_HEADS) for half in range(HEADS_PER_COL) for col in range(COLS_PER_GROUP))
KV_EXPAND = KV_HEADS * HEADS_PER_COL


def _expand_kv(x, with_ones=False):
    assert HEADS_PER_COL == 2 and KV_HEADS == 2 and x.shape[1] == LANES
    low = lax.broadcasted_iota(I32, x.shape, 1) < HEAD_DIM
    xr = pltpu.roll(x, HEAD_DIM, axis=1)
    zero = jnp.zeros_like(x)
    blocks = [jnp.where(low, x, zero), jnp.where(low, zero, xr), jnp.where(low, xr, zero), jnp.where(low, zero, x)]
    if with_ones:
        blocks = [b for blk in blocks for b in (blk, jnp.ones_like(x))]
    return jnp.concatenate(blocks, axis=1).astype(BF16)


def _attn_core(q, kx, vx, bias):
    tq = q.shape[0]
    scores = []
    for g in range(KV_HEADS):
        cols = [q[:, (g * COLS_PER_GROUP + c) * LANES:(g * COLS_PER_GROUP + c + 1) * LANES]
                for c in range(COLS_PER_GROUP)]
        qst = jnp.concatenate(cols, axis=0)
        for half in range(HEADS_PER_COL):
            blk = g * HEADS_PER_COL + half
            scores.append(_dot_nt(qst, kx[:, blk * LANES:(blk + 1) * LANES]))
    s = jnp.concatenate(scores, axis=0) + bias
    pb = jnp.exp2(s - jnp.max(s, axis=-1, keepdims=True)).astype(BF16)
    rows = COLS_PER_GROUP * tq
    outs = []
    for g in range(KV_HEADS):
        o = None
        for half in range(HEADS_PER_COL):
            blk = g * HEADS_PER_COL + half
            pv = _dot(pb[blk * rows:(blk + 1) * rows], vx[:, 2 * blk * LANES:2 * (blk + 1) * LANES])
            part = pv[:, :LANES] * (1.0 / pv[:, LANES:])
            o = part if o is None else o + part
        outs.extend(o[c * tq:(c + 1) * tq] for c in range(COLS_PER_GROUP))
    return outs


def _bias_rows(table, bucket):
    onehot = (jnp.asarray(bucket)[..., None] == jnp.arange(NUM_BUCKETS)).astype(F32)
    cols = jnp.stack([table[:, h] for h in HEAD_ORDER], axis=1)
    bias = jnp.einsum('...qkb,bh->...hqk', onehot, cols, precision=lax.Precision.HIGHEST)
    return bias.reshape(*bucket.shape[:-2], ATTN_HEADS * bucket.shape[-2], bucket.shape[-1]) * LOG2E


def _pad_keys(bias, sinks):
    tk = bias.shape[-1]
    n_pad = -tk % LANES or LANES
    tq = bias.shape[-2] // ATTN_HEADS
    sink = jnp.repeat(jnp.stack([sinks[h] for h in HEAD_ORDER]).astype(F32), tq) * LOG2E
    sink = jnp.broadcast_to(sink[:, None], bias.shape[:-1] + (1,))
    masked = jnp.full(bias.shape[:-1] + (n_pad - 1,), -jnp.inf, F32)
    return jnp.concatenate([bias, sink, masked], axis=-1)


def _store_heads(ref, x, stream=0):
    for g in range(KV_HEADS):
        ref[0, stream, :, g, :] = x[:, g * HEAD_DIM:(g + 1) * HEAD_DIM]


def _load_heads(ref, stream=0):
    return jnp.concatenate([ref[0, stream, :, g, :] for g in range(KV_HEADS)], axis=1)


def _attn_prompt_kernel(cb, q_ref, k_ref, v_ref, km_ref, vm_ref, bias_ref, o_ref, nk_ref, nv_ref,
                        kx_ref, vx_ref):
    step = pl.program_id(1)
    s_len = k_ref.shape[1]
    meta_at = WINDOW + s_len

    @pl.when(step == pl.num_programs(1) - 1)
    def _():
        _store_heads(nk_ref, k_ref[0, s_len - WINDOW:, :])
        _store_heads(nv_ref, v_ref[0, s_len - WINDOW:, :])

    @pl.when(step == 0)
    def _():
        piece = min(s_len, 512)
        for src, meta, dst in ((k_ref, km_ref, kx_ref), (v_ref, vm_ref, vx_ref)):
            expand = functools.partial(_expand_kv, with_ones=dst is vx_ref)
            blank = lambda n: expand(jnp.zeros((n, src.shape[2]), F32))
            dst[0:WINDOW] = blank(WINDOW)
            for r in range(0, s_len, piece):
                dst[WINDOW + r:WINDOW + r + piece] = expand(src[0, r:r + piece, :])
            dst[meta_at:meta_at + N_META] = expand(meta[...])
            dst[meta_at + N_META:] = blank(dst.shape[0] - meta_at - N_META)

    win = WINDOW + CHUNK
    tail = kx_ref.shape[0] - meta_at
    for j in range(cb):
        c = step * cb + j
        start = pl.multiple_of(c * CHUNK, CHUNK)
        kall = jnp.concatenate([kx_ref[pl.ds(start, win), :], kx_ref[meta_at:meta_at + tail, :]], axis=0)
        vall = jnp.concatenate([vx_ref[pl.ds(start, win), :], vx_ref[meta_at:meta_at + tail, :]], axis=0)
        rows = slice(j * CHUNK, (j + 1) * CHUNK)
        outs = _attn_core(q_ref[0, rows, :], kall, vall, bias_ref[jnp.minimum(c, bias_ref.shape[0] - 1)])
        for ci, o in enumerate(outs):
            o_ref[0, rows, ci * LANES:(ci + 1) * LANES] = o.astype(o_ref.dtype)


def _attn_prompt(q, k, v, k_meta, v_meta, table, sinks):
    b, s, aw = q.shape
    kvw = k.shape[-1]
    nc = s // CHUNK
    cb = ATTN_CHUNKS_PER_STEP if nc % ATTN_CHUNKS_PER_STEP == 0 else 1
    assert s % min(s, 512) == 0
    n_bias = 1
    while True:
        qpos = N_META + (n_bias - 1) * CHUNK
        if np.all(_t5_bucket_np(np.arange(N_META) - qpos) == _t5_bucket_np(np.arange(N_META) - qpos - 10 ** 6)):
            break
        n_bias += 1
    n_bias = min(max(n_bias, WINDOW_CHUNKS + 1), nc)
    cs = np.arange(n_bias)[:, None]
    qpos = N_META + cs * CHUNK + np.arange(CHUNK)[None]
    wpos = N_META + (cs - WINDOW_CHUNKS) * CHUNK + np.arange(WINDOW + CHUNK)[None]
    kpos = np.concatenate([wpos, np.broadcast_to(np.arange(N_META), (n_bias, N_META))], axis=1)
    valid = np.concatenate([wpos >= N_META, np.ones((n_bias, N_META), bool)], axis=1)
    bias = _bias_rows(table, _t5_bucket_np(kpos[:, None, :] - qpos[:, :, None]))
    bias = _pad_keys(jnp.where(valid[:, None, :], bias, -jnp.inf), sinks)
    qs =pl.BlockSpec((1, cb * CHUNK, aw), lambda i, c: (i, c, 0))
    kv = pl.BlockSpec((1, s, kvw), lambda i, c: (i, 0, 0))
    meta = pl.BlockSpec((N_META, kvw), lambda i, c: (0, 0))
    xrows = WINDOW + s + bias.shape[-1] - (WINDOW + CHUNK)
    cache_shape = (1, b, WINDOW, KV_HEADS, HEAD_DIM)
    cache = pl.BlockSpec((1, 1) + cache_shape[2:], lambda i, c: (0, i, 0, 0, 0))
    return pl.pallas_call(
        functools.partial(_attn_prompt_kernel, cb),
        grid=(b, nc // cb),
        in_specs=[qs, kv, kv, meta, meta, pl.BlockSpec(bias.shape, lambda i, c: (0, 0, 0))],
        out_specs=[qs, cache, cache],
        out_shape=[jax.ShapeDtypeStruct((b, s, aw), BF16)] + [jax.ShapeDtypeStruct(cache_shape, F32)] * 2,
        scratch_shapes=[pltpu.VMEM((xrows, KV_EXPAND * LANES), BF16),
                        pltpu.VMEM((xrows, 2 * KV_EXPAND * LANES), BF16)],
        compiler_params=_params("arbitrary", "arbitrary"),
        name="attn_prompt",
    )(q, k, v, k_meta, v_meta, bias)


def _attn_sample_kernel(q_ref, kc_ref, vc_ref, kn_ref, vn_ref, km_ref, vm_ref, bias_ref, o_ref, nk_ref, nv_ref):
    tk = kc_ref.shape[2] + kn_ref.shape[1] + km_ref.shape[0]
    zeros = jnp.zeros((bias_ref.shape[1] - tk, km_ref.shape[1]), F32)
    for s in range(q_ref.shape[0]):
        kc, vc, kn, vn = _load_heads(kc_ref, s), _load_heads(vc_ref, s), kn_ref[s], vn_ref[s]
        kall = _expand_kv(jnp.concatenate([kc, kn, km_ref[...], zeros], axis=0))
        vall = _expand_kv(jnp.concatenate([vc, vn, vm_ref[...], zeros], axis=0), with_ones=True)
        outs = _attn_core(q_ref[s], kall, vall, bias_ref[...])
        for ci, o in enumerate(outs):
            o_ref[s, :, ci * LANES:(ci + 1) * LANES] = o.astype(o_ref.dtype)
        n_new = kn.shape[0]
        _store_heads(nk_ref, jnp.concatenate([kc[n_new:], kn], axis=0), s)
        _store_heads(nv_ref, jnp.concatenate([vc[n_new:], vn], axis=0), s)


def _attn_sample(q, k_cache, v_cache, k_new, v_new, k_meta, v_meta, table, sinks):
    bd, sd, aw = q.shape
    kvw = k_new.shape[-1]
    win = k_cache.shape[2]
    sb = SAMPLE_STREAMS_PER_STEP if bd % SAMPLE_STREAMS_PER_STEP == 0 else 1
    cache = pl.BlockSpec((1, sb) + k_cache.shape[2:], lambda i: (0, i, 0, 0, 0))
    qpos = N_META + PAST_LEN + np.arange(sd)
    kpos = np.concatenate([N_META + PAST_LEN - win + np.arange(win), qpos, np.arange(N_META)])
    bias = _pad_keys(_bias_rows(table, _t5_bucket_np(kpos[None, :] - qpos[:, None])), sinks)
    per = lambda n, w: pl.BlockSpec((sb, n, w), lambda i: (i, 0, 0))
    meta = pl.BlockSpec((N_META, kvw), lambda i: (0, 0))
    return pl.pallas_call(
        _attn_sample_kernel,
        grid=(bd // sb,),
        in_specs=[per(sd, aw), cache, cache, per(sd, kvw), per(sd, kvw), meta, meta,
                  pl.BlockSpec(bias.shape, lambda i: (0, 0))],
        out_specs=[per(sd, aw), cache, cache],
        out_shape=[jax.ShapeDtypeStruct((bd, sd, aw), BF16)] + [jax.ShapeDtypeStruct(k_cache.shape, F32)] * 2,
        compiler_params=_params("arbitrary"),
        name="attn_sample",
    )(q, k_cache, v_cache, k_new, v_new, k_meta, v_meta, bias)


ROUTE_E1, ROUTE_E2, ROUTE_R1, ROUTE_R2, ROUTE_W1, ROUTE_W2 = range(6)
META_FIELDS = 2 * TOP_K


def _merge_kernel(hw, x_ref, yr_ref, at_ref, sga_ref, sgb_ref, wb_ref, wo_ref, wn_ref, wr_ref, br_ref, cnt0_ref, tri_ref,
                  h_ref, xn_ref, route_ref, meta_ref, cnt_ref, carry_ref):
    @pl.when(pl.program_id(0) == 0)
    def _():
        carry_ref[...] = cnt0_ref[...]

    logits = _project_rows(hw, x_ref, yr_ref, at_ref, sga_ref, sgb_ref, wb_ref, wo_ref, wn_ref, wr_ref, br_ref,
                           h_ref, xn_ref)
    _route_rows(logits, tri_ref, route_ref, meta_ref, carry_ref)
    cnt_ref[...] = carry_ref[...]


def _project_rows(hw, x_ref, yr_ref, at_ref, sga_ref, sgb_ref, wb_ref, wo_ref, wn_ref, wr_ref, br_ref, h_ref, xn_ref):
    br = _dot(yr_ref[...], wb_ref[0:hw, :])
    ba = _dot(at_ref[...], wb_ref[hw:, :])
    merged = sga_ref[...].astype(F32) * br + sgb_ref[...].astype(F32) * ba
    h = x_ref[...] + _dot(merged.astype(BF16), wo_ref[...])
    h_ref[...] = h
    ms = jnp.mean(h * h, axis=-1, keepdims=True)
    xn = h * lax.rsqrt(ms + EPS) * wn_ref[...]
    _store_row_tiles(xn_ref, xn)

    x_hi = xn.astype(BF16)
    x_lo = (xn - x_hi.astype(F32)).astype(BF16)
    both = _dot(x_hi, wr_ref[...])
    return both[:, :LANES] + both[:, LANES:] + _dot(x_lo, wr_ref[:, :LANES]) + br_ref[...]


def _route_rows(logits, tri_ref, route_ref, meta_ref, carry_ref):
    tm = logits.shape[0]
    lane_i = lax.broadcasted_iota(I32, (tm, LANES), 1)
    lane = lane_i.astype(F32)
    group_of_lane = (lane_i >> int(math.log2(EXPERTS_PER_GROUP))).astype(F32)
    ninf = -jnp.inf
    first = lambda hit, idx: jnp.min(jnp.where(hit, idx, float(LANES)), axis=-1, keepdims=True)
    gmask = (lane_i >= N_EXPERTS) & (lane_i < N_EXPERTS + N_GROUPS)
    gl = jnp.where(gmask, logits, ninf)
    gmax = jnp.max(gl, axis=-1, keepdims=True)
    gidx = first(gl == gmax, lane - N_EXPERTS)
    gval = 1.0 / jnp.sum(jnp.exp(gl - gmax), axis=-1, keepdims=True)
    emask = (lane_i < N_EXPERTS) & (group_of_lane == gidx)
    el = jnp.where(emask, logits, ninf)
    m1 = jnp.max(el, axis=-1, keepdims=True)
    i1 = first(el == m1, lane)
    el2 = jnp.where(lane == i1, ninf, el)
    m2 = jnp.max(el2, axis=-1, keepdims=True)
    i2 = first(el2 == m2, lane)
    e21 = jnp.exp(m2 - m1)
    w1 = gval / (1.0 + e21)
    w2 = gval * e21 / (1.0 + e21)

    sel1 = lane == i1
    sel2 = lane == i2
    oh = (sel1 | sel2).astype(BF16)
    before = _dot(tri_ref[...], oh) + carry_ref[...]
    r1 = jnp.sum(jnp.where(sel1, before, 0.0), axis=-1, keepdims=True)
    r2 = jnp.sum(jnp.where(sel2, before, 0.0), axis=-1, keepdims=True)
    carry_ref[...] = carry_ref[...] + jnp.sum(oh.astype(F32), axis=0, keepdims=True)

    rec = jnp.zeros((tm, LANES), F32)
    for slot, val in ((ROUTE_E1, i1), (ROUTE_E2, i2), (ROUTE_W1, w1), (ROUTE_W2, w2), (ROUTE_R1, r1), (ROUTE_R2, r2)):
        rec = jnp.where(lane_i == slot, val, rec)
    route_ref[...] = rec
    meta_ref[0] = rec.T[ROUTE_E1:ROUTE_E1 + META_FIELDS].astype(I32)


def _merge(x, y_rec, att, sga, sgb, wb, wo, w_norm, w_router, b_router, cnt0):
    t, d = x.shape
    hw = y_rec.shape[1]
    tm = PROJ_TILE if t % PROJ_TILE == 0 else t
    row = lambda w: pl.BlockSpec((tm, w), lambda i: (i, 0))
    const = lambda a, b: pl.BlockSpec((a, b), lambda i: (0, 0))
    return pl.pallas_call(
        functools.partial(_merge_kernel, hw),
        grid=(t // tm,),
        in_specs=[row(d), row(hw), row(att.shape[1]), row(d), row(d), const(*wb.shape), const(d, d), const(1, d),
                  const(d, 2 * LANES), const(1, LANES), const(1, LANES), const(tm, tm)],
        out_specs=[row(d), pl.BlockSpec((tm * d // LANES, LANES), lambda i: (i, 0)), row(LANES),
                   pl.BlockSpec((1, META_FIELDS, tm), lambda i: (i, 0, 0)), const(1, LANES)],
        out_shape=[jax.ShapeDtypeStruct((t, d), F32), jax.ShapeDtypeStruct((t * d // LANES, LANES), F32),
                   jax.ShapeDtypeStruct((t, LANES), F32), jax.ShapeDtypeStruct((t // tm, META_FIELDS, tm), I32),
                   jax.ShapeDtypeStruct((1, LANES), F32)],
        scratch_shapes=[pltpu.VMEM((1, LANES), F32)],
        compiler_params=_params("arbitrary"),
        name="merge_route",
    )(x, y_rec, att, sga, sgb, wb, wo, w_norm.reshape(1, d), w_router, b_router, cnt0,
      jnp.asarray(np.tri(tm, k=-1), dtype=BF16))


def _row_copy(src, dst, sem):
    return pltpu.make_async_copy(src, dst, sem)


ROW_UNROLL = 8


def _tile_rows(ref, row, sub):
    return ref.at[pl.ds(pl.multiple_of(row * sub, sub), sub)]


def _scatter_kernel(n_tok, sub, first, slot_ref, pstart_ref, pend_ref, x_ref, *rest):
    xs_ref, zero_ref, sem, zsem, tsem = rest[-5:]

    def zero_blocks(blocks, zs, wait):
        for cond, row in blocks:
            @pl.when(cond)
            def _():
                at = row * sub if isinstance(row, int) else pl.multiple_of(row * sub, EXPERT_TILE * sub)
                cp = _row_copy(zero_ref, xs_ref.at[pl.ds(at, EXPERT_TILE * sub)], zs)
                cp.wait() if wait else cp.start()

    if first:
        n_rows = xs_ref.shape[0] // sub
        tails = [(pend_ref[e] > pstart_ref[e], pend_ref[e] - EXPERT_TILE) for e in range(N_EXPERTS)]
        unused = [(n_rows - (j + 1) * EXPERT_TILE >= pend_ref[N_EXPERTS - 1], n_rows - (j + 1) * EXPERT_TILE)
                  for j in range(N_EXPERTS)]

        @pl.when(pl.program_id(0) == 0)
        def _():
            zero_ref[...] = jnp.zeros_like(zero_ref)
            zero_blocks(tails, zsem, False)
            zero_blocks(unused, tsem, False)
            zero_blocks(tails, zsem, True)

    def issue(grp, _):
        for u in range(ROW_UNROLL):
            r = grp * ROW_UNROLL + u
            for k in range(TOP_K):
                _row_copy(_tile_rows(x_ref, r, sub), _tile_rows(xs_ref, slot_ref[0, k, r], sub),
                          sem).start(priority=(u * TOP_K + k) % 2)
        return 0

    lax.fori_loop(0, n_tok // ROW_UNROLL, issue, 0)
    for k in range(TOP_K):
        _row_copy(x_ref, xs_ref.at[pl.ds(0, n_tok * sub)], sem).wait()

    if first:
        @pl.when(pl.program_id(0) == 0)
        def _():
            zero_blocks(unused, tsem, True)


def _scatter_rows(x, slots, pstarts, pends, xs, rows):
    n, fields, tm = slots.shape
    sub = x.shape[0] // (n * tm)
    first = xs is None
    smem = pl.BlockSpec(memory_space=pltpu.SMEM)
    return pl.pallas_call(
        functools.partial(_scatter_kernel, tm, sub, first),
        grid=(n,),
        in_specs=[pl.BlockSpec((1, fields, tm), lambda i: (i, 0, 0), memory_space=pltpu.SMEM), smem, smem,
                  pl.BlockSpec((tm * sub, LANES), lambda i: (i, 0))]
        + ([] if first else [pl.BlockSpec(memory_space=pl.ANY)]),
        out_specs=pl.BlockSpec(memory_space=pl.ANY),
        out_shape=jax.ShapeDtypeStruct((rows * sub, LANES), F32),
        scratch_shapes=[pltpu.VMEM((EXPERT_TILE * sub, LANES), F32)] + [pltpu.SemaphoreType.DMA(())] * 3,
        input_output_aliases={} if first else {4: 0},
        compiler_params=_params("arbitrary"),
        name="moe_scatter",
    )(slots, pstarts, pends, x, *([] if first else [xs]))


def _expert_kernel(tm, sub, be_ref, nb_ref, x_ref, wg_ref, wu_ref, wd_ref, y_ref, wgb_ref, wub_ref, wdb_ref):
    i = pl.program_id(0)
    used = i < nb_ref[0]

    @pl.when(used & ((i == 0) | (be_ref[i] != be_ref[jnp.maximum(i - 1, 0)])))
    def _():
        wgb_ref[...] = wg_ref[0].astype(BF16)
        wub_ref[...] = wu_ref[0].astype(BF16)
        wdb_ref[...] = wd_ref[0].astype(BF16)

    @pl.when(used)
    def _():
        x = _load_row_tiles(x_ref, tm, sub).astype(BF16)
        g = _dot(x, wgb_ref[...])
        u = _dot(x, wub_ref[...])
        hmid = (g * _sigmoid(g) * u).astype(BF16)
        _store_row_tiles(y_ref, _dot(hmid, wdb_ref[...]))

    @pl.when(jnp.logical_not(used))
    def _():
        y_ref[...] = jnp.zeros_like(y_ref)


def _experts(xs, block_e, n_used, wg, wu, wd):
    _, d, ff = wg.shape
    sub = d // LANES
    tm = EXPERT_TILE
    blk = pl.BlockSpec((tm * sub, LANES), lambda i, be, nb: (i, 0))
    grid_spec = pltpu.PrefetchScalarGridSpec(
        num_scalar_prefetch=2,
        grid=(xs.shape[0] // (tm * sub),),
        in_specs=[blk,
                  pl.BlockSpec((1, d, ff), lambda i, be, nb: (be[i], 0, 0)),
                  pl.BlockSpec((1, d, ff), lambda i, be, nb: (be[i], 0, 0)),
                  pl.BlockSpec((1, ff, d), lambda i, be, nb: (be[i], 0, 0))],
        out_specs=blk,
        scratch_shapes=[pltpu.VMEM((d, ff), BF16), pltpu.VMEM((d, ff), BF16), pltpu.VMEM((ff, d), BF16)],
    )
    return pl.pallas_call(
        functools.partial(_expert_kernel, tm, sub),
        grid_spec=grid_spec,
        out_shape=jax.ShapeDtypeStruct(xs.shape, F32),
        compiler_params=_params("arbitrary"),
        name="moe_experts",
    )(block_e, n_used, xs, wg, wu, wd)


def _combine_kernel(n_tok, sub, scur_ref, snext_ref, h_ref, route_ref, ys_ref, o_ref, buf, sem):
    i = pl.program_id(0)
    n = pl.num_programs(0)
    slot = i % 2

    def start(slot_ref, s):
        def issue(grp, _):
            for u in range(ROW_UNROLL):
                r = grp * ROW_UNROLL + u
                for k in range(TOP_K):
                    _row_copy(_tile_rows(ys_ref, slot_ref[0, k, r], sub), _tile_rows(buf.at[s, k], r, sub),
                              sem.at[s]).start(priority=(u * TOP_K + k) % 2)
            return 0
        lax.fori_loop(0, n_tok // ROW_UNROLL, issue, 0)

    @pl.when(i == 0)
    def _():
        start(scur_ref, 0)

    @pl.when(i + 1 < n)
    def _():
        start(snext_ref, 1 - slot)

    for k in range(TOP_K):
        _row_copy(ys_ref.at[pl.ds(0, n_tok * sub)], buf.at[slot, k], sem.at[slot]).wait()
    route = route_ref[...]
    w1 = route[:, ROUTE_W1:ROUTE_W1 + 1]
    w2 = route[:, ROUTE_W2:ROUTE_W2 + 1]
    for j in range(sub):
        cols = slice(j * LANES, (j + 1) * LANES)
        part = lambda k: buf[slot, k, pl.ds(j, n_tok, stride=sub), :]
        o_ref[:, cols] = h_ref[:, cols] + (part(0) * w1 + part(1) * w2)


def _combine(h, route, slots, ys):
    t, d = h.shape
    n, fields, tm = slots.shape
    sub = d // LANES
    mspec = lambda f: pl.BlockSpec((1, fields, tm), f, memory_space=pltpu.SMEM)
    return pl.pallas_call(
        functools.partial(_combine_kernel, tm, sub),
        grid=(n,),
        in_specs=[mspec(lambda i: (i, 0, 0)), mspec(lambda i: (jnp.minimum(i + 1, n - 1), 0, 0)),
                  pl.BlockSpec((tm, d), lambda i: (i, 0)), pl.BlockSpec((tm, LANES), lambda i: (i, 0)),
                  pl.BlockSpec(memory_space=pl.ANY)],
        out_specs=pl.BlockSpec((tm, d), lambda i: (i, 0)),
        out_shape=jax.ShapeDtypeStruct((t, d), F32),
        scratch_shapes=[pltpu.VMEM((2, TOP_K, tm * sub, LANES), F32), pltpu.SemaphoreType.DMA((2,))],
        compiler_params=_params("arbitrary"),
        name="moe_combine",
    )(slots, slots, h, route, ys)


def _moe(parts, counts, wg, wu, wd):
    tm = EXPERT_TILE
    n_assign = sum(part[0].shape[0] for part in parts) * TOP_K
    n_blocks = -(-(n_assign + N_EXPERTS * (tm - 1)) // tm)
    counts = counts[0, :N_EXPERTS].astype(I32)
    pcounts = (counts + tm - 1) // tm * tm
    pends = jnp.cumsum(pcounts)
    pstarts = pends - pcounts
    block_start = jnp.arange(n_blocks, dtype=I32) * tm
    block_e = jnp.minimum(jnp.sum((pends[None, :] <= block_start[:, None]).astype(I32), axis=1), N_EXPERTS - 1)
    n_used = pends[-1:] // tm
    xs = None
    slots = []
    for _, xn, _, meta in parts:
        assert meta.shape[-1] % ROW_UNROLL == 0
        e, pos = meta[:, :TOP_K], meta[:, TOP_K:]
        seg = jnp.sum(jnp.where(e[..., None] == jnp.arange(N_EXPERTS, dtype=I32), pstarts, 0), axis=-1)
        slots.append(seg + pos)
        xs = _scatter_rows(xn, slots[-1], pstarts, pends, xs, n_blocks * tm)
    ys = _experts(xs, block_e, n_used, wg, wu, wd)
    return [_combine(h, route, s, ys) for (h, _, route, _), s in zip(parts, slots)]


def kernel(x_prompt, x_sample, cache_swa_k, cache_swa_v, state_hgrn, meta_tokens, rel_bias_table, hgrn_lower_bounds, w_norm_mix, w_in, hgrn_out_norm, q_norm, k_norm, attn_sinks, w_branch, w_out, w_norm_ffn, w_router_group, b_router_group, w_router_expert, b_router_expert, w_expert_gate, w_expert_up, w_expert_down):
    b, s, d = x_prompt.shape
    bd, sd, _ = x_sample.shape
    depth, _, heads, dk, dv = state_hgrn.shape
    assert depth == 1 and heads == HGRN_HEADS and dk == dv
    hw = heads * dk
    aw = ATTN_HEADS * HEAD_DIM
    kvw = KV_HEADS * HEAD_DIM
    assert w_in.shape[-1] == 4 * hw + aw + 2 * kvw + 2 * d
    assert s % HGRN_CHUNK == 0 and s % CHUNK == 0 and sd == N_META and N_EXPERTS + N_GROUPS <= LANES
    l = 0

    p = jax.nn.softmax(hgrn_lower_bounds.astype(F32), axis=0)
    lb = jnp.cumsum(p, axis=0)[l + 1] - p[0]

    w_in_b = w_in[l].astype(BF16)
    proj = functools.partial(_inproj, w_norm=w_norm_mix[l], w_in_bf16=w_in_b, q_gain=q_norm[l], k_gain=k_norm[l],
                             hw=hw, aw=aw, kvw=kvw)
    x_small = jnp.concatenate([x_sample.reshape(bd * sd, d), meta_tokens.astype(F32)], axis=0)
    qr_s, z_s, hv_s, hg_s, qa_s, k_s, v_s, sga_s, sgb_s = proj(x_small)
    qr_p, z_p, hv_p, hg_p, qa_p, k_p, v_p, sga_p, sgb_p = proj(x_prompt.reshape(b * s, d))
    ns = bd * sd
    k_meta, v_meta = k_s[ns:], v_s[ns:]

    streams = lambda a, n: a.reshape(n, -1, a.shape[-1])
    s0_small = jnp.concatenate([state_hgrn[l].astype(F32), jnp.zeros((1, heads, dk, dv), F32)], axis=0)
    y_small, st_small = _hgrn(streams(qr_s, bd + 1), streams(z_s, bd + 1), streams(hv_s, bd + 1),
                              streams(hg_s, bd + 1), lb, hgrn_out_norm[l], s0_small, sd)
    y_p, st_p = _hgrn(streams(qr_p, b), streams(z_p, b), streams(hv_p, b), streams(hg_p, b), lb,
                      hgrn_out_norm[l], st_small, HGRN_CHUNK, shared_s0=bd)

    table = rel_bias_table.astype(F32)
    att_p, new_k_p, new_v_p = _attn_prompt(streams(qa_p, b), streams(k_p, b), streams(v_p, b), k_meta, v_meta,
                                           table, attn_sinks[l])
    k_new, v_new = k_s[:ns].reshape(bd, sd, kvw), v_s[:ns].reshape(bd, sd, kvw)
    att_s, new_k_s, new_v_s = _attn_sample(qa_s[:ns].reshape(bd, sd, aw), cache_swa_k[l:l + 1].astype(F32),
                                           cache_swa_v[l:l + 1].astype(F32), k_new, v_new, k_meta, v_meta,
                                           table, attn_sinks[l])

    wb = w_branch[l].astype(BF16)
    wo = w_out[l].astype(BF16)
    w_router = jnp.pad(jnp.concatenate([w_router_expert[l], w_router_group[l]], axis=1).astype(F32),
                       ((0, 0), (0, LANES - N_EXPERTS - N_GROUPS)))
    b_router = jnp.pad(jnp.concatenate([b_router_expert[l], b_router_group[l]]).astype(F32),
                       (0, LANES - N_EXPERTS - N_GROUPS)).reshape(1, LANES)
    w_router_hi = w_router.astype(BF16)
    w_router_lo = (w_router - w_router_hi.astype(F32)).astype(BF16)
    w_router = jnp.concatenate([w_router_hi, w_router_lo], axis=1)
    merge = functools.partial(_merge, wb=wb, wo=wo, w_norm=w_norm_ffn[l], w_router=w_router, b_router=b_router)
    *part_p, cnt_p = merge(x_prompt.reshape(b * s, d), y_p.reshape(b * s, hw), att_p.reshape(b * s, aw),
                           sga_p, sgb_p, cnt0=jnp.zeros((1, LANES), F32))
    *part_s, cnt_s = merge(x_sample.reshape(ns, d), y_small[:bd].reshape(ns, hw), att_s.reshape(ns, aw),
                           sga_s[:ns], sgb_s[:ns], cnt0=cnt_p)

    out_p, out_s = _moe([part_p, part_s], cnt_s, w_expert_gate[l], w_expert_up[l], w_expert_down[l])

    return (out_p.reshape(b, s, d), out_s.reshape(bd, sd, d), new_k_p, new_v_p, st_p[None],
            new_k_s, new_v_s, st_small[:bd][None])
```

```python
import functools
import math

import numpy as np
import jax
import jax.numpy as jnp
from jax import lax
from jax.experimental import pallas as pl
from jax.experimental.pallas import tpu as pltpu

F32 = jnp.float32
BF16 = jnp.bfloat16
I32 = jnp.int32

CHUNK = 64
N_META = 16
PAST_LEN = 2048
EPS = 1e-6
HGRN_HEADS = 4
ATTN_HEADS = 8
KV_HEADS = 2
HEAD_DIM = 64
GQA_GROUP = ATTN_HEADS // KV_HEADS
WINDOW = 128
WINDOW_CHUNKS = WINDOW // CHUNK
NUM_BUCKETS = 32
MAX_DISTANCE = 128
N_GROUPS = 4
EXPERTS_PER_GROUP = 8
N_EXPERTS = N_GROUPS * EXPERTS_PER_GROUP
TOP_K = 2

LANES = 128
MXU_WIDTH = 256
VMEM_LIMIT = 56 * 1024 * 1024

INPROJ_TILE = 512
PROJ_TILE = 512
HGRN_CHUNK = 128
HGRN_CHUNKS_PER_STEP = 8
ATTN_CHUNKS_PER_STEP = 16
SAMPLE_STREAMS_PER_STEP = 4
EXPERT_TILE = 512


LOG2E = math.log2(math.e)


def _sigmoid(x):
    return 1.0 / (1.0 + jnp.exp(-x))


def _split3(x):
    hi = x.astype(BF16)
    r1 = x - hi.astype(F32)
    mid = r1.astype(BF16)
    lo = (r1 - mid.astype(F32)).astype(BF16)
    return hi, mid, lo


def _dot(a, b):
    return jnp.dot(a, b, preferred_element_type=F32)


def _dot_nt(a, b):
    return lax.dot_general(a, b, (((1,), (1,)), ((), ())), preferred_element_type=F32)


def _dot_tn(a, b):
    return lax.dot_general(a, b, (((0,), (0,)), ((), ())), preferred_element_type=F32)


SUBLANES = 8


def _store_row_tiles(ref, x):
    n, d = x.shape
    sub = d // LANES
    for j in range(sub):
        ref[pl.ds(j, n, stride=sub), :] = x[:, j * LANES:(j + 1) * LANES]


def _load_row_tiles(ref, n, sub):
    return jnp.concatenate([ref[pl.ds(j, n, stride=sub), :] for j in range(sub)], axis=1)


def _params(*sem):
    return pltpu.CompilerParams(dimension_semantics=sem, vmem_limit_bytes=VMEM_LIMIT)


def _inproj_kernel(hw, aw, kvw, d, x_ref, wn_ref, w_ref, qg_ref, kg_ref, bdq_ref, bdk_ref,
                   qr_ref, z_ref, hv_ref, hg_ref, qa_ref, k_ref, v_ref, sga_ref, sgb_ref):
    x = x_ref[...]
    ms = jnp.mean(x * x, axis=-1, keepdims=True)
    xn = (x * lax.rsqrt(ms + EPS) * wn_ref[...]).astype(BF16)

    def seg(a, b):
        return _dot(xn, w_ref[:, a:b])

    def head_rms(a, bd_ref, gain):
        sq = (a * a).astype(BF16)
        wb = bd_ref.shape[0]
        m = jnp.concatenate([_dot(sq[:, c:c + wb], bd_ref[...]) for c in range(0, a.shape[1], wb)], axis=1)
        return a * lax.rsqrt(m + EPS) * gain

    o = 0
    hq = seg(o, o + hw)
    qr_ref[...] = (hq * _sigmoid(hq) * (hw // HGRN_HEADS) ** -0.5).astype(BF16)
    o += hw
    z_ref[...] = seg(o, o + hw)
    o += hw
    hv_ref[...] = seg(o, o + hw).astype(BF16)
    o += hw
    hg_ref[...] = seg(o, o + hw).astype(BF16)
    o += hw
    aq = seg(o, o + aw)
    qa_ref[...] = (head_rms(aq, bdq_ref, qg_ref[...]) * (HEAD_DIM ** -0.5 * LOG2E)).astype(BF16)
    o += aw
    k_ref[...] = head_rms(seg(o, o + kvw), bdk_ref, kg_ref[...])
    o += kvw
    v_ref[...] = seg(o, o + kvw)
    o += kvw
    sga_ref[...] = _sigmoid(seg(o, o + d)).astype(BF16)
    o += d
    sgb_ref[...] = _sigmoid(seg(o, o + d)).astype(BF16)


def _block_diag_mean(width, group):
    i = np.arange(width)
    return jnp.asarray((i[:, None] // group == i[None, :] // group) / group, dtype=BF16)


def _inproj(x, w_norm, w_in_bf16, q_gain, k_gain, hw, aw, kvw):
    t, d = x.shape
    tm = INPROJ_TILE if t % INPROJ_TILE == 0 else t
    cols = w_in_bf16.shape[1]
    row = lambda w: pl.BlockSpec((tm, w), lambda i: (i, 0))
    const = lambda a, b: pl.BlockSpec((a, b), lambda i: (0, 0))
    outs = [(hw, BF16), (hw, F32), (hw, BF16), (hw, BF16), (aw, BF16), (kvw, F32), (kvw, F32), (d, BF16), (d, BF16)]
    bdq = min(aw, MXU_WIDTH)
    assert aw % bdq == 0
    return pl.pallas_call(
        functools.partial(_inproj_kernel, hw, aw, kvw, d),
        grid=(t // tm,),
        in_specs=[row(d), const(1, d), const(d, cols), const(1, aw), const(1, kvw), const(bdq, bdq), const(kvw, kvw)],
        out_specs=[row(w) for w, _ in outs],
        out_shape=[jax.ShapeDtypeStruct((t, w), dt) for w, dt in outs],
        compiler_params=_params("arbitrary"),
        name="inproj",
    )(x, w_norm.reshape(1, d), w_in_bf16,
      jnp.tile(q_gain, aw // HEAD_DIM).reshape(1, aw), jnp.tile(k_gain, kvw // HEAD_DIM).reshape(1, kvw),
      _block_diag_mean(bdq, HEAD_DIM), _block_diag_mean(kvw, HEAD_DIM))


def _hgrn_consts(L):
    t = np.arange(L)
    u = t[None, :]
    blocks = [u <= t[:, None], u > t[:, None]]
    levels = []
    m = L // 2
    while m >= 1:
        levels.append(m)
        m //= 2
    lvl = np.full((L, L), -1, np.int32)
    lvl[t, t] = len(levels)
    isq_cols = []
    for j, m in enumerate(levels):
        bnd = (t // (2 * m)) * (2 * m) + m - 1
        isq = (t % (2 * m)) >= m
        cq = isq[:, None] & (u > bnd[:, None]) & (u <= t[:, None])
        ck = (~isq)[:, None] & (u > t[:, None]) & (u <= bnd[:, None])
        blocks.append(cq | ck)
        same = (t[:, None] // (2 * m)) == (t[None, :] // (2 * m))
        lvl[same & isq[:, None] & (~isq)[None, :]] = j
        isq_cols.append(isq)
    c = np.concatenate(blocks, axis=0).astype(np.float32)
    isq = np.stack(isq_cols, axis=1).astype(np.float32)
    isq = np.pad(isq, ((0, 0), (0, LANES - isq.shape[1])))
    c2 = np.tile(c, (1, 2))
    return jnp.asarray(c2, dtype=BF16), jnp.asarray(np.tile(lvl, (1, 2))), jnp.asarray(isq), len(levels)


def _hgrn_kernel(L, nlev, heads, dk, qr_ref, z_ref, hv_ref, hg_ref, lb_ref, og_ref, c_ref, lvl_ref, isq_ref,
                 s0_ref, y_ref, sout_ref, st_ref):
    c = pl.program_id(1)

    @pl.when(c == 0)
    def _():
        for h in range(heads):
            st_ref[h] = s0_ref[0, h].T

    for cc in range(z_ref.shape[1] // L):
        _hgrn_chunk(L, nlev, heads, dk, slice(cc * L, (cc + 1) * L), qr_ref, z_ref, hv_ref, hg_ref, lb_ref, og_ref,
                    c_ref, lvl_ref, isq_ref, y_ref, st_ref)

    @pl.when(c == pl.num_programs(1) - 1)
    def _():
        for h in range(heads):
            sout_ref[0, h] = st_ref[h].T


def _hgrn_chunk(L, nlev, heads, dk, rows, qr_ref, z_ref, hv_ref, hg_ref, lb_ref, og_ref, c_ref, lvl_ref, isq_ref,
                y_ref, st_ref):
    z = z_ref[0, rows, :]
    lb = lb_ref[...]
    e = jnp.exp(-jnp.abs(z))
    r = 1.0 / (1.0 + e)
    pos = z >= 0
    sig = jnp.where(pos, r, e * r)
    sig_neg = jnp.where(pos, e * r, r)
    logf = jnp.log(lb + (1.0 - lb) * sig)
    kin = (1.0 - lb) * sig_neg
    q = qr_ref[0, rows, :].astype(F32)

    hi, mid, _ = _split3(logf * LOG2E)
    ex = jnp.exp2(_dot(c_ref[...], jnp.concatenate([hi, mid], axis=0)))
    e_b = ex[0:L]
    e_rev = ex[L:2 * L]

    q_in = (q * e_b).astype(BF16)
    k_out = (kin * e_rev).astype(BF16)
    q_b = q.astype(BF16)
    k_b = kin.astype(BF16)
    xs = []
    for j in range(nlev):
        m = L >> (j + 1)
        if m % SUBLANES == 0:
            qk = jnp.concatenate([(q if blk % 2 else kin)[blk * m:(blk + 1) * m] for blk in range(L // m)], axis=0)
        else:
            qk = jnp.where(isq_ref[:, j:j + 1] > 0.5, q, kin)
        xs.append((qk * ex[(2 + j) * L:(3 + j) * L]).astype(BF16))
    lvl = lvl_ref[...]
    v = hv_ref[0, rows, :]
    g = hg_ref[0, rows, :].astype(F32)
    og = og_ref[...]

    def block_diag(x):
        zero = jnp.zeros((x.shape[0], dk), x.dtype)
        return jnp.concatenate([jnp.concatenate([x[:, :dk], zero], axis=1),
                                jnp.concatenate([zero, x[:, dk:]], axis=1)], axis=0)

    for pair in range(heads // 2):
        sl = slice(2 * pair * dk, 2 * (pair + 1) * dk)
        a = jnp.where(lvl == nlev, _dot_nt(q_b[:, sl], block_diag(k_b[:, sl])), 0.0)
        for j in range(nlev):
            xp = xs[j][:, sl]
            a = jnp.where(lvl == j, _dot_nt(xp, block_diag(xp)), a)
        st = jnp.concatenate([st_ref[2 * pair], st_ref[2 * pair + 1]], axis=1)
        vp = v[:, sl]
        o = _dot(a.astype(BF16), block_diag(vp)) + _dot_nt(q_in[:, sl], block_diag(st.astype(BF16)))
        for half in range(2):
            h = 2 * pair + half
            hs = slice(h * dk, (h + 1) * dk)
            st_ref[h] = st_ref[h] * e_b[L - 1:L, hs] + _dot_tn(v[:, hs], k_out[:, hs])
            oh = o[:, half * dk:(half + 1) * dk]
            ms = jnp.mean(oh * oh, axis=-1, keepdims=True)
            gh = g[:, hs]
            y_ref[0, rows, hs] = (oh * lax.rsqrt(ms + EPS) * og[:, hs] * (gh * _sigmoid(gh))).astype(BF16)


def _hgrn(qr, z, hv, hg, lb, out_gain, s0, L, shared_s0=None):
    b, s, w = z.shape
    heads, dk = s0.shape[1], s0.shape[2]
    cm, lvl, isq, nlev = _hgrn_consts(L)
    per_step = HGRN_CHUNKS_PER_STEP if s % (HGRN_CHUNKS_PER_STEP * L) == 0 else 1
    seq = pl.BlockSpec((1, per_step * L, w), lambda i, c: (i, c, 0))
    const = lambda a: pl.BlockSpec(a.shape, lambda i, c: (0,) * a.ndim)
    state = pl.BlockSpec((1, heads, dk, dk), lambda i, c: (i, 0, 0, 0))
    start = state if shared_s0 is None else pl.BlockSpec((1, heads, dk, dk), lambda i, c: (shared_s0, 0, 0, 0))
    lb2 = lb.reshape(1, w)
    og2 = jnp.tile(out_gain, heads).reshape(1, w)
    return pl.pallas_call(
        functools.partial(_hgrn_kernel, L, nlev, heads, dk),
        grid=(b, s // (per_step * L)),
        in_specs=[seq, seq, seq, seq, const(lb2), const(og2), const(cm), const(lvl), const(isq), start],
        out_specs=[seq, state],
        out_shape=[jax.ShapeDtypeStruct((b, s, w), BF16), jax.ShapeDtypeStruct((b,) + s0.shape[1:], F32)],
        scratch_shapes=[pltpu.VMEM((heads, dk, dk), F32)],
        compiler_params=_params("arbitrary", "arbitrary"),
        name=f"hgrn_scan_{L}",
    )(qr, z, hv, hg, lb2, og2, cm, lvl, isq, s0)


def _t5_bucket_np(rel):
    half = NUM_BUCKETS // 2
    max_exact = half // 2
    assert (NUM_BUCKETS, MAX_DISTANCE) == (32, 128)
    n = np.abs(rel).astype(np.int64)
    nn = np.maximum(n, 1)
    k = np.zeros_like(nn)
    for j in range(1, 48):
        k = np.where(64 * (1 << j) <= nn * nn, j, k)
    large = np.minimum(max_exact + k, half - 1)
    return np.where(rel > 0, half, 0) + np.where(n < max_exact, n, large)


HEADS_PER_COL = LANES // HEAD_DIM
COLS_PER_GROUP = GQA_GROUP // HEADS_PER_COL
HEAD_ORDER = tuple(g * GQA_GROUP + col * HEADS_PER_COL + half
                   for g in range(KV_HEADS) for half in range(HEADS_PER_COL) for col in range(COLS_PER_GROUP))
KV_EXPAND = KV_HEADS * HEADS_PER_COL


def _expand_kv(x, with_ones=False):
    assert HEADS_PER_COL == 2 and KV_HEADS == 2 and x.shape[1] == LANES
    low = lax.broadcasted_iota(I32, x.shape, 1) < HEAD_DIM
    xr = pltpu.roll(x, HEAD_DIM, axis=1)
    zero = jnp.zeros_like(x)
    blocks = [jnp.where(low, x, zero), jnp.where(low, zero, xr), jnp.where(low, xr, zero), jnp.where(low, zero, x)]
    if with_ones:
        blocks = [b for blk in blocks for b in (blk, jnp.ones_like(x))]
    return jnp.concatenate(blocks, axis=1).astype(BF16)


def _attn_core(q, kx, vx, bias):
    tq = q.shape[0]
    scores = []
    for g in range(KV_HEADS):
        cols = [q[:, (g * COLS_PER_GROUP + c) * LANES:(g * COLS_PER_GROUP + c + 1) * LANES]
                for c in range(COLS_PER_GROUP)]
        qst = jnp.concatenate(cols, axis=0)
        for half in range(HEADS_PER_COL):
            blk = g * HEADS_PER_COL + half
            scores.append(_dot_nt(qst, kx[:, blk * LANES:(blk + 1) * LANES]))
    s = jnp.concatenate(scores, axis=0) + bias
    pb = jnp.exp2(s - jnp.max(s, axis=-1, keepdims=True)).astype(BF16)
    rows = COLS_PER_GROUP * tq
    outs = []
    for g in range(KV_HEADS):
        o = None
        for half in range(HEADS_PER_COL):
            blk = g * HEADS_PER_COL + half
            pv = _dot(pb[blk * rows:(blk + 1) * rows], vx[:, 2 * blk * LANES:2 * (blk + 1) * LANES])
            part = pv[:, :LANES] * (1.0 / pv[:, LANES:])
            o = part if o is None else o + part
        outs.extend(o[c * tq:(c + 1) * tq] for c in range(COLS_PER_GROUP))
    return outs


def _bias_rows(table, bucket):
    onehot = (jnp.asarray(bucket)[..., None] == jnp.arange(NUM_BUCKETS)).astype(F32)
    cols = jnp.stack([table[:, h] for h in HEAD_ORDER], axis=1)
    bias = jnp.einsum('...qkb,bh->...hqk', onehot, cols, precision=lax.Precision.HIGHEST)
    return bias.reshape(*bucket.shape[:-2], ATTN_HEADS * bucket.shape[-2], bucket.shape[-1]) * LOG2E


def _pad_keys(bias, sinks):
    tk = bias.shape[-1]
    n_pad = -tk % LANES or LANES
    tq = bias.shape[-2] // ATTN_HEADS
    sink = jnp.repeat(jnp.stack([sinks[h] for h in HEAD_ORDER]).astype(F32), tq) * LOG2E
    sink = jnp.broadcast_to(sink[:, None], bias.shape[:-1] + (1,))
    masked = jnp.full(bias.shape[:-1] + (n_pad - 1,), -jnp.inf, F32)
    return jnp.concatenate([bias, sink, masked], axis=-1)


def _store_heads(ref, x, stream=0):
    for g in range(KV_HEADS):
        ref[0, stream, :, g, :] = x[:, g * HEAD_DIM:(g + 1) * HEAD_DIM]


def _load_heads(ref, stream=0):
    return jnp.concatenate([ref[0, stream, :, g, :] for g in range(KV_HEADS)], axis=1)


def _attn_prompt_kernel(cb, q_ref, k_ref, v_ref, km_ref, vm_ref, bias_ref, o_ref, nk_ref, nv_ref,
                        kx_ref, vx_ref):
    step = pl.program_id(1)
    s_len = k_ref.shape[1]
    meta_at = WINDOW + s_len

    @pl.when(step == pl.num_programs(1) - 1)
    def _():
        _store_heads(nk_ref, k_ref[0, s_len - WINDOW:, :])
        _store_heads(nv_ref, v_ref[0, s_len - WINDOW:, :])

    @pl.when(step == 0)
    def _():
        piece = min(s_len, 512)
        for src, meta, dst in ((k_ref, km_ref, kx_ref), (v_ref, vm_ref, vx_ref)):
            expand = functools.partial(_expand_kv, with_ones=dst is vx_ref)
            blank = lambda n: expand(jnp.zeros((n, src.shape[2]), F32))
            dst[0:WINDOW] = blank(WINDOW)
            for r in range(0, s_len, piece):
                dst[WINDOW + r:WINDOW + r + piece] = expand(src[0, r:r + piece, :])
            dst[meta_at:meta_at + N_META] = expand(meta[...])
            dst[meta_at + N_META:] = blank(dst.shape[0] - meta_at - N_META)

    win = WINDOW + CHUNK
    tail = kx_ref.shape[0] - meta_at
    for j in range(cb):
        c = step * cb + j
        start = pl.multiple_of(c * CHUNK, CHUNK)
        kall = jnp.concatenate([kx_ref[pl.ds(start, win), :], kx_ref[meta_at:meta_at + tail, :]], axis=0)
        vall = jnp.concatenate([vx_ref[pl.ds(start, win), :], vx_ref[meta_at:meta_at + tail, :]], axis=0)
        rows = slice(j * CHUNK, (j + 1) * CHUNK)
        outs = _attn_core(q_ref[0, rows, :], kall, vall, bias_ref[jnp.minimum(c, bias_ref.shape[0] - 1)])
        for ci, o in enumerate(outs):
            o_ref[0, rows, ci * LANES:(ci + 1) * LANES] = o.astype(o_ref.dtype)


def _attn_prompt(q, k, v, k_meta, v_meta, table, sinks):
    b, s, aw = q.shape
    kvw = k.shape[-1]
    nc = s // CHUNK
    cb = ATTN_CHUNKS_PER_STEP if nc % ATTN_CHUNKS_PER_STEP == 0 else 1
    assert s % min(s, 512) == 0
    n_bias = 1
    while True:
        qpos = N_META + (n_bias - 1) * CHUNK
        if np.all(_t5_bucket_np(np.arange(N_META) - qpos) == _t5_bucket_np(np.arange(N_META) - qpos - 10 ** 6)):
            break
        n_bias += 1
    n_bias = min(max(n_bias, WINDOW_CHUNKS + 1), nc)
    cs = np.arange(n_bias)[:, None]
    qpos = N_META + cs * CHUNK + np.arange(CHUNK)[None]
    wpos = N_META + (cs - WINDOW_CHUNKS) * CHUNK + np.arange(WINDOW + CHUNK)[None]
    kpos = np.concatenate([wpos, np.broadcast_to(np.arange(N_META), (n_bias, N_META))], axis=1)
    valid = np.concatenate([wpos >= N_META, np.ones((n_bias, N_META), bool)], axis=1)
    bias = _bias_rows(table, _t5_bucket_np(kpos[:, None, :] - qpos[:, :, None]))
    bias = _pad_keys(jnp.where(valid[:, None, :], bias, -jnp.inf), sinks)
    qs =pl.BlockSpec((1, cb * CHUNK, aw), lambda i, c: (i, c, 0))
    kv = pl.BlockSpec((1, s, kvw), lambda i, c: (i, 0, 0))
    meta = pl.BlockSpec((N_META, kvw), lambda i, c: (0, 0))
    xrows = WINDOW + s + bias.shape[-1] - (WINDOW + CHUNK)
    cache_shape = (1, b, WINDOW, KV_HEADS, HEAD_DIM)
    cache = pl.BlockSpec((1, 1) + cache_shape[2:], lambda i, c: (0, i, 0, 0, 0))
    return pl.pallas_call(
        functools.partial(_attn_prompt_kernel, cb),
        grid=(b, nc // cb),
        in_specs=[qs, kv, kv, meta, meta, pl.BlockSpec(bias.shape, lambda i, c: (0, 0, 0))],
        out_specs=[qs, cache, cache],
        out_shape=[jax.ShapeDtypeStruct((b, s, aw), BF16)] + [jax.ShapeDtypeStruct(cache_shape, F32)] * 2,
        scratch_shapes=[pltpu.VMEM((xrows, KV_EXPAND * LANES), BF16),
                        pltpu.VMEM((xrows, 2 * KV_EXPAND * LANES), BF16)],
        compiler_params=_params("arbitrary", "arbitrary"),
        name="attn_prompt",
    )(q, k, v, k_meta, v_meta, bias)


def _attn_sample_kernel(q_ref, kc_ref, vc_ref, kn_ref, vn_ref, km_ref, vm_ref, bias_ref, o_ref, nk_ref, nv_ref):
    tk = kc_ref.shape[2] + kn_ref.shape[1] + km_ref.shape[0]
    zeros = jnp.zeros((bias_ref.shape[1] - tk, km_ref.shape[1]), F32)
    for s in range(q_ref.shape[0]):
        kc, vc, kn, vn = _load_heads(kc_ref, s), _load_heads(vc_ref, s), kn_ref[s], vn_ref[s]
        kall = _expand_kv(jnp.concatenate([kc, kn, km_ref[...], zeros], axis=0))
        vall = _expand_kv(jnp.concatenate([vc, vn, vm_ref[...], zeros], axis=0), with_ones=True)
        outs = _attn_core(q_ref[s], kall, vall, bias_ref[...])
        for ci, o in enumerate(outs):
            o_ref[s, :, ci * LANES:(ci + 1) * LANES] = o.astype(o_ref.dtype)
        n_new = kn.shape[0]
        _store_heads(nk_ref, jnp.concatenate([kc[n_new:], kn], axis=0), s)
        _store_heads(nv_ref, jnp.concatenate([vc[n_new:], vn], axis=0), s)


def _attn_sample(q, k_cache, v_cache, k_new, v_new, k_meta, v_meta, table, sinks):
    bd, sd, aw = q.shape
    kvw = k_new.shape[-1]
    win = k_cache.shape[2]
    sb = SAMPLE_STREAMS_PER_STEP if bd % SAMPLE_STREAMS_PER_STEP == 0 else 1
    cache = pl.BlockSpec((1, sb) + k_cache.shape[2:], lambda i: (0, i, 0, 0, 0))
    qpos = N_META + PAST_LEN + np.arange(sd)
    kpos = np.concatenate([N_META + PAST_LEN - win + np.arange(win), qpos, np.arange(N_META)])
    bias = _pad_keys(_bias_rows(table, _t5_bucket_np(kpos[None, :] - qpos[:, None])), sinks)
    per = lambda n, w: pl.BlockSpec((sb, n, w), lambda i: (i, 0, 0))
    meta = pl.BlockSpec((N_META, kvw), lambda i: (0, 0))
    return pl.pallas_call(
        _attn_sample_kernel,
        grid=(bd // sb,),
        in_specs=[per(sd, aw), cache, cache, per(sd, kvw), per(sd, kvw), meta, meta,
                  pl.BlockSpec(bias.shape, lambda i: (0, 0))],
        out_specs=[per(sd, aw), cache, cache],
        out_shape=[jax.ShapeDtypeStruct((bd, sd, aw), BF16)] + [jax.ShapeDtypeStruct(k_cache.shape, F32)] * 2,
        compiler_params=_params("arbitrary"),
        name="attn_sample",
    )(q, k_cache, v_cache, k_new, v_new, k_meta, v_meta, bias)


ROUTE_E1, ROUTE_E2, ROUTE_R1, ROUTE_R2, ROUTE_W1, ROUTE_W2 = range(6)
META_FIELDS = 2 * TOP_K


def _merge_kernel(hw, x_ref, yr_ref, at_ref, sga_ref, sgb_ref, wb_ref, wo_ref, wn_ref, wr_ref, br_ref, cnt0_ref, tri_ref,
                  h_ref, xn_ref, route_ref, meta_ref, cnt_ref, carry_ref):
    @pl.when(pl.program_id(0) == 0)
    def _():
        carry_ref[...] = cnt0_ref[...]

    logits = _project_rows(hw, x_ref, yr_ref, at_ref, sga_ref, sgb_ref, wb_ref, wo_ref, wn_ref, wr_ref, br_ref,
                           h_ref, xn_ref)
    _route_rows(logits, tri_ref, route_ref, meta_ref, carry_ref)
    cnt_ref[...] = carry_ref[...]


def _project_rows(hw, x_ref, yr_ref, at_ref, sga_ref, sgb_ref, wb_ref, wo_ref, wn_ref, wr_ref, br_ref, h_ref, xn_ref):
    br = _dot(yr_ref[...], wb_ref[0:hw, :])
    ba = _dot(at_ref[...], wb_ref[hw:, :])
    merged = sga_ref[...].astype(F32) * br + sgb_ref[...].astype(F32) * ba
    h = x_ref[...] + _dot(merged.astype(BF16), wo_ref[...])
    h_ref[...] = h
    ms = jnp.mean(h * h, axis=-1, keepdims=True)
    xn = h * lax.rsqrt(ms + EPS) * wn_ref[...]
    _store_row_tiles(xn_ref, xn)

    x_hi = xn.astype(BF16)
    x_lo = (xn - x_hi.astype(F32)).astype(BF16)
    both = _dot(x_hi, wr_ref[...])
    return both[:, :LANES] + both[:, LANES:] + _dot(x_lo, wr_ref[:, :LANES]) + br_ref[...]


def _route_rows(logits, tri_ref, route_ref, meta_ref, carry_ref):
    tm = logits.shape[0]
    lane_i = lax.broadcasted_iota(I32, (tm, LANES), 1)
    lane = lane_i.astype(F32)
    group_of_lane = (lane_i >> int(math.log2(EXPERTS_PER_GROUP))).astype(F32)
    ninf = -jnp.inf
    first = lambda hit, idx: jnp.min(jnp.where(hit, idx, float(LANES)), axis=-1, keepdims=True)
    gmask = (lane_i >= N_EXPERTS) & (lane_i < N_EXPERTS + N_GROUPS)
    gl = jnp.where(gmask, logits, ninf)
    gmax = jnp.max(gl, axis=-1, keepdims=True)
    gidx = first(gl == gmax, lane - N_EXPERTS)
    gval = 1.0 / jnp.sum(jnp.exp(gl - gmax), axis=-1, keepdims=True)
    emask = (lane_i < N_EXPERTS) & (group_of_lane == gidx)
    el = jnp.where(emask, logits, ninf)
    m1 = jnp.max(el, axis=-1, keepdims=True)
    i1 = first(el == m1, lane)
    el2 = jnp.where(lane == i1, ninf, el)
    m2 = jnp.max(el2, axis=-1, keepdims=True)
    i2 = first(el2 == m2, lane)
    e21 = jnp.exp(m2 - m1)
    w1 = gval / (1.0 + e21)
    w2 = gval * e21 / (1.0 + e21)

    sel1 = lane == i1
    sel2 = lane == i2
    oh = (sel1 | sel2).astype(BF16)
    before = _dot(tri_ref[...], oh) + carry_ref[...]
    r1 = jnp.sum(jnp.where(sel1, before, 0.0), axis=-1, keepdims=True)
    r2 = jnp.sum(jnp.where(sel2, before, 0.0), axis=-1, keepdims=True)
    carry_ref[...] = carry_ref[...] + jnp.sum(oh.astype(F32), axis=0, keepdims=True)

    rec = jnp.zeros((tm, LANES), F32)
    for slot, val in ((ROUTE_E1, i1), (ROUTE_E2, i2), (ROUTE_W1, w1), (ROUTE_W2, w2), (ROUTE_R1, r1), (ROUTE_R2, r2)):
        rec = jnp.where(lane_i == slot, val, rec)
    route_ref[...] = rec
    meta_ref[0] = rec.T[ROUTE_E1:ROUTE_E1 + META_FIELDS].astype(I32)


def _merge(x, y_rec, att, sga, sgb, wb, wo, w_norm, w_router, b_router, cnt0):
    t, d = x.shape
    hw = y_rec.shape[1]
    tm = PROJ_TILE if t % PROJ_TILE == 0 else t
    row = lambda w: pl.BlockSpec((tm, w), lambda i: (i, 0))
    const = lambda a, b: pl.BlockSpec((a, b), lambda i: (0, 0))
    return pl.pallas_call(
        functools.partial(_merge_kernel, hw),
        grid=(t // tm,),
        in_specs=[row(d), row(hw), row(att.shape[1]), row(d), row(d), const(*wb.shape), const(d, d), const(1, d),
                  const(d, 2 * LANES), const(1, LANES), const(1, LANES), const(tm, tm)],
        out_specs=[row(d), pl.BlockSpec((tm * d // LANES, LANES), lambda i: (i, 0)), row(LANES),
                   pl.BlockSpec((1, META_FIELDS, tm), lambda i: (i, 0, 0)), const(1, LANES)],
        out_shape=[jax.ShapeDtypeStruct((t, d), F32), jax.ShapeDtypeStruct((t * d // LANES, LANES), F32),
                   jax.ShapeDtypeStruct((t, LANES), F32), jax.ShapeDtypeStruct((t // tm, META_FIELDS, tm), I32),
                   jax.ShapeDtypeStruct((1, LANES), F32)],
        scratch_shapes=[pltpu.VMEM((1, LANES), F32)],
        compiler_params=_params("arbitrary"),
        name="merge_route",
    )(x, y_rec, att, sga, sgb, wb, wo, w_norm.reshape(1, d), w_router, b_router, cnt0,
      jnp.asarray(np.tri(tm, k=-1), dtype=BF16))


def _row_copy(src, dst, sem):
    return pltpu.make_async_copy(src, dst, sem)


ROW_UNROLL = 8


def _tile_rows(ref, row, sub):
    return ref.at[pl.ds(pl.multiple_of(row * sub, sub), sub)]


def _scatter_kernel(n_tok, sub, first, slot_ref, pstart_ref, pend_ref, x_ref, *rest):
    xs_ref, zero_ref, sem, zsem, tsem = rest[-5:]

    def zero_blocks(blocks, zs, wait):
        for cond, row in blocks:
            @pl.when(cond)
            def _():
                at = row * sub if isinstance(row, int) else pl.multiple_of(row * sub, EXPERT_TILE * sub)
                cp = _row_copy(zero_ref, xs_ref.at[pl.ds(at, EXPERT_TILE * sub)], zs)
                cp.wait() if wait else cp.start()

    if first:
        n_rows = xs_ref.shape[0] // sub
        tails = [(pend_ref[e] > pstart_ref[e], pend_ref[e] - EXPERT_TILE) for e in range(N_EXPERTS)]
        unused = [(n_rows - (j + 1) * EXPERT_TILE >= pend_ref[N_EXPERTS - 1], n_rows - (j + 1) * EXPERT_TILE)
                  for j in range(N_EXPERTS)]

        @pl.when(pl.program_id(0) == 0)
        def _():
            zero_ref[...] = jnp.zeros_like(zero_ref)
            zero_blocks(tails, zsem, False)
            zero_blocks(unused, tsem, False)
            zero_blocks(tails, zsem, True)

    def issue(grp, _):
        for u in range(ROW_UNROLL):
            r = grp * ROW_UNROLL + u
            for k in range(TOP_K):
                _row_copy(_tile_rows(x_ref, r, sub), _tile_rows(xs_ref, slot_ref[0, k, r], sub),
                          sem).start(priority=(u * TOP_K + k) % 2)
        return 0

    lax.fori_loop(0, n_tok // ROW_UNROLL, issue, 0)
    for k in range(TOP_K):
        _row_copy(x_ref, xs_ref.at[pl.ds(0, n_tok * sub)], sem).wait()

    if first:
        @pl.when(pl.program_id(0) == 0)
        def _():
            zero_blocks(unused, tsem, True)


def _scatter_rows(x, slots, pstarts, pends, xs, rows):
    n, fields, tm = slots.shape
    sub = x.shape[0] // (n * tm)
    first = xs is None
    smem = pl.BlockSpec(memory_space=pltpu.SMEM)
    return pl.pallas_call(
        functools.partial(_scatter_kernel, tm, sub, first),
        grid=(n,),
        in_specs=[pl.BlockSpec((1, fields, tm), lambda i: (i, 0, 0), memory_space=pltpu.SMEM), smem, smem,
                  pl.BlockSpec((tm * sub, LANES), lambda i: (i, 0))]
        + ([] if first else [pl.BlockSpec(memory_space=pl.ANY)]),
        out_specs=pl.BlockSpec(memory_space=pl.ANY),
        out_shape=jax.ShapeDtypeStruct((rows * sub, LANES), F32),
        scratch_shapes=[pltpu.VMEM((EXPERT_TILE * sub, LANES), F32)] + [pltpu.SemaphoreType.DMA(())] * 3,
        input_output_aliases={} if first else {4: 0},
        compiler_params=_params("arbitrary"),
        name="moe_scatter",
    )(slots, pstarts, pends, x, *([] if first else [xs]))


def _expert_kernel(tm, sub, be_ref, nb_ref, nxt_ref, x_ref, wg_ref, wu_ref, wd_ref, y_ref,
                   raw_g, raw_u, raw_d, wgb_ref, wub_ref, wdb_ref, sem, slot_ref):
    i = pl.program_id(0)
    used = i < nb_ref[0]
    first = used & ((i == 0) | (be_ref[i] != be_ref[jnp.maximum(i - 1, 0)]))

    def fetch(e, slot):
        return [_row_copy(w.at[e], raw.at[slot], sem.at[slot])
                for w, raw in ((wg_ref, raw_g), (wu_ref, raw_u), (wd_ref, raw_d))]

    @pl.when(i == 0)
    def _():
        slot_ref[0] = 0
        for cp in fetch(be_ref[0], 0):
            cp.start()

    @pl.when(first)
    def _():
        slot = slot_ref[0]
        for cp in fetch(be_ref[i], slot):
            cp.wait()
        wgb_ref[...] = raw_g[slot].astype(BF16)
        wub_ref[...] = raw_u[slot].astype(BF16)
        wdb_ref[...] = raw_d[slot].astype(BF16)
        slot_ref[0] = 1 - slot

        @pl.when(nxt_ref[i] >= 0)
        def _():
            for cp in fetch(nxt_ref[i], 1 - slot):
                cp.start()

    @pl.when(used)
    def _():
        x = _load_row_tiles(x_ref, tm, sub).astype(BF16)
        g = _dot(x, wgb_ref[...])
        u = _dot(x, wub_ref[...])
        hmid = (g * _sigmoid(g) * u).astype(BF16)
        _store_row_tiles(y_ref, _dot(hmid, wdb_ref[...]))

    @pl.when(jnp.logical_not(used))
    def _():
        y_ref[...] = jnp.zeros_like(y_ref)


def _experts(xs, block_e, n_used, nxt_e, wg, wu, wd):
    _, d, ff = wg.shape
    sub = d // LANES
    tm = EXPERT_TILE
    blk = pl.BlockSpec((tm * sub, LANES), lambda i, be, nb, nx: (i, 0))
    hbm = pl.BlockSpec(memory_space=pl.ANY)
    grid_spec = pltpu.PrefetchScalarGridSpec(
        num_scalar_prefetch=3,
        grid=(xs.shape[0] // (tm * sub),),
        in_specs=[blk, hbm, hbm, hbm],
        out_specs=blk,
        scratch_shapes=[pltpu.VMEM((2, d, ff), F32), pltpu.VMEM((2, d, ff), F32), pltpu.VMEM((2, ff, d), F32),
                        pltpu.VMEM((d, ff), BF16), pltpu.VMEM((d, ff), BF16), pltpu.VMEM((ff, d), BF16),
                        pltpu.SemaphoreType.DMA((2,)), pltpu.SMEM((1,), I32)],
    )
    return pl.pallas_call(
        functools.partial(_expert_kernel, tm, sub),
        grid_spec=grid_spec,
        out_shape=jax.ShapeDtypeStruct(xs.shape, F32),
        compiler_params=_params("arbitrary"),
        name="moe_experts",
    )(block_e, n_used, nxt_e, xs, wg, wu, wd)


def _combine_kernel(n_tok, sub, scur_ref, snext_ref, h_ref, route_ref, ys_ref, o_ref, buf, sem):
    i = pl.program_id(0)
    n = pl.num_programs(0)
    slot = i % 2

    def start(slot_ref, s):
        def issue(grp, _):
            for u in range(ROW_UNROLL):
                r = grp * ROW_UNROLL + u
                for k in range(TOP_K):
                    _row_copy(_tile_rows(ys_ref, slot_ref[0, k, r], sub), _tile_rows(buf.at[s, k], r, sub),
                              sem.at[s]).start(priority=(u * TOP_K + k) % 2)
            return 0
        lax.fori_loop(0, n_tok // ROW_UNROLL, issue, 0)

    @pl.when(i == 0)
    def _():
        start(scur_ref, 0)

    @pl.when(i + 1 < n)
    def _():
        start(snext_ref, 1 - slot)

    for k in range(TOP_K):
        _row_copy(ys_ref.at[pl.ds(0, n_tok * sub)], buf.at[slot, k], sem.at[slot]).wait()
    route = route_ref[...]
    w1 = route[:, ROUTE_W1:ROUTE_W1 + 1]
    w2 = route[:, ROUTE_W2:ROUTE_W2 + 1]
    for j in range(sub):
        cols = slice(j * LANES, (j + 1) * LANES)
        part = lambda k: buf[slot, k, pl.ds(j, n_tok, stride=sub), :]
        o_ref[:, cols] = h_ref[:, cols] + (part(0) * w1 + part(1) * w2)


def _combine(h, route, slots, ys):
    t, d = h.shape
    n, fields, tm = slots.shape
    sub = d // LANES
    mspec = lambda f: pl.BlockSpec((1, fields, tm), f, memory_space=pltpu.SMEM)
    return pl.pallas_call(
        functools.partial(_combine_kernel, tm, sub),
        grid=(n,),
        in_specs=[mspec(lambda i: (i, 0, 0)), mspec(lambda i: (jnp.minimum(i + 1, n - 1), 0, 0)),
                  pl.BlockSpec((tm, d), lambda i: (i, 0)), pl.BlockSpec((tm, LANES), lambda i: (i, 0)),
                  pl.BlockSpec(memory_space=pl.ANY)],
        out_specs=pl.BlockSpec((tm, d), lambda i: (i, 0)),
        out_shape=jax.ShapeDtypeStruct((t, d), F32),
        scratch_shapes=[pltpu.VMEM((2, TOP_K, tm * sub, LANES), F32), pltpu.SemaphoreType.DMA((2,))],
        compiler_params=_params("arbitrary"),
        name="moe_combine",
    )(slots, slots, h, route, ys)


def _moe(parts, counts, wg, wu, wd):
    tm = EXPERT_TILE
    n_assign = sum(part[0].shape[0] for part in parts) * TOP_K
    n_blocks = -(-(n_assign + N_EXPERTS * (tm - 1)) // tm)
    counts = counts[0, :N_EXPERTS].astype(I32)
    pcounts = (counts + tm - 1) // tm * tm
    pends = jnp.cumsum(pcounts)
    pstarts = pends - pcounts
    block_start = jnp.arange(n_blocks, dtype=I32) * tm
    block_e = jnp.minimum(jnp.sum((pends[None, :] <= block_start[:, None]).astype(I32), axis=1), N_EXPERTS - 1)
    n_used = pends[-1:] // tm
    ids = jnp.arange(N_EXPERTS, dtype=I32)
    later = (ids[None, :] > ids[:, None]) & (pcounts[None, :] > 0)
    nxt_of = jnp.min(jnp.where(later, ids[None, :], N_EXPERTS), axis=1)
    nxt_e = jnp.where(nxt_of < N_EXPERTS, nxt_of, -1)[block_e]
    xs = None
    slots = []
    for _, xn, _, meta in parts:
        assert meta.shape[-1] % ROW_UNROLL == 0
        e, pos = meta[:, :TOP_K], meta[:, TOP_K:]
        seg = jnp.sum(jnp.where(e[..., None] == jnp.arange(N_EXPERTS, dtype=I32), pstarts, 0), axis=-1)
        slots.append(seg + pos)
        xs = _scatter_rows(xn, slots[-1], pstarts, pends, xs, n_blocks * tm)
    ys = _experts(xs, block_e, n_used, nxt_e, wg, wu, wd)
    return [_combine(h, route, s, ys) for (h, _, route, _), s in zip(parts, slots)]


def kernel(x_prompt, x_sample, cache_swa_k, cache_swa_v, state_hgrn, meta_tokens, rel_bias_table, hgrn_lower_bounds, w_norm_mix, w_in, hgrn_out_norm, q_norm, k_norm, attn_sinks, w_branch, w_out, w_norm_ffn, w_router_group, b_router_group, w_router_expert, b_router_expert, w_expert_gate, w_expert_up, w_expert_down):
    b, s, d = x_prompt.shape
    bd, sd, _ = x_sample.shape
    depth, _, heads, dk, dv = state_hgrn.shape
    assert depth == 1 and heads == HGRN_HEADS and dk == dv
    hw = heads * dk
    aw = ATTN_HEADS * HEAD_DIM
    kvw = KV_HEADS * HEAD_DIM
    assert w_in.shape[-1] == 4 * hw + aw + 2 * kvw + 2 * d
    assert s % HGRN_CHUNK == 0 and s % CHUNK == 0 and sd == N_META and N_EXPERTS + N_GROUPS <= LANES
    l = 0

    p = jax.nn.softmax(hgrn_lower_bounds.astype(F32), axis=0)
    lb = jnp.cumsum(p, axis=0)[l + 1] - p[0]

    w_in_b = w_in[l].astype(BF16)
    proj = functools.partial(_inproj, w_norm=w_norm_mix[l], w_in_bf16=w_in_b, q_gain=q_norm[l], k_gain=k_norm[l],
                             hw=hw, aw=aw, kvw=kvw)
    x_small = jnp.concatenate([x_sample.reshape(bd * sd, d), meta_tokens.astype(F32)], axis=0)
    qr_s, z_s, hv_s, hg_s, qa_s, k_s, v_s, sga_s, sgb_s = proj(x_small)
    qr_p, z_p, hv_p, hg_p, qa_p, k_p, v_p, sga_p, sgb_p = proj(x_prompt.reshape(b * s, d))
    ns = bd * sd
    k_meta, v_meta = k_s[ns:], v_s[ns:]

    streams = lambda a, n: a.reshape(n, -1, a.shape[-1])
    s0_small = jnp.concatenate([state_hgrn[l].astype(F32), jnp.zeros((1, heads, dk, dv), F32)], axis=0)
    y_small, st_small = _hgrn(streams(qr_s, bd + 1), streams(z_s, bd + 1), streams(hv_s, bd + 1),
                              streams(hg_s, bd + 1), lb, hgrn_out_norm[l], s0_small, sd)
    y_p, st_p = _hgrn(streams(qr_p, b), streams(z_p, b), streams(hv_p, b), streams(hg_p, b), lb,
                      hgrn_out_norm[l], st_small, HGRN_CHUNK, shared_s0=bd)

    table = rel_bias_table.astype(F32)
    att_p, new_k_p, new_v_p = _attn_prompt(streams(qa_p, b), streams(k_p, b), streams(v_p, b), k_meta, v_meta,
                                           table, attn_sinks[l])
    k_new, v_new = k_s[:ns].reshape(bd, sd, kvw), v_s[:ns].reshape(bd, sd, kvw)
    att_s, new_k_s, new_v_s = _attn_sample(qa_s[:ns].reshape(bd, sd, aw), cache_swa_k[l:l + 1].astype(F32),
                                           cache_swa_v[l:l + 1].astype(F32), k_new, v_new, k_meta, v_meta,
                                           table, attn_sinks[l])

    wb = w_branch[l].astype(BF16)
    wo = w_out[l].astype(BF16)
    w_router = jnp.pad(jnp.concatenate([w_router_expert[l], w_router_group[l]], axis=1).astype(F32),
                       ((0, 0), (0, LANES - N_EXPERTS - N_GROUPS)))
    b_router = jnp.pad(jnp.concatenate([b_router_expert[l], b_router_group[l]]).astype(F32),
                       (0, LANES - N_EXPERTS - N_GROUPS)).reshape(1, LANES)
    w_router_hi = w_router.astype(BF16)
    w_router_lo = (w_router - w_router_hi.astype(F32)).astype(BF16)
    w_router = jnp.concatenate([w_router_hi, w_router_lo], axis=1)
    merge = functools.partial(_merge, wb=wb, wo=wo, w_norm=w_norm_ffn[l], w_router=w_router, b_router=b_router)
    *part_p, cnt_p = merge(x_prompt.reshape(b * s, d), y_p.reshape(b * s, hw), att_p.reshape(b * s, aw),
                           sga_p, sgb_p, cnt0=jnp.zeros((1, LANES), F32))
    *part_s, cnt_s = merge(x_sample.reshape(ns, d), y_small[:bd].reshape(ns, hw), att_s.reshape(ns, aw),
                           sga_s[:ns], sgb_s[:ns], cnt0=cnt_p)

    out_p, out_s = _moe([part_p, part_s], cnt_s, w_expert_gate[l], w_expert_up[l], w_expert_down[l])

    return (out_p.reshape(b, s, d), out_s.reshape(bd, sd, d), new_k_p, new_v_p, st_p[None],
            new_k_s, new_v_s, st_small[:bd][None])
```

```python
import functools
import math

import numpy as np
import jax
import jax.numpy as jnp
from jax import lax
from jax.experimental import pallas as pl
from jax.experimental.pallas import tpu as pltpu

F32 = jnp.float32
BF16 = jnp.bfloat16
I32 = jnp.int32

CHUNK = 64
N_META = 16
PAST_LEN = 2048
EPS = 1e-6
HGRN_HEADS = 4
ATTN_HEADS = 8
KV_HEADS = 2
HEAD_DIM = 64
GQA_GROUP = ATTN_HEADS // KV_HEADS
WINDOW = 128
WINDOW_CHUNKS = WINDOW // CHUNK
NUM_BUCKETS = 32
MAX_DISTANCE = 128
N_GROUPS = 4
EXPERTS_PER_GROUP = 8
N_EXPERTS = N_GROUPS * EXPERTS_PER_GROUP
TOP_K = 2

LANES = 128
MXU_WIDTH = 256
VMEM_LIMIT = 56 * 1024 * 1024

INPROJ_TILE = 512
PROJ_TILE = 512
HGRN_CHUNK = 128
HGRN_CHUNKS_PER_STEP = 8
ATTN_CHUNKS_PER_STEP = 16
SAMPLE_STREAMS_PER_STEP = 4
EXPERT_TILE = 512


LOG2E = math.log2(math.e)


def _sigmoid(x):
    return 1.0 / (1.0 + jnp.exp(-x))


def _split3(x):
    hi = x.astype(BF16)
    r1 = x - hi.astype(F32)
    mid = r1.astype(BF16)
    lo = (r1 - mid.astype(F32)).astype(BF16)
    return hi, mid, lo


def _dot(a, b):
    return jnp.dot(a, b, preferred_element_type=F32)


def _dot_nt(a, b):
    return lax.dot_general(a, b, (((1,), (1,)), ((), ())), preferred_element_type=F32)


def _dot_tn(a, b):
    return lax.dot_general(a, b, (((0,), (0,)), ((), ())), preferred_element_type=F32)


SUBLANES = 8


def _store_row_tiles(ref, x):
    n, d = x.shape
    sub = d // LANES
    for j in range(sub):
        ref[pl.ds(j, n, stride=sub), :] = x[:, j * LANES:(j + 1) * LANES]


def _load_row_tiles(ref, n, sub):
    return jnp.concatenate([ref[pl.ds(j, n, stride=sub), :] for j in range(sub)], axis=1)


def _params(*sem):
    return pltpu.CompilerParams(dimension_semantics=sem, vmem_limit_bytes=VMEM_LIMIT)


def _inproj_kernel(hw, aw, kvw, d, x_ref, wn_ref, w_ref, qg_ref, kg_ref, bdq_ref, bdk_ref,
                   qr_ref, z_ref, hv_ref, hg_ref, qa_ref, k_ref, v_ref, sga_ref, sgb_ref):
    x = x_ref[...]
    ms = jnp.mean(x * x, axis=-1, keepdims=True)
    xn = (x * lax.rsqrt(ms + EPS) * wn_ref[...]).astype(BF16)

    def seg(a, b):
        return _dot(xn, w_ref[:, a:b])

    def head_rms(a, bd_ref, gain):
        sq = (a * a).astype(BF16)
        wb = bd_ref.shape[0]
        m = jnp.concatenate([_dot(sq[:, c:c + wb], bd_ref[...]) for c in range(0, a.shape[1], wb)], axis=1)
        return a * lax.rsqrt(m + EPS) * gain

    o = 0
    hq = seg(o, o + hw)
    qr_ref[...] = (hq * _sigmoid(hq) * (hw // HGRN_HEADS) ** -0.5).astype(BF16)
    o += hw
    z_ref[...] = seg(o, o + hw)
    o += hw
    hv_ref[...] = seg(o, o + hw).astype(BF16)
    o += hw
    hg_ref[...] = seg(o, o + hw).astype(BF16)
    o += hw
    aq = seg(o, o + aw)
    qa_ref[...] = (head_rms(aq, bdq_ref, qg_ref[...]) * (HEAD_DIM ** -0.5 * LOG2E)).astype(BF16)
    o += aw
    k_ref[...] = head_rms(seg(o, o + kvw), bdk_ref, kg_ref[...])
    o += kvw
    v_ref[...] = seg(o, o + kvw)
    o += kvw
    sga_ref[...] = _sigmoid(seg(o, o + d)).astype(BF16)
    o += d
    sgb_ref[...] = _sigmoid(seg(o, o + d)).astype(BF16)


def _block_diag_mean(width, group):
    i = np.arange(width)
    return jnp.asarray((i[:, None] // group == i[None, :] // group) / group, dtype=BF16)


def _inproj(x, w_norm, w_in_bf16, q_gain, k_gain, hw, aw, kvw):
    t, d = x.shape
    tm = INPROJ_TILE if t % INPROJ_TILE == 0 else t
    cols = w_in_bf16.shape[1]
    row = lambda w: pl.BlockSpec((tm, w), lambda i: (i, 0))
    const = lambda a, b: pl.BlockSpec((a, b), lambda i: (0, 0))
    outs = [(hw, BF16), (hw, F32), (hw, BF16), (hw, BF16), (aw, BF16), (kvw, F32), (kvw, F32), (d, BF16), (d, BF16)]
    bdq = min(aw, MXU_WIDTH)
    assert aw % bdq == 0
    return pl.pallas_call(
        functools.partial(_inproj_kernel, hw, aw, kvw, d),
        grid=(t // tm,),
        in_specs=[row(d), const(1, d), const(d, cols), const(1, aw), const(1, kvw), const(bdq, bdq), const(kvw, kvw)],
        out_specs=[row(w) for w, _ in outs],
        out_shape=[jax.ShapeDtypeStruct((t, w), dt) for w, dt in outs],
        compiler_params=_params("arbitrary"),
        name="inproj",
    )(x, w_norm.reshape(1, d), w_in_bf16,
      jnp.tile(q_gain, aw // HEAD_DIM).reshape(1, aw), jnp.tile(k_gain, kvw // HEAD_DIM).reshape(1, kvw),
      _block_diag_mean(bdq, HEAD_DIM), _block_diag_mean(kvw, HEAD_DIM))


def _hgrn_consts(L):
    t = np.arange(L)
    u = t[None, :]
    blocks = [u <= t[:, None], u > t[:, None]]
    levels = []
    m = L // 2
    while m >= 1:
        levels.append(m)
        m //= 2
    lvl = np.full((L, L), -1, np.int32)
    lvl[t, t] = len(levels)
    isq_cols = []
    for j, m in enumerate(levels):
        bnd = (t // (2 * m)) * (2 * m) + m - 1
        isq = (t % (2 * m)) >= m
        cq = isq[:, None] & (u > bnd[:, None]) & (u <= t[:, None])
        ck = (~isq)[:, None] & (u > t[:, None]) & (u <= bnd[:, None])
        blocks.append(cq | ck)
        same = (t[:, None] // (2 * m)) == (t[None, :] // (2 * m))
        lvl[same & isq[:, None] & (~isq)[None, :]] = j
        isq_cols.append(isq)
    c = np.concatenate(blocks, axis=0).astype(np.float32)
    isq = np.stack(isq_cols, axis=1).astype(np.float32)
    isq = np.pad(isq, ((0, 0), (0, LANES - isq.shape[1])))
    c2 = np.tile(c, (1, 2))
    return jnp.asarray(c2, dtype=BF16), jnp.asarray(np.tile(lvl, (1, 2))), jnp.asarray(isq), len(levels)


def _hgrn_kernel(L, nlev, heads, dk, qr_ref, z_ref, hv_ref, hg_ref, lb_ref, og_ref, c_ref, lvl_ref, isq_ref,
                 s0_ref, y_ref, sout_ref, st_ref):
    c = pl.program_id(1)

    @pl.when(c == 0)
    def _():
        for h in range(heads):
            st_ref[h] = s0_ref[0, h].T

    for cc in range(z_ref.shape[1] // L):
        _hgrn_chunk(L, nlev, heads, dk, slice(cc * L, (cc + 1) * L), qr_ref, z_ref, hv_ref, hg_ref, lb_ref, og_ref,
                    c_ref, lvl_ref, isq_ref, y_ref, st_ref)

    @pl.when(c == pl.num_programs(1) - 1)
    def _():
        for h in range(heads):
            sout_ref[0, h] = st_ref[h].T


def _hgrn_chunk(L, nlev, heads, dk, rows, qr_ref, z_ref, hv_ref, hg_ref, lb_ref, og_ref, c_ref, lvl_ref, isq_ref,
                y_ref, st_ref):
    z = z_ref[0, rows, :]
    lb = lb_ref[...]
    e = jnp.exp(-jnp.abs(z))
    r = 1.0 / (1.0 + e)
    pos = z >= 0
    sig = jnp.where(pos, r, e * r)
    sig_neg = jnp.where(pos, e * r, r)
    logf = jnp.log(lb + (1.0 - lb) * sig)
    kin = (1.0 - lb) * sig_neg
    q = qr_ref[0, rows, :].astype(F32)

    hi, mid, _ = _split3(logf * LOG2E)
    ex = jnp.exp2(_dot(c_ref[...], jnp.concatenate([hi, mid], axis=0)))
    e_b = ex[0:L]
    e_rev = ex[L:2 * L]

    q_in = (q * e_b).astype(BF16)
    k_out = (kin * e_rev).astype(BF16)
    q_b = q.astype(BF16)
    k_b = kin.astype(BF16)
    xs = []
    for j in range(nlev):
        m = L >> (j + 1)
        if m % SUBLANES == 0:
            qk = jnp.concatenate([(q if blk % 2 else kin)[blk * m:(blk + 1) * m] for blk in range(L // m)], axis=0)
        else:
            qk = jnp.where(isq_ref[:, j:j + 1] > 0.5, q, kin)
        xs.append((qk * ex[(2 + j) * L:(3 + j) * L]).astype(BF16))
    lvl = lvl_ref[...]
    v = hv_ref[0, rows, :]
    g = hg_ref[0, rows, :].astype(F32)
    og = og_ref[...]

    def block_diag(x):
        zero = jnp.zeros((x.shape[0], dk), x.dtype)
        return jnp.concatenate([jnp.concatenate([x[:, :dk], zero], axis=1),
                                jnp.concatenate([zero, x[:, dk:]], axis=1)], axis=0)

    for pair in range(heads // 2):
        sl = slice(2 * pair * dk, 2 * (pair + 1) * dk)
        a = jnp.where(lvl == nlev, _dot_nt(q_b[:, sl], block_diag(k_b[:, sl])), 0.0)
        for j in range(nlev):
            xp = xs[j][:, sl]
            a = jnp.where(lvl == j, _dot_nt(xp, block_diag(xp)), a)
        st = jnp.concatenate([st_ref[2 * pair], st_ref[2 * pair + 1]], axis=1)
        vp = v[:, sl]
        o = _dot(a.astype(BF16), block_diag(vp)) + _dot_nt(q_in[:, sl], block_diag(st.astype(BF16)))
        for half in range(2):
            h = 2 * pair + half
            hs = slice(h * dk, (h + 1) * dk)
            st_ref[h] = st_ref[h] * e_b[L - 1:L, hs] + _dot_tn(v[:, hs], k_out[:, hs])
            oh = o[:, half * dk:(half + 1) * dk]
            ms = jnp.mean(oh * oh, axis=-1, keepdims=True)
            gh = g[:, hs]
            y_ref[0, rows, hs] = (oh * lax.rsqrt(ms + EPS) * og[:, hs] * (gh * _sigmoid(gh))).astype(BF16)


def _hgrn(qr, z, hv, hg, lb, out_gain, s0, L, shared_s0=None):
    b, s, w = z.shape
    heads, dk = s0.shape[1], s0.shape[2]
    cm, lvl, isq, nlev = _hgrn_consts(L)
    per_step = HGRN_CHUNKS_PER_STEP if s % (HGRN_CHUNKS_PER_STEP * L) == 0 else 1
    seq = pl.BlockSpec((1, per_step * L, w), lambda i, c: (i, c, 0))
    const = lambda a: pl.BlockSpec(a.shape, lambda i, c: (0,) * a.ndim)
    state = pl.BlockSpec((1, heads, dk, dk), lambda i, c: (i, 0, 0, 0))
    start = state if shared_s0 is None else pl.BlockSpec((1, heads, dk, dk), lambda i, c: (shared_s0, 0, 0, 0))
    lb2 = lb.reshape(1, w)
    og2 = jnp.tile(out_gain, heads).reshape(1, w)
    return pl.pallas_call(
        functools.partial(_hgrn_kernel, L, nlev, heads, dk),
        grid=(b, s // (per_step * L)),
        in_specs=[seq, seq, seq, seq, const(lb2), const(og2), const(cm), const(lvl), const(isq), start],
        out_specs=[seq, state],
        out_shape=[jax.ShapeDtypeStruct((b, s, w), BF16), jax.ShapeDtypeStruct((b,) + s0.shape[1:], F32)],
        scratch_shapes=[pltpu.VMEM((heads, dk, dk), F32)],
        compiler_params=_params("arbitrary", "arbitrary"),
        name=f"hgrn_scan_{L}",
    )(qr, z, hv, hg, lb2, og2, cm, lvl, isq, s0)


def _t5_bucket_np(rel):
    half = NUM_BUCKETS // 2
    max_exact = half // 2
    assert (NUM_BUCKETS, MAX_DISTANCE) == (32, 128)
    n = np.abs(rel).astype(np.int64)
    nn = np.maximum(n, 1)
    k = np.zeros_like(nn)
    for j in range(1, 48):
        k = np.where(64 * (1 << j) <= nn * nn, j, k)
    large = np.minimum(max_exact + k, half - 1)
    return np.where(rel > 0, half, 0) + np.where(n < max_exact, n, large)


HEADS_PER_COL = LANES // HEAD_DIM
COLS_PER_GROUP = GQA_GROUP // HEADS_PER_COL
HEAD_ORDER = tuple(g * GQA_GROUP + col * HEADS_PER_COL + half
                   for g in range(KV_HEADS) for half in range(HEADS_PER_COL) for col in range(COLS_PER_GROUP))
KV_EXPAND = KV_HEADS * HEADS_PER_COL


def _expand_kv(x, with_ones=False):
    assert HEADS_PER_COL == 2 and KV_HEADS == 2 and x.shape[1] == LANES
    low = lax.broadcasted_iota(I32, x.shape, 1) < HEAD_DIM
    xr = pltpu.roll(x, HEAD_DIM, axis=1)
    zero = jnp.zeros_like(x)
    blocks = [jnp.where(low, x, zero), jnp.where(low, zero, xr), jnp.where(low, xr, zero), jnp.where(low, zero, x)]
    if with_ones:
        blocks = [b for blk in blocks for b in (blk, jnp.ones_like(x))]
    return jnp.concatenate(blocks, axis=1).astype(BF16)


def _attn_core(q, kx, vx, bias):
    tq = q.shape[0]
    scores = []
    for g in range(KV_HEADS):
        cols = [q[:, (g * COLS_PER_GROUP + c) * LANES:(g * COLS_PER_GROUP + c + 1) * LANES]
                for c in range(COLS_PER_GROUP)]
        qst = jnp.concatenate(cols, axis=0)
        for half in range(HEADS_PER_COL):
            blk = g * HEADS_PER_COL + half
            scores.append(_dot_nt(qst, kx[:, blk * LANES:(blk + 1) * LANES]))
    s = jnp.concatenate(scores, axis=0) + bias
    pb = jnp.exp2(s - jnp.max(s, axis=-1, keepdims=True)).astype(BF16)
    rows = COLS_PER_GROUP * tq
    outs = []
    for g in range(KV_HEADS):
        o = None
        for half in range(HEADS_PER_COL):
            blk = g * HEADS_PER_COL + half
            pv = _dot(pb[blk * rows:(blk + 1) * rows], vx[:, 2 * blk * LANES:2 * (blk + 1) * LANES])
            part = pv[:, :LANES] * (1.0 / pv[:, LANES:])
            o = part if o is None else o + part
        outs.extend(o[c * tq:(c + 1) * tq] for c in range(COLS_PER_GROUP))
    return outs


def _bias_rows(table, bucket):
    onehot = (jnp.asarray(bucket)[..., None] == jnp.arange(NUM_BUCKETS)).astype(F32)
    cols = jnp.stack([table[:, h] for h in HEAD_ORDER], axis=1)
    bias = jnp.einsum('...qkb,bh->...hqk', onehot, cols, precision=lax.Precision.HIGHEST)
    return bias.reshape(*bucket.shape[:-2], ATTN_HEADS * bucket.shape[-2], bucket.shape[-1]) * LOG2E


def _pad_keys(bias, sinks):
    tk = bias.shape[-1]
    n_pad = -tk % LANES or LANES
    tq = bias.shape[-2] // ATTN_HEADS
    sink = jnp.repeat(jnp.stack([sinks[h] for h in HEAD_ORDER]).astype(F32), tq) * LOG2E
    sink = jnp.broadcast_to(sink[:, None], bias.shape[:-1] + (1,))
    masked = jnp.full(bias.shape[:-1] + (n_pad - 1,), -jnp.inf, F32)
    return jnp.concatenate([bias, sink, masked], axis=-1)


def _store_heads(ref, x, stream=0):
    for g in range(KV_HEADS):
        ref[0, stream, :, g, :] = x[:, g * HEAD_DIM:(g + 1) * HEAD_DIM]


def _load_heads(ref, stream=0):
    return jnp.concatenate([ref[0, stream, :, g, :] for g in range(KV_HEADS)], axis=1)


def _attn_prompt_kernel(cb, q_ref, k_ref, v_ref, km_ref, vm_ref, bias_ref, o_ref, nk_ref, nv_ref,
                        kx_ref, vx_ref):
    step = pl.program_id(1)
    s_len = k_ref.shape[1]
    meta_at = WINDOW + s_len

    @pl.when(step == pl.num_programs(1) - 1)
    def _():
        _store_heads(nk_ref, k_ref[0, s_len - WINDOW:, :])
        _store_heads(nv_ref, v_ref[0, s_len - WINDOW:, :])

    @pl.when(step == 0)
    def _():
        piece = min(s_len, 512)
        for src, meta, dst in ((k_ref, km_ref, kx_ref), (v_ref, vm_ref, vx_ref)):
            expand = functools.partial(_expand_kv, with_ones=dst is vx_ref)
            blank = lambda n: expand(jnp.zeros((n, src.shape[2]), F32))
            dst[0:WINDOW] = blank(WINDOW)
            for r in range(0, s_len, piece):
                dst[WINDOW + r:WINDOW + r + piece] = expand(src[0, r:r + piece, :])
            dst[meta_at:meta_at + N_META] = expand(meta[...])
            dst[meta_at + N_META:] = blank(dst.shape[0] - meta_at - N_META)

    win = WINDOW + CHUNK
    tail = kx_ref.shape[0] - meta_at
    for j in range(cb):
        c = step * cb + j
        start = pl.multiple_of(c * CHUNK, CHUNK)
        kall = jnp.concatenate([kx_ref[pl.ds(start, win), :], kx_ref[meta_at:meta_at + tail, :]], axis=0)
        vall = jnp.concatenate([vx_ref[pl.ds(start, win), :], vx_ref[meta_at:meta_at + tail, :]], axis=0)
        rows = slice(j * CHUNK, (j + 1) * CHUNK)
        outs = _attn_core(q_ref[0, rows, :], kall, vall, bias_ref[jnp.minimum(c, bias_ref.shape[0] - 1)])
        for ci, o in enumerate(outs):
            o_ref[0, rows, ci * LANES:(ci + 1) * LANES] = o.astype(o_ref.dtype)


def _attn_prompt(q, k, v, k_meta, v_meta, table, sinks):
    b, s, aw = q.shape
    kvw = k.shape[-1]
    nc = s // CHUNK
    cb = ATTN_CHUNKS_PER_STEP if nc % ATTN_CHUNKS_PER_STEP == 0 else 1
    assert s % min(s, 512) == 0
    n_bias = 1
    while True:
        qpos = N_META + (n_bias - 1) * CHUNK
        if np.all(_t5_bucket_np(np.arange(N_META) - qpos) == _t5_bucket_np(np.arange(N_META) - qpos - 10 ** 6)):
            break
        n_bias += 1
    n_bias = min(max(n_bias, WINDOW_CHUNKS + 1), nc)
    cs = np.arange(n_bias)[:, None]
    qpos = N_META + cs * CHUNK + np.arange(CHUNK)[None]
    wpos = N_META + (cs - WINDOW_CHUNKS) * CHUNK + np.arange(WINDOW + CHUNK)[None]
    kpos = np.concatenate([wpos, np.broadcast_to(np.arange(N_META), (n_bias, N_META))], axis=1)
    valid = np.concatenate([wpos >= N_META, np.ones((n_bias, N_META), bool)], axis=1)
    bias = _bias_rows(table, _t5_bucket_np(kpos[:, None, :] - qpos[:, :, None]))
    bias = _pad_keys(jnp.where(valid[:, None, :], bias, -jnp.inf), sinks)
    qs =pl.BlockSpec((1, cb * CHUNK, aw), lambda i, c: (i, c, 0))
    kv = pl.BlockSpec((1, s, kvw), lambda i, c: (i, 0, 0))
    meta = pl.BlockSpec((N_META, kvw), lambda i, c: (0, 0))
    xrows = WINDOW + s + bias.shape[-1] - (WINDOW + CHUNK)
    cache_shape = (1, b, WINDOW, KV_HEADS, HEAD_DIM)
    cache = pl.BlockSpec((1, 1) + cache_shape[2:], lambda i, c: (0, i, 0, 0, 0))
    return pl.pallas_call(
        functools.partial(_attn_prompt_kernel, cb),
        grid=(b, nc // cb),
        in_specs=[qs, kv, kv, meta, meta, pl.BlockSpec(bias.shape, lambda i, c: (0, 0, 0))],
        out_specs=[qs, cache, cache],
        out_shape=[jax.ShapeDtypeStruct((b, s, aw), BF16)] + [jax.ShapeDtypeStruct(cache_shape, F32)] * 2,
        scratch_shapes=[pltpu.VMEM((xrows, KV_EXPAND * LANES), BF16),
                        pltpu.VMEM((xrows, 2 * KV_EXPAND * LANES), BF16)],
        compiler_params=_params("arbitrary", "arbitrary"),
        name="attn_prompt",
    )(q, k, v, k_meta, v_meta, bias)


def _attn_sample_kernel(q_ref, kc_ref, vc_ref, kn_ref, vn_ref, km_ref, vm_ref, bias_ref, o_ref, nk_ref, nv_ref):
    tk = kc_ref.shape[2] + kn_ref.shape[1] + km_ref.shape[0]
    zeros = jnp.zeros((bias_ref.shape[1] - tk, km_ref.shape[1]), F32)
    for s in range(q_ref.shape[0]):
        kc, vc, kn, vn = _load_heads(kc_ref, s), _load_heads(vc_ref, s), kn_ref[s], vn_ref[s]
        kall = _expand_kv(jnp.concatenate([kc, kn, km_ref[...], zeros], axis=0))
        vall = _expand_kv(jnp.concatenate([vc, vn, vm_ref[...], zeros], axis=0), with_ones=True)
        outs = _attn_core(q_ref[s], kall, vall, bias_ref[...])
        for ci, o in enumerate(outs):
            o_ref[s, :, ci * LANES:(ci + 1) * LANES] = o.astype(o_ref.dtype)
        n_new = kn.shape[0]
        _store_heads(nk_ref, jnp.concatenate([kc[n_new:], kn], axis=0), s)
        _store_heads(nv_ref, jnp.concatenate([vc[n_new:], vn], axis=0), s)


def _attn_sample(q, k_cache, v_cache, k_new, v_new, k_meta, v_meta, table, sinks):
    bd, sd, aw = q.shape
    kvw = k_new.shape[-1]
    win = k_cache.shape[2]
    sb = SAMPLE_STREAMS_PER_STEP if bd % SAMPLE_STREAMS_PER_STEP == 0 else 1
    cache = pl.BlockSpec((1, sb) + k_cache.shape[2:], lambda i: (0, i, 0, 0, 0))
    qpos = N_META + PAST_LEN + np.arange(sd)
    kpos = np.concatenate([N_META + PAST_LEN - win + np.arange(win), qpos, np.arange(N_META)])
    bias = _pad_keys(_bias_rows(table, _t5_bucket_np(kpos[None, :] - qpos[:, None])), sinks)
    per = lambda n, w: pl.BlockSpec((sb, n, w), lambda i: (i, 0, 0))
    meta = pl.BlockSpec((N_META, kvw), lambda i: (0, 0))
    return pl.pallas_call(
        _attn_sample_kernel,
        grid=(bd // sb,),
        in_specs=[per(sd, aw), cache, cache, per(sd, kvw), per(sd, kvw), meta, meta,
                  pl.BlockSpec(bias.shape, lambda i: (0, 0))],
        out_specs=[per(sd, aw), cache, cache],
        out_shape=[jax.ShapeDtypeStruct((bd, sd, aw), BF16)] + [jax.ShapeDtypeStruct(k_cache.shape, F32)] * 2,
        compiler_params=_params("arbitrary"),
        name="attn_sample",
    )(q, k_cache, v_cache, k_new, v_new, k_meta, v_meta, bias)


ROUTE_E1, ROUTE_E2, ROUTE_R1, ROUTE_R2, ROUTE_W1, ROUTE_W2 = range(6)
META_FIELDS = 2 * TOP_K


def _merge_kernel(hw, x_ref, yr_ref, at_ref, sga_ref, sgb_ref, wb_ref, wo_ref, wn_ref, wr_ref, br_ref, cnt0_ref, tri_ref,
                  h_ref, xn_ref, route_ref, meta_ref, cnt_ref, carry_ref):
    @pl.when(pl.program_id(0) == 0)
    def _():
        carry_ref[...] = cnt0_ref[...]

    logits = _project_rows(hw, x_ref, yr_ref, at_ref, sga_ref, sgb_ref, wb_ref, wo_ref, wn_ref, wr_ref, br_ref,
                           h_ref, xn_ref)
    _route_rows(logits, tri_ref, route_ref, meta_ref, carry_ref)
    cnt_ref[...] = carry_ref[...]


def _project_rows(hw, x_ref, yr_ref, at_ref, sga_ref, sgb_ref, wb_ref, wo_ref, wn_ref, wr_ref, br_ref, h_ref, xn_ref):
    br = _dot(yr_ref[...], wb_ref[0:hw, :])
    ba = _dot(at_ref[...], wb_ref[hw:, :])
    merged = sga_ref[...].astype(F32) * br + sgb_ref[...].astype(F32) * ba
    h = x_ref[...] + _dot(merged.astype(BF16), wo_ref[...])
    h_ref[...] = h
    ms = jnp.mean(h * h, axis=-1, keepdims=True)
    xn = h * lax.rsqrt(ms + EPS) * wn_ref[...]
    _store_row_tiles(xn_ref, xn)

    x_hi = xn.astype(BF16)
    x_lo = (xn - x_hi.astype(F32)).astype(BF16)
    both = _dot(x_hi, wr_ref[...])
    return both[:, :LANES] + both[:, LANES:] + _dot(x_lo, wr_ref[:, :LANES]) + br_ref[...]


def _route_rows(logits, tri_ref, route_ref, meta_ref, carry_ref):
    tm = logits.shape[0]
    lane_i = lax.broadcasted_iota(I32, (tm, LANES), 1)
    lane = lane_i.astype(F32)
    group_of_lane = (lane_i >> int(math.log2(EXPERTS_PER_GROUP))).astype(F32)
    ninf = -jnp.inf
    first = lambda hit, idx: jnp.min(jnp.where(hit, idx, float(LANES)), axis=-1, keepdims=True)
    gmask = (lane_i >= N_EXPERTS) & (lane_i < N_EXPERTS + N_GROUPS)
    gl = jnp.where(gmask, logits, ninf)
    gmax = jnp.max(gl, axis=-1, keepdims=True)
    gidx = first(gl == gmax, lane - N_EXPERTS)
    gval = 1.0 / jnp.sum(jnp.exp(gl - gmax), axis=-1, keepdims=True)
    emask = (lane_i < N_EXPERTS) & (group_of_lane == gidx)
    el = jnp.where(emask, logits, ninf)
    m1 = jnp.max(el, axis=-1, keepdims=True)
    i1 = first(el == m1, lane)
    el2 = jnp.where(lane == i1, ninf, el)
    m2 = jnp.max(el2, axis=-1, keepdims=True)
    i2 = first(el2 == m2, lane)
    e21 = jnp.exp(m2 - m1)
    w1 = gval / (1.0 + e21)
    w2 = gval * e21 / (1.0 + e21)

    sel1 = lane == i1
    sel2 = lane == i2
    oh = (sel1 | sel2).astype(BF16)
    before = _dot(tri_ref[...], oh) + carry_ref[...]
    r1 = jnp.sum(jnp.where(sel1, before, 0.0), axis=-1, keepdims=True)
    r2 = jnp.sum(jnp.where(sel2, before, 0.0), axis=-1, keepdims=True)
    carry_ref[...] = carry_ref[...] + jnp.sum(oh.astype(F32), axis=0, keepdims=True)

    rec = jnp.zeros((tm, LANES), F32)
    for slot, val in ((ROUTE_E1, i1), (ROUTE_E2, i2), (ROUTE_W1, w1), (ROUTE_W2, w2), (ROUTE_R1, r1), (ROUTE_R2, r2)):
        rec = jnp.where(lane_i == slot, val, rec)
    route_ref[...] = rec
    meta_ref[0] = rec.T[ROUTE_E1:ROUTE_E1 + META_FIELDS].astype(I32)


def _merge(x, y_rec, att, sga, sgb, wb, wo, w_norm, w_router, b_router, cnt0):
    t, d = x.shape
    hw = y_rec.shape[1]
    tm = PROJ_TILE if t % PROJ_TILE == 0 else t
    row = lambda w: pl.BlockSpec((tm, w), lambda i: (i, 0))
    const = lambda a, b: pl.BlockSpec((a, b), lambda i: (0, 0))
    return pl.pallas_call(
        functools.partial(_merge_kernel, hw),
        grid=(t // tm,),
        in_specs=[row(d), row(hw), row(att.shape[1]), row(d), row(d), const(*wb.shape), const(d, d), const(1, d),
                  const(d, 2 * LANES), const(1, LANES), const(1, LANES), const(tm, tm)],
        out_specs=[row(d), pl.BlockSpec((tm * d // LANES, LANES), lambda i: (i, 0)), row(LANES),
                   pl.BlockSpec((1, META_FIELDS, tm), lambda i: (i, 0, 0)), const(1, LANES)],
        out_shape=[jax.ShapeDtypeStruct((t, d), F32), jax.ShapeDtypeStruct((t * d // LANES, LANES), F32),
                   jax.ShapeDtypeStruct((t, LANES), F32), jax.ShapeDtypeStruct((t // tm, META_FIELDS, tm), I32),
                   jax.ShapeDtypeStruct((1, LANES), F32)],
        scratch_shapes=[pltpu.VMEM((1, LANES), F32)],
        compiler_params=_params("arbitrary"),
        name="merge_route",
    )(x, y_rec, att, sga, sgb, wb, wo, w_norm.reshape(1, d), w_router, b_router, cnt0,
      jnp.asarray(np.tri(tm, k=-1), dtype=BF16))


def _row_copy(src, dst, sem):
    return pltpu.make_async_copy(src, dst, sem)


ROW_UNROLL = 8


def _tile_rows(ref, row, sub):
    return ref.at[pl.ds(pl.multiple_of(row * sub, sub), sub)]


def _scatter_kernel(n_tok, sub, first, slot_ref, pstart_ref, pend_ref, x_ref, *rest):
    xs_ref, zero_ref, sem, zsem, tsem = rest[-5:]

    def zero_blocks(blocks, zs, wait):
        for cond, row in blocks:
            @pl.when(cond)
            def _():
                at = row * sub if isinstance(row, int) else pl.multiple_of(row * sub, EXPERT_TILE * sub)
                cp = _row_copy(zero_ref, xs_ref.at[pl.ds(at, EXPERT_TILE * sub)], zs)
                cp.wait() if wait else cp.start()

    if first:
        n_rows = xs_ref.shape[0] // sub
        tails = [(pend_ref[e] > pstart_ref[e], pend_ref[e] - EXPERT_TILE) for e in range(N_EXPERTS)]
        unused = [(n_rows - (j + 1) * EXPERT_TILE >= pend_ref[N_EXPERTS - 1], n_rows - (j + 1) * EXPERT_TILE)
                  for j in range(N_EXPERTS)]

        @pl.when(pl.program_id(0) == 0)
        def _():
            zero_ref[...] = jnp.zeros_like(zero_ref)
            zero_blocks(tails, zsem, False)
            zero_blocks(unused, tsem, False)
            zero_blocks(tails, zsem, True)

    def issue(grp, _):
        for u in range(ROW_UNROLL):
            r = grp * ROW_UNROLL + u
            for k in range(TOP_K):
                _row_copy(_tile_rows(x_ref, r, sub), _tile_rows(xs_ref, slot_ref[0, k, r], sub),
                          sem).start(priority=(u * TOP_K + k) % 2)
        return 0

    lax.fori_loop(0, n_tok // ROW_UNROLL, issue, 0)
    for k in range(TOP_K):
        _row_copy(x_ref, xs_ref.at[pl.ds(0, n_tok * sub)], sem).wait()

    if first:
        @pl.when(pl.program_id(0) == 0)
        def _():
            zero_blocks(unused, tsem, True)


def _scatter_rows(x, slots, pstarts, pends, xs, rows):
    n, fields, tm = slots.shape
    sub = x.shape[0] // (n * tm)
    first = xs is None
    smem = pl.BlockSpec(memory_space=pltpu.SMEM)
    return pl.pallas_call(
        functools.partial(_scatter_kernel, tm, sub, first),
        grid=(n,),
        in_specs=[pl.BlockSpec((1, fields, tm), lambda i: (i, 0, 0), memory_space=pltpu.SMEM), smem, smem,
                  pl.BlockSpec((tm * sub, LANES), lambda i: (i, 0))]
        + ([] if first else [pl.BlockSpec(memory_space=pl.ANY)]),
        out_specs=pl.BlockSpec(memory_space=pl.ANY),
        out_shape=jax.ShapeDtypeStruct((rows * sub, LANES), F32),
        scratch_shapes=[pltpu.VMEM((EXPERT_TILE * sub, LANES), F32)] + [pltpu.SemaphoreType.DMA(())] * 3,
        input_output_aliases={} if first else {4: 0},
        compiler_params=_params("arbitrary"),
        name="moe_scatter",
    )(slots, pstarts, pends, x, *([] if first else [xs]))


def _expert_kernel(tm, sub, be_ref, nb_ref, nxt_ref, x_ref, wg_ref, wu_ref, wd_ref, y_ref,
                   raw_g, raw_u, raw_d, wgb_ref, wub_ref, wdb_ref, sem, slot_ref):
    i = pl.program_id(0)
    used = i < nb_ref[0]
    first = used & ((i == 0) | (be_ref[i] != be_ref[jnp.maximum(i - 1, 0)]))

    def fetch(e, slot):
        return [_row_copy(w.at[e], raw.at[slot], sem.at[slot])
                for w, raw in ((wg_ref, raw_g), (wu_ref, raw_u), (wd_ref, raw_d))]

    @pl.when(i == 0)
    def _():
        slot_ref[0] = 0
        for cp in fetch(be_ref[0], 0):
            cp.start()

    @pl.when(first)
    def _():
        slot = slot_ref[0]
        for cp in fetch(be_ref[i], slot):
            cp.wait()
        wgb_ref[...] = raw_g[slot].astype(BF16)
        wub_ref[...] = raw_u[slot].astype(BF16)
        wdb_ref[...] = raw_d[slot].astype(BF16)
        slot_ref[0] = 1 - slot

        @pl.when(nxt_ref[i] >= 0)
        def _():
            for cp in fetch(nxt_ref[i], 1 - slot):
                cp.start()

    @pl.when(used)
    def _():
        x = _load_row_tiles(x_ref, tm, sub).astype(BF16)
        g = _dot(x, wgb_ref[...])
        u = _dot(x, wub_ref[...])
        hmid = (g * _sigmoid(g) * u).astype(BF16)
        _store_row_tiles(y_ref, _dot(hmid, wdb_ref[...]))

    @pl.when(jnp.logical_not(used))
    def _():
        y_ref[...] = jnp.zeros_like(y_ref)


def _experts(xs, block_e, n_used, nxt_e, wg, wu, wd):
    _, d, ff = wg.shape
    sub = d // LANES
    tm = EXPERT_TILE
    blk = pl.BlockSpec((tm * sub, LANES), lambda i, be, nb, nx: (i, 0))
    hbm = pl.BlockSpec(memory_space=pl.ANY)
    grid_spec = pltpu.PrefetchScalarGridSpec(
        num_scalar_prefetch=3,
        grid=(xs.shape[0] // (tm * sub),),
        in_specs=[blk, hbm, hbm, hbm],
        out_specs=blk,
        scratch_shapes=[pltpu.VMEM((2, d, ff), F32), pltpu.VMEM((2, d, ff), F32), pltpu.VMEM((2, ff, d), F32),
                        pltpu.VMEM((d, ff), BF16), pltpu.VMEM((d, ff), BF16), pltpu.VMEM((ff, d), BF16),
                        pltpu.SemaphoreType.DMA((2,)), pltpu.SMEM((1,), I32)],
    )
    return pl.pallas_call(
        functools.partial(_expert_kernel, tm, sub),
        grid_spec=grid_spec,
        out_shape=jax.ShapeDtypeStruct(xs.shape, F32),
        compiler_params=_params("arbitrary"),
        name="moe_experts",
    )(block_e, n_used, nxt_e, xs, wg, wu, wd)


def _combine_kernel(n_tok, sub, scur_ref, snext_ref, h_ref, route_ref, ys_ref, o_ref, buf, sem):
    i = pl.program_id(0)
    n = pl.num_programs(0)
    slot = i % 2

    def start(slot_ref, s):
        def issue(grp, _):
            for u in range(ROW_UNROLL):
                r = grp * ROW_UNROLL + u
                for k in range(TOP_K):
                    _row_copy(_tile_rows(ys_ref, slot_ref[0, k, r], sub), _tile_rows(buf.at[s, k], r, sub),
                              sem.at[s]).start(priority=(u * TOP_K + k) % 2)
            return 0
        lax.fori_loop(0, n_tok // ROW_UNROLL, issue, 0)

    @pl.when(i == 0)
    def _():
        start(scur_ref, 0)

    @pl.when(i + 1 < n)
    def _():
        start(snext_ref, 1 - slot)

    for k in range(TOP_K):
        _row_copy(ys_ref.at[pl.ds(0, n_tok * sub)], buf.at[slot, k], sem.at[slot]).wait()
    route = route_ref[...]
    w1 = route[:, ROUTE_W1:ROUTE_W1 + 1]
    w2 = route[:, ROUTE_W2:ROUTE_W2 + 1]
    for j in range(sub):
        cols = slice(j * LANES, (j + 1) * LANES)
        part = lambda k: buf[slot, k, pl.ds(j, n_tok, stride=sub), :]
        o_ref[:, cols] = h_ref[:, cols] + (part(0) * w1 + part(1) * w2)


def _combine(h, route, slots, ys):
    t, d = h.shape
    n, fields, tm = slots.shape
    sub = d // LANES
    mspec = lambda f: pl.BlockSpec((1, fields, tm), f, memory_space=pltpu.SMEM)
    return pl.pallas_call(
        functools.partial(_combine_kernel, tm, sub),
        grid=(n,),
        in_specs=[mspec(lambda i: (i, 0, 0)), mspec(lambda i: (jnp.minimum(i + 1, n - 1), 0, 0)),
                  pl.BlockSpec((tm, d), lambda i: (i, 0)), pl.BlockSpec((tm, LANES), lambda i: (i, 0)),
                  pl.BlockSpec(memory_space=pl.ANY)],
        out_specs=pl.BlockSpec((tm, d), lambda i: (i, 0)),
        out_shape=jax.ShapeDtypeStruct((t, d), F32),
        scratch_shapes=[pltpu.VMEM((2, TOP_K, tm * sub, LANES), F32), pltpu.SemaphoreType.DMA((2,))],
        compiler_params=_params("arbitrary"),
        name="moe_combine",
    )(slots, slots, h, route, ys)


def _moe(parts, counts, wg, wu, wd):
    tm = EXPERT_TILE
    n_assign = sum(part[0].shape[0] for part in parts) * TOP_K
    n_blocks = -(-(n_assign + N_EXPERTS * (tm - 1)) // tm)
    counts = counts[0, :N_EXPERTS].astype(I32)
    pcounts = (counts + tm - 1) // tm * tm
    pends = jnp.cumsum(pcounts)
    pstarts = pends - pcounts
    block_start = jnp.arange(n_blocks, dtype=I32) * tm
    block_e = jnp.minimum(jnp.sum((pends[None, :] <= block_start[:, None]).astype(I32), axis=1), N_EXPERTS - 1)
    n_used = pends[-1:] // tm
    ids = jnp.arange(N_EXPERTS, dtype=I32)
    later = (ids[None, :] > ids[:, None]) & (pcounts[None, :] > 0)
    nxt_of = jnp.min(jnp.where(later, ids[None, :], N_EXPERTS), axis=1)
    nxt_of = jnp.where(nxt_of < N_EXPERTS, nxt_of, -1)
    nxt_e = jnp.sum(jnp.where(block_e[:, None] == ids[None, :], nxt_of[None, :], 0), axis=1)
    xs = None
    slots = []
    for _, xn, _, meta in parts:
        assert meta.shape[-1] % ROW_UNROLL == 0
        e, pos = meta[:, :TOP_K], meta[:, TOP_K:]
        seg = jnp.sum(jnp.where(e[..., None] == jnp.arange(N_EXPERTS, dtype=I32), pstarts, 0), axis=-1)
        slots.append(seg + pos)
        xs = _scatter_rows(xn, slots[-1], pstarts, pends, xs, n_blocks * tm)
    ys = _experts(xs, block_e, n_used, nxt_e, wg, wu, wd)
    return [_combine(h, route, s, ys) for (h, _, route, _), s in zip(parts, slots)]


def kernel(x_prompt, x_sample, cache_swa_k, cache_swa_v, state_hgrn, meta_tokens, rel_bias_table, hgrn_lower_bounds, w_norm_mix, w_in, hgrn_out_norm, q_norm, k_norm, attn_sinks, w_branch, w_out, w_norm_ffn, w_router_group, b_router_group, w_router_expert, b_router_expert, w_expert_gate, w_expert_up, w_expert_down):
    b, s, d = x_prompt.shape
    bd, sd, _ = x_sample.shape
    depth, _, heads, dk, dv = state_hgrn.shape
    assert depth == 1 and heads == HGRN_HEADS and dk == dv
    hw = heads * dk
    aw = ATTN_HEADS * HEAD_DIM
    kvw = KV_HEADS * HEAD_DIM
    assert w_in.shape[-1] == 4 * hw + aw + 2 * kvw + 2 * d
    assert s % HGRN_CHUNK == 0 and s % CHUNK == 0 and sd == N_META and N_EXPERTS + N_GROUPS <= LANES
    l = 0

    p = jax.nn.softmax(hgrn_lower_bounds.astype(F32), axis=0)
    lb = jnp.cumsum(p, axis=0)[l + 1] - p[0]

    w_in_b = w_in[l].astype(BF16)
    proj = functools.partial(_inproj, w_norm=w_norm_mix[l], w_in_bf16=w_in_b, q_gain=q_norm[l], k_gain=k_norm[l],
                             hw=hw, aw=aw, kvw=kvw)
    x_small = jnp.concatenate([x_sample.reshape(bd * sd, d), meta_tokens.astype(F32)], axis=0)
    qr_s, z_s, hv_s, hg_s, qa_s, k_s, v_s, sga_s, sgb_s = proj(x_small)
    qr_p, z_p, hv_p, hg_p, qa_p, k_p, v_p, sga_p, sgb_p = proj(x_prompt.reshape(b * s, d))
    ns = bd * sd
    k_meta, v_meta = k_s[ns:], v_s[ns:]

    streams = lambda a, n: a.reshape(n, -1, a.shape[-1])
    s0_small = jnp.concatenate([state_hgrn[l].astype(F32), jnp.zeros((1, heads, dk, dv), F32)], axis=0)
    y_small, st_small = _hgrn(streams(qr_s, bd + 1), streams(z_s, bd + 1), streams(hv_s, bd + 1),
                              streams(hg_s, bd + 1), lb, hgrn_out_norm[l], s0_small, sd)
    y_p, st_p = _hgrn(streams(qr_p, b), streams(z_p, b), streams(hv_p, b), streams(hg_p, b), lb,
                      hgrn_out_norm[l], st_small, HGRN_CHUNK, shared_s0=bd)

    table = rel_bias_table.astype(F32)
    att_p, new_k_p, new_v_p = _attn_prompt(streams(qa_p, b), streams(k_p, b), streams(v_p, b), k_meta, v_meta,
                                           table, attn_sinks[l])
    k_new, v_new = k_s[:ns].reshape(bd, sd, kvw), v_s[:ns].reshape(bd, sd, kvw)
    att_s, new_k_s, new_v_s = _attn_sample(qa_s[:ns].reshape(bd, sd, aw), cache_swa_k[l:l + 1].astype(F32),
                                           cache_swa_v[l:l + 1].astype(F32), k_new, v_new, k_meta, v_meta,
                                           table, attn_sinks[l])

    wb = w_branch[l].astype(BF16)
    wo = w_out[l].astype(BF16)
    w_router = jnp.pad(jnp.concatenate([w_router_expert[l], w_router_group[l]], axis=1).astype(F32),
                       ((0, 0), (0, LANES - N_EXPERTS - N_GROUPS)))
    b_router = jnp.pad(jnp.concatenate([b_router_expert[l], b_router_group[l]]).astype(F32),
                       (0, LANES - N_EXPERTS - N_GROUPS)).reshape(1, LANES)
    w_router_hi = w_router.astype(BF16)
    w_router_lo = (w_router - w_router_hi.astype(F32)).astype(BF16)
    w_router = jnp.concatenate([w_router_hi, w_router_lo], axis=1)
    merge = functools.partial(_merge, wb=wb, wo=wo, w_norm=w_norm_ffn[l], w_router=w_router, b_router=b_router)
    *part_p, cnt_p = merge(x_prompt.reshape(b * s, d), y_p.reshape(b * s, hw), att_p.reshape(b * s, aw),
                           sga_p, sgb_p, cnt0=jnp.zeros((1, LANES), F32))
    *part_s, cnt_s = merge(x_sample.reshape(ns, d), y_small[:bd].reshape(ns, hw), att_s.reshape(ns, aw),
                           sga_s[:ns], sgb_s[:ns], cnt0=cnt_p)

    out_p, out_s = _moe([part_p, part_s], cnt_s, w_expert_gate[l], w_expert_up[l], w_expert_down[l])

    return (out_p.reshape(b, s, d), out_s.reshape(bd, sd, d), new_k_p, new_v_p, st_p[None],
            new_k_s, new_v_s, st_small[:bd][None])
```

```python
import functools
import math

import numpy as np
import jax
import jax.numpy as jnp
from jax import lax
from jax.experimental import pallas as pl
from jax.experimental.pallas import tpu as pltpu

F32 = jnp.float32
BF16 = jnp.bfloat16
I32 = jnp.int32

CHUNK = 64
N_META = 16
PAST_LEN = 2048
EPS = 1e-6
HGRN_HEADS = 4
ATTN_HEADS = 8
KV_HEADS = 2
HEAD_DIM = 64
GQA_GROUP = ATTN_HEADS // KV_HEADS
WINDOW = 128
WINDOW_CHUNKS = WINDOW // CHUNK
NUM_BUCKETS = 32
MAX_DISTANCE = 128
N_GROUPS = 4
EXPERTS_PER_GROUP = 8
N_EXPERTS = N_GROUPS * EXPERTS_PER_GROUP
TOP_K = 2

LANES = 128
MXU_WIDTH = 256
VMEM_LIMIT = 56 * 1024 * 1024

INPROJ_TILE = 512
PROJ_TILE = 512
HGRN_CHUNK = 128
HGRN_CHUNKS_PER_STEP = 8
ATTN_CHUNKS_PER_STEP = 16
SAMPLE_STREAMS_PER_STEP = 4
EXPERT_TILE = 512


LOG2E = math.log2(math.e)


def _sigmoid(x):
    return 1.0 / (1.0 + jnp.exp(-x))


def _split3(x):
    hi = x.astype(BF16)
    r1 = x - hi.astype(F32)
    mid = r1.astype(BF16)
    lo = (r1 - mid.astype(F32)).astype(BF16)
    return hi, mid, lo


def _dot(a, b):
    return jnp.dot(a, b, preferred_element_type=F32)


def _dot_nt(a, b):
    return lax.dot_general(a, b, (((1,), (1,)), ((), ())), preferred_element_type=F32)


def _dot_tn(a, b):
    return lax.dot_general(a, b, (((0,), (0,)), ((), ())), preferred_element_type=F32)


SUBLANES = 8


def _store_row_tiles(ref, x):
    n, d = x.shape
    sub = d // LANES
    for j in range(sub):
        ref[pl.ds(j, n, stride=sub), :] = x[:, j * LANES:(j + 1) * LANES]


def _load_row_tiles(ref, n, sub):
    return jnp.concatenate([ref[pl.ds(j, n, stride=sub), :] for j in range(sub)], axis=1)


def _params(*sem):
    return pltpu.CompilerParams(dimension_semantics=sem, vmem_limit_bytes=VMEM_LIMIT)


def _inproj_kernel(hw, aw, kvw, d, x_ref, wn_ref, w_ref, qg_ref, kg_ref, bdq_ref, bdk_ref,
                   qr_ref, z_ref, hv_ref, hg_ref, qa_ref, k_ref, v_ref, sga_ref, sgb_ref):
    x = x_ref[...]
    ms = jnp.mean(x * x, axis=-1, keepdims=True)
    xn = (x * lax.rsqrt(ms + EPS) * wn_ref[...]).astype(BF16)

    def seg(a, b):
        return _dot(xn, w_ref[:, a:b])

    def head_rms(a, bd_ref, gain):
        sq = (a * a).astype(BF16)
        wb = bd_ref.shape[0]
        m = jnp.concatenate([_dot(sq[:, c:c + wb], bd_ref[...]) for c in range(0, a.shape[1], wb)], axis=1)
        return a * lax.rsqrt(m + EPS) * gain

    o = 0
    hq = seg(o, o + hw)
    qr_ref[...] = (hq * _sigmoid(hq) * (hw // HGRN_HEADS) ** -0.5).astype(BF16)
    o += hw
    z_ref[...] = seg(o, o + hw)
    o += hw
    hv_ref[...] = seg(o, o + hw).astype(BF16)
    o += hw
    hg_ref[...] = seg(o, o + hw).astype(BF16)
    o += hw
    aq = seg(o, o + aw)
    qa_ref[...] = (head_rms(aq, bdq_ref, qg_ref[...]) * (HEAD_DIM ** -0.5 * LOG2E)).astype(BF16)
    o += aw
    kv = seg(o, o + 2 * kvw)
    k_ref[...] = head_rms(kv[:, :kvw], bdk_ref, kg_ref[...])
    v_ref[...] = kv[:, kvw:]
    o += 2 * kvw
    sga_ref[...] = _sigmoid(seg(o, o + d)).astype(BF16)
    o += d
    sgb_ref[...] = _sigmoid(seg(o, o + d)).astype(BF16)


def _block_diag_mean(width, group):
    i = np.arange(width)
    return jnp.asarray((i[:, None] // group == i[None, :] // group) / group, dtype=BF16)


def _inproj(x, w_norm, w_in_bf16, q_gain, k_gain, hw, aw, kvw):
    t, d = x.shape
    tm = INPROJ_TILE if t % INPROJ_TILE == 0 else t
    cols = w_in_bf16.shape[1]
    row = lambda w: pl.BlockSpec((tm, w), lambda i: (i, 0))
    const = lambda a, b: pl.BlockSpec((a, b), lambda i: (0, 0))
    outs = [(hw, BF16), (hw, F32), (hw, BF16), (hw, BF16), (aw, BF16), (kvw, F32), (kvw, F32), (d, BF16), (d, BF16)]
    bdq = min(aw, MXU_WIDTH)
    assert aw % bdq == 0
    return pl.pallas_call(
        functools.partial(_inproj_kernel, hw, aw, kvw, d),
        grid=(t // tm,),
        in_specs=[row(d), const(1, d), const(d, cols), const(1, aw), const(1, kvw), const(bdq, bdq), const(kvw, kvw)],
        out_specs=[row(w) for w, _ in outs],
        out_shape=[jax.ShapeDtypeStruct((t, w), dt) for w, dt in outs],
        compiler_params=_params("arbitrary"),
        name="inproj",
    )(x, w_norm.reshape(1, d), w_in_bf16,
      jnp.tile(q_gain, aw // HEAD_DIM).reshape(1, aw), jnp.tile(k_gain, kvw // HEAD_DIM).reshape(1, kvw),
      _block_diag_mean(bdq, HEAD_DIM), _block_diag_mean(kvw, HEAD_DIM))


def _hgrn_consts(L):
    t = np.arange(L)
    u = t[None, :]
    blocks = [u <= t[:, None], u > t[:, None]]
    levels = []
    m = L // 2
    while m >= 1:
        levels.append(m)
        m //= 2
    lvl = np.full((L, L), -1, np.int32)
    lvl[t, t] = len(levels)
    isq_cols = []
    for j, m in enumerate(levels):
        bnd = (t // (2 * m)) * (2 * m) + m - 1
        isq = (t % (2 * m)) >= m
        cq = isq[:, None] & (u > bnd[:, None]) & (u <= t[:, None])
        ck = (~isq)[:, None] & (u > t[:, None]) & (u <= bnd[:, None])
        blocks.append(cq | ck)
        same = (t[:, None] // (2 * m)) == (t[None, :] // (2 * m))
        lvl[same & isq[:, None] & (~isq)[None, :]] = j
        isq_cols.append(isq)
    c = np.concatenate(blocks, axis=0).astype(np.float32)
    isq = np.stack(isq_cols, axis=1).astype(np.float32)
    isq = np.pad(isq, ((0, 0), (0, LANES - isq.shape[1])))
    c2 = np.tile(c, (1, 2))
    return jnp.asarray(c2, dtype=BF16), jnp.asarray(np.tile(lvl, (1, 2))), jnp.asarray(isq), len(levels)


def _hgrn_kernel(L, nlev, heads, dk, qr_ref, z_ref, hv_ref, hg_ref, lb_ref, og_ref, c_ref, lvl_ref, isq_ref,
                 s0_ref, y_ref, sout_ref, st_ref):
    c = pl.program_id(1)

    @pl.when(c == 0)
    def _():
        for h in range(heads):
            st_ref[h] = s0_ref[0, h].T

    for cc in range(z_ref.shape[1] // L):
        _hgrn_chunk(L, nlev, heads, dk, slice(cc * L, (cc + 1) * L), qr_ref, z_ref, hv_ref, hg_ref, lb_ref, og_ref,
                    c_ref, lvl_ref, isq_ref, y_ref, st_ref)

    @pl.when(c == pl.num_programs(1) - 1)
    def _():
        for h in range(heads):
            sout_ref[0, h] = st_ref[h].T


def _hgrn_chunk(L, nlev, heads, dk, rows, qr_ref, z_ref, hv_ref, hg_ref, lb_ref, og_ref, c_ref, lvl_ref, isq_ref,
                y_ref, st_ref):
    z = z_ref[0, rows, :]
    lb = lb_ref[...]
    e = jnp.exp(-jnp.abs(z))
    r = 1.0 / (1.0 + e)
    pos = z >= 0
    sig = jnp.where(pos, r, e * r)
    sig_neg = jnp.where(pos, e * r, r)
    logf = jnp.log(lb + (1.0 - lb) * sig)
    kin = (1.0 - lb) * sig_neg
    q = qr_ref[0, rows, :].astype(F32)

    hi, mid, _ = _split3(logf * LOG2E)
    ex = jnp.exp2(_dot(c_ref[...], jnp.concatenate([hi, mid], axis=0)))
    e_b = ex[0:L]
    e_rev = ex[L:2 * L]

    q_in = (q * e_b).astype(BF16)
    k_out = (kin * e_rev).astype(BF16)
    q_b = q.astype(BF16)
    k_b = kin.astype(BF16)
    xs = []
    for j in range(nlev):
        m = L >> (j + 1)
        if m % SUBLANES == 0:
            qk = jnp.concatenate([(q if blk % 2 else kin)[blk * m:(blk + 1) * m] for blk in range(L // m)], axis=0)
        else:
            qk = jnp.where(isq_ref[:, j:j + 1] > 0.5, q, kin)
        xs.append((qk * ex[(2 + j) * L:(3 + j) * L]).astype(BF16))
    lvl = lvl_ref[...]
    v = hv_ref[0, rows, :]
    g = hg_ref[0, rows, :].astype(F32)
    og = og_ref[...]

    def block_diag(x):
        zero = jnp.zeros((x.shape[0], dk), x.dtype)
        return jnp.concatenate([jnp.concatenate([x[:, :dk], zero], axis=1),
                                jnp.concatenate([zero, x[:, dk:]], axis=1)], axis=0)

    for pair in range(heads // 2):
        sl = slice(2 * pair * dk, 2 * (pair + 1) * dk)
        a = jnp.where(lvl == nlev, _dot_nt(q_b[:, sl], block_diag(k_b[:, sl])), 0.0)
        for j in range(nlev):
            xp = xs[j][:, sl]
            a = jnp.where(lvl == j, _dot_nt(xp, block_diag(xp)), a)
        st = jnp.concatenate([st_ref[2 * pair], st_ref[2 * pair + 1]], axis=1)
        vp = v[:, sl]
        o = _dot(a.astype(BF16), block_diag(vp)) + _dot_nt(q_in[:, sl], block_diag(st.astype(BF16)))
        for half in range(2):
            h = 2 * pair + half
            hs = slice(h * dk, (h + 1) * dk)
            st_ref[h] = st_ref[h] * e_b[L - 1:L, hs] + _dot_tn(v[:, hs], k_out[:, hs])
            oh = o[:, half * dk:(half + 1) * dk]
            ms = jnp.mean(oh * oh, axis=-1, keepdims=True)
            gh = g[:, hs]
            y_ref[0, rows, hs] = (oh * lax.rsqrt(ms + EPS) * og[:, hs] * (gh * _sigmoid(gh))).astype(BF16)


def _hgrn(qr, z, hv, hg, lb, out_gain, s0, L, shared_s0=None):
    b, s, w = z.shape
    heads, dk = s0.shape[1], s0.shape[2]
    cm, lvl, isq, nlev = _hgrn_consts(L)
    per_step = HGRN_CHUNKS_PER_STEP if s % (HGRN_CHUNKS_PER_STEP * L) == 0 else 1
    seq = pl.BlockSpec((1, per_step * L, w), lambda i, c: (i, c, 0))
    const = lambda a: pl.BlockSpec(a.shape, lambda i, c: (0,) * a.ndim)
    state = pl.BlockSpec((1, heads, dk, dk), lambda i, c: (i, 0, 0, 0))
    start = state if shared_s0 is None else pl.BlockSpec((1, heads, dk, dk), lambda i, c: (shared_s0, 0, 0, 0))
    lb2 = lb.reshape(1, w)
    og2 = jnp.tile(out_gain, heads).reshape(1, w)
    return pl.pallas_call(
        functools.partial(_hgrn_kernel, L, nlev, heads, dk),
        grid=(b, s // (per_step * L)),
        in_specs=[seq, seq, seq, seq, const(lb2), const(og2), const(cm), const(lvl), const(isq), start],
        out_specs=[seq, state],
        out_shape=[jax.ShapeDtypeStruct((b, s, w), BF16), jax.ShapeDtypeStruct((b,) + s0.shape[1:], F32)],
        scratch_shapes=[pltpu.VMEM((heads, dk, dk), F32)],
        compiler_params=_params("arbitrary", "arbitrary"),
        name=f"hgrn_scan_{L}",
    )(qr, z, hv, hg, lb2, og2, cm, lvl, isq, s0)


def _t5_bucket_np(rel):
    half = NUM_BUCKETS // 2
    max_exact = half // 2
    assert (NUM_BUCKETS, MAX_DISTANCE) == (32, 128)
    n = np.abs(rel).astype(np.int64)
    nn = np.maximum(n, 1)
    k = np.zeros_like(nn)
    for j in range(1, 48):
        k = np.where(64 * (1 << j) <= nn * nn, j, k)
    large = np.minimum(max_exact + k, half - 1)
    return np.where(rel > 0, half, 0) + np.where(n < max_exact, n, large)


HEADS_PER_COL = LANES // HEAD_DIM
COLS_PER_GROUP = GQA_GROUP // HEADS_PER_COL
HEAD_ORDER = tuple(g * GQA_GROUP + col * HEADS_PER_COL + half
                   for g in range(KV_HEADS) for half in range(HEADS_PER_COL) for col in range(COLS_PER_GROUP))
KV_EXPAND = KV_HEADS * HEADS_PER_COL


def _expand_kv(x, with_ones=False):
    assert HEADS_PER_COL == 2 and KV_HEADS == 2 and x.shape[1] == LANES
    low = lax.broadcasted_iota(I32, x.shape, 1) < HEAD_DIM
    xr = pltpu.roll(x, HEAD_DIM, axis=1)
    zero = jnp.zeros_like(x)
    blocks = [jnp.where(low, x, zero), jnp.where(low, zero, xr), jnp.where(low, xr, zero), jnp.where(low, zero, x)]
    if with_ones:
        blocks = [b for blk in blocks for b in (blk, jnp.ones_like(x))]
    return jnp.concatenate(blocks, axis=1).astype(BF16)


def _attn_core(q, kx, vx, bias):
    tq = q.shape[0]
    scores = []
    for g in range(KV_HEADS):
        cols = [q[:, (g * COLS_PER_GROUP + c) * LANES:(g * COLS_PER_GROUP + c + 1) * LANES]
                for c in range(COLS_PER_GROUP)]
        qst = jnp.concatenate(cols, axis=0)
        for half in range(HEADS_PER_COL):
            blk = g * HEADS_PER_COL + half
            scores.append(_dot_nt(qst, kx[:, blk * LANES:(blk + 1) * LANES]))
    s = jnp.concatenate(scores, axis=0) + bias
    pb = jnp.exp2(s - jnp.max(s, axis=-1, keepdims=True)).astype(BF16)
    rows = COLS_PER_GROUP * tq
    outs = []
    for g in range(KV_HEADS):
        o = None
        for half in range(HEADS_PER_COL):
            blk = g * HEADS_PER_COL + half
            pv = _dot(pb[blk * rows:(blk + 1) * rows], vx[:, 2 * blk * LANES:2 * (blk + 1) * LANES])
            part = pv[:, :LANES] * (1.0 / pv[:, LANES:])
            o = part if o is None else o + part
        outs.extend(o[c * tq:(c + 1) * tq] for c in range(COLS_PER_GROUP))
    return outs


def _bias_rows(table, bucket):
    onehot = (jnp.asarray(bucket)[..., None] == jnp.arange(NUM_BUCKETS)).astype(F32)
    cols = jnp.stack([table[:, h] for h in HEAD_ORDER], axis=1)
    bias = jnp.einsum('...qkb,bh->...hqk', onehot, cols, precision=lax.Precision.HIGHEST)
    return bias.reshape(*bucket.shape[:-2], ATTN_HEADS * bucket.shape[-2], bucket.shape[-1]) * LOG2E


def _pad_keys(bias, sinks):
    tk = bias.shape[-1]
    n_pad = -tk % LANES or LANES
    tq = bias.shape[-2] // ATTN_HEADS
    sink = jnp.repeat(jnp.stack([sinks[h] for h in HEAD_ORDER]).astype(F32), tq) * LOG2E
    sink = jnp.broadcast_to(sink[:, None], bias.shape[:-1] + (1,))
    masked = jnp.full(bias.shape[:-1] + (n_pad - 1,), -jnp.inf, F32)
    return jnp.concatenate([bias, sink, masked], axis=-1)


def _store_heads(ref, x, stream=0):
    for g in range(KV_HEADS):
        ref[0, stream, :, g, :] = x[:, g * HEAD_DIM:(g + 1) * HEAD_DIM]


def _load_heads(ref, stream=0):
    return jnp.concatenate([ref[0, stream, :, g, :] for g in range(KV_HEADS)], axis=1)


def _attn_prompt_kernel(cb, q_ref, k_ref, v_ref, km_ref, vm_ref, bias_ref, o_ref, nk_ref, nv_ref,
                        kx_ref, vx_ref):
    step = pl.program_id(1)
    s_len = k_ref.shape[1]
    meta_at = WINDOW + s_len

    @pl.when(step == pl.num_programs(1) - 1)
    def _():
        _store_heads(nk_ref, k_ref[0, s_len - WINDOW:, :])
        _store_heads(nv_ref, v_ref[0, s_len - WINDOW:, :])

    @pl.when(step == 0)
    def _():
        piece = min(s_len, 512)
        for src, meta, dst in ((k_ref, km_ref, kx_ref), (v_ref, vm_ref, vx_ref)):
            expand = functools.partial(_expand_kv, with_ones=dst is vx_ref)
            blank = lambda n: expand(jnp.zeros((n, src.shape[2]), F32))
            dst[0:WINDOW] = blank(WINDOW)
            for r in range(0, s_len, piece):
                dst[WINDOW + r:WINDOW + r + piece] = expand(src[0, r:r + piece, :])
            dst[meta_at:meta_at + N_META] = expand(meta[...])
            dst[meta_at + N_META:] = blank(dst.shape[0] - meta_at - N_META)

    win = WINDOW + CHUNK
    tail = kx_ref.shape[0] - meta_at
    for j in range(cb):
        c = step * cb + j
        start = pl.multiple_of(c * CHUNK, CHUNK)
        kall = jnp.concatenate([kx_ref[pl.ds(start, win), :], kx_ref[meta_at:meta_at + tail, :]], axis=0)
        vall = jnp.concatenate([vx_ref[pl.ds(start, win), :], vx_ref[meta_at:meta_at + tail, :]], axis=0)
        rows = slice(j * CHUNK, (j + 1) * CHUNK)
        outs = _attn_core(q_ref[0, rows, :], kall, vall, bias_ref[jnp.minimum(c, bias_ref.shape[0] - 1)])
        for ci, o in enumerate(outs):
            o_ref[0, rows, ci * LANES:(ci + 1) * LANES] = o.astype(o_ref.dtype)


def _attn_prompt(q, k, v, k_meta, v_meta, table, sinks):
    b, s, aw = q.shape
    kvw = k.shape[-1]
    nc = s // CHUNK
    cb = ATTN_CHUNKS_PER_STEP if nc % ATTN_CHUNKS_PER_STEP == 0 else 1
    assert s % min(s, 512) == 0
    n_bias = 1
    while True:
        qpos = N_META + (n_bias - 1) * CHUNK
        if np.all(_t5_bucket_np(np.arange(N_META) - qpos) == _t5_bucket_np(np.arange(N_META) - qpos - 10 ** 6)):
            break
        n_bias += 1
    n_bias = min(max(n_bias, WINDOW_CHUNKS + 1), nc)
    cs = np.arange(n_bias)[:, None]
    qpos = N_META + cs * CHUNK + np.arange(CHUNK)[None]
    wpos = N_META + (cs - WINDOW_CHUNKS) * CHUNK + np.arange(WINDOW + CHUNK)[None]
    kpos = np.concatenate([wpos, np.broadcast_to(np.arange(N_META), (n_bias, N_META))], axis=1)
    valid = np.concatenate([wpos >= N_META, np.ones((n_bias, N_META), bool)], axis=1)
    bias = _bias_rows(table, _t5_bucket_np(kpos[:, None, :] - qpos[:, :, None]))
    bias = _pad_keys(jnp.where(valid[:, None, :], bias, -jnp.inf), sinks)
    qs =pl.BlockSpec((1, cb * CHUNK, aw), lambda i, c: (i, c, 0))
    kv = pl.BlockSpec((1, s, kvw), lambda i, c: (i, 0, 0))
    meta = pl.BlockSpec((N_META, kvw), lambda i, c: (0, 0))
    xrows = WINDOW + s + bias.shape[-1] - (WINDOW + CHUNK)
    cache_shape = (1, b, WINDOW, KV_HEADS, HEAD_DIM)
    cache = pl.BlockSpec((1, 1) + cache_shape[2:], lambda i, c: (0, i, 0, 0, 0))
    return pl.pallas_call(
        functools.partial(_attn_prompt_kernel, cb),
        grid=(b, nc // cb),
        in_specs=[qs, kv, kv, meta, meta, pl.BlockSpec(bias.shape, lambda i, c: (0, 0, 0))],
        out_specs=[qs, cache, cache],
        out_shape=[jax.ShapeDtypeStruct((b, s, aw), BF16)] + [jax.ShapeDtypeStruct(cache_shape, F32)] * 2,
        scratch_shapes=[pltpu.VMEM((xrows, KV_EXPAND * LANES), BF16),
                        pltpu.VMEM((xrows, 2 * KV_EXPAND * LANES), BF16)],
        compiler_params=_params("arbitrary", "arbitrary"),
        name="attn_prompt",
    )(q, k, v, k_meta, v_meta, bias)


def _attn_sample_kernel(q_ref, kc_ref, vc_ref, kn_ref, vn_ref, km_ref, vm_ref, bias_ref, o_ref, nk_ref, nv_ref):
    tk = kc_ref.shape[2] + kn_ref.shape[1] + km_ref.shape[0]
    zeros = jnp.zeros((bias_ref.shape[1] - tk, km_ref.shape[1]), F32)
    for s in range(q_ref.shape[0]):
        kc, vc, kn, vn = _load_heads(kc_ref, s), _load_heads(vc_ref, s), kn_ref[s], vn_ref[s]
        kall = _expand_kv(jnp.concatenate([kc, kn, km_ref[...], zeros], axis=0))
        vall = _expand_kv(jnp.concatenate([vc, vn, vm_ref[...], zeros], axis=0), with_ones=True)
        outs = _attn_core(q_ref[s], kall, vall, bias_ref[...])
        for ci, o in enumerate(outs):
            o_ref[s, :, ci * LANES:(ci + 1) * LANES] = o.astype(o_ref.dtype)
        n_new = kn.shape[0]
        _store_heads(nk_ref, jnp.concatenate([kc[n_new:], kn], axis=0), s)
        _store_heads(nv_ref, jnp.concatenate([vc[n_new:], vn], axis=0), s)


def _attn_sample(q, k_cache, v_cache, k_new, v_new, k_meta, v_meta, table, sinks):
    bd, sd, aw = q.shape
    kvw = k_new.shape[-1]
    win = k_cache.shape[2]
    sb = SAMPLE_STREAMS_PER_STEP if bd % SAMPLE_STREAMS_PER_STEP == 0 else 1
    cache = pl.BlockSpec((1, sb) + k_cache.shape[2:], lambda i: (0, i, 0, 0, 0))
    qpos = N_META + PAST_LEN + np.arange(sd)
    kpos = np.concatenate([N_META + PAST_LEN - win + np.arange(win), qpos, np.arange(N_META)])
    bias = _pad_keys(_bias_rows(table, _t5_bucket_np(kpos[None, :] - qpos[:, None])), sinks)
    per = lambda n, w: pl.BlockSpec((sb, n, w), lambda i: (i, 0, 0))
    meta = pl.BlockSpec((N_META, kvw), lambda i: (0, 0))
    return pl.pallas_call(
        _attn_sample_kernel,
        grid=(bd // sb,),
        in_specs=[per(sd, aw), cache, cache, per(sd, kvw), per(sd, kvw), meta, meta,
                  pl.BlockSpec(bias.shape, lambda i: (0, 0))],
        out_specs=[per(sd, aw), cache, cache],
        out_shape=[jax.ShapeDtypeStruct((bd, sd, aw), BF16)] + [jax.ShapeDtypeStruct(k_cache.shape, F32)] * 2,
        compiler_params=_params("arbitrary"),
        name="attn_sample",
    )(q, k_cache, v_cache, k_new, v_new, k_meta, v_meta, bias)


ROUTE_E1, ROUTE_E2, ROUTE_R1, ROUTE_R2, ROUTE_W1, ROUTE_W2 = range(6)
META_FIELDS = 2 * TOP_K


def _merge_kernel(hw, x_ref, yr_ref, at_ref, sga_ref, sgb_ref, wb_ref, wo_ref, wn_ref, wr_ref, br_ref, cnt0_ref, tri_ref,
                  h_ref, xn_ref, route_ref, meta_ref, cnt_ref, carry_ref):
    @pl.when(pl.program_id(0) == 0)
    def _():
        carry_ref[...] = cnt0_ref[...]

    logits = _project_rows(hw, x_ref, yr_ref, at_ref, sga_ref, sgb_ref, wb_ref, wo_ref, wn_ref, wr_ref, br_ref,
                           h_ref, xn_ref)
    _route_rows(logits, tri_ref, route_ref, meta_ref, carry_ref)
    cnt_ref[...] = carry_ref[...]


def _project_rows(hw, x_ref, yr_ref, at_ref, sga_ref, sgb_ref, wb_ref, wo_ref, wn_ref, wr_ref, br_ref, h_ref, xn_ref):
    br = _dot(yr_ref[...], wb_ref[0:hw, :])
    ba = _dot(at_ref[...], wb_ref[hw:, :])
    merged = sga_ref[...].astype(F32) * br + sgb_ref[...].astype(F32) * ba
    h = x_ref[...] + _dot(merged.astype(BF16), wo_ref[...])
    h_ref[...] = h
    ms = jnp.mean(h * h, axis=-1, keepdims=True)
    xn = h * lax.rsqrt(ms + EPS) * wn_ref[...]
    _store_row_tiles(xn_ref, xn)

    x_hi = xn.astype(BF16)
    x_lo = (xn - x_hi.astype(F32)).astype(BF16)
    both = _dot(x_hi, wr_ref[...])
    return both[:, :LANES] + both[:, LANES:] + _dot(x_lo, wr_ref[:, :LANES]) + br_ref[...]


def _route_rows(logits, tri_ref, route_ref, meta_ref, carry_ref):
    tm = logits.shape[0]
    lane_i = lax.broadcasted_iota(I32, (tm, LANES), 1)
    lane = lane_i.astype(F32)
    group_of_lane = (lane_i >> int(math.log2(EXPERTS_PER_GROUP))).astype(F32)
    ninf = -jnp.inf
    first = lambda hit, idx: jnp.min(jnp.where(hit, idx, float(LANES)), axis=-1, keepdims=True)
    gmask = (lane_i >= N_EXPERTS) & (lane_i < N_EXPERTS + N_GROUPS)
    gl = jnp.where(gmask, logits, ninf)
    gmax = jnp.max(gl, axis=-1, keepdims=True)
    gidx = first(gl == gmax, lane - N_EXPERTS)
    gval = 1.0 / jnp.sum(jnp.exp(gl - gmax), axis=-1, keepdims=True)
    emask = (lane_i < N_EXPERTS) & (group_of_lane == gidx)
    el = jnp.where(emask, logits, ninf)
    m1 = jnp.max(el, axis=-1, keepdims=True)
    i1 = first(el == m1, lane)
    el2 = jnp.where(lane == i1, ninf, el)
    m2 = jnp.max(el2, axis=-1, keepdims=True)
    i2 = first(el2 == m2, lane)
    e21 = jnp.exp(m2 - m1)
    w1 = gval / (1.0 + e21)
    w2 = gval * e21 / (1.0 + e21)

    sel1 = lane == i1
    sel2 = lane == i2
    oh = (sel1 | sel2).astype(BF16)
    before = _dot(tri_ref[...], oh) + carry_ref[...]
    r1 = jnp.sum(jnp.where(sel1, before, 0.0), axis=-1, keepdims=True)
    r2 = jnp.sum(jnp.where(sel2, before, 0.0), axis=-1, keepdims=True)
    carry_ref[...] = carry_ref[...] + jnp.sum(oh.astype(F32), axis=0, keepdims=True)

    rec = jnp.zeros((tm, LANES), F32)
    for slot, val in ((ROUTE_E1, i1), (ROUTE_E2, i2), (ROUTE_W1, w1), (ROUTE_W2, w2), (ROUTE_R1, r1), (ROUTE_R2, r2)):
        rec = jnp.where(lane_i == slot, val, rec)
    route_ref[...] = rec
    meta_ref[0] = rec.T[ROUTE_E1:ROUTE_E1 + META_FIELDS].astype(I32)


def _merge(x, y_rec, att, sga, sgb, wb, wo, w_norm, w_router, b_router, cnt0):
    t, d = x.shape
    hw = y_rec.shape[1]
    tm = PROJ_TILE if t % PROJ_TILE == 0 else t
    row = lambda w: pl.BlockSpec((tm, w), lambda i: (i, 0))
    const = lambda a, b: pl.BlockSpec((a, b), lambda i: (0, 0))
    return pl.pallas_call(
        functools.partial(_merge_kernel, hw),
        grid=(t // tm,),
        in_specs=[row(d), row(hw), row(att.shape[1]), row(d), row(d), const(*wb.shape), const(d, d), const(1, d),
                  const(d, 2 * LANES), const(1, LANES), const(1, LANES), const(tm, tm)],
        out_specs=[row(d), pl.BlockSpec((tm * d // LANES, LANES), lambda i: (i, 0)), row(LANES),
                   pl.BlockSpec((1, META_FIELDS, tm), lambda i: (i, 0, 0)), const(1, LANES)],
        out_shape=[jax.ShapeDtypeStruct((t, d), F32), jax.ShapeDtypeStruct((t * d // LANES, LANES), F32),
                   jax.ShapeDtypeStruct((t, LANES), F32), jax.ShapeDtypeStruct((t // tm, META_FIELDS, tm), I32),
                   jax.ShapeDtypeStruct((1, LANES), F32)],
        scratch_shapes=[pltpu.VMEM((1, LANES), F32)],
        compiler_params=_params("arbitrary"),
        name="merge_route",
    )(x, y_rec, att, sga, sgb, wb, wo, w_norm.reshape(1, d), w_router, b_router, cnt0,
      jnp.asarray(np.tri(tm, k=-1), dtype=BF16))


def _row_copy(src, dst, sem):
    return pltpu.make_async_copy(src, dst, sem)


ROW_UNROLL = 8


def _tile_rows(ref, row, sub):
    return ref.at[pl.ds(pl.multiple_of(row * sub, sub), sub)]


def _scatter_kernel(n_tok, sub, first, slot_ref, pstart_ref, pend_ref, x_ref, *rest):
    xs_ref, zero_ref, sem, zsem, tsem = rest[-5:]

    def zero_blocks(blocks, zs, wait):
        for cond, row in blocks:
            @pl.when(cond)
            def _():
                at = row * sub if isinstance(row, int) else pl.multiple_of(row * sub, EXPERT_TILE * sub)
                cp = _row_copy(zero_ref, xs_ref.at[pl.ds(at, EXPERT_TILE * sub)], zs)
                cp.wait() if wait else cp.start()

    if first:
        n_rows = xs_ref.shape[0] // sub
        tails = [(pend_ref[e] > pstart_ref[e], pend_ref[e] - EXPERT_TILE) for e in range(N_EXPERTS)]
        unused = [(n_rows - (j + 1) * EXPERT_TILE >= pend_ref[N_EXPERTS - 1], n_rows - (j + 1) * EXPERT_TILE)
                  for j in range(N_EXPERTS)]

        @pl.when(pl.program_id(0) == 0)
        def _():
            zero_ref[...] = jnp.zeros_like(zero_ref)
            zero_blocks(tails, zsem, False)
            zero_blocks(unused, tsem, False)
            zero_blocks(tails, zsem, True)

    def issue(grp, _):
        for u in range(ROW_UNROLL):
            r = grp * ROW_UNROLL + u
            for k in range(TOP_K):
                _row_copy(_tile_rows(x_ref, r, sub), _tile_rows(xs_ref, slot_ref[0, k, r], sub),
                          sem).start(priority=(u * TOP_K + k) % 2)
        return 0

    lax.fori_loop(0, n_tok // ROW_UNROLL, issue, 0)
    for k in range(TOP_K):
        _row_copy(x_ref, xs_ref.at[pl.ds(0, n_tok * sub)], sem).wait()

    if first:
        @pl.when(pl.program_id(0) == 0)
        def _():
            zero_blocks(unused, tsem, True)


def _scatter_rows(x, slots, pstarts, pends, xs, rows):
    n, fields, tm = slots.shape
    sub = x.shape[0] // (n * tm)
    first = xs is None
    smem = pl.BlockSpec(memory_space=pltpu.SMEM)
    return pl.pallas_call(
        functools.partial(_scatter_kernel, tm, sub, first),
        grid=(n,),
        in_specs=[pl.BlockSpec((1, fields, tm), lambda i: (i, 0, 0), memory_space=pltpu.SMEM), smem, smem,
                  pl.BlockSpec((tm * sub, LANES), lambda i: (i, 0))]
        + ([] if first else [pl.BlockSpec(memory_space=pl.ANY)]),
        out_specs=pl.BlockSpec(memory_space=pl.ANY),
        out_shape=jax.ShapeDtypeStruct((rows * sub, LANES), F32),
        scratch_shapes=[pltpu.VMEM((EXPERT_TILE * sub, LANES), F32)] + [pltpu.SemaphoreType.DMA(())] * 3,
        input_output_aliases={} if first else {4: 0},
        compiler_params=_params("arbitrary"),
        name="moe_scatter",
    )(slots, pstarts, pends, x, *([] if first else [xs]))


def _expert_kernel(tm, sub, be_ref, nb_ref, nxt_ref, x_ref, wg_ref, wu_ref, wd_ref, y_ref,
                   raw_g, raw_u, raw_d, wgb_ref, wub_ref, wdb_ref, sem, slot_ref):
    i = pl.program_id(0)
    used = i < nb_ref[0]
    first = used & ((i == 0) | (be_ref[i] != be_ref[jnp.maximum(i - 1, 0)]))

    def fetch(e, slot):
        return [_row_copy(w.at[e], raw.at[slot], sem.at[slot])
                for w, raw in ((wg_ref, raw_g), (wu_ref, raw_u), (wd_ref, raw_d))]

    @pl.when(i == 0)
    def _():
        slot_ref[0] = 0
        for cp in fetch(be_ref[0], 0):
            cp.start()

    @pl.when(first)
    def _():
        slot = slot_ref[0]
        for cp in fetch(be_ref[i], slot):
            cp.wait()
        wgb_ref[...] = raw_g[slot].astype(BF16)
        wub_ref[...] = raw_u[slot].astype(BF16)
        wdb_ref[...] = raw_d[slot].astype(BF16)
        slot_ref[0] = 1 - slot

        @pl.when(nxt_ref[i] >= 0)
        def _():
            for cp in fetch(nxt_ref[i], 1 - slot):
                cp.start()

    @pl.when(used)
    def _():
        x = _load_row_tiles(x_ref, tm, sub).astype(BF16)
        g = _dot(x, wgb_ref[...])
        u = _dot(x, wub_ref[...])
        hmid = (g * _sigmoid(g) * u).astype(BF16)
        _store_row_tiles(y_ref, _dot(hmid, wdb_ref[...]))

    @pl.when(jnp.logical_not(used))
    def _():
        y_ref[...] = jnp.zeros_like(y_ref)


def _experts(xs, block_e, n_used, nxt_e, wg, wu, wd):
    _, d, ff = wg.shape
    sub = d // LANES
    tm = EXPERT_TILE
    blk = pl.BlockSpec((tm * sub, LANES), lambda i, be, nb, nx: (i, 0))
    hbm = pl.BlockSpec(memory_space=pl.ANY)
    grid_spec = pltpu.PrefetchScalarGridSpec(
        num_scalar_prefetch=3,
        grid=(xs.shape[0] // (tm * sub),),
        in_specs=[blk, hbm, hbm, hbm],
        out_specs=blk,
        scratch_shapes=[pltpu.VMEM((2, d, ff), F32), pltpu.VMEM((2, d, ff), F32), pltpu.VMEM((2, ff, d), F32),
                        pltpu.VMEM((d, ff), BF16), pltpu.VMEM((d, ff), BF16), pltpu.VMEM((ff, d), BF16),
                        pltpu.SemaphoreType.DMA((2,)), pltpu.SMEM((1,), I32)],
    )
    return pl.pallas_call(
        functools.partial(_expert_kernel, tm, sub),
        grid_spec=grid_spec,
        out_shape=jax.ShapeDtypeStruct(xs.shape, F32),
        compiler_params=_params("arbitrary"),
        name="moe_experts",
    )(block_e, n_used, nxt_e, xs, wg, wu, wd)


def _combine_kernel(n_tok, sub, scur_ref, snext_ref, h_ref, route_ref, ys_ref, o_ref, buf, sem):
    i = pl.program_id(0)
    n = pl.num_programs(0)
    slot = i % 2

    def start(slot_ref, s):
        def issue(grp, _):
            for u in range(ROW_UNROLL):
                r = grp * ROW_UNROLL + u
                for k in range(TOP_K):
                    _row_copy(_tile_rows(ys_ref, slot_ref[0, k, r], sub), _tile_rows(buf.at[s, k], r, sub),
                              sem.at[s]).start(priority=(u * TOP_K + k) % 2)
            return 0
        lax.fori_loop(0, n_tok // ROW_UNROLL, issue, 0)

    @pl.when(i == 0)
    def _():
        start(scur_ref, 0)

    @pl.when(i + 1 < n)
    def _():
        start(snext_ref, 1 - slot)

    for k in range(TOP_K):
        _row_copy(ys_ref.at[pl.ds(0, n_tok * sub)], buf.at[slot, k], sem.at[slot]).wait()
    route = route_ref[...]
    w1 = route[:, ROUTE_W1:ROUTE_W1 + 1]
    w2 = route[:, ROUTE_W2:ROUTE_W2 + 1]
    for j in range(sub):
        cols = slice(j * LANES, (j + 1) * LANES)
        part = lambda k: buf[slot, k, pl.ds(j, n_tok, stride=sub), :]
        o_ref[:, cols] = h_ref[:, cols] + (part(0) * w1 + part(1) * w2)


def _combine(h, route, slots, ys):
    t, d = h.shape
    n, fields, tm = slots.shape
    sub = d // LANES
    mspec = lambda f: pl.BlockSpec((1, fields, tm), f, memory_space=pltpu.SMEM)
    return pl.pallas_call(
        functools.partial(_combine_kernel, tm, sub),
        grid=(n,),
        in_specs=[mspec(lambda i: (i, 0, 0)), mspec(lambda i: (jnp.minimum(i + 1, n - 1), 0, 0)),
                  pl.BlockSpec((tm, d), lambda i: (i, 0)), pl.BlockSpec((tm, LANES), lambda i: (i, 0)),
                  pl.BlockSpec(memory_space=pl.ANY)],
        out_specs=pl.BlockSpec((tm, d), lambda i: (i, 0)),
        out_shape=jax.ShapeDtypeStruct((t, d), F32),
        scratch_shapes=[pltpu.VMEM((2, TOP_K, tm * sub, LANES), F32), pltpu.SemaphoreType.DMA((2,))],
        compiler_params=_params("arbitrary"),
        name="moe_combine",
    )(slots, slots, h, route, ys)


def _moe(parts, counts, wg, wu, wd):
    tm = EXPERT_TILE
    n_assign = sum(part[0].shape[0] for part in parts) * TOP_K
    n_blocks = -(-(n_assign + N_EXPERTS * (tm - 1)) // tm)
    counts = counts[0, :N_EXPERTS].astype(I32)
    pcounts = (counts + tm - 1) // tm * tm
    pends = jnp.cumsum(pcounts)
    pstarts = pends - pcounts
    block_start = jnp.arange(n_blocks, dtype=I32) * tm
    block_e = jnp.minimum(jnp.sum((pends[None, :] <= block_start[:, None]).astype(I32), axis=1), N_EXPERTS - 1)
    n_used = pends[-1:] // tm
    ids = jnp.arange(N_EXPERTS, dtype=I32)
    later = (ids[None, :] > ids[:, None]) & (pcounts[None, :] > 0)
    nxt_of = jnp.min(jnp.where(later, ids[None, :], N_EXPERTS), axis=1)
    nxt_of = jnp.where(nxt_of < N_EXPERTS, nxt_of, -1)
    nxt_e = jnp.sum(jnp.where(block_e[:, None] == ids[None, :], nxt_of[None, :], 0), axis=1)
    xs = None
    slots = []
    for _, xn, _, meta in parts:
        assert meta.shape[-1] % ROW_UNROLL == 0
        e, pos = meta[:, :TOP_K], meta[:, TOP_K:]
        seg = jnp.sum(jnp.where(e[..., None] == jnp.arange(N_EXPERTS, dtype=I32), pstarts, 0), axis=-1)
        slots.append(seg + pos)
        xs = _scatter_rows(xn, slots[-1], pstarts, pends, xs, n_blocks * tm)
    ys = _experts(xs, block_e, n_used, nxt_e, wg, wu, wd)
    return [_combine(h, route, s, ys) for (h, _, route, _), s in zip(parts, slots)]


def kernel(x_prompt, x_sample, cache_swa_k, cache_swa_v, state_hgrn, meta_tokens, rel_bias_table, hgrn_lower_bounds, w_norm_mix, w_in, hgrn_out_norm, q_norm, k_norm, attn_sinks, w_branch, w_out, w_norm_ffn, w_router_group, b_router_group, w_router_expert, b_router_expert, w_expert_gate, w_expert_up, w_expert_down):
    b, s, d = x_prompt.shape
    bd, sd, _ = x_sample.shape
    depth, _, heads, dk, dv = state_hgrn.shape
    assert depth == 1 and heads == HGRN_HEADS and dk == dv
    hw = heads * dk
    aw = ATTN_HEADS * HEAD_DIM
    kvw = KV_HEADS * HEAD_DIM
    assert w_in.shape[-1] == 4 * hw + aw + 2 * kvw + 2 * d
    assert s % HGRN_CHUNK == 0 and s % CHUNK == 0 and sd == N_META and N_EXPERTS + N_GROUPS <= LANES
    l = 0

    p = jax.nn.softmax(hgrn_lower_bounds.astype(F32), axis=0)
    lb = jnp.cumsum(p, axis=0)[l + 1] - p[0]

    w_in_b = w_in[l].astype(BF16)
    proj = functools.partial(_inproj, w_norm=w_norm_mix[l], w_in_bf16=w_in_b, q_gain=q_norm[l], k_gain=k_norm[l],
                             hw=hw, aw=aw, kvw=kvw)
    x_small = jnp.concatenate([x_sample.reshape(bd * sd, d), meta_tokens.astype(F32)], axis=0)
    qr_s, z_s, hv_s, hg_s, qa_s, k_s, v_s, sga_s, sgb_s = proj(x_small)
    qr_p, z_p, hv_p, hg_p, qa_p, k_p, v_p, sga_p, sgb_p = proj(x_prompt.reshape(b * s, d))
    ns = bd * sd
    k_meta, v_meta = k_s[ns:], v_s[ns:]

    streams = lambda a, n: a.reshape(n, -1, a.shape[-1])
    s0_small = jnp.concatenate([state_hgrn[l].astype(F32), jnp.zeros((1, heads, dk, dv), F32)], axis=0)
    y_small, st_small = _hgrn(streams(qr_s, bd + 1), streams(z_s, bd + 1), streams(hv_s, bd + 1),
                              streams(hg_s, bd + 1), lb, hgrn_out_norm[l], s0_small, sd)
    y_p, st_p = _hgrn(streams(qr_p, b), streams(z_p, b), streams(hv_p, b), streams(hg_p, b), lb,
                      hgrn_out_norm[l], st_small, HGRN_CHUNK, shared_s0=bd)

    table = rel_bias_table.astype(F32)
    att_p, new_k_p, new_v_p = _attn_prompt(streams(qa_p, b), streams(k_p, b), streams(v_p, b), k_meta, v_meta,
                                           table, attn_sinks[l])
    k_new, v_new = k_s[:ns].reshape(bd, sd, kvw), v_s[:ns].reshape(bd, sd, kvw)
    att_s, new_k_s, new_v_s = _attn_sample(qa_s[:ns].reshape(bd, sd, aw), cache_swa_k[l:l + 1].astype(F32),
                                           cache_swa_v[l:l + 1].astype(F32), k_new, v_new, k_meta, v_meta,
                                           table, attn_sinks[l])

    wb = w_branch[l].astype(BF16)
    wo = w_out[l].astype(BF16)
    w_router = jnp.pad(jnp.concatenate([w_router_expert[l], w_router_group[l]], axis=1).astype(F32),
                       ((0, 0), (0, LANES - N_EXPERTS - N_GROUPS)))
    b_router = jnp.pad(jnp.concatenate([b_router_expert[l], b_router_group[l]]).astype(F32),
                       (0, LANES - N_EXPERTS - N_GROUPS)).reshape(1, LANES)
    w_router_hi = w_router.astype(BF16)
    w_router_lo = (w_router - w_router_hi.astype(F32)).astype(BF16)
    w_router = jnp.concatenate([w_router_hi, w_router_lo], axis=1)
    merge = functools.partial(_merge, wb=wb, wo=wo, w_norm=w_norm_ffn[l], w_router=w_router, b_router=b_router)
    *part_p, cnt_p = merge(x_prompt.reshape(b * s, d), y_p.reshape(b * s, hw), att_p.reshape(b * s, aw),
                           sga_p, sgb_p, cnt0=jnp.zeros((1, LANES), F32))
    *part_s, cnt_s = merge(x_sample.reshape(ns, d), y_small[:bd].reshape(ns, hw), att_s.reshape(ns, aw),
                           sga_s[:ns], sgb_s[:ns], cnt0=cnt_p)

    out_p, out_s = _moe([part_p, part_s], cnt_s, w_expert_gate[l], w_expert_up[l], w_expert_down[l])

    return (out_p.reshape(b, s, d), out_s.reshape(bd, sd, d), new_k_p, new_v_p, st_p[None],
            new_k_s, new_v_s, st_small[:bd][None])
```

```python
import functools
import math

import numpy as np
import jax
import jax.numpy as jnp
from jax import lax
from jax.experimental import pallas as pl
from jax.experimental.pallas import tpu as pltpu

F32 = jnp.float32
BF16 = jnp.bfloat16
I32 = jnp.int32

CHUNK = 64
N_META = 16
PAST_LEN = 2048
EPS = 1e-6
HGRN_HEADS = 4
ATTN_HEADS = 8
KV_HEADS = 2
HEAD_DIM = 64
GQA_GROUP = ATTN_HEADS // KV_HEADS
WINDOW = 128
WINDOW_CHUNKS = WINDOW // CHUNK
NUM_BUCKETS = 32
MAX_DISTANCE = 128
N_GROUPS = 4
EXPERTS_PER_GROUP = 8
N_EXPERTS = N_GROUPS * EXPERTS_PER_GROUP
TOP_K = 2

LANES = 128
MXU_WIDTH = 256
VMEM_LIMIT = 56 * 1024 * 1024

INPROJ_TILE = 512
PROJ_TILE = 512
HGRN_CHUNK = 128
HGRN_CHUNKS_PER_STEP = 8
ATTN_CHUNKS_PER_STEP = 16
SAMPLE_STREAMS_PER_STEP = 4
EXPERT_TILE = 512


LOG2E = math.log2(math.e)


def _sigmoid(x):
    return 1.0 / (1.0 + jnp.exp(-x))


def _split3(x):
    hi = x.astype(BF16)
    r1 = x - hi.astype(F32)
    mid = r1.astype(BF16)
    lo = (r1 - mid.astype(F32)).astype(BF16)
    return hi, mid, lo


def _dot(a, b):
    return jnp.dot(a, b, preferred_element_type=F32)


def _dot_nt(a, b):
    return lax.dot_general(a, b, (((1,), (1,)), ((), ())), preferred_element_type=F32)


def _dot_tn(a, b):
    return lax.dot_general(a, b, (((0,), (0,)), ((), ())), preferred_element_type=F32)


SUBLANES = 8


def _store_row_tiles(ref, x):
    n, d = x.shape
    sub = d // LANES
    for j in range(sub):
        ref[pl.ds(j, n, stride=sub), :] = x[:, j * LANES:(j + 1) * LANES]


def _load_row_tiles(ref, n, sub):
    return jnp.concatenate([ref[pl.ds(j, n, stride=sub), :] for j in range(sub)], axis=1)


def _params(*sem):
    return pltpu.CompilerParams(dimension_semantics=sem, vmem_limit_bytes=VMEM_LIMIT)


def _inproj_kernel(hw, aw, kvw, d, x_ref, wn_ref, w_ref, qg_ref, kg_ref, bdq_ref, bdk_ref,
                   qr_ref, z_ref, hv_ref, hg_ref, qa_ref, k_ref, v_ref, sga_ref, sgb_ref):
    x = x_ref[...]
    ms = jnp.mean(x * x, axis=-1, keepdims=True)
    xn = (x * lax.rsqrt(ms + EPS) * wn_ref[...]).astype(BF16)

    def seg(a, b):
        return _dot(xn, w_ref[:, a:b])

    def head_rms(a, bd_ref, gain):
        sq = (a * a).astype(BF16)
        wb = bd_ref.shape[0]
        m = jnp.concatenate([_dot(sq[:, c:c + wb], bd_ref[...]) for c in range(0, a.shape[1], wb)], axis=1)
        return a * lax.rsqrt(m + EPS) * gain

    o = 0
    hq = seg(o, o + hw)
    qr_ref[...] = (hq * _sigmoid(hq) * (hw // HGRN_HEADS) ** -0.5).astype(BF16)
    o += hw
    z_ref[...] = seg(o, o + hw)
    o += hw
    hv_ref[...] = seg(o, o + hw).astype(BF16)
    o += hw
    hg_ref[...] = seg(o, o + hw).astype(BF16)
    o += hw
    aq = seg(o, o + aw)
    qa_ref[...] = (head_rms(aq, bdq_ref, qg_ref[...]) * (HEAD_DIM ** -0.5 * LOG2E)).astype(BF16)
    o += aw
    kv = seg(o, o + 2 * kvw)
    k_ref[...] = head_rms(kv[:, :kvw], bdk_ref, kg_ref[...])
    v_ref[...] = kv[:, kvw:]
    o += 2 * kvw
    sga_ref[...] = _sigmoid(seg(o, o + d)).astype(BF16)
    o += d
    sgb_ref[...] = _sigmoid(seg(o, o + d)).astype(BF16)


def _block_diag_mean(width, group):
    i = np.arange(width)
    return jnp.asarray((i[:, None] // group == i[None, :] // group) / group, dtype=BF16)


def _inproj(x, w_norm, w_in_bf16, q_gain, k_gain, hw, aw, kvw):
    t, d = x.shape
    tm = INPROJ_TILE if t % INPROJ_TILE == 0 else t
    cols = w_in_bf16.shape[1]
    row = lambda w: pl.BlockSpec((tm, w), lambda i: (i, 0))
    const = lambda a, b: pl.BlockSpec((a, b), lambda i: (0, 0))
    outs = [(hw, BF16), (hw, F32), (hw, BF16), (hw, BF16), (aw, BF16), (kvw, F32), (kvw, F32), (d, BF16), (d, BF16)]
    bdq = min(aw, MXU_WIDTH)
    assert aw % bdq == 0
    return pl.pallas_call(
        functools.partial(_inproj_kernel, hw, aw, kvw, d),
        grid=(t // tm,),
        in_specs=[row(d), const(1, d), const(d, cols), const(1, aw), const(1, kvw), const(bdq, bdq), const(kvw, kvw)],
        out_specs=[row(w) for w, _ in outs],
        out_shape=[jax.ShapeDtypeStruct((t, w), dt) for w, dt in outs],
        compiler_params=_params("arbitrary"),
        name="inproj",
    )(x, w_norm.reshape(1, d), w_in_bf16,
      jnp.tile(q_gain, aw // HEAD_DIM).reshape(1, aw), jnp.tile(k_gain, kvw // HEAD_DIM).reshape(1, kvw),
      _block_diag_mean(bdq, HEAD_DIM), _block_diag_mean(kvw, HEAD_DIM))


def _hgrn_consts(L):
    t = np.arange(L)
    u = t[None, :]
    blocks = [u <= t[:, None], u > t[:, None]]
    levels = []
    m = L // 2
    while m >= 1:
        levels.append(m)
        m //= 2
    lvl = np.full((L, L), -1, np.int32)
    lvl[t, t] = len(levels)
    isq_cols = []
    for j, m in enumerate(levels):
        bnd = (t // (2 * m)) * (2 * m) + m - 1
        isq = (t % (2 * m)) >= m
        cq = isq[:, None] & (u > bnd[:, None]) & (u <= t[:, None])
        ck = (~isq)[:, None] & (u > t[:, None]) & (u <= bnd[:, None])
        blocks.append(cq | ck)
        same = (t[:, None] // (2 * m)) == (t[None, :] // (2 * m))
        lvl[same & isq[:, None] & (~isq)[None, :]] = j
        isq_cols.append(isq)
    c = np.concatenate(blocks, axis=0).astype(np.float32)
    isq = np.stack(isq_cols, axis=1).astype(np.float32)
    isq = np.pad(isq, ((0, 0), (0, LANES - isq.shape[1])))
    c2 = np.tile(c, (1, 2))
    return jnp.asarray(c2, dtype=BF16), jnp.asarray(np.tile(lvl, (1, 2))), jnp.asarray(isq), len(levels)


def _hgrn_kernel(L, nlev, heads, dk, qr_ref, z_ref, hv_ref, hg_ref, lb_ref, og_ref, c_ref, lvl_ref, isq_ref,
                 s0_ref, y_ref, sout_ref, st_ref):
    c = pl.program_id(1)

    @pl.when(c == 0)
    def _():
        for h in range(heads):
            st_ref[h] = s0_ref[0, h].T

    for cc in range(z_ref.shape[1] // L):
        _hgrn_chunk(L, nlev, heads, dk, slice(cc * L, (cc + 1) * L), qr_ref, z_ref, hv_ref, hg_ref, lb_ref, og_ref,
                    c_ref, lvl_ref, isq_ref, y_ref, st_ref)

    @pl.when(c == pl.num_programs(1) - 1)
    def _():
        for h in range(heads):
            sout_ref[0, h] = st_ref[h].T


def _hgrn_chunk(L, nlev, heads, dk, rows, qr_ref, z_ref, hv_ref, hg_ref, lb_ref, og_ref, c_ref, lvl_ref, isq_ref,
                y_ref, st_ref):
    z = z_ref[0, rows, :]
    lb = lb_ref[...]
    e = jnp.exp(-jnp.abs(z))
    r = 1.0 / (1.0 + e)
    pos = z >= 0
    sig = jnp.where(pos, r, e * r)
    sig_neg = jnp.where(pos, e * r, r)
    logf = jnp.log(lb + (1.0 - lb) * sig)
    kin = (1.0 - lb) * sig_neg
    q = qr_ref[0, rows, :].astype(F32)

    hi, mid, _ = _split3(logf * LOG2E)
    ex = jnp.exp2(_dot(c_ref[...], jnp.concatenate([hi, mid], axis=0)))
    e_b = ex[0:L]
    e_rev = ex[L:2 * L]

    q_in = (q * e_b).astype(BF16)
    k_out = (kin * e_rev).astype(BF16)
    q_b = q.astype(BF16)
    k_b = kin.astype(BF16)
    xs = []
    for j in range(nlev):
        m = L >> (j + 1)
        if m % SUBLANES == 0:
            qk = jnp.concatenate([(q if blk % 2 else kin)[blk * m:(blk + 1) * m] for blk in range(L // m)], axis=0)
        else:
            qk = jnp.where(isq_ref[:, j:j + 1] > 0.5, q, kin)
        xs.append((qk * ex[(2 + j) * L:(3 + j) * L]).astype(BF16))
    lvl = lvl_ref[...]
    v = hv_ref[0, rows, :]
    g = hg_ref[0, rows, :].astype(F32)
    og = og_ref[...]

    def block_diag(x):
        zero = jnp.zeros((x.shape[0], dk), x.dtype)
        return jnp.concatenate([jnp.concatenate([x[:, :dk], zero], axis=1),
                                jnp.concatenate([zero, x[:, dk:]], axis=1)], axis=0)

    for pair in range(heads // 2):
        sl = slice(2 * pair * dk, 2 * (pair + 1) * dk)
        a = jnp.where(lvl == nlev, _dot_nt(q_b[:, sl], block_diag(k_b[:, sl])), 0.0)
        for j in range(nlev):
            xp = xs[j][:, sl]
            a = jnp.where(lvl == j, _dot_nt(xp, block_diag(xp)), a)
        st = jnp.concatenate([st_ref[2 * pair], st_ref[2 * pair + 1]], axis=1)
        vp = v[:, sl]
        o = _dot(a.astype(BF16), block_diag(vp)) + _dot_nt(q_in[:, sl], block_diag(st.astype(BF16)))
        for half in range(2):
            h = 2 * pair + half
            hs = slice(h * dk, (h + 1) * dk)
            st_ref[h] = st_ref[h] * e_b[L - 1:L, hs] + _dot_tn(v[:, hs], k_out[:, hs])
            oh = o[:, half * dk:(half + 1) * dk]
            ms = jnp.mean(oh * oh, axis=-1, keepdims=True)
            gh = g[:, hs]
            y_ref[0, rows, hs] = (oh * lax.rsqrt(ms + EPS) * og[:, hs] * (gh * _sigmoid(gh))).astype(BF16)


def _hgrn(qr, z, hv, hg, lb, out_gain, s0, L, shared_s0=None):
    b, s, w = z.shape
    heads, dk = s0.shape[1], s0.shape[2]
    cm, lvl, isq, nlev = _hgrn_consts(L)
    per_step = HGRN_CHUNKS_PER_STEP if s % (HGRN_CHUNKS_PER_STEP * L) == 0 else 1
    seq = pl.BlockSpec((1, per_step * L, w), lambda i, c: (i, c, 0))
    const = lambda a: pl.BlockSpec(a.shape, lambda i, c: (0,) * a.ndim)
    state = pl.BlockSpec((1, heads, dk, dk), lambda i, c: (i, 0, 0, 0))
    start = state if shared_s0 is None else pl.BlockSpec((1, heads, dk, dk), lambda i, c: (shared_s0, 0, 0, 0))
    lb2 = lb.reshape(1, w)
    og2 = jnp.tile(out_gain, heads).reshape(1, w)
    return pl.pallas_call(
        functools.partial(_hgrn_kernel, L, nlev, heads, dk),
        grid=(b, s // (per_step * L)),
        in_specs=[seq, seq, seq, seq, const(lb2), const(og2), const(cm), const(lvl), const(isq), start],
        out_specs=[seq, state],
        out_shape=[jax.ShapeDtypeStruct((b, s, w), BF16), jax.ShapeDtypeStruct((b,) + s0.shape[1:], F32)],
        scratch_shapes=[pltpu.VMEM((heads, dk, dk), F32)],
        compiler_params=_params("arbitrary", "arbitrary"),
        name=f"hgrn_scan_{L}",
    )(qr, z, hv, hg, lb2, og2, cm, lvl, isq, s0)


def _t5_bucket_np(rel):
    half = NUM_BUCKETS // 2
    max_exact = half // 2
    assert (NUM_BUCKETS, MAX_DISTANCE) == (32, 128)
    n = np.abs(rel).astype(np.int64)
    nn = np.maximum(n, 1)
    k = np.zeros_like(nn)
    for j in range(1, 48):
        k = np.where(64 * (1 << j) <= nn * nn, j, k)
    large = np.minimum(max_exact + k, half - 1)
    return np.where(rel > 0, half, 0) + np.where(n < max_exact, n, large)


HEADS_PER_COL = LANES // HEAD_DIM
COLS_PER_GROUP = GQA_GROUP // HEADS_PER_COL
HEAD_ORDER = tuple(g * GQA_GROUP + col * HEADS_PER_COL + half
                   for g in range(KV_HEADS) for half in range(HEADS_PER_COL) for col in range(COLS_PER_GROUP))
KV_EXPAND = KV_HEADS * HEADS_PER_COL


def _expand_kv(x, with_ones=False):
    assert HEADS_PER_COL == 2 and KV_HEADS == 2 and x.shape[1] == LANES
    low = lax.broadcasted_iota(I32, x.shape, 1) < HEAD_DIM
    xr = pltpu.roll(x, HEAD_DIM, axis=1)
    zero = jnp.zeros_like(x)
    blocks = [jnp.where(low, x, zero), jnp.where(low, zero, xr), jnp.where(low, xr, zero), jnp.where(low, zero, x)]
    if with_ones:
        blocks = [b for blk in blocks for b in (blk, jnp.ones_like(x))]
    return jnp.concatenate(blocks, axis=1).astype(BF16)


def _attn_core(q, kx, vx, bias):
    tq = q.shape[0]
    scores = []
    for g in range(KV_HEADS):
        cols = [q[:, (g * COLS_PER_GROUP + c) * LANES:(g * COLS_PER_GROUP + c + 1) * LANES]
                for c in range(COLS_PER_GROUP)]
        qst = jnp.concatenate(cols, axis=0)
        for half in range(HEADS_PER_COL):
            blk = g * HEADS_PER_COL + half
            scores.append(_dot_nt(qst, kx[:, blk * LANES:(blk + 1) * LANES]))
    s = jnp.concatenate(scores, axis=0) + bias
    pb = jnp.exp2(s - jnp.max(s, axis=-1, keepdims=True)).astype(BF16)
    rows = COLS_PER_GROUP * tq
    outs = []
    for g in range(KV_HEADS):
        o = None
        for half in range(HEADS_PER_COL):
            blk = g * HEADS_PER_COL + half
            pv = _dot(pb[blk * rows:(blk + 1) * rows], vx[:, 2 * blk * LANES:2 * (blk + 1) * LANES])
            part = pv[:, :LANES] * (1.0 / pv[:, LANES:])
            o = part if o is None else o + part
        outs.extend(o[c * tq:(c + 1) * tq] for c in range(COLS_PER_GROUP))
    return outs


def _bias_rows(table, bucket):
    onehot = (jnp.asarray(bucket)[..., None] == jnp.arange(NUM_BUCKETS)).astype(F32)
    cols = jnp.stack([table[:, h] for h in HEAD_ORDER], axis=1)
    bias = jnp.einsum('...qkb,bh->...hqk', onehot, cols, precision=lax.Precision.HIGHEST)
    return bias.reshape(*bucket.shape[:-2], ATTN_HEADS * bucket.shape[-2], bucket.shape[-1]) * LOG2E


def _pad_keys(bias, sinks):
    tk = bias.shape[-1]
    n_pad = -tk % LANES or LANES
    tq = bias.shape[-2] // ATTN_HEADS
    sink = jnp.repeat(jnp.stack([sinks[h] for h in HEAD_ORDER]).astype(F32), tq) * LOG2E
    sink = jnp.broadcast_to(sink[:, None], bias.shape[:-1] + (1,))
    masked = jnp.full(bias.shape[:-1] + (n_pad - 1,), -jnp.inf, F32)
    return jnp.concatenate([bias, sink, masked], axis=-1)


def _store_heads(ref, x, stream=0):
    for g in range(KV_HEADS):
        ref[0, stream, :, g, :] = x[:, g * HEAD_DIM:(g + 1) * HEAD_DIM]


def _load_heads(ref, stream=0):
    return jnp.concatenate([ref[0, stream, :, g, :] for g in range(KV_HEADS)], axis=1)


def _attn_prompt_kernel(cb, q_ref, k_ref, v_ref, km_ref, vm_ref, bias_ref, o_ref, nk_ref, nv_ref,
                        kx_ref, vx_ref):
    step = pl.program_id(1)
    s_len = k_ref.shape[1]
    meta_at = WINDOW + s_len

    @pl.when(step == pl.num_programs(1) - 1)
    def _():
        _store_heads(nk_ref, k_ref[0, s_len - WINDOW:, :])
        _store_heads(nv_ref, v_ref[0, s_len - WINDOW:, :])

    @pl.when(step == 0)
    def _():
        piece = min(s_len, 512)
        for src, meta, dst in ((k_ref, km_ref, kx_ref), (v_ref, vm_ref, vx_ref)):
            expand = functools.partial(_expand_kv, with_ones=dst is vx_ref)
            blank = lambda n: expand(jnp.zeros((n, src.shape[2]), F32))
            dst[0:WINDOW] = blank(WINDOW)
            for r in range(0, s_len, piece):
                dst[WINDOW + r:WINDOW + r + piece] = expand(src[0, r:r + piece, :])
            dst[meta_at:meta_at + N_META] = expand(meta[...])
            dst[meta_at + N_META:] = blank(dst.shape[0] - meta_at - N_META)

    win = WINDOW + CHUNK
    tail = kx_ref.shape[0] - meta_at
    for j in range(cb):
        c = step * cb + j
        start = pl.multiple_of(c * CHUNK, CHUNK)
        kall = jnp.concatenate([kx_ref[pl.ds(start, win), :], kx_ref[meta_at:meta_at + tail, :]], axis=0)
        vall = jnp.concatenate([vx_ref[pl.ds(start, win), :], vx_ref[meta_at:meta_at + tail, :]], axis=0)
        rows = slice(j * CHUNK, (j + 1) * CHUNK)
        outs = _attn_core(q_ref[0, rows, :], kall, vall, bias_ref[jnp.minimum(c, bias_ref.shape[0] - 1)])
        for ci, o in enumerate(outs):
            o_ref[0, rows, ci * LANES:(ci + 1) * LANES] = o.astype(o_ref.dtype)


def _attn_prompt(q, k, v, k_meta, v_meta, table, sinks):
    b, s, aw = q.shape
    kvw = k.shape[-1]
    nc = s // CHUNK
    cb = ATTN_CHUNKS_PER_STEP if nc % ATTN_CHUNKS_PER_STEP == 0 else 1
    assert s % min(s, 512) == 0
    n_bias = 1
    while True:
        qpos = N_META + (n_bias - 1) * CHUNK
        if np.all(_t5_bucket_np(np.arange(N_META) - qpos) == _t5_bucket_np(np.arange(N_META) - qpos - 10 ** 6)):
            break
        n_bias += 1
    n_bias = min(max(n_bias, WINDOW_CHUNKS + 1), nc)
    cs = np.arange(n_bias)[:, None]
    qpos = N_META + cs * CHUNK + np.arange(CHUNK)[None]
    wpos = N_META + (cs - WINDOW_CHUNKS) * CHUNK + np.arange(WINDOW + CHUNK)[None]
    kpos = np.concatenate([wpos, np.broadcast_to(np.arange(N_META), (n_bias, N_META))], axis=1)
    valid = np.concatenate([wpos >= N_META, np.ones((n_bias, N_META), bool)], axis=1)
    bias = _bias_rows(table, _t5_bucket_np(kpos[:, None, :] - qpos[:, :, None]))
    bias = _pad_keys(jnp.where(valid[:, None, :], bias, -jnp.inf), sinks)
    qs =pl.BlockSpec((1, cb * CHUNK, aw), lambda i, c: (i, c, 0))
    kv = pl.BlockSpec((1, s, kvw), lambda i, c: (i, 0, 0))
    meta = pl.BlockSpec((N_META, kvw), lambda i, c: (0, 0))
    xrows = WINDOW + s + bias.shape[-1] - (WINDOW + CHUNK)
    cache_shape = (1, b, WINDOW, KV_HEADS, HEAD_DIM)
    cache = pl.BlockSpec((1, 1) + cache_shape[2:], lambda i, c: (0, i, 0, 0, 0))
    return pl.pallas_call(
        functools.partial(_attn_prompt_kernel, cb),
        grid=(b, nc // cb),
        in_specs=[qs, kv, kv, meta, meta, pl.BlockSpec(bias.shape, lambda i, c: (0, 0, 0))],
        out_specs=[qs, cache, cache],
        out_shape=[jax.ShapeDtypeStruct((b, s, aw), BF16)] + [jax.ShapeDtypeStruct(cache_shape, F32)] * 2,
        scratch_shapes=[pltpu.VMEM((xrows, KV_EXPAND * LANES), BF16),
                        pltpu.VMEM((xrows, 2 * KV_EXPAND * LANES), BF16)],
        compiler_params=_params("arbitrary", "arbitrary"),
        name="attn_prompt",
    )(q, k, v, k_meta, v_meta, bias)


def _attn_sample_kernel(q_ref, kc_ref, vc_ref, kn_ref, vn_ref, km_ref, vm_ref, bias_ref, o_ref, nk_ref, nv_ref):
    tk = kc_ref.shape[2] + kn_ref.shape[1] + km_ref.shape[0]
    zeros = jnp.zeros((bias_ref.shape[1] - tk, km_ref.shape[1]), F32)
    for s in range(q_ref.shape[0]):
        kc, vc, kn, vn = _load_heads(kc_ref, s), _load_heads(vc_ref, s), kn_ref[s], vn_ref[s]
        kall = _expand_kv(jnp.concatenate([kc, kn, km_ref[...], zeros], axis=0))
        vall = _expand_kv(jnp.concatenate([vc, vn, vm_ref[...], zeros], axis=0), with_ones=True)
        outs = _attn_core(q_ref[s], kall, vall, bias_ref[...])
        for ci, o in enumerate(outs):
            o_ref[s, :, ci * LANES:(ci + 1) * LANES] = o.astype(o_ref.dtype)
        n_new = kn.shape[0]
        _store_heads(nk_ref, jnp.concatenate([kc[n_new:], kn], axis=0), s)
        _store_heads(nv_ref, jnp.concatenate([vc[n_new:], vn], axis=0), s)


def _attn_sample(q, k_cache, v_cache, k_new, v_new, k_meta, v_meta, table, sinks):
    bd, sd, aw = q.shape
    kvw = k_new.shape[-1]
    win = k_cache.shape[2]
    sb = SAMPLE_STREAMS_PER_STEP if bd % SAMPLE_STREAMS_PER_STEP == 0 else 1
    cache = pl.BlockSpec((1, sb) + k_cache.shape[2:], lambda i: (0, i, 0, 0, 0))
    qpos = N_META + PAST_LEN + np.arange(sd)
    kpos = np.concatenate([N_META + PAST_LEN - win + np.arange(win), qpos, np.arange(N_META)])
    bias = _pad_keys(_bias_rows(table, _t5_bucket_np(kpos[None, :] - qpos[:, None])), sinks)
    per = lambda n, w: pl.BlockSpec((sb, n, w), lambda i: (i, 0, 0))
    meta = pl.BlockSpec((N_META, kvw), lambda i: (0, 0))
    return pl.pallas_call(
        _attn_sample_kernel,
        grid=(bd // sb,),
        in_specs=[per(sd, aw), cache, cache, per(sd, kvw), per(sd, kvw), meta, meta,
                  pl.BlockSpec(bias.shape, lambda i: (0, 0))],
        out_specs=[per(sd, aw), cache, cache],
        out_shape=[jax.ShapeDtypeStruct((bd, sd, aw), BF16)] + [jax.ShapeDtypeStruct(k_cache.shape, F32)] * 2,
        compiler_params=_params("arbitrary"),
        name="attn_sample",
    )(q, k_cache, v_cache, k_new, v_new, k_meta, v_meta, bias)


ROUTE_E1, ROUTE_E2, ROUTE_R1, ROUTE_R2, ROUTE_W1, ROUTE_W2 = range(6)
META_FIELDS = 2 * TOP_K


def _merge_kernel(hw, x_ref, yr_ref, at_ref, sga_ref, sgb_ref, wb_ref, wo_ref, wn_ref, wr_ref, br_ref, cnt0_ref, tri_ref,
                  h_ref, xn_ref, route_ref, meta_ref, cnt_ref, carry_ref):
    @pl.when(pl.program_id(0) == 0)
    def _():
        carry_ref[...] = cnt0_ref[...]

    logits = _project_rows(hw, x_ref, yr_ref, at_ref, sga_ref, sgb_ref, wb_ref, wo_ref, wn_ref, wr_ref, br_ref,
                           h_ref, xn_ref)
    _route_rows(logits, tri_ref, route_ref, meta_ref, carry_ref)
    cnt_ref[...] = carry_ref[...]


def _project_rows(hw, x_ref, yr_ref, at_ref, sga_ref, sgb_ref, wb_ref, wo_ref, wn_ref, wr_ref, br_ref, h_ref, xn_ref):
    br = _dot(yr_ref[...], wb_ref[0:hw, :])
    ba = _dot(at_ref[...], wb_ref[hw:, :])
    merged = sga_ref[...].astype(F32) * br + sgb_ref[...].astype(F32) * ba
    h = x_ref[...] + _dot(merged.astype(BF16), wo_ref[...])
    h_ref[...] = h
    ms = jnp.mean(h * h, axis=-1, keepdims=True)
    xn = h * lax.rsqrt(ms + EPS) * wn_ref[...]
    _store_row_tiles(xn_ref, xn)

    x_hi = xn.astype(BF16)
    x_lo = (xn - x_hi.astype(F32)).astype(BF16)
    both = _dot(x_hi, wr_ref[...])
    return both[:, :LANES] + both[:, LANES:] + _dot(x_lo, wr_ref[:, :LANES]) + br_ref[...]


def _route_rows(logits, tri_ref, route_ref, meta_ref, carry_ref):
    tm = logits.shape[0]
    lt = logits.T
    row_i = lax.broadcasted_iota(I32, (LANES, tm), 0)
    row = row_i.astype(F32)
    group_of_row = (row_i >> int(math.log2(EXPERTS_PER_GROUP))).astype(F32)
    ninf = -jnp.inf
    first = lambda hit, idx: jnp.min(jnp.where(hit, idx, float(LANES)), axis=0, keepdims=True)
    gmask = (row_i >= N_EXPERTS) & (row_i < N_EXPERTS + N_GROUPS)
    gl = jnp.where(gmask, lt, ninf)
    gmax = jnp.max(gl, axis=0, keepdims=True)
    gidx = first(gl == gmax, row - N_EXPERTS)
    gval = 1.0 / jnp.sum(jnp.exp(gl - gmax), axis=0, keepdims=True)
    emask = (row_i < N_EXPERTS) & (group_of_row == gidx)
    el = jnp.where(emask, lt, ninf)
    m1 = jnp.max(el, axis=0, keepdims=True)
    i1 = first(el == m1, row)
    el2 = jnp.where(row == i1, ninf, el)
    m2 = jnp.max(el2, axis=0, keepdims=True)
    i2 = first(el2 == m2, row)
    e21 = jnp.exp(m2 - m1)
    w1 = gval / (1.0 + e21)
    w2 = gval * e21 / (1.0 + e21)

    sel1 = row == i1
    sel2 = row == i2
    oh = (sel1 | sel2).astype(BF16)
    before = _dot(oh, tri_ref[...]) + carry_ref[...]
    r1 = jnp.sum(jnp.where(sel1, before, 0.0), axis=0, keepdims=True)
    r2 = jnp.sum(jnp.where(sel2, before, 0.0), axis=0, keepdims=True)
    carry_ref[...] = carry_ref[...] + jnp.sum(oh.astype(F32), axis=1, keepdims=True)

    rec = jnp.zeros((LANES, tm), F32)
    for slot, val in ((ROUTE_E1, i1), (ROUTE_E2, i2), (ROUTE_W1, w1), (ROUTE_W2, w2), (ROUTE_R1, r1), (ROUTE_R2, r2)):
        rec = jnp.where(row_i == slot, val, rec)
    route_ref[...] = rec.T
    meta_ref[0] = rec[ROUTE_E1:ROUTE_E1 + META_FIELDS].astype(I32)


def _merge(x, y_rec, att, sga, sgb, wb, wo, w_norm, w_router, b_router, cnt0):
    t, d = x.shape
    hw = y_rec.shape[1]
    tm = PROJ_TILE if t % PROJ_TILE == 0 else t
    row = lambda w: pl.BlockSpec((tm, w), lambda i: (i, 0))
    const = lambda a, b: pl.BlockSpec((a, b), lambda i: (0, 0))
    return pl.pallas_call(
        functools.partial(_merge_kernel, hw),
        grid=(t // tm,),
        in_specs=[row(d), row(hw), row(att.shape[1]), row(d), row(d), const(*wb.shape), const(d, d), const(1, d),
                  const(d, 2 * LANES), const(1, LANES), const(LANES, 1), const(tm, tm)],
        out_specs=[row(d), pl.BlockSpec((tm * d // LANES, LANES), lambda i: (i, 0)), row(LANES),
                   pl.BlockSpec((1, META_FIELDS, tm), lambda i: (i, 0, 0)), const(LANES, 1)],
        out_shape=[jax.ShapeDtypeStruct((t, d), F32), jax.ShapeDtypeStruct((t * d // LANES, LANES), F32),
                   jax.ShapeDtypeStruct((t, LANES), F32), jax.ShapeDtypeStruct((t // tm, META_FIELDS, tm), I32),
                   jax.ShapeDtypeStruct((LANES, 1), F32)],
        scratch_shapes=[pltpu.VMEM((LANES, 1), F32)],
        compiler_params=_params("arbitrary"),
        name="merge_route",
    )(x, y_rec, att, sga, sgb, wb, wo, w_norm.reshape(1, d), w_router, b_router, cnt0,
      jnp.asarray(np.triu(np.ones((tm, tm)), k=1), dtype=BF16))


def _row_copy(src, dst, sem):
    return pltpu.make_async_copy(src, dst, sem)


ROW_UNROLL = 8


def _tile_rows(ref, row, sub):
    return ref.at[pl.ds(pl.multiple_of(row * sub, sub), sub)]


def _scatter_kernel(n_tok, sub, first, slot_ref, pstart_ref, pend_ref, x_ref, *rest):
    xs_ref, zero_ref, sem, zsem, tsem = rest[-5:]

    def zero_blocks(blocks, zs, wait):
        for cond, row in blocks:
            @pl.when(cond)
            def _():
                at = row * sub if isinstance(row, int) else pl.multiple_of(row * sub, EXPERT_TILE * sub)
                cp = _row_copy(zero_ref, xs_ref.at[pl.ds(at, EXPERT_TILE * sub)], zs)
                cp.wait() if wait else cp.start()

    if first:
        n_rows = xs_ref.shape[0] // sub
        tails = [(pend_ref[e] > pstart_ref[e], pend_ref[e] - EXPERT_TILE) for e in range(N_EXPERTS)]
        unused = [(n_rows - (j + 1) * EXPERT_TILE >= pend_ref[N_EXPERTS - 1], n_rows - (j + 1) * EXPERT_TILE)
                  for j in range(N_EXPERTS)]

        @pl.when(pl.program_id(0) == 0)
        def _():
            zero_ref[...] = jnp.zeros_like(zero_ref)
            zero_blocks(tails, zsem, False)
            zero_blocks(unused, tsem, False)
            zero_blocks(tails, zsem, True)

    def issue(grp, _):
        for u in range(ROW_UNROLL):
            r = grp * ROW_UNROLL + u
            for k in range(TOP_K):
                _row_copy(_tile_rows(x_ref, r, sub), _tile_rows(xs_ref, slot_ref[0, k, r], sub),
                          sem).start(priority=(u * TOP_K + k) % 2)
        return 0

    lax.fori_loop(0, n_tok // ROW_UNROLL, issue, 0)
    for k in range(TOP_K):
        _row_copy(x_ref, xs_ref.at[pl.ds(0, n_tok * sub)], sem).wait()

    if first:
        @pl.when(pl.program_id(0) == 0)
        def _():
            zero_blocks(unused, tsem, True)


def _scatter_rows(x, slots, pstarts, pends, xs, rows):
    n, fields, tm = slots.shape
    sub = x.shape[0] // (n * tm)
    first = xs is None
    smem = pl.BlockSpec(memory_space=pltpu.SMEM)
    return pl.pallas_call(
        functools.partial(_scatter_kernel, tm, sub, first),
        grid=(n,),
        in_specs=[pl.BlockSpec((1, fields, tm), lambda i: (i, 0, 0), memory_space=pltpu.SMEM), smem, smem,
                  pl.BlockSpec((tm * sub, LANES), lambda i: (i, 0))]
        + ([] if first else [pl.BlockSpec(memory_space=pl.ANY)]),
        out_specs=pl.BlockSpec(memory_space=pl.ANY),
        out_shape=jax.ShapeDtypeStruct((rows * sub, LANES), F32),
        scratch_shapes=[pltpu.VMEM((EXPERT_TILE * sub, LANES), F32)] + [pltpu.SemaphoreType.DMA(())] * 3,
        input_output_aliases={} if first else {4: 0},
        compiler_params=_params("arbitrary"),
        name="moe_scatter",
    )(slots, pstarts, pends, x, *([] if first else [xs]))


def _expert_kernel(tm, sub, be_ref, nb_ref, nxt_ref, x_ref, wg_ref, wu_ref, wd_ref, y_ref,
                   raw_g, raw_u, raw_d, wgb_ref, wub_ref, wdb_ref, sem, slot_ref):
    i = pl.program_id(0)
    used = i < nb_ref[0]
    first = used & ((i == 0) | (be_ref[i] != be_ref[jnp.maximum(i - 1, 0)]))

    def fetch(e, slot):
        return [_row_copy(w.at[e], raw.at[slot], sem.at[slot])
                for w, raw in ((wg_ref, raw_g), (wu_ref, raw_u), (wd_ref, raw_d))]

    @pl.when(i == 0)
    def _():
        slot_ref[0] = 0
        for cp in fetch(be_ref[0], 0):
            cp.start()

    @pl.when(first)
    def _():
        slot = slot_ref[0]
        for cp in fetch(be_ref[i], slot):
            cp.wait()
        wgb_ref[...] = raw_g[slot].astype(BF16)
        wub_ref[...] = raw_u[slot].astype(BF16)
        wdb_ref[...] = raw_d[slot].astype(BF16)
        slot_ref[0] = 1 - slot

        @pl.when(nxt_ref[i] >= 0)
        def _():
            for cp in fetch(nxt_ref[i], 1 - slot):
                cp.start()

    @pl.when(used)
    def _():
        x = _load_row_tiles(x_ref, tm, sub).astype(BF16)
        g = _dot(x, wgb_ref[...])
        u = _dot(x, wub_ref[...])
        hmid = (g * _sigmoid(g) * u).astype(BF16)
        _store_row_tiles(y_ref, _dot(hmid, wdb_ref[...]))

    @pl.when(jnp.logical_not(used))
    def _():
        y_ref[...] = jnp.zeros_like(y_ref)


def _experts(xs, block_e, n_used, nxt_e, wg, wu, wd):
    _, d, ff = wg.shape
    sub = d // LANES
    tm = EXPERT_TILE
    blk = pl.BlockSpec((tm * sub, LANES), lambda i, be, nb, nx: (i, 0))
    hbm = pl.BlockSpec(memory_space=pl.ANY)
    grid_spec = pltpu.PrefetchScalarGridSpec(
        num_scalar_prefetch=3,
        grid=(xs.shape[0] // (tm * sub),),
        in_specs=[blk, hbm, hbm, hbm],
        out_specs=blk,
        scratch_shapes=[pltpu.VMEM((2, d, ff), F32), pltpu.VMEM((2, d, ff), F32), pltpu.VMEM((2, ff, d), F32),
                        pltpu.VMEM((d, ff), BF16), pltpu.VMEM((d, ff), BF16), pltpu.VMEM((ff, d), BF16),
                        pltpu.SemaphoreType.DMA((2,)), pltpu.SMEM((1,), I32)],
    )
    return pl.pallas_call(
        functools.partial(_expert_kernel, tm, sub),
        grid_spec=grid_spec,
        out_shape=jax.ShapeDtypeStruct(xs.shape, F32),
        compiler_params=_params("arbitrary"),
        name="moe_experts",
    )(block_e, n_used, nxt_e, xs, wg, wu, wd)


def _combine_kernel(n_tok, sub, scur_ref, snext_ref, h_ref, route_ref, ys_ref, o_ref, buf, sem):
    i = pl.program_id(0)
    n = pl.num_programs(0)
    slot = i % 2

    def start(slot_ref, s):
        def issue(grp, _):
            for u in range(ROW_UNROLL):
                r = grp * ROW_UNROLL + u
                for k in range(TOP_K):
                    _row_copy(_tile_rows(ys_ref, slot_ref[0, k, r], sub), _tile_rows(buf.at[s, k], r, sub),
                              sem.at[s]).start(priority=(u * TOP_K + k) % 2)
            return 0
        lax.fori_loop(0, n_tok // ROW_UNROLL, issue, 0)

    @pl.when(i == 0)
    def _():
        start(scur_ref, 0)

    @pl.when(i + 1 < n)
    def _():
        start(snext_ref, 1 - slot)

    for k in range(TOP_K):
        _row_copy(ys_ref.at[pl.ds(0, n_tok * sub)], buf.at[slot, k], sem.at[slot]).wait()
    route = route_ref[...]
    w1 = route[:, ROUTE_W1:ROUTE_W1 + 1]
    w2 = route[:, ROUTE_W2:ROUTE_W2 + 1]
    for j in range(sub):
        cols = slice(j * LANES, (j + 1) * LANES)
        part = lambda k: buf[slot, k, pl.ds(j, n_tok, stride=sub), :]
        o_ref[:, cols] = h_ref[:, cols] + (part(0) * w1 + part(1) * w2)


def _combine(h, route, slots, ys):
    t, d = h.shape
    n, fields, tm = slots.shape
    sub = d // LANES
    mspec = lambda f: pl.BlockSpec((1, fields, tm), f, memory_space=pltpu.SMEM)
    return pl.pallas_call(
        functools.partial(_combine_kernel, tm, sub),
        grid=(n,),
        in_specs=[mspec(lambda i: (i, 0, 0)), mspec(lambda i: (jnp.minimum(i + 1, n - 1), 0, 0)),
                  pl.BlockSpec((tm, d), lambda i: (i, 0)), pl.BlockSpec((tm, LANES), lambda i: (i, 0)),
                  pl.BlockSpec(memory_space=pl.ANY)],
        out_specs=pl.BlockSpec((tm, d), lambda i: (i, 0)),
        out_shape=jax.ShapeDtypeStruct((t, d), F32),
        scratch_shapes=[pltpu.VMEM((2, TOP_K, tm * sub, LANES), F32), pltpu.SemaphoreType.DMA((2,))],
        compiler_params=_params("arbitrary"),
        name="moe_combine",
    )(slots, slots, h, route, ys)


def _moe(parts, counts, wg, wu, wd):
    tm = EXPERT_TILE
    n_assign = sum(part[0].shape[0] for part in parts) * TOP_K
    n_blocks = -(-(n_assign + N_EXPERTS * (tm - 1)) // tm)
    counts = counts[:N_EXPERTS, 0].astype(I32)
    pcounts = (counts + tm - 1) // tm * tm
    pends = jnp.cumsum(pcounts)
    pstarts = pends - pcounts
    block_start = jnp.arange(n_blocks, dtype=I32) * tm
    block_e = jnp.minimum(jnp.sum((pends[None, :] <= block_start[:, None]).astype(I32), axis=1), N_EXPERTS - 1)
    n_used = pends[-1:] // tm
    ids = jnp.arange(N_EXPERTS, dtype=I32)
    later = (ids[None, :] > ids[:, None]) & (pcounts[None, :] > 0)
    nxt_of = jnp.min(jnp.where(later, ids[None, :], N_EXPERTS), axis=1)
    nxt_of = jnp.where(nxt_of < N_EXPERTS, nxt_of, -1)
    nxt_e = jnp.sum(jnp.where(block_e[:, None] == ids[None, :], nxt_of[None, :], 0), axis=1)
    xs = None
    slots = []
    for _, xn, _, meta in parts:
        assert meta.shape[-1] % ROW_UNROLL == 0
        e, pos = meta[:, :TOP_K], meta[:, TOP_K:]
        seg = jnp.sum(jnp.where(e[..., None] == jnp.arange(N_EXPERTS, dtype=I32), pstarts, 0), axis=-1)
        slots.append(seg + pos)
        xs = _scatter_rows(xn, slots[-1], pstarts, pends, xs, n_blocks * tm)
    ys = _experts(xs, block_e, n_used, nxt_e, wg, wu, wd)
    return [_combine(h, route, s, ys) for (h, _, route, _), s in zip(parts, slots)]


def kernel(x_prompt, x_sample, cache_swa_k, cache_swa_v, state_hgrn, meta_tokens, rel_bias_table, hgrn_lower_bounds, w_norm_mix, w_in, hgrn_out_norm, q_norm, k_norm, attn_sinks, w_branch, w_out, w_norm_ffn, w_router_group, b_router_group, w_router_expert, b_router_expert, w_expert_gate, w_expert_up, w_expert_down):
    b, s, d = x_prompt.shape
    bd, sd, _ = x_sample.shape
    depth, _, heads, dk, dv = state_hgrn.shape
    assert depth == 1 and heads == HGRN_HEADS and dk == dv
    hw = heads * dk
    aw = ATTN_HEADS * HEAD_DIM
    kvw = KV_HEADS * HEAD_DIM
    assert w_in.shape[-1] == 4 * hw + aw + 2 * kvw + 2 * d
    assert s % HGRN_CHUNK == 0 and s % CHUNK == 0 and sd == N_META and N_EXPERTS + N_GROUPS <= LANES
    l = 0

    p = jax.nn.softmax(hgrn_lower_bounds.astype(F32), axis=0)
    lb = jnp.cumsum(p, axis=0)[l + 1] - p[0]

    w_in_b = w_in[l].astype(BF16)
    proj = functools.partial(_inproj, w_norm=w_norm_mix[l], w_in_bf16=w_in_b, q_gain=q_norm[l], k_gain=k_norm[l],
                             hw=hw, aw=aw, kvw=kvw)
    x_small = jnp.concatenate([x_sample.reshape(bd * sd, d), meta_tokens.astype(F32)], axis=0)
    qr_s, z_s, hv_s, hg_s, qa_s, k_s, v_s, sga_s, sgb_s = proj(x_small)
    qr_p, z_p, hv_p, hg_p, qa_p, k_p, v_p, sga_p, sgb_p = proj(x_prompt.reshape(b * s, d))
    ns = bd * sd
    k_meta, v_meta = k_s[ns:], v_s[ns:]

    streams = lambda a, n: a.reshape(n, -1, a.shape[-1])
    s0_small = jnp.concatenate([state_hgrn[l].astype(F32), jnp.zeros((1, heads, dk, dv), F32)], axis=0)
    y_small, st_small = _hgrn(streams(qr_s, bd + 1), streams(z_s, bd + 1), streams(hv_s, bd + 1),
                              streams(hg_s, bd + 1), lb, hgrn_out_norm[l], s0_small, sd)
    y_p, st_p = _hgrn(streams(qr_p, b), streams(z_p, b), streams(hv_p, b), streams(hg_p, b), lb,
                      hgrn_out_norm[l], st_small, HGRN_CHUNK, shared_s0=bd)

    table = rel_bias_table.astype(F32)
    att_p, new_k_p, new_v_p = _attn_prompt(streams(qa_p, b), streams(k_p, b), streams(v_p, b), k_meta, v_meta,
                                           table, attn_sinks[l])
    k_new, v_new = k_s[:ns].reshape(bd, sd, kvw), v_s[:ns].reshape(bd, sd, kvw)
    att_s, new_k_s, new_v_s = _attn_sample(qa_s[:ns].reshape(bd, sd, aw), cache_swa_k[l:l + 1].astype(F32),
                                           cache_swa_v[l:l + 1].astype(F32), k_new, v_new, k_meta, v_meta,
                                           table, attn_sinks[l])

    wb = w_branch[l].astype(BF16)
    wo = w_out[l].astype(BF16)
    w_router = jnp.pad(jnp.concatenate([w_router_expert[l], w_router_group[l]], axis=1).astype(F32),
                       ((0, 0), (0, LANES - N_EXPERTS - N_GROUPS)))
    b_router = jnp.pad(jnp.concatenate([b_router_expert[l], b_router_group[l]]).astype(F32),
                       (0, LANES - N_EXPERTS - N_GROUPS)).reshape(1, LANES)
    w_router_hi = w_router.astype(BF16)
    w_router_lo = (w_router - w_router_hi.astype(F32)).astype(BF16)
    w_router = jnp.concatenate([w_router_hi, w_router_lo], axis=1)
    merge = functools.partial(_merge, wb=wb, wo=wo, w_norm=w_norm_ffn[l], w_router=w_router, b_router=b_router)
    *part_p, cnt_p = merge(x_prompt.reshape(b * s, d), y_p.reshape(b * s, hw), att_p.reshape(b * s, aw),
                           sga_p, sgb_p, cnt0=jnp.zeros((LANES, 1), F32))
    *part_s, cnt_s = merge(x_sample.reshape(ns, d), y_small[:bd].reshape(ns, hw), att_s.reshape(ns, aw),
                           sga_s[:ns], sgb_s[:ns], cnt0=cnt_p)

    out_p, out_s = _moe([part_p, part_s], cnt_s, w_expert_gate[l], w_expert_up[l], w_expert_down[l])

    return (out_p.reshape(b, s, d), out_s.reshape(bd, sd, d), new_k_p, new_v_p, st_p[None],
            new_k_s, new_v_s, st_small[:bd][None])
```

```python
import functools
import math

import numpy as np
import jax
import jax.numpy as jnp
from jax import lax
from jax.experimental import pallas as pl
from jax.experimental.pallas import tpu as pltpu

F32 = jnp.float32
BF16 = jnp.bfloat16
I32 = jnp.int32

CHUNK = 64
N_META = 16
PAST_LEN = 2048
EPS = 1e-6
HGRN_HEADS = 4
ATTN_HEADS = 8
KV_HEADS = 2
HEAD_DIM = 64
GQA_GROUP = ATTN_HEADS // KV_HEADS
WINDOW = 128
WINDOW_CHUNKS = WINDOW // CHUNK
NUM_BUCKETS = 32
MAX_DISTANCE = 128
N_GROUPS = 4
EXPERTS_PER_GROUP = 8
N_EXPERTS = N_GROUPS * EXPERTS_PER_GROUP
TOP_K = 2

LANES = 128
MXU_WIDTH = 256
VMEM_LIMIT = 56 * 1024 * 1024

INPROJ_TILE = 512
PROJ_TILE = 512
HGRN_CHUNK = 128
HGRN_CHUNKS_PER_STEP = 8
ATTN_CHUNKS_PER_STEP = 16
SAMPLE_STREAMS_PER_STEP = 4
EXPERT_TILE = 512


LOG2E = math.log2(math.e)


def _sigmoid(x):
    return 1.0 / (1.0 + jnp.exp(-x))


def _split3(x):
    hi = x.astype(BF16)
    r1 = x - hi.astype(F32)
    mid = r1.astype(BF16)
    lo = (r1 - mid.astype(F32)).astype(BF16)
    return hi, mid, lo


def _dot(a, b):
    return jnp.dot(a, b, preferred_element_type=F32)


def _dot_nt(a, b):
    return lax.dot_general(a, b, (((1,), (1,)), ((), ())), preferred_element_type=F32)


def _dot_tn(a, b):
    return lax.dot_general(a, b, (((0,), (0,)), ((), ())), preferred_element_type=F32)


SUBLANES = 8


def _store_row_tiles(ref, x):
    n, d = x.shape
    sub = d // LANES
    for j in range(sub):
        ref[pl.ds(j, n, stride=sub), :] = x[:, j * LANES:(j + 1) * LANES]


def _load_row_tiles(ref, n, sub):
    return jnp.concatenate([ref[pl.ds(j, n, stride=sub), :] for j in range(sub)], axis=1)


def _params(*sem):
    return pltpu.CompilerParams(dimension_semantics=sem, vmem_limit_bytes=VMEM_LIMIT)


def _inproj_kernel(hw, aw, kvw, d, x_ref, wn_ref, w_ref, qg_ref, kg_ref, bdq_ref, bdk_ref, lb_ref,
                   qr_ref, lf_ref, kin_ref, hv_ref, hg_ref, qa_ref, k_ref, v_ref, sga_ref, sgb_ref):
    x = x_ref[...]
    ms = jnp.mean(x * x, axis=-1, keepdims=True)
    xn = (x * lax.rsqrt(ms + EPS) * wn_ref[...]).astype(BF16)

    def seg(a, b):
        return _dot(xn, w_ref[:, a:b])

    def head_rms(a, bd_ref, gain):
        sq = (a * a).astype(BF16)
        wb = bd_ref.shape[0]
        m = jnp.concatenate([_dot(sq[:, c:c + wb], bd_ref[...]) for c in range(0, a.shape[1], wb)], axis=1)
        return a * lax.rsqrt(m + EPS) * gain

    o = 0
    hq = seg(o, o + hw)
    qr_ref[...] = (hq * _sigmoid(hq) * (hw // HGRN_HEADS) ** -0.5).astype(BF16)
    o += hw
    z = seg(o, o + hw)
    lb = lb_ref[...]
    e = jnp.exp(-jnp.abs(z))
    r = 1.0 / (1.0 + e)
    pos = z >= 0
    lf_ref[...] = jnp.log2(lb + (1.0 - lb) * jnp.where(pos, r, e * r))
    kin_ref[...] = ((1.0 - lb) * jnp.where(pos, e * r, r)).astype(BF16)
    o += hw
    hv_ref[...] = seg(o, o + hw).astype(BF16)
    o += hw
    hg = seg(o, o + hw)
    hg_ref[...] = (hg * _sigmoid(hg)).astype(BF16)
    o += hw
    aq = seg(o, o + aw)
    qa_ref[...] = (head_rms(aq, bdq_ref, qg_ref[...]) * (HEAD_DIM ** -0.5 * LOG2E)).astype(BF16)
    o += aw
    kv = seg(o, o + 2 * kvw)
    k_ref[...] = head_rms(kv[:, :kvw], bdk_ref, kg_ref[...])
    v_ref[...] = kv[:, kvw:]
    o += 2 * kvw
    sga_ref[...] = _sigmoid(seg(o, o + d)).astype(BF16)
    o += d
    sgb_ref[...] = _sigmoid(seg(o, o + d)).astype(BF16)


def _block_diag_mean(width, group):
    i = np.arange(width)
    return jnp.asarray((i[:, None] // group == i[None, :] // group) / group, dtype=BF16)


def _inproj(x, w_norm, w_in_bf16, q_gain, k_gain, lb, hw, aw, kvw):
    t, d = x.shape
    tm = INPROJ_TILE if t % INPROJ_TILE == 0 else t
    cols = w_in_bf16.shape[1]
    row = lambda w: pl.BlockSpec((tm, w), lambda i: (i, 0))
    const = lambda a, b: pl.BlockSpec((a, b), lambda i: (0, 0))
    outs = [(hw, BF16), (hw, F32), (hw, BF16), (hw, BF16), (hw, BF16), (aw, BF16), (kvw, F32), (kvw, F32),
            (d, BF16), (d, BF16)]
    bdq = min(aw, MXU_WIDTH)
    assert aw % bdq == 0
    return pl.pallas_call(
        functools.partial(_inproj_kernel, hw, aw, kvw, d),
        grid=(t // tm,),
        in_specs=[row(d), const(1, d), const(d, cols), const(1, aw), const(1, kvw), const(bdq, bdq), const(kvw, kvw),
                  const(1, hw)],
        out_specs=[row(w) for w, _ in outs],
        out_shape=[jax.ShapeDtypeStruct((t, w), dt) for w, dt in outs],
        compiler_params=_params("arbitrary"),
        name="inproj",
    )(x, w_norm.reshape(1, d), w_in_bf16,
      jnp.tile(q_gain, aw // HEAD_DIM).reshape(1, aw), jnp.tile(k_gain, kvw // HEAD_DIM).reshape(1, kvw),
      _block_diag_mean(bdq, HEAD_DIM), _block_diag_mean(kvw, HEAD_DIM), lb.reshape(1, hw))


def _hgrn_consts(L):
    t = np.arange(L)
    u = t[None, :]
    blocks = [u <= t[:, None], u > t[:, None]]
    levels = []
    m = L // 2
    while m >= 1:
        levels.append(m)
        m //= 2
    lvl = np.full((L, L), -1, np.int32)
    lvl[t, t] = len(levels)
    isq_cols = []
    for j, m in enumerate(levels):
        bnd = (t // (2 * m)) * (2 * m) + m - 1
        isq = (t % (2 * m)) >= m
        cq = isq[:, None] & (u > bnd[:, None]) & (u <= t[:, None])
        ck = (~isq)[:, None] & (u > t[:, None]) & (u <= bnd[:, None])
        blocks.append(cq | ck)
        same = (t[:, None] // (2 * m)) == (t[None, :] // (2 * m))
        lvl[same & isq[:, None] & (~isq)[None, :]] = j
        isq_cols.append(isq)
    c = np.concatenate(blocks, axis=0).astype(np.float32)
    isq = np.stack(isq_cols, axis=1).astype(np.float32)
    isq = np.pad(isq, ((0, 0), (0, LANES - isq.shape[1])))
    c2 = np.tile(c, (1, 2))
    return jnp.asarray(c2, dtype=BF16), jnp.asarray(np.tile(lvl, (1, 2))), jnp.asarray(isq), len(levels)


def _hgrn_kernel(L, nlev, heads, dk, qr_ref, lf_ref, kin_ref, hv_ref, hg_ref, og_ref, c_ref, lvl_ref, isq_ref,
                 s0_ref, y_ref, sout_ref, st_ref):
    c = pl.program_id(1)

    @pl.when(c == 0)
    def _():
        for h in range(heads):
            st_ref[h] = s0_ref[0, h].T

    for cc in range(lf_ref.shape[1] // L):
        _hgrn_chunk(L, nlev, heads, dk, slice(cc * L, (cc + 1) * L), qr_ref, lf_ref, kin_ref, hv_ref, hg_ref, og_ref,
                    c_ref, lvl_ref, isq_ref, y_ref, st_ref)

    @pl.when(c == pl.num_programs(1) - 1)
    def _():
        for h in range(heads):
            sout_ref[0, h] = st_ref[h].T


def _hgrn_chunk(L, nlev, heads, dk, rows, qr_ref, lf_ref, kin_ref, hv_ref, hg_ref, og_ref, c_ref, lvl_ref, isq_ref,
                y_ref, st_ref):
    kin = kin_ref[0, rows, :].astype(F32)
    q = qr_ref[0, rows, :].astype(F32)

    hi, mid, _ = _split3(lf_ref[0, rows, :])
    ex = jnp.exp2(_dot(c_ref[...], jnp.concatenate([hi, mid], axis=0)))
    e_b = ex[0:L]
    e_rev = ex[L:2 * L]

    q_in = (q * e_b).astype(BF16)
    k_out = (kin * e_rev).astype(BF16)
    q_b = q.astype(BF16)
    k_b = kin.astype(BF16)
    xs = []
    for j in range(nlev):
        m = L >> (j + 1)
        if m % SUBLANES == 0:
            qk = jnp.concatenate([(q if blk % 2 else kin)[blk * m:(blk + 1) * m] for blk in range(L // m)], axis=0)
        else:
            qk = jnp.where(isq_ref[:, j:j + 1] > 0.5, q, kin)
        xs.append((qk * ex[(2 + j) * L:(3 + j) * L]).astype(BF16))
    lvl = lvl_ref[...]
    v = hv_ref[0, rows, :]
    g = hg_ref[0, rows, :].astype(F32)
    og = og_ref[...]

    def block_diag(x):
        zero = jnp.zeros((x.shape[0], dk), x.dtype)
        return jnp.concatenate([jnp.concatenate([x[:, :dk], zero], axis=1),
                                jnp.concatenate([zero, x[:, dk:]], axis=1)], axis=0)

    for pair in range(heads // 2):
        sl = slice(2 * pair * dk, 2 * (pair + 1) * dk)
        a = jnp.where(lvl == nlev, _dot_nt(q_b[:, sl], block_diag(k_b[:, sl])), 0.0)
        for j in range(nlev):
            xp = xs[j][:, sl]
            a = jnp.where(lvl == j, _dot_nt(xp, block_diag(xp)), a)
        st = jnp.concatenate([st_ref[2 * pair], st_ref[2 * pair + 1]], axis=1)
        vp = v[:, sl]
        o = _dot(a.astype(BF16), block_diag(vp)) + _dot_nt(q_in[:, sl], block_diag(st.astype(BF16)))
        for half in range(2):
            h = 2 * pair + half
            hs = slice(h * dk, (h + 1) * dk)
            st_ref[h] = st_ref[h] * e_b[L - 1:L, hs] + _dot_tn(v[:, hs], k_out[:, hs])
            oh = o[:, half * dk:(half + 1) * dk]
            ms = jnp.mean(oh * oh, axis=-1, keepdims=True)
            gh = g[:, hs]
            y_ref[0, rows, hs] = (oh * lax.rsqrt(ms + EPS) * og[:, hs] * gh).astype(BF16)


def _hgrn(qr, lf, kin, hv, hg, out_gain, s0, L, shared_s0=None):
    b, s, w = lf.shape
    heads, dk = s0.shape[1], s0.shape[2]
    cm, lvl, isq, nlev = _hgrn_consts(L)
    per_step = HGRN_CHUNKS_PER_STEP if s % (HGRN_CHUNKS_PER_STEP * L) == 0 else 1
    seq = pl.BlockSpec((1, per_step * L, w), lambda i, c: (i, c, 0))
    const = lambda a: pl.BlockSpec(a.shape, lambda i, c: (0,) * a.ndim)
    state = pl.BlockSpec((1, heads, dk, dk), lambda i, c: (i, 0, 0, 0))
    start = state if shared_s0 is None else pl.BlockSpec((1, heads, dk, dk), lambda i, c: (shared_s0, 0, 0, 0))
    og2 = jnp.tile(out_gain, heads).reshape(1, w)
    return pl.pallas_call(
        functools.partial(_hgrn_kernel, L, nlev, heads, dk),
        grid=(b, s // (per_step * L)),
        in_specs=[seq, seq, seq, seq, seq, const(og2), const(cm), const(lvl), const(isq), start],
        out_specs=[seq, state],
        out_shape=[jax.ShapeDtypeStruct((b, s, w), BF16), jax.ShapeDtypeStruct((b,) + s0.shape[1:], F32)],
        scratch_shapes=[pltpu.VMEM((heads, dk, dk), F32)],
        compiler_params=_params("arbitrary", "arbitrary"),
        name=f"hgrn_scan_{L}",
    )(qr, lf, kin, hv, hg, og2, cm, lvl, isq, s0)


def _t5_bucket_np(rel):
    half = NUM_BUCKETS // 2
    max_exact = half // 2
    assert (NUM_BUCKETS, MAX_DISTANCE) == (32, 128)
    n = np.abs(rel).astype(np.int64)
    nn = np.maximum(n, 1)
    k = np.zeros_like(nn)
    for j in range(1, 48):
        k = np.where(64 * (1 << j) <= nn * nn, j, k)
    large = np.minimum(max_exact + k, half - 1)
    return np.where(rel > 0, half, 0) + np.where(n < max_exact, n, large)


HEADS_PER_COL = LANES // HEAD_DIM
COLS_PER_GROUP = GQA_GROUP // HEADS_PER_COL
HEAD_ORDER = tuple(g * GQA_GROUP + col * HEADS_PER_COL + half
                   for g in range(KV_HEADS) for half in range(HEADS_PER_COL) for col in range(COLS_PER_GROUP))
KV_EXPAND = KV_HEADS * HEADS_PER_COL


def _expand_kv(x, with_ones=False):
    assert HEADS_PER_COL == 2 and KV_HEADS == 2 and x.shape[1] == LANES
    low = lax.broadcasted_iota(I32, x.shape, 1) < HEAD_DIM
    xr = pltpu.roll(x, HEAD_DIM, axis=1)
    zero = jnp.zeros_like(x)
    blocks = [jnp.where(low, x, zero), jnp.where(low, zero, xr), jnp.where(low, xr, zero), jnp.where(low, zero, x)]
    if with_ones:
        blocks = [b for blk in blocks for b in (blk, jnp.ones_like(x))]
    return jnp.concatenate(blocks, axis=1).astype(BF16)


def _attn_core(q, kx, vx, bias):
    tq = q.shape[0]
    scores = []
    for g in range(KV_HEADS):
        cols = [q[:, (g * COLS_PER_GROUP + c) * LANES:(g * COLS_PER_GROUP + c + 1) * LANES]
                for c in range(COLS_PER_GROUP)]
        qst = jnp.concatenate(cols, axis=0)
        for half in range(HEADS_PER_COL):
            blk = g * HEADS_PER_COL + half
            scores.append(_dot_nt(qst, kx[:, blk * LANES:(blk + 1) * LANES]))
    s = jnp.concatenate(scores, axis=0) + bias
    pb = jnp.exp2(s - jnp.max(s, axis=-1, keepdims=True)).astype(BF16)
    rows = COLS_PER_GROUP * tq
    outs = []
    for g in range(KV_HEADS):
        o = None
        for half in range(HEADS_PER_COL):
            blk = g * HEADS_PER_COL + half
            pv = _dot(pb[blk * rows:(blk + 1) * rows], vx[:, 2 * blk * LANES:2 * (blk + 1) * LANES])
            part = pv[:, :LANES] * (1.0 / pv[:, LANES:])
            o = part if o is None else o + part
        outs.extend(o[c * tq:(c + 1) * tq] for c in range(COLS_PER_GROUP))
    return outs


def _bias_rows(table, bucket):
    onehot = (jnp.asarray(bucket)[..., None] == jnp.arange(NUM_BUCKETS)).astype(F32)
    cols = jnp.stack([table[:, h] for h in HEAD_ORDER], axis=1)
    bias = jnp.einsum('...qkb,bh->...hqk', onehot, cols, precision=lax.Precision.HIGHEST)
    return bias.reshape(*bucket.shape[:-2], ATTN_HEADS * bucket.shape[-2], bucket.shape[-1]) * LOG2E


def _pad_keys(bias, sinks):
    tk = bias.shape[-1]
    n_pad = -tk % LANES or LANES
    tq = bias.shape[-2] // ATTN_HEADS
    sink = jnp.repeat(jnp.stack([sinks[h] for h in HEAD_ORDER]).astype(F32), tq) * LOG2E
    sink = jnp.broadcast_to(sink[:, None], bias.shape[:-1] + (1,))
    masked = jnp.full(bias.shape[:-1] + (n_pad - 1,), -jnp.inf, F32)
    return jnp.concatenate([bias, sink, masked], axis=-1)


def _store_heads(ref, x, stream=0):
    for g in range(KV_HEADS):
        ref[0, stream, :, g, :] = x[:, g * HEAD_DIM:(g + 1) * HEAD_DIM]


def _load_heads(ref, stream=0):
    return jnp.concatenate([ref[0, stream, :, g, :] for g in range(KV_HEADS)], axis=1)


def _attn_prompt_kernel(cb, q_ref, k_ref, v_ref, km_ref, vm_ref, bias_ref, o_ref, nk_ref, nv_ref,
                        kx_ref, vx_ref):
    step = pl.program_id(1)
    s_len = k_ref.shape[1]
    meta_at = WINDOW + s_len

    @pl.when(step == pl.num_programs(1) - 1)
    def _():
        _store_heads(nk_ref, k_ref[0, s_len - WINDOW:, :])
        _store_heads(nv_ref, v_ref[0, s_len - WINDOW:, :])

    @pl.when(step == 0)
    def _():
        piece = min(s_len, 512)
        for src, meta, dst in ((k_ref, km_ref, kx_ref), (v_ref, vm_ref, vx_ref)):
            expand = functools.partial(_expand_kv, with_ones=dst is vx_ref)
            blank = lambda n: expand(jnp.zeros((n, src.shape[2]), F32))
            dst[0:WINDOW] = blank(WINDOW)
            for r in range(0, s_len, piece):
                dst[WINDOW + r:WINDOW + r + piece] = expand(src[0, r:r + piece, :])
            dst[meta_at:meta_at + N_META] = expand(meta[...])
            dst[meta_at + N_META:] = blank(dst.shape[0] - meta_at - N_META)

    win = WINDOW + CHUNK
    tail = kx_ref.shape[0] - meta_at
    for j in range(cb):
        c = step * cb + j
        start = pl.multiple_of(c * CHUNK, CHUNK)
        kall = jnp.concatenate([kx_ref[pl.ds(start, win), :], kx_ref[meta_at:meta_at + tail, :]], axis=0)
        vall = jnp.concatenate([vx_ref[pl.ds(start, win), :], vx_ref[meta_at:meta_at + tail, :]], axis=0)
        rows = slice(j * CHUNK, (j + 1) * CHUNK)
        outs = _attn_core(q_ref[0, rows, :], kall, vall, bias_ref[jnp.minimum(c, bias_ref.shape[0] - 1)])
        for ci, o in enumerate(outs):
            o_ref[0, rows, ci * LANES:(ci + 1) * LANES] = o.astype(o_ref.dtype)


def _attn_prompt(q, k, v, k_meta, v_meta, table, sinks):
    b, s, aw = q.shape
    kvw = k.shape[-1]
    nc = s // CHUNK
    cb = ATTN_CHUNKS_PER_STEP if nc % ATTN_CHUNKS_PER_STEP == 0 else 1
    assert s % min(s, 512) == 0
    n_bias = 1
    while True:
        qpos = N_META + (n_bias - 1) * CHUNK
        if np.all(_t5_bucket_np(np.arange(N_META) - qpos) == _t5_bucket_np(np.arange(N_META) - qpos - 10 ** 6)):
            break
        n_bias += 1
    n_bias = min(max(n_bias, WINDOW_CHUNKS + 1), nc)
    cs = np.arange(n_bias)[:, None]
    qpos = N_META + cs * CHUNK + np.arange(CHUNK)[None]
    wpos = N_META + (cs - WINDOW_CHUNKS) * CHUNK + np.arange(WINDOW + CHUNK)[None]
    kpos = np.concatenate([wpos, np.broadcast_to(np.arange(N_META), (n_bias, N_META))], axis=1)
    valid = np.concatenate([wpos >= N_META, np.ones((n_bias, N_META), bool)], axis=1)
    bias = _bias_rows(table, _t5_bucket_np(kpos[:, None, :] - qpos[:, :, None]))
    bias = _pad_keys(jnp.where(valid[:, None, :], bias, -jnp.inf), sinks)
    qs =pl.BlockSpec((1, cb * CHUNK, aw), lambda i, c: (i, c, 0))
    kv = pl.BlockSpec((1, s, kvw), lambda i, c: (i, 0, 0))
    meta = pl.BlockSpec((N_META, kvw), lambda i, c: (0, 0))
    xrows = WINDOW + s + bias.shape[-1] - (WINDOW + CHUNK)
    cache_shape = (1, b, WINDOW, KV_HEADS, HEAD_DIM)
    cache = pl.BlockSpec((1, 1) + cache_shape[2:], lambda i, c: (0, i, 0, 0, 0))
    return pl.pallas_call(
        functools.partial(_attn_prompt_kernel, cb),
        grid=(b, nc // cb),
        in_specs=[qs, kv, kv, meta, meta, pl.BlockSpec(bias.shape, lambda i, c: (0, 0, 0))],
        out_specs=[qs, cache, cache],
        out_shape=[jax.ShapeDtypeStruct((b, s, aw), BF16)] + [jax.ShapeDtypeStruct(cache_shape, F32)] * 2,
        scratch_shapes=[pltpu.VMEM((xrows, KV_EXPAND * LANES), BF16),
                        pltpu.VMEM((xrows, 2 * KV_EXPAND * LANES), BF16)],
        compiler_params=_params("arbitrary", "arbitrary"),
        name="attn_prompt",
    )(q, k, v, k_meta, v_meta, bias)


def _attn_sample_kernel(q_ref, kc_ref, vc_ref, kn_ref, vn_ref, km_ref, vm_ref, bias_ref, o_ref, nk_ref, nv_ref):
    tk = kc_ref.shape[2] + kn_ref.shape[1] + km_ref.shape[0]
    zeros = jnp.zeros((bias_ref.shape[1] - tk, km_ref.shape[1]), F32)
    for s in range(q_ref.shape[0]):
        kc, vc, kn, vn = _load_heads(kc_ref, s), _load_heads(vc_ref, s), kn_ref[s], vn_ref[s]
        kall = _expand_kv(jnp.concatenate([kc, kn, km_ref[...], zeros], axis=0))
        vall = _expand_kv(jnp.concatenate([vc, vn, vm_ref[...], zeros], axis=0), with_ones=True)
        outs = _attn_core(q_ref[s], kall, vall, bias_ref[...])
        for ci, o in enumerate(outs):
            o_ref[s, :, ci * LANES:(ci + 1) * LANES] = o.astype(o_ref.dtype)
        n_new = kn.shape[0]
        _store_heads(nk_ref, jnp.concatenate([kc[n_new:], kn], axis=0), s)
        _store_heads(nv_ref, jnp.concatenate([vc[n_new:], vn], axis=0), s)


def _attn_sample(q, k_cache, v_cache, k_new, v_new, k_meta, v_meta, table, sinks):
    bd, sd, aw = q.shape
    kvw = k_new.shape[-1]
    win = k_cache.shape[2]
    sb = SAMPLE_STREAMS_PER_STEP if bd % SAMPLE_STREAMS_PER_STEP == 0 else 1
    cache = pl.BlockSpec((1, sb) + k_cache.shape[2:], lambda i: (0, i, 0, 0, 0))
    qpos = N_META + PAST_LEN + np.arange(sd)
    kpos = np.concatenate([N_META + PAST_LEN - win + np.arange(win), qpos, np.arange(N_META)])
    bias = _pad_keys(_bias_rows(table, _t5_bucket_np(kpos[None, :] - qpos[:, None])), sinks)
    per = lambda n, w: pl.BlockSpec((sb, n, w), lambda i: (i, 0, 0))
    meta = pl.BlockSpec((N_META, kvw), lambda i: (0, 0))
    return pl.pallas_call(
        _attn_sample_kernel,
        grid=(bd // sb,),
        in_specs=[per(sd, aw), cache, cache, per(sd, kvw), per(sd, kvw), meta, meta,
                  pl.BlockSpec(bias.shape, lambda i: (0, 0))],
        out_specs=[per(sd, aw), cache, cache],
        out_shape=[jax.ShapeDtypeStruct((bd, sd, aw), BF16)] + [jax.ShapeDtypeStruct(k_cache.shape, F32)] * 2,
        compiler_params=_params("arbitrary"),
        name="attn_sample",
    )(q, k_cache, v_cache, k_new, v_new, k_meta, v_meta, bias)


ROUTE_E1, ROUTE_E2, ROUTE_R1, ROUTE_R2, ROUTE_W1, ROUTE_W2 = range(6)
META_FIELDS = 2 * TOP_K


def _merge_kernel(hw, x_ref, yr_ref, at_ref, sga_ref, sgb_ref, wb_ref, wo_ref, wn_ref, wr_ref, br_ref, cnt0_ref, tri_ref,
                  h_ref, xn_ref, route_ref, meta_ref, cnt_ref, carry_ref):
    @pl.when(pl.program_id(0) == 0)
    def _():
        carry_ref[...] = cnt0_ref[...]

    logits = _project_rows(hw, x_ref, yr_ref, at_ref, sga_ref, sgb_ref, wb_ref, wo_ref, wn_ref, wr_ref, br_ref,
                           h_ref, xn_ref)
    _route_rows(logits, tri_ref, route_ref, meta_ref, carry_ref)
    cnt_ref[...] = carry_ref[...]


def _project_rows(hw, x_ref, yr_ref, at_ref, sga_ref, sgb_ref, wb_ref, wo_ref, wn_ref, wr_ref, br_ref, h_ref, xn_ref):
    br = _dot(yr_ref[...], wb_ref[0:hw, :])
    ba = _dot(at_ref[...], wb_ref[hw:, :])
    merged = sga_ref[...].astype(F32) * br + sgb_ref[...].astype(F32) * ba
    h = x_ref[...] + _dot(merged.astype(BF16), wo_ref[...])
    h_ref[...] = h
    ms = jnp.mean(h * h, axis=-1, keepdims=True)
    xn = h * lax.rsqrt(ms + EPS) * wn_ref[...]
    _store_row_tiles(xn_ref, xn)

    x_hi = xn.astype(BF16)
    x_lo = (xn - x_hi.astype(F32)).astype(BF16)
    both = _dot(x_hi, wr_ref[...])
    return both[:, :LANES] + both[:, LANES:] + _dot(x_lo, wr_ref[:, :LANES]) + br_ref[...]


def _route_rows(logits, tri_ref, route_ref, meta_ref, carry_ref):
    tm = logits.shape[0]
    lt = logits.T
    row_i = lax.broadcasted_iota(I32, (LANES, tm), 0)
    row = row_i.astype(F32)
    group_of_row = (row_i >> int(math.log2(EXPERTS_PER_GROUP))).astype(F32)
    ninf = -jnp.inf
    first = lambda hit, idx: jnp.min(jnp.where(hit, idx, float(LANES)), axis=0, keepdims=True)
    gmask = (row_i >= N_EXPERTS) & (row_i < N_EXPERTS + N_GROUPS)
    gl = jnp.where(gmask, lt, ninf)
    gmax = jnp.max(gl, axis=0, keepdims=True)
    gidx = first(gl == gmax, row - N_EXPERTS)
    gval = 1.0 / jnp.sum(jnp.exp(gl - gmax), axis=0, keepdims=True)
    emask = (row_i < N_EXPERTS) & (group_of_row == gidx)
    el = jnp.where(emask, lt, ninf)
    m1 = jnp.max(el, axis=0, keepdims=True)
    i1 = first(el == m1, row)
    el2 = jnp.where(row == i1, ninf, el)
    m2 = jnp.max(el2, axis=0, keepdims=True)
    i2 = first(el2 == m2, row)
    e21 = jnp.exp(m2 - m1)
    w1 = gval / (1.0 + e21)
    w2 = gval * e21 / (1.0 + e21)

    sel1 = row == i1
    sel2 = row == i2
    oh = (sel1 | sel2).astype(BF16)
    before = _dot(oh, tri_ref[...]) + carry_ref[...]
    r1 = jnp.sum(jnp.where(sel1, before, 0.0), axis=0, keepdims=True)
    r2 = jnp.sum(jnp.where(sel2, before, 0.0), axis=0, keepdims=True)
    carry_ref[...] = carry_ref[...] + jnp.sum(oh.astype(F32), axis=1, keepdims=True)

    rec = jnp.zeros((LANES, tm), F32)
    for slot, val in ((ROUTE_E1, i1), (ROUTE_E2, i2), (ROUTE_W1, w1), (ROUTE_W2, w2), (ROUTE_R1, r1), (ROUTE_R2, r2)):
        rec = jnp.where(row_i == slot, val, rec)
    route_ref[...] = rec.T
    meta_ref[0] = rec[ROUTE_E1:ROUTE_E1 + META_FIELDS].astype(I32)


def _merge(x, y_rec, att, sga, sgb, wb, wo, w_norm, w_router, b_router, cnt0):
    t, d = x.shape
    hw = y_rec.shape[1]
    tm = PROJ_TILE if t % PROJ_TILE == 0 else t
    row = lambda w: pl.BlockSpec((tm, w), lambda i: (i, 0))
    const = lambda a, b: pl.BlockSpec((a, b), lambda i: (0, 0))
    return pl.pallas_call(
        functools.partial(_merge_kernel, hw),
        grid=(t // tm,),
        in_specs=[row(d), row(hw), row(att.shape[1]), row(d), row(d), const(*wb.shape), const(d, d), const(1, d),
                  const(d, 2 * LANES), const(1, LANES), const(LANES, 1), const(tm, tm)],
        out_specs=[row(d), pl.BlockSpec((tm * d // LANES, LANES), lambda i: (i, 0)), row(LANES),
                   pl.BlockSpec((1, META_FIELDS, tm), lambda i: (i, 0, 0)), const(LANES, 1)],
        out_shape=[jax.ShapeDtypeStruct((t, d), F32), jax.ShapeDtypeStruct((t * d // LANES, LANES), F32),
                   jax.ShapeDtypeStruct((t, LANES), F32), jax.ShapeDtypeStruct((t // tm, META_FIELDS, tm), I32),
                   jax.ShapeDtypeStruct((LANES, 1), F32)],
        scratch_shapes=[pltpu.VMEM((LANES, 1), F32)],
        compiler_params=_params("arbitrary"),
        name="merge_route",
    )(x, y_rec, att, sga, sgb, wb, wo, w_norm.reshape(1, d), w_router, b_router, cnt0,
      jnp.asarray(np.triu(np.ones((tm, tm)), k=1), dtype=BF16))


def _row_copy(src, dst, sem):
    return pltpu.make_async_copy(src, dst, sem)


ROW_UNROLL = 8


def _tile_rows(ref, row, sub):
    return ref.at[pl.ds(pl.multiple_of(row * sub, sub), sub)]


def _scatter_kernel(n_tok, sub, first, slot_ref, pstart_ref, pend_ref, x_ref, *rest):
    xs_ref, zero_ref, sem, zsem, tsem = rest[-5:]

    def zero_blocks(blocks, zs, wait):
        for cond, row in blocks:
            @pl.when(cond)
            def _():
                at = row * sub if isinstance(row, int) else pl.multiple_of(row * sub, EXPERT_TILE * sub)
                cp = _row_copy(zero_ref, xs_ref.at[pl.ds(at, EXPERT_TILE * sub)], zs)
                cp.wait() if wait else cp.start()

    if first:
        n_rows = xs_ref.shape[0] // sub
        tails = [(pend_ref[e] > pstart_ref[e], pend_ref[e] - EXPERT_TILE) for e in range(N_EXPERTS)]
        unused = [(n_rows - (j + 1) * EXPERT_TILE >= pend_ref[N_EXPERTS - 1], n_rows - (j + 1) * EXPERT_TILE)
                  for j in range(N_EXPERTS)]

        @pl.when(pl.program_id(0) == 0)
        def _():
            zero_ref[...] = jnp.zeros_like(zero_ref)
            zero_blocks(tails, zsem, False)
            zero_blocks(unused, tsem, False)
            zero_blocks(tails, zsem, True)

    def issue(grp, _):
        for u in range(ROW_UNROLL):
            r = grp * ROW_UNROLL + u
            for k in range(TOP_K):
                _row_copy(_tile_rows(x_ref, r, sub), _tile_rows(xs_ref, slot_ref[0, k, r], sub),
                          sem).start(priority=(u * TOP_K + k) % 2)
        return 0

    lax.fori_loop(0, n_tok // ROW_UNROLL, issue, 0)
    for k in range(TOP_K):
        _row_copy(x_ref, xs_ref.at[pl.ds(0, n_tok * sub)], sem).wait()

    if first:
        @pl.when(pl.program_id(0) == 0)
        def _():
            zero_blocks(unused, tsem, True)


def _scatter_rows(x, slots, pstarts, pends, xs, rows):
    n, fields, tm = slots.shape
    sub = x.shape[0] // (n * tm)
    first = xs is None
    smem = pl.BlockSpec(memory_space=pltpu.SMEM)
    return pl.pallas_call(
        functools.partial(_scatter_kernel, tm, sub, first),
        grid=(n,),
        in_specs=[pl.BlockSpec((1, fields, tm), lambda i: (i, 0, 0), memory_space=pltpu.SMEM), smem, smem,
                  pl.BlockSpec((tm * sub, LANES), lambda i: (i, 0))]
        + ([] if first else [pl.BlockSpec(memory_space=pl.ANY)]),
        out_specs=pl.BlockSpec(memory_space=pl.ANY),
        out_shape=jax.ShapeDtypeStruct((rows * sub, LANES), F32),
        scratch_shapes=[pltpu.VMEM((EXPERT_TILE * sub, LANES), F32)] + [pltpu.SemaphoreType.DMA(())] * 3,
        input_output_aliases={} if first else {4: 0},
        compiler_params=_params("arbitrary"),
        name="moe_scatter",
    )(slots, pstarts, pends, x, *([] if first else [xs]))


def _expert_kernel(tm, sub, be_ref, nb_ref, nxt_ref, x_ref, wg_ref, wu_ref, wd_ref, y_ref,
                   raw_g, raw_u, raw_d, wgb_ref, wub_ref, wdb_ref, sem, slot_ref):
    i = pl.program_id(0)
    used = i < nb_ref[0]
    first = used & ((i == 0) | (be_ref[i] != be_ref[jnp.maximum(i - 1, 0)]))

    def fetch(e, slot):
        return [_row_copy(w.at[e], raw.at[slot], sem.at[slot])
                for w, raw in ((wg_ref, raw_g), (wu_ref, raw_u), (wd_ref, raw_d))]

    @pl.when(i == 0)
    def _():
        slot_ref[0] = 0
        for cp in fetch(be_ref[0], 0):
            cp.start()

    @pl.when(first)
    def _():
        slot = slot_ref[0]
        for cp in fetch(be_ref[i], slot):
            cp.wait()
        wgb_ref[...] = raw_g[slot].astype(BF16)
        wub_ref[...] = raw_u[slot].astype(BF16)
        wdb_ref[...] = raw_d[slot].astype(BF16)
        slot_ref[0] = 1 - slot

        @pl.when(nxt_ref[i] >= 0)
        def _():
            for cp in fetch(nxt_ref[i], 1 - slot):
                cp.start()

    @pl.when(used)
    def _():
        x = _load_row_tiles(x_ref, tm, sub).astype(BF16)
        g = _dot(x, wgb_ref[...])
        u = _dot(x, wub_ref[...])
        hmid = (g * _sigmoid(g) * u).astype(BF16)
        _store_row_tiles(y_ref, _dot(hmid, wdb_ref[...]))

    @pl.when(jnp.logical_not(used))
    def _():
        y_ref[...] = jnp.zeros_like(y_ref)


def _experts(xs, block_e, n_used, nxt_e, wg, wu, wd):
    _, d, ff = wg.shape
    sub = d // LANES
    tm = EXPERT_TILE
    blk = pl.BlockSpec((tm * sub, LANES), lambda i, be, nb, nx: (i, 0))
    hbm = pl.BlockSpec(memory_space=pl.ANY)
    grid_spec = pltpu.PrefetchScalarGridSpec(
        num_scalar_prefetch=3,
        grid=(xs.shape[0] // (tm * sub),),
        in_specs=[blk, hbm, hbm, hbm],
        out_specs=blk,
        scratch_shapes=[pltpu.VMEM((2, d, ff), F32), pltpu.VMEM((2, d, ff), F32), pltpu.VMEM((2, ff, d), F32),
                        pltpu.VMEM((d, ff), BF16), pltpu.VMEM((d, ff), BF16), pltpu.VMEM((ff, d), BF16),
                        pltpu.SemaphoreType.DMA((2,)), pltpu.SMEM((1,), I32)],
    )
    return pl.pallas_call(
        functools.partial(_expert_kernel, tm, sub),
        grid_spec=grid_spec,
        out_shape=jax.ShapeDtypeStruct(xs.shape, F32),
        compiler_params=_params("arbitrary"),
        name="moe_experts",
    )(block_e, n_used, nxt_e, xs, wg, wu, wd)


def _combine_kernel(n_tok, sub, scur_ref, snext_ref, h_ref, route_ref, ys_ref, o_ref, buf, sem):
    i = pl.program_id(0)
    n = pl.num_programs(0)
    slot = i % 2

    def start(slot_ref, s):
        def issue(grp, _):
            for u in range(ROW_UNROLL):
                r = grp * ROW_UNROLL + u
                for k in range(TOP_K):
                    _row_copy(_tile_rows(ys_ref, slot_ref[0, k, r], sub), _tile_rows(buf.at[s, k], r, sub),
                              sem.at[s]).start(priority=(u * TOP_K + k) % 2)
            return 0
        lax.fori_loop(0, n_tok // ROW_UNROLL, issue, 0)

    @pl.when(i == 0)
    def _():
        start(scur_ref, 0)

    @pl.when(i + 1 < n)
    def _():
        start(snext_ref, 1 - slot)

    for k in range(TOP_K):
        _row_copy(ys_ref.at[pl.ds(0, n_tok * sub)], buf.at[slot, k], sem.at[slot]).wait()
    route = route_ref[...]
    w1 = route[:, ROUTE_W1:ROUTE_W1 + 1]
    w2 = route[:, ROUTE_W2:ROUTE_W2 + 1]
    for j in range(sub):
        cols = slice(j * LANES, (j + 1) * LANES)
        part = lambda k: buf[slot, k, pl.ds(j, n_tok, stride=sub), :]
        o_ref[:, cols] = h_ref[:, cols] + (part(0) * w1 + part(1) * w2)


def _combine(h, route, slots, ys):
    t, d = h.shape
    n, fields, tm = slots.shape
    sub = d // LANES
    mspec = lambda f: pl.BlockSpec((1, fields, tm), f, memory_space=pltpu.SMEM)
    return pl.pallas_call(
        functools.partial(_combine_kernel, tm, sub),
        grid=(n,),
        in_specs=[mspec(lambda i: (i, 0, 0)), mspec(lambda i: (jnp.minimum(i + 1, n - 1), 0, 0)),
                  pl.BlockSpec((tm, d), lambda i: (i, 0)), pl.BlockSpec((tm, LANES), lambda i: (i, 0)),
                  pl.BlockSpec(memory_space=pl.ANY)],
        out_specs=pl.BlockSpec((tm, d), lambda i: (i, 0)),
        out_shape=jax.ShapeDtypeStruct((t, d), F32),
        scratch_shapes=[pltpu.VMEM((2, TOP_K, tm * sub, LANES), F32), pltpu.SemaphoreType.DMA((2,))],
        compiler_params=_params("arbitrary"),
        name="moe_combine",
    )(slots, slots, h, route, ys)


def _moe(parts, counts, wg, wu, wd):
    tm = EXPERT_TILE
    n_assign = sum(part[0].shape[0] for part in parts) * TOP_K
    n_blocks = -(-(n_assign + N_EXPERTS * (tm - 1)) // tm)
    counts = counts[:N_EXPERTS, 0].astype(I32)
    pcounts = (counts + tm - 1) // tm * tm
    pends = jnp.cumsum(pcounts)
    pstarts = pends - pcounts
    block_start = jnp.arange(n_blocks, dtype=I32) * tm
    block_e = jnp.minimum(jnp.sum((pends[None, :] <= block_start[:, None]).astype(I32), axis=1), N_EXPERTS - 1)
    n_used = pends[-1:] // tm
    ids = jnp.arange(N_EXPERTS, dtype=I32)
    later = (ids[None, :] > ids[:, None]) & (pcounts[None, :] > 0)
    nxt_of = jnp.min(jnp.where(later, ids[None, :], N_EXPERTS), axis=1)
    nxt_of = jnp.where(nxt_of < N_EXPERTS, nxt_of, -1)
    nxt_e = jnp.sum(jnp.where(block_e[:, None] == ids[None, :], nxt_of[None, :], 0), axis=1)
    xs = None
    slots = []
    for _, xn, _, meta in parts:
        assert meta.shape[-1] % ROW_UNROLL == 0
        e, pos = meta[:, :TOP_K], meta[:, TOP_K:]
        seg = jnp.sum(jnp.where(e[..., None] == jnp.arange(N_EXPERTS, dtype=I32), pstarts, 0), axis=-1)
        slots.append(seg + pos)
        xs = _scatter_rows(xn, slots[-1], pstarts, pends, xs, n_blocks * tm)
    ys = _experts(xs, block_e, n_used, nxt_e, wg, wu, wd)
    return [_combine(h, route, s, ys) for (h, _, route, _), s in zip(parts, slots)]


def kernel(x_prompt, x_sample, cache_swa_k, cache_swa_v, state_hgrn, meta_tokens, rel_bias_table, hgrn_lower_bounds, w_norm_mix, w_in, hgrn_out_norm, q_norm, k_norm, attn_sinks, w_branch, w_out, w_norm_ffn, w_router_group, b_router_group, w_router_expert, b_router_expert, w_expert_gate, w_expert_up, w_expert_down):
    b, s, d = x_prompt.shape
    bd, sd, _ = x_sample.shape
    depth, _, heads, dk, dv = state_hgrn.shape
    assert depth == 1 and heads == HGRN_HEADS and dk == dv
    hw = heads * dk
    aw = ATTN_HEADS * HEAD_DIM
    kvw = KV_HEADS * HEAD_DIM
    assert w_in.shape[-1] == 4 * hw + aw + 2 * kvw + 2 * d
    assert s % HGRN_CHUNK == 0 and s % CHUNK == 0 and sd == N_META and N_EXPERTS + N_GROUPS <= LANES
    l = 0

    p = jax.nn.softmax(hgrn_lower_bounds.astype(F32), axis=0)
    lb = jnp.cumsum(p, axis=0)[l + 1] - p[0]

    w_in_b = w_in[l].astype(BF16)
    proj = functools.partial(_inproj, w_norm=w_norm_mix[l], w_in_bf16=w_in_b, q_gain=q_norm[l], k_gain=k_norm[l],
                             lb=lb, hw=hw, aw=aw, kvw=kvw)
    x_small = jnp.concatenate([x_sample.reshape(bd * sd, d), meta_tokens.astype(F32)], axis=0)
    *rec_s, qa_s, k_s, v_s, sga_s, sgb_s = proj(x_small)
    *rec_p, qa_p, k_p, v_p, sga_p, sgb_p = proj(x_prompt.reshape(b * s, d))
    ns = bd * sd
    k_meta, v_meta = k_s[ns:], v_s[ns:]

    streams = lambda a, n: a.reshape(n, -1, a.shape[-1])
    s0_small = jnp.concatenate([state_hgrn[l].astype(F32), jnp.zeros((1, heads, dk, dv), F32)], axis=0)
    y_small, st_small = _hgrn(*(streams(a, bd + 1) for a in rec_s), hgrn_out_norm[l], s0_small, sd)
    y_p, st_p = _hgrn(*(streams(a, b) for a in rec_p), hgrn_out_norm[l], st_small, HGRN_CHUNK, shared_s0=bd)

    table = rel_bias_table.astype(F32)
    att_p, new_k_p, new_v_p = _attn_prompt(streams(qa_p, b), streams(k_p, b), streams(v_p, b), k_meta, v_meta,
                                           table, attn_sinks[l])
    k_new, v_new = k_s[:ns].reshape(bd, sd, kvw), v_s[:ns].reshape(bd, sd, kvw)
    att_s, new_k_s, new_v_s = _attn_sample(qa_s[:ns].reshape(bd, sd, aw), cache_swa_k[l:l + 1].astype(F32),
                                           cache_swa_v[l:l + 1].astype(F32), k_new, v_new, k_meta, v_meta,
                                           table, attn_sinks[l])

    wb = w_branch[l].astype(BF16)
    wo = w_out[l].astype(BF16)
    w_router = jnp.pad(jnp.concatenate([w_router_expert[l], w_router_group[l]], axis=1).astype(F32),
                       ((0, 0), (0, LANES - N_EXPERTS - N_GROUPS)))
    b_router = jnp.pad(jnp.concatenate([b_router_expert[l], b_router_group[l]]).astype(F32),
                       (0, LANES - N_EXPERTS - N_GROUPS)).reshape(1, LANES)
    w_router_hi = w_router.astype(BF16)
    w_router_lo = (w_router - w_router_hi.astype(F32)).astype(BF16)
    w_router = jnp.concatenate([w_router_hi, w_router_lo], axis=1)
    merge = functools.partial(_merge, wb=wb, wo=wo, w_norm=w_norm_ffn[l], w_router=w_router, b_router=b_router)
    *part_p, cnt_p = merge(x_prompt.reshape(b * s, d), y_p.reshape(b * s, hw), att_p.reshape(b * s, aw),
                           sga_p, sgb_p, cnt0=jnp.zeros((LANES, 1), F32))
    *part_s, cnt_s = merge(x_sample.reshape(ns, d), y_small[:bd].reshape(ns, hw), att_s.reshape(ns, aw),
                           sga_s[:ns], sgb_s[:ns], cnt0=cnt_p)

    out_p, out_s = _moe([part_p, part_s], cnt_s, w_expert_gate[l], w_expert_up[l], w_expert_down[l])

    return (out_p.reshape(b, s, d), out_s.reshape(bd, sd, d), new_k_p, new_v_p, st_p[None],
            new_k_s, new_v_s, st_small[:bd][None])
```

```python
import functools
import math

import numpy as np
import jax
import jax.numpy as jnp
from jax import lax
from jax.experimental import pallas as pl
from jax.experimental.pallas import tpu as pltpu

F32 = jnp.float32
BF16 = jnp.bfloat16
I32 = jnp.int32

CHUNK = 64
N_META = 16
PAST_LEN = 2048
EPS = 1e-6
HGRN_HEADS = 4
ATTN_HEADS = 8
KV_HEADS = 2
HEAD_DIM = 64
GQA_GROUP = ATTN_HEADS // KV_HEADS
WINDOW = 128
WINDOW_CHUNKS = WINDOW // CHUNK
NUM_BUCKETS = 32
MAX_DISTANCE = 128
N_GROUPS = 4
EXPERTS_PER_GROUP = 8
N_EXPERTS = N_GROUPS * EXPERTS_PER_GROUP
TOP_K = 2

LANES = 128
MXU_WIDTH = 256
VMEM_LIMIT = 56 * 1024 * 1024

INPROJ_TILE = 512
PROJ_TILE = 1024
HGRN_CHUNK = 128
HGRN_CHUNKS_PER_STEP = 8
ATTN_CHUNKS_PER_STEP = 16
SAMPLE_STREAMS_PER_STEP = 4
EXPERT_TILE = 512


LOG2E = math.log2(math.e)


def _sigmoid(x):
    return 1.0 / (1.0 + jnp.exp(-x))


def _split3(x):
    hi = x.astype(BF16)
    r1 = x - hi.astype(F32)
    mid = r1.astype(BF16)
    lo = (r1 - mid.astype(F32)).astype(BF16)
    return hi, mid, lo


def _dot(a, b):
    return jnp.dot(a, b, preferred_element_type=F32)


def _dot_nt(a, b):
    return lax.dot_general(a, b, (((1,), (1,)), ((), ())), preferred_element_type=F32)


def _dot_tn(a, b):
    return lax.dot_general(a, b, (((0,), (0,)), ((), ())), preferred_element_type=F32)


SUBLANES = 8


def _store_row_tiles(ref, x):
    n, d = x.shape
    sub = d // LANES
    for j in range(sub):
        ref[pl.ds(j, n, stride=sub), :] = x[:, j * LANES:(j + 1) * LANES]


def _load_row_tiles(ref, n, sub):
    return jnp.concatenate([ref[pl.ds(j, n, stride=sub), :] for j in range(sub)], axis=1)


def _params(*sem):
    return pltpu.CompilerParams(dimension_semantics=sem, vmem_limit_bytes=VMEM_LIMIT)


def _inproj_kernel(hw, aw, kvw, d, x_ref, wn_ref, w_ref, qg_ref, kg_ref, bdq_ref, bdk_ref, lb_ref,
                   qr_ref, lf_ref, kin_ref, hv_ref, hg_ref, qa_ref, k_ref, v_ref, sga_ref, sgb_ref):
    x = x_ref[...]
    ms = jnp.mean(x * x, axis=-1, keepdims=True)
    xn = (x * lax.rsqrt(ms + EPS) * wn_ref[...]).astype(BF16)

    def seg(a, b):
        return _dot(xn, w_ref[:, a:b])

    def head_rms(a, bd_ref, gain):
        sq = (a * a).astype(BF16)
        wb = bd_ref.shape[0]
        m = jnp.concatenate([_dot(sq[:, c:c + wb], bd_ref[...]) for c in range(0, a.shape[1], wb)], axis=1)
        return a * lax.rsqrt(m + EPS) * gain

    o = 0
    hq = seg(o, o + hw)
    qr_ref[...] = (hq * _sigmoid(hq) * (hw // HGRN_HEADS) ** -0.5).astype(BF16)
    o += hw
    z = seg(o, o + hw)
    lb = lb_ref[...]
    e = jnp.exp(-jnp.abs(z))
    r = 1.0 / (1.0 + e)
    pos = z >= 0
    lf_ref[...] = jnp.log2(lb + (1.0 - lb) * jnp.where(pos, r, e * r))
    kin_ref[...] = ((1.0 - lb) * jnp.where(pos, e * r, r)).astype(BF16)
    o += hw
    hv_ref[...] = seg(o, o + hw).astype(BF16)
    o += hw
    hg = seg(o, o + hw)
    hg_ref[...] = (hg * _sigmoid(hg)).astype(BF16)
    o += hw
    aq = seg(o, o + aw)
    qa_ref[...] = (head_rms(aq, bdq_ref, qg_ref[...]) * (HEAD_DIM ** -0.5 * LOG2E)).astype(BF16)
    o += aw
    kv = seg(o, o + 2 * kvw)
    k_ref[...] = head_rms(kv[:, :kvw], bdk_ref, kg_ref[...])
    v_ref[...] = kv[:, kvw:]
    o += 2 * kvw
    sga_ref[...] = _sigmoid(seg(o, o + d)).astype(BF16)
    o += d
    sgb_ref[...] = _sigmoid(seg(o, o + d)).astype(BF16)


def _block_diag_mean(width, group):
    i = np.arange(width)
    return jnp.asarray((i[:, None] // group == i[None, :] // group) / group, dtype=BF16)


def _inproj(x, w_norm, w_in_bf16, q_gain, k_gain, lb, hw, aw, kvw):
    t, d = x.shape
    tm = INPROJ_TILE if t % INPROJ_TILE == 0 else t
    cols = w_in_bf16.shape[1]
    row = lambda w: pl.BlockSpec((tm, w), lambda i: (i, 0))
    const = lambda a, b: pl.BlockSpec((a, b), lambda i: (0, 0))
    outs = [(hw, BF16), (hw, F32), (hw, BF16), (hw, BF16), (hw, BF16), (aw, BF16), (kvw, F32), (kvw, F32),
            (d, BF16), (d, BF16)]
    bdq = min(aw, MXU_WIDTH)
    assert aw % bdq == 0
    return pl.pallas_call(
        functools.partial(_inproj_kernel, hw, aw, kvw, d),
        grid=(t // tm,),
        in_specs=[row(d), const(1, d), const(d, cols), const(1, aw), const(1, kvw), const(bdq, bdq), const(kvw, kvw),
                  const(1, hw)],
        out_specs=[row(w) for w, _ in outs],
        out_shape=[jax.ShapeDtypeStruct((t, w), dt) for w, dt in outs],
        compiler_params=_params("arbitrary"),
        name="inproj",
    )(x, w_norm.reshape(1, d), w_in_bf16,
      jnp.tile(q_gain, aw // HEAD_DIM).reshape(1, aw), jnp.tile(k_gain, kvw // HEAD_DIM).reshape(1, kvw),
      _block_diag_mean(bdq, HEAD_DIM), _block_diag_mean(kvw, HEAD_DIM), lb.reshape(1, hw))


def _hgrn_consts(L):
    t = np.arange(L)
    u = t[None, :]
    blocks = [u <= t[:, None], u > t[:, None]]
    levels = []
    m = L // 2
    while m >= 1:
        levels.append(m)
        m //= 2
    lvl = np.full((L, L), -1, np.int32)
    lvl[t, t] = len(levels)
    isq_cols = []
    for j, m in enumerate(levels):
        bnd = (t // (2 * m)) * (2 * m) + m - 1
        isq = (t % (2 * m)) >= m
        cq = isq[:, None] & (u > bnd[:, None]) & (u <= t[:, None])
        ck = (~isq)[:, None] & (u > t[:, None]) & (u <= bnd[:, None])
        blocks.append(cq | ck)
        same = (t[:, None] // (2 * m)) == (t[None, :] // (2 * m))
        lvl[same & isq[:, None] & (~isq)[None, :]] = j
        isq_cols.append(isq)
    c = np.concatenate(blocks, axis=0).astype(np.float32)
    isq = np.stack(isq_cols, axis=1).astype(np.float32)
    isq = np.pad(isq, ((0, 0), (0, LANES - isq.shape[1])))
    c2 = np.tile(c, (1, 2))
    return jnp.asarray(c2, dtype=BF16), jnp.asarray(np.tile(lvl, (1, 2))), jnp.asarray(isq), len(levels)


def _hgrn_kernel(L, nlev, heads, dk, qr_ref, lf_ref, kin_ref, hv_ref, hg_ref, og_ref, c_ref, lvl_ref, isq_ref,
                 s0_ref, y_ref, sout_ref, st_ref):
    c = pl.program_id(1)

    @pl.when(c == 0)
    def _():
        for h in range(heads):
            st_ref[h] = s0_ref[0, h].T

    for cc in range(lf_ref.shape[1] // L):
        _hgrn_chunk(L, nlev, heads, dk, slice(cc * L, (cc + 1) * L), qr_ref, lf_ref, kin_ref, hv_ref, hg_ref, og_ref,
                    c_ref, lvl_ref, isq_ref, y_ref, st_ref)

    @pl.when(c == pl.num_programs(1) - 1)
    def _():
        for h in range(heads):
            sout_ref[0, h] = st_ref[h].T


def _hgrn_chunk(L, nlev, heads, dk, rows, qr_ref, lf_ref, kin_ref, hv_ref, hg_ref, og_ref, c_ref, lvl_ref, isq_ref,
                y_ref, st_ref):
    kin = kin_ref[0, rows, :].astype(F32)
    q = qr_ref[0, rows, :].astype(F32)

    hi, mid, _ = _split3(lf_ref[0, rows, :])
    ex = jnp.exp2(_dot(c_ref[...], jnp.concatenate([hi, mid], axis=0)))
    e_b = ex[0:L]
    e_rev = ex[L:2 * L]

    q_in = (q * e_b).astype(BF16)
    k_out = (kin * e_rev).astype(BF16)
    q_b = q.astype(BF16)
    k_b = kin.astype(BF16)
    xs = []
    for j in range(nlev):
        m = L >> (j + 1)
        if m % SUBLANES == 0:
            qk = jnp.concatenate([(q if blk % 2 else kin)[blk * m:(blk + 1) * m] for blk in range(L // m)], axis=0)
        else:
            qk = jnp.where(isq_ref[:, j:j + 1] > 0.5, q, kin)
        xs.append((qk * ex[(2 + j) * L:(3 + j) * L]).astype(BF16))
    lvl = lvl_ref[...]
    v = hv_ref[0, rows, :]
    g = hg_ref[0, rows, :].astype(F32)
    og = og_ref[...]

    def block_diag(x):
        zero = jnp.zeros((x.shape[0], dk), x.dtype)
        return jnp.concatenate([jnp.concatenate([x[:, :dk], zero], axis=1),
                                jnp.concatenate([zero, x[:, dk:]], axis=1)], axis=0)

    for pair in range(heads // 2):
        sl = slice(2 * pair * dk, 2 * (pair + 1) * dk)
        a = jnp.where(lvl == nlev, _dot_nt(q_b[:, sl], block_diag(k_b[:, sl])), 0.0)
        for j in range(nlev):
            xp = xs[j][:, sl]
            a = jnp.where(lvl == j, _dot_nt(xp, block_diag(xp)), a)
        st = jnp.concatenate([st_ref[2 * pair], st_ref[2 * pair + 1]], axis=1)
        vp = v[:, sl]
        o = _dot(a.astype(BF16), block_diag(vp)) + _dot_nt(q_in[:, sl], block_diag(st.astype(BF16)))
        for half in range(2):
            h = 2 * pair + half
            hs = slice(h * dk, (h + 1) * dk)
            st_ref[h] = st_ref[h] * e_b[L - 1:L, hs] + _dot_tn(v[:, hs], k_out[:, hs])
            oh = o[:, half * dk:(half + 1) * dk]
            ms = jnp.mean(oh * oh, axis=-1, keepdims=True)
            gh = g[:, hs]
            y_ref[0, rows, hs] = (oh * lax.rsqrt(ms + EPS) * og[:, hs] * gh).astype(BF16)


def _hgrn(qr, lf, kin, hv, hg, out_gain, s0, L, shared_s0=None):
    b, s, w = lf.shape
    heads, dk = s0.shape[1], s0.shape[2]
    cm, lvl, isq, nlev = _hgrn_consts(L)
    per_step = HGRN_CHUNKS_PER_STEP if s % (HGRN_CHUNKS_PER_STEP * L) == 0 else 1
    seq = pl.BlockSpec((1, per_step * L, w), lambda i, c: (i, c, 0))
    const = lambda a: pl.BlockSpec(a.shape, lambda i, c: (0,) * a.ndim)
    state = pl.BlockSpec((1, heads, dk, dk), lambda i, c: (i, 0, 0, 0))
    start = state if shared_s0 is None else pl.BlockSpec((1, heads, dk, dk), lambda i, c: (shared_s0, 0, 0, 0))
    og2 = jnp.tile(out_gain, heads).reshape(1, w)
    return pl.pallas_call(
        functools.partial(_hgrn_kernel, L, nlev, heads, dk),
        grid=(b, s // (per_step * L)),
        in_specs=[seq, seq, seq, seq, seq, const(og2), const(cm), const(lvl), const(isq), start],
        out_specs=[seq, state],
        out_shape=[jax.ShapeDtypeStruct((b, s, w), BF16), jax.ShapeDtypeStruct((b,) + s0.shape[1:], F32)],
        scratch_shapes=[pltpu.VMEM((heads, dk, dk), F32)],
        compiler_params=_params("arbitrary", "arbitrary"),
        name=f"hgrn_scan_{L}",
    )(qr, lf, kin, hv, hg, og2, cm, lvl, isq, s0)


def _t5_bucket_np(rel):
    half = NUM_BUCKETS // 2
    max_exact = half // 2
    assert (NUM_BUCKETS, MAX_DISTANCE) == (32, 128)
    n = np.abs(rel).astype(np.int64)
    nn = np.maximum(n, 1)
    k = np.zeros_like(nn)
    for j in range(1, 48):
        k = np.where(64 * (1 << j) <= nn * nn, j, k)
    large = np.minimum(max_exact + k, half - 1)
    return np.where(rel > 0, half, 0) + np.where(n < max_exact, n, large)


HEADS_PER_COL = LANES // HEAD_DIM
COLS_PER_GROUP = GQA_GROUP // HEADS_PER_COL
HEAD_ORDER = tuple(g * GQA_GROUP + col * HEADS_PER_COL + half
                   for g in range(KV_HEADS) for half in range(HEADS_PER_COL) for col in range(COLS_PER_GROUP))
KV_EXPAND = KV_HEADS * HEADS_PER_COL


def _expand_kv(x, with_ones=False):
    assert HEADS_PER_COL == 2 and KV_HEADS == 2 and x.shape[1] == LANES
    low = lax.broadcasted_iota(I32, x.shape, 1) < HEAD_DIM
    xr = pltpu.roll(x, HEAD_DIM, axis=1)
    zero = jnp.zeros_like(x)
    blocks = [jnp.where(low, x, zero), jnp.where(low, zero, xr), jnp.where(low, xr, zero), jnp.where(low, zero, x)]
    if with_ones:
        blocks = [b for blk in blocks for b in (blk, jnp.ones_like(x))]
    return jnp.concatenate(blocks, axis=1).astype(BF16)


def _attn_core(q, kx, vx, bias):
    tq = q.shape[0]
    scores = []
    for g in range(KV_HEADS):
        cols = [q[:, (g * COLS_PER_GROUP + c) * LANES:(g * COLS_PER_GROUP + c + 1) * LANES]
                for c in range(COLS_PER_GROUP)]
        qst = jnp.concatenate(cols, axis=0)
        for half in range(HEADS_PER_COL):
            blk = g * HEADS_PER_COL + half
            scores.append(_dot_nt(qst, kx[:, blk * LANES:(blk + 1) * LANES]))
    s = jnp.concatenate(scores, axis=0) + bias
    pb = jnp.exp2(s - jnp.max(s, axis=-1, keepdims=True)).astype(BF16)
    rows = COLS_PER_GROUP * tq
    outs = []
    for g in range(KV_HEADS):
        o = None
        for half in range(HEADS_PER_COL):
            blk = g * HEADS_PER_COL + half
            pv = _dot(pb[blk * rows:(blk + 1) * rows], vx[:, 2 * blk * LANES:2 * (blk + 1) * LANES])
            part = pv[:, :LANES] * (1.0 / pv[:, LANES:])
            o = part if o is None else o + part
        outs.extend(o[c * tq:(c + 1) * tq] for c in range(COLS_PER_GROUP))
    return outs


def _bias_rows(table, bucket):
    onehot = (jnp.asarray(bucket)[..., None] == jnp.arange(NUM_BUCKETS)).astype(F32)
    cols = jnp.stack([table[:, h] for h in HEAD_ORDER], axis=1)
    bias = jnp.einsum('...qkb,bh->...hqk', onehot, cols, precision=lax.Precision.HIGHEST)
    return bias.reshape(*bucket.shape[:-2], ATTN_HEADS * bucket.shape[-2], bucket.shape[-1]) * LOG2E


def _pad_keys(bias, sinks):
    tk = bias.shape[-1]
    n_pad = -tk % LANES or LANES
    tq = bias.shape[-2] // ATTN_HEADS
    sink = jnp.repeat(jnp.stack([sinks[h] for h in HEAD_ORDER]).astype(F32), tq) * LOG2E
    sink = jnp.broadcast_to(sink[:, None], bias.shape[:-1] + (1,))
    masked = jnp.full(bias.shape[:-1] + (n_pad - 1,), -jnp.inf, F32)
    return jnp.concatenate([bias, sink, masked], axis=-1)


def _store_heads(ref, x, stream=0):
    for g in range(KV_HEADS):
        ref[0, stream, :, g, :] = x[:, g * HEAD_DIM:(g + 1) * HEAD_DIM]


def _load_heads(ref, stream=0):
    return jnp.concatenate([ref[0, stream, :, g, :] for g in range(KV_HEADS)], axis=1)


def _attn_prompt_kernel(cb, q_ref, k_ref, v_ref, km_ref, vm_ref, bias_ref, o_ref, nk_ref, nv_ref,
                        kx_ref, vx_ref):
    step = pl.program_id(1)
    s_len = k_ref.shape[1]
    meta_at = WINDOW + s_len

    @pl.when(step == pl.num_programs(1) - 1)
    def _():
        _store_heads(nk_ref, k_ref[0, s_len - WINDOW:, :])
        _store_heads(nv_ref, v_ref[0, s_len - WINDOW:, :])

    @pl.when(step == 0)
    def _():
        piece = min(s_len, 512)
        for src, meta, dst in ((k_ref, km_ref, kx_ref), (v_ref, vm_ref, vx_ref)):
            expand = functools.partial(_expand_kv, with_ones=dst is vx_ref)
            blank = lambda n: expand(jnp.zeros((n, src.shape[2]), F32))
            dst[0:WINDOW] = blank(WINDOW)
            for r in range(0, s_len, piece):
                dst[WINDOW + r:WINDOW + r + piece] = expand(src[0, r:r + piece, :])
            dst[meta_at:meta_at + N_META] = expand(meta[...])
            dst[meta_at + N_META:] = blank(dst.shape[0] - meta_at - N_META)

    win = WINDOW + CHUNK
    tail = kx_ref.shape[0] - meta_at
    for j in range(cb):
        c = step * cb + j
        start = pl.multiple_of(c * CHUNK, CHUNK)
        kall = jnp.concatenate([kx_ref[pl.ds(start, win), :], kx_ref[meta_at:meta_at + tail, :]], axis=0)
        vall = jnp.concatenate([vx_ref[pl.ds(start, win), :], vx_ref[meta_at:meta_at + tail, :]], axis=0)
        rows = slice(j * CHUNK, (j + 1) * CHUNK)
        outs = _attn_core(q_ref[0, rows, :], kall, vall, bias_ref[jnp.minimum(c, bias_ref.shape[0] - 1)])
        for ci, o in enumerate(outs):
            o_ref[0, rows, ci * LANES:(ci + 1) * LANES] = o.astype(o_ref.dtype)


def _attn_prompt(q, k, v, k_meta, v_meta, table, sinks):
    b, s, aw = q.shape
    kvw = k.shape[-1]
    nc = s // CHUNK
    cb = ATTN_CHUNKS_PER_STEP if nc % ATTN_CHUNKS_PER_STEP == 0 else 1
    assert s % min(s, 512) == 0
    n_bias = 1
    while True:
        qpos = N_META + (n_bias - 1) * CHUNK
        if np.all(_t5_bucket_np(np.arange(N_META) - qpos) == _t5_bucket_np(np.arange(N_META) - qpos - 10 ** 6)):
            break
        n_bias += 1
    n_bias = min(max(n_bias, WINDOW_CHUNKS + 1), nc)
    cs = np.arange(n_bias)[:, None]
    qpos = N_META + cs * CHUNK + np.arange(CHUNK)[None]
    wpos = N_META + (cs - WINDOW_CHUNKS) * CHUNK + np.arange(WINDOW + CHUNK)[None]
    kpos = np.concatenate([wpos, np.broadcast_to(np.arange(N_META), (n_bias, N_META))], axis=1)
    valid = np.concatenate([wpos >= N_META, np.ones((n_bias, N_META), bool)], axis=1)
    bias = _bias_rows(table, _t5_bucket_np(kpos[:, None, :] - qpos[:, :, None]))
    bias = _pad_keys(jnp.where(valid[:, None, :], bias, -jnp.inf), sinks)
    qs =pl.BlockSpec((1, cb * CHUNK, aw), lambda i, c: (i, c, 0))
    kv = pl.BlockSpec((1, s, kvw), lambda i, c: (i, 0, 0))
    meta = pl.BlockSpec((N_META, kvw), lambda i, c: (0, 0))
    xrows = WINDOW + s + bias.shape[-1] - (WINDOW + CHUNK)
    cache_shape = (1, b, WINDOW, KV_HEADS, HEAD_DIM)
    cache = pl.BlockSpec((1, 1) + cache_shape[2:], lambda i, c: (0, i, 0, 0, 0))
    return pl.pallas_call(
        functools.partial(_attn_prompt_kernel, cb),
        grid=(b, nc // cb),
        in_specs=[qs, kv, kv, meta, meta, pl.BlockSpec(bias.shape, lambda i, c: (0, 0, 0))],
        out_specs=[qs, cache, cache],
        out_shape=[jax.ShapeDtypeStruct((b, s, aw), BF16)] + [jax.ShapeDtypeStruct(cache_shape, F32)] * 2,
        scratch_shapes=[pltpu.VMEM((xrows, KV_EXPAND * LANES), BF16),
                        pltpu.VMEM((xrows, 2 * KV_EXPAND * LANES), BF16)],
        compiler_params=_params("arbitrary", "arbitrary"),
        name="attn_prompt",
    )(q, k, v, k_meta, v_meta, bias)


def _attn_sample_kernel(q_ref, kc_ref, vc_ref, kn_ref, vn_ref, km_ref, vm_ref, bias_ref, o_ref, nk_ref, nv_ref):
    tk = kc_ref.shape[2] + kn_ref.shape[1] + km_ref.shape[0]
    zeros = jnp.zeros((bias_ref.shape[1] - tk, km_ref.shape[1]), F32)
    for s in range(q_ref.shape[0]):
        kc, vc, kn, vn = _load_heads(kc_ref, s), _load_heads(vc_ref, s), kn_ref[s], vn_ref[s]
        kall = _expand_kv(jnp.concatenate([kc, kn, km_ref[...], zeros], axis=0))
        vall = _expand_kv(jnp.concatenate([vc, vn, vm_ref[...], zeros], axis=0), with_ones=True)
        outs = _attn_core(q_ref[s], kall, vall, bias_ref[...])
        for ci, o in enumerate(outs):
            o_ref[s, :, ci * LANES:(ci + 1) * LANES] = o.astype(o_ref.dtype)
        n_new = kn.shape[0]
        _store_heads(nk_ref, jnp.concatenate([kc[n_new:], kn], axis=0), s)
        _store_heads(nv_ref, jnp.concatenate([vc[n_new:], vn], axis=0), s)


def _attn_sample(q, k_cache, v_cache, k_new, v_new, k_meta, v_meta, table, sinks):
    bd, sd, aw = q.shape
    kvw = k_new.shape[-1]
    win = k_cache.shape[2]
    sb = SAMPLE_STREAMS_PER_STEP if bd % SAMPLE_STREAMS_PER_STEP == 0 else 1
    cache = pl.BlockSpec((1, sb) + k_cache.shape[2:], lambda i: (0, i, 0, 0, 0))
    qpos = N_META + PAST_LEN + np.arange(sd)
    kpos = np.concatenate([N_META + PAST_LEN - win + np.arange(win), qpos, np.arange(N_META)])
    bias = _pad_keys(_bias_rows(table, _t5_bucket_np(kpos[None, :] - qpos[:, None])), sinks)
    per = lambda n, w: pl.BlockSpec((sb, n, w), lambda i: (i, 0, 0))
    meta = pl.BlockSpec((N_META, kvw), lambda i: (0, 0))
    return pl.pallas_call(
        _attn_sample_kernel,
        grid=(bd // sb,),
        in_specs=[per(sd, aw), cache, cache, per(sd, kvw), per(sd, kvw), meta, meta,
                  pl.BlockSpec(bias.shape, lambda i: (0, 0))],
        out_specs=[per(sd, aw), cache, cache],
        out_shape=[jax.ShapeDtypeStruct((bd, sd, aw), BF16)] + [jax.ShapeDtypeStruct(k_cache.shape, F32)] * 2,
        compiler_params=_params("arbitrary"),
        name="attn_sample",
    )(q, k_cache, v_cache, k_new, v_new, k_meta, v_meta, bias)


ROUTE_E1, ROUTE_E2, ROUTE_R1, ROUTE_R2, ROUTE_W1, ROUTE_W2 = range(6)
META_FIELDS = 2 * TOP_K


def _merge_kernel(hw, x_ref, yr_ref, at_ref, sga_ref, sgb_ref, wb_ref, wo_ref, wn_ref, wr_ref, br_ref, cnt0_ref, tri_ref,
                  h_ref, xn_ref, route_ref, meta_ref, cnt_ref, carry_ref):
    @pl.when(pl.program_id(0) == 0)
    def _():
        carry_ref[...] = cnt0_ref[...]

    logits = _project_rows(hw, x_ref, yr_ref, at_ref, sga_ref, sgb_ref, wb_ref, wo_ref, wn_ref, wr_ref, br_ref,
                           h_ref, xn_ref)
    _route_rows(logits, tri_ref, route_ref, meta_ref, carry_ref)
    cnt_ref[...] = carry_ref[...]


def _project_rows(hw, x_ref, yr_ref, at_ref, sga_ref, sgb_ref, wb_ref, wo_ref, wn_ref, wr_ref, br_ref, h_ref, xn_ref):
    br = _dot(yr_ref[...], wb_ref[0:hw, :])
    ba = _dot(at_ref[...], wb_ref[hw:, :])
    merged = sga_ref[...].astype(F32) * br + sgb_ref[...].astype(F32) * ba
    h = x_ref[...] + _dot(merged.astype(BF16), wo_ref[...])
    h_ref[...] = h
    ms = jnp.mean(h * h, axis=-1, keepdims=True)
    xn = h * lax.rsqrt(ms + EPS) * wn_ref[...]
    _store_row_tiles(xn_ref, xn)

    x_hi = xn.astype(BF16)
    x_lo = (xn - x_hi.astype(F32)).astype(BF16)
    both = _dot(x_hi, wr_ref[...])
    return both[:, :LANES] + both[:, LANES:] + _dot(x_lo, wr_ref[:, :LANES]) + br_ref[...]


def _route_rows(logits, tri_ref, route_ref, meta_ref, carry_ref):
    tm = logits.shape[0]
    lt = logits.T
    row_i = lax.broadcasted_iota(I32, (LANES, tm), 0)
    row = row_i.astype(F32)
    group_of_row = (row_i >> int(math.log2(EXPERTS_PER_GROUP))).astype(F32)
    ninf = -jnp.inf
    first = lambda hit, idx: jnp.min(jnp.where(hit, idx, float(LANES)), axis=0, keepdims=True)
    gmask = (row_i >= N_EXPERTS) & (row_i < N_EXPERTS + N_GROUPS)
    gl = jnp.where(gmask, lt, ninf)
    gmax = jnp.max(gl, axis=0, keepdims=True)
    gidx = first(gl == gmax, row - N_EXPERTS)
    gval = 1.0 / jnp.sum(jnp.exp(gl - gmax), axis=0, keepdims=True)
    emask = (row_i < N_EXPERTS) & (group_of_row == gidx)
    el = jnp.where(emask, lt, ninf)
    m1 = jnp.max(el, axis=0, keepdims=True)
    i1 = first(el == m1, row)
    el2 = jnp.where(row == i1, ninf, el)
    m2 = jnp.max(el2, axis=0, keepdims=True)
    i2 = first(el2 == m2, row)
    e21 = jnp.exp(m2 - m1)
    w1 = gval / (1.0 + e21)
    w2 = gval * e21 / (1.0 + e21)

    sel1 = row == i1
    sel2 = row == i2
    oh = (sel1 | sel2).astype(BF16)
    before = _dot(oh, tri_ref[...]) + carry_ref[...]
    r1 = jnp.sum(jnp.where(sel1, before, 0.0), axis=0, keepdims=True)
    r2 = jnp.sum(jnp.where(sel2, before, 0.0), axis=0, keepdims=True)
    carry_ref[...] = carry_ref[...] + jnp.sum(oh.astype(F32), axis=1, keepdims=True)

    rec = jnp.zeros((LANES, tm), F32)
    for slot, val in ((ROUTE_E1, i1), (ROUTE_E2, i2), (ROUTE_W1, w1), (ROUTE_W2, w2), (ROUTE_R1, r1), (ROUTE_R2, r2)):
        rec = jnp.where(row_i == slot, val, rec)
    route_ref[...] = rec.T
    meta_ref[0] = rec[ROUTE_E1:ROUTE_E1 + META_FIELDS].astype(I32)


def _merge(x, y_rec, att, sga, sgb, wb, wo, w_norm, w_router, b_router, cnt0):
    t, d = x.shape
    hw = y_rec.shape[1]
    tm = PROJ_TILE if t % PROJ_TILE == 0 else t
    row = lambda w: pl.BlockSpec((tm, w), lambda i: (i, 0))
    const = lambda a, b: pl.BlockSpec((a, b), lambda i: (0, 0))
    return pl.pallas_call(
        functools.partial(_merge_kernel, hw),
        grid=(t // tm,),
        in_specs=[row(d), row(hw), row(att.shape[1]), row(d), row(d), const(*wb.shape), const(d, d), const(1, d),
                  const(d, 2 * LANES), const(1, LANES), const(LANES, 1), const(tm, tm)],
        out_specs=[row(d), pl.BlockSpec((tm * d // LANES, LANES), lambda i: (i, 0)), row(LANES),
                   pl.BlockSpec((1, META_FIELDS, tm), lambda i: (i, 0, 0)), const(LANES, 1)],
        out_shape=[jax.ShapeDtypeStruct((t, d), F32), jax.ShapeDtypeStruct((t * d // LANES, LANES), F32),
                   jax.ShapeDtypeStruct((t, LANES), F32), jax.ShapeDtypeStruct((t // tm, META_FIELDS, tm), I32),
                   jax.ShapeDtypeStruct((LANES, 1), F32)],
        scratch_shapes=[pltpu.VMEM((LANES, 1), F32)],
        compiler_params=_params("arbitrary"),
        name="merge_route",
    )(x, y_rec, att, sga, sgb, wb, wo, w_norm.reshape(1, d), w_router, b_router, cnt0,
      jnp.asarray(np.triu(np.ones((tm, tm)), k=1), dtype=BF16))


def _row_copy(src, dst, sem):
    return pltpu.make_async_copy(src, dst, sem)


ROW_UNROLL = 8


def _tile_rows(ref, row, sub):
    return ref.at[pl.ds(pl.multiple_of(row * sub, sub), sub)]


def _scatter_kernel(n_tok, sub, first, slot_ref, pstart_ref, pend_ref, x_ref, *rest):
    xs_ref, zero_ref, sem, zsem, tsem = rest[-5:]

    def zero_blocks(blocks, zs, wait):
        for cond, row in blocks:
            @pl.when(cond)
            def _():
                at = row * sub if isinstance(row, int) else pl.multiple_of(row * sub, EXPERT_TILE * sub)
                cp = _row_copy(zero_ref, xs_ref.at[pl.ds(at, EXPERT_TILE * sub)], zs)
                cp.wait() if wait else cp.start()

    if first:
        n_rows = xs_ref.shape[0] // sub
        tails = [(pend_ref[e] > pstart_ref[e], pend_ref[e] - EXPERT_TILE) for e in range(N_EXPERTS)]
        unused = [(n_rows - (j + 1) * EXPERT_TILE >= pend_ref[N_EXPERTS - 1], n_rows - (j + 1) * EXPERT_TILE)
                  for j in range(N_EXPERTS)]

        @pl.when(pl.program_id(0) == 0)
        def _():
            zero_ref[...] = jnp.zeros_like(zero_ref)
            zero_blocks(tails, zsem, False)
            zero_blocks(unused, tsem, False)
            zero_blocks(tails, zsem, True)

    def issue(grp, _):
        for u in range(ROW_UNROLL):
            r = grp * ROW_UNROLL + u
            for k in range(TOP_K):
                _row_copy(_tile_rows(x_ref, r, sub), _tile_rows(xs_ref, slot_ref[0, k, r], sub),
                          sem).start(priority=(u * TOP_K + k) % 2)
        return 0

    lax.fori_loop(0, n_tok // ROW_UNROLL, issue, 0)
    for k in range(TOP_K):
        _row_copy(x_ref, xs_ref.at[pl.ds(0, n_tok * sub)], sem).wait()

    if first:
        @pl.when(pl.program_id(0) == 0)
        def _():
            zero_blocks(unused, tsem, True)


def _scatter_rows(x, slots, pstarts, pends, xs, rows):
    n, fields, tm = slots.shape
    sub = x.shape[0] // (n * tm)
    first = xs is None
    smem = pl.BlockSpec(memory_space=pltpu.SMEM)
    return pl.pallas_call(
        functools.partial(_scatter_kernel, tm, sub, first),
        grid=(n,),
        in_specs=[pl.BlockSpec((1, fields, tm), lambda i: (i, 0, 0), memory_space=pltpu.SMEM), smem, smem,
                  pl.BlockSpec((tm * sub, LANES), lambda i: (i, 0))]
        + ([] if first else [pl.BlockSpec(memory_space=pl.ANY)]),
        out_specs=pl.BlockSpec(memory_space=pl.ANY),
        out_shape=jax.ShapeDtypeStruct((rows * sub, LANES), F32),
        scratch_shapes=[pltpu.VMEM((EXPERT_TILE * sub, LANES), F32)] + [pltpu.SemaphoreType.DMA(())] * 3,
        input_output_aliases={} if first else {4: 0},
        compiler_params=_params("arbitrary"),
        name="moe_scatter",
    )(slots, pstarts, pends, x, *([] if first else [xs]))


def _expert_kernel(tm, sub, be_ref, nb_ref, nxt_ref, x_ref, wg_ref, wu_ref, wd_ref, y_ref,
                   raw_g, raw_u, raw_d, wgb_ref, wub_ref, wdb_ref, sem, slot_ref):
    i = pl.program_id(0)
    used = i < nb_ref[0]
    first = used & ((i == 0) | (be_ref[i] != be_ref[jnp.maximum(i - 1, 0)]))

    def fetch(e, slot):
        return [_row_copy(w.at[e], raw.at[slot], sem.at[slot])
                for w, raw in ((wg_ref, raw_g), (wu_ref, raw_u), (wd_ref, raw_d))]

    @pl.when(i == 0)
    def _():
        slot_ref[0] = 0
        for cp in fetch(be_ref[0], 0):
            cp.start()

    @pl.when(first)
    def _():
        slot = slot_ref[0]
        for cp in fetch(be_ref[i], slot):
            cp.wait()
        wgb_ref[...] = raw_g[slot].astype(BF16)
        wub_ref[...] = raw_u[slot].astype(BF16)
        wdb_ref[...] = raw_d[slot].astype(BF16)
        slot_ref[0] = 1 - slot

        @pl.when(nxt_ref[i] >= 0)
        def _():
            for cp in fetch(nxt_ref[i], 1 - slot):
                cp.start()

    @pl.when(used)
    def _():
        x = _load_row_tiles(x_ref, tm, sub).astype(BF16)
        g = _dot(x, wgb_ref[...])
        u = _dot(x, wub_ref[...])
        hmid = (g * _sigmoid(g) * u).astype(BF16)
        _store_row_tiles(y_ref, _dot(hmid, wdb_ref[...]))

    @pl.when(jnp.logical_not(used))
    def _():
        y_ref[...] = jnp.zeros_like(y_ref)


def _experts(xs, block_e, n_used, nxt_e, wg, wu, wd):
    _, d, ff = wg.shape
    sub = d // LANES
    tm = EXPERT_TILE
    blk = pl.BlockSpec((tm * sub, LANES), lambda i, be, nb, nx: (i, 0))
    hbm = pl.BlockSpec(memory_space=pl.ANY)
    grid_spec = pltpu.PrefetchScalarGridSpec(
        num_scalar_prefetch=3,
        grid=(xs.shape[0] // (tm * sub),),
        in_specs=[blk, hbm, hbm, hbm],
        out_specs=blk,
        scratch_shapes=[pltpu.VMEM((2, d, ff), F32), pltpu.VMEM((2, d, ff), F32), pltpu.VMEM((2, ff, d), F32),
                        pltpu.VMEM((d, ff), BF16), pltpu.VMEM((d, ff), BF16), pltpu.VMEM((ff, d), BF16),
                        pltpu.SemaphoreType.DMA((2,)), pltpu.SMEM((1,), I32)],
    )
    return pl.pallas_call(
        functools.partial(_expert_kernel, tm, sub),
        grid_spec=grid_spec,
        out_shape=jax.ShapeDtypeStruct(xs.shape, F32),
        compiler_params=_params("arbitrary"),
        name="moe_experts",
    )(block_e, n_used, nxt_e, xs, wg, wu, wd)


def _combine_kernel(n_tok, sub, scur_ref, snext_ref, h_ref, route_ref, ys_ref, o_ref, buf, sem):
    i = pl.program_id(0)
    n = pl.num_programs(0)
    slot = i % 2

    def start(slot_ref, s):
        def issue(grp, _):
            for u in range(ROW_UNROLL):
                r = grp * ROW_UNROLL + u
                for k in range(TOP_K):
                    _row_copy(_tile_rows(ys_ref, slot_ref[0, k, r], sub), _tile_rows(buf.at[s, k], r, sub),
                              sem.at[s]).start(priority=(u * TOP_K + k) % 2)
            return 0
        lax.fori_loop(0, n_tok // ROW_UNROLL, issue, 0)

    @pl.when(i == 0)
    def _():
        start(scur_ref, 0)

    @pl.when(i + 1 < n)
    def _():
        start(snext_ref, 1 - slot)

    for k in range(TOP_K):
        _row_copy(ys_ref.at[pl.ds(0, n_tok * sub)], buf.at[slot, k], sem.at[slot]).wait()
    route = route_ref[...]
    w1 = route[:, ROUTE_W1:ROUTE_W1 + 1]
    w2 = route[:, ROUTE_W2:ROUTE_W2 + 1]
    for j in range(sub):
        cols = slice(j * LANES, (j + 1) * LANES)
        part = lambda k: buf[slot, k, pl.ds(j, n_tok, stride=sub), :]
        o_ref[:, cols] = h_ref[:, cols] + (part(0) * w1 + part(1) * w2)


def _combine(h, route, slots, ys):
    t, d = h.shape
    n, fields, tm = slots.shape
    sub = d // LANES
    mspec = lambda f: pl.BlockSpec((1, fields, tm), f, memory_space=pltpu.SMEM)
    return pl.pallas_call(
        functools.partial(_combine_kernel, tm, sub),
        grid=(n,),
        in_specs=[mspec(lambda i: (i, 0, 0)), mspec(lambda i: (jnp.minimum(i + 1, n - 1), 0, 0)),
                  pl.BlockSpec((tm, d), lambda i: (i, 0)), pl.BlockSpec((tm, LANES), lambda i: (i, 0)),
                  pl.BlockSpec(memory_space=pl.ANY)],
        out_specs=pl.BlockSpec((tm, d), lambda i: (i, 0)),
        out_shape=jax.ShapeDtypeStruct((t, d), F32),
        scratch_shapes=[pltpu.VMEM((2, TOP_K, tm * sub, LANES), F32), pltpu.SemaphoreType.DMA((2,))],
        compiler_params=_params("arbitrary"),
        name="moe_combine",
    )(slots, slots, h, route, ys)


def _moe(parts, counts, wg, wu, wd):
    tm = EXPERT_TILE
    n_assign = sum(part[0].shape[0] for part in parts) * TOP_K
    n_blocks = -(-(n_assign + N_EXPERTS * (tm - 1)) // tm)
    counts = counts[:N_EXPERTS, 0].astype(I32)
    pcounts = (counts + tm - 1) // tm * tm
    pends = jnp.cumsum(pcounts)
    pstarts = pends - pcounts
    block_start = jnp.arange(n_blocks, dtype=I32) * tm
    block_e = jnp.minimum(jnp.sum((pends[None, :] <= block_start[:, None]).astype(I32), axis=1), N_EXPERTS - 1)
    n_used = pends[-1:] // tm
    ids = jnp.arange(N_EXPERTS, dtype=I32)
    later = (ids[None, :] > ids[:, None]) & (pcounts[None, :] > 0)
    nxt_of = jnp.min(jnp.where(later, ids[None, :], N_EXPERTS), axis=1)
    nxt_of = jnp.where(nxt_of < N_EXPERTS, nxt_of, -1)
    nxt_e = jnp.sum(jnp.where(block_e[:, None] == ids[None, :], nxt_of[None, :], 0), axis=1)
    xs = None
    slots = []
    for _, xn, _, meta in parts:
        assert meta.shape[-1] % ROW_UNROLL == 0
        e, pos = meta[:, :TOP_K], meta[:, TOP_K:]
        seg = jnp.sum(jnp.where(e[..., None] == jnp.arange(N_EXPERTS, dtype=I32), pstarts, 0), axis=-1)
        slots.append(seg + pos)
        xs = _scatter_rows(xn, slots[-1], pstarts, pends, xs, n_blocks * tm)
    ys = _experts(xs, block_e, n_used, nxt_e, wg, wu, wd)
    return [_combine(h, route, s, ys) for (h, _, route, _), s in zip(parts, slots)]


def kernel(x_prompt, x_sample, cache_swa_k, cache_swa_v, state_hgrn, meta_tokens, rel_bias_table, hgrn_lower_bounds, w_norm_mix, w_in, hgrn_out_norm, q_norm, k_norm, attn_sinks, w_branch, w_out, w_norm_ffn, w_router_group, b_router_group, w_router_expert, b_router_expert, w_expert_gate, w_expert_up, w_expert_down):
    b, s, d = x_prompt.shape
    bd, sd, _ = x_sample.shape
    depth, _, heads, dk, dv = state_hgrn.shape
    assert depth == 1 and heads == HGRN_HEADS and dk == dv
    hw = heads * dk
    aw = ATTN_HEADS * HEAD_DIM
    kvw = KV_HEADS * HEAD_DIM
    assert w_in.shape[-1] == 4 * hw + aw + 2 * kvw + 2 * d
    assert s % HGRN_CHUNK == 0 and s % CHUNK == 0 and sd == N_META and N_EXPERTS + N_GROUPS <= LANES
    l = 0

    p = jax.nn.softmax(hgrn_lower_bounds.astype(F32), axis=0)
    lb = jnp.cumsum(p, axis=0)[l + 1] - p[0]

    w_in_b = w_in[l].astype(BF16)
    proj = functools.partial(_inproj, w_norm=w_norm_mix[l], w_in_bf16=w_in_b, q_gain=q_norm[l], k_gain=k_norm[l],
                             lb=lb, hw=hw, aw=aw, kvw=kvw)
    x_small = jnp.concatenate([x_sample.reshape(bd * sd, d), meta_tokens.astype(F32)], axis=0)
    *rec_s, qa_s, k_s, v_s, sga_s, sgb_s = proj(x_small)
    *rec_p, qa_p, k_p, v_p, sga_p, sgb_p = proj(x_prompt.reshape(b * s, d))
    ns = bd * sd
    k_meta, v_meta = k_s[ns:], v_s[ns:]

    streams = lambda a, n: a.reshape(n, -1, a.shape[-1])
    s0_small = jnp.concatenate([state_hgrn[l].astype(F32), jnp.zeros((1, heads, dk, dv), F32)], axis=0)
    y_small, st_small = _hgrn(*(streams(a, bd + 1) for a in rec_s), hgrn_out_norm[l], s0_small, sd)
    y_p, st_p = _hgrn(*(streams(a, b) for a in rec_p), hgrn_out_norm[l], st_small, HGRN_CHUNK, shared_s0=bd)

    table = rel_bias_table.astype(F32)
    att_p, new_k_p, new_v_p = _attn_prompt(streams(qa_p, b), streams(k_p, b), streams(v_p, b), k_meta, v_meta,
                                           table, attn_sinks[l])
    k_new, v_new = k_s[:ns].reshape(bd, sd, kvw), v_s[:ns].reshape(bd, sd, kvw)
    att_s, new_k_s, new_v_s = _attn_sample(qa_s[:ns].reshape(bd, sd, aw), cache_swa_k[l:l + 1].astype(F32),
                                           cache_swa_v[l:l + 1].astype(F32), k_new, v_new, k_meta, v_meta,
                                           table, attn_sinks[l])

    wb = w_branch[l].astype(BF16)
    wo = w_out[l].astype(BF16)
    w_router = jnp.pad(jnp.concatenate([w_router_expert[l], w_router_group[l]], axis=1).astype(F32),
                       ((0, 0), (0, LANES - N_EXPERTS - N_GROUPS)))
    b_router = jnp.pad(jnp.concatenate([b_router_expert[l], b_router_group[l]]).astype(F32),
                       (0, LANES - N_EXPERTS - N_GROUPS)).reshape(1, LANES)
    w_router_hi = w_router.astype(BF16)
    w_router_lo = (w_router - w_router_hi.astype(F32)).astype(BF16)
    w_router = jnp.concatenate([w_router_hi, w_router_lo], axis=1)
    merge = functools.partial(_merge, wb=wb, wo=wo, w_norm=w_norm_ffn[l], w_router=w_router, b_router=b_router)
    *part_p, cnt_p = merge(x_prompt.reshape(b * s, d), y_p.reshape(b * s, hw), att_p.reshape(b * s, aw),
                           sga_p, sgb_p, cnt0=jnp.zeros((LANES, 1), F32))
    *part_s, cnt_s = merge(x_sample.reshape(ns, d), y_small[:bd].reshape(ns, hw), att_s.reshape(ns, aw),
                           sga_s[:ns], sgb_s[:ns], cnt0=cnt_p)

    out_p, out_s = _moe([part_p, part_s], cnt_s, w_expert_gate[l], w_expert_up[l], w_expert_down[l])

    return (out_p.reshape(b, s, d), out_s.reshape(bd, sd, d), new_k_p, new_v_p, st_p[None],
            new_k_s, new_v_s, st_small[:bd][None])
```

```python
import functools
import math

import numpy as np
import jax
import jax.numpy as jnp
from jax import lax
from jax.experimental import pallas as pl
from jax.experimental.pallas import tpu as pltpu

F32 = jnp.float32
BF16 = jnp.bfloat16
I32 = jnp.int32

CHUNK = 64
N_META = 16
PAST_LEN = 2048
EPS = 1e-6
HGRN_HEADS = 4
ATTN_HEADS = 8
KV_HEADS = 2
HEAD_DIM = 64
GQA_GROUP = ATTN_HEADS // KV_HEADS
WINDOW = 128
WINDOW_CHUNKS = WINDOW // CHUNK
NUM_BUCKETS = 32
MAX_DISTANCE = 128
N_GROUPS = 4
EXPERTS_PER_GROUP = 8
N_EXPERTS = N_GROUPS * EXPERTS_PER_GROUP
TOP_K = 2

LANES = 128
MXU_WIDTH = 256
VMEM_LIMIT = 56 * 1024 * 1024

INPROJ_TILE = 512
PROJ_TILE = 1024
HGRN_CHUNK = 128
HGRN_CHUNKS_PER_STEP = 8
ATTN_CHUNKS_PER_STEP = 16
SAMPLE_STREAMS_PER_STEP = 4
EXPERT_TILE = 512
COMBINE_TILE = 512


LOG2E = math.log2(math.e)


def _sigmoid(x):
    return 1.0 / (1.0 + jnp.exp(-x))


def _split3(x):
    hi = x.astype(BF16)
    r1 = x - hi.astype(F32)
    mid = r1.astype(BF16)
    lo = (r1 - mid.astype(F32)).astype(BF16)
    return hi, mid, lo


def _dot(a, b):
    return jnp.dot(a, b, preferred_element_type=F32)


def _dot_nt(a, b):
    return lax.dot_general(a, b, (((1,), (1,)), ((), ())), preferred_element_type=F32)


def _dot_tn(a, b):
    return lax.dot_general(a, b, (((0,), (0,)), ((), ())), preferred_element_type=F32)


SUBLANES = 8


def _store_row_tiles(ref, x):
    n, d = x.shape
    sub = d // LANES
    for j in range(sub):
        ref[pl.ds(j, n, stride=sub), :] = x[:, j * LANES:(j + 1) * LANES]


def _load_row_tiles(ref, n, sub):
    return jnp.concatenate([ref[pl.ds(j, n, stride=sub), :] for j in range(sub)], axis=1)


def _params(*sem):
    return pltpu.CompilerParams(dimension_semantics=sem, vmem_limit_bytes=VMEM_LIMIT)


def _inproj_kernel(hw, aw, kvw, d, x_ref, wn_ref, w_ref, qg_ref, kg_ref, bdq_ref, bdk_ref, lb_ref,
                   qr_ref, lf_ref, kin_ref, hv_ref, hg_ref, qa_ref, k_ref, v_ref, sga_ref, sgb_ref):
    x = x_ref[...]
    ms = jnp.mean(x * x, axis=-1, keepdims=True)
    xn = (x * lax.rsqrt(ms + EPS) * wn_ref[...]).astype(BF16)

    def seg(a, b):
        return _dot(xn, w_ref[:, a:b])

    def head_rms(a, bd_ref, gain):
        sq = (a * a).astype(BF16)
        wb = bd_ref.shape[0]
        m = jnp.concatenate([_dot(sq[:, c:c + wb], bd_ref[...]) for c in range(0, a.shape[1], wb)], axis=1)
        return a * lax.rsqrt(m + EPS) * gain

    o = 0
    hq = seg(o, o + hw)
    qr_ref[...] = (hq * _sigmoid(hq) * (hw // HGRN_HEADS) ** -0.5).astype(BF16)
    o += hw
    z = seg(o, o + hw)
    lb = lb_ref[...]
    e = jnp.exp(-jnp.abs(z))
    r = 1.0 / (1.0 + e)
    pos = z >= 0
    lf_ref[...] = jnp.log2(lb + (1.0 - lb) * jnp.where(pos, r, e * r))
    kin_ref[...] = ((1.0 - lb) * jnp.where(pos, e * r, r)).astype(BF16)
    o += hw
    hv_ref[...] = seg(o, o + hw).astype(BF16)
    o += hw
    hg = seg(o, o + hw)
    hg_ref[...] = (hg * _sigmoid(hg)).astype(BF16)
    o += hw
    aq = seg(o, o + aw)
    qa_ref[...] = (head_rms(aq, bdq_ref, qg_ref[...]) * (HEAD_DIM ** -0.5 * LOG2E)).astype(BF16)
    o += aw
    kv = seg(o, o + 2 * kvw)
    k_ref[...] = head_rms(kv[:, :kvw], bdk_ref, kg_ref[...])
    v_ref[...] = kv[:, kvw:]
    o += 2 * kvw
    sga_ref[...] = _sigmoid(seg(o, o + d)).astype(BF16)
    o += d
    sgb_ref[...] = _sigmoid(seg(o, o + d)).astype(BF16)


def _block_diag_mean(width, group):
    i = np.arange(width)
    return jnp.asarray((i[:, None] // group == i[None, :] // group) / group, dtype=BF16)


def _inproj(x, w_norm, w_in_bf16, q_gain, k_gain, lb, hw, aw, kvw):
    t, d = x.shape
    tm = INPROJ_TILE if t % INPROJ_TILE == 0 else t
    cols = w_in_bf16.shape[1]
    row = lambda w: pl.BlockSpec((tm, w), lambda i: (i, 0))
    const = lambda a, b: pl.BlockSpec((a, b), lambda i: (0, 0))
    outs = [(hw, BF16), (hw, F32), (hw, BF16), (hw, BF16), (hw, BF16), (aw, BF16), (kvw, F32), (kvw, F32),
            (d, BF16), (d, BF16)]
    bdq = min(aw, MXU_WIDTH)
    assert aw % bdq == 0
    return pl.pallas_call(
        functools.partial(_inproj_kernel, hw, aw, kvw, d),
        grid=(t // tm,),
        in_specs=[row(d), const(1, d), const(d, cols), const(1, aw), const(1, kvw), const(bdq, bdq), const(kvw, kvw),
                  const(1, hw)],
        out_specs=[row(w) for w, _ in outs],
        out_shape=[jax.ShapeDtypeStruct((t, w), dt) for w, dt in outs],
        compiler_params=_params("arbitrary"),
        name="inproj",
    )(x, w_norm.reshape(1, d), w_in_bf16,
      jnp.tile(q_gain, aw // HEAD_DIM).reshape(1, aw), jnp.tile(k_gain, kvw // HEAD_DIM).reshape(1, kvw),
      _block_diag_mean(bdq, HEAD_DIM), _block_diag_mean(kvw, HEAD_DIM), lb.reshape(1, hw))


def _hgrn_consts(L):
    t = np.arange(L)
    u = t[None, :]
    blocks = [u <= t[:, None], u > t[:, None]]
    levels = []
    m = L // 2
    while m >= 1:
        levels.append(m)
        m //= 2
    lvl = np.full((L, L), -1, np.int32)
    lvl[t, t] = len(levels)
    isq_cols = []
    for j, m in enumerate(levels):
        bnd = (t // (2 * m)) * (2 * m) + m - 1
        isq = (t % (2 * m)) >= m
        cq = isq[:, None] & (u > bnd[:, None]) & (u <= t[:, None])
        ck = (~isq)[:, None] & (u > t[:, None]) & (u <= bnd[:, None])
        blocks.append(cq | ck)
        same = (t[:, None] // (2 * m)) == (t[None, :] // (2 * m))
        lvl[same & isq[:, None] & (~isq)[None, :]] = j
        isq_cols.append(isq)
    c = np.concatenate(blocks, axis=0).astype(np.float32)
    isq = np.stack(isq_cols, axis=1).astype(np.float32)
    isq = np.pad(isq, ((0, 0), (0, LANES - isq.shape[1])))
    c2 = np.tile(c, (1, 2))
    return jnp.asarray(c2, dtype=BF16), jnp.asarray(np.tile(lvl, (1, 2))), jnp.asarray(isq), len(levels)


def _hgrn_kernel(L, nlev, heads, dk, qr_ref, lf_ref, kin_ref, hv_ref, hg_ref, og_ref, c_ref, lvl_ref, isq_ref,
                 s0_ref, y_ref, sout_ref, st_ref):
    c = pl.program_id(1)

    @pl.when(c == 0)
    def _():
        for h in range(heads):
            st_ref[h] = s0_ref[0, h].T

    for cc in range(lf_ref.shape[1] // L):
        _hgrn_chunk(L, nlev, heads, dk, slice(cc * L, (cc + 1) * L), qr_ref, lf_ref, kin_ref, hv_ref, hg_ref, og_ref,
                    c_ref, lvl_ref, isq_ref, y_ref, st_ref)

    @pl.when(c == pl.num_programs(1) - 1)
    def _():
        for h in range(heads):
            sout_ref[0, h] = st_ref[h].T


def _hgrn_chunk(L, nlev, heads, dk, rows, qr_ref, lf_ref, kin_ref, hv_ref, hg_ref, og_ref, c_ref, lvl_ref, isq_ref,
                y_ref, st_ref):
    kin = kin_ref[0, rows, :].astype(F32)
    q = qr_ref[0, rows, :].astype(F32)

    hi, mid, _ = _split3(lf_ref[0, rows, :])
    ex = jnp.exp2(_dot(c_ref[...], jnp.concatenate([hi, mid], axis=0)))
    e_b = ex[0:L]
    e_rev = ex[L:2 * L]

    q_in = (q * e_b).astype(BF16)
    k_out = (kin * e_rev).astype(BF16)
    q_b = q.astype(BF16)
    k_b = kin.astype(BF16)
    xs = []
    for j in range(nlev):
        m = L >> (j + 1)
        if m % SUBLANES == 0:
            qk = jnp.concatenate([(q if blk % 2 else kin)[blk * m:(blk + 1) * m] for blk in range(L // m)], axis=0)
        else:
            qk = jnp.where(isq_ref[:, j:j + 1] > 0.5, q, kin)
        xs.append((qk * ex[(2 + j) * L:(3 + j) * L]).astype(BF16))
    lvl = lvl_ref[...]
    v = hv_ref[0, rows, :]
    g = hg_ref[0, rows, :].astype(F32)
    og = og_ref[...]

    def block_diag(x):
        zero = jnp.zeros((x.shape[0], dk), x.dtype)
        return jnp.concatenate([jnp.concatenate([x[:, :dk], zero], axis=1),
                                jnp.concatenate([zero, x[:, dk:]], axis=1)], axis=0)

    for pair in range(heads // 2):
        sl = slice(2 * pair * dk, 2 * (pair + 1) * dk)
        a = jnp.where(lvl == nlev, _dot_nt(q_b[:, sl], block_diag(k_b[:, sl])), 0.0)
        for j in range(nlev):
            xp = xs[j][:, sl]
            a = jnp.where(lvl == j, _dot_nt(xp, block_diag(xp)), a)
        st = jnp.concatenate([st_ref[2 * pair], st_ref[2 * pair + 1]], axis=1)
        vp = v[:, sl]
        o = _dot(a.astype(BF16), block_diag(vp)) + _dot_nt(q_in[:, sl], block_diag(st.astype(BF16)))
        for half in range(2):
            h = 2 * pair + half
            hs = slice(h * dk, (h + 1) * dk)
            st_ref[h] = st_ref[h] * e_b[L - 1:L, hs] + _dot_tn(v[:, hs], k_out[:, hs])
            oh = o[:, half * dk:(half + 1) * dk]
            ms = jnp.mean(oh * oh, axis=-1, keepdims=True)
            gh = g[:, hs]
            y_ref[0, rows, hs] = (oh * lax.rsqrt(ms + EPS) * og[:, hs] * gh).astype(BF16)


def _hgrn(qr, lf, kin, hv, hg, out_gain, s0, L, shared_s0=None):
    b, s, w = lf.shape
    heads, dk = s0.shape[1], s0.shape[2]
    cm, lvl, isq, nlev = _hgrn_consts(L)
    per_step = HGRN_CHUNKS_PER_STEP if s % (HGRN_CHUNKS_PER_STEP * L) == 0 else 1
    seq = pl.BlockSpec((1, per_step * L, w), lambda i, c: (i, c, 0))
    const = lambda a: pl.BlockSpec(a.shape, lambda i, c: (0,) * a.ndim)
    state = pl.BlockSpec((1, heads, dk, dk), lambda i, c: (i, 0, 0, 0))
    start = state if shared_s0 is None else pl.BlockSpec((1, heads, dk, dk), lambda i, c: (shared_s0, 0, 0, 0))
    og2 = jnp.tile(out_gain, heads).reshape(1, w)
    return pl.pallas_call(
        functools.partial(_hgrn_kernel, L, nlev, heads, dk),
        grid=(b, s // (per_step * L)),
        in_specs=[seq, seq, seq, seq, seq, const(og2), const(cm), const(lvl), const(isq), start],
        out_specs=[seq, state],
        out_shape=[jax.ShapeDtypeStruct((b, s, w), BF16), jax.ShapeDtypeStruct((b,) + s0.shape[1:], F32)],
        scratch_shapes=[pltpu.VMEM((heads, dk, dk), F32)],
        compiler_params=_params("arbitrary", "arbitrary"),
        name=f"hgrn_scan_{L}",
    )(qr, lf, kin, hv, hg, og2, cm, lvl, isq, s0)


def _t5_bucket_np(rel):
    half = NUM_BUCKETS // 2
    max_exact = half // 2
    assert (NUM_BUCKETS, MAX_DISTANCE) == (32, 128)
    n = np.abs(rel).astype(np.int64)
    nn = np.maximum(n, 1)
    k = np.zeros_like(nn)
    for j in range(1, 48):
        k = np.where(64 * (1 << j) <= nn * nn, j, k)
    large = np.minimum(max_exact + k, half - 1)
    return np.where(rel > 0, half, 0) + np.where(n < max_exact, n, large)


HEADS_PER_COL = LANES // HEAD_DIM
COLS_PER_GROUP = GQA_GROUP // HEADS_PER_COL
HEAD_ORDER = tuple(g * GQA_GROUP + col * HEADS_PER_COL + half
                   for g in range(KV_HEADS) for half in range(HEADS_PER_COL) for col in range(COLS_PER_GROUP))
KV_EXPAND = KV_HEADS * HEADS_PER_COL


def _expand_kv(x, with_ones=False):
    assert HEADS_PER_COL == 2 and KV_HEADS == 2 and x.shape[1] == LANES
    low = lax.broadcasted_iota(I32, x.shape, 1) < HEAD_DIM
    xr = pltpu.roll(x, HEAD_DIM, axis=1)
    zero = jnp.zeros_like(x)
    blocks = [jnp.where(low, x, zero), jnp.where(low, zero, xr), jnp.where(low, xr, zero), jnp.where(low, zero, x)]
    if with_ones:
        blocks = [b for blk in blocks for b in (blk, jnp.ones_like(x))]
    return jnp.concatenate(blocks, axis=1).astype(BF16)


def _attn_core(q, kx, vx, bias):
    tq = q.shape[0]
    scores = []
    for g in range(KV_HEADS):
        cols = [q[:, (g * COLS_PER_GROUP + c) * LANES:(g * COLS_PER_GROUP + c + 1) * LANES]
                for c in range(COLS_PER_GROUP)]
        qst = jnp.concatenate(cols, axis=0)
        for half in range(HEADS_PER_COL):
            blk = g * HEADS_PER_COL + half
            scores.append(_dot_nt(qst, kx[:, blk * LANES:(blk + 1) * LANES]))
    s = jnp.concatenate(scores, axis=0) + bias
    pb = jnp.exp2(s - jnp.max(s, axis=-1, keepdims=True)).astype(BF16)
    rows = COLS_PER_GROUP * tq
    outs = []
    for g in range(KV_HEADS):
        o = None
        for half in range(HEADS_PER_COL):
            blk = g * HEADS_PER_COL + half
            pv = _dot(pb[blk * rows:(blk + 1) * rows], vx[:, 2 * blk * LANES:2 * (blk + 1) * LANES])
            part = pv[:, :LANES] * (1.0 / pv[:, LANES:])
            o = part if o is None else o + part
        outs.extend(o[c * tq:(c + 1) * tq] for c in range(COLS_PER_GROUP))
    return outs


def _bias_rows(table, bucket):
    onehot = (jnp.asarray(bucket)[..., None] == jnp.arange(NUM_BUCKETS)).astype(F32)
    cols = jnp.stack([table[:, h] for h in HEAD_ORDER], axis=1)
    bias = jnp.einsum('...qkb,bh->...hqk', onehot, cols, precision=lax.Precision.HIGHEST)
    return bias.reshape(*bucket.shape[:-2], ATTN_HEADS * bucket.shape[-2], bucket.shape[-1]) * LOG2E


def _pad_keys(bias, sinks):
    tk = bias.shape[-1]
    n_pad = -tk % LANES or LANES
    tq = bias.shape[-2] // ATTN_HEADS
    sink = jnp.repeat(jnp.stack([sinks[h] for h in HEAD_ORDER]).astype(F32), tq) * LOG2E
    sink = jnp.broadcast_to(sink[:, None], bias.shape[:-1] + (1,))
    masked = jnp.full(bias.shape[:-1] + (n_pad - 1,), -jnp.inf, F32)
    return jnp.concatenate([bias, sink, masked], axis=-1)


def _store_heads(ref, x, stream=0):
    for g in range(KV_HEADS):
        ref[0, stream, :, g, :] = x[:, g * HEAD_DIM:(g + 1) * HEAD_DIM]


def _load_heads(ref, stream=0):
    return jnp.concatenate([ref[0, stream, :, g, :] for g in range(KV_HEADS)], axis=1)


def _attn_prompt_kernel(cb, q_ref, k_ref, v_ref, km_ref, vm_ref, bias_ref, o_ref, nk_ref, nv_ref,
                        kx_ref, vx_ref):
    step = pl.program_id(1)
    s_len = k_ref.shape[1]
    meta_at = WINDOW + s_len

    @pl.when(step == pl.num_programs(1) - 1)
    def _():
        _store_heads(nk_ref, k_ref[0, s_len - WINDOW:, :])
        _store_heads(nv_ref, v_ref[0, s_len - WINDOW:, :])

    @pl.when(step == 0)
    def _():
        piece = min(s_len, 512)
        for src, meta, dst in ((k_ref, km_ref, kx_ref), (v_ref, vm_ref, vx_ref)):
            expand = functools.partial(_expand_kv, with_ones=dst is vx_ref)
            blank = lambda n: expand(jnp.zeros((n, src.shape[2]), F32))
            dst[0:WINDOW] = blank(WINDOW)
            for r in range(0, s_len, piece):
                dst[WINDOW + r:WINDOW + r + piece] = expand(src[0, r:r + piece, :])
            dst[meta_at:meta_at + N_META] = expand(meta[...])
            dst[meta_at + N_META:] = blank(dst.shape[0] - meta_at - N_META)

    win = WINDOW + CHUNK
    tail = kx_ref.shape[0] - meta_at
    for j in range(cb):
        c = step * cb + j
        start = pl.multiple_of(c * CHUNK, CHUNK)
        kall = jnp.concatenate([kx_ref[pl.ds(start, win), :], kx_ref[meta_at:meta_at + tail, :]], axis=0)
        vall = jnp.concatenate([vx_ref[pl.ds(start, win), :], vx_ref[meta_at:meta_at + tail, :]], axis=0)
        rows = slice(j * CHUNK, (j + 1) * CHUNK)
        outs = _attn_core(q_ref[0, rows, :], kall, vall, bias_ref[jnp.minimum(c, bias_ref.shape[0] - 1)])
        for ci, o in enumerate(outs):
            o_ref[0, rows, ci * LANES:(ci + 1) * LANES] = o.astype(o_ref.dtype)


def _attn_prompt(q, k, v, k_meta, v_meta, table, sinks):
    b, s, aw = q.shape
    kvw = k.shape[-1]
    nc = s // CHUNK
    cb = ATTN_CHUNKS_PER_STEP if nc % ATTN_CHUNKS_PER_STEP == 0 else 1
    assert s % min(s, 512) == 0
    n_bias = 1
    while True:
        qpos = N_META + (n_bias - 1) * CHUNK
        if np.all(_t5_bucket_np(np.arange(N_META) - qpos) == _t5_bucket_np(np.arange(N_META) - qpos - 10 ** 6)):
            break
        n_bias += 1
    n_bias = min(max(n_bias, WINDOW_CHUNKS + 1), nc)
    cs = np.arange(n_bias)[:, None]
    qpos = N_META + cs * CHUNK + np.arange(CHUNK)[None]
    wpos = N_META + (cs - WINDOW_CHUNKS) * CHUNK + np.arange(WINDOW + CHUNK)[None]
    kpos = np.concatenate([wpos, np.broadcast_to(np.arange(N_META), (n_bias, N_META))], axis=1)
    valid = np.concatenate([wpos >= N_META, np.ones((n_bias, N_META), bool)], axis=1)
    bias = _bias_rows(table, _t5_bucket_np(kpos[:, None, :] - qpos[:, :, None]))
    bias = _pad_keys(jnp.where(valid[:, None, :], bias, -jnp.inf), sinks)
    qs =pl.BlockSpec((1, cb * CHUNK, aw), lambda i, c: (i, c, 0))
    kv = pl.BlockSpec((1, s, kvw), lambda i, c: (i, 0, 0))
    meta = pl.BlockSpec((N_META, kvw), lambda i, c: (0, 0))
    xrows = WINDOW + s + bias.shape[-1] - (WINDOW + CHUNK)
    cache_shape = (1, b, WINDOW, KV_HEADS, HEAD_DIM)
    cache = pl.BlockSpec((1, 1) + cache_shape[2:], lambda i, c: (0, i, 0, 0, 0))
    return pl.pallas_call(
        functools.partial(_attn_prompt_kernel, cb),
        grid=(b, nc // cb),
        in_specs=[qs, kv, kv, meta, meta, pl.BlockSpec(bias.shape, lambda i, c: (0, 0, 0))],
        out_specs=[qs, cache, cache],
        out_shape=[jax.ShapeDtypeStruct((b, s, aw), BF16)] + [jax.ShapeDtypeStruct(cache_shape, F32)] * 2,
        scratch_shapes=[pltpu.VMEM((xrows, KV_EXPAND * LANES), BF16),
                        pltpu.VMEM((xrows, 2 * KV_EXPAND * LANES), BF16)],
        compiler_params=_params("arbitrary", "arbitrary"),
        name="attn_prompt",
    )(q, k, v, k_meta, v_meta, bias)


def _attn_sample_kernel(q_ref, kc_ref, vc_ref, kn_ref, vn_ref, km_ref, vm_ref, bias_ref, o_ref, nk_ref, nv_ref):
    tk = kc_ref.shape[2] + kn_ref.shape[1] + km_ref.shape[0]
    zeros = jnp.zeros((bias_ref.shape[1] - tk, km_ref.shape[1]), F32)
    for s in range(q_ref.shape[0]):
        kc, vc, kn, vn = _load_heads(kc_ref, s), _load_heads(vc_ref, s), kn_ref[s], vn_ref[s]
        kall = _expand_kv(jnp.concatenate([kc, kn, km_ref[...], zeros], axis=0))
        vall = _expand_kv(jnp.concatenate([vc, vn, vm_ref[...], zeros], axis=0), with_ones=True)
        outs = _attn_core(q_ref[s], kall, vall, bias_ref[...])
        for ci, o in enumerate(outs):
            o_ref[s, :, ci * LANES:(ci + 1) * LANES] = o.astype(o_ref.dtype)
        n_new = kn.shape[0]
        _store_heads(nk_ref, jnp.concatenate([kc[n_new:], kn], axis=0), s)
        _store_heads(nv_ref, jnp.concatenate([vc[n_new:], vn], axis=0), s)


def _attn_sample(q, k_cache, v_cache, k_new, v_new, k_meta, v_meta, table, sinks):
    bd, sd, aw = q.shape
    kvw = k_new.shape[-1]
    win = k_cache.shape[2]
    sb = SAMPLE_STREAMS_PER_STEP if bd % SAMPLE_STREAMS_PER_STEP == 0 else 1
    cache = pl.BlockSpec((1, sb) + k_cache.shape[2:], lambda i: (0, i, 0, 0, 0))
    qpos = N_META + PAST_LEN + np.arange(sd)
    kpos = np.concatenate([N_META + PAST_LEN - win + np.arange(win), qpos, np.arange(N_META)])
    bias = _pad_keys(_bias_rows(table, _t5_bucket_np(kpos[None, :] - qpos[:, None])), sinks)
    per = lambda n, w: pl.BlockSpec((sb, n, w), lambda i: (i, 0, 0))
    meta = pl.BlockSpec((N_META, kvw), lambda i: (0, 0))
    return pl.pallas_call(
        _attn_sample_kernel,
        grid=(bd // sb,),
        in_specs=[per(sd, aw), cache, cache, per(sd, kvw), per(sd, kvw), meta, meta,
                  pl.BlockSpec(bias.shape, lambda i: (0, 0))],
        out_specs=[per(sd, aw), cache, cache],
        out_shape=[jax.ShapeDtypeStruct((bd, sd, aw), BF16)] + [jax.ShapeDtypeStruct(k_cache.shape, F32)] * 2,
        compiler_params=_params("arbitrary"),
        name="attn_sample",
    )(q, k_cache, v_cache, k_new, v_new, k_meta, v_meta, bias)


ROUTE_E1, ROUTE_E2, ROUTE_R1, ROUTE_R2, ROUTE_W1, ROUTE_W2 = range(6)
META_FIELDS = 2 * TOP_K


def _merge_kernel(hw, x_ref, yr_ref, at_ref, sga_ref, sgb_ref, wb_ref, wo_ref, wn_ref, wr_ref, br_ref, cnt0_ref, tri_ref,
                  h_ref, xn_ref, route_ref, meta_ref, cnt_ref, carry_ref):
    @pl.when(pl.program_id(0) == 0)
    def _():
        carry_ref[...] = cnt0_ref[...]

    logits = _project_rows(hw, x_ref, yr_ref, at_ref, sga_ref, sgb_ref, wb_ref, wo_ref, wn_ref, wr_ref, br_ref,
                           h_ref, xn_ref)
    _route_rows(logits, tri_ref, route_ref, meta_ref, carry_ref)
    cnt_ref[...] = carry_ref[...]


def _project_rows(hw, x_ref, yr_ref, at_ref, sga_ref, sgb_ref, wb_ref, wo_ref, wn_ref, wr_ref, br_ref, h_ref, xn_ref):
    br = _dot(yr_ref[...], wb_ref[0:hw, :])
    ba = _dot(at_ref[...], wb_ref[hw:, :])
    merged = sga_ref[...].astype(F32) * br + sgb_ref[...].astype(F32) * ba
    h = x_ref[...] + _dot(merged.astype(BF16), wo_ref[...])
    h_ref[...] = h
    ms = jnp.mean(h * h, axis=-1, keepdims=True)
    xn = h * lax.rsqrt(ms + EPS) * wn_ref[...]
    _store_row_tiles(xn_ref, xn)

    x_hi = xn.astype(BF16)
    x_lo = (xn - x_hi.astype(F32)).astype(BF16)
    both = _dot(x_hi, wr_ref[...])
    return both[:, :LANES] + both[:, LANES:] + _dot(x_lo, wr_ref[:, :LANES]) + br_ref[...]


def _route_rows(logits, tri_ref, route_ref, meta_ref, carry_ref):
    tm = logits.shape[0]
    lt = logits.T
    row_i = lax.broadcasted_iota(I32, (LANES, tm), 0)
    row = row_i.astype(F32)
    group_of_row = (row_i >> int(math.log2(EXPERTS_PER_GROUP))).astype(F32)
    ninf = -jnp.inf
    first = lambda hit, idx: jnp.min(jnp.where(hit, idx, float(LANES)), axis=0, keepdims=True)
    gmask = (row_i >= N_EXPERTS) & (row_i < N_EXPERTS + N_GROUPS)
    gl = jnp.where(gmask, lt, ninf)
    gmax = jnp.max(gl, axis=0, keepdims=True)
    gidx = first(gl == gmax, row - N_EXPERTS)
    gval = 1.0 / jnp.sum(jnp.exp(gl - gmax), axis=0, keepdims=True)
    emask = (row_i < N_EXPERTS) & (group_of_row == gidx)
    el = jnp.where(emask, lt, ninf)
    m1 = jnp.max(el, axis=0, keepdims=True)
    i1 = first(el == m1, row)
    el2 = jnp.where(row == i1, ninf, el)
    m2 = jnp.max(el2, axis=0, keepdims=True)
    i2 = first(el2 == m2, row)
    e21 = jnp.exp(m2 - m1)
    w1 = gval / (1.0 + e21)
    w2 = gval * e21 / (1.0 + e21)

    sel1 = row == i1
    sel2 = row == i2
    oh = (sel1 | sel2).astype(BF16)
    before = _dot(oh, tri_ref[...]) + carry_ref[...]
    r1 = jnp.sum(jnp.where(sel1, before, 0.0), axis=0, keepdims=True)
    r2 = jnp.sum(jnp.where(sel2, before, 0.0), axis=0, keepdims=True)
    carry_ref[...] = carry_ref[...] + jnp.sum(oh.astype(F32), axis=1, keepdims=True)

    rec = jnp.zeros((LANES, tm), F32)
    for slot, val in ((ROUTE_E1, i1), (ROUTE_E2, i2), (ROUTE_W1, w1), (ROUTE_W2, w2), (ROUTE_R1, r1), (ROUTE_R2, r2)):
        rec = jnp.where(row_i == slot, val, rec)
    route_ref[...] = rec.T
    meta_ref[0] = rec[ROUTE_E1:ROUTE_E1 + META_FIELDS].astype(I32)


def _merge(x, y_rec, att, sga, sgb, wb, wo, w_norm, w_router, b_router, cnt0):
    t, d = x.shape
    hw = y_rec.shape[1]
    tm = PROJ_TILE if t % PROJ_TILE == 0 else t
    row = lambda w: pl.BlockSpec((tm, w), lambda i: (i, 0))
    const = lambda a, b: pl.BlockSpec((a, b), lambda i: (0, 0))
    return pl.pallas_call(
        functools.partial(_merge_kernel, hw),
        grid=(t // tm,),
        in_specs=[row(d), row(hw), row(att.shape[1]), row(d), row(d), const(*wb.shape), const(d, d), const(1, d),
                  const(d, 2 * LANES), const(1, LANES), const(LANES, 1), const(tm, tm)],
        out_specs=[row(d), pl.BlockSpec((tm * d // LANES, LANES), lambda i: (i, 0)), row(LANES),
                   pl.BlockSpec((1, META_FIELDS, tm), lambda i: (i, 0, 0)), const(LANES, 1)],
        out_shape=[jax.ShapeDtypeStruct((t, d), F32), jax.ShapeDtypeStruct((t * d // LANES, LANES), F32),
                   jax.ShapeDtypeStruct((t, LANES), F32), jax.ShapeDtypeStruct((t // tm, META_FIELDS, tm), I32),
                   jax.ShapeDtypeStruct((LANES, 1), F32)],
        scratch_shapes=[pltpu.VMEM((LANES, 1), F32)],
        compiler_params=_params("arbitrary"),
        name="merge_route",
    )(x, y_rec, att, sga, sgb, wb, wo, w_norm.reshape(1, d), w_router, b_router, cnt0,
      jnp.asarray(np.triu(np.ones((tm, tm)), k=1), dtype=BF16))


def _row_copy(src, dst, sem):
    return pltpu.make_async_copy(src, dst, sem)


ROW_UNROLL = 8


def _tile_rows(ref, row, sub):
    return ref.at[pl.ds(pl.multiple_of(row * sub, sub), sub)]


def _scatter_kernel(n_tok, sub, first, slot_ref, pstart_ref, pend_ref, x_ref, *rest):
    xs_ref, zero_ref, sem, zsem, tsem = rest[-5:]

    def zero_blocks(blocks, zs, wait):
        for cond, row in blocks:
            @pl.when(cond)
            def _():
                at = row * sub if isinstance(row, int) else pl.multiple_of(row * sub, EXPERT_TILE * sub)
                cp = _row_copy(zero_ref, xs_ref.at[pl.ds(at, EXPERT_TILE * sub)], zs)
                cp.wait() if wait else cp.start()

    if first:
        n_rows = xs_ref.shape[0] // sub
        tails = [(pend_ref[e] > pstart_ref[e], pend_ref[e] - EXPERT_TILE) for e in range(N_EXPERTS)]
        unused = [(n_rows - (j + 1) * EXPERT_TILE >= pend_ref[N_EXPERTS - 1], n_rows - (j + 1) * EXPERT_TILE)
                  for j in range(N_EXPERTS)]

        @pl.when(pl.program_id(0) == 0)
        def _():
            zero_ref[...] = jnp.zeros_like(zero_ref)
            zero_blocks(tails, zsem, False)
            zero_blocks(unused, tsem, False)
            zero_blocks(tails, zsem, True)

    def issue(grp, _):
        for u in range(ROW_UNROLL):
            r = grp * ROW_UNROLL + u
            for k in range(TOP_K):
                _row_copy(_tile_rows(x_ref, r, sub), _tile_rows(xs_ref, slot_ref[0, k, r], sub),
                          sem).start(priority=(u * TOP_K + k) % 2)
        return 0

    lax.fori_loop(0, n_tok // ROW_UNROLL, issue, 0)
    for k in range(TOP_K):
        _row_copy(x_ref, xs_ref.at[pl.ds(0, n_tok * sub)], sem).wait()

    if first:
        @pl.when(pl.program_id(0) == 0)
        def _():
            zero_blocks(unused, tsem, True)


def _scatter_rows(x, slots, pstarts, pends, xs, rows):
    n, fields, tm = slots.shape
    sub = x.shape[0] // (n * tm)
    first = xs is None
    smem = pl.BlockSpec(memory_space=pltpu.SMEM)
    return pl.pallas_call(
        functools.partial(_scatter_kernel, tm, sub, first),
        grid=(n,),
        in_specs=[pl.BlockSpec((1, fields, tm), lambda i: (i, 0, 0), memory_space=pltpu.SMEM), smem, smem,
                  pl.BlockSpec((tm * sub, LANES), lambda i: (i, 0))]
        + ([] if first else [pl.BlockSpec(memory_space=pl.ANY)]),
        out_specs=pl.BlockSpec(memory_space=pl.ANY),
        out_shape=jax.ShapeDtypeStruct((rows * sub, LANES), F32),
        scratch_shapes=[pltpu.VMEM((EXPERT_TILE * sub, LANES), F32)] + [pltpu.SemaphoreType.DMA(())] * 3,
        input_output_aliases={} if first else {4: 0},
        compiler_params=_params("arbitrary"),
        name="moe_scatter",
    )(slots, pstarts, pends, x, *([] if first else [xs]))


def _expert_kernel(tm, sub, be_ref, nb_ref, nxt_ref, x_ref, wg_ref, wu_ref, wd_ref, y_ref,
                   raw_g, raw_u, raw_d, wgb_ref, wub_ref, wdb_ref, sem, slot_ref):
    i = pl.program_id(0)
    used = i < nb_ref[0]
    first = used & ((i == 0) | (be_ref[i] != be_ref[jnp.maximum(i - 1, 0)]))

    def fetch(e, slot):
        return [_row_copy(w.at[e], raw.at[slot], sem.at[slot])
                for w, raw in ((wg_ref, raw_g), (wu_ref, raw_u), (wd_ref, raw_d))]

    @pl.when(i == 0)
    def _():
        slot_ref[0] = 0
        for cp in fetch(be_ref[0], 0):
            cp.start()

    @pl.when(first)
    def _():
        slot = slot_ref[0]
        for cp in fetch(be_ref[i], slot):
            cp.wait()
        wgb_ref[...] = raw_g[slot].astype(BF16)
        wub_ref[...] = raw_u[slot].astype(BF16)
        wdb_ref[...] = raw_d[slot].astype(BF16)
        slot_ref[0] = 1 - slot

        @pl.when(nxt_ref[i] >= 0)
        def _():
            for cp in fetch(nxt_ref[i], 1 - slot):
                cp.start()

    @pl.when(used)
    def _():
        x = _load_row_tiles(x_ref, tm, sub).astype(BF16)
        g = _dot(x, wgb_ref[...])
        u = _dot(x, wub_ref[...])
        hmid = (g * _sigmoid(g) * u).astype(BF16)
        _store_row_tiles(y_ref, _dot(hmid, wdb_ref[...]))

    @pl.when(jnp.logical_not(used))
    def _():
        y_ref[...] = jnp.zeros_like(y_ref)


def _experts(xs, block_e, n_used, nxt_e, wg, wu, wd):
    _, d, ff = wg.shape
    sub = d // LANES
    tm = EXPERT_TILE
    blk = pl.BlockSpec((tm * sub, LANES), lambda i, be, nb, nx: (i, 0))
    hbm = pl.BlockSpec(memory_space=pl.ANY)
    grid_spec = pltpu.PrefetchScalarGridSpec(
        num_scalar_prefetch=3,
        grid=(xs.shape[0] // (tm * sub),),
        in_specs=[blk, hbm, hbm, hbm],
        out_specs=blk,
        scratch_shapes=[pltpu.VMEM((2, d, ff), F32), pltpu.VMEM((2, d, ff), F32), pltpu.VMEM((2, ff, d), F32),
                        pltpu.VMEM((d, ff), BF16), pltpu.VMEM((d, ff), BF16), pltpu.VMEM((ff, d), BF16),
                        pltpu.SemaphoreType.DMA((2,)), pltpu.SMEM((1,), I32)],
    )
    return pl.pallas_call(
        functools.partial(_expert_kernel, tm, sub),
        grid_spec=grid_spec,
        out_shape=jax.ShapeDtypeStruct(xs.shape, F32),
        compiler_params=_params("arbitrary"),
        name="moe_experts",
    )(block_e, n_used, nxt_e, xs, wg, wu, wd)


def _combine_kernel(n_tok, sub, scur_ref, snext_ref, h_ref, route_ref, ys_ref, o_ref, buf, sem):
    i = pl.program_id(0)
    n = pl.num_programs(0)
    slot = i % 2

    def start(slot_ref, s):
        def issue(grp, _):
            for u in range(ROW_UNROLL):
                r = grp * ROW_UNROLL + u
                for k in range(TOP_K):
                    _row_copy(_tile_rows(ys_ref, slot_ref[0, k, r], sub), _tile_rows(buf.at[s, k], r, sub),
                              sem.at[s]).start(priority=(u * TOP_K + k) % 2)
            return 0
        lax.fori_loop(0, n_tok // ROW_UNROLL, issue, 0)

    @pl.when(i == 0)
    def _():
        start(scur_ref, 0)

    @pl.when(i + 1 < n)
    def _():
        start(snext_ref, 1 - slot)

    for k in range(TOP_K):
        _row_copy(ys_ref.at[pl.ds(0, n_tok * sub)], buf.at[slot, k], sem.at[slot]).wait()
    route = route_ref[...]
    w1 = route[:, ROUTE_W1:ROUTE_W1 + 1]
    w2 = route[:, ROUTE_W2:ROUTE_W2 + 1]
    for j in range(sub):
        cols = slice(j * LANES, (j + 1) * LANES)
        part = lambda k: buf[slot, k, pl.ds(j, n_tok, stride=sub), :]
        o_ref[:, cols] = h_ref[:, cols] + (part(0) * w1 + part(1) * w2)


def _combine(h, route, slots, ys):
    t, d = h.shape
    n, fields, tm = slots.shape
    sub = d // LANES
    mspec = lambda f: pl.BlockSpec((1, fields, tm), f, memory_space=pltpu.SMEM)
    return pl.pallas_call(
        functools.partial(_combine_kernel, tm, sub),
        grid=(n,),
        in_specs=[mspec(lambda i: (i, 0, 0)), mspec(lambda i: (jnp.minimum(i + 1, n - 1), 0, 0)),
                  pl.BlockSpec((tm, d), lambda i: (i, 0)), pl.BlockSpec((tm, LANES), lambda i: (i, 0)),
                  pl.BlockSpec(memory_space=pl.ANY)],
        out_specs=pl.BlockSpec((tm, d), lambda i: (i, 0)),
        out_shape=jax.ShapeDtypeStruct((t, d), F32),
        scratch_shapes=[pltpu.VMEM((2, TOP_K, tm * sub, LANES), F32), pltpu.SemaphoreType.DMA((2,))],
        compiler_params=_params("arbitrary"),
        name="moe_combine",
    )(slots, slots, h, route, ys)


def _moe(parts, counts, wg, wu, wd):
    tm = EXPERT_TILE
    n_assign = sum(part[0].shape[0] for part in parts) * TOP_K
    n_blocks = -(-(n_assign + N_EXPERTS * (tm - 1)) // tm)
    counts = counts[:N_EXPERTS, 0].astype(I32)
    pcounts = (counts + tm - 1) // tm * tm
    pends = jnp.cumsum(pcounts)
    pstarts = pends - pcounts
    block_start = jnp.arange(n_blocks, dtype=I32) * tm
    block_e = jnp.minimum(jnp.sum((pends[None, :] <= block_start[:, None]).astype(I32), axis=1), N_EXPERTS - 1)
    n_used = pends[-1:] // tm
    ids = jnp.arange(N_EXPERTS, dtype=I32)
    later = (ids[None, :] > ids[:, None]) & (pcounts[None, :] > 0)
    nxt_of = jnp.min(jnp.where(later, ids[None, :], N_EXPERTS), axis=1)
    nxt_of = jnp.where(nxt_of < N_EXPERTS, nxt_of, -1)
    nxt_e = jnp.sum(jnp.where(block_e[:, None] == ids[None, :], nxt_of[None, :], 0), axis=1)
    xs = None
    slots = []
    for _, xn, _, meta in parts:
        assert meta.shape[-1] % ROW_UNROLL == 0
        e, pos = meta[:, :TOP_K], meta[:, TOP_K:]
        seg = jnp.sum(jnp.where(e[..., None] == jnp.arange(N_EXPERTS, dtype=I32), pstarts, 0), axis=-1)
        slots.append(seg + pos)
        xs = _scatter_rows(xn, slots[-1], pstarts, pends, xs, n_blocks * tm)
    ys = _experts(xs, block_e, n_used, nxt_e, wg, wu, wd)
    return [_combine(h, route, _retile(s, COMBINE_TILE), ys) for (h, _, route, _), s in zip(parts, slots)]


def _retile(slots, tile):
    n, k, t = slots.shape
    if t % tile or t == tile:
        return slots
    return slots.reshape(n, k, t // tile, tile).transpose(0, 2, 1, 3).reshape(n * (t // tile), k, tile)


def kernel(x_prompt, x_sample, cache_swa_k, cache_swa_v, state_hgrn, meta_tokens, rel_bias_table, hgrn_lower_bounds, w_norm_mix, w_in, hgrn_out_norm, q_norm, k_norm, attn_sinks, w_branch, w_out, w_norm_ffn, w_router_group, b_router_group, w_router_expert, b_router_expert, w_expert_gate, w_expert_up, w_expert_down):
    b, s, d = x_prompt.shape
    bd, sd, _ = x_sample.shape
    depth, _, heads, dk, dv = state_hgrn.shape
    assert depth == 1 and heads == HGRN_HEADS and dk == dv
    hw = heads * dk
    aw = ATTN_HEADS * HEAD_DIM
    kvw = KV_HEADS * HEAD_DIM
    assert w_in.shape[-1] == 4 * hw + aw + 2 * kvw + 2 * d
    assert s % HGRN_CHUNK == 0 and s % CHUNK == 0 and sd == N_META and N_EXPERTS + N_GROUPS <= LANES
    l = 0

    p = jax.nn.softmax(hgrn_lower_bounds.astype(F32), axis=0)
    lb = jnp.cumsum(p, axis=0)[l + 1] - p[0]

    w_in_b = w_in[l].astype(BF16)
    proj = functools.partial(_inproj, w_norm=w_norm_mix[l], w_in_bf16=w_in_b, q_gain=q_norm[l], k_gain=k_norm[l],
                             lb=lb, hw=hw, aw=aw, kvw=kvw)
    x_small = jnp.concatenate([x_sample.reshape(bd * sd, d), meta_tokens.astype(F32)], axis=0)
    *rec_s, qa_s, k_s, v_s, sga_s, sgb_s = proj(x_small)
    *rec_p, qa_p, k_p, v_p, sga_p, sgb_p = proj(x_prompt.reshape(b * s, d))
    ns = bd * sd
    k_meta, v_meta = k_s[ns:], v_s[ns:]

    streams = lambda a, n: a.reshape(n, -1, a.shape[-1])
    s0_small = jnp.concatenate([state_hgrn[l].astype(F32), jnp.zeros((1, heads, dk, dv), F32)], axis=0)
    y_small, st_small = _hgrn(*(streams(a, bd + 1) for a in rec_s), hgrn_out_norm[l], s0_small, sd)
    y_p, st_p = _hgrn(*(streams(a, b) for a in rec_p), hgrn_out_norm[l], st_small, HGRN_CHUNK, shared_s0=bd)

    table = rel_bias_table.astype(F32)
    att_p, new_k_p, new_v_p = _attn_prompt(streams(qa_p, b), streams(k_p, b), streams(v_p, b), k_meta, v_meta,
                                           table, attn_sinks[l])
    k_new, v_new = k_s[:ns].reshape(bd, sd, kvw), v_s[:ns].reshape(bd, sd, kvw)
    att_s, new_k_s, new_v_s = _attn_sample(qa_s[:ns].reshape(bd, sd, aw), cache_swa_k[l:l + 1].astype(F32),
                                           cache_swa_v[l:l + 1].astype(F32), k_new, v_new, k_meta, v_meta,
                                           table, attn_sinks[l])

    wb = w_branch[l].astype(BF16)
    wo = w_out[l].astype(BF16)
    w_router = jnp.pad(jnp.concatenate([w_router_expert[l], w_router_group[l]], axis=1).astype(F32),
                       ((0, 0), (0, LANES - N_EXPERTS - N_GROUPS)))
    b_router = jnp.pad(jnp.concatenate([b_router_expert[l], b_router_group[l]]).astype(F32),
                       (0, LANES - N_EXPERTS - N_GROUPS)).reshape(1, LANES)
    w_router_hi = w_router.astype(BF16)
    w_router_lo = (w_router - w_router_hi.astype(F32)).astype(BF16)
    w_router = jnp.concatenate([w_router_hi, w_router_lo], axis=1)
    merge = functools.partial(_merge, wb=wb, wo=wo, w_norm=w_norm_ffn[l], w_router=w_router, b_router=b_router)
    *part_p, cnt_p = merge(x_prompt.reshape(b * s, d), y_p.reshape(b * s, hw), att_p.reshape(b * s, aw),
                           sga_p, sgb_p, cnt0=jnp.zeros((LANES, 1), F32))
    *part_s, cnt_s = merge(x_sample.reshape(ns, d), y_small[:bd].reshape(ns, hw), att_s.reshape(ns, aw),
                           sga_s[:ns], sgb_s[:ns], cnt0=cnt_p)

    out_p, out_s = _moe([part_p, part_s], cnt_s, w_expert_gate[l], w_expert_up[l], w_expert_down[l])

    return (out_p.reshape(b, s, d), out_s.reshape(bd, sd, d), new_k_p, new_v_p, st_p[None],
            new_k_s, new_v_s, st_small[:bd][None])
```

```python
import functools
import math

import numpy as np
import jax
import jax.numpy as jnp
from jax import lax
from jax.experimental import pallas as pl
from jax.experimental.pallas import tpu as pltpu

F32 = jnp.float32
BF16 = jnp.bfloat16
I32 = jnp.int32

CHUNK = 64
N_META = 16
PAST_LEN = 2048
EPS = 1e-6
HGRN_HEADS = 4
ATTN_HEADS = 8
KV_HEADS = 2
HEAD_DIM = 64
GQA_GROUP = ATTN_HEADS // KV_HEADS
WINDOW = 128
WINDOW_CHUNKS = WINDOW // CHUNK
NUM_BUCKETS = 32
MAX_DISTANCE = 128
N_GROUPS = 4
EXPERTS_PER_GROUP = 8
N_EXPERTS = N_GROUPS * EXPERTS_PER_GROUP
TOP_K = 2

LANES = 128
MXU_WIDTH = 256
VMEM_LIMIT = 56 * 1024 * 1024

INPROJ_TILE = 512
PROJ_TILE = 1024
HGRN_CHUNK = 128
HGRN_CHUNKS_PER_STEP = 8
ATTN_CHUNKS_PER_STEP = 16
SAMPLE_STREAMS_PER_STEP = 4
EXPERT_TILE = 512
SCATTER_TILE = 2048
COMBINE_TILE = 512


LOG2E = math.log2(math.e)


def _sigmoid(x):
    return 1.0 / (1.0 + jnp.exp(-x))


def _split3(x):
    hi = x.astype(BF16)
    r1 = x - hi.astype(F32)
    mid = r1.astype(BF16)
    lo = (r1 - mid.astype(F32)).astype(BF16)
    return hi, mid, lo


def _dot(a, b):
    return jnp.dot(a, b, preferred_element_type=F32)


def _dot_nt(a, b):
    return lax.dot_general(a, b, (((1,), (1,)), ((), ())), preferred_element_type=F32)


def _dot_tn(a, b):
    return lax.dot_general(a, b, (((0,), (0,)), ((), ())), preferred_element_type=F32)


SUBLANES = 8


def _store_row_tiles(ref, x):
    n, d = x.shape
    sub = d // LANES
    for j in range(sub):
        ref[pl.ds(j, n, stride=sub), :] = x[:, j * LANES:(j + 1) * LANES]


def _load_row_tiles(ref, n, sub):
    return jnp.concatenate([ref[pl.ds(j, n, stride=sub), :] for j in range(sub)], axis=1)


def _params(*sem):
    return pltpu.CompilerParams(dimension_semantics=sem, vmem_limit_bytes=VMEM_LIMIT)


def _inproj_kernel(hw, aw, kvw, d, x_ref, wn_ref, w_ref, qg_ref, kg_ref, bdq_ref, bdk_ref, lb_ref,
                   qr_ref, lf_ref, kin_ref, hv_ref, hg_ref, qa_ref, k_ref, v_ref, sga_ref, sgb_ref):
    x = x_ref[...]
    ms = jnp.mean(x * x, axis=-1, keepdims=True)
    xn = (x * lax.rsqrt(ms + EPS) * wn_ref[...]).astype(BF16)

    def seg(a, b):
        return _dot(xn, w_ref[:, a:b])

    def head_rms(a, bd_ref, gain):
        sq = (a * a).astype(BF16)
        wb = bd_ref.shape[0]
        m = jnp.concatenate([_dot(sq[:, c:c + wb], bd_ref[...]) for c in range(0, a.shape[1], wb)], axis=1)
        return a * lax.rsqrt(m + EPS) * gain

    o = 0
    hq = seg(o, o + hw)
    qr_ref[...] = (hq * _sigmoid(hq) * (hw // HGRN_HEADS) ** -0.5).astype(BF16)
    o += hw
    z = seg(o, o + hw)
    lb = lb_ref[...]
    e = jnp.exp(-jnp.abs(z))
    r = 1.0 / (1.0 + e)
    pos = z >= 0
    lf_ref[...] = jnp.log2(lb + (1.0 - lb) * jnp.where(pos, r, e * r))
    kin_ref[...] = ((1.0 - lb) * jnp.where(pos, e * r, r)).astype(BF16)
    o += hw
    hv_ref[...] = seg(o, o + hw).astype(BF16)
    o += hw
    hg = seg(o, o + hw)
    hg_ref[...] = (hg * _sigmoid(hg)).astype(BF16)
    o += hw
    aq = seg(o, o + aw)
    qa_ref[...] = (head_rms(aq, bdq_ref, qg_ref[...]) * (HEAD_DIM ** -0.5 * LOG2E)).astype(BF16)
    o += aw
    kv = seg(o, o + 2 * kvw)
    k_ref[...] = head_rms(kv[:, :kvw], bdk_ref, kg_ref[...])
    v_ref[...] = kv[:, kvw:]
    o += 2 * kvw
    sga_ref[...] = _sigmoid(seg(o, o + d)).astype(BF16)
    o += d
    sgb_ref[...] = _sigmoid(seg(o, o + d)).astype(BF16)


def _block_diag_mean(width, group):
    i = np.arange(width)
    return jnp.asarray((i[:, None] // group == i[None, :] // group) / group, dtype=BF16)


def _inproj(x, w_norm, w_in_bf16, q_gain, k_gain, lb, hw, aw, kvw):
    t, d = x.shape
    tm = INPROJ_TILE if t % INPROJ_TILE == 0 else t
    cols = w_in_bf16.shape[1]
    row = lambda w: pl.BlockSpec((tm, w), lambda i: (i, 0))
    const = lambda a, b: pl.BlockSpec((a, b), lambda i: (0, 0))
    outs = [(hw, BF16), (hw, F32), (hw, BF16), (hw, BF16), (hw, BF16), (aw, BF16), (kvw, F32), (kvw, F32),
            (d, BF16), (d, BF16)]
    bdq = min(aw, MXU_WIDTH)
    assert aw % bdq == 0
    return pl.pallas_call(
        functools.partial(_inproj_kernel, hw, aw, kvw, d),
        grid=(t // tm,),
        in_specs=[row(d), const(1, d), const(d, cols), const(1, aw), const(1, kvw), const(bdq, bdq), const(kvw, kvw),
                  const(1, hw)],
        out_specs=[row(w) for w, _ in outs],
        out_shape=[jax.ShapeDtypeStruct((t, w), dt) for w, dt in outs],
        compiler_params=_params("arbitrary"),
        name="inproj",
    )(x, w_norm.reshape(1, d), w_in_bf16,
      jnp.tile(q_gain, aw // HEAD_DIM).reshape(1, aw), jnp.tile(k_gain, kvw // HEAD_DIM).reshape(1, kvw),
      _block_diag_mean(bdq, HEAD_DIM), _block_diag_mean(kvw, HEAD_DIM), lb.reshape(1, hw))


def _hgrn_consts(L):
    t = np.arange(L)
    u = t[None, :]
    blocks = [u <= t[:, None], u > t[:, None]]
    levels = []
    m = L // 2
    while m >= 1:
        levels.append(m)
        m //= 2
    lvl = np.full((L, L), -1, np.int32)
    lvl[t, t] = len(levels)
    isq_cols = []
    for j, m in enumerate(levels):
        bnd = (t // (2 * m)) * (2 * m) + m - 1
        isq = (t % (2 * m)) >= m
        cq = isq[:, None] & (u > bnd[:, None]) & (u <= t[:, None])
        ck = (~isq)[:, None] & (u > t[:, None]) & (u <= bnd[:, None])
        blocks.append(cq | ck)
        same = (t[:, None] // (2 * m)) == (t[None, :] // (2 * m))
        lvl[same & isq[:, None] & (~isq)[None, :]] = j
        isq_cols.append(isq)
    c = np.concatenate(blocks, axis=0).astype(np.float32)
    isq = np.stack(isq_cols, axis=1).astype(np.float32)
    isq = np.pad(isq, ((0, 0), (0, LANES - isq.shape[1])))
    c2 = np.tile(c, (1, 2))
    return jnp.asarray(c2, dtype=BF16), jnp.asarray(np.tile(lvl, (1, 2))), jnp.asarray(isq), len(levels)


def _hgrn_kernel(L, nlev, heads, dk, qr_ref, lf_ref, kin_ref, hv_ref, hg_ref, og_ref, c_ref, lvl_ref, isq_ref,
                 s0_ref, y_ref, sout_ref, st_ref):
    c = pl.program_id(1)

    @pl.when(c == 0)
    def _():
        for h in range(heads):
            st_ref[h] = s0_ref[0, h].T

    for cc in range(lf_ref.shape[1] // L):
        _hgrn_chunk(L, nlev, heads, dk, slice(cc * L, (cc + 1) * L), qr_ref, lf_ref, kin_ref, hv_ref, hg_ref, og_ref,
                    c_ref, lvl_ref, isq_ref, y_ref, st_ref)

    @pl.when(c == pl.num_programs(1) - 1)
    def _():
        for h in range(heads):
            sout_ref[0, h] = st_ref[h].T


def _hgrn_chunk(L, nlev, heads, dk, rows, qr_ref, lf_ref, kin_ref, hv_ref, hg_ref, og_ref, c_ref, lvl_ref, isq_ref,
                y_ref, st_ref):
    kin = kin_ref[0, rows, :].astype(F32)
    q = qr_ref[0, rows, :].astype(F32)

    hi, mid, _ = _split3(lf_ref[0, rows, :])
    ex = jnp.exp2(_dot(c_ref[...], jnp.concatenate([hi, mid], axis=0)))
    e_b = ex[0:L]
    e_rev = ex[L:2 * L]

    q_in = (q * e_b).astype(BF16)
    k_out = (kin * e_rev).astype(BF16)
    q_b = q.astype(BF16)
    k_b = kin.astype(BF16)
    xs = []
    for j in range(nlev):
        m = L >> (j + 1)
        if m % SUBLANES == 0:
            qk = jnp.concatenate([(q if blk % 2 else kin)[blk * m:(blk + 1) * m] for blk in range(L // m)], axis=0)
        else:
            qk = jnp.where(isq_ref[:, j:j + 1] > 0.5, q, kin)
        xs.append((qk * ex[(2 + j) * L:(3 + j) * L]).astype(BF16))
    lvl = lvl_ref[...]
    v = hv_ref[0, rows, :]
    g = hg_ref[0, rows, :].astype(F32)
    og = og_ref[...]

    def block_diag(x):
        zero = jnp.zeros((x.shape[0], dk), x.dtype)
        return jnp.concatenate([jnp.concatenate([x[:, :dk], zero], axis=1),
                                jnp.concatenate([zero, x[:, dk:]], axis=1)], axis=0)

    for pair in range(heads // 2):
        sl = slice(2 * pair * dk, 2 * (pair + 1) * dk)
        a = jnp.where(lvl == nlev, _dot_nt(q_b[:, sl], block_diag(k_b[:, sl])), 0.0)
        for j in range(nlev):
            xp = xs[j][:, sl]
            a = jnp.where(lvl == j, _dot_nt(xp, block_diag(xp)), a)
        st = jnp.concatenate([st_ref[2 * pair], st_ref[2 * pair + 1]], axis=1)
        vp = v[:, sl]
        o = _dot(a.astype(BF16), block_diag(vp)) + _dot_nt(q_in[:, sl], block_diag(st.astype(BF16)))
        for half in range(2):
            h = 2 * pair + half
            hs = slice(h * dk, (h + 1) * dk)
            st_ref[h] = st_ref[h] * e_b[L - 1:L, hs] + _dot_tn(v[:, hs], k_out[:, hs])
            oh = o[:, half * dk:(half + 1) * dk]
            ms = jnp.mean(oh * oh, axis=-1, keepdims=True)
            gh = g[:, hs]
            y_ref[0, rows, hs] = (oh * lax.rsqrt(ms + EPS) * og[:, hs] * gh).astype(BF16)


def _hgrn(qr, lf, kin, hv, hg, out_gain, s0, L, shared_s0=None):
    b, s, w = lf.shape
    heads, dk = s0.shape[1], s0.shape[2]
    cm, lvl, isq, nlev = _hgrn_consts(L)
    per_step = HGRN_CHUNKS_PER_STEP if s % (HGRN_CHUNKS_PER_STEP * L) == 0 else 1
    seq = pl.BlockSpec((1, per_step * L, w), lambda i, c: (i, c, 0))
    const = lambda a: pl.BlockSpec(a.shape, lambda i, c: (0,) * a.ndim)
    state = pl.BlockSpec((1, heads, dk, dk), lambda i, c: (i, 0, 0, 0))
    start = state if shared_s0 is None else pl.BlockSpec((1, heads, dk, dk), lambda i, c: (shared_s0, 0, 0, 0))
    og2 = jnp.tile(out_gain, heads).reshape(1, w)
    return pl.pallas_call(
        functools.partial(_hgrn_kernel, L, nlev, heads, dk),
        grid=(b, s // (per_step * L)),
        in_specs=[seq, seq, seq, seq, seq, const(og2), const(cm), const(lvl), const(isq), start],
        out_specs=[seq, state],
        out_shape=[jax.ShapeDtypeStruct((b, s, w), BF16), jax.ShapeDtypeStruct((b,) + s0.shape[1:], F32)],
        scratch_shapes=[pltpu.VMEM((heads, dk, dk), F32)],
        compiler_params=_params("arbitrary", "arbitrary"),
        name=f"hgrn_scan_{L}",
    )(qr, lf, kin, hv, hg, og2, cm, lvl, isq, s0)


def _t5_bucket_np(rel):
    half = NUM_BUCKETS // 2
    max_exact = half // 2
    assert (NUM_BUCKETS, MAX_DISTANCE) == (32, 128)
    n = np.abs(rel).astype(np.int64)
    nn = np.maximum(n, 1)
    k = np.zeros_like(nn)
    for j in range(1, 48):
        k = np.where(64 * (1 << j) <= nn * nn, j, k)
    large = np.minimum(max_exact + k, half - 1)
    return np.where(rel > 0, half, 0) + np.where(n < max_exact, n, large)


HEADS_PER_COL = LANES // HEAD_DIM
COLS_PER_GROUP = GQA_GROUP // HEADS_PER_COL
HEAD_ORDER = tuple(g * GQA_GROUP + col * HEADS_PER_COL + half
                   for g in range(KV_HEADS) for half in range(HEADS_PER_COL) for col in range(COLS_PER_GROUP))
KV_EXPAND = KV_HEADS * HEADS_PER_COL


def _expand_kv(x, with_ones=False):
    assert HEADS_PER_COL == 2 and KV_HEADS == 2 and x.shape[1] == LANES
    low = lax.broadcasted_iota(I32, x.shape, 1) < HEAD_DIM
    xr = pltpu.roll(x, HEAD_DIM, axis=1)
    zero = jnp.zeros_like(x)
    blocks = [jnp.where(low, x, zero), jnp.where(low, zero, xr), jnp.where(low, xr, zero), jnp.where(low, zero, x)]
    if with_ones:
        blocks = [b for blk in blocks for b in (blk, jnp.ones_like(x))]
    return jnp.concatenate(blocks, axis=1).astype(BF16)


def _attn_core(q, kx, vx, bias):
    tq = q.shape[0]
    scores = []
    for g in range(KV_HEADS):
        cols = [q[:, (g * COLS_PER_GROUP + c) * LANES:(g * COLS_PER_GROUP + c + 1) * LANES]
                for c in range(COLS_PER_GROUP)]
        qst = jnp.concatenate(cols, axis=0)
        for half in range(HEADS_PER_COL):
            blk = g * HEADS_PER_COL + half
            scores.append(_dot_nt(qst, kx[:, blk * LANES:(blk + 1) * LANES]))
    s = jnp.concatenate(scores, axis=0) + bias
    pb = jnp.exp2(s - jnp.max(s, axis=-1, keepdims=True)).astype(BF16)
    rows = COLS_PER_GROUP * tq
    outs = []
    for g in range(KV_HEADS):
        o = None
        for half in range(HEADS_PER_COL):
            blk = g * HEADS_PER_COL + half
            pv = _dot(pb[blk * rows:(blk + 1) * rows], vx[:, 2 * blk * LANES:2 * (blk + 1) * LANES])
            part = pv[:, :LANES] * (1.0 / pv[:, LANES:])
            o = part if o is None else o + part
        outs.extend(o[c * tq:(c + 1) * tq] for c in range(COLS_PER_GROUP))
    return outs


def _bias_rows(table, bucket):
    onehot = (jnp.asarray(bucket)[..., None] == jnp.arange(NUM_BUCKETS)).astype(F32)
    cols = jnp.stack([table[:, h] for h in HEAD_ORDER], axis=1)
    bias = jnp.einsum('...qkb,bh->...hqk', onehot, cols, precision=lax.Precision.HIGHEST)
    return bias.reshape(*bucket.shape[:-2], ATTN_HEADS * bucket.shape[-2], bucket.shape[-1]) * LOG2E


def _pad_keys(bias, sinks):
    tk = bias.shape[-1]
    n_pad = -tk % LANES or LANES
    tq = bias.shape[-2] // ATTN_HEADS
    sink = jnp.repeat(jnp.stack([sinks[h] for h in HEAD_ORDER]).astype(F32), tq) * LOG2E
    sink = jnp.broadcast_to(sink[:, None], bias.shape[:-1] + (1,))
    masked = jnp.full(bias.shape[:-1] + (n_pad - 1,), -jnp.inf, F32)
    return jnp.concatenate([bias, sink, masked], axis=-1)


def _store_heads(ref, x, stream=0):
    for g in range(KV_HEADS):
        ref[0, stream, :, g, :] = x[:, g * HEAD_DIM:(g + 1) * HEAD_DIM]


def _load_heads(ref, stream=0):
    return jnp.concatenate([ref[0, stream, :, g, :] for g in range(KV_HEADS)], axis=1)


def _attn_prompt_kernel(cb, q_ref, k_ref, v_ref, km_ref, vm_ref, bias_ref, o_ref, nk_ref, nv_ref,
                        kx_ref, vx_ref):
    step = pl.program_id(1)
    s_len = k_ref.shape[1]
    meta_at = WINDOW + s_len

    @pl.when(step == pl.num_programs(1) - 1)
    def _():
        _store_heads(nk_ref, k_ref[0, s_len - WINDOW:, :])
        _store_heads(nv_ref, v_ref[0, s_len - WINDOW:, :])

    @pl.when(step == 0)
    def _():
        piece = min(s_len, 512)
        for src, meta, dst in ((k_ref, km_ref, kx_ref), (v_ref, vm_ref, vx_ref)):
            expand = functools.partial(_expand_kv, with_ones=dst is vx_ref)
            blank = lambda n: expand(jnp.zeros((n, src.shape[2]), F32))
            dst[0:WINDOW] = blank(WINDOW)
            for r in range(0, s_len, piece):
                dst[WINDOW + r:WINDOW + r + piece] = expand(src[0, r:r + piece, :])
            dst[meta_at:meta_at + N_META] = expand(meta[...])
            dst[meta_at + N_META:] = blank(dst.shape[0] - meta_at - N_META)

    win = WINDOW + CHUNK
    tail = kx_ref.shape[0] - meta_at
    for j in range(cb):
        c = step * cb + j
        start = pl.multiple_of(c * CHUNK, CHUNK)
        kall = jnp.concatenate([kx_ref[pl.ds(start, win), :], kx_ref[meta_at:meta_at + tail, :]], axis=0)
        vall = jnp.concatenate([vx_ref[pl.ds(start, win), :], vx_ref[meta_at:meta_at + tail, :]], axis=0)
        rows = slice(j * CHUNK, (j + 1) * CHUNK)
        outs = _attn_core(q_ref[0, rows, :], kall, vall, bias_ref[jnp.minimum(c, bias_ref.shape[0] - 1)])
        for ci, o in enumerate(outs):
            o_ref[0, rows, ci * LANES:(ci + 1) * LANES] = o.astype(o_ref.dtype)


def _attn_prompt(q, k, v, k_meta, v_meta, table, sinks):
    b, s, aw = q.shape
    kvw = k.shape[-1]
    nc = s // CHUNK
    cb = ATTN_CHUNKS_PER_STEP if nc % ATTN_CHUNKS_PER_STEP == 0 else 1
    assert s % min(s, 512) == 0
    n_bias = 1
    while True:
        qpos = N_META + (n_bias - 1) * CHUNK
        if np.all(_t5_bucket_np(np.arange(N_META) - qpos) == _t5_bucket_np(np.arange(N_META) - qpos - 10 ** 6)):
            break
        n_bias += 1
    n_bias = min(max(n_bias, WINDOW_CHUNKS + 1), nc)
    cs = np.arange(n_bias)[:, None]
    qpos = N_META + cs * CHUNK + np.arange(CHUNK)[None]
    wpos = N_META + (cs - WINDOW_CHUNKS) * CHUNK + np.arange(WINDOW + CHUNK)[None]
    kpos = np.concatenate([wpos, np.broadcast_to(np.arange(N_META), (n_bias, N_META))], axis=1)
    valid = np.concatenate([wpos >= N_META, np.ones((n_bias, N_META), bool)], axis=1)
    bias = _bias_rows(table, _t5_bucket_np(kpos[:, None, :] - qpos[:, :, None]))
    bias = _pad_keys(jnp.where(valid[:, None, :], bias, -jnp.inf), sinks)
    qs =pl.BlockSpec((1, cb * CHUNK, aw), lambda i, c: (i, c, 0))
    kv = pl.BlockSpec((1, s, kvw), lambda i, c: (i, 0, 0))
    meta = pl.BlockSpec((N_META, kvw), lambda i, c: (0, 0))
    xrows = WINDOW + s + bias.shape[-1] - (WINDOW + CHUNK)
    cache_shape = (1, b, WINDOW, KV_HEADS, HEAD_DIM)
    cache = pl.BlockSpec((1, 1) + cache_shape[2:], lambda i, c: (0, i, 0, 0, 0))
    return pl.pallas_call(
        functools.partial(_attn_prompt_kernel, cb),
        grid=(b, nc // cb),
        in_specs=[qs, kv, kv, meta, meta, pl.BlockSpec(bias.shape, lambda i, c: (0, 0, 0))],
        out_specs=[qs, cache, cache],
        out_shape=[jax.ShapeDtypeStruct((b, s, aw), BF16)] + [jax.ShapeDtypeStruct(cache_shape, F32)] * 2,
        scratch_shapes=[pltpu.VMEM((xrows, KV_EXPAND * LANES), BF16),
                        pltpu.VMEM((xrows, 2 * KV_EXPAND * LANES), BF16)],
        compiler_params=_params("arbitrary", "arbitrary"),
        name="attn_prompt",
    )(q, k, v, k_meta, v_meta, bias)


def _attn_sample_kernel(q_ref, kc_ref, vc_ref, kn_ref, vn_ref, km_ref, vm_ref, bias_ref, o_ref, nk_ref, nv_ref):
    tk = kc_ref.shape[2] + kn_ref.shape[1] + km_ref.shape[0]
    zeros = jnp.zeros((bias_ref.shape[1] - tk, km_ref.shape[1]), F32)
    for s in range(q_ref.shape[0]):
        kc, vc, kn, vn = _load_heads(kc_ref, s), _load_heads(vc_ref, s), kn_ref[s], vn_ref[s]
        kall = _expand_kv(jnp.concatenate([kc, kn, km_ref[...], zeros], axis=0))
        vall = _expand_kv(jnp.concatenate([vc, vn, vm_ref[...], zeros], axis=0), with_ones=True)
        outs = _attn_core(q_ref[s], kall, vall, bias_ref[...])
        for ci, o in enumerate(outs):
            o_ref[s, :, ci * LANES:(ci + 1) * LANES] = o.astype(o_ref.dtype)
        n_new = kn.shape[0]
        _store_heads(nk_ref, jnp.concatenate([kc[n_new:], kn], axis=0), s)
        _store_heads(nv_ref, jnp.concatenate([vc[n_new:], vn], axis=0), s)


def _attn_sample(q, k_cache, v_cache, k_new, v_new, k_meta, v_meta, table, sinks):
    bd, sd, aw = q.shape
    kvw = k_new.shape[-1]
    win = k_cache.shape[2]
    sb = SAMPLE_STREAMS_PER_STEP if bd % SAMPLE_STREAMS_PER_STEP == 0 else 1
    cache = pl.BlockSpec((1, sb) + k_cache.shape[2:], lambda i: (0, i, 0, 0, 0))
    qpos = N_META + PAST_LEN + np.arange(sd)
    kpos = np.concatenate([N_META + PAST_LEN - win + np.arange(win), qpos, np.arange(N_META)])
    bias = _pad_keys(_bias_rows(table, _t5_bucket_np(kpos[None, :] - qpos[:, None])), sinks)
    per = lambda n, w: pl.BlockSpec((sb, n, w), lambda i: (i, 0, 0))
    meta = pl.BlockSpec((N_META, kvw), lambda i: (0, 0))
    return pl.pallas_call(
        _attn_sample_kernel,
        grid=(bd // sb,),
        in_specs=[per(sd, aw), cache, cache, per(sd, kvw), per(sd, kvw), meta, meta,
                  pl.BlockSpec(bias.shape, lambda i: (0, 0))],
        out_specs=[per(sd, aw), cache, cache],
        out_shape=[jax.ShapeDtypeStruct((bd, sd, aw), BF16)] + [jax.ShapeDtypeStruct(k_cache.shape, F32)] * 2,
        compiler_params=_params("arbitrary"),
        name="attn_sample",
    )(q, k_cache, v_cache, k_new, v_new, k_meta, v_meta, bias)


ROUTE_E1, ROUTE_E2, ROUTE_R1, ROUTE_R2, ROUTE_W1, ROUTE_W2 = range(6)
META_FIELDS = 2 * TOP_K


def _merge_kernel(hw, x_ref, yr_ref, at_ref, sga_ref, sgb_ref, wb_ref, wo_ref, wn_ref, wr_ref, br_ref, cnt0_ref, tri_ref,
                  h_ref, xn_ref, route_ref, meta_ref, cnt_ref, carry_ref):
    @pl.when(pl.program_id(0) == 0)
    def _():
        carry_ref[...] = cnt0_ref[...]

    logits = _project_rows(hw, x_ref, yr_ref, at_ref, sga_ref, sgb_ref, wb_ref, wo_ref, wn_ref, wr_ref, br_ref,
                           h_ref, xn_ref)
    _route_rows(logits, tri_ref, route_ref, meta_ref, carry_ref)
    cnt_ref[...] = carry_ref[...]


def _project_rows(hw, x_ref, yr_ref, at_ref, sga_ref, sgb_ref, wb_ref, wo_ref, wn_ref, wr_ref, br_ref, h_ref, xn_ref):
    br = _dot(yr_ref[...], wb_ref[0:hw, :])
    ba = _dot(at_ref[...], wb_ref[hw:, :])
    merged = sga_ref[...].astype(F32) * br + sgb_ref[...].astype(F32) * ba
    h = x_ref[...] + _dot(merged.astype(BF16), wo_ref[...])
    h_ref[...] = h
    ms = jnp.mean(h * h, axis=-1, keepdims=True)
    xn = h * lax.rsqrt(ms + EPS) * wn_ref[...]
    _store_row_tiles(xn_ref, xn)

    x_hi = xn.astype(BF16)
    x_lo = (xn - x_hi.astype(F32)).astype(BF16)
    both = _dot(x_hi, wr_ref[...])
    return both[:, :LANES] + both[:, LANES:] + _dot(x_lo, wr_ref[:, :LANES]) + br_ref[...]


def _route_rows(logits, tri_ref, route_ref, meta_ref, carry_ref):
    tm = logits.shape[0]
    lt = logits.T
    row_i = lax.broadcasted_iota(I32, (LANES, tm), 0)
    row = row_i.astype(F32)
    group_of_row = (row_i >> int(math.log2(EXPERTS_PER_GROUP))).astype(F32)
    ninf = -jnp.inf
    first = lambda hit, idx: jnp.min(jnp.where(hit, idx, float(LANES)), axis=0, keepdims=True)
    gmask = (row_i >= N_EXPERTS) & (row_i < N_EXPERTS + N_GROUPS)
    gl = jnp.where(gmask, lt, ninf)
    gmax = jnp.max(gl, axis=0, keepdims=True)
    gidx = first(gl == gmax, row - N_EXPERTS)
    gval = 1.0 / jnp.sum(jnp.exp(gl - gmax), axis=0, keepdims=True)
    emask = (row_i < N_EXPERTS) & (group_of_row == gidx)
    el = jnp.where(emask, lt, ninf)
    m1 = jnp.max(el, axis=0, keepdims=True)
    i1 = first(el == m1, row)
    el2 = jnp.where(row == i1, ninf, el)
    m2 = jnp.max(el2, axis=0, keepdims=True)
    i2 = first(el2 == m2, row)
    e21 = jnp.exp(m2 - m1)
    w1 = gval / (1.0 + e21)
    w2 = gval * e21 / (1.0 + e21)

    sel1 = row == i1
    sel2 = row == i2
    oh = (sel1 | sel2).astype(BF16)
    before = _dot(oh, tri_ref[...]) + carry_ref[...]
    r1 = jnp.sum(jnp.where(sel1, before, 0.0), axis=0, keepdims=True)
    r2 = jnp.sum(jnp.where(sel2, before, 0.0), axis=0, keepdims=True)
    carry_ref[...] = carry_ref[...] + jnp.sum(oh.astype(F32), axis=1, keepdims=True)

    rec = jnp.zeros((LANES, tm), F32)
    for slot, val in ((ROUTE_E1, i1), (ROUTE_E2, i2), (ROUTE_W1, w1), (ROUTE_W2, w2), (ROUTE_R1, r1), (ROUTE_R2, r2)):
        rec = jnp.where(row_i == slot, val, rec)
    route_ref[...] = rec.T
    meta_ref[0] = rec[ROUTE_E1:ROUTE_E1 + META_FIELDS].astype(I32)


def _merge(x, y_rec, att, sga, sgb, wb, wo, w_norm, w_router, b_router, cnt0):
    t, d = x.shape
    hw = y_rec.shape[1]
    tm = PROJ_TILE if t % PROJ_TILE == 0 else t
    row = lambda w: pl.BlockSpec((tm, w), lambda i: (i, 0))
    const = lambda a, b: pl.BlockSpec((a, b), lambda i: (0, 0))
    return pl.pallas_call(
        functools.partial(_merge_kernel, hw),
        grid=(t // tm,),
        in_specs=[row(d), row(hw), row(att.shape[1]), row(d), row(d), const(*wb.shape), const(d, d), const(1, d),
                  const(d, 2 * LANES), const(1, LANES), const(LANES, 1), const(tm, tm)],
        out_specs=[row(d), pl.BlockSpec((tm * d // LANES, LANES), lambda i: (i, 0)), row(LANES),
                   pl.BlockSpec((1, META_FIELDS, tm), lambda i: (i, 0, 0)), const(LANES, 1)],
        out_shape=[jax.ShapeDtypeStruct((t, d), F32), jax.ShapeDtypeStruct((t * d // LANES, LANES), F32),
                   jax.ShapeDtypeStruct((t, LANES), F32), jax.ShapeDtypeStruct((t // tm, META_FIELDS, tm), I32),
                   jax.ShapeDtypeStruct((LANES, 1), F32)],
        scratch_shapes=[pltpu.VMEM((LANES, 1), F32)],
        compiler_params=_params("arbitrary"),
        name="merge_route",
    )(x, y_rec, att, sga, sgb, wb, wo, w_norm.reshape(1, d), w_router, b_router, cnt0,
      jnp.asarray(np.triu(np.ones((tm, tm)), k=1), dtype=BF16))


def _row_copy(src, dst, sem):
    return pltpu.make_async_copy(src, dst, sem)


ROW_UNROLL = 8


def _tile_rows(ref, row, sub):
    return ref.at[pl.ds(pl.multiple_of(row * sub, sub), sub)]


def _scatter_kernel(n_tok, sub, first, slot_ref, pstart_ref, pend_ref, x_ref, *rest):
    xs_ref, zero_ref, sem, zsem, tsem = rest[-5:]

    def zero_blocks(blocks, zs, wait):
        for cond, row in blocks:
            @pl.when(cond)
            def _():
                at = row * sub if isinstance(row, int) else pl.multiple_of(row * sub, EXPERT_TILE * sub)
                cp = _row_copy(zero_ref, xs_ref.at[pl.ds(at, EXPERT_TILE * sub)], zs)
                cp.wait() if wait else cp.start()

    if first:
        n_rows = xs_ref.shape[0] // sub
        tails = [(pend_ref[e] > pstart_ref[e], pend_ref[e] - EXPERT_TILE) for e in range(N_EXPERTS)]
        unused = [(n_rows - (j + 1) * EXPERT_TILE >= pend_ref[N_EXPERTS - 1], n_rows - (j + 1) * EXPERT_TILE)
                  for j in range(N_EXPERTS)]

        @pl.when(pl.program_id(0) == 0)
        def _():
            zero_ref[...] = jnp.zeros_like(zero_ref)
            zero_blocks(tails, zsem, False)
            zero_blocks(unused, tsem, False)
            zero_blocks(tails, zsem, True)

    def issue(grp, _):
        for u in range(ROW_UNROLL):
            r = grp * ROW_UNROLL + u
            for k in range(TOP_K):
                _row_copy(_tile_rows(x_ref, r, sub), _tile_rows(xs_ref, slot_ref[0, k, r], sub),
                          sem).start(priority=(u * TOP_K + k) % 2)
        return 0

    lax.fori_loop(0, n_tok // ROW_UNROLL, issue, 0)
    for k in range(TOP_K):
        _row_copy(x_ref, xs_ref.at[pl.ds(0, n_tok * sub)], sem).wait()

    if first:
        @pl.when(pl.program_id(0) == 0)
        def _():
            zero_blocks(unused, tsem, True)


def _scatter_rows(x, slots, pstarts, pends, xs, rows):
    n, fields, tm = slots.shape
    sub = x.shape[0] // (n * tm)
    first = xs is None
    smem = pl.BlockSpec(memory_space=pltpu.SMEM)
    return pl.pallas_call(
        functools.partial(_scatter_kernel, tm, sub, first),
        grid=(n,),
        in_specs=[pl.BlockSpec((1, fields, tm), lambda i: (i, 0, 0), memory_space=pltpu.SMEM), smem, smem,
                  pl.BlockSpec((tm * sub, LANES), lambda i: (i, 0))]
        + ([] if first else [pl.BlockSpec(memory_space=pl.ANY)]),
        out_specs=pl.BlockSpec(memory_space=pl.ANY),
        out_shape=jax.ShapeDtypeStruct((rows * sub, LANES), F32),
        scratch_shapes=[pltpu.VMEM((EXPERT_TILE * sub, LANES), F32)] + [pltpu.SemaphoreType.DMA(())] * 3,
        input_output_aliases={} if first else {4: 0},
        compiler_params=_params("arbitrary"),
        name="moe_scatter",
    )(slots, pstarts, pends, x, *([] if first else [xs]))


def _expert_kernel(tm, sub, be_ref, nb_ref, nxt_ref, x_ref, wg_ref, wu_ref, wd_ref, y_ref,
                   raw_g, raw_u, raw_d, wgb_ref, wub_ref, wdb_ref, sem, slot_ref):
    i = pl.program_id(0)
    used = i < nb_ref[0]
    first = used & ((i == 0) | (be_ref[i] != be_ref[jnp.maximum(i - 1, 0)]))

    def fetch(e, slot):
        return [_row_copy(w.at[e], raw.at[slot], sem.at[slot])
                for w, raw in ((wg_ref, raw_g), (wu_ref, raw_u), (wd_ref, raw_d))]

    @pl.when(i == 0)
    def _():
        slot_ref[0] = 0
        for cp in fetch(be_ref[0], 0):
            cp.start()

    @pl.when(first)
    def _():
        slot = slot_ref[0]
        for cp in fetch(be_ref[i], slot):
            cp.wait()
        wgb_ref[...] = raw_g[slot].astype(BF16)
        wub_ref[...] = raw_u[slot].astype(BF16)
        wdb_ref[...] = raw_d[slot].astype(BF16)
        slot_ref[0] = 1 - slot

        @pl.when(nxt_ref[i] >= 0)
        def _():
            for cp in fetch(nxt_ref[i], 1 - slot):
                cp.start()

    @pl.when(used)
    def _():
        x = _load_row_tiles(x_ref, tm, sub).astype(BF16)
        g = _dot(x, wgb_ref[...])
        u = _dot(x, wub_ref[...])
        hmid = (g * _sigmoid(g) * u).astype(BF16)
        _store_row_tiles(y_ref, _dot(hmid, wdb_ref[...]))

    @pl.when(jnp.logical_not(used))
    def _():
        y_ref[...] = jnp.zeros_like(y_ref)


def _experts(xs, block_e, n_used, nxt_e, wg, wu, wd):
    _, d, ff = wg.shape
    sub = d // LANES
    tm = EXPERT_TILE
    blk = pl.BlockSpec((tm * sub, LANES), lambda i, be, nb, nx: (i, 0))
    hbm = pl.BlockSpec(memory_space=pl.ANY)
    grid_spec = pltpu.PrefetchScalarGridSpec(
        num_scalar_prefetch=3,
        grid=(xs.shape[0] // (tm * sub),),
        in_specs=[blk, hbm, hbm, hbm],
        out_specs=blk,
        scratch_shapes=[pltpu.VMEM((2, d, ff), F32), pltpu.VMEM((2, d, ff), F32), pltpu.VMEM((2, ff, d), F32),
                        pltpu.VMEM((d, ff), BF16), pltpu.VMEM((d, ff), BF16), pltpu.VMEM((ff, d), BF16),
                        pltpu.SemaphoreType.DMA((2,)), pltpu.SMEM((1,), I32)],
    )
    return pl.pallas_call(
        functools.partial(_expert_kernel, tm, sub),
        grid_spec=grid_spec,
        out_shape=jax.ShapeDtypeStruct(xs.shape, F32),
        compiler_params=_params("arbitrary"),
        name="moe_experts",
    )(block_e, n_used, nxt_e, xs, wg, wu, wd)


def _combine_kernel(n_tok, sub, scur_ref, snext_ref, h_ref, route_ref, ys_ref, o_ref, buf, sem):
    i = pl.program_id(0)
    n = pl.num_programs(0)
    slot = i % 2

    def start(slot_ref, s):
        def issue(grp, _):
            for u in range(ROW_UNROLL):
                r = grp * ROW_UNROLL + u
                for k in range(TOP_K):
                    _row_copy(_tile_rows(ys_ref, slot_ref[0, k, r], sub), _tile_rows(buf.at[s, k], r, sub),
                              sem.at[s]).start(priority=(u * TOP_K + k) % 2)
            return 0
        lax.fori_loop(0, n_tok // ROW_UNROLL, issue, 0)

    @pl.when(i == 0)
    def _():
        start(scur_ref, 0)

    @pl.when(i + 1 < n)
    def _():
        start(snext_ref, 1 - slot)

    for k in range(TOP_K):
        _row_copy(ys_ref.at[pl.ds(0, n_tok * sub)], buf.at[slot, k], sem.at[slot]).wait()
    route = route_ref[...]
    w1 = route[:, ROUTE_W1:ROUTE_W1 + 1]
    w2 = route[:, ROUTE_W2:ROUTE_W2 + 1]
    for j in range(sub):
        cols = slice(j * LANES, (j + 1) * LANES)
        part = lambda k: buf[slot, k, pl.ds(j, n_tok, stride=sub), :]
        o_ref[:, cols] = h_ref[:, cols] + (part(0) * w1 + part(1) * w2)


def _combine(h, route, slots, ys):
    t, d = h.shape
    n, fields, tm = slots.shape
    sub = d // LANES
    mspec = lambda f: pl.BlockSpec((1, fields, tm), f, memory_space=pltpu.SMEM)
    return pl.pallas_call(
        functools.partial(_combine_kernel, tm, sub),
        grid=(n,),
        in_specs=[mspec(lambda i: (i, 0, 0)), mspec(lambda i: (jnp.minimum(i + 1, n - 1), 0, 0)),
                  pl.BlockSpec((tm, d), lambda i: (i, 0)), pl.BlockSpec((tm, LANES), lambda i: (i, 0)),
                  pl.BlockSpec(memory_space=pl.ANY)],
        out_specs=pl.BlockSpec((tm, d), lambda i: (i, 0)),
        out_shape=jax.ShapeDtypeStruct((t, d), F32),
        scratch_shapes=[pltpu.VMEM((2, TOP_K, tm * sub, LANES), F32), pltpu.SemaphoreType.DMA((2,))],
        compiler_params=_params("arbitrary"),
        name="moe_combine",
    )(slots, slots, h, route, ys)


def _moe(parts, counts, wg, wu, wd):
    tm = EXPERT_TILE
    n_assign = sum(part[0].shape[0] for part in parts) * TOP_K
    n_blocks = -(-(n_assign + N_EXPERTS * (tm - 1)) // tm)
    counts = counts[:N_EXPERTS, 0].astype(I32)
    pcounts = (counts + tm - 1) // tm * tm
    pends = jnp.cumsum(pcounts)
    pstarts = pends - pcounts
    block_start = jnp.arange(n_blocks, dtype=I32) * tm
    block_e = jnp.minimum(jnp.sum((pends[None, :] <= block_start[:, None]).astype(I32), axis=1), N_EXPERTS - 1)
    n_used = pends[-1:] // tm
    ids = jnp.arange(N_EXPERTS, dtype=I32)
    later = (ids[None, :] > ids[:, None]) & (pcounts[None, :] > 0)
    nxt_of = jnp.min(jnp.where(later, ids[None, :], N_EXPERTS), axis=1)
    nxt_of = jnp.where(nxt_of < N_EXPERTS, nxt_of, -1)
    nxt_e = jnp.sum(jnp.where(block_e[:, None] == ids[None, :], nxt_of[None, :], 0), axis=1)
    xs = None
    slots = []
    for _, xn, _, meta in parts:
        assert meta.shape[-1] % ROW_UNROLL == 0
        e, pos = meta[:, :TOP_K], meta[:, TOP_K:]
        seg = jnp.sum(jnp.where(e[..., None] == jnp.arange(N_EXPERTS, dtype=I32), pstarts, 0), axis=-1)
        slots.append(seg + pos)
        xs = _scatter_rows(xn, _retile(slots[-1], SCATTER_TILE), pstarts, pends, xs, n_blocks * tm)
    ys = _experts(xs, block_e, n_used, nxt_e, wg, wu, wd)
    return [_combine(h, route, _retile(s, COMBINE_TILE), ys) for (h, _, route, _), s in zip(parts, slots)]


def _retile(slots, tile):
    n, k, t = slots.shape
    if t % tile == 0 and t > tile:
        return slots.reshape(n, k, t // tile, tile).transpose(0, 2, 1, 3).reshape(n * (t // tile), k, tile)
    if tile % t == 0 and tile > t and n % (tile // t) == 0:
        f = tile // t
        return slots.reshape(n // f, f, k, t).transpose(0, 2, 1, 3).reshape(n // f, k, tile)
    return slots


def kernel(x_prompt, x_sample, cache_swa_k, cache_swa_v, state_hgrn, meta_tokens, rel_bias_table, hgrn_lower_bounds, w_norm_mix, w_in, hgrn_out_norm, q_norm, k_norm, attn_sinks, w_branch, w_out, w_norm_ffn, w_router_group, b_router_group, w_router_expert, b_router_expert, w_expert_gate, w_expert_up, w_expert_down):
    b, s, d = x_prompt.shape
    bd, sd, _ = x_sample.shape
    depth, _, heads, dk, dv = state_hgrn.shape
    assert depth == 1 and heads == HGRN_HEADS and dk == dv
    hw = heads * dk
    aw = ATTN_HEADS * HEAD_DIM
    kvw = KV_HEADS * HEAD_DIM
    assert w_in.shape[-1] == 4 * hw + aw + 2 * kvw + 2 * d
    assert s % HGRN_CHUNK == 0 and s % CHUNK == 0 and sd == N_META and N_EXPERTS + N_GROUPS <= LANES
    l = 0

    p = jax.nn.softmax(hgrn_lower_bounds.astype(F32), axis=0)
    lb = jnp.cumsum(p, axis=0)[l + 1] - p[0]

    w_in_b = w_in[l].astype(BF16)
    proj = functools.partial(_inproj, w_norm=w_norm_mix[l], w_in_bf16=w_in_b, q_gain=q_norm[l], k_gain=k_norm[l],
                             lb=lb, hw=hw, aw=aw, kvw=kvw)
    x_small = jnp.concatenate([x_sample.reshape(bd * sd, d), meta_tokens.astype(F32)], axis=0)
    *rec_s, qa_s, k_s, v_s, sga_s, sgb_s = proj(x_small)
    *rec_p, qa_p, k_p, v_p, sga_p, sgb_p = proj(x_prompt.reshape(b * s, d))
    ns = bd * sd
    k_meta, v_meta = k_s[ns:], v_s[ns:]

    streams = lambda a, n: a.reshape(n, -1, a.shape[-1])
    s0_small = jnp.concatenate([state_hgrn[l].astype(F32), jnp.zeros((1, heads, dk, dv), F32)], axis=0)
    y_small, st_small = _hgrn(*(streams(a, bd + 1) for a in rec_s), hgrn_out_norm[l], s0_small, sd)
    y_p, st_p = _hgrn(*(streams(a, b) for a in rec_p), hgrn_out_norm[l], st_small, HGRN_CHUNK, shared_s0=bd)

    table = rel_bias_table.astype(F32)
    att_p, new_k_p, new_v_p = _attn_prompt(streams(qa_p, b), streams(k_p, b), streams(v_p, b), k_meta, v_meta,
                                           table, attn_sinks[l])
    k_new, v_new = k_s[:ns].reshape(bd, sd, kvw), v_s[:ns].reshape(bd, sd, kvw)
    att_s, new_k_s, new_v_s = _attn_sample(qa_s[:ns].reshape(bd, sd, aw), cache_swa_k[l:l + 1].astype(F32),
                                           cache_swa_v[l:l + 1].astype(F32), k_new, v_new, k_meta, v_meta,
                                           table, attn_sinks[l])

    wb = w_branch[l].astype(BF16)
    wo = w_out[l].astype(BF16)
    w_router = jnp.pad(jnp.concatenate([w_router_expert[l], w_router_group[l]], axis=1).astype(F32),
                       ((0, 0), (0, LANES - N_EXPERTS - N_GROUPS)))
    b_router = jnp.pad(jnp.concatenate([b_router_expert[l], b_router_group[l]]).astype(F32),
                       (0, LANES - N_EXPERTS - N_GROUPS)).reshape(1, LANES)
    w_router_hi = w_router.astype(BF16)
    w_router_lo = (w_router - w_router_hi.astype(F32)).astype(BF16)
    w_router = jnp.concatenate([w_router_hi, w_router_lo], axis=1)
    merge = functools.partial(_merge, wb=wb, wo=wo, w_norm=w_norm_ffn[l], w_router=w_router, b_router=b_router)
    *part_p, cnt_p = merge(x_prompt.reshape(b * s, d), y_p.reshape(b * s, hw), att_p.reshape(b * s, aw),
                           sga_p, sgb_p, cnt0=jnp.zeros((LANES, 1), F32))
    *part_s, cnt_s = merge(x_sample.reshape(ns, d), y_small[:bd].reshape(ns, hw), att_s.reshape(ns, aw),
                           sga_s[:ns], sgb_s[:ns], cnt0=cnt_p)

    out_p, out_s = _moe([part_p, part_s], cnt_s, w_expert_gate[l], w_expert_up[l], w_expert_down[l])

    return (out_p.reshape(b, s, d), out_s.reshape(bd, sd, d), new_k_p, new_v_p, st_p[None],
            new_k_s, new_v_s, st_small[:bd][None])
```
